```python
import math, functools
import jax, jax.numpy as jnp
from jax import lax
import numpy as np

D_MODEL = 1024
BATCH = 2
SEQ = 8192
DEPTH = 2
DEC_BATCH = 128
DEC_SEQ = 8
PAST_LEN = 8192
PAGE_SIZE = 128

N_META = 16
CONV_WIDTH = 4
CHUNK = 64
BLOCK = 128
FRONT_PAD = BLOCK - N_META

DN_HEADS = 4
DN_DK = 128
DN_DV = 128
DN_QK = DN_HEADS * DN_DK
DN_V = DN_HEADS * DN_DV
DN_CONV = 2 * DN_QK + DN_V
SSM_HEADS = 4
SSM_HEADDIM = 64
SSM_GROUPS = 2
SSM_STATE = 128
SSM_INNER = SSM_HEADS * SSM_HEADDIM
SSM_BC = SSM_GROUPS * SSM_STATE
SSM_CONV = SSM_INNER + 2 * SSM_BC
SWA_Q_HEADS = 4
SWA_KV_HEADS = 2
SWA_HEAD_DIM = 64
SWA_Q = SWA_Q_HEADS * SWA_HEAD_DIM
SWA_KV = SWA_KV_HEADS * SWA_HEAD_DIM
WINDOW = 128
ROPE_THETA = 10000.0

D_MIX = DN_V + SSM_INNER + SWA_Q
D_FF = -(-8 * D_MODEL // (3 * 256)) * 256
IN_WIDTHS = (DN_CONV, DN_V, DN_HEADS, DN_HEADS, SSM_CONV, SSM_INNER, SSM_HEADS, SWA_Q, SWA_KV, SWA_KV)
D_IN = sum(IN_WIDTHS)

kernel_name = "hymba_style_deltanet_ssd_swa_decoder_step"


def rmsnorm(x, g, eps=1e-6):
    xf = x.astype(jnp.float32)
    y = xf * lax.rsqrt(jnp.mean(xf * xf, axis=-1, keepdims=True) + eps)
    return (y * g.astype(jnp.float32)).astype(x.dtype)


def l2norm(x, eps=1e-6):
    xf = x.astype(jnp.float32)
    return (xf * lax.rsqrt(jnp.sum(xf * xf, axis=-1, keepdims=True) + eps)).astype(x.dtype)


def split_cols(a, widths):
    return jnp.split(a, np.cumsum(widths)[:-1].tolist(), axis=-1)


def pad_front(a, n):
    return jnp.pad(a, [(0, 0), (n, 0)] + [(0, 0)] * (a.ndim - 2))


def causal_conv(u, buf, w):
    t = u.shape[1]
    full = jnp.concatenate([buf.astype(u.dtype), u], axis=1)
    out = full[:, 0:t] * w[0]
    for i in range(1, CONV_WIDTH):
        out = out + full[:, i:i + t] * w[i]
    return out, full[:, t:]


def rope(x, pos):
    half = x.shape[-1] // 2
    inv = ROPE_THETA ** (-jnp.arange(half, dtype=jnp.float32) / half)
    ang = pos.astype(jnp.float32)[:, None] * inv[None, :]
    cos = jnp.cos(ang)[None, :, None, :]
    sin = jnp.sin(ang)[None, :, None, :]
    xf = x.astype(jnp.float32)
    x1, x2 = xf[..., :half], xf[..., half:]
    return jnp.concatenate([x1 * cos - x2 * sin, x2 * cos + x1 * sin], axis=-1).astype(x.dtype)


def to_chunks(a, c):
    b, t = a.shape[:2]
    return jnp.swapaxes(a.reshape((b, t // c, c) + a.shape[2:]), 2, 3)


def from_chunks(a):
    a = jnp.swapaxes(a, 2, 3)
    return a.reshape((a.shape[0], a.shape[1] * a.shape[2]) + a.shape[3:])


def segment_decay(gc):
    c = gc.shape[-1]
    causal = jnp.tril(jnp.ones((c, c), bool))
    diff = gc[..., :, None] - gc[..., None, :]
    return jnp.where(causal, jnp.exp(jnp.where(causal, diff, 0.0)), 0.0)


def gated_delta_rule(q, k, v, g, beta, s0, chunk):
    f32 = jnp.float32
    dv = v.shape[-1]
    qc, kc, vc = (to_chunks(a.astype(f32), chunk) for a in (q, k, v))
    gc = jnp.cumsum(to_chunks(g.astype(f32), chunk), axis=-1)
    bc = to_chunks(beta.astype(f32), chunk)
    decay = segment_decay(gc)
    strict = jnp.tril(jnp.ones((chunk, chunk), bool), -1)
    a_mat = jnp.where(strict, bc[..., :, None] * jnp.einsum('bnhid,bnhjd->bnhij', kc, kc) * decay, 0.0)
    m = a_mat + jnp.eye(chunk, dtype=f32)
    rhs = jnp.concatenate([vc * bc[..., None], kc * (bc * jnp.exp(gc))[..., None]], axis=-1)
    sol = lax.linalg.triangular_solve(m, rhs, left_side=True, lower=True, unit_diagonal=True)
    u, w = sol[..., :dv], sol[..., dv:]
    attn = jnp.einsum('bnhid,bnhjd->bnhij', qc, kc) * decay
    q_dec = qc * jnp.exp(gc)[..., None]
    k_dec = kc * jnp.exp(gc[..., -1:] - gc)[..., None]
    g_last = jnp.exp(gc[..., -1])

    def step(s, inp):
        u_c, w_c, attn_c, qd_c, kd_c, gl_c = inp
        v_new = u_c - jnp.einsum('bhcd,bhde->bhce', w_c, s)
        o_c = jnp.einsum('bhcd,bhde->bhce', qd_c, s) + jnp.einsum('bhij,bhje->bhie', attn_c, v_new)
        s = s * gl_c[..., None, None] + jnp.einsum('bhcd,bhce->bhde', kd_c, v_new)
        return s, o_c

    xs = tuple(jnp.moveaxis(a, 1, 0) for a in (u, w, attn, q_dec, k_dec, g_last))
    s_final, o = lax.scan(step, s0.astype(f32), xs)
    return from_chunks(jnp.moveaxis(o, 0, 1)).astype(v.dtype), s_final


def ssd_scan(x, dt, a_neg, b_in, c_in, h0, chunk):
    f32 = jnp.float32
    rep = SSM_HEADS // SSM_GROUPS
    bc = to_chunks(jnp.repeat(b_in.astype(f32), rep, axis=2), chunk)
    cc = to_chunks(jnp.repeat(c_in.astype(f32), rep, axis=2), chunk)
    xdt = to_chunks(x.astype(f32) * dt.astype(f32)[..., None], chunk)
    gc = jnp.cumsum(to_chunks(dt.astype(f32) * a_neg.astype(f32), chunk), axis=-1)
    attn = jnp.einsum('bnhis,bnhjs->bnhij', cc, bc) * segment_decay(gc)
    y_intra = jnp.einsum('bnhij,bnhjp->bnhip', attn, xdt)
    c_dec = cc * jnp.exp(gc)[..., None]
    b_dec = bc * jnp.exp(gc[..., -1:] - gc)[..., None]
    g_last = jnp.exp(gc[..., -1])

    def step(h, inp):
        cd, bd, xd, gl = inp
        y_c = jnp.einsum('bhcs,bhps->bhcp', cd, h)
        h = h * gl[..., None, None] + jnp.einsum('bhcp,bhcs->bhps', xd, bd)
        return h, y_c

    xs = tuple(jnp.moveaxis(a, 1, 0) for a in (c_dec, b_dec, xdt, g_last))
    h_final, y_inter = lax.scan(step, h0.astype(f32), xs)
    y = y_intra + jnp.moveaxis(y_inter, 0, 1)
    return from_chunks(y).astype(x.dtype), h_final


def sink_softmax(scores, mask, sinks):
    s = jnp.where(mask, scores.astype(jnp.float32), -jnp.inf)
    sink = jnp.broadcast_to(sinks.astype(jnp.float32)[:, :, None, None], s.shape[:-1] + (1,))
    p = jax.nn.softmax(jnp.concatenate([s, sink], axis=-1), axis=-1)
    return p[..., :-1]


def swa_prompt(q, k, v, sinks):
    b, L = q.shape[:2]
    grp = SWA_Q_HEADS // SWA_KV_HEADS
    lp = L + FRONT_PAD
    nb = lp // BLOCK
    qb = pad_front(q, FRONT_PAD).reshape(b, nb, BLOCK, SWA_KV_HEADS, grp, SWA_HEAD_DIM)

    def band(a):
        ap = pad_front(a, FRONT_PAD + BLOCK).reshape(b, nb + 1, BLOCK, SWA_KV_HEADS, SWA_HEAD_DIM)
        return jnp.concatenate([ap[:, :-1], ap[:, 1:]], axis=2)

    kb, vb = band(k), band(v)
    scores = jnp.einsum('bnqkgd,bnskd->bnkgqs', qb, kb) * SWA_HEAD_DIM ** -0.5
    qi = jnp.arange(BLOCK)[:, None] + BLOCK
    kj = jnp.arange(2 * BLOCK)[None, :]
    rel = qi - kj
    kpos = (jnp.arange(nb)[:, None, None] - 1) * BLOCK + kj[None]
    mask = (rel >= 0) & (rel < WINDOW) & (kpos >= FRONT_PAD)
    p = sink_softmax(scores, mask[None, :, None, None], sinks.reshape(SWA_KV_HEADS, grp))
    o = jnp.einsum('bnkgqs,bnskd->bnqkgd', p.astype(vb.dtype), vb)
    o = o.reshape(b, lp, SWA_Q_HEADS, SWA_HEAD_DIM)[:, FRONT_PAD:]
    return o, k[:, -WINDOW:], v[:, -WINDOW:]


def swa_sample(q, k, v, sinks, k_buf, v_buf):
    t = q.shape[1]
    grp = SWA_Q_HEADS // SWA_KV_HEADS
    keys = jnp.concatenate([k_buf.astype(k.dtype), k], axis=1)
    vals = jnp.concatenate([v_buf.astype(v.dtype), v], axis=1)
    qg = q.reshape(q.shape[0], t, SWA_KV_HEADS, grp, SWA_HEAD_DIM)
    scores = jnp.einsum('btkgd,bskd->bkgts', qg, keys) * SWA_HEAD_DIM ** -0.5
    rel = (WINDOW + jnp.arange(t))[:, None] - jnp.arange(WINDOW + t)[None, :]
    mask = (rel >= 0) & (rel < WINDOW)
    p = sink_softmax(scores, mask, sinks.reshape(SWA_KV_HEADS, grp))
    o = jnp.einsum('bkgts,bskd->btkgd', p.astype(vals.dtype), vals).reshape(q.shape)
    return o, keys[:, -WINDOW:], vals[:, -WINDOW:]


def hybrid_mixer(h, pos, lw, states, front_pad, chunk, attend):
    dn_s0, dn_buf, ssm_s0, ssm_buf = states
    f32 = jnp.float32
    bsz, t = h.shape[:2]
    (dn_qkv, dn_z, dn_b, dn_a, ssm_xbc, ssm_z, ssm_dt, sw_q, sw_k, sw_v) = split_cols(h @ lw['w_in'], IN_WIDTHS)
    qkv, dn_buf_new = causal_conv(dn_qkv, dn_buf, lw['dn_conv_w'])
    q, k, v = split_cols(jax.nn.silu(qkv), (DN_QK, DN_QK, DN_V))
    q = l2norm(q.reshape(bsz, t, DN_HEADS, DN_DK)) * DN_DK ** -0.5
    k = l2norm(k.reshape(bsz, t, DN_HEADS, DN_DK))
    v = v.reshape(bsz, t, DN_HEADS, DN_DV)
    beta = jax.nn.sigmoid(dn_b.astype(f32))
    g = -jnp.exp(lw['dn_a_log'].astype(f32)) * jax.nn.softplus(dn_a.astype(f32) + lw['dn_dt_bias'].astype(f32))
    o_dn, dn_s = gated_delta_rule(*(pad_front(a, front_pad) for a in (q, k, v, g, beta)), dn_s0, chunk)
    o_dn = rmsnorm(o_dn[:, front_pad:], lw['dn_norm_w']) * jax.nn.silu(dn_z.reshape(bsz, t, DN_HEADS, DN_DV))
    xbc, ssm_buf_new = causal_conv(ssm_xbc, ssm_buf, lw['ssm_conv_w'])
    xs, bs, cs = split_cols(jax.nn.silu(xbc + lw['ssm_conv_b']), (SSM_INNER, SSM_BC, SSM_BC))
    xs = xs.reshape(bsz, t, SSM_HEADS, SSM_HEADDIM)
    bs = bs.reshape(bsz, t, SSM_GROUPS, SSM_STATE)
    cs = cs.reshape(bsz, t, SSM_GROUPS, SSM_STATE)
    dt = jax.nn.softplus(ssm_dt.astype(f32) + lw['ssm_dt_bias'].astype(f32))
    a_neg = -jnp.exp(lw['ssm_a_log'].astype(f32))
    y, ssm_s = ssd_scan(pad_front(xs, front_pad), pad_front(dt, front_pad), a_neg,
                        pad_front(bs, front_pad), pad_front(cs, front_pad), ssm_s0, chunk)
    y = (y[:, front_pad:] + xs * lw['ssm_d'][:, None]) * jax.nn.silu(ssm_z.reshape(bsz, t, SSM_HEADS, SSM_HEADDIM))
    y = rmsnorm(y.reshape(bsz, t, SSM_GROUPS, SSM_INNER // SSM_GROUPS),
                lw['ssm_norm_w'].reshape(SSM_GROUPS, SSM_INNER // SSM_GROUPS))
    qa = rope(sw_q.reshape(bsz, t, SWA_Q_HEADS, SWA_HEAD_DIM), pos)
    ka = rope(sw_k.reshape(bsz, t, SWA_KV_HEADS, SWA_HEAD_DIM), pos)
    va = sw_v.reshape(bsz, t, SWA_KV_HEADS, SWA_HEAD_DIM)
    o_sw, k_new, v_new = attend(qa, ka, va, lw['swa_sinks'])
    mixed = jnp.concatenate([o_dn.reshape(bsz, t, DN_V), y.reshape(bsz, t, SSM_INNER),
                             o_sw.reshape(bsz, t, SWA_Q)], axis=-1)
    return mixed @ lw['w_out'], (dn_s, dn_buf_new, ssm_s, ssm_buf_new, k_new, v_new)


def trunk_layer(x, pos, lw, states, front_pad, chunk, attend):
    m, new_states = hybrid_mixer(rmsnorm(x, lw['g_pre_mix']), pos, lw, states, front_pad, chunk, attend)
    x = x + rmsnorm(m, lw['g_post_mix'])
    gate, up = jnp.split(rmsnorm(x, lw['g_pre_ffn']) @ lw['w_ffn_in'], 2, axis=-1)
    x = x + rmsnorm((jax.nn.silu(gate) * up) @ lw['w_ffn_out'], lw['g_post_ffn'])
    return x, new_states


def setup_inputs(seed: int = 0) -> dict:
    key = jax.random.key(seed)
    ks = jax.random.split(key, 32)
    f32 = jnp.float32

    def normal(kk, shape, scale):
        return jax.random.normal(kk, shape, f32) * scale

    def gain(kk, shape):
        return 1.0 + normal(kk, shape, 0.02)

    def dt_bias(kk, shape):
        dt = jnp.exp(jax.random.uniform(kk, shape, f32, math.log(1e-3), math.log(1e-1)))
        return dt + jnp.log(-jnp.expm1(-dt))

    def a_log(kk, shape):
        return jnp.log(jax.random.uniform(kk, shape, f32, 1.0, 16.0))

    return {
        "x_prompt": normal(ks[0], (BATCH, SEQ, D_MODEL), 1.0),
        "x_sample": normal(ks[1], (DEC_BATCH, DEC_SEQ, D_MODEL), 1.0),
        "state_dn": normal(ks[2], (DEPTH, DEC_BATCH, DN_HEADS, DN_DK, DN_DV), 0.1),
        "state_dn_conv": normal(ks[3], (DEPTH, DEC_BATCH, CONV_WIDTH - 1, DN_CONV), 1.0),
        "state_ssm": normal(ks[4], (DEPTH, DEC_BATCH, SSM_HEADS, SSM_HEADDIM, SSM_STATE), 0.1),
        "state_ssm_conv": normal(ks[5], (DEPTH, DEC_BATCH, CONV_WIDTH - 1, SSM_CONV), 1.0),
        "cache_swa_k": normal(ks[6], (DEPTH, DEC_BATCH, WINDOW, SWA_KV_HEADS, SWA_HEAD_DIM), 1.0),
        "cache_swa_v": normal(ks[7], (DEPTH, DEC_BATCH, WINDOW, SWA_KV_HEADS, SWA_HEAD_DIM), 1.0),
        "meta_tokens": normal(ks[8], (N_META, D_MODEL), 1.0),
        "w_in": normal(ks[9], (DEPTH, D_MODEL, D_IN), D_MODEL ** -0.5),
        "dn_conv_w": normal(ks[10], (DEPTH, CONV_WIDTH, DN_CONV), CONV_WIDTH ** -0.5),
        "dn_a_log": a_log(ks[11], (DEPTH, DN_HEADS)),
        "dn_dt_bias": dt_bias(ks[12], (DEPTH, DN_HEADS)),
        "dn_norm_w": gain(ks[13], (DEPTH, DN_DV)),
        "ssm_conv_w": normal(ks[14], (DEPTH, CONV_WIDTH, SSM_CONV), CONV_WIDTH ** -0.5),
        "ssm_conv_b": normal(ks[15], (DEPTH, SSM_CONV), 0.02),
        "ssm_a_log": a_log(ks[16], (DEPTH, SSM_HEADS)),
        "ssm_dt_bias": dt_bias(ks[17], (DEPTH, SSM_HEADS)),
        "ssm_d": 1.0 + normal(ks[18], (DEPTH, SSM_HEADS), 0.1),
        "ssm_norm_w": gain(ks[19], (DEPTH, SSM_INNER)),
        "swa_sinks": normal(ks[20], (DEPTH, SWA_Q_HEADS), 0.5),
        "w_out": normal(ks[21], (DEPTH, D_MIX, D_MODEL), D_MIX ** -0.5),
        "g_pre_mix": gain(ks[22], (DEPTH, D_MODEL)),
        "g_post_mix": gain(ks[23], (DEPTH, D_MODEL)),
        "g_pre_ffn": gain(ks[24], (DEPTH, D_MODEL)),
        "g_post_ffn": gain(ks[25], (DEPTH, D_MODEL)),
        "w_ffn_in": normal(ks[26], (DEPTH, D_MODEL, 2 * D_FF), D_MODEL ** -0.5),
        "w_ffn_out": normal(ks[27], (DEPTH, D_FF, D_MODEL), D_FF ** -0.5),
    }


def reference(x_prompt, x_sample, state_dn, state_dn_conv, state_ssm, state_ssm_conv, cache_swa_k, cache_swa_v,
              meta_tokens, w_in, dn_conv_w, dn_a_log, dn_dt_bias, dn_norm_w, ssm_conv_w, ssm_conv_b, ssm_a_log,
              ssm_dt_bias, ssm_d, ssm_norm_w, swa_sinks, w_out, g_pre_mix, g_post_mix, g_pre_ffn, g_post_ffn,
              w_ffn_in, w_ffn_out):
    bp = x_prompt.shape[0]
    meta = jnp.broadcast_to(meta_tokens[None].astype(x_prompt.dtype), (bp, N_META, D_MODEL))
    xp = jnp.concatenate([meta, x_prompt], axis=1)
    xs = x_sample
    pos_p = jnp.arange(xp.shape[1], dtype=jnp.int32)
    pos_s = PAST_LEN + jnp.arange(xs.shape[1], dtype=jnp.int32)
    zero_states = (jnp.zeros((bp, DN_HEADS, DN_DK, DN_DV), jnp.float32),
                   jnp.zeros((bp, CONV_WIDTH - 1, DN_CONV), xp.dtype),
                   jnp.zeros((bp, SSM_HEADS, SSM_HEADDIM, SSM_STATE), jnp.float32),
                   jnp.zeros((bp, CONV_WIDTH - 1, SSM_CONV), xp.dtype))
    new_p, new_s = [], []
    for l in range(DEPTH):
        lw = dict(w_in=w_in[l], dn_conv_w=dn_conv_w[l], dn_a_log=dn_a_log[l], dn_dt_bias=dn_dt_bias[l],
                  dn_norm_w=dn_norm_w[l], ssm_conv_w=ssm_conv_w[l], ssm_conv_b=ssm_conv_b[l],
                  ssm_a_log=ssm_a_log[l], ssm_dt_bias=ssm_dt_bias[l], ssm_d=ssm_d[l], ssm_norm_w=ssm_norm_w[l],
                  swa_sinks=swa_sinks[l], w_out=w_out[l], g_pre_mix=g_pre_mix[l], g_post_mix=g_post_mix[l],
                  g_pre_ffn=g_pre_ffn[l], g_post_ffn=g_post_ffn[l], w_ffn_in=w_ffn_in[l], w_ffn_out=w_ffn_out[l])
        xp, st_p = trunk_layer(xp, pos_p, lw, zero_states, FRONT_PAD, CHUNK, swa_prompt)
        xs, st_s = trunk_layer(xs, pos_s, lw, (state_dn[l], state_dn_conv[l], state_ssm[l], state_ssm_conv[l]),
                               0, xs.shape[1],
                               functools.partial(swa_sample, k_buf=cache_swa_k[l], v_buf=cache_swa_v[l]))
        new_p.append(st_p)
        new_s.append(st_s)
    dn_p, dnc_p, ssm_p, ssmc_p, k_p, v_p = (jnp.stack([st[i] for st in new_p]) for i in range(6))
    dn_s, dnc_s, ssm_s, ssmc_s, k_s, v_s = (jnp.stack([st[i] for st in new_s]) for i in range(6))
    return (xp[:, N_META:], xs, dn_p, dnc_p, ssm_p, ssmc_p, k_p, v_p, dn_s, dnc_s, ssm_s, ssmc_s, k_s, v_s)
```

```python
import functools

import jax
import jax.numpy as jnp
from jax import lax
from jax.experimental import pallas as pl
from jax.experimental.pallas import tpu as pltpu

F32 = jnp.float32
BF16 = jnp.bfloat16
HI = lax.Precision.HIGHEST
NT = (((1,), (1,)), ((), ()))
TN = (((0,), (0,)), ((), ()))

N_META = 16
CONV_WIDTH = 4
CHUNK = 64
BLOCK = 128
WINDOW = 128
FRONT_PAD = BLOCK - N_META
ROPE_THETA = 10000.0
PAST_LEN = 8192
EPS = 1e-6

DN_HEADS, DN_DK, DN_DV = 4, 128, 128
DN_QK = DN_HEADS * DN_DK
DN_V = DN_HEADS * DN_DV
DN_CONV = 2 * DN_QK + DN_V
SSM_HEADS, SSM_HEADDIM, SSM_GROUPS, SSM_STATE = 4, 64, 2, 128
SSM_INNER = SSM_HEADS * SSM_HEADDIM
SSM_BC = SSM_GROUPS * SSM_STATE
SSM_CONV = SSM_INNER + 2 * SSM_BC
SWA_Q_HEADS, SWA_KV_HEADS, SWA_HEAD_DIM = 4, 2, 64
SWA_Q = SWA_Q_HEADS * SWA_HEAD_DIM
SWA_KV = SWA_KV_HEADS * SWA_HEAD_DIM
IN_WIDTHS = (DN_CONV, DN_V, DN_HEADS, DN_HEADS, SSM_CONV, SSM_INNER, SSM_HEADS, SWA_Q, SWA_KV, SWA_KV)

LANES = 128
SUBLANES = 8
COL_QKV = (0, DN_CONV)
COL_DNZ = (1536, DN_V)
COL_SSZ = (2048, SSM_INNER)
COL_XBC = (2304, SSM_CONV)
COL_SWQ = (3072, SWA_Q)
COL_SWK = (3328, SWA_KV)
COL_SWV = (3456, SWA_KV)
COL_SM = (3584, LANES)
D_PROJ = 3712
SM_B, SM_A, SM_DT = 0, 4, 8
PRM_DN_ALOG, PRM_DN_DTB, PRM_SSM_ALOG, PRM_SSM_DTB, PRM_SINK = 0, 1, 2, 3, 4
NEG_BIG = -1e30
VMEM_LIMIT = 56 * 1024 * 1024


def _bdot(a, b, dims=None):
    a = a.astype(BF16)
    b = b.astype(BF16)
    if dims is None:
        return jnp.dot(a, b, preferred_element_type=F32)
    return lax.dot_general(a, b, dims, preferred_element_type=F32)


def _hdot(a, b):
    return jnp.dot(a, b, precision=HI, preferred_element_type=F32)


def _rmsnorm(x, g):
    return x * lax.rsqrt(jnp.mean(x * x, axis=-1, keepdims=True) + EPS) * g


def _l2norm(x):
    return x * lax.rsqrt(jnp.sum(x * x, axis=-1, keepdims=True) + EPS)


def _sigmoid(x):
    return 1.0 / (1.0 + jnp.exp(-x))


def _silu(x):
    return x * _sigmoid(x)


def _softplus(x):
    return jnp.maximum(x, 0.0) + jnp.log1p(jnp.exp(-jnp.abs(x)))


def _pick_tile(n, target):
    best = None
    for t in range(SUBLANES, min(n, target) + 1, SUBLANES):
        if n % t == 0:
            best = t
    assert best is not None, n
    return best


def _const_spec(shape):
    nd = len(shape)
    return pl.BlockSpec(shape, lambda *_: (0,) * nd, pipeline_mode=pl.Buffered(1))


def _inproj_kernel(x_ref, g_ref, w_ref, o_ref, *, tm, pad_ranges):
    h = _rmsnorm(x_ref[...], g_ref[...])
    if pad_ranges:
        r = pl.program_id(0) * tm + lax.broadcasted_iota(jnp.int32, (tm, 1), 0)
        is_pad = None
        for lo, hi in pad_ranges:
            m = (r >= lo) & (r < hi)
            is_pad = m if is_pad is None else (is_pad | m)
        h = jnp.where(is_pad, 0.0, h)
    o_ref[...] = jnp.dot(h.astype(BF16), w_ref[...], preferred_element_type=F32)


def _inproj(x, g, w, pad_ranges):
    n, d = x.shape
    tm = _pick_tile(n, 256)
    return pl.pallas_call(
        functools.partial(_inproj_kernel, tm=tm, pad_ranges=tuple(pad_ranges)),
        out_shape=jax.ShapeDtypeStruct((n, D_PROJ), F32),
        grid=(n // tm,),
        in_specs=[pl.BlockSpec((tm, d), lambda i: (i, 0)), _const_spec((1, d)), _const_spec((d, D_PROJ))],
        out_specs=pl.BlockSpec((tm, D_PROJ), lambda i: (i, 0)),
        compiler_params=pltpu.CompilerParams(dimension_semantics=("arbitrary",), vmem_limit_bytes=VMEM_LIMIT),
        name="inproj",
    )(x, g, w)


def _causal_conv(xbuf, raw_ref, cb_ref, cw_ref, cbo_ref, first, rows):
    @pl.when(first)
    def _():
        xbuf[0:SUBLANES, :] = cb_ref[0]
    xbuf[SUBLANES:SUBLANES + rows, :] = raw_ref[...]
    base = SUBLANES - (CONV_WIDTH - 1)
    cw = cw_ref[...]
    acc = xbuf[base:base + rows, :] * cw[0:1, :]
    for i in range(1, CONV_WIDTH):
        acc = acc + xbuf[base + i:base + i + rows, :] * cw[i:i + 1, :]
    tail = xbuf[rows:rows + SUBLANES, :]
    cbo_ref[0] = tail
    xbuf[0:SUBLANES, :] = tail
    return acc


def _chunk_masks(c):
    ii = lax.broadcasted_iota(jnp.int32, (c, c), 0)
    jj = lax.broadcasted_iota(jnp.int32, (c, c), 1)
    return ii >= jj, ii > jj, (ii == jj).astype(F32)


def _segment_decay(gc, gct, lane, ge):
    col = gc[:, lane:lane + 1]
    row = gct[lane:lane + 1, :]
    return jnp.where(ge, jnp.exp(jnp.where(ge, col - row, 0.0)), 0.0)


def _inv_unit_lower(a, c, eye):
    x = eye - a
    p = _hdot(a, a)
    n = 2
    while n < c:
        x = x + _hdot(x, p)
        n *= 2
        if n < c:
            p = _hdot(p, p)
    return x


def _dn_kernel(qkv_ref, z_ref, sm_ref, s0_ref, cb_ref, cw_ref, prm_ref, nw_ref,
               o_ref, so_ref, cbo_ref, xbuf, *, rows, chunk, front_pad):
    blk = pl.program_id(1)
    first = blk == 0

    @pl.when(first)
    def _():
        so_ref[...] = s0_ref[...]

    qkv = _silu(_causal_conv(xbuf, qkv_ref, cb_ref, cw_ref, cbo_ref, first, rows))
    sm = sm_ref[...]
    prm = prm_ref[...]
    beta_all = _sigmoid(sm)
    g_all = -jnp.exp(prm[PRM_DN_ALOG:PRM_DN_ALOG + 1, :]) * _softplus(sm + prm[PRM_DN_DTB:PRM_DN_DTB + 1, :])
    if front_pad:
        pos = blk * rows + lax.broadcasted_iota(jnp.int32, (rows, 1), 0)
        g_all = jnp.where(pos < front_pad, 0.0, g_all)
    z = z_ref[...]
    nw = nw_ref[...]
    ge, gt, eye = _chunk_masks(chunk)
    lmat = ge.astype(F32)
    for c in range(rows // chunk):
        r0 = c * chunk
        gc = _hdot(lmat, g_all[r0:r0 + chunk, :])
        gct = gc.T
        glast = gc[chunk - 1:chunk, :]
        eg = jnp.exp(gc)
        ekd = jnp.exp(glast - gc)
        egl = jnp.exp(glast)
        for h in range(DN_HEADS):
            lo = h * DN_DK
            qh = _l2norm(qkv[r0:r0 + chunk, lo:lo + DN_DK]) * DN_DK ** -0.5
            kh = _l2norm(qkv[r0:r0 + chunk, DN_QK + lo:DN_QK + lo + DN_DK])
            vh = qkv[r0:r0 + chunk, 2 * DN_QK + lo:2 * DN_QK + lo + DN_DV]
            beta = beta_all[r0:r0 + chunk, SM_B + h:SM_B + h + 1]
            eg_h = eg[:, SM_A + h:SM_A + h + 1]
            dec = _segment_decay(gc, gct, SM_A + h, ge)
            kb = kh.astype(BF16)
            a = jnp.where(gt, beta * _bdot(kb, kb, NT) * dec, 0.0)
            t = _inv_unit_lower(a, chunk, eye)
            u = _hdot(t, vh * beta)
            w = _hdot(t, kh * (beta * eg_h))
            attn = _bdot(qh, kb, NT) * dec
            s = so_ref[0, h]
            sb = s.astype(BF16)
            v_new = u - _bdot(w, sb)
            o = _bdot(qh * eg_h, sb) + _bdot(attn, v_new)
            kd = kh * ekd[:, SM_A + h:SM_A + h + 1]
            so_ref[0, h] = s * egl[:, SM_A + h:SM_A + h + 1] + _bdot(kd, v_new, TN)
            zh = z[r0:r0 + chunk, lo:lo + DN_DV]
            o_ref[r0:r0 + chunk, lo:lo + DN_DV] = _rmsnorm(o, nw) * _silu(zh)


def _dn_mixer(proj, row0, nb, nblk, rows, chunk, front_pad, s0, cbuf, cw, prm, nw):
    base = row0 // rows
    rmap = lambda b, i: base + b * nblk + i
    col = lambda c: c[0] // c[1]
    n_out = nb * nblk * rows
    return pl.pallas_call(
        functools.partial(_dn_kernel, rows=rows, chunk=chunk, front_pad=front_pad),
        out_shape=(jax.ShapeDtypeStruct((n_out, DN_V), F32),
                   jax.ShapeDtypeStruct((nb, DN_HEADS, DN_DK, DN_DV), F32),
                   jax.ShapeDtypeStruct((nb, SUBLANES, DN_CONV), F32)),
        grid=(nb, nblk),
        in_specs=[
            pl.BlockSpec((rows, DN_CONV), lambda b, i: (rmap(b, i), col(COL_QKV))),
            pl.BlockSpec((rows, DN_V), lambda b, i: (rmap(b, i), col(COL_DNZ))),
            pl.BlockSpec((rows, LANES), lambda b, i: (rmap(b, i), col(COL_SM))),
            pl.BlockSpec((1, DN_HEADS, DN_DK, DN_DV), lambda b, i: (b, 0, 0, 0)),
            pl.BlockSpec((1, SUBLANES, DN_CONV), lambda b, i: (b, 0, 0)),
            pl.BlockSpec((CONV_WIDTH, DN_CONV), lambda b, i: (0, 0)),
            pl.BlockSpec((SUBLANES, LANES), lambda b, i: (0, 0)),
            pl.BlockSpec((1, DN_DV), lambda b, i: (0, 0)),
        ],
        out_specs=(
            pl.BlockSpec((rows, DN_V), lambda b, i: (b * nblk + i, 0)),
            pl.BlockSpec((1, DN_HEADS, DN_DK, DN_DV), lambda b, i: (b, 0, 0, 0)),
            pl.BlockSpec((1, SUBLANES, DN_CONV), lambda b, i: (b, 0, 0)),
        ),
        scratch_shapes=[pltpu.VMEM((rows + SUBLANES, DN_CONV), F32)],
        compiler_params=pltpu.CompilerParams(dimension_semantics=("arbitrary", "arbitrary"),
                                             vmem_limit_bytes=VMEM_LIMIT),
        name="dn_mixer",
    )(proj, proj, proj, s0, cbuf, cw, prm, nw)


def _ssd_kernel(xbc_ref, z_ref, sm_ref, h0_ref, cb_ref, cw_ref, cbias_ref, prm_ref, drow_ref, nw_ref,
                y_ref, ho_ref, cbo_ref, xbuf, *, rows, chunk, front_pad):
    blk = pl.program_id(1)
    first = blk == 0

    @pl.when(first)
    def _():
        ho_ref[...] = h0_ref[...]

    act = _silu(_causal_conv(xbuf, xbc_ref, cb_ref, cw_ref, cbo_ref, first, rows) + cbias_ref[...])
    sm = sm_ref[...]
    prm = prm_ref[...]
    dt_all = _softplus(sm + prm[PRM_SSM_DTB:PRM_SSM_DTB + 1, :])
    if front_pad:
        pos = blk * rows + lax.broadcasted_iota(jnp.int32, (rows, 1), 0)
        dt_all = jnp.where(pos < front_pad, 0.0, dt_all)
    g_all = dt_all * (-jnp.exp(prm[PRM_SSM_ALOG:PRM_SSM_ALOG + 1, :]))
    z = z_ref[...]
    nw = nw_ref[...]
    drow = drow_ref[...]
    ge, _, _ = _chunk_masks(chunk)
    lmat = ge.astype(F32)
    heads_per_group = SSM_HEADS // SSM_GROUPS
    gw = heads_per_group * SSM_HEADDIM
    lane = lax.broadcasted_iota(jnp.int32, (1, gw), 1)
    srow = lax.broadcasted_iota(jnp.int32, (gw, 1), 0)
    for c in range(rows // chunk):
        r0 = c * chunk
        gc = _hdot(lmat, g_all[r0:r0 + chunk, :])
        gct = gc.T
        glast = gc[chunk - 1:chunk, :]
        eg = jnp.exp(gc)
        ekd = jnp.exp(glast - gc)
        egl = jnp.exp(glast)
        for g in range(SSM_GROUPS):
            xg = act[r0:r0 + chunk, g * gw:(g + 1) * gw]
            bg = act[r0:r0 + chunk, SSM_INNER + g * SSM_STATE:SSM_INNER + (g + 1) * SSM_STATE]
            cg = act[r0:r0 + chunk, SSM_INNER + SSM_BC + g * SSM_STATE:SSM_INNER + SSM_BC + (g + 1) * SSM_STATE]
            cb = _bdot(cg, bg, NT)
            hs = ho_ref[0, g]
            hb = hs.astype(BF16)
            y_scan = jnp.zeros((chunk, gw), F32)
            h_new = None
            gl_col = None
            for j in range(heads_per_group):
                h = g * heads_per_group + j
                ln = SM_DT + h
                in_head = (lane >= j * SSM_HEADDIM) & (lane < (j + 1) * SSM_HEADDIM)
                dec = _segment_decay(gc, gct, ln, ge)
                xdt = jnp.where(in_head, xg * dt_all[r0:r0 + chunk, ln:ln + 1], 0.0)
                y_intra = _bdot(cb * dec, xdt)
                y_inter = _bdot(cg * eg[:, ln:ln + 1], hb, NT)
                y_scan = y_scan + y_intra + jnp.where(in_head, y_inter, 0.0)
                upd = _bdot(xdt, bg * ekd[:, ln:ln + 1], TN)
                h_new = upd if h_new is None else h_new + upd
                gl_h = egl[:, ln:ln + 1]
                gl_col = gl_h if gl_col is None else jnp.where(srow < j * SSM_HEADDIM, gl_col, gl_h)
            ho_ref[0, g] = hs * gl_col + h_new
            yg = (y_scan + xg * drow[:, g * gw:(g + 1) * gw]) * _silu(z[r0:r0 + chunk, g * gw:(g + 1) * gw])
            y_ref[r0:r0 + chunk, g * gw:(g + 1) * gw] = _rmsnorm(yg, nw[:, g * gw:(g + 1) * gw])


def _ssd_mixer(proj, row0, nb, nblk, rows, chunk, front_pad, h0, cbuf, cw, cbias, prm, drow, nw):
    base = row0 // rows
    rmap = lambda b, i: base + b * nblk + i
    col = lambda c: c[0] // c[1]
    n_out = nb * nblk * rows
    gw = (SSM_HEADS // SSM_GROUPS) * SSM_HEADDIM
    return pl.pallas_call(
        functools.partial(_ssd_kernel, rows=rows, chunk=chunk, front_pad=front_pad),
        out_shape=(jax.ShapeDtypeStruct((n_out, SSM_INNER), F32),
                   jax.ShapeDtypeStruct((nb, SSM_GROUPS, gw, SSM_STATE), F32),
                   jax.ShapeDtypeStruct((nb, SUBLANES, SSM_CONV), F32)),
        grid=(nb, nblk),
        in_specs=[
            pl.BlockSpec((rows, SSM_CONV), lambda b, i: (rmap(b, i), col(COL_XBC))),
            pl.BlockSpec((rows, SSM_INNER), lambda b, i: (rmap(b, i), col(COL_SSZ))),
            pl.BlockSpec((rows, LANES), lambda b, i: (rmap(b, i), col(COL_SM))),
            pl.BlockSpec((1, SSM_GROUPS, gw, SSM_STATE), lambda b, i: (b, 0, 0, 0)),
            pl.BlockSpec((1, SUBLANES, SSM_CONV), lambda b, i: (b, 0, 0)),
            pl.BlockSpec((CONV_WIDTH, SSM_CONV), lambda b, i: (0, 0)),
            pl.BlockSpec((1, SSM_CONV), lambda b, i: (0, 0)),
            pl.BlockSpec((SUBLANES, LANES), lambda b, i: (0, 0)),
            pl.BlockSpec((1, SSM_INNER), lambda b, i: (0, 0)),
            pl.BlockSpec((1, SSM_INNER), lambda b, i: (0, 0)),
        ],
        out_specs=(
            pl.BlockSpec((rows, SSM_INNER), lambda b, i: (b * nblk + i, 0)),
            pl.BlockSpec((1, SSM_GROUPS, gw, SSM_STATE), lambda b, i: (b, 0, 0, 0)),
            pl.BlockSpec((1, SUBLANES, SSM_CONV), lambda b, i: (b, 0, 0)),
        ),
        scratch_shapes=[pltpu.VMEM((rows + SUBLANES, SSM_CONV), F32)],
        compiler_params=pltpu.CompilerParams(dimension_semantics=("arbitrary", "arbitrary"),
                                             vmem_limit_bytes=VMEM_LIMIT),
        name="ssd_mixer",
    )(proj, proj, proj, h0, cbuf, cw, cbias, prm, drow, nw)


def _rope(x, cos, sin_signed):
    w = x.shape[-1]
    half = SWA_HEAD_DIM // 2
    lane = lax.broadcasted_iota(jnp.int32, (1, w), 1)
    first_half = (lane % SWA_HEAD_DIM) < half
    swapped = jnp.where(first_half, pltpu.roll(x, w - half, axis=1), pltpu.roll(x, half, axis=1))
    return x * cos + swapped * sin_signed


def _sink_attend(qh, keys, vals, masks, sink):
    ss = [jnp.where(m, _bdot(qh, k, NT) * SWA_HEAD_DIM ** -0.5, NEG_BIG) for k, m in zip(keys, masks)]
    mx = sink
    for s in ss:
        mx = jnp.maximum(mx, jnp.max(s, axis=-1, keepdims=True))
    den = jnp.exp(sink - mx)
    acc = None
    for s, v in zip(ss, vals):
        p = jnp.exp(s - mx)
        den = den + jnp.sum(p, axis=-1, keepdims=True)
        pv = _bdot(p, v)
        acc = pv if acc is None else acc + pv
    return acc / den


def _swa_prompt_kernel(q_ref, k_ref, v_ref, cos_ref, sin_ref, prm_ref, o_ref, ko_ref, vo_ref, kprev, vprev,
                       *, front_pad):
    blk = pl.program_id(1)

    @pl.when(blk == 0)
    def _():
        kprev[...] = jnp.zeros_like(kprev)
        vprev[...] = jnp.zeros_like(vprev)

    cos = cos_ref[...]
    sin = sin_ref[...]
    q = _rope(q_ref[...], jnp.concatenate([cos, cos], axis=1), jnp.concatenate([sin, sin], axis=1))
    k = _rope(k_ref[...], cos, sin)
    v = v_ref[...]
    kp = kprev[...]
    vp = vprev[...]
    qi = lax.broadcasted_iota(jnp.int32, (BLOCK, BLOCK), 0)
    kj = lax.broadcasted_iota(jnp.int32, (BLOCK, BLOCK), 1)
    mask_cur = (kj <= qi) & (blk * BLOCK + kj >= front_pad)
    mask_prev = (kj > qi) & ((blk - 1) * BLOCK + kj >= front_pad)
    prm = prm_ref[...]
    grp = SWA_Q_HEADS // SWA_KV_HEADS
    for h in range(SWA_Q_HEADS):
        j = h // grp
        ks = slice(j * SWA_HEAD_DIM, (j + 1) * SWA_HEAD_DIM)
        qs = slice(h * SWA_HEAD_DIM, (h + 1) * SWA_HEAD_DIM)
        sink = prm[PRM_SINK:PRM_SINK + 1, h:h + 1]
        o_ref[:, qs] = _sink_attend(q[:, qs], (kp[:, ks], k[:, ks]), (vp[:, ks], v[:, ks]),
                                    (mask_prev, mask_cur), sink)
    kprev[...] = k
    vprev[...] = v
    ko_ref[0] = k
    vo_ref[0] = v


def _swa_prompt(proj, nb, nblk, front_pad, cos, sin, prm):
    col = lambda c: c[0] // c[1]
    return pl.pallas_call(
        functools.partial(_swa_prompt_kernel, front_pad=front_pad),
        out_shape=(jax.ShapeDtypeStruct((nb * nblk * BLOCK, SWA_Q), F32),
                   jax.ShapeDtypeStruct((nb, WINDOW, SWA_KV), F32),
                   jax.ShapeDtypeStruct((nb, WINDOW, SWA_KV), F32)),
        grid=(nb, nblk),
        in_specs=[
            pl.BlockSpec((BLOCK, SWA_Q), lambda b, i: (b * nblk + i, col(COL_SWQ))),
            pl.BlockSpec((BLOCK, SWA_KV), lambda b, i: (b * nblk + i, col(COL_SWK))),
            pl.BlockSpec((BLOCK, SWA_KV), lambda b, i: (b * nblk + i, col(COL_SWV))),
            pl.BlockSpec((BLOCK, SWA_KV), lambda b, i: (i, 0)),
            pl.BlockSpec((BLOCK, SWA_KV), lambda b, i: (i, 0)),
            pl.BlockSpec((SUBLANES, LANES), lambda b, i: (0, 0)),
        ],
        out_specs=(
            pl.BlockSpec((BLOCK, SWA_Q), lambda b, i: (b * nblk + i, 0)),
            pl.BlockSpec((1, WINDOW, SWA_KV), lambda b, i: (b, 0, 0)),
            pl.BlockSpec((1, WINDOW, SWA_KV), lambda b, i: (b, 0, 0)),
        ),
        scratch_shapes=[pltpu.VMEM((BLOCK, SWA_KV), F32), pltpu.VMEM((BLOCK, SWA_KV), F32)],
        compiler_params=pltpu.CompilerParams(dimension_semantics=("arbitrary", "arbitrary"),
                                             vmem_limit_bytes=VMEM_LIMIT),
        name="swa_prompt",
    )(proj, proj, proj, cos, sin, prm)


def _swa_sample_kernel(q_ref, k_ref, v_ref, kc_ref, vc_ref, cos_ref, sin_ref, prm_ref, o_ref, ko_ref, vo_ref,
                       *, nseq, steps):
    cos = cos_ref[...]
    sin = sin_ref[...]
    cos_q = jnp.concatenate([cos, cos], axis=1)
    sin_q = jnp.concatenate([sin, sin], axis=1)
    prm = prm_ref[...]
    ti = lax.broadcasted_iota(jnp.int32, (steps, WINDOW), 0)
    sj = lax.broadcasted_iota(jnp.int32, (steps, WINDOW), 1)
    mask_cache = sj > ti
    tn = lax.broadcasted_iota(jnp.int32, (steps, steps), 0)
    sn = lax.broadcasted_iota(jnp.int32, (steps, steps), 1)
    mask_new = sn <= tn
    grp = SWA_Q_HEADS // SWA_KV_HEADS

    def body(b, carry):
        r0 = pl.multiple_of(b * steps, steps)
        q = _rope(q_ref[pl.ds(r0, steps), :], cos_q, sin_q)
        k = _rope(k_ref[pl.ds(r0, steps), :], cos, sin)
        v = v_ref[pl.ds(r0, steps), :]
        kc = kc_ref[b]
        vc = vc_ref[b]
        ko_ref[b, 0:WINDOW - steps, :] = kc[steps:WINDOW, :]
        ko_ref[b, WINDOW - steps:WINDOW, :] = k
        vo_ref[b, 0:WINDOW - steps, :] = vc[steps:WINDOW, :]
        vo_ref[b, WINDOW - steps:WINDOW, :] = v
        for h in range(SWA_Q_HEADS):
            j = h // grp
            ks = slice(j * SWA_HEAD_DIM, (j + 1) * SWA_HEAD_DIM)
            qs = slice(h * SWA_HEAD_DIM, (h + 1) * SWA_HEAD_DIM)
            sink = prm[PRM_SINK:PRM_SINK + 1, h:h + 1]
            o_ref[pl.ds(r0, steps), qs] = _sink_attend(q[:, qs], (kc[:, ks], k[:, ks]), (vc[:, ks], v[:, ks]),
                                                       (mask_cache, mask_new), sink)
        return carry

    lax.fori_loop(0, nseq, body, 0)


def _swa_sample(proj, row0, nb, steps, kc, vc, cos, sin, prm):
    assert steps == SUBLANES and WINDOW > steps
    nseq = _pick_tile(nb, 8) if nb % SUBLANES == 0 else nb
    base = row0 // (nseq * steps)
    assert row0 % (nseq * steps) == 0
    col = lambda c: c[0] // c[1]
    return pl.pallas_call(
        functools.partial(_swa_sample_kernel, nseq=nseq, steps=steps),
        out_shape=(jax.ShapeDtypeStruct((nb * steps, SWA_Q), F32),
                   jax.ShapeDtypeStruct((nb, WINDOW, SWA_KV), F32),
                   jax.ShapeDtypeStruct((nb, WINDOW, SWA_KV), F32)),
        grid=(nb // nseq,),
        in_specs=[
            pl.BlockSpec((nseq * steps, SWA_Q), lambda i: (base + i, col(COL_SWQ))),
            pl.BlockSpec((nseq * steps, SWA_KV), lambda i: (base + i, col(COL_SWK))),
            pl.BlockSpec((nseq * steps, SWA_KV), lambda i: (base + i, col(COL_SWV))),
            pl.BlockSpec((nseq, WINDOW, SWA_KV), lambda i: (i, 0, 0)),
            pl.BlockSpec((nseq, WINDOW, SWA_KV), lambda i: (i, 0, 0)),
            pl.BlockSpec((steps, SWA_KV), lambda i: (0, 0)),
            pl.BlockSpec((steps, SWA_KV), lambda i: (0, 0)),
            pl.BlockSpec((SUBLANES, LANES), lambda i: (0, 0)),
        ],
        out_specs=(
            pl.BlockSpec((nseq * steps, SWA_Q), lambda i: (i, 0)),
            pl.BlockSpec((nseq, WINDOW, SWA_KV), lambda i: (i, 0, 0)),
            pl.BlockSpec((nseq, WINDOW, SWA_KV), lambda i: (i, 0, 0)),
        ),
        compiler_params=pltpu.CompilerParams(dimension_semantics=("arbitrary",), vmem_limit_bytes=VMEM_LIMIT),
        name="swa_sample",
    )(proj, proj, proj, kc, vc, cos, sin, prm)


def _tail_kernel(x_ref, odn_ref, y_ref, osw_ref, wout_ref, g1_ref, g2_ref, g3_ref, wfi_ref, wfo_ref, o_ref, *, d_ff):
    mixed = jnp.concatenate([odn_ref[...], y_ref[...], osw_ref[...]], axis=1).astype(BF16)
    m = jnp.dot(mixed, wout_ref[...], preferred_element_type=F32)
    x1 = x_ref[...] + _rmsnorm(m, g1_ref[...])
    h = _rmsnorm(x1, g2_ref[...]).astype(BF16)
    gu = jnp.dot(h, wfi_ref[...], preferred_element_type=F32)
    act = (_silu(gu[:, :d_ff]) * gu[:, d_ff:]).astype(BF16)
    y2 = jnp.dot(act, wfo_ref[...], preferred_element_type=F32)
    o_ref[...] = x1 + _rmsnorm(y2, g3_ref[...])


def _tail(x, odn, y, osw, wout, g1, g2, g3, wfi, wfo):
    n, d = x.shape
    d_ff = wfo.shape[0]
    tm = _pick_tile(n, 256)
    row = lambda w: pl.BlockSpec((tm, w), lambda i: (i, 0))
    return pl.pallas_call(
        functools.partial(_tail_kernel, d_ff=d_ff),
        out_shape=jax.ShapeDtypeStruct((n, d), F32),
        grid=(n // tm,),
        in_specs=[row(d), row(DN_V), row(SSM_INNER), row(SWA_Q), _const_spec(wout.shape),
                  _const_spec((1, d)), _const_spec((1, d)), _const_spec((1, d)),
                  _const_spec(wfi.shape), _const_spec(wfo.shape)],
        out_specs=row(d),
        compiler_params=pltpu.CompilerParams(dimension_semantics=("arbitrary",), vmem_limit_bytes=VMEM_LIMIT),
        name="outproj_ffn",
    )(x, odn, y, osw, wout, g1, g2, g3, wfi, wfo)


def _reorder_w_in(w):
    offs = [0]
    for wd in IN_WIDTHS:
        offs.append(offs[-1] + wd)
    seg = lambda i: w[:, offs[i]:offs[i + 1]]
    dn_qkv, dn_z, dn_b, dn_a, ssm_xbc, ssm_z, ssm_dt, sw_q, sw_k, sw_v = (seg(i) for i in range(len(IN_WIDTHS)))
    small = jnp.concatenate([dn_b, dn_a, ssm_dt], axis=1)
    small = jnp.pad(small, ((0, 0), (0, LANES - small.shape[1])))
    out = jnp.concatenate([dn_qkv, dn_z, ssm_z, ssm_xbc, sw_q, sw_k, sw_v, small], axis=1)
    assert out.shape[1] == D_PROJ
    return out.astype(BF16)


def _lane_row(pairs):
    row = jnp.zeros((LANES,), F32)
    for off, vec in pairs:
        row = row.at[off:off + vec.shape[0]].set(vec.astype(F32))
    return row[None, :]


def _rope_tables(pos):
    half = SWA_HEAD_DIM // 2
    inv = ROPE_THETA ** (-jnp.arange(half, dtype=F32) / half)
    ang = pos.astype(F32)[:, None] * inv[None, :]
    cos = jnp.cos(ang)
    sin = jnp.sin(ang)
    cos_t = jnp.concatenate([cos, cos] * SWA_KV_HEADS, axis=1)
    sin_t = jnp.concatenate([-sin, sin] * SWA_KV_HEADS, axis=1)
    return cos_t, sin_t


def _pad_conv_state(buf):
    return jnp.pad(buf, ((0, 0), (SUBLANES - (CONV_WIDTH - 1), 0), (0, 0)))


def kernel(x_prompt, x_sample, state_dn, state_dn_conv, state_ssm, state_ssm_conv, cache_swa_k, cache_swa_v,
           meta_tokens, w_in, dn_conv_w, dn_a_log, dn_dt_bias, dn_norm_w, ssm_conv_w, ssm_conv_b, ssm_a_log,
           ssm_dt_bias, ssm_d, ssm_norm_w, swa_sinks, w_out, g_pre_mix, g_post_mix, g_pre_ffn, g_post_ffn,
           w_ffn_in, w_ffn_out):
    bp, seq, d = x_prompt.shape
    bs, ts, _ = x_sample.shape
    depth = w_in.shape[0]
    lp = N_META + seq + FRONT_PAD
    assert lp % BLOCK == 0 and BLOCK % CHUNK == 0
    nblk = lp // BLOCK

    zpad = jnp.zeros((FRONT_PAD, d), x_prompt.dtype)
    meta = meta_tokens.astype(x_prompt.dtype)
    xp = jnp.concatenate([jnp.concatenate([zpad, meta, x_prompt[b]], axis=0) for b in range(bp)], axis=0)
    xs = x_sample.reshape(bs * ts, d)
    pad_ranges = [(b * lp, b * lp + FRONT_PAD) for b in range(bp)]

    cos_p, sin_p = _rope_tables(jnp.arange(lp, dtype=jnp.int32) - FRONT_PAD)
    cos_s, sin_s = _rope_tables(PAST_LEN + jnp.arange(ts, dtype=jnp.int32))

    gw = (SSM_HEADS // SSM_GROUPS) * SSM_HEADDIM
    zero_dn = jnp.zeros((bp, DN_HEADS, DN_DK, DN_DV), F32)
    zero_dnc = jnp.zeros((bp, SUBLANES, DN_CONV), F32)
    zero_ssm = jnp.zeros((bp, SSM_GROUPS, gw, SSM_STATE), F32)
    zero_ssmc = jnp.zeros((bp, SUBLANES, SSM_CONV), F32)

    new_p, new_s = [], []
    for l in range(depth):
        w_in_l = _reorder_w_in(w_in[l])
        w_out_l = w_out[l].astype(BF16)
        w_fi_l = w_ffn_in[l].astype(BF16)
        w_fo_l = w_ffn_out[l].astype(BF16)
        prm = jnp.concatenate([
            _lane_row([(SM_A, dn_a_log[l])]),
            _lane_row([(SM_A, dn_dt_bias[l])]),
            _lane_row([(SM_DT, ssm_a_log[l])]),
            _lane_row([(SM_DT, ssm_dt_bias[l])]),
            _lane_row([(0, swa_sinks[l])]),
            jnp.zeros((SUBLANES - 5, LANES), F32)], axis=0)
        dn_nw = dn_norm_w[l][None, :]
        ssm_nw = ssm_norm_w[l][None, :]
        drow = jnp.repeat(ssm_d[l], SSM_HEADDIM)[None, :]
        cbias = ssm_conv_b[l][None, :]
        g1, g2, g3, g4 = (a[l][None, :] for a in (g_pre_mix, g_post_mix, g_pre_ffn, g_post_ffn))

        proj = _inproj(xp, g1, w_in_l, pad_ranges)
        odn, dn_p, dnc_p = _dn_mixer(proj, 0, bp, nblk, BLOCK, CHUNK, FRONT_PAD, zero_dn, zero_dnc,
                                     dn_conv_w[l], prm, dn_nw)
        ys, ssm_p, ssmc_p = _ssd_mixer(proj, 0, bp, nblk, BLOCK, CHUNK, FRONT_PAD, zero_ssm, zero_ssmc,
                                       ssm_conv_w[l], cbias, prm, drow, ssm_nw)
        osw, k_p, v_p = _swa_prompt(proj, bp, nblk, FRONT_PAD, cos_p, sin_p, prm)
        xp = _tail(xp, odn, ys, osw, w_out_l, g2, g3, g4, w_fi_l, w_fo_l)
        new_p.append((dn_p, dnc_p[:, -(CONV_WIDTH - 1):], ssm_p.reshape(bp, SSM_HEADS, SSM_HEADDIM, SSM_STATE),
                      ssmc_p[:, -(CONV_WIDTH - 1):], k_p.reshape(bp, WINDOW, SWA_KV_HEADS, SWA_HEAD_DIM),
                      v_p.reshape(bp, WINDOW, SWA_KV_HEADS, SWA_HEAD_DIM)))

        proj = _inproj(xs, g1, w_in_l, ())
        odn, dn_s, dnc_s = _dn_mixer(proj, 0, bs, 1, ts, ts, 0, state_dn[l], _pad_conv_state(state_dn_conv[l]),
                                     dn_conv_w[l], prm, dn_nw)
        ys, ssm_s, ssmc_s = _ssd_mixer(proj, 0, bs, 1, ts, ts, 0,
                                       state_ssm[l].reshape(bs, SSM_GROUPS, gw, SSM_STATE),
                                       _pad_conv_state(state_ssm_conv[l]), ssm_conv_w[l], cbias, prm, drow, ssm_nw)
        osw, k_s, v_s = _swa_sample(proj, 0, bs, ts, cache_swa_k[l].reshape(bs, WINDOW, SWA_KV),
                                    cache_swa_v[l].reshape(bs, WINDOW, SWA_KV), cos_s, sin_s, prm)
        xs = _tail(xs, odn, ys, osw, w_out_l, g2, g3, g4, w_fi_l, w_fo_l)
        new_s.append((dn_s, dnc_s[:, -(CONV_WIDTH - 1):], ssm_s.reshape(bs, SSM_HEADS, SSM_HEADDIM, SSM_STATE),
                      ssmc_s[:, -(CONV_WIDTH - 1):], k_s.reshape(bs, WINDOW, SWA_KV_HEADS, SWA_HEAD_DIM),
                      v_s.reshape(bs, WINDOW, SWA_KV_HEADS, SWA_HEAD_DIM)))

    outs_p = tuple(jnp.stack([st[i] for st in new_p]) for i in range(6))
    outs_s = tuple(jnp.stack([st[i] for st in new_s]) for i in range(6))
    y_prompt = xp.reshape(bp, lp, d)[:, FRONT_PAD + N_META:]
    return (y_prompt, xs.reshape(bs, ts, d)) + outs_p + outs_s
```

```python
import functools

import jax
import jax.numpy as jnp
from jax import lax
from jax.experimental import pallas as pl
from jax.experimental.pallas import tpu as pltpu

F32 = jnp.float32
BF16 = jnp.bfloat16
HI = lax.Precision.HIGHEST
NT = (((1,), (1,)), ((), ()))
TN = (((0,), (0,)), ((), ()))

N_META = 16
CONV_WIDTH = 4
CHUNK = 64
BLOCK = 128
WINDOW = 128
FRONT_PAD = BLOCK - N_META
ROPE_THETA = 10000.0
PAST_LEN = 8192
EPS = 1e-6

DN_HEADS, DN_DK, DN_DV = 4, 128, 128
DN_QK = DN_HEADS * DN_DK
DN_V = DN_HEADS * DN_DV
DN_CONV = 2 * DN_QK + DN_V
SSM_HEADS, SSM_HEADDIM, SSM_GROUPS, SSM_STATE = 4, 64, 2, 128
SSM_INNER = SSM_HEADS * SSM_HEADDIM
SSM_BC = SSM_GROUPS * SSM_STATE
SSM_CONV = SSM_INNER + 2 * SSM_BC
SWA_Q_HEADS, SWA_KV_HEADS, SWA_HEAD_DIM = 4, 2, 64
SWA_Q = SWA_Q_HEADS * SWA_HEAD_DIM
SWA_KV = SWA_KV_HEADS * SWA_HEAD_DIM
IN_WIDTHS = (DN_CONV, DN_V, DN_HEADS, DN_HEADS, SSM_CONV, SSM_INNER, SSM_HEADS, SWA_Q, SWA_KV, SWA_KV)

LANES = 128
SUBLANES = 8
COL_QKV = (0, DN_CONV)
COL_DNZ = (1536, DN_V)
COL_SSZ = (2048, SSM_INNER)
COL_XBC = (2304, SSM_CONV)
COL_SWQ = (3072, SWA_Q)
COL_SWK = (3328, SWA_KV)
COL_SWV = (3456, SWA_KV)
COL_SM = (3584, LANES)
D_PROJ = 3712
SM_B, SM_A, SM_DT = 0, 4, 8
PRM_DN_ALOG, PRM_DN_DTB, PRM_SSM_ALOG, PRM_SSM_DTB, PRM_SINK = 0, 1, 2, 3, 4
NEG_BIG = -1e30
VMEM_LIMIT = 56 * 1024 * 1024


def _bdot(a, b, dims=None):
    a = a.astype(BF16)
    b = b.astype(BF16)
    if dims is None:
        return jnp.dot(a, b, preferred_element_type=F32)
    return lax.dot_general(a, b, dims, preferred_element_type=F32)


def _hdot(a, b):
    return jnp.dot(a, b, precision=HI, preferred_element_type=F32)


def _rmsnorm(x, g):
    return x * lax.rsqrt(jnp.mean(x * x, axis=-1, keepdims=True) + EPS) * g


def _l2norm(x):
    return x * lax.rsqrt(jnp.sum(x * x, axis=-1, keepdims=True) + EPS)


def _sigmoid(x):
    return 1.0 / (1.0 + jnp.exp(-x))


def _silu(x):
    return x * _sigmoid(x)


def _softplus(x):
    return jnp.maximum(x, 0.0) + jnp.log1p(jnp.exp(-jnp.abs(x)))


def _pick_tile(n, target):
    best = None
    for t in range(SUBLANES, min(n, target) + 1, SUBLANES):
        if n % t == 0:
            best = t
    assert best is not None, n
    return best


def _const_spec(shape):
    nd = len(shape)
    return pl.BlockSpec(shape, lambda *_: (0,) * nd, pipeline_mode=pl.Buffered(1))


def _inproj_kernel(x_ref, g_ref, w_ref, o_ref, *, tm, pad_ranges):
    h = _rmsnorm(x_ref[...], g_ref[...])
    if pad_ranges:
        r = pl.program_id(0) * tm + lax.broadcasted_iota(jnp.int32, (tm, 1), 0)
        is_pad = None
        for lo, hi in pad_ranges:
            m = (r >= lo) & (r < hi)
            is_pad = m if is_pad is None else (is_pad | m)
        h = jnp.where(is_pad, 0.0, h)
    o_ref[...] = jnp.dot(h.astype(BF16), w_ref[...], preferred_element_type=F32)


def _inproj(x, g, w, pad_ranges):
    n, d = x.shape
    tm = _pick_tile(n, 256)
    return pl.pallas_call(
        functools.partial(_inproj_kernel, tm=tm, pad_ranges=tuple(pad_ranges)),
        out_shape=jax.ShapeDtypeStruct((n, D_PROJ), F32),
        grid=(n // tm,),
        in_specs=[pl.BlockSpec((tm, d), lambda i: (i, 0)), _const_spec((1, d)), _const_spec((d, D_PROJ))],
        out_specs=pl.BlockSpec((tm, D_PROJ), lambda i: (i, 0)),
        compiler_params=pltpu.CompilerParams(dimension_semantics=("arbitrary",), vmem_limit_bytes=VMEM_LIMIT),
        name="inproj",
    )(x, g, w)


def _causal_conv(xbuf, raw_ref, cb_ref, cw_ref, cbo_ref, first, rows):
    @pl.when(first)
    def _():
        xbuf[0:SUBLANES, :] = cb_ref[0]
    xbuf[SUBLANES:SUBLANES + rows, :] = raw_ref[...]
    base = SUBLANES - (CONV_WIDTH - 1)
    cw = cw_ref[...]
    acc = xbuf[base:base + rows, :] * cw[0:1, :]
    for i in range(1, CONV_WIDTH):
        acc = acc + xbuf[base + i:base + i + rows, :] * cw[i:i + 1, :]
    tail = xbuf[rows:rows + SUBLANES, :]
    cbo_ref[0] = tail
    xbuf[0:SUBLANES, :] = tail
    return acc


def _chunk_masks(c):
    ii = lax.broadcasted_iota(jnp.int32, (c, c), 0)
    jj = lax.broadcasted_iota(jnp.int32, (c, c), 1)
    return ii >= jj, ii > jj, (ii == jj).astype(F32)


def _segment_decay(gc, gct, lane, ge):
    col = gc[:, lane:lane + 1]
    row = gct[lane:lane + 1, :]
    return jnp.where(ge, jnp.exp(jnp.where(ge, col - row, 0.0)), 0.0)


def _inv_unit_lower_minus_eye(a, c, nh):
    w = nh * c
    blk_r = lax.broadcasted_iota(jnp.int32, (w, w), 0) // c
    blk_c = lax.broadcasted_iota(jnp.int32, (w, w), 1) // c
    same = blk_r == blk_c

    def block_diag(p):
        return jnp.where(same, jnp.concatenate([p] * nh, axis=0), 0.0).astype(BF16)

    y = -a
    p = _bdot(a, block_diag(a))
    n = 2
    while n < c:
        pbd = block_diag(p)
        n *= 2
        if n < c:
            st = _bdot(jnp.concatenate([y, p], axis=0), pbd)
            y = y + p + st[:c]
            p = st[c:]
        else:
            y = y + p + _bdot(y, pbd)
    return y


def _dn_kernel(qkv_ref, z_ref, sm_ref, s0_ref, cb_ref, cw_ref, prm_ref, nw_ref,
               o_ref, so_ref, cbo_ref, xbuf, *, rows, chunk, front_pad):
    blk = pl.program_id(1)
    first = blk == 0

    @pl.when(first)
    def _():
        so_ref[...] = s0_ref[...]

    qkv = _silu(_causal_conv(xbuf, qkv_ref, cb_ref, cw_ref, cbo_ref, first, rows))
    sm = sm_ref[...]
    prm = prm_ref[...]
    beta_all = _sigmoid(sm)
    g_all = -jnp.exp(prm[PRM_DN_ALOG:PRM_DN_ALOG + 1, :]) * _softplus(sm + prm[PRM_DN_DTB:PRM_DN_DTB + 1, :])
    if front_pad:
        pos = blk * rows + lax.broadcasted_iota(jnp.int32, (rows, 1), 0)
        g_all = jnp.where(pos < front_pad, 0.0, g_all)
    z = z_ref[...]
    nw = nw_ref[...]
    ge, gt, _ = _chunk_masks(chunk)
    lmat = ge.astype(F32)
    wide = (chunk, DN_HEADS * chunk)
    gt_all = lax.broadcasted_iota(jnp.int32, wide, 0) > lax.broadcasted_iota(jnp.int32, wide, 1) % chunk

    prep = []
    for c in range(rows // chunk):
        r0 = c * chunk
        gc = _hdot(lmat, g_all[r0:r0 + chunk, :])
        gct = gc.T
        glast = gc[chunk - 1:chunk, :]
        eg = jnp.exp(gc)
        ekd = jnp.exp(glast - gc)
        egl = jnp.exp(glast)
        qs, ks, vs, betas = [], [], [], []
        for h in range(DN_HEADS):
            lo = h * DN_DK
            qs.append(_l2norm(qkv[r0:r0 + chunk, lo:lo + DN_DK]) * DN_DK ** -0.5)
            ks.append(_l2norm(qkv[r0:r0 + chunk, DN_QK + lo:DN_QK + lo + DN_DK]))
            vs.append(qkv[r0:r0 + chunk, 2 * DN_QK + lo:2 * DN_QK + lo + DN_DV])
            betas.append(beta_all[r0:r0 + chunk, SM_B + h:SM_B + h + 1])
        kbs = [k.astype(BF16) for k in ks]
        dec = jnp.concatenate([_segment_decay(gc, gct, SM_A + h, ge) for h in range(DN_HEADS)], axis=1)
        kk = jnp.concatenate([_bdot(kb, kb, NT) for kb in kbs], axis=1)
        qk = jnp.concatenate([_bdot(q, kb, NT) for q, kb in zip(qs, kbs)], axis=1)
        beta_w = jnp.concatenate([jnp.broadcast_to(b, (chunk, chunk)) for b in betas], axis=1)
        a = jnp.where(gt_all, beta_w * kk * dec, 0.0)
        tm = _inv_unit_lower_minus_eye(a, chunk, DN_HEADS)
        attn = (qk * dec).astype(BF16)
        heads = []
        for h in range(DN_HEADS):
            eg_h = eg[:, SM_A + h:SM_A + h + 1]
            rhs = jnp.concatenate([vs[h] * betas[h], ks[h] * (betas[h] * eg_h)], axis=1)
            uw = rhs + _bdot(tm[:, h * chunk:(h + 1) * chunk], rhs)
            wq = jnp.concatenate([uw[:, DN_DV:], qs[h] * eg_h], axis=0).astype(BF16)
            kd = (ks[h] * ekd[:, SM_A + h:SM_A + h + 1]).astype(BF16)
            heads.append((uw[:, :DN_DV], wq, attn[:, h * chunk:(h + 1) * chunk], kd, egl[:, SM_A + h:SM_A + h + 1]))
        prep.append(heads)

    states = [so_ref[0, h] for h in range(DN_HEADS)]
    for c in range(rows // chunk):
        r0 = c * chunk
        for h in range(DN_HEADS):
            u, wq, attn, kd, gl = prep[c][h]
            lo = h * DN_DV
            s = states[h]
            m1 = _bdot(wq, s)
            v_new = (u - m1[:chunk]).astype(BF16)
            o = m1[chunk:] + _bdot(attn, v_new)
            states[h] = s * gl + _bdot(kd, v_new, TN)
            zh = z[r0:r0 + chunk, lo:lo + DN_DV]
            o_ref[r0:r0 + chunk, lo:lo + DN_DV] = _rmsnorm(o, nw) * _silu(zh)
    for h in range(DN_HEADS):
        so_ref[0, h] = states[h]


def _dn_mixer(proj, row0, nb, nblk, rows, chunk, front_pad, s0, cbuf, cw, prm, nw):
    base = row0 // rows
    rmap = lambda b, i: base + b * nblk + i
    col = lambda c: c[0] // c[1]
    n_out = nb * nblk * rows
    return pl.pallas_call(
        functools.partial(_dn_kernel, rows=rows, chunk=chunk, front_pad=front_pad),
        out_shape=(jax.ShapeDtypeStruct((n_out, DN_V), F32),
                   jax.ShapeDtypeStruct((nb, DN_HEADS, DN_DK, DN_DV), F32),
                   jax.ShapeDtypeStruct((nb, SUBLANES, DN_CONV), F32)),
        grid=(nb, nblk),
        in_specs=[
            pl.BlockSpec((rows, DN_CONV), lambda b, i: (rmap(b, i), col(COL_QKV))),
            pl.BlockSpec((rows, DN_V), lambda b, i: (rmap(b, i), col(COL_DNZ))),
            pl.BlockSpec((rows, LANES), lambda b, i: (rmap(b, i), col(COL_SM))),
            pl.BlockSpec((1, DN_HEADS, DN_DK, DN_DV), lambda b, i: (b, 0, 0, 0)),
            pl.BlockSpec((1, SUBLANES, DN_CONV), lambda b, i: (b, 0, 0)),
            pl.BlockSpec((CONV_WIDTH, DN_CONV), lambda b, i: (0, 0)),
            pl.BlockSpec((SUBLANES, LANES), lambda b, i: (0, 0)),
            pl.BlockSpec((1, DN_DV), lambda b, i: (0, 0)),
        ],
        out_specs=(
            pl.BlockSpec((rows, DN_V), lambda b, i: (b * nblk + i, 0)),
            pl.BlockSpec((1, DN_HEADS, DN_DK, DN_DV), lambda b, i: (b, 0, 0, 0)),
            pl.BlockSpec((1, SUBLANES, DN_CONV), lambda b, i: (b, 0, 0)),
        ),
        scratch_shapes=[pltpu.VMEM((rows + SUBLANES, DN_CONV), F32)],
        compiler_params=pltpu.CompilerParams(dimension_semantics=("arbitrary", "arbitrary"),
                                             vmem_limit_bytes=VMEM_LIMIT),
        name="dn_mixer",
    )(proj, proj, proj, s0, cbuf, cw, prm, nw)


def _ssd_kernel(xbc_ref, z_ref, sm_ref, h0_ref, cb_ref, cw_ref, cbias_ref, prm_ref, drow_ref, nw_ref,
                y_ref, ho_ref, cbo_ref, xbuf, *, rows, chunk, front_pad):
    blk = pl.program_id(1)
    first = blk == 0

    @pl.when(first)
    def _():
        ho_ref[...] = h0_ref[...]

    act = _silu(_causal_conv(xbuf, xbc_ref, cb_ref, cw_ref, cbo_ref, first, rows) + cbias_ref[...])
    sm = sm_ref[...]
    prm = prm_ref[...]
    dt_all = _softplus(sm + prm[PRM_SSM_DTB:PRM_SSM_DTB + 1, :])
    if front_pad:
        pos = blk * rows + lax.broadcasted_iota(jnp.int32, (rows, 1), 0)
        dt_all = jnp.where(pos < front_pad, 0.0, dt_all)
    g_all = dt_all * (-jnp.exp(prm[PRM_SSM_ALOG:PRM_SSM_ALOG + 1, :]))
    z = z_ref[...]
    nw = nw_ref[...]
    drow = drow_ref[...]
    ge, _, _ = _chunk_masks(chunk)
    lmat = ge.astype(F32)
    heads_per_group = SSM_HEADS // SSM_GROUPS
    gw = heads_per_group * SSM_HEADDIM
    lane = lax.broadcasted_iota(jnp.int32, (1, gw), 1)
    srow = lax.broadcasted_iota(jnp.int32, (gw, 1), 0)
    for c in range(rows // chunk):
        r0 = c * chunk
        gc = _hdot(lmat, g_all[r0:r0 + chunk, :])
        gct = gc.T
        glast = gc[chunk - 1:chunk, :]
        eg = jnp.exp(gc)
        ekd = jnp.exp(glast - gc)
        egl = jnp.exp(glast)
        for g in range(SSM_GROUPS):
            xg = act[r0:r0 + chunk, g * gw:(g + 1) * gw]
            bg = act[r0:r0 + chunk, SSM_INNER + g * SSM_STATE:SSM_INNER + (g + 1) * SSM_STATE]
            cg = act[r0:r0 + chunk, SSM_INNER + SSM_BC + g * SSM_STATE:SSM_INNER + SSM_BC + (g + 1) * SSM_STATE]
            cb = _bdot(cg, bg, NT)
            hs = ho_ref[0, g]
            hb = hs.astype(BF16)
            y_scan = jnp.zeros((chunk, gw), F32)
            h_new = None
            gl_col = None
            for j in range(heads_per_group):
                h = g * heads_per_group + j
                ln = SM_DT + h
                in_head = (lane >= j * SSM_HEADDIM) & (lane < (j + 1) * SSM_HEADDIM)
                dec = _segment_decay(gc, gct, ln, ge)
                xdt = jnp.where(in_head, xg * dt_all[r0:r0 + chunk, ln:ln + 1], 0.0)
                y_intra = _bdot(cb * dec, xdt)
                y_inter = _bdot(cg * eg[:, ln:ln + 1], hb, NT)
                y_scan = y_scan + y_intra + jnp.where(in_head, y_inter, 0.0)
                upd = _bdot(xdt, bg * ekd[:, ln:ln + 1], TN)
                h_new = upd if h_new is None else h_new + upd
                gl_h = egl[:, ln:ln + 1]
                gl_col = gl_h if gl_col is None else jnp.where(srow < j * SSM_HEADDIM, gl_col, gl_h)
            ho_ref[0, g] = hs * gl_col + h_new
            yg = (y_scan + xg * drow[:, g * gw:(g + 1) * gw]) * _silu(z[r0:r0 + chunk, g * gw:(g + 1) * gw])
            y_ref[r0:r0 + chunk, g * gw:(g + 1) * gw] = _rmsnorm(yg, nw[:, g * gw:(g + 1) * gw])


def _ssd_mixer(proj, row0, nb, nblk, rows, chunk, front_pad, h0, cbuf, cw, cbias, prm, drow, nw):
    base = row0 // rows
    rmap = lambda b, i: base + b * nblk + i
    col = lambda c: c[0] // c[1]
    n_out = nb * nblk * rows
    gw = (SSM_HEADS // SSM_GROUPS) * SSM_HEADDIM
    return pl.pallas_call(
        functools.partial(_ssd_kernel, rows=rows, chunk=chunk, front_pad=front_pad),
        out_shape=(jax.ShapeDtypeStruct((n_out, SSM_INNER), F32),
                   jax.ShapeDtypeStruct((nb, SSM_GROUPS, gw, SSM_STATE), F32),
                   jax.ShapeDtypeStruct((nb, SUBLANES, SSM_CONV), F32)),
        grid=(nb, nblk),
        in_specs=[
            pl.BlockSpec((rows, SSM_CONV), lambda b, i: (rmap(b, i), col(COL_XBC))),
            pl.BlockSpec((rows, SSM_INNER), lambda b, i: (rmap(b, i), col(COL_SSZ))),
            pl.BlockSpec((rows, LANES), lambda b, i: (rmap(b, i), col(COL_SM))),
            pl.BlockSpec((1, SSM_GROUPS, gw, SSM_STATE), lambda b, i: (b, 0, 0, 0)),
            pl.BlockSpec((1, SUBLANES, SSM_CONV), lambda b, i: (b, 0, 0)),
            pl.BlockSpec((CONV_WIDTH, SSM_CONV), lambda b, i: (0, 0)),
            pl.BlockSpec((1, SSM_CONV), lambda b, i: (0, 0)),
            pl.BlockSpec((SUBLANES, LANES), lambda b, i: (0, 0)),
            pl.BlockSpec((1, SSM_INNER), lambda b, i: (0, 0)),
            pl.BlockSpec((1, SSM_INNER), lambda b, i: (0, 0)),
        ],
        out_specs=(
            pl.BlockSpec((rows, SSM_INNER), lambda b, i: (b * nblk + i, 0)),
            pl.BlockSpec((1, SSM_GROUPS, gw, SSM_STATE), lambda b, i: (b, 0, 0, 0)),
            pl.BlockSpec((1, SUBLANES, SSM_CONV), lambda b, i: (b, 0, 0)),
        ),
        scratch_shapes=[pltpu.VMEM((rows + SUBLANES, SSM_CONV), F32)],
        compiler_params=pltpu.CompilerParams(dimension_semantics=("arbitrary", "arbitrary"),
                                             vmem_limit_bytes=VMEM_LIMIT),
        name="ssd_mixer",
    )(proj, proj, proj, h0, cbuf, cw, cbias, prm, drow, nw)


def _rope(x, cos, sin_signed):
    w = x.shape[-1]
    half = SWA_HEAD_DIM // 2
    lane = lax.broadcasted_iota(jnp.int32, (1, w), 1)
    first_half = (lane % SWA_HEAD_DIM) < half
    swapped = jnp.where(first_half, pltpu.roll(x, w - half, axis=1), pltpu.roll(x, half, axis=1))
    return x * cos + swapped * sin_signed


def _sink_attend(qh, keys, vals, masks, sink):
    ss = [jnp.where(m, _bdot(qh, k, NT) * SWA_HEAD_DIM ** -0.5, NEG_BIG) for k, m in zip(keys, masks)]
    mx = sink
    for s in ss:
        mx = jnp.maximum(mx, jnp.max(s, axis=-1, keepdims=True))
    den = jnp.exp(sink - mx)
    acc = None
    for s, v in zip(ss, vals):
        p = jnp.exp(s - mx)
        den = den + jnp.sum(p, axis=-1, keepdims=True)
        pv = _bdot(p, v)
        acc = pv if acc is None else acc + pv
    return acc / den


def _swa_prompt_kernel(q_ref, k_ref, v_ref, cos_ref, sin_ref, prm_ref, o_ref, ko_ref, vo_ref, kprev, vprev,
                       *, front_pad):
    blk = pl.program_id(1)

    @pl.when(blk == 0)
    def _():
        kprev[...] = jnp.zeros_like(kprev)
        vprev[...] = jnp.zeros_like(vprev)

    cos = cos_ref[...]
    sin = sin_ref[...]
    q = _rope(q_ref[...], jnp.concatenate([cos, cos], axis=1), jnp.concatenate([sin, sin], axis=1))
    k = _rope(k_ref[...], cos, sin)
    v = v_ref[...]
    kp = kprev[...]
    vp = vprev[...]
    qi = lax.broadcasted_iota(jnp.int32, (BLOCK, BLOCK), 0)
    kj = lax.broadcasted_iota(jnp.int32, (BLOCK, BLOCK), 1)
    mask_cur = (kj <= qi) & (blk * BLOCK + kj >= front_pad)
    mask_prev = (kj > qi) & ((blk - 1) * BLOCK + kj >= front_pad)
    prm = prm_ref[...]
    grp = SWA_Q_HEADS // SWA_KV_HEADS
    for h in range(SWA_Q_HEADS):
        j = h // grp
        ks = slice(j * SWA_HEAD_DIM, (j + 1) * SWA_HEAD_DIM)
        qs = slice(h * SWA_HEAD_DIM, (h + 1) * SWA_HEAD_DIM)
        sink = prm[PRM_SINK:PRM_SINK + 1, h:h + 1]
        o_ref[:, qs] = _sink_attend(q[:, qs], (kp[:, ks], k[:, ks]), (vp[:, ks], v[:, ks]),
                                    (mask_prev, mask_cur), sink)
    kprev[...] = k
    vprev[...] = v
    ko_ref[0] = k
    vo_ref[0] = v


def _swa_prompt(proj, nb, nblk, front_pad, cos, sin, prm):
    col = lambda c: c[0] // c[1]
    return pl.pallas_call(
        functools.partial(_swa_prompt_kernel, front_pad=front_pad),
        out_shape=(jax.ShapeDtypeStruct((nb * nblk * BLOCK, SWA_Q), F32),
                   jax.ShapeDtypeStruct((nb, WINDOW, SWA_KV), F32),
                   jax.ShapeDtypeStruct((nb, WINDOW, SWA_KV), F32)),
        grid=(nb, nblk),
        in_specs=[
            pl.BlockSpec((BLOCK, SWA_Q), lambda b, i: (b * nblk + i, col(COL_SWQ))),
            pl.BlockSpec((BLOCK, SWA_KV), lambda b, i: (b * nblk + i, col(COL_SWK))),
            pl.BlockSpec((BLOCK, SWA_KV), lambda b, i: (b * nblk + i, col(COL_SWV))),
            pl.BlockSpec((BLOCK, SWA_KV), lambda b, i: (i, 0)),
            pl.BlockSpec((BLOCK, SWA_KV), lambda b, i: (i, 0)),
            pl.BlockSpec((SUBLANES, LANES), lambda b, i: (0, 0)),
        ],
        out_specs=(
            pl.BlockSpec((BLOCK, SWA_Q), lambda b, i: (b * nblk + i, 0)),
            pl.BlockSpec((1, WINDOW, SWA_KV), lambda b, i: (b, 0, 0)),
            pl.BlockSpec((1, WINDOW, SWA_KV), lambda b, i: (b, 0, 0)),
        ),
        scratch_shapes=[pltpu.VMEM((BLOCK, SWA_KV), F32), pltpu.VMEM((BLOCK, SWA_KV), F32)],
        compiler_params=pltpu.CompilerParams(dimension_semantics=("arbitrary", "arbitrary"),
                                             vmem_limit_bytes=VMEM_LIMIT),
        name="swa_prompt",
    )(proj, proj, proj, cos, sin, prm)


def _swa_sample_kernel(q_ref, k_ref, v_ref, kc_ref, vc_ref, cos_ref, sin_ref, prm_ref, o_ref, ko_ref, vo_ref,
                       *, nseq, steps):
    cos = cos_ref[...]
    sin = sin_ref[...]
    cos_q = jnp.concatenate([cos, cos], axis=1)
    sin_q = jnp.concatenate([sin, sin], axis=1)
    prm = prm_ref[...]
    ti = lax.broadcasted_iota(jnp.int32, (steps, WINDOW), 0)
    sj = lax.broadcasted_iota(jnp.int32, (steps, WINDOW), 1)
    mask_cache = sj > ti
    tn = lax.broadcasted_iota(jnp.int32, (steps, steps), 0)
    sn = lax.broadcasted_iota(jnp.int32, (steps, steps), 1)
    mask_new = sn <= tn
    grp = SWA_Q_HEADS // SWA_KV_HEADS

    def body(b, carry):
        r0 = pl.multiple_of(b * steps, steps)
        q = _rope(q_ref[pl.ds(r0, steps), :], cos_q, sin_q)
        k = _rope(k_ref[pl.ds(r0, steps), :], cos, sin)
        v = v_ref[pl.ds(r0, steps), :]
        kc = kc_ref[b]
        vc = vc_ref[b]
        ko_ref[b, 0:WINDOW - steps, :] = kc[steps:WINDOW, :]
        ko_ref[b, WINDOW - steps:WINDOW, :] = k
        vo_ref[b, 0:WINDOW - steps, :] = vc[steps:WINDOW, :]
        vo_ref[b, WINDOW - steps:WINDOW, :] = v
        for h in range(SWA_Q_HEADS):
            j = h // grp
            ks = slice(j * SWA_HEAD_DIM, (j + 1) * SWA_HEAD_DIM)
            qs = slice(h * SWA_HEAD_DIM, (h + 1) * SWA_HEAD_DIM)
            sink = prm[PRM_SINK:PRM_SINK + 1, h:h + 1]
            o_ref[pl.ds(r0, steps), qs] = _sink_attend(q[:, qs], (kc[:, ks], k[:, ks]), (vc[:, ks], v[:, ks]),
                                                       (mask_cache, mask_new), sink)
        return carry

    lax.fori_loop(0, nseq, body, 0)


def _swa_sample(proj, row0, nb, steps, kc, vc, cos, sin, prm):
    assert steps == SUBLANES and WINDOW > steps
    nseq = _pick_tile(nb, 8) if nb % SUBLANES == 0 else nb
    base = row0 // (nseq * steps)
    assert row0 % (nseq * steps) == 0
    col = lambda c: c[0] // c[1]
    return pl.pallas_call(
        functools.partial(_swa_sample_kernel, nseq=nseq, steps=steps),
        out_shape=(jax.ShapeDtypeStruct((nb * steps, SWA_Q), F32),
                   jax.ShapeDtypeStruct((nb, WINDOW, SWA_KV), F32),
                   jax.ShapeDtypeStruct((nb, WINDOW, SWA_KV), F32)),
        grid=(nb // nseq,),
        in_specs=[
            pl.BlockSpec((nseq * steps, SWA_Q), lambda i: (base + i, col(COL_SWQ))),
            pl.BlockSpec((nseq * steps, SWA_KV), lambda i: (base + i, col(COL_SWK))),
            pl.BlockSpec((nseq * steps, SWA_KV), lambda i: (base + i, col(COL_SWV))),
            pl.BlockSpec((nseq, WINDOW, SWA_KV), lambda i: (i, 0, 0)),
            pl.BlockSpec((nseq, WINDOW, SWA_KV), lambda i: (i, 0, 0)),
            pl.BlockSpec((steps, SWA_KV), lambda i: (0, 0)),
            pl.BlockSpec((steps, SWA_KV), lambda i: (0, 0)),
            pl.BlockSpec((SUBLANES, LANES), lambda i: (0, 0)),
        ],
        out_specs=(
            pl.BlockSpec((nseq * steps, SWA_Q), lambda i: (i, 0)),
            pl.BlockSpec((nseq, WINDOW, SWA_KV), lambda i: (i, 0, 0)),
            pl.BlockSpec((nseq, WINDOW, SWA_KV), lambda i: (i, 0, 0)),
        ),
        compiler_params=pltpu.CompilerParams(dimension_semantics=("arbitrary",), vmem_limit_bytes=VMEM_LIMIT),
        name="swa_sample",
    )(proj, proj, proj, kc, vc, cos, sin, prm)


def _tail_kernel(x_ref, odn_ref, y_ref, osw_ref, wout_ref, g1_ref, g2_ref, g3_ref, wfi_ref, wfo_ref, o_ref, *, d_ff):
    mixed = jnp.concatenate([odn_ref[...], y_ref[...], osw_ref[...]], axis=1).astype(BF16)
    m = jnp.dot(mixed, wout_ref[...], preferred_element_type=F32)
    x1 = x_ref[...] + _rmsnorm(m, g1_ref[...])
    h = _rmsnorm(x1, g2_ref[...]).astype(BF16)
    gu = jnp.dot(h, wfi_ref[...], preferred_element_type=F32)
    act = (_silu(gu[:, :d_ff]) * gu[:, d_ff:]).astype(BF16)
    y2 = jnp.dot(act, wfo_ref[...], preferred_element_type=F32)
    o_ref[...] = x1 + _rmsnorm(y2, g3_ref[...])


def _tail(x, odn, y, osw, wout, g1, g2, g3, wfi, wfo):
    n, d = x.shape
    d_ff = wfo.shape[0]
    tm = _pick_tile(n, 256)
    row = lambda w: pl.BlockSpec((tm, w), lambda i: (i, 0))
    return pl.pallas_call(
        functools.partial(_tail_kernel, d_ff=d_ff),
        out_shape=jax.ShapeDtypeStruct((n, d), F32),
        grid=(n // tm,),
        in_specs=[row(d), row(DN_V), row(SSM_INNER), row(SWA_Q), _const_spec(wout.shape),
                  _const_spec((1, d)), _const_spec((1, d)), _const_spec((1, d)),
                  _const_spec(wfi.shape), _const_spec(wfo.shape)],
        out_specs=row(d),
        compiler_params=pltpu.CompilerParams(dimension_semantics=("arbitrary",), vmem_limit_bytes=VMEM_LIMIT),
        name="outproj_ffn",
    )(x, odn, y, osw, wout, g1, g2, g3, wfi, wfo)


def _reorder_w_in(w):
    offs = [0]
    for wd in IN_WIDTHS:
        offs.append(offs[-1] + wd)
    seg = lambda i: w[:, offs[i]:offs[i + 1]]
    dn_qkv, dn_z, dn_b, dn_a, ssm_xbc, ssm_z, ssm_dt, sw_q, sw_k, sw_v = (seg(i) for i in range(len(IN_WIDTHS)))
    small = jnp.concatenate([dn_b, dn_a, ssm_dt], axis=1)
    small = jnp.pad(small, ((0, 0), (0, LANES - small.shape[1])))
    out = jnp.concatenate([dn_qkv, dn_z, ssm_z, ssm_xbc, sw_q, sw_k, sw_v, small], axis=1)
    assert out.shape[1] == D_PROJ
    return out.astype(BF16)


def _lane_row(pairs):
    row = jnp.zeros((LANES,), F32)
    for off, vec in pairs:
        row = row.at[off:off + vec.shape[0]].set(vec.astype(F32))
    return row[None, :]


def _rope_tables(pos):
    half = SWA_HEAD_DIM // 2
    inv = ROPE_THETA ** (-jnp.arange(half, dtype=F32) / half)
    ang = pos.astype(F32)[:, None] * inv[None, :]
    cos = jnp.cos(ang)
    sin = jnp.sin(ang)
    cos_t = jnp.concatenate([cos, cos] * SWA_KV_HEADS, axis=1)
    sin_t = jnp.concatenate([-sin, sin] * SWA_KV_HEADS, axis=1)
    return cos_t, sin_t


def _pad_conv_state(buf):
    return jnp.pad(buf, ((0, 0), (SUBLANES - (CONV_WIDTH - 1), 0), (0, 0)))


def kernel(x_prompt, x_sample, state_dn, state_dn_conv, state_ssm, state_ssm_conv, cache_swa_k, cache_swa_v,
           meta_tokens, w_in, dn_conv_w, dn_a_log, dn_dt_bias, dn_norm_w, ssm_conv_w, ssm_conv_b, ssm_a_log,
           ssm_dt_bias, ssm_d, ssm_norm_w, swa_sinks, w_out, g_pre_mix, g_post_mix, g_pre_ffn, g_post_ffn,
           w_ffn_in, w_ffn_out):
    bp, seq, d = x_prompt.shape
    bs, ts, _ = x_sample.shape
    depth = w_in.shape[0]
    lp = N_META + seq + FRONT_PAD
    assert lp % BLOCK == 0 and BLOCK % CHUNK == 0
    nblk = lp // BLOCK

    zpad = jnp.zeros((FRONT_PAD, d), x_prompt.dtype)
    meta = meta_tokens.astype(x_prompt.dtype)
    xp = jnp.concatenate([jnp.concatenate([zpad, meta, x_prompt[b]], axis=0) for b in range(bp)], axis=0)
    xs = x_sample.reshape(bs * ts, d)
    pad_ranges = [(b * lp, b * lp + FRONT_PAD) for b in range(bp)]

    cos_p, sin_p = _rope_tables(jnp.arange(lp, dtype=jnp.int32) - FRONT_PAD)
    cos_s, sin_s = _rope_tables(PAST_LEN + jnp.arange(ts, dtype=jnp.int32))

    gw = (SSM_HEADS // SSM_GROUPS) * SSM_HEADDIM
    zero_dn = jnp.zeros((bp, DN_HEADS, DN_DK, DN_DV), F32)
    zero_dnc = jnp.zeros((bp, SUBLANES, DN_CONV), F32)
    zero_ssm = jnp.zeros((bp, SSM_GROUPS, gw, SSM_STATE), F32)
    zero_ssmc = jnp.zeros((bp, SUBLANES, SSM_CONV), F32)

    new_p, new_s = [], []
    for l in range(depth):
        w_in_l = _reorder_w_in(w_in[l])
        w_out_l = w_out[l].astype(BF16)
        w_fi_l = w_ffn_in[l].astype(BF16)
        w_fo_l = w_ffn_out[l].astype(BF16)
        prm = jnp.concatenate([
            _lane_row([(SM_A, dn_a_log[l])]),
            _lane_row([(SM_A, dn_dt_bias[l])]),
            _lane_row([(SM_DT, ssm_a_log[l])]),
            _lane_row([(SM_DT, ssm_dt_bias[l])]),
            _lane_row([(0, swa_sinks[l])]),
            jnp.zeros((SUBLANES - 5, LANES), F32)], axis=0)
        dn_nw = dn_norm_w[l][None, :]
        ssm_nw = ssm_norm_w[l][None, :]
        drow = jnp.repeat(ssm_d[l], SSM_HEADDIM)[None, :]
        cbias = ssm_conv_b[l][None, :]
        g1, g2, g3, g4 = (a[l][None, :] for a in (g_pre_mix, g_post_mix, g_pre_ffn, g_post_ffn))

        proj = _inproj(xp, g1, w_in_l, pad_ranges)
        odn, dn_p, dnc_p = _dn_mixer(proj, 0, bp, nblk, BLOCK, CHUNK, FRONT_PAD, zero_dn, zero_dnc,
                                     dn_conv_w[l], prm, dn_nw)
        ys, ssm_p, ssmc_p = _ssd_mixer(proj, 0, bp, nblk, BLOCK, CHUNK, FRONT_PAD, zero_ssm, zero_ssmc,
                                       ssm_conv_w[l], cbias, prm, drow, ssm_nw)
        osw, k_p, v_p = _swa_prompt(proj, bp, nblk, FRONT_PAD, cos_p, sin_p, prm)
        xp = _tail(xp, odn, ys, osw, w_out_l, g2, g3, g4, w_fi_l, w_fo_l)
        new_p.append((dn_p, dnc_p[:, -(CONV_WIDTH - 1):], ssm_p.reshape(bp, SSM_HEADS, SSM_HEADDIM, SSM_STATE),
                      ssmc_p[:, -(CONV_WIDTH - 1):], k_p.reshape(bp, WINDOW, SWA_KV_HEADS, SWA_HEAD_DIM),
                      v_p.reshape(bp, WINDOW, SWA_KV_HEADS, SWA_HEAD_DIM)))

        proj = _inproj(xs, g1, w_in_l, ())
        odn, dn_s, dnc_s = _dn_mixer(proj, 0, bs, 1, ts, ts, 0, state_dn[l], _pad_conv_state(state_dn_conv[l]),
                                     dn_conv_w[l], prm, dn_nw)
        ys, ssm_s, ssmc_s = _ssd_mixer(proj, 0, bs, 1, ts, ts, 0,
                                       state_ssm[l].reshape(bs, SSM_GROUPS, gw, SSM_STATE),
                                       _pad_conv_state(state_ssm_conv[l]), ssm_conv_w[l], cbias, prm, drow, ssm_nw)
        osw, k_s, v_s = _swa_sample(proj, 0, bs, ts, cache_swa_k[l].reshape(bs, WINDOW, SWA_KV),
                                    cache_swa_v[l].reshape(bs, WINDOW, SWA_KV), cos_s, sin_s, prm)
        xs = _tail(xs, odn, ys, osw, w_out_l, g2, g3, g4, w_fi_l, w_fo_l)
        new_s.append((dn_s, dnc_s[:, -(CONV_WIDTH - 1):], ssm_s.reshape(bs, SSM_HEADS, SSM_HEADDIM, SSM_STATE),
                      ssmc_s[:, -(CONV_WIDTH - 1):], k_s.reshape(bs, WINDOW, SWA_KV_HEADS, SWA_HEAD_DIM),
                      v_s.reshape(bs, WINDOW, SWA_KV_HEADS, SWA_HEAD_DIM)))

    outs_p = tuple(jnp.stack([st[i] for st in new_p]) for i in range(6))
    outs_s = tuple(jnp.stack([st[i] for st in new_s]) for i in range(6))
    y_prompt = xp.reshape(bp, lp, d)[:, FRONT_PAD + N_META:]
    return (y_prompt, xs.reshape(bs, ts, d)) + outs_p + outs_s
```

```python
import functools

import jax
import jax.numpy as jnp
from jax import lax
from jax.experimental import pallas as pl
from jax.experimental.pallas import tpu as pltpu

F32 = jnp.float32
BF16 = jnp.bfloat16
HI = lax.Precision.HIGHEST
NT = (((1,), (1,)), ((), ()))
TN = (((0,), (0,)), ((), ()))

N_META = 16
CONV_WIDTH = 4
CHUNK = 64
BLOCK = 128
WINDOW = 128
FRONT_PAD = BLOCK - N_META
ROPE_THETA = 10000.0
PAST_LEN = 8192
EPS = 1e-6

DN_HEADS, DN_DK, DN_DV = 4, 128, 128
DN_QK = DN_HEADS * DN_DK
DN_V = DN_HEADS * DN_DV
DN_CONV = 2 * DN_QK + DN_V
SSM_HEADS, SSM_HEADDIM, SSM_GROUPS, SSM_STATE = 4, 64, 2, 128
SSM_INNER = SSM_HEADS * SSM_HEADDIM
SSM_BC = SSM_GROUPS * SSM_STATE
SSM_CONV = SSM_INNER + 2 * SSM_BC
SWA_Q_HEADS, SWA_KV_HEADS, SWA_HEAD_DIM = 4, 2, 64
SWA_Q = SWA_Q_HEADS * SWA_HEAD_DIM
SWA_KV = SWA_KV_HEADS * SWA_HEAD_DIM
IN_WIDTHS = (DN_CONV, DN_V, DN_HEADS, DN_HEADS, SSM_CONV, SSM_INNER, SSM_HEADS, SWA_Q, SWA_KV, SWA_KV)

LANES = 128
SUBLANES = 8
COL_QKV = (0, DN_CONV)
COL_DNZ = (1536, DN_V)
COL_SSZ = (2048, SSM_INNER)
COL_XBC = (2304, SSM_CONV)
COL_SWQ = (3072, SWA_Q)
COL_SWK = (3328, SWA_KV)
COL_SWV = (3456, SWA_KV)
COL_SM = (3584, LANES)
D_PROJ = 3712
SM_B, SM_A, SM_DT = 0, 4, 8
PRM_DN_ALOG, PRM_DN_DTB, PRM_SSM_ALOG, PRM_SSM_DTB, PRM_SINK = 0, 1, 2, 3, 4
NEG_BIG = -1e30
VMEM_LIMIT = 56 * 1024 * 1024


def _bdot(a, b, dims=None):
    a = a.astype(BF16)
    b = b.astype(BF16)
    if dims is None:
        return jnp.dot(a, b, preferred_element_type=F32)
    return lax.dot_general(a, b, dims, preferred_element_type=F32)


def _hdot(a, b):
    return jnp.dot(a, b, precision=HI, preferred_element_type=F32)


def _rmsnorm(x, g):
    return x * lax.rsqrt(jnp.mean(x * x, axis=-1, keepdims=True) + EPS) * g


def _l2norm(x):
    return x * lax.rsqrt(jnp.sum(x * x, axis=-1, keepdims=True) + EPS)


def _sigmoid(x):
    return 1.0 / (1.0 + jnp.exp(-x))


def _silu(x):
    return x * _sigmoid(x)


def _softplus(x):
    return jnp.maximum(x, 0.0) + jnp.log1p(jnp.exp(-jnp.abs(x)))


def _pick_tile(n, target):
    best = None
    for t in range(SUBLANES, min(n, target) + 1, SUBLANES):
        if n % t == 0:
            best = t
    assert best is not None, n
    return best


def _const_spec(shape):
    nd = len(shape)
    return pl.BlockSpec(shape, lambda *_: (0,) * nd, pipeline_mode=pl.Buffered(1))


def _inproj_kernel(x_ref, g_ref, w_ref, o_ref, *, tm, pad_ranges):
    h = _rmsnorm(x_ref[...], g_ref[...])
    if pad_ranges:
        r = pl.program_id(0) * tm + lax.broadcasted_iota(jnp.int32, (tm, 1), 0)
        is_pad = None
        for lo, hi in pad_ranges:
            m = (r >= lo) & (r < hi)
            is_pad = m if is_pad is None else (is_pad | m)
        h = jnp.where(is_pad, 0.0, h)
    o_ref[...] = jnp.dot(h.astype(BF16), w_ref[...], preferred_element_type=F32)


def _inproj(x, g, w, pad_ranges):
    n, d = x.shape
    tm = _pick_tile(n, 256)
    return pl.pallas_call(
        functools.partial(_inproj_kernel, tm=tm, pad_ranges=tuple(pad_ranges)),
        out_shape=jax.ShapeDtypeStruct((n, D_PROJ), F32),
        grid=(n // tm,),
        in_specs=[pl.BlockSpec((tm, d), lambda i: (i, 0)), _const_spec((1, d)), _const_spec((d, D_PROJ))],
        out_specs=pl.BlockSpec((tm, D_PROJ), lambda i: (i, 0)),
        compiler_params=pltpu.CompilerParams(dimension_semantics=("arbitrary",), vmem_limit_bytes=VMEM_LIMIT),
        name="inproj",
    )(x, g, w)


def _causal_conv(xbuf, raw_ref, cw_ref, cbo_ref, rows):
    xbuf[SUBLANES:SUBLANES + rows, :] = raw_ref[...]
    base = SUBLANES - (CONV_WIDTH - 1)
    cw = cw_ref[...]
    acc = xbuf[base:base + rows, :] * cw[0:1, :]
    for i in range(1, CONV_WIDTH):
        acc = acc + xbuf[base + i:base + i + rows, :] * cw[i:i + 1, :]
    tail = xbuf[rows:rows + SUBLANES, :]
    cbo_ref[...] = tail
    xbuf[0:SUBLANES, :] = tail
    return acc


def _chunk_masks(c):
    ii = lax.broadcasted_iota(jnp.int32, (c, c), 0)
    jj = lax.broadcasted_iota(jnp.int32, (c, c), 1)
    return ii >= jj, ii > jj, (ii == jj).astype(F32)


def _segment_decay(gc, gct, lane, ge):
    col = gc[:, lane:lane + 1]
    row = gct[lane:lane + 1, :]
    return jnp.where(ge, jnp.exp(jnp.where(ge, col - row, 0.0)), 0.0)


def _inv_unit_lower_minus_eye(a_list, c, nh):
    w = nh * c
    blk_r = lax.broadcasted_iota(jnp.int32, (w, w), 0) // c
    blk_c = lax.broadcasted_iota(jnp.int32, (w, w), 1) // c
    same = blk_r == blk_c

    def block_diag(p):
        return jnp.where(same, jnp.concatenate([p] * nh, axis=0), 0.0).astype(BF16)

    ys = [-a for a in a_list]
    ps = [_bdot(a, block_diag(a)) for a in a_list]
    n = 2
    while n < c:
        pbds = [block_diag(p) for p in ps]
        n *= 2
        if n < c:
            sts = [_bdot(jnp.concatenate([y, p], axis=0), pbd) for y, p, pbd in zip(ys, ps, pbds)]
            ys = [y + p + st[:c] for y, p, st in zip(ys, ps, sts)]
            ps = [st[c:] for st in sts]
        else:
            ys = [y + p + _bdot(y, pbd) for y, p, pbd in zip(ys, ps, pbds)]
    return ys


def _dn_prep(items, lmat, ge, gt_all, chunk):
    heads = range(DN_HEADS)
    gcs = [_hdot(lmat, g_all) for _, _, g_all in items]
    gcts = [gc.T for gc in gcs]
    qs = [[_l2norm(qkv[:, h * DN_DK:(h + 1) * DN_DK]) * DN_DK ** -0.5 for h in heads] for qkv, _, _ in items]
    ks = [[_l2norm(qkv[:, DN_QK + h * DN_DK:DN_QK + (h + 1) * DN_DK]) for h in heads] for qkv, _, _ in items]
    vs = [[qkv[:, 2 * DN_QK + h * DN_DV:2 * DN_QK + (h + 1) * DN_DV] for h in heads] for qkv, _, _ in items]
    betas = [[beta_all[:, SM_B + h:SM_B + h + 1] for h in heads] for _, beta_all, _ in items]
    kbs = [[k.astype(BF16) for k in kk] for kk in ks]
    kks = [jnp.concatenate([_bdot(kb, kb, NT) for kb in kb4], axis=1) for kb4 in kbs]
    qks = [jnp.concatenate([_bdot(q, kb, NT) for q, kb in zip(q4, kb4)], axis=1) for q4, kb4 in zip(qs, kbs)]
    decs = [jnp.concatenate([_segment_decay(gc, gct, SM_A + h, ge) for h in heads], axis=1)
            for gc, gct in zip(gcs, gcts)]
    beta_ws = [jnp.concatenate([jnp.broadcast_to(b, (chunk, chunk)) for b in b4], axis=1) for b4 in betas]
    a_list = [jnp.where(gt_all, bw * kk * dec, 0.0) for bw, kk, dec in zip(beta_ws, kks, decs)]
    tms = _inv_unit_lower_minus_eye(a_list, chunk, DN_HEADS)
    attns = [(qk * dec).astype(BF16) for qk, dec in zip(qks, decs)]
    egs = [jnp.exp(gc) for gc in gcs]
    rhss = [[jnp.concatenate([vs[i][h] * betas[i][h], ks[i][h] * (betas[i][h] * egs[i][:, SM_A + h:SM_A + h + 1])],
                             axis=1) for h in heads] for i in range(len(items))]
    uws = [[rhss[i][h] + _bdot(tms[i][:, h * chunk:(h + 1) * chunk], rhss[i][h]) for h in heads]
           for i in range(len(items))]
    out = []
    for i, gc in enumerate(gcs):
        glast = gc[chunk - 1:chunk, :]
        ekd = jnp.exp(glast - gc)
        egl = jnp.exp(glast)
        per_head = []
        for h in heads:
            uw = uws[i][h]
            wq = jnp.concatenate([uw[:, DN_DV:], qs[i][h] * egs[i][:, SM_A + h:SM_A + h + 1]], axis=0).astype(BF16)
            kd = (ks[i][h] * ekd[:, SM_A + h:SM_A + h + 1]).astype(BF16)
            per_head.append((uw[:, :DN_DV], wq, attns[i][:, h * chunk:(h + 1) * chunk], kd,
                             egl[:, SM_A + h:SM_A + h + 1]))
        out.append(per_head)
    return out


def _dn_kernel(qkv_ref, z_ref, sm_ref, s0_ref, cb_ref, cw_ref, prm_ref, nw_ref,
               o_ref, so_ref, cbo_ref, xbuf, *, nseq, rows, chunk, front_pad):
    blk = pl.program_id(1)
    first = blk == 0

    @pl.when(first)
    def _():
        so_ref[...] = s0_ref[...]
        xbuf[:, 0:SUBLANES, :] = cb_ref[...]

    prm = prm_ref[...]
    nw = nw_ref[...]
    ge, _, _ = _chunk_masks(chunk)
    lmat = ge.astype(F32)
    wide = (chunk, DN_HEADS * chunk)
    gt_all = lax.broadcasted_iota(jnp.int32, wide, 0) > lax.broadcasted_iota(jnp.int32, wide, 1) % chunk
    nchunk = rows // chunk

    items = []
    for s in range(nseq):
        qkv = _silu(_causal_conv(xbuf.at[s], qkv_ref.at[s], cw_ref, cbo_ref.at[s], rows))
        sm = sm_ref[s]
        beta_all = _sigmoid(sm)
        g_all = -jnp.exp(prm[PRM_DN_ALOG:PRM_DN_ALOG + 1, :]) * _softplus(sm + prm[PRM_DN_DTB:PRM_DN_DTB + 1, :])
        if front_pad:
            pos = blk * rows + lax.broadcasted_iota(jnp.int32, (rows, 1), 0)
            g_all = jnp.where(pos < front_pad, 0.0, g_all)
        for c in range(nchunk):
            cs = slice(c * chunk, (c + 1) * chunk)
            items.append((qkv[cs], beta_all[cs], g_all[cs]))
    prep = _dn_prep(items, lmat, ge, gt_all, chunk)

    chains = [(s, h) for s in range(nseq) for h in range(DN_HEADS)]
    states = [so_ref[s, h] for s, h in chains]
    for c in range(nchunk):
        r0 = c * chunk
        fac = [prep[s * nchunk + c][h] for s, h in chains]
        m1s = [_bdot(f[1], st) for f, st in zip(fac, states)]
        v_news = [(f[0] - m1[:chunk]).astype(BF16) for f, m1 in zip(fac, m1s)]
        ups = [_bdot(f[3], v, TN) for f, v in zip(fac, v_news)]
        os_ = [m1[chunk:] + _bdot(f[2], v) for f, m1, v in zip(fac, m1s, v_news)]
        states = [st * f[4] + up for f, st, up in zip(fac, states, ups)]
        for (s, h), o in zip(chains, os_):
            lo = h * DN_DV
            zh = z_ref[s, r0:r0 + chunk, lo:lo + DN_DV]
            o_ref[s, r0:r0 + chunk, lo:lo + DN_DV] = _rmsnorm(o, nw) * _silu(zh)
    for (s, h), st in zip(chains, states):
        so_ref[s, h] = st


def _dn_mixer(proj, nblk, rows, chunk, front_pad, nseq, s0, layer, cbuf, cw, prm, nw):
    nb = proj.shape[0]
    col = lambda c: c[0] // c[1]
    return pl.pallas_call(
        functools.partial(_dn_kernel, nseq=nseq, rows=rows, chunk=chunk, front_pad=front_pad),
        out_shape=(jax.ShapeDtypeStruct((nb, nblk * rows, DN_V), F32),
                   jax.ShapeDtypeStruct((nb, DN_HEADS, DN_DK, DN_DV), F32),
                   jax.ShapeDtypeStruct((nb, SUBLANES, DN_CONV), F32)),
        grid=(nb // nseq, nblk),
        in_specs=[
            pl.BlockSpec((nseq, rows, DN_CONV), lambda b, i: (b, i, col(COL_QKV))),
            pl.BlockSpec((nseq, rows, DN_V), lambda b, i: (b, i, col(COL_DNZ))),
            pl.BlockSpec((nseq, rows, LANES), lambda b, i: (b, i, col(COL_SM))),
            pl.BlockSpec((None, nseq, DN_HEADS, DN_DK, DN_DV), lambda b, i: (layer, b, 0, 0, 0)),
            pl.BlockSpec((nseq, SUBLANES, DN_CONV), lambda b, i: (b, 0, 0)),
            pl.BlockSpec((CONV_WIDTH, DN_CONV), lambda b, i: (0, 0)),
            pl.BlockSpec((SUBLANES, LANES), lambda b, i: (0, 0)),
            pl.BlockSpec((1, DN_DV), lambda b, i: (0, 0)),
        ],
        out_specs=(
            pl.BlockSpec((nseq, rows, DN_V), lambda b, i: (b, i, 0)),
            pl.BlockSpec((nseq, DN_HEADS, DN_DK, DN_DV), lambda b, i: (b, 0, 0, 0)),
            pl.BlockSpec((nseq, SUBLANES, DN_CONV), lambda b, i: (b, 0, 0)),
        ),
        scratch_shapes=[pltpu.VMEM((nseq, rows + SUBLANES, DN_CONV), F32)],
        compiler_params=pltpu.CompilerParams(dimension_semantics=("arbitrary", "arbitrary"),
                                             vmem_limit_bytes=VMEM_LIMIT),
        name="dn_mixer",
    )(proj, proj, proj, s0, cbuf, cw, prm, nw)


def _ssd_kernel(xbc_ref, z_ref, sm_ref, h0_ref, cb_ref, cw_ref, cbias_ref, prm_ref, drow_ref, nw_ref,
                y_ref, ho_ref, cbo_ref, xbuf, *, nseq, rows, chunk, front_pad):
    blk = pl.program_id(1)

    @pl.when(blk == 0)
    def _():
        ho_ref[...] = h0_ref[...]
        xbuf[:, 0:SUBLANES, :] = cb_ref[...]

    prm = prm_ref[...]
    nw = nw_ref[...]
    drow = drow_ref[...]
    cbias = cbias_ref[...]
    ge, _, _ = _chunk_masks(chunk)
    lmat = ge.astype(F32)
    hpg = SSM_HEADS // SSM_GROUPS
    gw = hpg * SSM_HEADDIM
    lane = lax.broadcasted_iota(jnp.int32, (1, gw), 1)
    srow = lax.broadcasted_iota(jnp.int32, (gw, 1), 0)
    in_head = [(lane >= j * SSM_HEADDIM) & (lane < (j + 1) * SSM_HEADDIM) for j in range(hpg)]
    nchunk = rows // chunk
    groups = range(SSM_GROUPS)

    items = []
    for s in range(nseq):
        act = _silu(_causal_conv(xbuf.at[s], xbc_ref.at[s], cw_ref, cbo_ref.at[s], rows) + cbias)
        dt_all = _softplus(sm_ref[s] + prm[PRM_SSM_DTB:PRM_SSM_DTB + 1, :])
        if front_pad:
            pos = blk * rows + lax.broadcasted_iota(jnp.int32, (rows, 1), 0)
            dt_all = jnp.where(pos < front_pad, 0.0, dt_all)
        g_all = dt_all * (-jnp.exp(prm[PRM_SSM_ALOG:PRM_SSM_ALOG + 1, :]))
        for c in range(nchunk):
            cs = slice(c * chunk, (c + 1) * chunk)
            items.append((act[cs], dt_all[cs], g_all[cs]))
    n_items = len(items)
    gcs = [_hdot(lmat, g) for _, _, g in items]
    gcts = [gc.T for gc in gcs]
    xgs = [[a[:, g * gw:(g + 1) * gw] for g in groups] for a, _, _ in items]
    bgs = [[a[:, SSM_INNER + g * SSM_STATE:SSM_INNER + (g + 1) * SSM_STATE] for g in groups] for a, _, _ in items]
    cgs = [[a[:, SSM_INNER + SSM_BC + g * SSM_STATE:SSM_INNER + SSM_BC + (g + 1) * SSM_STATE] for g in groups]
           for a, _, _ in items]
    cbs = [[_bdot(cgs[i][g], bgs[i][g], NT) for g in groups] for i in range(n_items)]
    egs = [jnp.exp(gc) for gc in gcs]
    ekds = [jnp.exp(gc[chunk - 1:chunk, :] - gc) for gc in gcs]
    egls = [jnp.exp(gc[chunk - 1:chunk, :]) for gc in gcs]
    heads = [(g, j) for g in groups for j in range(hpg)]
    ln = lambda g, j: SM_DT + g * hpg + j
    xdts = [[jnp.where(in_head[j], xgs[i][g] * items[i][1][:, ln(g, j):ln(g, j) + 1], 0.0).astype(BF16)
             for g, j in heads] for i in range(n_items)]
    attns = [[cbs[i][g] * _segment_decay(gcs[i], gcts[i], ln(g, j), ge) for g, j in heads] for i in range(n_items)]
    y_intras = [[_bdot(attns[i][k], xdts[i][k]) for k in range(len(heads))] for i in range(n_items)]
    upds = [[_bdot(xdts[i][k], bgs[i][g] * ekds[i][:, ln(g, j):ln(g, j) + 1], TN) for k, (g, j) in enumerate(heads)]
            for i in range(n_items)]
    cds = [[jnp.concatenate([cgs[i][g] * egs[i][:, ln(g, j):ln(g, j) + 1] for j in range(hpg)], axis=0).astype(BF16)
            for g in groups] for i in range(n_items)]
    y_loc = [[sum(y_intras[i][g * hpg + j] for j in range(hpg)) for g in groups] for i in range(n_items)]
    h_inc = [[sum(upds[i][g * hpg + j] for j in range(hpg)) for g in groups] for i in range(n_items)]
    gl_cols = []
    for i in range(n_items):
        per_group = []
        for g in groups:
            gl = egls[i][:, ln(g, 0):ln(g, 0) + 1]
            for j in range(1, hpg):
                gl = jnp.where(srow < j * SSM_HEADDIM, gl, egls[i][:, ln(g, j):ln(g, j) + 1])
            per_group.append(gl)
        gl_cols.append(per_group)

    chains = [(s, g) for s in range(nseq) for g in groups]
    states = [ho_ref[s, g] for s, g in chains]
    for c in range(nchunk):
        r0 = c * chunk
        idx = [s * nchunk + c for s, _ in chains]
        yis = [_bdot(cds[i][g], st, NT) for i, (_, g), st in zip(idx, chains, states)]
        states = [st * gl_cols[i][g] + h_inc[i][g] for i, (_, g), st in zip(idx, chains, states)]
        for i, (s, g), yi in zip(idx, chains, yis):
            y_inter = yi[0:chunk]
            for j in range(1, hpg):
                y_inter = jnp.where(in_head[j], yi[j * chunk:(j + 1) * chunk], y_inter)
            yg = y_loc[i][g] + y_inter + xgs[i][g] * drow[:, g * gw:(g + 1) * gw]
            yg = yg * _silu(z_ref[s, r0:r0 + chunk, g * gw:(g + 1) * gw])
            y_ref[s, r0:r0 + chunk, g * gw:(g + 1) * gw] = _rmsnorm(yg, nw[:, g * gw:(g + 1) * gw])
    for (s, g), st in zip(chains, states):
        ho_ref[s, g] = st


def _ssd_mixer(proj, nblk, rows, chunk, front_pad, nseq, h0, layer, cbuf, cw, cbias, prm, drow, nw):
    nb = proj.shape[0]
    col = lambda c: c[0] // c[1]
    gw = (SSM_HEADS // SSM_GROUPS) * SSM_HEADDIM
    return pl.pallas_call(
        functools.partial(_ssd_kernel, nseq=nseq, rows=rows, chunk=chunk, front_pad=front_pad),
        out_shape=(jax.ShapeDtypeStruct((nb, nblk * rows, SSM_INNER), F32),
                   jax.ShapeDtypeStruct((nb, SSM_GROUPS, gw, SSM_STATE), F32),
                   jax.ShapeDtypeStruct((nb, SUBLANES, SSM_CONV), F32)),
        grid=(nb // nseq, nblk),
        in_specs=[
            pl.BlockSpec((nseq, rows, SSM_CONV), lambda b, i: (b, i, col(COL_XBC))),
            pl.BlockSpec((nseq, rows, SSM_INNER), lambda b, i: (b, i, col(COL_SSZ))),
            pl.BlockSpec((nseq, rows, LANES), lambda b, i: (b, i, col(COL_SM))),
            pl.BlockSpec((None, nseq, SSM_GROUPS, gw, SSM_STATE), lambda b, i: (layer, b, 0, 0, 0)),
            pl.BlockSpec((nseq, SUBLANES, SSM_CONV), lambda b, i: (b, 0, 0)),
            pl.BlockSpec((CONV_WIDTH, SSM_CONV), lambda b, i: (0, 0)),
            pl.BlockSpec((1, SSM_CONV), lambda b, i: (0, 0)),
            pl.BlockSpec((SUBLANES, LANES), lambda b, i: (0, 0)),
            pl.BlockSpec((1, SSM_INNER), lambda b, i: (0, 0)),
            pl.BlockSpec((1, SSM_INNER), lambda b, i: (0, 0)),
        ],
        out_specs=(
            pl.BlockSpec((nseq, rows, SSM_INNER), lambda b, i: (b, i, 0)),
            pl.BlockSpec((nseq, SSM_GROUPS, gw, SSM_STATE), lambda b, i: (b, 0, 0, 0)),
            pl.BlockSpec((nseq, SUBLANES, SSM_CONV), lambda b, i: (b, 0, 0)),
        ),
        scratch_shapes=[pltpu.VMEM((nseq, rows + SUBLANES, SSM_CONV), F32)],
        compiler_params=pltpu.CompilerParams(dimension_semantics=("arbitrary", "arbitrary"),
                                             vmem_limit_bytes=VMEM_LIMIT),
        name="ssd_mixer",
    )(proj, proj, proj, h0, cbuf, cw, cbias, prm, drow, nw)


def _rope(x, cos, sin_signed):
    w = x.shape[-1]
    half = SWA_HEAD_DIM // 2
    lane = lax.broadcasted_iota(jnp.int32, (1, w), 1)
    first_half = (lane % SWA_HEAD_DIM) < half
    swapped = jnp.where(first_half, pltpu.roll(x, w - half, axis=1), pltpu.roll(x, half, axis=1))
    return x * cos + swapped * sin_signed


def _sink_attend(problems):
    scale = SWA_HEAD_DIM ** -0.5
    scores = [[jnp.where(m, _bdot(q, k, NT) * scale, NEG_BIG) for k, m in zip(keys, masks)]
              for q, keys, _, masks, _ in problems]
    outs = []
    probs, dens = [], []
    for (_, _, _, _, sink), ss in zip(problems, scores):
        mx = sink
        for s in ss:
            mx = jnp.maximum(mx, jnp.max(s, axis=-1, keepdims=True))
        ps = [jnp.exp(s - mx) for s in ss]
        den = jnp.exp(sink - mx)
        for p in ps:
            den = den + jnp.sum(p, axis=-1, keepdims=True)
        probs.append(ps)
        dens.append(den)
    pvs = [[_bdot(p, v) for p, v in zip(ps, vals)] for (_, _, vals, _, _), ps in zip(problems, probs)]
    for pv, den in zip(pvs, dens):
        acc = pv[0]
        for extra in pv[1:]:
            acc = acc + extra
        outs.append(acc / den)
    return outs


def _swa_problems(q, key_sets, val_sets, masks, prm, tq):
    grp = SWA_Q_HEADS // SWA_KV_HEADS
    row = lax.broadcasted_iota(jnp.int32, (grp * tq, 1), 0)
    problems = []
    for j in range(SWA_KV_HEADS):
        ks = slice(j * SWA_HEAD_DIM, (j + 1) * SWA_HEAD_DIM)
        heads = [j * grp + g for g in range(grp)]
        qst = jnp.concatenate([q[:, h * SWA_HEAD_DIM:(h + 1) * SWA_HEAD_DIM] for h in heads], axis=0)
        sink = prm[PRM_SINK:PRM_SINK + 1, heads[0]:heads[0] + 1]
        for g in range(1, grp):
            sink = jnp.where(row < g * tq, sink, prm[PRM_SINK:PRM_SINK + 1, heads[g]:heads[g] + 1])
        problems.append((qst, [k[:, ks] for k in key_sets], [v[:, ks] for v in val_sets], masks, sink))
    return problems


def _swa_store(o_ref_at, outs, tq):
    grp = SWA_Q_HEADS // SWA_KV_HEADS
    for j, o in enumerate(outs):
        for g in range(grp):
            h = j * grp + g
            o_ref_at(slice(h * SWA_HEAD_DIM, (h + 1) * SWA_HEAD_DIM), o[g * tq:(g + 1) * tq])


def _swa_prompt_kernel(q_ref, k_ref, v_ref, cos_ref, sin_ref, prm_ref, o_ref, ko_ref, vo_ref, kprev, vprev,
                       *, nseq, front_pad):
    blk = pl.program_id(0)

    @pl.when(blk == 0)
    def _():
        kprev[...] = jnp.zeros_like(kprev)
        vprev[...] = jnp.zeros_like(vprev)

    cos = cos_ref[...]
    sin = sin_ref[...]
    cos_q = jnp.concatenate([cos, cos], axis=1)
    sin_q = jnp.concatenate([sin, sin], axis=1)
    grp = SWA_Q_HEADS // SWA_KV_HEADS
    qi = lax.broadcasted_iota(jnp.int32, (grp * BLOCK, BLOCK), 0) % BLOCK
    kj = lax.broadcasted_iota(jnp.int32, (grp * BLOCK, BLOCK), 1)
    mask_cur = (kj <= qi) & (blk * BLOCK + kj >= front_pad)
    mask_prev = (kj > qi) & ((blk - 1) * BLOCK + kj >= front_pad)
    prm = prm_ref[...]
    problems = []
    for s in range(nseq):
        q = _rope(q_ref[s], cos_q, sin_q)
        k = _rope(k_ref[s], cos, sin)
        v = v_ref[s]
        problems += _swa_problems(q, (kprev[s], k), (vprev[s], v), (mask_prev, mask_cur), prm, BLOCK)
        kprev[s] = k
        vprev[s] = v
        ko_ref[s] = k
        vo_ref[s] = v
    outs = _sink_attend(problems)
    for s in range(nseq):
        def put(cols, val, s=s):
            o_ref[s, :, cols] = val
        _swa_store(put, outs[s * SWA_KV_HEADS:(s + 1) * SWA_KV_HEADS], BLOCK)


def _swa_prompt(proj, nblk, front_pad, cos, sin, prm):
    nb = proj.shape[0]
    col = lambda c: c[0] // c[1]
    return pl.pallas_call(
        functools.partial(_swa_prompt_kernel, nseq=nb, front_pad=front_pad),
        out_shape=(jax.ShapeDtypeStruct((nb, nblk * BLOCK, SWA_Q), F32),
                   jax.ShapeDtypeStruct((nb, WINDOW, SWA_KV), F32),
                   jax.ShapeDtypeStruct((nb, WINDOW, SWA_KV), F32)),
        grid=(nblk,),
        in_specs=[
            pl.BlockSpec((nb, BLOCK, SWA_Q), lambda i: (0, i, col(COL_SWQ))),
            pl.BlockSpec((nb, BLOCK, SWA_KV), lambda i: (0, i, col(COL_SWK))),
            pl.BlockSpec((nb, BLOCK, SWA_KV), lambda i: (0, i, col(COL_SWV))),
            pl.BlockSpec((BLOCK, SWA_KV), lambda i: (i, 0)),
            pl.BlockSpec((BLOCK, SWA_KV), lambda i: (i, 0)),
            pl.BlockSpec((SUBLANES, LANES), lambda i: (0, 0)),
        ],
        out_specs=(
            pl.BlockSpec((nb, BLOCK, SWA_Q), lambda i: (0, i, 0)),
            pl.BlockSpec((nb, WINDOW, SWA_KV), lambda i: (0, 0, 0)),
            pl.BlockSpec((nb, WINDOW, SWA_KV), lambda i: (0, 0, 0)),
        ),
        scratch_shapes=[pltpu.VMEM((nb, BLOCK, SWA_KV), F32), pltpu.VMEM((nb, BLOCK, SWA_KV), F32)],
        compiler_params=pltpu.CompilerParams(dimension_semantics=("arbitrary",), vmem_limit_bytes=VMEM_LIMIT),
        name="swa_prompt",
    )(proj, proj, proj, cos, sin, prm)


def _swa_sample_kernel(q_ref, k_ref, v_ref, kc_ref, vc_ref, cos_ref, sin_ref, prm_ref, o_ref, ko_ref, vo_ref,
                       *, nseq, steps):
    cos = cos_ref[...]
    sin = sin_ref[...]
    cos_q = jnp.concatenate([cos, cos], axis=1)
    sin_q = jnp.concatenate([sin, sin], axis=1)
    prm = prm_ref[...]
    grp = SWA_Q_HEADS // SWA_KV_HEADS
    ti = lax.broadcasted_iota(jnp.int32, (grp * steps, WINDOW), 0) % steps
    sj = lax.broadcasted_iota(jnp.int32, (grp * steps, WINDOW), 1)
    mask_cache = sj > ti
    tn = lax.broadcasted_iota(jnp.int32, (grp * steps, steps), 0) % steps
    sn = lax.broadcasted_iota(jnp.int32, (grp * steps, steps), 1)
    mask_new = sn <= tn
    problems = []
    for b in range(nseq):
        q = _rope(q_ref[b], cos_q, sin_q)
        k = _rope(k_ref[b], cos, sin)
        v = v_ref[b]
        kc = kc_ref[b]
        vc = vc_ref[b]
        ko_ref[b, 0:WINDOW - steps, :] = kc[steps:WINDOW, :]
        ko_ref[b, WINDOW - steps:WINDOW, :] = k
        vo_ref[b, 0:WINDOW - steps, :] = vc[steps:WINDOW, :]
        vo_ref[b, WINDOW - steps:WINDOW, :] = v
        problems += _swa_problems(q, (kc, k), (vc, v), (mask_cache, mask_new), prm, steps)
    outs = _sink_attend(problems)
    for b in range(nseq):
        def put(cols, val, b=b):
            o_ref[b, :, cols] = val
        _swa_store(put, outs[b * SWA_KV_HEADS:(b + 1) * SWA_KV_HEADS], steps)


def _swa_sample(proj, nseq, kc, vc, layer, cos, sin, prm):
    nb, steps, _ = proj.shape
    assert WINDOW > steps
    col = lambda c: c[0] // c[1]
    return pl.pallas_call(
        functools.partial(_swa_sample_kernel, nseq=nseq, steps=steps),
        out_shape=(jax.ShapeDtypeStruct((nb, steps, SWA_Q), F32),
                   jax.ShapeDtypeStruct((nb, WINDOW, SWA_KV), F32),
                   jax.ShapeDtypeStruct((nb, WINDOW, SWA_KV), F32)),
        grid=(nb // nseq,),
        in_specs=[
            pl.BlockSpec((nseq, steps, SWA_Q), lambda i: (i, 0, col(COL_SWQ))),
            pl.BlockSpec((nseq, steps, SWA_KV), lambda i: (i, 0, col(COL_SWK))),
            pl.BlockSpec((nseq, steps, SWA_KV), lambda i: (i, 0, col(COL_SWV))),
            pl.BlockSpec((None, nseq, WINDOW, SWA_KV), lambda i: (layer, i, 0, 0)),
            pl.BlockSpec((None, nseq, WINDOW, SWA_KV), lambda i: (layer, i, 0, 0)),
            pl.BlockSpec((steps, SWA_KV), lambda i: (0, 0)),
            pl.BlockSpec((steps, SWA_KV), lambda i: (0, 0)),
            pl.BlockSpec((SUBLANES, LANES), lambda i: (0, 0)),
        ],
        out_specs=(
            pl.BlockSpec((nseq, steps, SWA_Q), lambda i: (i, 0, 0)),
            pl.BlockSpec((nseq, WINDOW, SWA_KV), lambda i: (i, 0, 0)),
            pl.BlockSpec((nseq, WINDOW, SWA_KV), lambda i: (i, 0, 0)),
        ),
        compiler_params=pltpu.CompilerParams(dimension_semantics=("arbitrary",), vmem_limit_bytes=VMEM_LIMIT),
        name="swa_sample",
    )(proj, proj, proj, kc, vc, cos, sin, prm)


def _tail_kernel(x_ref, odn_ref, y_ref, osw_ref, wout_ref, g1_ref, g2_ref, g3_ref, wfi_ref, wfo_ref, o_ref, *, d_ff):
    mixed = jnp.concatenate([odn_ref[...], y_ref[...], osw_ref[...]], axis=1).astype(BF16)
    m = jnp.dot(mixed, wout_ref[...], preferred_element_type=F32)
    x1 = x_ref[...] + _rmsnorm(m, g1_ref[...])
    h = _rmsnorm(x1, g2_ref[...]).astype(BF16)
    gu = jnp.dot(h, wfi_ref[...], preferred_element_type=F32)
    act = (_silu(gu[:, :d_ff]) * gu[:, d_ff:]).astype(BF16)
    y2 = jnp.dot(act, wfo_ref[...], preferred_element_type=F32)
    o_ref[...] = x1 + _rmsnorm(y2, g3_ref[...])


def _tail(x, odn, y, osw, wout, g1, g2, g3, wfi, wfo):
    n, d = x.shape
    d_ff = wfo.shape[0]
    tm = _pick_tile(n, 256)
    row = lambda w: pl.BlockSpec((tm, w), lambda i: (i, 0))
    return pl.pallas_call(
        functools.partial(_tail_kernel, d_ff=d_ff),
        out_shape=jax.ShapeDtypeStruct((n, d), F32),
        grid=(n // tm,),
        in_specs=[row(d), row(DN_V), row(SSM_INNER), row(SWA_Q), _const_spec(wout.shape),
                  _const_spec((1, d)), _const_spec((1, d)), _const_spec((1, d)),
                  _const_spec(wfi.shape), _const_spec(wfo.shape)],
        out_specs=row(d),
        compiler_params=pltpu.CompilerParams(dimension_semantics=("arbitrary",), vmem_limit_bytes=VMEM_LIMIT),
        name="outproj_ffn",
    )(x, odn, y, osw, wout, g1, g2, g3, wfi, wfo)


def _reorder_w_in_kernel(w_ref, o_ref):
    w = w_ref[0]
    offs = [0]
    for wd in IN_WIDTHS:
        offs.append(offs[-1] + wd)
    seg = lambda i: w[:, offs[i]:offs[i + 1]]
    dn_qkv, dn_z, dn_b, dn_a, ssm_xbc, ssm_z, ssm_dt, sw_q, sw_k, sw_v = (seg(i) for i in range(len(IN_WIDTHS)))
    n_small = dn_b.shape[1] + dn_a.shape[1] + ssm_dt.shape[1]
    small = jnp.concatenate([dn_b, dn_a, ssm_dt, jnp.zeros((w.shape[0], LANES - n_small), F32)], axis=1)
    out = jnp.concatenate([dn_qkv, dn_z, ssm_z, ssm_xbc, sw_q, sw_k, sw_v, small], axis=1)
    o_ref[0] = out.astype(BF16)


def _reorder_w_in(w):
    depth, d, d_in = w.shape
    assert d_in == sum(IN_WIDTHS)
    tr = _pick_tile(d, 256)
    return pl.pallas_call(
        _reorder_w_in_kernel,
        out_shape=jax.ShapeDtypeStruct((depth, d, D_PROJ), BF16),
        grid=(depth, d // tr),
        in_specs=[pl.BlockSpec((1, tr, d_in), lambda l, i: (l, i, 0))],
        out_specs=pl.BlockSpec((1, tr, D_PROJ), lambda l, i: (l, i, 0)),
        compiler_params=pltpu.CompilerParams(dimension_semantics=("arbitrary", "arbitrary")),
        name="reorder_w_in",
    )(w)


def _lane_row(pairs):
    row = jnp.zeros((LANES,), F32)
    for off, vec in pairs:
        row = row.at[off:off + vec.shape[0]].set(vec.astype(F32))
    return row[None, :]


def _rope_tables(pos):
    half = SWA_HEAD_DIM // 2
    inv = ROPE_THETA ** (-jnp.arange(half, dtype=F32) / half)
    ang = pos.astype(F32)[:, None] * inv[None, :]
    cos = jnp.cos(ang)
    sin = jnp.sin(ang)
    cos_t = jnp.concatenate([cos, cos] * SWA_KV_HEADS, axis=1)
    sin_t = jnp.concatenate([-sin, sin] * SWA_KV_HEADS, axis=1)
    return cos_t, sin_t


def _pad_conv_state(buf):
    return jnp.pad(buf, ((0, 0), (SUBLANES - (CONV_WIDTH - 1), 0), (0, 0)))


def kernel(x_prompt, x_sample, state_dn, state_dn_conv, state_ssm, state_ssm_conv, cache_swa_k, cache_swa_v,
           meta_tokens, w_in, dn_conv_w, dn_a_log, dn_dt_bias, dn_norm_w, ssm_conv_w, ssm_conv_b, ssm_a_log,
           ssm_dt_bias, ssm_d, ssm_norm_w, swa_sinks, w_out, g_pre_mix, g_post_mix, g_pre_ffn, g_post_ffn,
           w_ffn_in, w_ffn_out):
    bp, seq, d = x_prompt.shape
    bs, ts, _ = x_sample.shape
    depth = w_in.shape[0]
    lp = N_META + seq + FRONT_PAD
    assert lp % BLOCK == 0 and BLOCK % CHUNK == 0
    nblk = lp // BLOCK

    zpad = jnp.zeros((FRONT_PAD, d), x_prompt.dtype)
    meta = meta_tokens.astype(x_prompt.dtype)
    xp = jnp.concatenate([jnp.concatenate([zpad, meta, x_prompt[b]], axis=0) for b in range(bp)], axis=0)
    xs = x_sample.reshape(bs * ts, d)
    pad_ranges = [(b * lp, b * lp + FRONT_PAD) for b in range(bp)]

    cos_p, sin_p = _rope_tables(jnp.arange(lp, dtype=jnp.int32) - FRONT_PAD)
    cos_s, sin_s = _rope_tables(PAST_LEN + jnp.arange(ts, dtype=jnp.int32))

    gw = (SSM_HEADS // SSM_GROUPS) * SSM_HEADDIM
    zero_dn = jnp.zeros((1, bp, DN_HEADS, DN_DK, DN_DV), F32)
    w_in_r = _reorder_w_in(w_in)
    nseq_s = _pick_tile(bs, 8) if bs % SUBLANES == 0 else bs
    zero_dnc = jnp.zeros((bp, SUBLANES, DN_CONV), F32)
    zero_ssm = jnp.zeros((1, bp, SSM_GROUPS, gw, SSM_STATE), F32)
    state_ssm_g = state_ssm.reshape(depth, bs, SSM_GROUPS, gw, SSM_STATE)
    zero_ssmc = jnp.zeros((bp, SUBLANES, SSM_CONV), F32)
    cache_k = cache_swa_k.reshape(depth, bs, WINDOW, SWA_KV)
    cache_v = cache_swa_v.reshape(depth, bs, WINDOW, SWA_KV)

    new_p, new_s = [], []
    for l in range(depth):
        w_in_l = w_in_r[l]
        w_out_l = w_out[l].astype(BF16)
        w_fi_l = w_ffn_in[l].astype(BF16)
        w_fo_l = w_ffn_out[l].astype(BF16)
        prm = jnp.concatenate([
            _lane_row([(SM_A, dn_a_log[l])]),
            _lane_row([(SM_A, dn_dt_bias[l])]),
            _lane_row([(SM_DT, ssm_a_log[l])]),
            _lane_row([(SM_DT, ssm_dt_bias[l])]),
            _lane_row([(0, swa_sinks[l])]),
            jnp.zeros((SUBLANES - 5, LANES), F32)], axis=0)
        dn_nw = dn_norm_w[l][None, :]
        ssm_nw = ssm_norm_w[l][None, :]
        drow = jnp.repeat(ssm_d[l], SSM_HEADDIM)[None, :]
        cbias = ssm_conv_b[l][None, :]
        g1, g2, g3, g4 = (a[l][None, :] for a in (g_pre_mix, g_post_mix, g_pre_ffn, g_post_ffn))

        proj = _inproj(xp, g1, w_in_l, pad_ranges)
        odn, dn_p, dnc_p = _dn_mixer(proj.reshape(bp, lp, D_PROJ), nblk, BLOCK, CHUNK, FRONT_PAD, bp, zero_dn, 0,
                                     zero_dnc, dn_conv_w[l], prm, dn_nw)
        odn = odn.reshape(bp * lp, DN_V)
        ys, ssm_p, ssmc_p = _ssd_mixer(proj.reshape(bp, lp, D_PROJ), nblk, BLOCK, CHUNK, FRONT_PAD, bp, zero_ssm, 0,
                                       zero_ssmc, ssm_conv_w[l], cbias, prm, drow, ssm_nw)
        ys = ys.reshape(bp * lp, SSM_INNER)
        osw, k_p, v_p = _swa_prompt(proj.reshape(bp, lp, D_PROJ), nblk, FRONT_PAD, cos_p, sin_p, prm)
        osw = osw.reshape(bp * lp, SWA_Q)
        xp = _tail(xp, odn, ys, osw, w_out_l, g2, g3, g4, w_fi_l, w_fo_l)
        new_p.append((dn_p, dnc_p[:, -(CONV_WIDTH - 1):], ssm_p.reshape(bp, SSM_HEADS, SSM_HEADDIM, SSM_STATE),
                      ssmc_p[:, -(CONV_WIDTH - 1):], k_p.reshape(bp, WINDOW, SWA_KV_HEADS, SWA_HEAD_DIM),
                      v_p.reshape(bp, WINDOW, SWA_KV_HEADS, SWA_HEAD_DIM)))

        proj = _inproj(xs, g1, w_in_l, ())
        odn, dn_s, dnc_s = _dn_mixer(proj.reshape(bs, ts, D_PROJ), 1, ts, ts, 0, nseq_s, state_dn, l,
                                     _pad_conv_state(state_dn_conv[l]), dn_conv_w[l], prm, dn_nw)
        odn = odn.reshape(bs * ts, DN_V)
        ys, ssm_s, ssmc_s = _ssd_mixer(proj.reshape(bs, ts, D_PROJ), 1, ts, ts, 0, nseq_s, state_ssm_g, l,
                                       _pad_conv_state(state_ssm_conv[l]), ssm_conv_w[l], cbias, prm, drow, ssm_nw)
        ys = ys.reshape(bs * ts, SSM_INNER)
        osw, k_s, v_s = _swa_sample(proj.reshape(bs, ts, D_PROJ), nseq_s, cache_k, cache_v, l, cos_s, sin_s, prm)
        osw = osw.reshape(bs * ts, SWA_Q)
        xs = _tail(xs, odn, ys, osw, w_out_l, g2, g3, g4, w_fi_l, w_fo_l)
        new_s.append((dn_s, dnc_s[:, -(CONV_WIDTH - 1):], ssm_s.reshape(bs, SSM_HEADS, SSM_HEADDIM, SSM_STATE),
                      ssmc_s[:, -(CONV_WIDTH - 1):], k_s.reshape(bs, WINDOW, SWA_KV_HEADS, SWA_HEAD_DIM),
                      v_s.reshape(bs, WINDOW, SWA_KV_HEADS, SWA_HEAD_DIM)))

    outs_p = tuple(jnp.stack([st[i] for st in new_p]) for i in range(6))
    outs_s = tuple(jnp.stack([st[i] for st in new_s]) for i in range(6))
    y_prompt = xp.reshape(bp, lp, d)[:, FRONT_PAD + N_META:]
    return (y_prompt, xs.reshape(bs, ts, d)) + outs_p + outs_s
```

```python
import functools

import jax
import jax.numpy as jnp
from jax import lax
from jax.experimental import pallas as pl
from jax.experimental.pallas import tpu as pltpu

F32 = jnp.float32
BF16 = jnp.bfloat16
HI = lax.Precision.HIGHEST
NT = (((1,), (1,)), ((), ()))
TN = (((0,), (0,)), ((), ()))

N_META = 16
CONV_WIDTH = 4
CHUNK = 64
BLOCK = 128
WINDOW = 128
FRONT_PAD = BLOCK - N_META
ROPE_THETA = 10000.0
PAST_LEN = 8192
EPS = 1e-6

DN_HEADS, DN_DK, DN_DV = 4, 128, 128
DN_QK = DN_HEADS * DN_DK
DN_V = DN_HEADS * DN_DV
DN_CONV = 2 * DN_QK + DN_V
SSM_HEADS, SSM_HEADDIM, SSM_GROUPS, SSM_STATE = 4, 64, 2, 128
SSM_INNER = SSM_HEADS * SSM_HEADDIM
SSM_BC = SSM_GROUPS * SSM_STATE
SSM_CONV = SSM_INNER + 2 * SSM_BC
SWA_Q_HEADS, SWA_KV_HEADS, SWA_HEAD_DIM = 4, 2, 64
SWA_Q = SWA_Q_HEADS * SWA_HEAD_DIM
SWA_KV = SWA_KV_HEADS * SWA_HEAD_DIM
IN_WIDTHS = (DN_CONV, DN_V, DN_HEADS, DN_HEADS, SSM_CONV, SSM_INNER, SSM_HEADS, SWA_Q, SWA_KV, SWA_KV)

LANES = 128
SUBLANES = 8
COL_QKV = (0, DN_CONV)
COL_DNZ = (1536, DN_V)
COL_SSZ = (2048, SSM_INNER)
COL_XBC = (2304, SSM_CONV)
COL_SWQ = (3072, SWA_Q)
COL_SWK = (3328, SWA_KV)
COL_SWV = (3456, SWA_KV)
COL_SM = (3584, LANES)
D_PROJ = 3712
SM_B, SM_A, SM_DT = 0, 4, 8
PRM_DN_ALOG, PRM_DN_DTB, PRM_SSM_ALOG, PRM_SSM_DTB, PRM_SINK = 0, 1, 2, 3, 4
NEG_BIG = -1e30
VMEM_LIMIT = 56 * 1024 * 1024
TM_DENSE = 640


def _bdot(a, b, dims=None):
    a = a.astype(BF16)
    b = b.astype(BF16)
    if dims is None:
        return jnp.dot(a, b, preferred_element_type=F32)
    return lax.dot_general(a, b, dims, preferred_element_type=F32)


def _hdot(a, b):
    return jnp.dot(a, b, precision=HI, preferred_element_type=F32)


def _rmsnorm(x, g):
    return x * lax.rsqrt(jnp.mean(x * x, axis=-1, keepdims=True) + EPS) * g


def _l2norm(x):
    return x * lax.rsqrt(jnp.sum(x * x, axis=-1, keepdims=True) + EPS)


def _sigmoid(x):
    return 1.0 / (1.0 + jnp.exp(-x))


def _silu(x):
    return x * _sigmoid(x)


def _softplus(x):
    return jnp.maximum(x, 0.0) + jnp.log1p(jnp.exp(-jnp.abs(x)))


def _pick_tile(n, target):
    best = None
    for t in range(SUBLANES, min(n, target) + 1, SUBLANES):
        if n % t == 0:
            best = t
    assert best is not None, n
    return best


def _inproj_kernel(x_ref, g_ref, w_ref, o_ref, *, tm, pad_range):
    h = _rmsnorm(x_ref[...], g_ref[...])
    if pad_range is not None:
        r = pl.program_id(1) * tm + lax.broadcasted_iota(jnp.int32, (tm, 1), 0)
        is_pad = (r >= pad_range[0]) & (r < pad_range[1])
        h = jnp.where(is_pad, 0.0, h)
    o_ref[...] = jnp.dot(h.astype(BF16), w_ref[...], preferred_element_type=F32)


def _inproj(x, g, w, layer, pad_range, tm_target):
    nb, rows, d = x.shape
    tm = _pick_tile(rows, tm_target)
    return pl.pallas_call(
        functools.partial(_inproj_kernel, tm=tm, pad_range=pad_range),
        out_shape=jax.ShapeDtypeStruct((nb, rows, D_PROJ), F32),
        grid=(nb, rows // tm),
        in_specs=[pl.BlockSpec((None, tm, d), lambda b, i: (b, i, 0)), _layer_spec(g.shape, layer),
                  _layer_spec(w.shape, layer)],
        out_specs=pl.BlockSpec((None, tm, D_PROJ), lambda b, i: (b, i, 0)),
        compiler_params=pltpu.CompilerParams(dimension_semantics=("arbitrary", "arbitrary"),
                                             vmem_limit_bytes=VMEM_LIMIT),
        name="inproj",
    )(x, g, w)


def _causal_conv(xbuf, raw_ref, cw_ref, cbo_ref, rows):
    x = raw_ref[...]
    prev = xbuf[0:SUBLANES, :]
    cw = cw_ref[...]
    row = lax.broadcasted_iota(jnp.int32, (SUBLANES, 1), 0)
    acc = None
    for i in range(CONV_WIDTH):
        s = CONV_WIDTH - 1 - i
        if s == 0:
            xs = x
        else:
            r = pltpu.roll(x, s, axis=0)
            head = jnp.where(row < s, pltpu.roll(prev, s, axis=0), r[0:SUBLANES])
            xs = head if rows == SUBLANES else jnp.concatenate([head, r[SUBLANES:]], axis=0)
        term = xs * cw[i:i + 1, :]
        acc = term if acc is None else acc + term
    tail = x[rows - SUBLANES:rows]
    cbo_ref[...] = tail
    xbuf[0:SUBLANES, :] = tail
    return acc


def _chunk_masks(c):
    ii = lax.broadcasted_iota(jnp.int32, (c, c), 0)
    jj = lax.broadcasted_iota(jnp.int32, (c, c), 1)
    return ii >= jj, ii > jj, (ii == jj).astype(F32)


def _segment_decay(gc, gct, lane, ge):
    col = gc[:, lane:lane + 1]
    row = gct[lane:lane + 1, :]
    return jnp.where(ge, jnp.exp(jnp.where(ge, col - row, 0.0)), 0.0)


def _inv_unit_lower_minus_eye(a_list, c, nh):
    w = nh * c
    blk_r = lax.broadcasted_iota(jnp.int32, (w, w), 0) // c
    blk_c = lax.broadcasted_iota(jnp.int32, (w, w), 1) // c
    same = blk_r == blk_c

    def block_diag(p):
        return jnp.where(same, jnp.concatenate([p] * nh, axis=0), 0.0).astype(BF16)

    ys = [-a for a in a_list]
    ps = [_bdot(a, block_diag(a)) for a in a_list]
    n = 2
    while n < c:
        pbds = [block_diag(p) for p in ps]
        n *= 2
        if n < c:
            sts = [_bdot(jnp.concatenate([y, p], axis=0), pbd) for y, p, pbd in zip(ys, ps, pbds)]
            ys = [y + p + st[:c] for y, p, st in zip(ys, ps, sts)]
            ps = [st[c:] for st in sts]
        else:
            ys = [y + p + _bdot(y, pbd) for y, p, pbd in zip(ys, ps, pbds)]
    return ys


def _dn_prep(items, lmat, ge, gt_all, chunk):
    heads = range(DN_HEADS)
    gcs = [_hdot(lmat, g_all) for _, _, g_all in items]
    gcts = [gc.T for gc in gcs]
    qs = [[_l2norm(qkv[:, h * DN_DK:(h + 1) * DN_DK]) * DN_DK ** -0.5 for h in heads] for qkv, _, _ in items]
    ks = [[_l2norm(qkv[:, DN_QK + h * DN_DK:DN_QK + (h + 1) * DN_DK]) for h in heads] for qkv, _, _ in items]
    vs = [[qkv[:, 2 * DN_QK + h * DN_DV:2 * DN_QK + (h + 1) * DN_DV] for h in heads] for qkv, _, _ in items]
    betas = [[beta_all[:, SM_B + h:SM_B + h + 1] for h in heads] for _, beta_all, _ in items]
    kbs = [[k.astype(BF16) for k in kk] for kk in ks]
    kks = [jnp.concatenate([_bdot(kb, kb, NT) for kb in kb4], axis=1) for kb4 in kbs]
    qks = [jnp.concatenate([_bdot(q, kb, NT) for q, kb in zip(q4, kb4)], axis=1) for q4, kb4 in zip(qs, kbs)]
    decs = [jnp.concatenate([_segment_decay(gc, gct, SM_A + h, ge) for h in heads], axis=1)
            for gc, gct in zip(gcs, gcts)]
    beta_ws = [jnp.concatenate([jnp.broadcast_to(b, (chunk, chunk)) for b in b4], axis=1) for b4 in betas]
    a_list = [jnp.where(gt_all, bw * kk * dec, 0.0) for bw, kk, dec in zip(beta_ws, kks, decs)]
    tms = _inv_unit_lower_minus_eye(a_list, chunk, DN_HEADS)
    attns = [(qk * dec).astype(BF16) for qk, dec in zip(qks, decs)]
    egs = [jnp.exp(gc) for gc in gcs]
    rhss = [[jnp.concatenate([vs[i][h] * betas[i][h], ks[i][h] * (betas[i][h] * egs[i][:, SM_A + h:SM_A + h + 1])],
                             axis=1) for h in heads] for i in range(len(items))]
    uws = [[rhss[i][h] + _bdot(tms[i][:, h * chunk:(h + 1) * chunk], rhss[i][h]) for h in heads]
           for i in range(len(items))]
    out = []
    for i, gc in enumerate(gcs):
        glast = gc[chunk - 1:chunk, :]
        ekd = jnp.exp(glast - gc)
        egl = jnp.exp(glast)
        per_head = []
        for h in heads:
            uw = uws[i][h]
            wq = jnp.concatenate([uw[:, DN_DV:], qs[i][h] * egs[i][:, SM_A + h:SM_A + h + 1]], axis=0).astype(BF16)
            kd = (ks[i][h] * ekd[:, SM_A + h:SM_A + h + 1]).astype(BF16)
            per_head.append((uw[:, :DN_DV], wq, attns[i][:, h * chunk:(h + 1) * chunk], kd,
                             egl[:, SM_A + h:SM_A + h + 1]))
        out.append(per_head)
    return out


def _dn_kernel(qkv_ref, z_ref, sm_ref, s0_ref, cb_ref, cw_ref, prm_ref, nw_ref,
               o_ref, so_ref, cbo_ref, xbuf, *, nseq, rows, chunk, front_pad):
    blk = pl.program_id(1)
    first = blk == 0

    @pl.when(first)
    def _():
        so_ref[...] = s0_ref[...]
        xbuf[:, 0:SUBLANES, :] = cb_ref[...]

    prm = prm_ref[...]
    nw = nw_ref[...]
    ge, _, _ = _chunk_masks(chunk)
    lmat = ge.astype(F32)
    wide = (chunk, DN_HEADS * chunk)
    gt_all = lax.broadcasted_iota(jnp.int32, wide, 0) > lax.broadcasted_iota(jnp.int32, wide, 1) % chunk
    nchunk = rows // chunk

    items = []
    for s in range(nseq):
        qkv = _silu(_causal_conv(xbuf.at[s], qkv_ref.at[s], cw_ref, cbo_ref.at[s], rows))
        sm = sm_ref[s]
        beta_all = _sigmoid(sm)
        g_all = -jnp.exp(prm[PRM_DN_ALOG:PRM_DN_ALOG + 1, :]) * _softplus(sm + prm[PRM_DN_DTB:PRM_DN_DTB + 1, :])
        if front_pad:
            pos = blk * rows + lax.broadcasted_iota(jnp.int32, (rows, 1), 0)
            g_all = jnp.where(pos < front_pad, 0.0, g_all)
        for c in range(nchunk):
            cs = slice(c * chunk, (c + 1) * chunk)
            items.append((qkv[cs], beta_all[cs], g_all[cs]))
    prep = _dn_prep(items, lmat, ge, gt_all, chunk)

    chains = [(s, h) for s in range(nseq) for h in range(DN_HEADS)]
    states = [so_ref[s, h] for s, h in chains]
    for c in range(nchunk):
        r0 = c * chunk
        fac = [prep[s * nchunk + c][h] for s, h in chains]
        m1s = [_bdot(f[1], st) for f, st in zip(fac, states)]
        v_news = [(f[0] - m1[:chunk]).astype(BF16) for f, m1 in zip(fac, m1s)]
        ups = [_bdot(f[3], v, TN) for f, v in zip(fac, v_news)]
        os_ = [m1[chunk:] + _bdot(f[2], v) for f, m1, v in zip(fac, m1s, v_news)]
        states = [st * f[4] + up for f, st, up in zip(fac, states, ups)]
        for (s, h), o in zip(chains, os_):
            lo = h * DN_DV
            zh = z_ref[s, r0:r0 + chunk, lo:lo + DN_DV]
            o_ref[s, r0:r0 + chunk, lo:lo + DN_DV] = _rmsnorm(o, nw) * _silu(zh)
    for (s, h), st in zip(chains, states):
        so_ref[s, h] = st


def _dn_mixer(proj, nblk, blk0, rows, chunk, front_pad, nseq, s0, layer, cbuf, cw, prm, nw):
    nb = proj.shape[0]
    col = lambda c: c[0] // c[1]
    phys = lambda i: (i + blk0) % nblk
    return pl.pallas_call(
        functools.partial(_dn_kernel, nseq=nseq, rows=rows, chunk=chunk, front_pad=front_pad),
        out_shape=(jax.ShapeDtypeStruct((nb, nblk * rows, DN_V), F32),
                   jax.ShapeDtypeStruct((nb, DN_HEADS, DN_DK, DN_DV), F32),
                   jax.ShapeDtypeStruct((nb, SUBLANES, DN_CONV), F32)),
        grid=(nb // nseq, nblk),
        in_specs=[
            pl.BlockSpec((nseq, rows, DN_CONV), lambda b, i: (b, phys(i), col(COL_QKV))),
            pl.BlockSpec((nseq, rows, DN_V), lambda b, i: (b, phys(i), col(COL_DNZ))),
            pl.BlockSpec((nseq, rows, LANES), lambda b, i: (b, phys(i), col(COL_SM))),
            pl.BlockSpec((None, nseq, DN_HEADS, DN_DK, DN_DV), lambda b, i: (layer, b, 0, 0, 0)),
            pl.BlockSpec((nseq, SUBLANES, DN_CONV), lambda b, i: (b, 0, 0)),
            pl.BlockSpec((CONV_WIDTH, DN_CONV), lambda b, i: (0, 0)),
            pl.BlockSpec((SUBLANES, LANES), lambda b, i: (0, 0)),
            pl.BlockSpec((1, DN_DV), lambda b, i: (0, 0)),
        ],
        out_specs=(
            pl.BlockSpec((nseq, rows, DN_V), lambda b, i: (b, phys(i), 0)),
            pl.BlockSpec((nseq, DN_HEADS, DN_DK, DN_DV), lambda b, i: (b, 0, 0, 0)),
            pl.BlockSpec((nseq, SUBLANES, DN_CONV), lambda b, i: (b, 0, 0)),
        ),
        scratch_shapes=[pltpu.VMEM((nseq, SUBLANES, DN_CONV), F32)],
        compiler_params=pltpu.CompilerParams(dimension_semantics=("arbitrary", "arbitrary"),
                                             vmem_limit_bytes=VMEM_LIMIT),
        name="dn_mixer",
    )(proj, proj, proj, s0, cbuf, cw, prm, nw)


def _ssd_kernel(xbc_ref, z_ref, sm_ref, h0_ref, cb_ref, cw_ref, cbias_ref, prm_ref, drow_ref, nw_ref,
                y_ref, ho_ref, cbo_ref, xbuf, *, nseq, rows, chunk, front_pad):
    blk = pl.program_id(1)

    @pl.when(blk == 0)
    def _():
        ho_ref[...] = h0_ref[...]
        xbuf[:, 0:SUBLANES, :] = cb_ref[...]

    prm = prm_ref[...]
    nw = nw_ref[...]
    drow = drow_ref[...]
    cbias = cbias_ref[...]
    ge, _, _ = _chunk_masks(chunk)
    lmat = ge.astype(F32)
    hpg = SSM_HEADS // SSM_GROUPS
    gw = hpg * SSM_HEADDIM
    lane = lax.broadcasted_iota(jnp.int32, (1, gw), 1)
    srow = lax.broadcasted_iota(jnp.int32, (gw, 1), 0)
    in_head = [(lane >= j * SSM_HEADDIM) & (lane < (j + 1) * SSM_HEADDIM) for j in range(hpg)]
    nchunk = rows // chunk
    groups = range(SSM_GROUPS)

    items = []
    for s in range(nseq):
        act = _silu(_causal_conv(xbuf.at[s], xbc_ref.at[s], cw_ref, cbo_ref.at[s], rows) + cbias)
        dt_all = _softplus(sm_ref[s] + prm[PRM_SSM_DTB:PRM_SSM_DTB + 1, :])
        if front_pad:
            pos = blk * rows + lax.broadcasted_iota(jnp.int32, (rows, 1), 0)
            dt_all = jnp.where(pos < front_pad, 0.0, dt_all)
        g_all = dt_all * (-jnp.exp(prm[PRM_SSM_ALOG:PRM_SSM_ALOG + 1, :]))
        for c in range(nchunk):
            cs = slice(c * chunk, (c + 1) * chunk)
            items.append((act[cs], dt_all[cs], g_all[cs]))
    n_items = len(items)
    gcs = [_hdot(lmat, g) for _, _, g in items]
    gcts = [gc.T for gc in gcs]
    xgs = [[a[:, g * gw:(g + 1) * gw] for g in groups] for a, _, _ in items]
    bgs = [[a[:, SSM_INNER + g * SSM_STATE:SSM_INNER + (g + 1) * SSM_STATE] for g in groups] for a, _, _ in items]
    cgs = [[a[:, SSM_INNER + SSM_BC + g * SSM_STATE:SSM_INNER + SSM_BC + (g + 1) * SSM_STATE] for g in groups]
           for a, _, _ in items]
    cbs = [[_bdot(cgs[i][g], bgs[i][g], NT) for g in groups] for i in range(n_items)]
    egs = [jnp.exp(gc) for gc in gcs]
    ekds = [jnp.exp(gc[chunk - 1:chunk, :] - gc) for gc in gcs]
    egls = [jnp.exp(gc[chunk - 1:chunk, :]) for gc in gcs]
    heads = [(g, j) for g in groups for j in range(hpg)]
    ln = lambda g, j: SM_DT + g * hpg + j
    xdts = [[jnp.where(in_head[j], xgs[i][g] * items[i][1][:, ln(g, j):ln(g, j) + 1], 0.0).astype(BF16)
             for g, j in heads] for i in range(n_items)]
    attns = [[cbs[i][g] * _segment_decay(gcs[i], gcts[i], ln(g, j), ge) for g, j in heads] for i in range(n_items)]
    y_intras = [[_bdot(attns[i][k], xdts[i][k]) for k in range(len(heads))] for i in range(n_items)]
    upds = [[_bdot(xdts[i][k], bgs[i][g] * ekds[i][:, ln(g, j):ln(g, j) + 1], TN) for k, (g, j) in enumerate(heads)]
            for i in range(n_items)]
    cds = [[jnp.concatenate([cgs[i][g] * egs[i][:, ln(g, j):ln(g, j) + 1] for j in range(hpg)], axis=0).astype(BF16)
            for g in groups] for i in range(n_items)]
    y_loc = [[sum(y_intras[i][g * hpg + j] for j in range(hpg)) for g in groups] for i in range(n_items)]
    h_inc = [[sum(upds[i][g * hpg + j] for j in range(hpg)) for g in groups] for i in range(n_items)]
    gl_cols = []
    for i in range(n_items):
        per_group = []
        for g in groups:
            gl = egls[i][:, ln(g, 0):ln(g, 0) + 1]
            for j in range(1, hpg):
                gl = jnp.where(srow < j * SSM_HEADDIM, gl, egls[i][:, ln(g, j):ln(g, j) + 1])
            per_group.append(gl)
        gl_cols.append(per_group)

    chains = [(s, g) for s in range(nseq) for g in groups]
    states = [ho_ref[s, g] for s, g in chains]
    for c in range(nchunk):
        r0 = c * chunk
        idx = [s * nchunk + c for s, _ in chains]
        yis = [_bdot(cds[i][g], st, NT) for i, (_, g), st in zip(idx, chains, states)]
        states = [st * gl_cols[i][g] + h_inc[i][g] for i, (_, g), st in zip(idx, chains, states)]
        for i, (s, g), yi in zip(idx, chains, yis):
            y_inter = yi[0:chunk]
            for j in range(1, hpg):
                y_inter = jnp.where(in_head[j], yi[j * chunk:(j + 1) * chunk], y_inter)
            yg = y_loc[i][g] + y_inter + xgs[i][g] * drow[:, g * gw:(g + 1) * gw]
            yg = yg * _silu(z_ref[s, r0:r0 + chunk, g * gw:(g + 1) * gw])
            y_ref[s, r0:r0 + chunk, g * gw:(g + 1) * gw] = _rmsnorm(yg, nw[:, g * gw:(g + 1) * gw])
    for (s, g), st in zip(chains, states):
        ho_ref[s, g] = st


def _ssd_mixer(proj, nblk, blk0, rows, chunk, front_pad, nseq, h0, layer, cbuf, cw, cbias, prm, drow, nw):
    nb = proj.shape[0]
    col = lambda c: c[0] // c[1]
    phys = lambda i: (i + blk0) % nblk
    gw = (SSM_HEADS // SSM_GROUPS) * SSM_HEADDIM
    return pl.pallas_call(
        functools.partial(_ssd_kernel, nseq=nseq, rows=rows, chunk=chunk, front_pad=front_pad),
        out_shape=(jax.ShapeDtypeStruct((nb, nblk * rows, SSM_INNER), F32),
                   jax.ShapeDtypeStruct((nb, SSM_GROUPS, gw, SSM_STATE), F32),
                   jax.ShapeDtypeStruct((nb, SUBLANES, SSM_CONV), F32)),
        grid=(nb // nseq, nblk),
        in_specs=[
            pl.BlockSpec((nseq, rows, SSM_CONV), lambda b, i: (b, phys(i), col(COL_XBC))),
            pl.BlockSpec((nseq, rows, SSM_INNER), lambda b, i: (b, phys(i), col(COL_SSZ))),
            pl.BlockSpec((nseq, rows, LANES), lambda b, i: (b, phys(i), col(COL_SM))),
            pl.BlockSpec((None, nseq, SSM_GROUPS, gw, SSM_STATE), lambda b, i: (layer, b, 0, 0, 0)),
            pl.BlockSpec((nseq, SUBLANES, SSM_CONV), lambda b, i: (b, 0, 0)),
            pl.BlockSpec((CONV_WIDTH, SSM_CONV), lambda b, i: (0, 0)),
            pl.BlockSpec((1, SSM_CONV), lambda b, i: (0, 0)),
            pl.BlockSpec((SUBLANES, LANES), lambda b, i: (0, 0)),
            pl.BlockSpec((1, SSM_INNER), lambda b, i: (0, 0)),
            pl.BlockSpec((1, SSM_INNER), lambda b, i: (0, 0)),
        ],
        out_specs=(
            pl.BlockSpec((nseq, rows, SSM_INNER), lambda b, i: (b, phys(i), 0)),
            pl.BlockSpec((nseq, SSM_GROUPS, gw, SSM_STATE), lambda b, i: (b, 0, 0, 0)),
            pl.BlockSpec((nseq, SUBLANES, SSM_CONV), lambda b, i: (b, 0, 0)),
        ),
        scratch_shapes=[pltpu.VMEM((nseq, SUBLANES, SSM_CONV), F32)],
        compiler_params=pltpu.CompilerParams(dimension_semantics=("arbitrary", "arbitrary"),
                                             vmem_limit_bytes=VMEM_LIMIT),
        name="ssd_mixer",
    )(proj, proj, proj, h0, cbuf, cw, cbias, prm, drow, nw)


def _rope(x, cos, sin_signed):
    w = x.shape[-1]
    half = SWA_HEAD_DIM // 2
    lane = lax.broadcasted_iota(jnp.int32, (1, w), 1)
    first_half = (lane % SWA_HEAD_DIM) < half
    swapped = jnp.where(first_half, pltpu.roll(x, w - half, axis=1), pltpu.roll(x, half, axis=1))
    return x * cos + swapped * sin_signed


def _sink_attend(problems):
    scale = SWA_HEAD_DIM ** -0.5
    scores = [[jnp.where(m, _bdot(q, k, NT) * scale, NEG_BIG) for k, m in zip(keys, masks)]
              for q, keys, _, masks, _ in problems]
    outs = []
    probs, dens = [], []
    for (_, _, _, _, sink), ss in zip(problems, scores):
        mx = sink
        for s in ss:
            mx = jnp.maximum(mx, jnp.max(s, axis=-1, keepdims=True))
        ps = [jnp.exp(s - mx) for s in ss]
        den = jnp.exp(sink - mx)
        for p in ps:
            den = den + jnp.sum(p, axis=-1, keepdims=True)
        probs.append(ps)
        dens.append(den)
    pvs = [[_bdot(p, v) for p, v in zip(ps, vals)] for (_, _, vals, _, _), ps in zip(problems, probs)]
    for pv, den in zip(pvs, dens):
        acc = pv[0]
        for extra in pv[1:]:
            acc = acc + extra
        outs.append(acc / den)
    return outs


def _swa_problems(q, key_sets, val_sets, masks, prm, tq):
    grp = SWA_Q_HEADS // SWA_KV_HEADS
    row = lax.broadcasted_iota(jnp.int32, (grp * tq, 1), 0)
    problems = []
    for j in range(SWA_KV_HEADS):
        ks = slice(j * SWA_HEAD_DIM, (j + 1) * SWA_HEAD_DIM)
        heads = [j * grp + g for g in range(grp)]
        qst = jnp.concatenate([q[:, h * SWA_HEAD_DIM:(h + 1) * SWA_HEAD_DIM] for h in heads], axis=0)
        sink = prm[PRM_SINK:PRM_SINK + 1, heads[0]:heads[0] + 1]
        for g in range(1, grp):
            sink = jnp.where(row < g * tq, sink, prm[PRM_SINK:PRM_SINK + 1, heads[g]:heads[g] + 1])
        problems.append((qst, [k[:, ks] for k in key_sets], [v[:, ks] for v in val_sets], masks, sink))
    return problems


def _swa_store(o_ref_at, outs, tq):
    grp = SWA_Q_HEADS // SWA_KV_HEADS
    for j, o in enumerate(outs):
        for g in range(grp):
            h = j * grp + g
            o_ref_at(slice(h * SWA_HEAD_DIM, (h + 1) * SWA_HEAD_DIM), o[g * tq:(g + 1) * tq])


def _swa_prompt_kernel(q_ref, k_ref, v_ref, cos_ref, sin_ref, prm_ref, o_ref, ko_ref, vo_ref, kprev, vprev,
                       *, nseq, front_pad):
    blk = pl.program_id(0)

    @pl.when(blk == 0)
    def _():
        kprev[...] = jnp.zeros_like(kprev)
        vprev[...] = jnp.zeros_like(vprev)

    cos = cos_ref[...]
    sin = sin_ref[...]
    cos_q = jnp.concatenate([cos, cos], axis=1)
    sin_q = jnp.concatenate([sin, sin], axis=1)
    grp = SWA_Q_HEADS // SWA_KV_HEADS
    qi = lax.broadcasted_iota(jnp.int32, (grp * BLOCK, BLOCK), 0) % BLOCK
    kj = lax.broadcasted_iota(jnp.int32, (grp * BLOCK, BLOCK), 1)
    mask_cur = (kj <= qi) & (blk * BLOCK + kj >= front_pad)
    mask_prev = (kj > qi) & ((blk - 1) * BLOCK + kj >= front_pad)
    prm = prm_ref[...]
    problems = []
    for s in range(nseq):
        q = _rope(q_ref[s], cos_q, sin_q)
        k = _rope(k_ref[s], cos, sin)
        v = v_ref[s]
        problems += _swa_problems(q, (kprev[s], k), (vprev[s], v), (mask_prev, mask_cur), prm, BLOCK)
        kprev[s] = k
        vprev[s] = v
        ko_ref[s] = k
        vo_ref[s] = v
    outs = _sink_attend(problems)
    for s in range(nseq):
        def put(cols, val, s=s):
            o_ref[s, :, cols] = val
        _swa_store(put, outs[s * SWA_KV_HEADS:(s + 1) * SWA_KV_HEADS], BLOCK)


def _swa_prompt(proj, nblk, blk0, front_pad, cos, sin, prm):
    nb = proj.shape[0]
    col = lambda c: c[0] // c[1]
    phys = lambda i: (i + blk0) % nblk
    return pl.pallas_call(
        functools.partial(_swa_prompt_kernel, nseq=nb, front_pad=front_pad),
        out_shape=(jax.ShapeDtypeStruct((nb, nblk * BLOCK, SWA_Q), F32),
                   jax.ShapeDtypeStruct((nb, WINDOW, SWA_KV), F32),
                   jax.ShapeDtypeStruct((nb, WINDOW, SWA_KV), F32)),
        grid=(nblk,),
        in_specs=[
            pl.BlockSpec((nb, BLOCK, SWA_Q), lambda i: (0, phys(i), col(COL_SWQ))),
            pl.BlockSpec((nb, BLOCK, SWA_KV), lambda i: (0, phys(i), col(COL_SWK))),
            pl.BlockSpec((nb, BLOCK, SWA_KV), lambda i: (0, phys(i), col(COL_SWV))),
            pl.BlockSpec((BLOCK, SWA_KV), lambda i: (i, 0)),
            pl.BlockSpec((BLOCK, SWA_KV), lambda i: (i, 0)),
            pl.BlockSpec((SUBLANES, LANES), lambda i: (0, 0)),
        ],
        out_specs=(
            pl.BlockSpec((nb, BLOCK, SWA_Q), lambda i: (0, phys(i), 0)),
            pl.BlockSpec((nb, WINDOW, SWA_KV), lambda i: (0, 0, 0)),
            pl.BlockSpec((nb, WINDOW, SWA_KV), lambda i: (0, 0, 0)),
        ),
        scratch_shapes=[pltpu.VMEM((nb, BLOCK, SWA_KV), F32), pltpu.VMEM((nb, BLOCK, SWA_KV), F32)],
        compiler_params=pltpu.CompilerParams(dimension_semantics=("arbitrary",), vmem_limit_bytes=VMEM_LIMIT),
        name="swa_prompt",
    )(proj, proj, proj, cos, sin, prm)


def _swa_sample_kernel(q_ref, k_ref, v_ref, kc_ref, vc_ref, cos_ref, sin_ref, prm_ref, o_ref, ko_ref, vo_ref,
                       *, nseq, steps):
    cos = cos_ref[...]
    sin = sin_ref[...]
    cos_q = jnp.concatenate([cos, cos], axis=1)
    sin_q = jnp.concatenate([sin, sin], axis=1)
    prm = prm_ref[...]
    grp = SWA_Q_HEADS // SWA_KV_HEADS
    ti = lax.broadcasted_iota(jnp.int32, (grp * steps, WINDOW), 0) % steps
    sj = lax.broadcasted_iota(jnp.int32, (grp * steps, WINDOW), 1)
    mask_cache = sj > ti
    tn = lax.broadcasted_iota(jnp.int32, (grp * steps, steps), 0) % steps
    sn = lax.broadcasted_iota(jnp.int32, (grp * steps, steps), 1)
    mask_new = sn <= tn
    problems = []
    for b in range(nseq):
        q = _rope(q_ref[b], cos_q, sin_q)
        k = _rope(k_ref[b], cos, sin)
        v = v_ref[b]
        kc = kc_ref[b]
        vc = vc_ref[b]
        ko_ref[b, 0:WINDOW - steps, :] = kc[steps:WINDOW, :]
        ko_ref[b, WINDOW - steps:WINDOW, :] = k
        vo_ref[b, 0:WINDOW - steps, :] = vc[steps:WINDOW, :]
        vo_ref[b, WINDOW - steps:WINDOW, :] = v
        problems += _swa_problems(q, (kc, k), (vc, v), (mask_cache, mask_new), prm, steps)
    outs = _sink_attend(problems)
    for b in range(nseq):
        def put(cols, val, b=b):
            o_ref[b, :, cols] = val
        _swa_store(put, outs[b * SWA_KV_HEADS:(b + 1) * SWA_KV_HEADS], steps)


def _swa_sample(proj, nseq, kc, vc, layer, cos, sin, prm):
    nb, steps, _ = proj.shape
    assert WINDOW > steps
    col = lambda c: c[0] // c[1]
    return pl.pallas_call(
        functools.partial(_swa_sample_kernel, nseq=nseq, steps=steps),
        out_shape=(jax.ShapeDtypeStruct((nb, steps, SWA_Q), F32),
                   jax.ShapeDtypeStruct((nb, WINDOW, SWA_KV), F32),
                   jax.ShapeDtypeStruct((nb, WINDOW, SWA_KV), F32)),
        grid=(nb // nseq,),
        in_specs=[
            pl.BlockSpec((nseq, steps, SWA_Q), lambda i: (i, 0, col(COL_SWQ))),
            pl.BlockSpec((nseq, steps, SWA_KV), lambda i: (i, 0, col(COL_SWK))),
            pl.BlockSpec((nseq, steps, SWA_KV), lambda i: (i, 0, col(COL_SWV))),
            pl.BlockSpec((None, nseq, WINDOW, SWA_KV), lambda i: (layer, i, 0, 0)),
            pl.BlockSpec((None, nseq, WINDOW, SWA_KV), lambda i: (layer, i, 0, 0)),
            pl.BlockSpec((steps, SWA_KV), lambda i: (0, 0)),
            pl.BlockSpec((steps, SWA_KV), lambda i: (0, 0)),
            pl.BlockSpec((SUBLANES, LANES), lambda i: (0, 0)),
        ],
        out_specs=(
            pl.BlockSpec((nseq, steps, SWA_Q), lambda i: (i, 0, 0)),
            pl.BlockSpec((nseq, WINDOW, SWA_KV), lambda i: (i, 0, 0)),
            pl.BlockSpec((nseq, WINDOW, SWA_KV), lambda i: (i, 0, 0)),
        ),
        compiler_params=pltpu.CompilerParams(dimension_semantics=("arbitrary",), vmem_limit_bytes=VMEM_LIMIT),
        name="swa_sample",
    )(proj, proj, proj, kc, vc, cos, sin, prm)


def _tail_kernel(x_ref, odn_ref, y_ref, osw_ref, wout_ref, g1_ref, g2_ref, g3_ref, wfi_ref, wfo_ref, o_ref,
                 *, d_ff, tf):
    mixed = jnp.concatenate([odn_ref[...], y_ref[...], osw_ref[...]], axis=1).astype(BF16)
    m = jnp.dot(mixed, wout_ref[...], preferred_element_type=F32)
    x1 = x_ref[...] + _rmsnorm(m, g1_ref[...])
    h = _rmsnorm(x1, g2_ref[...]).astype(BF16)
    y2 = None
    for c in range(d_ff // tf):
        gate = jnp.dot(h, wfi_ref[:, c * tf:(c + 1) * tf], preferred_element_type=F32)
        up = jnp.dot(h, wfi_ref[:, d_ff + c * tf:d_ff + (c + 1) * tf], preferred_element_type=F32)
        part = jnp.dot((_silu(gate) * up).astype(BF16), wfo_ref[c * tf:(c + 1) * tf, :], preferred_element_type=F32)
        y2 = part if y2 is None else y2 + part
    o_ref[...] = x1 + _rmsnorm(y2, g3_ref[...])


def _layer_spec(shape, layer):
    nd = len(shape) - 1
    return pl.BlockSpec((None,) + tuple(shape[1:]), lambda *_: (layer,) + (0,) * nd, pipeline_mode=pl.Buffered(1))


def _tail(x, odn, y, osw, wout, g1, g2, g3, wfi, wfo, layer, l_out, tm_target):
    nb, _, d = x.shape
    d_ff = wfo.shape[1]
    tm = _pick_tile(l_out, tm_target)
    tf = 2 * LANES if d_ff % (2 * LANES) == 0 else d_ff
    row = lambda w: pl.BlockSpec((None, tm, w), lambda b, i: (b, i, 0))
    return pl.pallas_call(
        functools.partial(_tail_kernel, d_ff=d_ff, tf=tf),
        out_shape=jax.ShapeDtypeStruct((nb, l_out, d), F32),
        grid=(nb, l_out // tm),
        in_specs=[row(d), row(DN_V), row(SSM_INNER), row(SWA_Q), _layer_spec(wout.shape, layer),
                  _layer_spec(g1.shape, layer), _layer_spec(g2.shape, layer), _layer_spec(g3.shape, layer),
                  _layer_spec(wfi.shape, layer), _layer_spec(wfo.shape, layer)],
        out_specs=row(d),
        compiler_params=pltpu.CompilerParams(dimension_semantics=("arbitrary", "arbitrary"),
                                             vmem_limit_bytes=VMEM_LIMIT),
        name="outproj_ffn",
    )(x, odn, y, osw, wout, g1, g2, g3, wfi, wfo)


def _reorder_w_in_kernel(w_ref, o_ref):
    w = w_ref[0]
    offs = [0]
    for wd in IN_WIDTHS:
        offs.append(offs[-1] + wd)
    seg = lambda i: w[:, offs[i]:offs[i + 1]]
    dn_qkv, dn_z, dn_b, dn_a, ssm_xbc, ssm_z, ssm_dt, sw_q, sw_k, sw_v = (seg(i) for i in range(len(IN_WIDTHS)))
    n_small = dn_b.shape[1] + dn_a.shape[1] + ssm_dt.shape[1]
    small = jnp.concatenate([dn_b, dn_a, ssm_dt, jnp.zeros((w.shape[0], LANES - n_small), F32)], axis=1)
    out = jnp.concatenate([dn_qkv, dn_z, ssm_z, ssm_xbc, sw_q, sw_k, sw_v, small], axis=1)
    o_ref[0] = out.astype(BF16)


def _reorder_w_in(w):
    depth, d, d_in = w.shape
    assert d_in == sum(IN_WIDTHS)
    tr = _pick_tile(d, 256)
    return pl.pallas_call(
        _reorder_w_in_kernel,
        out_shape=jax.ShapeDtypeStruct((depth, d, D_PROJ), BF16),
        grid=(depth, d // tr),
        in_specs=[pl.BlockSpec((1, tr, d_in), lambda l, i: (l, i, 0))],
        out_specs=pl.BlockSpec((1, tr, D_PROJ), lambda l, i: (l, i, 0)),
        compiler_params=pltpu.CompilerParams(dimension_semantics=("arbitrary", "arbitrary")),
        name="reorder_w_in",
    )(w)


def _lane_row(pairs):
    row = jnp.zeros((LANES,), F32)
    for off, vec in pairs:
        row = row.at[off:off + vec.shape[0]].set(vec.astype(F32))
    return row[None, :]


def _rope_tables(pos):
    half = SWA_HEAD_DIM // 2
    inv = ROPE_THETA ** (-jnp.arange(half, dtype=F32) / half)
    ang = pos.astype(F32)[:, None] * inv[None, :]
    cos = jnp.cos(ang)
    sin = jnp.sin(ang)
    cos_t = jnp.concatenate([cos, cos] * SWA_KV_HEADS, axis=1)
    sin_t = jnp.concatenate([-sin, sin] * SWA_KV_HEADS, axis=1)
    return cos_t, sin_t


def _pad_conv_state(buf):
    return jnp.pad(buf, ((0, 0), (SUBLANES - (CONV_WIDTH - 1), 0), (0, 0)))


def kernel(x_prompt, x_sample, state_dn, state_dn_conv, state_ssm, state_ssm_conv, cache_swa_k, cache_swa_v,
           meta_tokens, w_in, dn_conv_w, dn_a_log, dn_dt_bias, dn_norm_w, ssm_conv_w, ssm_conv_b, ssm_a_log,
           ssm_dt_bias, ssm_d, ssm_norm_w, swa_sinks, w_out, g_pre_mix, g_post_mix, g_pre_ffn, g_post_ffn,
           w_ffn_in, w_ffn_out):
    bp, seq, d = x_prompt.shape
    bs, ts, _ = x_sample.shape
    depth = w_in.shape[0]
    lp = N_META + seq + FRONT_PAD
    assert lp % BLOCK == 0 and BLOCK % CHUNK == 0 and seq % BLOCK == 0
    nblk = lp // BLOCK
    blk0 = nblk - 1
    pad_range = (seq, seq + FRONT_PAD)

    zpad = jnp.zeros((bp, FRONT_PAD, d), x_prompt.dtype)
    meta = jnp.broadcast_to(meta_tokens.astype(x_prompt.dtype)[None], (bp, N_META, d))
    xp = jnp.concatenate([x_prompt, zpad, meta], axis=1)
    xs = x_sample.reshape(1, bs * ts, d)

    cos_p, sin_p = _rope_tables(jnp.arange(lp, dtype=jnp.int32) - FRONT_PAD)
    cos_s, sin_s = _rope_tables(PAST_LEN + jnp.arange(ts, dtype=jnp.int32))

    gw = (SSM_HEADS // SSM_GROUPS) * SSM_HEADDIM
    nseq_s = _pick_tile(bs, 8) if bs % SUBLANES == 0 else bs
    zero_dn = jnp.zeros((1, bp, DN_HEADS, DN_DK, DN_DV), F32)
    zero_dnc = jnp.zeros((bp, SUBLANES, DN_CONV), F32)
    zero_ssm = jnp.zeros((1, bp, SSM_GROUPS, gw, SSM_STATE), F32)
    zero_ssmc = jnp.zeros((bp, SUBLANES, SSM_CONV), F32)
    state_ssm_g = state_ssm.reshape(depth, bs, SSM_GROUPS, gw, SSM_STATE)
    cache_k = cache_swa_k.reshape(depth, bs, WINDOW, SWA_KV)
    cache_v = cache_swa_v.reshape(depth, bs, WINDOW, SWA_KV)

    w_in_r = _reorder_w_in(w_in)
    w_out_b = w_out.astype(BF16)
    w_fi_b = w_ffn_in.astype(BF16)
    w_fo_b = w_ffn_out.astype(BF16)
    g1, g2, g3, g4 = (a[:, None, :] for a in (g_pre_mix, g_post_mix, g_pre_ffn, g_post_ffn))

    new_p, new_s = [], []
    for l in range(depth):
        prm = jnp.concatenate([
            _lane_row([(SM_A, dn_a_log[l])]),
            _lane_row([(SM_A, dn_dt_bias[l])]),
            _lane_row([(SM_DT, ssm_a_log[l])]),
            _lane_row([(SM_DT, ssm_dt_bias[l])]),
            _lane_row([(0, swa_sinks[l])]),
            jnp.zeros((SUBLANES - 5, LANES), F32)], axis=0)
        dn_nw = dn_norm_w[l][None, :]
        ssm_nw = ssm_norm_w[l][None, :]
        drow = jnp.repeat(ssm_d[l], SSM_HEADDIM)[None, :]
        cbias = ssm_conv_b[l][None, :]
        last = l == depth - 1

        proj = _inproj(xp, g1, w_in_r, l, pad_range, TM_DENSE)
        odn, dn_p, dnc_p = _dn_mixer(proj, nblk, blk0, BLOCK, CHUNK, FRONT_PAD, bp, zero_dn, 0,
                                     zero_dnc, dn_conv_w[l], prm, dn_nw)
        ys, ssm_p, ssmc_p = _ssd_mixer(proj, nblk, blk0, BLOCK, CHUNK, FRONT_PAD, bp, zero_ssm, 0,
                                       zero_ssmc, ssm_conv_w[l], cbias, prm, drow, ssm_nw)
        osw, k_p, v_p = _swa_prompt(proj, nblk, blk0, FRONT_PAD, cos_p, sin_p, prm)
        xp = _tail(xp, odn, ys, osw, w_out_b, g2, g3, g4, w_fi_b, w_fo_b, l, seq if last else lp, TM_DENSE)
        new_p.append((dn_p, dnc_p[:, -(CONV_WIDTH - 1):], ssm_p.reshape(bp, SSM_HEADS, SSM_HEADDIM, SSM_STATE),
                      ssmc_p[:, -(CONV_WIDTH - 1):], k_p.reshape(bp, WINDOW, SWA_KV_HEADS, SWA_HEAD_DIM),
                      v_p.reshape(bp, WINDOW, SWA_KV_HEADS, SWA_HEAD_DIM)))

        proj = _inproj(xs, g1, w_in_r, l, None, TM_DENSE).reshape(bs, ts, D_PROJ)
        odn, dn_s, dnc_s = _dn_mixer(proj, 1, 0, ts, ts, 0, nseq_s, state_dn, l,
                                     _pad_conv_state(state_dn_conv[l]), dn_conv_w[l], prm, dn_nw)
        ys, ssm_s, ssmc_s = _ssd_mixer(proj, 1, 0, ts, ts, 0, nseq_s, state_ssm_g, l,
                                       _pad_conv_state(state_ssm_conv[l]), ssm_conv_w[l], cbias, prm, drow, ssm_nw)
        osw, k_s, v_s = _swa_sample(proj, nseq_s, cache_k, cache_v, l, cos_s, sin_s, prm)
        flat = lambda a: a.reshape(1, bs * ts, a.shape[-1])
        xs = _tail(xs, flat(odn), flat(ys), flat(osw), w_out_b, g2, g3, g4, w_fi_b, w_fo_b, l, bs * ts, TM_DENSE)
        new_s.append((dn_s, dnc_s[:, -(CONV_WIDTH - 1):], ssm_s.reshape(bs, SSM_HEADS, SSM_HEADDIM, SSM_STATE),
                      ssmc_s[:, -(CONV_WIDTH - 1):], k_s.reshape(bs, WINDOW, SWA_KV_HEADS, SWA_HEAD_DIM),
                      v_s.reshape(bs, WINDOW, SWA_KV_HEADS, SWA_HEAD_DIM)))

    outs_p = tuple(jnp.stack([st[i] for st in new_p]) for i in range(6))
    outs_s = tuple(jnp.stack([st[i] for st in new_s]) for i in range(6))
    return (xp, xs.reshape(bs, ts, d)) + outs_p + outs_s
```

```python
import functools

import jax
import jax.numpy as jnp
from jax import lax
from jax.experimental import pallas as pl
from jax.experimental.pallas import tpu as pltpu

F32 = jnp.float32
BF16 = jnp.bfloat16
HI = lax.Precision.HIGHEST
NT = (((1,), (1,)), ((), ()))
TN = (((0,), (0,)), ((), ()))

N_META = 16
CONV_WIDTH = 4
CHUNK = 64
BLOCK = 128
WINDOW = 128
FRONT_PAD = BLOCK - N_META
ROPE_THETA = 10000.0
PAST_LEN = 8192
EPS = 1e-6

DN_HEADS, DN_DK, DN_DV = 4, 128, 128
DN_QK = DN_HEADS * DN_DK
DN_V = DN_HEADS * DN_DV
DN_CONV = 2 * DN_QK + DN_V
SSM_HEADS, SSM_HEADDIM, SSM_GROUPS, SSM_STATE = 4, 64, 2, 128
SSM_INNER = SSM_HEADS * SSM_HEADDIM
SSM_BC = SSM_GROUPS * SSM_STATE
SSM_CONV = SSM_INNER + 2 * SSM_BC
SWA_Q_HEADS, SWA_KV_HEADS, SWA_HEAD_DIM = 4, 2, 64
SWA_Q = SWA_Q_HEADS * SWA_HEAD_DIM
SWA_KV = SWA_KV_HEADS * SWA_HEAD_DIM
IN_WIDTHS = (DN_CONV, DN_V, DN_HEADS, DN_HEADS, SSM_CONV, SSM_INNER, SSM_HEADS, SWA_Q, SWA_KV, SWA_KV)

LANES = 128
SUBLANES = 8
COL_QKV = (0, DN_CONV)
COL_DNZ = (1536, DN_V)
COL_SSZ = (2048, SSM_INNER)
COL_XBC = (2304, SSM_CONV)
COL_SWQ = (3072, SWA_Q)
COL_SWK = (3328, SWA_KV)
COL_SWV = (3456, SWA_KV)
COL_SM = (3584, LANES)
D_PROJ = 3712
SM_B, SM_A, SM_DT = 0, 4, 8
PRM_DN_ALOG, PRM_DN_DTB, PRM_SSM_ALOG, PRM_SSM_DTB, PRM_SINK = 0, 1, 2, 3, 4
NEG_BIG = -1e30
VMEM_LIMIT = 56 * 1024 * 1024
TM_DENSE = 640


def _bdot(a, b, dims=None):
    a = a.astype(BF16)
    b = b.astype(BF16)
    if dims is None:
        return jnp.dot(a, b, preferred_element_type=F32)
    return lax.dot_general(a, b, dims, preferred_element_type=F32)


def _hdot(a, b):
    return jnp.dot(a, b, precision=HI, preferred_element_type=F32)


def _rmsnorm(x, g):
    return x * lax.rsqrt(jnp.mean(x * x, axis=-1, keepdims=True) + EPS) * g


def _l2norm(x):
    return x * lax.rsqrt(jnp.sum(x * x, axis=-1, keepdims=True) + EPS)


def _sigmoid(x):
    return 1.0 / (1.0 + jnp.exp(-x))


def _silu(x):
    return x * _sigmoid(x)


def _softplus(x):
    return jnp.maximum(x, 0.0) + jnp.log1p(jnp.exp(-jnp.abs(x)))


def _pick_tile(n, target):
    best = None
    for t in range(SUBLANES, min(n, target) + 1, SUBLANES):
        if n % t == 0:
            best = t
    assert best is not None, n
    return best


def _inproj_kernel(x_ref, g_ref, w_ref, o_ref, *, tm, pad_range):
    h = _rmsnorm(x_ref[...], g_ref[...])
    if pad_range is not None:
        r = pl.program_id(1) * tm + lax.broadcasted_iota(jnp.int32, (tm, 1), 0)
        is_pad = (r >= pad_range[0]) & (r < pad_range[1])
        h = jnp.where(is_pad, 0.0, h)
    o_ref[...] = jnp.dot(h.astype(BF16), w_ref[...], preferred_element_type=F32)


def _inproj(x, g, w, layer, pad_range, tm_target):
    nb, rows, d = x.shape
    tm = _pick_tile(rows, tm_target)
    return pl.pallas_call(
        functools.partial(_inproj_kernel, tm=tm, pad_range=pad_range),
        out_shape=jax.ShapeDtypeStruct((nb, rows, D_PROJ), F32),
        grid=(nb, rows // tm),
        in_specs=[pl.BlockSpec((None, tm, d), lambda b, i: (b, i, 0)), _layer_spec(g.shape, layer),
                  _layer_spec(w.shape, layer)],
        out_specs=pl.BlockSpec((None, tm, D_PROJ), lambda b, i: (b, i, 0)),
        compiler_params=pltpu.CompilerParams(dimension_semantics=("arbitrary", "arbitrary"),
                                             vmem_limit_bytes=VMEM_LIMIT),
        name="inproj",
    )(x, g, w)


def _causal_conv(xbuf, raw_ref, cw_ref, cbo_ref, rows):
    x = raw_ref[...]
    prev = xbuf[0:SUBLANES, :]
    cw = cw_ref[...]
    row = lax.broadcasted_iota(jnp.int32, (SUBLANES, 1), 0)
    acc = None
    for i in range(CONV_WIDTH):
        s = CONV_WIDTH - 1 - i
        if s == 0:
            xs = x
        else:
            r = pltpu.roll(x, s, axis=0)
            head = jnp.where(row < s, pltpu.roll(prev, s, axis=0), r[0:SUBLANES])
            xs = head if rows == SUBLANES else jnp.concatenate([head, r[SUBLANES:]], axis=0)
        term = xs * cw[i:i + 1, :]
        acc = term if acc is None else acc + term
    tail = x[rows - SUBLANES:rows]
    cbo_ref[...] = tail
    xbuf[0:SUBLANES, :] = tail
    return acc


def _chunk_masks(c):
    ii = lax.broadcasted_iota(jnp.int32, (c, c), 0)
    jj = lax.broadcasted_iota(jnp.int32, (c, c), 1)
    return ii >= jj, ii > jj, (ii == jj).astype(F32)


def _segment_decay(gc, gct, lane, ge):
    col = gc[:, lane:lane + 1]
    row = gct[lane:lane + 1, :]
    return jnp.where(ge, jnp.exp(jnp.where(ge, col - row, 0.0)), 0.0)


def _inv_unit_lower_minus_eye(a_list, c, nh):
    w = nh * c
    blk_r = lax.broadcasted_iota(jnp.int32, (w, w), 0) // c
    blk_c = lax.broadcasted_iota(jnp.int32, (w, w), 1) // c
    same = blk_r == blk_c

    def block_diag(p):
        return jnp.where(same, jnp.concatenate([p] * nh, axis=0), 0.0).astype(BF16)

    ys = [-a for a in a_list]
    ps = [_bdot(a, block_diag(a)) for a in a_list]
    n = 2
    while n < c:
        pbds = [block_diag(p) for p in ps]
        n *= 2
        if n < c:
            sts = [_bdot(jnp.concatenate([y, p], axis=0), pbd) for y, p, pbd in zip(ys, ps, pbds)]
            ys = [y + p + st[:c] for y, p, st in zip(ys, ps, sts)]
            ps = [st[c:] for st in sts]
        else:
            ys = [y + p + _bdot(y, pbd) for y, p, pbd in zip(ys, ps, pbds)]
    return ys


def _state_out(stack, layer, nb, nseq, tail, out_index, n_inputs):
    zeros = (0,) * len(tail)
    if stack is None:
        return ((nb,) + tail, pl.BlockSpec((nseq,) + tail, lambda b, *_: (b,) + zeros), [], [], {}, None, 0)
    depth, prev = stack
    shape = (depth, nb) + tail
    if prev is None:
        spec = pl.BlockSpec((depth, nseq) + tail, lambda b, *_: (0, b) + zeros)
        return (shape, spec, [], [], {}, layer, depth)
    spec = pl.BlockSpec((None, nseq) + tail, lambda b, *_: (layer, b) + zeros)
    return (shape, spec, [prev], [pl.BlockSpec(memory_space=pl.ANY)], {n_inputs: out_index}, None, 0)


def _state_view(ref, slot, n_slots):
    if slot is None:
        return ref
    for other in range(n_slots):
        if other != slot:
            ref[other] = jnp.zeros(ref.shape[1:], ref.dtype)
    return ref.at[slot]


def _dn_prep(items, lmat, ge, gt_all, chunk):
    heads = range(DN_HEADS)
    gcs = [_hdot(lmat, g_all) for _, _, g_all in items]
    gcts = [gc.T for gc in gcs]
    qs = [[_l2norm(qkv[:, h * DN_DK:(h + 1) * DN_DK]) * DN_DK ** -0.5 for h in heads] for qkv, _, _ in items]
    ks = [[_l2norm(qkv[:, DN_QK + h * DN_DK:DN_QK + (h + 1) * DN_DK]) for h in heads] for qkv, _, _ in items]
    vs = [[qkv[:, 2 * DN_QK + h * DN_DV:2 * DN_QK + (h + 1) * DN_DV] for h in heads] for qkv, _, _ in items]
    betas = [[beta_all[:, SM_B + h:SM_B + h + 1] for h in heads] for _, beta_all, _ in items]
    kbs = [[k.astype(BF16) for k in kk] for kk in ks]
    kks = [jnp.concatenate([_bdot(kb, kb, NT) for kb in kb4], axis=1) for kb4 in kbs]
    qks = [jnp.concatenate([_bdot(q, kb, NT) for q, kb in zip(q4, kb4)], axis=1) for q4, kb4 in zip(qs, kbs)]
    decs = [jnp.concatenate([_segment_decay(gc, gct, SM_A + h, ge) for h in heads], axis=1)
            for gc, gct in zip(gcs, gcts)]
    beta_ws = [jnp.concatenate([jnp.broadcast_to(b, (chunk, chunk)) for b in b4], axis=1) for b4 in betas]
    a_list = [jnp.where(gt_all, bw * kk * dec, 0.0) for bw, kk, dec in zip(beta_ws, kks, decs)]
    tms = _inv_unit_lower_minus_eye(a_list, chunk, DN_HEADS)
    attns = [(qk * dec).astype(BF16) for qk, dec in zip(qks, decs)]
    egs = [jnp.exp(gc) for gc in gcs]
    rhss = [[jnp.concatenate([vs[i][h] * betas[i][h], ks[i][h] * (betas[i][h] * egs[i][:, SM_A + h:SM_A + h + 1])],
                             axis=1) for h in heads] for i in range(len(items))]
    uws = [[rhss[i][h] + _bdot(tms[i][:, h * chunk:(h + 1) * chunk], rhss[i][h]) for h in heads]
           for i in range(len(items))]
    out = []
    for i, gc in enumerate(gcs):
        glast = gc[chunk - 1:chunk, :]
        ekd = jnp.exp(glast - gc)
        egl = jnp.exp(glast)
        per_head = []
        for h in heads:
            uw = uws[i][h]
            wq = jnp.concatenate([uw[:, DN_DV:], qs[i][h] * egs[i][:, SM_A + h:SM_A + h + 1]], axis=0).astype(BF16)
            kd = (ks[i][h] * ekd[:, SM_A + h:SM_A + h + 1]).astype(BF16)
            per_head.append((uw[:, :DN_DV], wq, attns[i][:, h * chunk:(h + 1) * chunk], kd,
                             egl[:, SM_A + h:SM_A + h + 1]))
        out.append(per_head)
    return out


def _dn_kernel(*refs, nseq, rows, chunk, front_pad, slot, n_slots, n_extra):
    qkv_ref, z_ref, sm_ref, s0_ref, cb_ref, cw_ref, prm_ref, nw_ref = refs[:8]
    o_ref, so_full, cbo_ref, xbuf = refs[8 + n_extra:]
    so_ref = _state_view(so_full, slot, n_slots)
    blk = pl.program_id(1)
    first = blk == 0

    @pl.when(first)
    def _():
        so_ref[...] = s0_ref[...]
        xbuf[:, 0:SUBLANES, :] = cb_ref[...]

    prm = prm_ref[...]
    nw = nw_ref[...]
    ge, _, _ = _chunk_masks(chunk)
    lmat = ge.astype(F32)
    wide = (chunk, DN_HEADS * chunk)
    gt_all = lax.broadcasted_iota(jnp.int32, wide, 0) > lax.broadcasted_iota(jnp.int32, wide, 1) % chunk
    nchunk = rows // chunk

    items = []
    for s in range(nseq):
        qkv = _silu(_causal_conv(xbuf.at[s], qkv_ref.at[s], cw_ref, cbo_ref.at[s], rows))
        sm = sm_ref[s]
        beta_all = _sigmoid(sm)
        g_all = -jnp.exp(prm[PRM_DN_ALOG:PRM_DN_ALOG + 1, :]) * _softplus(sm + prm[PRM_DN_DTB:PRM_DN_DTB + 1, :])
        if front_pad:
            pos = blk * rows + lax.broadcasted_iota(jnp.int32, (rows, 1), 0)
            g_all = jnp.where(pos < front_pad, 0.0, g_all)
        for c in range(nchunk):
            cs = slice(c * chunk, (c + 1) * chunk)
            items.append((qkv[cs], beta_all[cs], g_all[cs]))
    prep = _dn_prep(items, lmat, ge, gt_all, chunk)

    chains = [(s, h) for s in range(nseq) for h in range(DN_HEADS)]
    states = [so_ref[s, h] for s, h in chains]
    for c in range(nchunk):
        r0 = c * chunk
        fac = [prep[s * nchunk + c][h] for s, h in chains]
        m1s = [_bdot(f[1], st) for f, st in zip(fac, states)]
        v_news = [(f[0] - m1[:chunk]).astype(BF16) for f, m1 in zip(fac, m1s)]
        ups = [_bdot(f[3], v, TN) for f, v in zip(fac, v_news)]
        os_ = [m1[chunk:] + _bdot(f[2], v) for f, m1, v in zip(fac, m1s, v_news)]
        states = [st * f[4] + up for f, st, up in zip(fac, states, ups)]
        for (s, h), o in zip(chains, os_):
            lo = h * DN_DV
            zh = z_ref[s, r0:r0 + chunk, lo:lo + DN_DV]
            o_ref[s, r0:r0 + chunk, lo:lo + DN_DV] = _rmsnorm(o, nw) * _silu(zh)
    for (s, h), st in zip(chains, states):
        so_ref[s, h] = st


def _dn_mixer(proj, nblk, blk0, rows, chunk, front_pad, nseq, s0, layer, cbuf, cw, prm, nw, stack=None):
    nb = proj.shape[0]
    col = lambda c: c[0] // c[1]
    phys = lambda i: (i + blk0) % nblk
    so_shape, so_spec, extra, extra_specs, aliases, slot, n_slots = _state_out(
        stack, layer, nb, nseq, (DN_HEADS, DN_DK, DN_DV), 1, 8)
    return pl.pallas_call(
        functools.partial(_dn_kernel, nseq=nseq, rows=rows, chunk=chunk, front_pad=front_pad,
                          slot=slot, n_slots=n_slots, n_extra=len(extra)),
        out_shape=(jax.ShapeDtypeStruct((nb, nblk * rows, DN_V), F32),
                   jax.ShapeDtypeStruct(so_shape, F32),
                   jax.ShapeDtypeStruct((nb, SUBLANES, DN_CONV), F32)),
        grid=(nb // nseq, nblk),
        in_specs=[
            pl.BlockSpec((nseq, rows, DN_CONV), lambda b, i: (b, phys(i), col(COL_QKV))),
            pl.BlockSpec((nseq, rows, DN_V), lambda b, i: (b, phys(i), col(COL_DNZ))),
            pl.BlockSpec((nseq, rows, LANES), lambda b, i: (b, phys(i), col(COL_SM))),
            pl.BlockSpec((None, nseq, DN_HEADS, DN_DK, DN_DV), lambda b, i: (layer, b, 0, 0, 0)),
            pl.BlockSpec((nseq, SUBLANES, DN_CONV), lambda b, i: (b, 0, 0)),
            pl.BlockSpec((CONV_WIDTH, DN_CONV), lambda b, i: (0, 0)),
            pl.BlockSpec((SUBLANES, LANES), lambda b, i: (0, 0)),
            pl.BlockSpec((1, DN_DV), lambda b, i: (0, 0)),
        ] + extra_specs,
        out_specs=(
            pl.BlockSpec((nseq, rows, DN_V), lambda b, i: (b, phys(i), 0)),
            so_spec,
            pl.BlockSpec((nseq, SUBLANES, DN_CONV), lambda b, i: (b, 0, 0)),
        ),
        scratch_shapes=[pltpu.VMEM((nseq, SUBLANES, DN_CONV), F32)],
        input_output_aliases=aliases,
        compiler_params=pltpu.CompilerParams(dimension_semantics=("arbitrary", "arbitrary"),
                                             vmem_limit_bytes=VMEM_LIMIT),
        name="dn_mixer",
    )(proj, proj, proj, s0, cbuf, cw, prm, nw, *extra)


def _ssd_kernel(*refs, nseq, rows, chunk, front_pad, slot, n_slots, n_extra):
    xbc_ref, z_ref, sm_ref, h0_ref, cb_ref, cw_ref, cbias_ref, prm_ref, drow_ref, nw_ref = refs[:10]
    y_ref, ho_full, cbo_ref, xbuf = refs[10 + n_extra:]
    ho_ref = _state_view(ho_full, slot, n_slots)
    blk = pl.program_id(1)

    @pl.when(blk == 0)
    def _():
        ho_ref[...] = h0_ref[...]
        xbuf[:, 0:SUBLANES, :] = cb_ref[...]

    prm = prm_ref[...]
    nw = nw_ref[...]
    drow = drow_ref[...]
    cbias = cbias_ref[...]
    ge, _, _ = _chunk_masks(chunk)
    lmat = ge.astype(F32)
    hpg = SSM_HEADS // SSM_GROUPS
    gw = hpg * SSM_HEADDIM
    lane = lax.broadcasted_iota(jnp.int32, (1, gw), 1)
    srow = lax.broadcasted_iota(jnp.int32, (gw, 1), 0)
    in_head = [(lane >= j * SSM_HEADDIM) & (lane < (j + 1) * SSM_HEADDIM) for j in range(hpg)]
    nchunk = rows // chunk
    groups = range(SSM_GROUPS)

    items = []
    for s in range(nseq):
        act = _silu(_causal_conv(xbuf.at[s], xbc_ref.at[s], cw_ref, cbo_ref.at[s], rows) + cbias)
        dt_all = _softplus(sm_ref[s] + prm[PRM_SSM_DTB:PRM_SSM_DTB + 1, :])
        if front_pad:
            pos = blk * rows + lax.broadcasted_iota(jnp.int32, (rows, 1), 0)
            dt_all = jnp.where(pos < front_pad, 0.0, dt_all)
        g_all = dt_all * (-jnp.exp(prm[PRM_SSM_ALOG:PRM_SSM_ALOG + 1, :]))
        for c in range(nchunk):
            cs = slice(c * chunk, (c + 1) * chunk)
            items.append((act[cs], dt_all[cs], g_all[cs]))
    n_items = len(items)
    gcs = [_hdot(lmat, g) for _, _, g in items]
    gcts = [gc.T for gc in gcs]
    xgs = [[a[:, g * gw:(g + 1) * gw] for g in groups] for a, _, _ in items]
    bgs = [[a[:, SSM_INNER + g * SSM_STATE:SSM_INNER + (g + 1) * SSM_STATE] for g in groups] for a, _, _ in items]
    cgs = [[a[:, SSM_INNER + SSM_BC + g * SSM_STATE:SSM_INNER + SSM_BC + (g + 1) * SSM_STATE] for g in groups]
           for a, _, _ in items]
    cbs = [[_bdot(cgs[i][g], bgs[i][g], NT) for g in groups] for i in range(n_items)]
    egs = [jnp.exp(gc) for gc in gcs]
    ekds = [jnp.exp(gc[chunk - 1:chunk, :] - gc) for gc in gcs]
    egls = [jnp.exp(gc[chunk - 1:chunk, :]) for gc in gcs]
    heads = [(g, j) for g in groups for j in range(hpg)]
    ln = lambda g, j: SM_DT + g * hpg + j
    xdts = [[jnp.where(in_head[j], xgs[i][g] * items[i][1][:, ln(g, j):ln(g, j) + 1], 0.0).astype(BF16)
             for g, j in heads] for i in range(n_items)]
    attns = [[cbs[i][g] * _segment_decay(gcs[i], gcts[i], ln(g, j), ge) for g, j in heads] for i in range(n_items)]
    y_intras = [[_bdot(attns[i][k], xdts[i][k]) for k in range(len(heads))] for i in range(n_items)]
    upds = [[_bdot(xdts[i][k], bgs[i][g] * ekds[i][:, ln(g, j):ln(g, j) + 1], TN) for k, (g, j) in enumerate(heads)]
            for i in range(n_items)]
    cds = [[jnp.concatenate([cgs[i][g] * egs[i][:, ln(g, j):ln(g, j) + 1] for j in range(hpg)], axis=0).astype(BF16)
            for g in groups] for i in range(n_items)]
    y_loc = [[sum(y_intras[i][g * hpg + j] for j in range(hpg)) for g in groups] for i in range(n_items)]
    h_inc = [[sum(upds[i][g * hpg + j] for j in range(hpg)) for g in groups] for i in range(n_items)]
    gl_cols = []
    for i in range(n_items):
        per_group = []
        for g in groups:
            gl = egls[i][:, ln(g, 0):ln(g, 0) + 1]
            for j in range(1, hpg):
                gl = jnp.where(srow < j * SSM_HEADDIM, gl, egls[i][:, ln(g, j):ln(g, j) + 1])
            per_group.append(gl)
        gl_cols.append(per_group)

    chains = [(s, g) for s in range(nseq) for g in groups]
    states = [ho_ref[s, g] for s, g in chains]
    for c in range(nchunk):
        r0 = c * chunk
        idx = [s * nchunk + c for s, _ in chains]
        yis = [_bdot(cds[i][g], st, NT) for i, (_, g), st in zip(idx, chains, states)]
        states = [st * gl_cols[i][g] + h_inc[i][g] for i, (_, g), st in zip(idx, chains, states)]
        for i, (s, g), yi in zip(idx, chains, yis):
            y_inter = yi[0:chunk]
            for j in range(1, hpg):
                y_inter = jnp.where(in_head[j], yi[j * chunk:(j + 1) * chunk], y_inter)
            yg = y_loc[i][g] + y_inter + xgs[i][g] * drow[:, g * gw:(g + 1) * gw]
            yg = yg * _silu(z_ref[s, r0:r0 + chunk, g * gw:(g + 1) * gw])
            y_ref[s, r0:r0 + chunk, g * gw:(g + 1) * gw] = _rmsnorm(yg, nw[:, g * gw:(g + 1) * gw])
    for (s, g), st in zip(chains, states):
        ho_ref[s, g] = st


def _ssd_mixer(proj, nblk, blk0, rows, chunk, front_pad, nseq, h0, layer, cbuf, cw, cbias, prm, drow, nw,
               stack=None):
    nb = proj.shape[0]
    col = lambda c: c[0] // c[1]
    phys = lambda i: (i + blk0) % nblk
    gw = (SSM_HEADS // SSM_GROUPS) * SSM_HEADDIM
    ho_shape, ho_spec, extra, extra_specs, aliases, slot, n_slots = _state_out(
        stack, layer, nb, nseq, (SSM_GROUPS, gw, SSM_STATE), 1, 10)
    return pl.pallas_call(
        functools.partial(_ssd_kernel, nseq=nseq, rows=rows, chunk=chunk, front_pad=front_pad,
                          slot=slot, n_slots=n_slots, n_extra=len(extra)),
        out_shape=(jax.ShapeDtypeStruct((nb, nblk * rows, SSM_INNER), F32),
                   jax.ShapeDtypeStruct(ho_shape, F32),
                   jax.ShapeDtypeStruct((nb, SUBLANES, SSM_CONV), F32)),
        grid=(nb // nseq, nblk),
        in_specs=[
            pl.BlockSpec((nseq, rows, SSM_CONV), lambda b, i: (b, phys(i), col(COL_XBC))),
            pl.BlockSpec((nseq, rows, SSM_INNER), lambda b, i: (b, phys(i), col(COL_SSZ))),
            pl.BlockSpec((nseq, rows, LANES), lambda b, i: (b, phys(i), col(COL_SM))),
            pl.BlockSpec((None, nseq, SSM_GROUPS, gw, SSM_STATE), lambda b, i: (layer, b, 0, 0, 0)),
            pl.BlockSpec((nseq, SUBLANES, SSM_CONV), lambda b, i: (b, 0, 0)),
            pl.BlockSpec((CONV_WIDTH, SSM_CONV), lambda b, i: (0, 0)),
            pl.BlockSpec((1, SSM_CONV), lambda b, i: (0, 0)),
            pl.BlockSpec((SUBLANES, LANES), lambda b, i: (0, 0)),
            pl.BlockSpec((1, SSM_INNER), lambda b, i: (0, 0)),
            pl.BlockSpec((1, SSM_INNER), lambda b, i: (0, 0)),
        ] + extra_specs,
        out_specs=(
            pl.BlockSpec((nseq, rows, SSM_INNER), lambda b, i: (b, phys(i), 0)),
            ho_spec,
            pl.BlockSpec((nseq, SUBLANES, SSM_CONV), lambda b, i: (b, 0, 0)),
        ),
        scratch_shapes=[pltpu.VMEM((nseq, SUBLANES, SSM_CONV), F32)],
        input_output_aliases=aliases,
        compiler_params=pltpu.CompilerParams(dimension_semantics=("arbitrary", "arbitrary"),
                                             vmem_limit_bytes=VMEM_LIMIT),
        name="ssd_mixer",
    )(proj, proj, proj, h0, cbuf, cw, cbias, prm, drow, nw, *extra)


def _rope(x, cos, sin_signed):
    w = x.shape[-1]
    half = SWA_HEAD_DIM // 2
    lane = lax.broadcasted_iota(jnp.int32, (1, w), 1)
    first_half = (lane % SWA_HEAD_DIM) < half
    swapped = jnp.where(first_half, pltpu.roll(x, w - half, axis=1), pltpu.roll(x, half, axis=1))
    return x * cos + swapped * sin_signed


def _sink_attend(problems):
    scale = SWA_HEAD_DIM ** -0.5
    scores = [[jnp.where(m, _bdot(q, k, NT) * scale, NEG_BIG) for k, m in zip(keys, masks)]
              for q, keys, _, masks, _ in problems]
    outs = []
    probs, dens = [], []
    for (_, _, _, _, sink), ss in zip(problems, scores):
        mx = sink
        for s in ss:
            mx = jnp.maximum(mx, jnp.max(s, axis=-1, keepdims=True))
        ps = [jnp.exp(s - mx) for s in ss]
        den = jnp.exp(sink - mx)
        for p in ps:
            den = den + jnp.sum(p, axis=-1, keepdims=True)
        probs.append(ps)
        dens.append(den)
    pvs = [[_bdot(p, v) for p, v in zip(ps, vals)] for (_, _, vals, _, _), ps in zip(problems, probs)]
    for pv, den in zip(pvs, dens):
        acc = pv[0]
        for extra in pv[1:]:
            acc = acc + extra
        outs.append(acc / den)
    return outs


def _swa_problems(q, key_sets, val_sets, masks, prm, tq):
    grp = SWA_Q_HEADS // SWA_KV_HEADS
    row = lax.broadcasted_iota(jnp.int32, (grp * tq, 1), 0)
    problems = []
    for j in range(SWA_KV_HEADS):
        ks = slice(j * SWA_HEAD_DIM, (j + 1) * SWA_HEAD_DIM)
        heads = [j * grp + g for g in range(grp)]
        qst = jnp.concatenate([q[:, h * SWA_HEAD_DIM:(h + 1) * SWA_HEAD_DIM] for h in heads], axis=0)
        sink = prm[PRM_SINK:PRM_SINK + 1, heads[0]:heads[0] + 1]
        for g in range(1, grp):
            sink = jnp.where(row < g * tq, sink, prm[PRM_SINK:PRM_SINK + 1, heads[g]:heads[g] + 1])
        problems.append((qst, [k[:, ks] for k in key_sets], [v[:, ks] for v in val_sets], masks, sink))
    return problems


def _swa_store(o_ref_at, outs, tq):
    grp = SWA_Q_HEADS // SWA_KV_HEADS
    for j, o in enumerate(outs):
        for g in range(grp):
            h = j * grp + g
            o_ref_at(slice(h * SWA_HEAD_DIM, (h + 1) * SWA_HEAD_DIM), o[g * tq:(g + 1) * tq])


def _swa_prompt_kernel(q_ref, k_ref, v_ref, cos_ref, sin_ref, prm_ref, o_ref, ko_ref, vo_ref, kprev, vprev,
                       *, nseq, front_pad):
    blk = pl.program_id(0)

    @pl.when(blk == 0)
    def _():
        kprev[...] = jnp.zeros_like(kprev)
        vprev[...] = jnp.zeros_like(vprev)

    cos = cos_ref[...]
    sin = sin_ref[...]
    cos_q = jnp.concatenate([cos, cos], axis=1)
    sin_q = jnp.concatenate([sin, sin], axis=1)
    grp = SWA_Q_HEADS // SWA_KV_HEADS
    qi = lax.broadcasted_iota(jnp.int32, (grp * BLOCK, BLOCK), 0) % BLOCK
    kj = lax.broadcasted_iota(jnp.int32, (grp * BLOCK, BLOCK), 1)
    mask_cur = (kj <= qi) & (blk * BLOCK + kj >= front_pad)
    mask_prev = (kj > qi) & ((blk - 1) * BLOCK + kj >= front_pad)
    prm = prm_ref[...]
    problems = []
    for s in range(nseq):
        q = _rope(q_ref[s], cos_q, sin_q)
        k = _rope(k_ref[s], cos, sin)
        v = v_ref[s]
        problems += _swa_problems(q, (kprev[s], k), (vprev[s], v), (mask_prev, mask_cur), prm, BLOCK)
        kprev[s] = k
        vprev[s] = v
        ko_ref[s] = k
        vo_ref[s] = v
    outs = _sink_attend(problems)
    for s in range(nseq):
        def put(cols, val, s=s):
            o_ref[s, :, cols] = val
        _swa_store(put, outs[s * SWA_KV_HEADS:(s + 1) * SWA_KV_HEADS], BLOCK)


def _swa_prompt(proj, nblk, blk0, front_pad, cos, sin, prm):
    nb = proj.shape[0]
    col = lambda c: c[0] // c[1]
    phys = lambda i: (i + blk0) % nblk
    return pl.pallas_call(
        functools.partial(_swa_prompt_kernel, nseq=nb, front_pad=front_pad),
        out_shape=(jax.ShapeDtypeStruct((nb, nblk * BLOCK, SWA_Q), F32),
                   jax.ShapeDtypeStruct((nb, WINDOW, SWA_KV), F32),
                   jax.ShapeDtypeStruct((nb, WINDOW, SWA_KV), F32)),
        grid=(nblk,),
        in_specs=[
            pl.BlockSpec((nb, BLOCK, SWA_Q), lambda i: (0, phys(i), col(COL_SWQ))),
            pl.BlockSpec((nb, BLOCK, SWA_KV), lambda i: (0, phys(i), col(COL_SWK))),
            pl.BlockSpec((nb, BLOCK, SWA_KV), lambda i: (0, phys(i), col(COL_SWV))),
            pl.BlockSpec((BLOCK, SWA_KV), lambda i: (i, 0)),
            pl.BlockSpec((BLOCK, SWA_KV), lambda i: (i, 0)),
            pl.BlockSpec((SUBLANES, LANES), lambda i: (0, 0)),
        ],
        out_specs=(
            pl.BlockSpec((nb, BLOCK, SWA_Q), lambda i: (0, phys(i), 0)),
            pl.BlockSpec((nb, WINDOW, SWA_KV), lambda i: (0, 0, 0)),
            pl.BlockSpec((nb, WINDOW, SWA_KV), lambda i: (0, 0, 0)),
        ),
        scratch_shapes=[pltpu.VMEM((nb, BLOCK, SWA_KV), F32), pltpu.VMEM((nb, BLOCK, SWA_KV), F32)],
        compiler_params=pltpu.CompilerParams(dimension_semantics=("arbitrary",), vmem_limit_bytes=VMEM_LIMIT),
        name="swa_prompt",
    )(proj, proj, proj, cos, sin, prm)


def _swa_sample_kernel(*refs, nseq, steps, slot, n_slots, n_extra):
    q_ref, k_ref, v_ref, kc_ref, vc_ref, cos_ref, sin_ref, prm_ref = refs[:8]
    o_ref, ko_full, vo_full = refs[8 + n_extra:]
    ko_ref = _state_view(ko_full, slot, n_slots)
    vo_ref = _state_view(vo_full, slot, n_slots)
    cos = cos_ref[...]
    sin = sin_ref[...]
    cos_q = jnp.concatenate([cos, cos], axis=1)
    sin_q = jnp.concatenate([sin, sin], axis=1)
    prm = prm_ref[...]
    grp = SWA_Q_HEADS // SWA_KV_HEADS
    ti = lax.broadcasted_iota(jnp.int32, (grp * steps, WINDOW), 0) % steps
    sj = lax.broadcasted_iota(jnp.int32, (grp * steps, WINDOW), 1)
    mask_cache = sj > ti
    tn = lax.broadcasted_iota(jnp.int32, (grp * steps, steps), 0) % steps
    sn = lax.broadcasted_iota(jnp.int32, (grp * steps, steps), 1)
    mask_new = sn <= tn
    problems = []
    for b in range(nseq):
        q = _rope(q_ref[b], cos_q, sin_q)
        k = _rope(k_ref[b], cos, sin)
        v = v_ref[b]
        kc = kc_ref[b]
        vc = vc_ref[b]
        ko_ref[b, 0:WINDOW - steps, :] = kc[steps:WINDOW, :]
        ko_ref[b, WINDOW - steps:WINDOW, :] = k
        vo_ref[b, 0:WINDOW - steps, :] = vc[steps:WINDOW, :]
        vo_ref[b, WINDOW - steps:WINDOW, :] = v
        problems += _swa_problems(q, (kc, k), (vc, v), (mask_cache, mask_new), prm, steps)
    outs = _sink_attend(problems)
    for b in range(nseq):
        def put(cols, val, b=b):
            o_ref[b, :, cols] = val
        _swa_store(put, outs[b * SWA_KV_HEADS:(b + 1) * SWA_KV_HEADS], steps)


def _swa_sample(proj, nseq, kc, vc, layer, cos, sin, prm, stack_k=None, stack_v=None):
    nb, steps, _ = proj.shape
    assert WINDOW > steps
    col = lambda c: c[0] // c[1]
    ko_shape, ko_spec, extra_k, specs_k, alias_k, slot, n_slots = _state_out(
        stack_k, layer, nb, nseq, (WINDOW, SWA_KV), 1, 8)
    vo_shape, vo_spec, extra_v, specs_v, alias_v, _, _ = _state_out(
        stack_v, layer, nb, nseq, (WINDOW, SWA_KV), 2, 8 + len(extra_k))
    extra = extra_k + extra_v
    return pl.pallas_call(
        functools.partial(_swa_sample_kernel, nseq=nseq, steps=steps, slot=slot, n_slots=n_slots,
                          n_extra=len(extra)),
        out_shape=(jax.ShapeDtypeStruct((nb, steps, SWA_Q), F32),
                   jax.ShapeDtypeStruct(ko_shape, F32),
                   jax.ShapeDtypeStruct(vo_shape, F32)),
        grid=(nb // nseq,),
        in_specs=[
            pl.BlockSpec((nseq, steps, SWA_Q), lambda i: (i, 0, col(COL_SWQ))),
            pl.BlockSpec((nseq, steps, SWA_KV), lambda i: (i, 0, col(COL_SWK))),
            pl.BlockSpec((nseq, steps, SWA_KV), lambda i: (i, 0, col(COL_SWV))),
            pl.BlockSpec((None, nseq, WINDOW, SWA_KV), lambda i: (layer, i, 0, 0)),
            pl.BlockSpec((None, nseq, WINDOW, SWA_KV), lambda i: (layer, i, 0, 0)),
            pl.BlockSpec((steps, SWA_KV), lambda i: (0, 0)),
            pl.BlockSpec((steps, SWA_KV), lambda i: (0, 0)),
            pl.BlockSpec((SUBLANES, LANES), lambda i: (0, 0)),
        ] + specs_k + specs_v,
        out_specs=(pl.BlockSpec((nseq, steps, SWA_Q), lambda i: (i, 0, 0)), ko_spec, vo_spec),
        input_output_aliases={**alias_k, **alias_v},
        compiler_params=pltpu.CompilerParams(dimension_semantics=("arbitrary",), vmem_limit_bytes=VMEM_LIMIT),
        name="swa_sample",
    )(proj, proj, proj, kc, vc, cos, sin, prm, *extra)


def _tail_kernel(x_ref, odn_ref, y_ref, osw_ref, wout_ref, g1_ref, g2_ref, g3_ref, wfi_ref, wfo_ref, o_ref,
                 *, d_ff, tf):
    mixed = jnp.concatenate([odn_ref[...], y_ref[...], osw_ref[...]], axis=1).astype(BF16)
    m = jnp.dot(mixed, wout_ref[...], preferred_element_type=F32)
    x1 = x_ref[...] + _rmsnorm(m, g1_ref[...])
    h = _rmsnorm(x1, g2_ref[...]).astype(BF16)
    y2 = None
    for c in range(d_ff // tf):
        gate = jnp.dot(h, wfi_ref[:, c * tf:(c + 1) * tf], preferred_element_type=F32)
        up = jnp.dot(h, wfi_ref[:, d_ff + c * tf:d_ff + (c + 1) * tf], preferred_element_type=F32)
        part = jnp.dot((_silu(gate) * up).astype(BF16), wfo_ref[c * tf:(c + 1) * tf, :], preferred_element_type=F32)
        y2 = part if y2 is None else y2 + part
    o_ref[...] = x1 + _rmsnorm(y2, g3_ref[...])


def _layer_spec(shape, layer):
    nd = len(shape) - 1
    return pl.BlockSpec((None,) + tuple(shape[1:]), lambda *_: (layer,) + (0,) * nd, pipeline_mode=pl.Buffered(1))


def _tail(x, odn, y, osw, wout, g1, g2, g3, wfi, wfo, layer, l_out, tm_target):
    nb, _, d = x.shape
    d_ff = wfo.shape[1]
    tm = _pick_tile(l_out, tm_target)
    tf = 2 * LANES if d_ff % (2 * LANES) == 0 else d_ff
    row = lambda w: pl.BlockSpec((None, tm, w), lambda b, i: (b, i, 0))
    return pl.pallas_call(
        functools.partial(_tail_kernel, d_ff=d_ff, tf=tf),
        out_shape=jax.ShapeDtypeStruct((nb, l_out, d), F32),
        grid=(nb, l_out // tm),
        in_specs=[row(d), row(DN_V), row(SSM_INNER), row(SWA_Q), _layer_spec(wout.shape, layer),
                  _layer_spec(g1.shape, layer), _layer_spec(g2.shape, layer), _layer_spec(g3.shape, layer),
                  _layer_spec(wfi.shape, layer), _layer_spec(wfo.shape, layer)],
        out_specs=row(d),
        compiler_params=pltpu.CompilerParams(dimension_semantics=("arbitrary", "arbitrary"),
                                             vmem_limit_bytes=VMEM_LIMIT),
        name="outproj_ffn",
    )(x, odn, y, osw, wout, g1, g2, g3, wfi, wfo)


def _reorder_w_in_kernel(w_ref, o_ref):
    w = w_ref[0]
    offs = [0]
    for wd in IN_WIDTHS:
        offs.append(offs[-1] + wd)
    seg = lambda i: w[:, offs[i]:offs[i + 1]]
    dn_qkv, dn_z, dn_b, dn_a, ssm_xbc, ssm_z, ssm_dt, sw_q, sw_k, sw_v = (seg(i) for i in range(len(IN_WIDTHS)))
    n_small = dn_b.shape[1] + dn_a.shape[1] + ssm_dt.shape[1]
    small = jnp.concatenate([dn_b, dn_a, ssm_dt, jnp.zeros((w.shape[0], LANES - n_small), w.dtype)], axis=1)
    o_ref[0] = jnp.concatenate([dn_qkv, dn_z, ssm_z, ssm_xbc, sw_q, sw_k, sw_v, small], axis=1)


def _reorder_w_in(w):
    depth, d, d_in = w.shape
    assert d_in == sum(IN_WIDTHS)
    tr = _pick_tile(d, 256)
    return pl.pallas_call(
        _reorder_w_in_kernel,
        out_shape=jax.ShapeDtypeStruct((depth, d, D_PROJ), BF16),
        grid=(depth, d // tr),
        in_specs=[pl.BlockSpec((1, tr, d_in), lambda l, i: (l, i, 0))],
        out_specs=pl.BlockSpec((1, tr, D_PROJ), lambda l, i: (l, i, 0)),
        compiler_params=pltpu.CompilerParams(dimension_semantics=("arbitrary", "arbitrary")),
        name="reorder_w_in",
    )(w)


def _lane_row(pairs):
    row = jnp.zeros((LANES,), F32)
    for off, vec in pairs:
        row = row.at[off:off + vec.shape[0]].set(vec.astype(F32))
    return row[None, :]


def _rope_tables(pos):
    half = SWA_HEAD_DIM // 2
    inv = ROPE_THETA ** (-jnp.arange(half, dtype=F32) / half)
    ang = pos.astype(F32)[:, None] * inv[None, :]
    cos = jnp.cos(ang)
    sin = jnp.sin(ang)
    cos_t = jnp.concatenate([cos, cos] * SWA_KV_HEADS, axis=1)
    sin_t = jnp.concatenate([-sin, sin] * SWA_KV_HEADS, axis=1)
    return cos_t, sin_t


def _pad_conv_state(buf):
    return jnp.pad(buf, ((0, 0), (SUBLANES - (CONV_WIDTH - 1), 0), (0, 0)))


def kernel(x_prompt, x_sample, state_dn, state_dn_conv, state_ssm, state_ssm_conv, cache_swa_k, cache_swa_v,
           meta_tokens, w_in, dn_conv_w, dn_a_log, dn_dt_bias, dn_norm_w, ssm_conv_w, ssm_conv_b, ssm_a_log,
           ssm_dt_bias, ssm_d, ssm_norm_w, swa_sinks, w_out, g_pre_mix, g_post_mix, g_pre_ffn, g_post_ffn,
           w_ffn_in, w_ffn_out):
    bp, seq, d = x_prompt.shape
    bs, ts, _ = x_sample.shape
    depth = w_in.shape[0]
    lp = N_META + seq + FRONT_PAD
    assert lp % BLOCK == 0 and BLOCK % CHUNK == 0 and seq % BLOCK == 0
    nblk = lp // BLOCK
    blk0 = nblk - 1
    pad_range = (seq, seq + FRONT_PAD)

    zpad = jnp.zeros((bp, FRONT_PAD, d), x_prompt.dtype)
    meta = jnp.broadcast_to(meta_tokens.astype(x_prompt.dtype)[None], (bp, N_META, d))
    xp = jnp.concatenate([x_prompt, zpad, meta], axis=1)
    xs = x_sample.reshape(1, bs * ts, d)

    cos_p, sin_p = _rope_tables(jnp.arange(lp, dtype=jnp.int32) - FRONT_PAD)
    cos_s, sin_s = _rope_tables(PAST_LEN + jnp.arange(ts, dtype=jnp.int32))

    gw = (SSM_HEADS // SSM_GROUPS) * SSM_HEADDIM
    nseq_s = _pick_tile(bs, 8) if bs % SUBLANES == 0 else bs
    zero_dn = jnp.zeros((1, bp, DN_HEADS, DN_DK, DN_DV), F32)
    zero_dnc = jnp.zeros((bp, SUBLANES, DN_CONV), F32)
    zero_ssm = jnp.zeros((1, bp, SSM_GROUPS, gw, SSM_STATE), F32)
    zero_ssmc = jnp.zeros((bp, SUBLANES, SSM_CONV), F32)
    state_ssm_g = state_ssm.reshape(depth, bs, SSM_GROUPS, gw, SSM_STATE)
    cache_k = cache_swa_k.reshape(depth, bs, WINDOW, SWA_KV)
    cache_v = cache_swa_v.reshape(depth, bs, WINDOW, SWA_KV)

    w_in_r = _reorder_w_in(w_in.astype(BF16))
    w_out_b = w_out.astype(BF16)
    w_fi_b = w_ffn_in.astype(BF16)
    w_fo_b = w_ffn_out.astype(BF16)
    g1, g2, g3, g4 = (a[:, None, :] for a in (g_pre_mix, g_post_mix, g_pre_ffn, g_post_ffn))

    new_p, new_s = [], []
    dn_s = ssm_s = k_s = v_s = None
    for l in range(depth):
        prm = jnp.concatenate([
            _lane_row([(SM_A, dn_a_log[l])]),
            _lane_row([(SM_A, dn_dt_bias[l])]),
            _lane_row([(SM_DT, ssm_a_log[l])]),
            _lane_row([(SM_DT, ssm_dt_bias[l])]),
            _lane_row([(0, swa_sinks[l])]),
            jnp.zeros((SUBLANES - 5, LANES), F32)], axis=0)
        dn_nw = dn_norm_w[l][None, :]
        ssm_nw = ssm_norm_w[l][None, :]
        drow = jnp.repeat(ssm_d[l], SSM_HEADDIM)[None, :]
        cbias = ssm_conv_b[l][None, :]
        last = l == depth - 1

        proj = _inproj(xp, g1, w_in_r, l, pad_range, TM_DENSE)
        odn, dn_p, dnc_p = _dn_mixer(proj, nblk, blk0, BLOCK, CHUNK, FRONT_PAD, bp, zero_dn, 0,
                                     zero_dnc, dn_conv_w[l], prm, dn_nw)
        ys, ssm_p, ssmc_p = _ssd_mixer(proj, nblk, blk0, BLOCK, CHUNK, FRONT_PAD, bp, zero_ssm, 0,
                                       zero_ssmc, ssm_conv_w[l], cbias, prm, drow, ssm_nw)
        osw, k_p, v_p = _swa_prompt(proj, nblk, blk0, FRONT_PAD, cos_p, sin_p, prm)
        xp = _tail(xp, odn, ys, osw, w_out_b, g2, g3, g4, w_fi_b, w_fo_b, l, seq if last else lp, TM_DENSE)
        new_p.append((dn_p, dnc_p[:, -(CONV_WIDTH - 1):], ssm_p.reshape(bp, SSM_HEADS, SSM_HEADDIM, SSM_STATE),
                      ssmc_p[:, -(CONV_WIDTH - 1):], k_p.reshape(bp, WINDOW, SWA_KV_HEADS, SWA_HEAD_DIM),
                      v_p.reshape(bp, WINDOW, SWA_KV_HEADS, SWA_HEAD_DIM)))

        proj = _inproj(xs, g1, w_in_r, l, None, TM_DENSE).reshape(bs, ts, D_PROJ)
        odn, dn_s, dnc_s = _dn_mixer(proj, 1, 0, ts, ts, 0, nseq_s, state_dn, l,
                                     _pad_conv_state(state_dn_conv[l]), dn_conv_w[l], prm, dn_nw,
                                     stack=(depth, dn_s))
        ys, ssm_s, ssmc_s = _ssd_mixer(proj, 1, 0, ts, ts, 0, nseq_s, state_ssm_g, l,
                                       _pad_conv_state(state_ssm_conv[l]), ssm_conv_w[l], cbias, prm, drow, ssm_nw,
                                       stack=(depth, ssm_s))
        osw, k_s, v_s = _swa_sample(proj, nseq_s, cache_k, cache_v, l, cos_s, sin_s, prm,
                                    stack_k=(depth, k_s), stack_v=(depth, v_s))
        flat = lambda a: a.reshape(1, bs * ts, a.shape[-1])
        xs = _tail(xs, flat(odn), flat(ys), flat(osw), w_out_b, g2, g3, g4, w_fi_b, w_fo_b, l, bs * ts, TM_DENSE)
        new_s.append((dnc_s[:, -(CONV_WIDTH - 1):], ssmc_s[:, -(CONV_WIDTH - 1):]))

    outs_p = tuple(jnp.stack([st[i] for st in new_p]) for i in range(6))
    dnc_s, ssmc_s = (jnp.stack([st[i] for st in new_s]) for i in range(2))
    outs_s = (dn_s, dnc_s, ssm_s.reshape(depth, bs, SSM_HEADS, SSM_HEADDIM, SSM_STATE), ssmc_s,
              k_s.reshape(depth, bs, WINDOW, SWA_KV_HEADS, SWA_HEAD_DIM),
              v_s.reshape(depth, bs, WINDOW, SWA_KV_HEADS, SWA_HEAD_DIM))
    return (xp, xs.reshape(bs, ts, d)) + outs_p + outs_s
```

```python
import functools

import jax
import jax.numpy as jnp
from jax import lax
from jax.experimental import pallas as pl
from jax.experimental.pallas import tpu as pltpu

F32 = jnp.float32
BF16 = jnp.bfloat16
HI = lax.Precision.HIGHEST
NT = (((1,), (1,)), ((), ()))
TN = (((0,), (0,)), ((), ()))

N_META = 16
CONV_WIDTH = 4
CHUNK = 64
BLOCK = 128
WINDOW = 128
FRONT_PAD = BLOCK - N_META
ROPE_THETA = 10000.0
PAST_LEN = 8192
EPS = 1e-6

DN_HEADS, DN_DK, DN_DV = 4, 128, 128
DN_QK = DN_HEADS * DN_DK
DN_V = DN_HEADS * DN_DV
DN_CONV = 2 * DN_QK + DN_V
SSM_HEADS, SSM_HEADDIM, SSM_GROUPS, SSM_STATE = 4, 64, 2, 128
SSM_INNER = SSM_HEADS * SSM_HEADDIM
SSM_BC = SSM_GROUPS * SSM_STATE
SSM_CONV = SSM_INNER + 2 * SSM_BC
SWA_Q_HEADS, SWA_KV_HEADS, SWA_HEAD_DIM = 4, 2, 64
SWA_Q = SWA_Q_HEADS * SWA_HEAD_DIM
SWA_KV = SWA_KV_HEADS * SWA_HEAD_DIM
IN_WIDTHS = (DN_CONV, DN_V, DN_HEADS, DN_HEADS, SSM_CONV, SSM_INNER, SSM_HEADS, SWA_Q, SWA_KV, SWA_KV)

LANES = 128
SUBLANES = 8
COL_QKV = (0, DN_CONV)
COL_DNZ = (1536, DN_V)
COL_SSZ = (2048, SSM_INNER)
COL_XBC = (2304, SSM_CONV)
COL_SWQ = (3072, SWA_Q)
COL_SWK = (3328, SWA_KV)
COL_SWV = (3456, SWA_KV)
COL_SM = (3584, LANES)
D_PROJ = 3712
SM_B, SM_A, SM_DT = 0, 4, 8
PRM_DN_ALOG, PRM_DN_DTB, PRM_SSM_ALOG, PRM_SSM_DTB, PRM_SINK = 0, 1, 2, 3, 4
NEG_BIG = -1e30
VMEM_LIMIT = 56 * 1024 * 1024
TM_DENSE = 640


def _bdot(a, b, dims=None):
    a = a.astype(BF16)
    b = b.astype(BF16)
    if dims is None:
        return jnp.dot(a, b, preferred_element_type=F32)
    return lax.dot_general(a, b, dims, preferred_element_type=F32)


def _hdot(a, b):
    return jnp.dot(a, b, precision=HI, preferred_element_type=F32)


def _rmsnorm(x, g):
    return x * lax.rsqrt(jnp.mean(x * x, axis=-1, keepdims=True) + EPS) * g


def _l2norm(x):
    return x * lax.rsqrt(jnp.sum(x * x, axis=-1, keepdims=True) + EPS)


def _sigmoid(x):
    return 1.0 / (1.0 + jnp.exp(-x))


def _silu(x):
    return x * _sigmoid(x)


def _softplus(x):
    return jnp.maximum(x, 0.0) + jnp.log1p(jnp.exp(-jnp.abs(x)))


def _pick_tile(n, target):
    best = None
    for t in range(SUBLANES, min(n, target) + 1, SUBLANES):
        if n % t == 0:
            best = t
    assert best is not None, n
    return best


def _inproj_kernel(x_ref, g_ref, w_ref, o_ref, *, tm, pad_range):
    h = _rmsnorm(x_ref[...], g_ref[...])
    if pad_range is not None:
        r = pl.program_id(1) * tm + lax.broadcasted_iota(jnp.int32, (tm, 1), 0)
        is_pad = (r >= pad_range[0]) & (r < pad_range[1])
        h = jnp.where(is_pad, 0.0, h)
    o_ref[...] = jnp.dot(h.astype(BF16), w_ref[...], preferred_element_type=F32)


def _inproj(x, g, w, layer, pad_range, tm_target):
    nb, rows, d = x.shape
    tm = _pick_tile(rows, tm_target)
    return pl.pallas_call(
        functools.partial(_inproj_kernel, tm=tm, pad_range=pad_range),
        out_shape=jax.ShapeDtypeStruct((nb, rows, D_PROJ), F32),
        grid=(nb, rows // tm),
        in_specs=[pl.BlockSpec((None, tm, d), lambda b, i: (b, i, 0)), _layer_spec(g.shape, layer),
                  _layer_spec(w.shape, layer)],
        out_specs=pl.BlockSpec((None, tm, D_PROJ), lambda b, i: (b, i, 0)),
        compiler_params=pltpu.CompilerParams(dimension_semantics=("arbitrary", "arbitrary"),
                                             vmem_limit_bytes=VMEM_LIMIT),
        name="inproj",
    )(x, g, w)


def _causal_conv(xbuf, raw_ref, cw_ref, cbo_ref, rows):
    x = raw_ref[...]
    prev = xbuf[0:SUBLANES, :]
    cw = cw_ref[...]
    row = lax.broadcasted_iota(jnp.int32, (SUBLANES, 1), 0)
    acc = None
    for i in range(CONV_WIDTH):
        s = CONV_WIDTH - 1 - i
        if s == 0:
            xs = x
        else:
            r = pltpu.roll(x, s, axis=0)
            head = jnp.where(row < s, pltpu.roll(prev, s, axis=0), r[0:SUBLANES])
            xs = head if rows == SUBLANES else jnp.concatenate([head, r[SUBLANES:]], axis=0)
        term = xs * cw[i:i + 1, :]
        acc = term if acc is None else acc + term
    tail = x[rows - SUBLANES:rows]
    cbo_ref[...] = tail
    xbuf[0:SUBLANES, :] = tail
    return acc


def _chunk_masks(c):
    ii = lax.broadcasted_iota(jnp.int32, (c, c), 0)
    jj = lax.broadcasted_iota(jnp.int32, (c, c), 1)
    return ii >= jj, ii > jj, (ii == jj).astype(F32)


def _segment_decay(gc, gct, lane, ge):
    col = gc[:, lane:lane + 1]
    row = gct[lane:lane + 1, :]
    return jnp.where(ge, jnp.exp(jnp.where(ge, col - row, 0.0)), 0.0)


def _inv_unit_lower_minus_eye(a_list, c, nh, bd_ref=None):
    w = nh * c
    if bd_ref is None:
        blk_r = lax.broadcasted_iota(jnp.int32, (w, w), 0) // c
        blk_c = lax.broadcasted_iota(jnp.int32, (w, w), 1) // c
        same = blk_r == blk_c

    def block_diag(p, i):
        if bd_ref is None:
            return jnp.where(same, jnp.concatenate([p] * nh, axis=0), 0.0).astype(BF16)
        pb = p.astype(BF16)
        for h in range(nh):
            bd_ref[i, h * c:(h + 1) * c, h * c:(h + 1) * c] = pb[:, h * c:(h + 1) * c]
        return bd_ref[i]

    ys = [-a for a in a_list]
    ps = [_bdot(a, block_diag(a, i)) for i, a in enumerate(a_list)]
    yield
    n = 2
    while n < c:
        pbds = [block_diag(p, i) for i, p in enumerate(ps)]
        n *= 2
        if n < c:
            sts = [_bdot(jnp.concatenate([y, p], axis=0), pbd) for y, p, pbd in zip(ys, ps, pbds)]
            ys = [y + p + st[:c] for y, p, st in zip(ys, ps, sts)]
            ps = [st[c:] for st in sts]
        else:
            ys = [y + p + _bdot(y, pbd) for y, p, pbd in zip(ys, ps, pbds)]
        yield
    return ys


def _run_parts(parts, name):
    grid = parts[0]["grid"]
    assert all(p["grid"] == grid for p in parts)
    n_in = [len(p["inputs"]) for p in parts]
    n_out = [len(p["out_shape"]) for p in parts]
    n_scr = [len(p["scratch"]) for p in parts]
    aliases = {}
    for k, p in enumerate(parts):
        for i, o in p["aliases"].items():
            aliases[sum(n_in[:k]) + i] = sum(n_out[:k]) + o

    def kernel(*refs):
        ins = refs[:sum(n_in)]
        outs = refs[sum(n_in):sum(n_in) + sum(n_out)]
        scr = refs[sum(n_in) + sum(n_out):]
        gens = [p["body"](ins[sum(n_in[:k]):sum(n_in[:k + 1])], outs[sum(n_out[:k]):sum(n_out[:k + 1])],
                          scr[sum(n_scr[:k]):sum(n_scr[:k + 1])]) for k, p in enumerate(parts)]
        for tag in gens[0]:
            if tag == "chain":
                break
        live = list(gens)
        while live:
            for g in list(live):
                if next(g, StopIteration) is StopIteration:
                    live.remove(g)

    results = pl.pallas_call(
        kernel,
        out_shape=tuple(s for p in parts for s in p["out_shape"]),
        grid=grid,
        in_specs=[s for p in parts for s in p["in_specs"]],
        out_specs=tuple(s for p in parts for s in p["out_specs"]),
        scratch_shapes=[s for p in parts for s in p["scratch"]],
        input_output_aliases=aliases,
        compiler_params=pltpu.CompilerParams(dimension_semantics=("arbitrary",) * len(grid),
                                             vmem_limit_bytes=VMEM_LIMIT),
        name=name,
    )(*[a for p in parts for a in p["inputs"]])
    return [list(results[sum(n_out[:k]):sum(n_out[:k + 1])]) for k in range(len(parts))]


def _state_out(stack, layer, nb, nseq, tail, out_index, n_inputs):
    zeros = (0,) * len(tail)
    if stack is None:
        return ((nb,) + tail, pl.BlockSpec((nseq,) + tail, lambda b, *_: (b,) + zeros), [], [], {}, None, 0)
    depth, prev = stack
    shape = (depth, nb) + tail
    if prev is None:
        spec = pl.BlockSpec((depth, nseq) + tail, lambda b, *_: (0, b) + zeros)
        return (shape, spec, [], [], {}, layer, depth)
    spec = pl.BlockSpec((None, nseq) + tail, lambda b, *_: (layer, b) + zeros)
    return (shape, spec, [prev], [pl.BlockSpec(memory_space=pl.ANY)], {n_inputs: out_index}, None, 0)


def _state_view(ref, slot, n_slots):
    if slot is None:
        return ref
    for other in range(n_slots):
        if other != slot:
            ref[other] = jnp.zeros(ref.shape[1:], ref.dtype)
    return ref.at[slot]


def _dn_prep(items, lmat, ge, gt_all, chunk, bd_ref):
    heads = range(DN_HEADS)
    gcs = [_hdot(lmat, g_all) for _, _, g_all in items]
    gcts = [gc.T for gc in gcs]
    qs = [[_l2norm(qkv[:, h * DN_DK:(h + 1) * DN_DK]) * DN_DK ** -0.5 for h in heads] for qkv, _, _ in items]
    ks = [[_l2norm(qkv[:, DN_QK + h * DN_DK:DN_QK + (h + 1) * DN_DK]) for h in heads] for qkv, _, _ in items]
    vs = [[qkv[:, 2 * DN_QK + h * DN_DV:2 * DN_QK + (h + 1) * DN_DV] for h in heads] for qkv, _, _ in items]
    betas = [[beta_all[:, SM_B + h:SM_B + h + 1] for h in heads] for _, beta_all, _ in items]
    yield
    kbs = [[k.astype(BF16) for k in kk] for kk in ks]
    kks = [jnp.concatenate([_bdot(kb, kb, NT) for kb in kb4], axis=1) for kb4 in kbs]
    qks = [jnp.concatenate([_bdot(q, kb, NT) for q, kb in zip(q4, kb4)], axis=1) for q4, kb4 in zip(qs, kbs)]
    decs = [jnp.concatenate([_segment_decay(gc, gct, SM_A + h, ge) for h in heads], axis=1)
            for gc, gct in zip(gcs, gcts)]
    beta_ws = [jnp.concatenate([jnp.broadcast_to(b, (chunk, chunk)) for b in b4], axis=1) for b4 in betas]
    a_list = [jnp.where(gt_all, bw * kk * dec, 0.0) for bw, kk, dec in zip(beta_ws, kks, decs)]
    yield "chain"
    tms = yield from _inv_unit_lower_minus_eye(a_list, chunk, DN_HEADS, bd_ref)
    attns = [(qk * dec).astype(BF16) for qk, dec in zip(qks, decs)]
    egs = [jnp.exp(gc) for gc in gcs]
    rhss = [[jnp.concatenate([vs[i][h] * betas[i][h], ks[i][h] * (betas[i][h] * egs[i][:, SM_A + h:SM_A + h + 1])],
                             axis=1) for h in heads] for i in range(len(items))]
    uws = [[rhss[i][h] + _bdot(tms[i][:, h * chunk:(h + 1) * chunk], rhss[i][h]) for h in heads]
           for i in range(len(items))]
    yield
    out = []
    for i, gc in enumerate(gcs):
        glast = gc[chunk - 1:chunk, :]
        ekd = jnp.exp(glast - gc)
        egl = jnp.exp(glast)
        per_head = []
        for h in heads:
            uw = uws[i][h]
            wq = jnp.concatenate([uw[:, DN_DV:], qs[i][h] * egs[i][:, SM_A + h:SM_A + h + 1]], axis=0).astype(BF16)
            kd = (ks[i][h] * ekd[:, SM_A + h:SM_A + h + 1]).astype(BF16)
            per_head.append((uw[:, :DN_DV], wq, attns[i][:, h * chunk:(h + 1) * chunk], kd,
                             egl[:, SM_A + h:SM_A + h + 1]))
        out.append(per_head)
    return out


def _dn_body(ins, outs, scratch, *, nseq, rows, chunk, front_pad, slot, n_slots):
    qkv_ref, z_ref, sm_ref, s0_ref, cb_ref, cw_ref, prm_ref, nw_ref = ins[:8]
    o_ref, so_full, cbo_ref = outs
    xbuf, bd_ref = scratch if len(scratch) == 2 else (scratch[0], None)
    so_ref = _state_view(so_full, slot, n_slots)
    blk = pl.program_id(1)
    first = blk == 0

    @pl.when(first)
    def _():
        so_ref[...] = s0_ref[...]
        xbuf[:, 0:SUBLANES, :] = cb_ref[...]
        if bd_ref is not None:
            bd_ref[...] = jnp.zeros_like(bd_ref)

    prm = prm_ref[...]
    nw = nw_ref[...]
    ge, _, _ = _chunk_masks(chunk)
    lmat = ge.astype(F32)
    wide = (chunk, DN_HEADS * chunk)
    gt_all = lax.broadcasted_iota(jnp.int32, wide, 0) > lax.broadcasted_iota(jnp.int32, wide, 1) % chunk
    nchunk = rows // chunk

    items = []
    for s in range(nseq):
        qkv = _silu(_causal_conv(xbuf.at[s], qkv_ref.at[s], cw_ref, cbo_ref.at[s], rows))
        sm = sm_ref[s]
        beta_all = _sigmoid(sm)
        g_all = -jnp.exp(prm[PRM_DN_ALOG:PRM_DN_ALOG + 1, :]) * _softplus(sm + prm[PRM_DN_DTB:PRM_DN_DTB + 1, :])
        if front_pad:
            pos = blk * rows + lax.broadcasted_iota(jnp.int32, (rows, 1), 0)
            g_all = jnp.where(pos < front_pad, 0.0, g_all)
        for c in range(nchunk):
            cs = slice(c * chunk, (c + 1) * chunk)
            items.append((qkv[cs], beta_all[cs], g_all[cs]))
        yield
    prep = yield from _dn_prep(items, lmat, ge, gt_all, chunk, bd_ref)

    chains = [(s, h) for s in range(nseq) for h in range(DN_HEADS)]
    states = [so_ref[s, h] for s, h in chains]
    for c in range(nchunk):
        r0 = c * chunk
        fac = [prep[s * nchunk + c][h] for s, h in chains]
        m1s = [_bdot(f[1], st) for f, st in zip(fac, states)]
        yield
        v_news = [(f[0] - m1[:chunk]).astype(BF16) for f, m1 in zip(fac, m1s)]
        ups = [_bdot(f[3], v, TN) for f, v in zip(fac, v_news)]
        os_ = [m1[chunk:] + _bdot(f[2], v) for f, m1, v in zip(fac, m1s, v_news)]
        yield
        states = [st * f[4] + up for f, st, up in zip(fac, states, ups)]
        for (s, h), o in zip(chains, os_):
            lo = h * DN_DV
            zh = z_ref[s, r0:r0 + chunk, lo:lo + DN_DV]
            o_ref[s, r0:r0 + chunk, lo:lo + DN_DV] = _rmsnorm(o, nw) * _silu(zh)
        yield
    for (s, h), st in zip(chains, states):
        so_ref[s, h] = st


def _dn_mixer(proj, nblk, blk0, rows, chunk, front_pad, nseq, s0, layer, cbuf, cw, prm, nw, stack=None):
    nb = proj.shape[0]
    col = lambda c: c[0] // c[1]
    phys = lambda i: (i + blk0) % nblk
    so_shape, so_spec, extra, extra_specs, aliases, slot, n_slots = _state_out(
        stack, layer, nb, nseq, (DN_HEADS, DN_DK, DN_DV), 1, 8)
    return dict(
        body=functools.partial(_dn_body, nseq=nseq, rows=rows, chunk=chunk, front_pad=front_pad,
                               slot=slot, n_slots=n_slots),
        grid=(nb // nseq, nblk),
        inputs=[proj, proj, proj, s0, cbuf, cw, prm, nw] + extra,
        out_shape=[jax.ShapeDtypeStruct((nb, nblk * rows, DN_V), F32),
                   jax.ShapeDtypeStruct(so_shape, F32),
                   jax.ShapeDtypeStruct((nb, SUBLANES, DN_CONV), F32)],
        in_specs=[
            pl.BlockSpec((nseq, rows, DN_CONV), lambda b, i: (b, phys(i), col(COL_QKV))),
            pl.BlockSpec((nseq, rows, DN_V), lambda b, i: (b, phys(i), col(COL_DNZ))),
            pl.BlockSpec((nseq, rows, LANES), lambda b, i: (b, phys(i), col(COL_SM))),
            pl.BlockSpec((None, nseq, DN_HEADS, DN_DK, DN_DV), lambda b, i: (layer, b, 0, 0, 0)),
            pl.BlockSpec((nseq, SUBLANES, DN_CONV), lambda b, i: (b, 0, 0)),
            pl.BlockSpec((CONV_WIDTH, DN_CONV), lambda b, i: (0, 0)),
            pl.BlockSpec((SUBLANES, LANES), lambda b, i: (0, 0)),
            pl.BlockSpec((1, DN_DV), lambda b, i: (0, 0)),
        ] + extra_specs,
        out_specs=[
            pl.BlockSpec((nseq, rows, DN_V), lambda b, i: (b, phys(i), 0)),
            so_spec,
            pl.BlockSpec((nseq, SUBLANES, DN_CONV), lambda b, i: (b, 0, 0)),
        ],
        scratch=[pltpu.VMEM((nseq, SUBLANES, DN_CONV), F32)] + (
            [pltpu.VMEM((nseq * (rows // chunk), DN_HEADS * chunk, DN_HEADS * chunk), BF16)]
            if chunk % (2 * SUBLANES) == 0 else []),
        aliases=aliases,
    )


def _ssd_body(ins, outs, scratch, *, nseq, rows, chunk, front_pad, slot, n_slots):
    xbc_ref, z_ref, sm_ref, h0_ref, cb_ref, cw_ref, cbias_ref, prm_ref, drow_ref, nw_ref = ins[:10]
    y_ref, ho_full, cbo_ref = outs
    xbuf, = scratch
    ho_ref = _state_view(ho_full, slot, n_slots)
    blk = pl.program_id(1)

    @pl.when(blk == 0)
    def _():
        ho_ref[...] = h0_ref[...]
        xbuf[:, 0:SUBLANES, :] = cb_ref[...]

    prm = prm_ref[...]
    nw = nw_ref[...]
    drow = drow_ref[...]
    cbias = cbias_ref[...]
    ge, _, _ = _chunk_masks(chunk)
    lmat = ge.astype(F32)
    hpg = SSM_HEADS // SSM_GROUPS
    gw = hpg * SSM_HEADDIM
    lane = lax.broadcasted_iota(jnp.int32, (1, gw), 1)
    srow = lax.broadcasted_iota(jnp.int32, (gw, 1), 0)
    in_head = [(lane >= j * SSM_HEADDIM) & (lane < (j + 1) * SSM_HEADDIM) for j in range(hpg)]
    nchunk = rows // chunk
    groups = range(SSM_GROUPS)

    items = []
    for s in range(nseq):
        act = _silu(_causal_conv(xbuf.at[s], xbc_ref.at[s], cw_ref, cbo_ref.at[s], rows) + cbias)
        dt_all = _softplus(sm_ref[s] + prm[PRM_SSM_DTB:PRM_SSM_DTB + 1, :])
        if front_pad:
            pos = blk * rows + lax.broadcasted_iota(jnp.int32, (rows, 1), 0)
            dt_all = jnp.where(pos < front_pad, 0.0, dt_all)
        g_all = dt_all * (-jnp.exp(prm[PRM_SSM_ALOG:PRM_SSM_ALOG + 1, :]))
        for c in range(nchunk):
            cs = slice(c * chunk, (c + 1) * chunk)
            items.append((act[cs], dt_all[cs], g_all[cs]))
        yield
    n_items = len(items)
    gcs = [_hdot(lmat, g) for _, _, g in items]
    gcts = [gc.T for gc in gcs]
    xgs = [[a[:, g * gw:(g + 1) * gw] for g in groups] for a, _, _ in items]
    bgs = [[a[:, SSM_INNER + g * SSM_STATE:SSM_INNER + (g + 1) * SSM_STATE] for g in groups] for a, _, _ in items]
    cgs = [[a[:, SSM_INNER + SSM_BC + g * SSM_STATE:SSM_INNER + SSM_BC + (g + 1) * SSM_STATE] for g in groups]
           for a, _, _ in items]
    cbs = [[_bdot(cgs[i][g], bgs[i][g], NT) for g in groups] for i in range(n_items)]
    yield
    egs = [jnp.exp(gc) for gc in gcs]
    ekds = [jnp.exp(gc[chunk - 1:chunk, :] - gc) for gc in gcs]
    egls = [jnp.exp(gc[chunk - 1:chunk, :]) for gc in gcs]
    heads = [(g, j) for g in groups for j in range(hpg)]
    ln = lambda g, j: SM_DT + g * hpg + j
    xdts = [[jnp.where(in_head[j], xgs[i][g] * items[i][1][:, ln(g, j):ln(g, j) + 1], 0.0).astype(BF16)
             for g, j in heads] for i in range(n_items)]
    attns = [[cbs[i][g] * _segment_decay(gcs[i], gcts[i], ln(g, j), ge) for g, j in heads] for i in range(n_items)]
    y_intras = [[_bdot(attns[i][k], xdts[i][k]) for k in range(len(heads))] for i in range(n_items)]
    yield
    upds = [[_bdot(xdts[i][k], bgs[i][g] * ekds[i][:, ln(g, j):ln(g, j) + 1], TN) for k, (g, j) in enumerate(heads)]
            for i in range(n_items)]
    yield
    cds = [[jnp.concatenate([cgs[i][g] * egs[i][:, ln(g, j):ln(g, j) + 1] for j in range(hpg)], axis=0).astype(BF16)
            for g in groups] for i in range(n_items)]
    y_loc = [[sum(y_intras[i][g * hpg + j] for j in range(hpg)) for g in groups] for i in range(n_items)]
    h_inc = [[sum(upds[i][g * hpg + j] for j in range(hpg)) for g in groups] for i in range(n_items)]
    gl_cols = []
    for i in range(n_items):
        per_group = []
        for g in groups:
            gl = egls[i][:, ln(g, 0):ln(g, 0) + 1]
            for j in range(1, hpg):
                gl = jnp.where(srow < j * SSM_HEADDIM, gl, egls[i][:, ln(g, j):ln(g, j) + 1])
            per_group.append(gl)
        gl_cols.append(per_group)

    chains = [(s, g) for s in range(nseq) for g in groups]
    states = [ho_ref[s, g] for s, g in chains]
    for c in range(nchunk):
        r0 = c * chunk
        idx = [s * nchunk + c for s, _ in chains]
        yis = [_bdot(cds[i][g], st, NT) for i, (_, g), st in zip(idx, chains, states)]
        yield
        states = [st * gl_cols[i][g] + h_inc[i][g] for i, (_, g), st in zip(idx, chains, states)]
        for i, (s, g), yi in zip(idx, chains, yis):
            y_inter = yi[0:chunk]
            for j in range(1, hpg):
                y_inter = jnp.where(in_head[j], yi[j * chunk:(j + 1) * chunk], y_inter)
            yg = y_loc[i][g] + y_inter + xgs[i][g] * drow[:, g * gw:(g + 1) * gw]
            yg = yg * _silu(z_ref[s, r0:r0 + chunk, g * gw:(g + 1) * gw])
            y_ref[s, r0:r0 + chunk, g * gw:(g + 1) * gw] = _rmsnorm(yg, nw[:, g * gw:(g + 1) * gw])
        yield
    for (s, g), st in zip(chains, states):
        ho_ref[s, g] = st


def _ssd_mixer(proj, nblk, blk0, rows, chunk, front_pad, nseq, h0, layer, cbuf, cw, cbias, prm, drow, nw,
               stack=None):
    nb = proj.shape[0]
    col = lambda c: c[0] // c[1]
    phys = lambda i: (i + blk0) % nblk
    gw = (SSM_HEADS // SSM_GROUPS) * SSM_HEADDIM
    ho_shape, ho_spec, extra, extra_specs, aliases, slot, n_slots = _state_out(
        stack, layer, nb, nseq, (SSM_GROUPS, gw, SSM_STATE), 1, 10)
    return dict(
        body=functools.partial(_ssd_body, nseq=nseq, rows=rows, chunk=chunk, front_pad=front_pad,
                               slot=slot, n_slots=n_slots),
        grid=(nb // nseq, nblk),
        inputs=[proj, proj, proj, h0, cbuf, cw, cbias, prm, drow, nw] + extra,
        out_shape=[jax.ShapeDtypeStruct((nb, nblk * rows, SSM_INNER), F32),
                   jax.ShapeDtypeStruct(ho_shape, F32),
                   jax.ShapeDtypeStruct((nb, SUBLANES, SSM_CONV), F32)],
        in_specs=[
            pl.BlockSpec((nseq, rows, SSM_CONV), lambda b, i: (b, phys(i), col(COL_XBC))),
            pl.BlockSpec((nseq, rows, SSM_INNER), lambda b, i: (b, phys(i), col(COL_SSZ))),
            pl.BlockSpec((nseq, rows, LANES), lambda b, i: (b, phys(i), col(COL_SM))),
            pl.BlockSpec((None, nseq, SSM_GROUPS, gw, SSM_STATE), lambda b, i: (layer, b, 0, 0, 0)),
            pl.BlockSpec((nseq, SUBLANES, SSM_CONV), lambda b, i: (b, 0, 0)),
            pl.BlockSpec((CONV_WIDTH, SSM_CONV), lambda b, i: (0, 0)),
            pl.BlockSpec((1, SSM_CONV), lambda b, i: (0, 0)),
            pl.BlockSpec((SUBLANES, LANES), lambda b, i: (0, 0)),
            pl.BlockSpec((1, SSM_INNER), lambda b, i: (0, 0)),
            pl.BlockSpec((1, SSM_INNER), lambda b, i: (0, 0)),
        ] + extra_specs,
        out_specs=[
            pl.BlockSpec((nseq, rows, SSM_INNER), lambda b, i: (b, phys(i), 0)),
            ho_spec,
            pl.BlockSpec((nseq, SUBLANES, SSM_CONV), lambda b, i: (b, 0, 0)),
        ],
        scratch=[pltpu.VMEM((nseq, SUBLANES, SSM_CONV), F32)],
        aliases=aliases,
    )


def _rope(x, cos, sin_signed):
    w = x.shape[-1]
    half = SWA_HEAD_DIM // 2
    lane = lax.broadcasted_iota(jnp.int32, (1, w), 1)
    first_half = (lane % SWA_HEAD_DIM) < half
    swapped = jnp.where(first_half, pltpu.roll(x, w - half, axis=1), pltpu.roll(x, half, axis=1))
    return x * cos + swapped * sin_signed


def _sink_attend(problems):
    scale = SWA_HEAD_DIM ** -0.5
    scores = [[jnp.where(m, _bdot(q, k, NT) * scale, NEG_BIG) for k, m in zip(keys, masks)]
              for q, keys, _, masks, _ in problems]
    yield
    outs = []
    probs, dens = [], []
    def lane_reduce(tiles, combine, reduce):
        merged = {}
        for t in tiles:
            merged[t.shape[-1]] = t if t.shape[-1] not in merged else combine(merged[t.shape[-1]], t)
        return [reduce(t, axis=-1, keepdims=True) for t in merged.values()]

    for (_, _, _, _, sink), ss in zip(problems, scores):
        mx = sink
        for m in lane_reduce(ss, jnp.maximum, jnp.max):
            mx = jnp.maximum(mx, m)
        ps = [jnp.exp(s - mx) for s in ss]
        den = jnp.exp(sink - mx)
        for d in lane_reduce(ps, jnp.add, jnp.sum):
            den = den + d
        probs.append(ps)
        dens.append(den)
    yield
    pvs = [[_bdot(p, v) for p, v in zip(ps, vals)] for (_, _, vals, _, _), ps in zip(problems, probs)]
    yield
    for pv, den in zip(pvs, dens):
        acc = pv[0]
        for extra in pv[1:]:
            acc = acc + extra
        outs.append(acc / den)
    return outs


def _swa_problems(q, key_sets, val_sets, masks, prm, tq):
    grp = SWA_Q_HEADS // SWA_KV_HEADS
    row = lax.broadcasted_iota(jnp.int32, (grp * tq, 1), 0)
    problems = []
    for j in range(SWA_KV_HEADS):
        ks = slice(j * SWA_HEAD_DIM, (j + 1) * SWA_HEAD_DIM)
        heads = [j * grp + g for g in range(grp)]
        qst = jnp.concatenate([q[:, h * SWA_HEAD_DIM:(h + 1) * SWA_HEAD_DIM] for h in heads], axis=0)
        sink = prm[PRM_SINK:PRM_SINK + 1, heads[0]:heads[0] + 1]
        for g in range(1, grp):
            sink = jnp.where(row < g * tq, sink, prm[PRM_SINK:PRM_SINK + 1, heads[g]:heads[g] + 1])
        problems.append((qst, [k[:, ks] for k in key_sets], [v[:, ks] for v in val_sets], masks, sink))
    return problems


def _swa_store(o_ref_at, outs, tq):
    grp = SWA_Q_HEADS // SWA_KV_HEADS
    for j, o in enumerate(outs):
        for g in range(grp):
            h = j * grp + g
            o_ref_at(slice(h * SWA_HEAD_DIM, (h + 1) * SWA_HEAD_DIM), o[g * tq:(g + 1) * tq])


def _swa_prompt_body(ins, outs, scratch, *, nseq, front_pad):
    q_ref, k_ref, v_ref, cos_ref, sin_ref, prm_ref = ins
    o_ref, ko_ref, vo_ref = outs
    kprev, vprev = scratch
    blk = pl.program_id(1)

    @pl.when(blk == 0)
    def _():
        kprev[...] = jnp.zeros_like(kprev)
        vprev[...] = jnp.zeros_like(vprev)

    cos = cos_ref[...]
    sin = sin_ref[...]
    cos_q = jnp.concatenate([cos, cos], axis=1)
    sin_q = jnp.concatenate([sin, sin], axis=1)
    grp = SWA_Q_HEADS // SWA_KV_HEADS
    qi = lax.broadcasted_iota(jnp.int32, (grp * BLOCK, BLOCK), 0) % BLOCK
    kj = lax.broadcasted_iota(jnp.int32, (grp * BLOCK, BLOCK), 1)
    mask_cur = (kj <= qi) & (blk * BLOCK + kj >= front_pad)
    mask_prev = (kj > qi) & ((blk - 1) * BLOCK + kj >= front_pad)
    prm = prm_ref[...]
    problems = []
    for s in range(nseq):
        q = _rope(q_ref[s], cos_q, sin_q)
        k = _rope(k_ref[s], cos, sin)
        v = v_ref[s]
        problems += _swa_problems(q, (kprev[s], k), (vprev[s], v), (mask_prev, mask_cur), prm, BLOCK)
        kprev[s] = k
        vprev[s] = v
        ko_ref[s] = k
        vo_ref[s] = v
        yield
    res = yield from _sink_attend(problems)
    for s in range(nseq):
        def put(cols, val, s=s):
            o_ref[s, :, cols] = val
        _swa_store(put, res[s * SWA_KV_HEADS:(s + 1) * SWA_KV_HEADS], BLOCK)


def _swa_prompt(proj, nblk, blk0, front_pad, cos, sin, prm):
    nb = proj.shape[0]
    col = lambda c: c[0] // c[1]
    phys = lambda i: (i + blk0) % nblk
    return dict(
        body=functools.partial(_swa_prompt_body, nseq=nb, front_pad=front_pad),
        grid=(1, nblk),
        inputs=[proj, proj, proj, cos, sin, prm],
        out_shape=[jax.ShapeDtypeStruct((nb, nblk * BLOCK, SWA_Q), F32),
                   jax.ShapeDtypeStruct((nb, WINDOW, SWA_KV), F32),
                   jax.ShapeDtypeStruct((nb, WINDOW, SWA_KV), F32)],
        in_specs=[
            pl.BlockSpec((nb, BLOCK, SWA_Q), lambda b, i: (0, phys(i), col(COL_SWQ))),
            pl.BlockSpec((nb, BLOCK, SWA_KV), lambda b, i: (0, phys(i), col(COL_SWK))),
            pl.BlockSpec((nb, BLOCK, SWA_KV), lambda b, i: (0, phys(i), col(COL_SWV))),
            pl.BlockSpec((BLOCK, SWA_KV), lambda b, i: (i, 0)),
            pl.BlockSpec((BLOCK, SWA_KV), lambda b, i: (i, 0)),
            pl.BlockSpec((SUBLANES, LANES), lambda b, i: (0, 0)),
        ],
        out_specs=[
            pl.BlockSpec((nb, BLOCK, SWA_Q), lambda b, i: (0, phys(i), 0)),
            pl.BlockSpec((nb, WINDOW, SWA_KV), lambda b, i: (0, 0, 0)),
            pl.BlockSpec((nb, WINDOW, SWA_KV), lambda b, i: (0, 0, 0)),
        ],
        scratch=[pltpu.VMEM((nb, BLOCK, SWA_KV), F32), pltpu.VMEM((nb, BLOCK, SWA_KV), F32)],
        aliases={},
    )


def _swa_sample_body(ins, outs, scratch, *, nseq, steps, slot, n_slots):
    q_ref, k_ref, v_ref, kc_ref, vc_ref, cos_ref, sin_ref, prm_ref = ins[:8]
    o_ref, ko_full, vo_full = outs
    ko_ref = _state_view(ko_full, slot, n_slots)
    vo_ref = _state_view(vo_full, slot, n_slots)
    cos = cos_ref[...]
    sin = sin_ref[...]
    cos_q = jnp.concatenate([cos, cos], axis=1)
    sin_q = jnp.concatenate([sin, sin], axis=1)
    prm = prm_ref[...]
    grp = SWA_Q_HEADS // SWA_KV_HEADS
    ti = lax.broadcasted_iota(jnp.int32, (grp * steps, WINDOW), 0) % steps
    sj = lax.broadcasted_iota(jnp.int32, (grp * steps, WINDOW), 1)
    mask_cache = sj > ti
    tn = lax.broadcasted_iota(jnp.int32, (grp * steps, steps), 0) % steps
    sn = lax.broadcasted_iota(jnp.int32, (grp * steps, steps), 1)
    mask_new = sn <= tn
    problems = []
    for b in range(nseq):
        q = _rope(q_ref[b], cos_q, sin_q)
        k = _rope(k_ref[b], cos, sin)
        v = v_ref[b]
        kc = kc_ref[b]
        vc = vc_ref[b]
        ko_ref[b, 0:WINDOW - steps, :] = kc[steps:WINDOW, :]
        ko_ref[b, WINDOW - steps:WINDOW, :] = k
        vo_ref[b, 0:WINDOW - steps, :] = vc[steps:WINDOW, :]
        vo_ref[b, WINDOW - steps:WINDOW, :] = v
        problems += _swa_problems(q, (kc, k), (vc, v), (mask_cache, mask_new), prm, steps)
        yield
    res = yield from _sink_attend(problems)
    for b in range(nseq):
        def put(cols, val, b=b):
            o_ref[b, :, cols] = val
        _swa_store(put, res[b * SWA_KV_HEADS:(b + 1) * SWA_KV_HEADS], steps)


def _swa_sample(proj, nseq, kc, vc, layer, cos, sin, prm, stack_k=None, stack_v=None):
    nb, steps, _ = proj.shape
    assert WINDOW > steps
    col = lambda c: c[0] // c[1]
    ko_shape, ko_spec, extra_k, specs_k, alias_k, slot, n_slots = _state_out(
        stack_k, layer, nb, nseq, (WINDOW, SWA_KV), 1, 8)
    vo_shape, vo_spec, extra_v, specs_v, alias_v, _, _ = _state_out(
        stack_v, layer, nb, nseq, (WINDOW, SWA_KV), 2, 8 + len(extra_k))
    return dict(
        body=functools.partial(_swa_sample_body, nseq=nseq, steps=steps, slot=slot, n_slots=n_slots),
        grid=(nb // nseq, 1),
        inputs=[proj, proj, proj, kc, vc, cos, sin, prm] + extra_k + extra_v,
        out_shape=[jax.ShapeDtypeStruct((nb, steps, SWA_Q), F32),
                   jax.ShapeDtypeStruct(ko_shape, F32),
                   jax.ShapeDtypeStruct(vo_shape, F32)],
        in_specs=[
            pl.BlockSpec((nseq, steps, SWA_Q), lambda b, i: (b, 0, col(COL_SWQ))),
            pl.BlockSpec((nseq, steps, SWA_KV), lambda b, i: (b, 0, col(COL_SWK))),
            pl.BlockSpec((nseq, steps, SWA_KV), lambda b, i: (b, 0, col(COL_SWV))),
            pl.BlockSpec((None, nseq, WINDOW, SWA_KV), lambda b, i: (layer, b, 0, 0)),
            pl.BlockSpec((None, nseq, WINDOW, SWA_KV), lambda b, i: (layer, b, 0, 0)),
            pl.BlockSpec((steps, SWA_KV), lambda b, i: (0, 0)),
            pl.BlockSpec((steps, SWA_KV), lambda b, i: (0, 0)),
            pl.BlockSpec((SUBLANES, LANES), lambda b, i: (0, 0)),
        ] + specs_k + specs_v,
        out_specs=[pl.BlockSpec((nseq, steps, SWA_Q), lambda b, i: (b, 0, 0)), ko_spec, vo_spec],
        scratch=[],
        aliases={**alias_k, **alias_v},
    )


def _tail_kernel(x_ref, odn_ref, y_ref, osw_ref, wout_ref, g1_ref, g2_ref, g3_ref, wfi_ref, wfo_ref, o_ref,
                 *, d_ff, tf):
    mixed = jnp.concatenate([odn_ref[...], y_ref[...], osw_ref[...]], axis=1).astype(BF16)
    m = jnp.dot(mixed, wout_ref[...], preferred_element_type=F32)
    x1 = x_ref[...] + _rmsnorm(m, g1_ref[...])
    h = _rmsnorm(x1, g2_ref[...]).astype(BF16)
    y2 = None
    for c in range(d_ff // tf):
        gate = jnp.dot(h, wfi_ref[:, c * tf:(c + 1) * tf], preferred_element_type=F32)
        up = jnp.dot(h, wfi_ref[:, d_ff + c * tf:d_ff + (c + 1) * tf], preferred_element_type=F32)
        part = jnp.dot((_silu(gate) * up).astype(BF16), wfo_ref[c * tf:(c + 1) * tf, :], preferred_element_type=F32)
        y2 = part if y2 is None else y2 + part
    o_ref[...] = x1 + _rmsnorm(y2, g3_ref[...])


def _layer_spec(shape, layer):
    nd = len(shape) - 1
    return pl.BlockSpec((None,) + tuple(shape[1:]), lambda *_: (layer,) + (0,) * nd, pipeline_mode=pl.Buffered(1))


def _tail(x, odn, y, osw, wout, g1, g2, g3, wfi, wfo, layer, l_out, tm_target):
    nb, _, d = x.shape
    d_ff = wfo.shape[1]
    tm = _pick_tile(l_out, tm_target)
    tf = 2 * LANES if d_ff % (2 * LANES) == 0 else d_ff
    row = lambda w: pl.BlockSpec((None, tm, w), lambda b, i: (b, i, 0))
    return pl.pallas_call(
        functools.partial(_tail_kernel, d_ff=d_ff, tf=tf),
        out_shape=jax.ShapeDtypeStruct((nb, l_out, d), F32),
        grid=(nb, l_out // tm),
        in_specs=[row(d), row(DN_V), row(SSM_INNER), row(SWA_Q), _layer_spec(wout.shape, layer),
                  _layer_spec(g1.shape, layer), _layer_spec(g2.shape, layer), _layer_spec(g3.shape, layer),
                  _layer_spec(wfi.shape, layer), _layer_spec(wfo.shape, layer)],
        out_specs=row(d),
        compiler_params=pltpu.CompilerParams(dimension_semantics=("arbitrary", "arbitrary"),
                                             vmem_limit_bytes=VMEM_LIMIT),
        name="outproj_ffn",
    )(x, odn, y, osw, wout, g1, g2, g3, wfi, wfo)


def _reorder_w_in_kernel(w_ref, o_ref):
    w = w_ref[0]
    offs = [0]
    for wd in IN_WIDTHS:
        offs.append(offs[-1] + wd)
    seg = lambda i: w[:, offs[i]:offs[i + 1]]
    dn_qkv, dn_z, dn_b, dn_a, ssm_xbc, ssm_z, ssm_dt, sw_q, sw_k, sw_v = (seg(i) for i in range(len(IN_WIDTHS)))
    n_small = dn_b.shape[1] + dn_a.shape[1] + ssm_dt.shape[1]
    small = jnp.concatenate([dn_b, dn_a, ssm_dt, jnp.zeros((w.shape[0], LANES - n_small), w.dtype)], axis=1)
    o_ref[0] = jnp.concatenate([dn_qkv, dn_z, ssm_z, ssm_xbc, sw_q, sw_k, sw_v, small], axis=1)


def _reorder_w_in(w):
    depth, d, d_in = w.shape
    assert d_in == sum(IN_WIDTHS)
    tr = _pick_tile(d, 256)
    return pl.pallas_call(
        _reorder_w_in_kernel,
        out_shape=jax.ShapeDtypeStruct((depth, d, D_PROJ), BF16),
        grid=(depth, d // tr),
        in_specs=[pl.BlockSpec((1, tr, d_in), lambda l, i: (l, i, 0))],
        out_specs=pl.BlockSpec((1, tr, D_PROJ), lambda l, i: (l, i, 0)),
        compiler_params=pltpu.CompilerParams(dimension_semantics=("arbitrary", "arbitrary")),
        name="reorder_w_in",
    )(w)


def _lane_row(pairs):
    row = jnp.zeros((LANES,), F32)
    for off, vec in pairs:
        row = row.at[off:off + vec.shape[0]].set(vec.astype(F32))
    return row[None, :]


def _rope_tables(pos):
    half = SWA_HEAD_DIM // 2
    inv = ROPE_THETA ** (-jnp.arange(half, dtype=F32) / half)
    ang = pos.astype(F32)[:, None] * inv[None, :]
    cos = jnp.cos(ang)
    sin = jnp.sin(ang)
    cos_t = jnp.concatenate([cos, cos] * SWA_KV_HEADS, axis=1)
    sin_t = jnp.concatenate([-sin, sin] * SWA_KV_HEADS, axis=1)
    return cos_t, sin_t


def _pad_conv_state(buf):
    return jnp.pad(buf, ((0, 0), (SUBLANES - (CONV_WIDTH - 1), 0), (0, 0)))


def kernel(x_prompt, x_sample, state_dn, state_dn_conv, state_ssm, state_ssm_conv, cache_swa_k, cache_swa_v,
           meta_tokens, w_in, dn_conv_w, dn_a_log, dn_dt_bias, dn_norm_w, ssm_conv_w, ssm_conv_b, ssm_a_log,
           ssm_dt_bias, ssm_d, ssm_norm_w, swa_sinks, w_out, g_pre_mix, g_post_mix, g_pre_ffn, g_post_ffn,
           w_ffn_in, w_ffn_out):
    bp, seq, d = x_prompt.shape
    bs, ts, _ = x_sample.shape
    depth = w_in.shape[0]
    lp = N_META + seq + FRONT_PAD
    assert lp % BLOCK == 0 and BLOCK % CHUNK == 0 and seq % BLOCK == 0
    nblk = lp // BLOCK
    blk0 = nblk - 1
    pad_range = (seq, seq + FRONT_PAD)

    zpad = jnp.zeros((bp, FRONT_PAD, d), x_prompt.dtype)
    meta = jnp.broadcast_to(meta_tokens.astype(x_prompt.dtype)[None], (bp, N_META, d))
    xp = jnp.concatenate([x_prompt, zpad, meta], axis=1)
    xs = x_sample.reshape(1, bs * ts, d)

    cos_p, sin_p = _rope_tables(jnp.arange(lp, dtype=jnp.int32) - FRONT_PAD)
    cos_s, sin_s = _rope_tables(PAST_LEN + jnp.arange(ts, dtype=jnp.int32))

    gw = (SSM_HEADS // SSM_GROUPS) * SSM_HEADDIM
    nseq_s = _pick_tile(bs, 8) if bs % SUBLANES == 0 else bs
    zero_dn = jnp.zeros((1, bp, DN_HEADS, DN_DK, DN_DV), F32)
    zero_dnc = jnp.zeros((bp, SUBLANES, DN_CONV), F32)
    zero_ssm = jnp.zeros((1, bp, SSM_GROUPS, gw, SSM_STATE), F32)
    zero_ssmc = jnp.zeros((bp, SUBLANES, SSM_CONV), F32)
    state_ssm_g = state_ssm.reshape(depth, bs, SSM_GROUPS, gw, SSM_STATE)
    cache_k = cache_swa_k.reshape(depth, bs, WINDOW, SWA_KV)
    cache_v = cache_swa_v.reshape(depth, bs, WINDOW, SWA_KV)

    w_in_r = _reorder_w_in(w_in.astype(BF16))
    w_out_b = w_out.astype(BF16)
    w_fi_b = w_ffn_in.astype(BF16)
    w_fo_b = w_ffn_out.astype(BF16)
    g1, g2, g3, g4 = (a[:, None, :] for a in (g_pre_mix, g_post_mix, g_pre_ffn, g_post_ffn))

    new_p, new_s = [], []
    dn_s = ssm_s = k_s = v_s = None
    for l in range(depth):
        prm = jnp.concatenate([
            _lane_row([(SM_A, dn_a_log[l])]),
            _lane_row([(SM_A, dn_dt_bias[l])]),
            _lane_row([(SM_DT, ssm_a_log[l])]),
            _lane_row([(SM_DT, ssm_dt_bias[l])]),
            _lane_row([(0, swa_sinks[l])]),
            jnp.zeros((SUBLANES - 5, LANES), F32)], axis=0)
        dn_nw = dn_norm_w[l][None, :]
        ssm_nw = ssm_norm_w[l][None, :]
        drow = jnp.repeat(ssm_d[l], SSM_HEADDIM)[None, :]
        cbias = ssm_conv_b[l][None, :]
        last = l == depth - 1

        proj = _inproj(xp, g1, w_in_r, l, pad_range, TM_DENSE)
        (odn, dn_p, dnc_p), = _run_parts([_dn_mixer(proj, nblk, blk0, BLOCK, CHUNK, FRONT_PAD, bp, zero_dn, 0,
                                                    zero_dnc, dn_conv_w[l], prm, dn_nw)], "dn_mixer")
        (ys, ssm_p, ssmc_p), = _run_parts([_ssd_mixer(proj, nblk, blk0, BLOCK, CHUNK, FRONT_PAD, bp, zero_ssm, 0,
                                                      zero_ssmc, ssm_conv_w[l], cbias, prm, drow, ssm_nw)],
                                          "ssd_mixer")
        (osw, k_p, v_p), = _run_parts([_swa_prompt(proj, nblk, blk0, FRONT_PAD, cos_p, sin_p, prm)], "swa_prompt")
        xp = _tail(xp, odn, ys, osw, w_out_b, g2, g3, g4, w_fi_b, w_fo_b, l, seq if last else lp, TM_DENSE)
        new_p.append((dn_p, dnc_p[:, -(CONV_WIDTH - 1):], ssm_p.reshape(bp, SSM_HEADS, SSM_HEADDIM, SSM_STATE),
                      ssmc_p[:, -(CONV_WIDTH - 1):], k_p.reshape(bp, WINDOW, SWA_KV_HEADS, SWA_HEAD_DIM),
                      v_p.reshape(bp, WINDOW, SWA_KV_HEADS, SWA_HEAD_DIM)))

        proj = _inproj(xs, g1, w_in_r, l, None, TM_DENSE).reshape(bs, ts, D_PROJ)
        (odn, dn_s, dnc_s), (ys, ssm_s, ssmc_s), (osw, k_s, v_s) = _run_parts([
            _dn_mixer(proj, 1, 0, ts, ts, 0, nseq_s, state_dn, l, _pad_conv_state(state_dn_conv[l]), dn_conv_w[l],
                      prm, dn_nw, stack=(depth, dn_s)),
            _ssd_mixer(proj, 1, 0, ts, ts, 0, nseq_s, state_ssm_g, l, _pad_conv_state(state_ssm_conv[l]),
                       ssm_conv_w[l], cbias, prm, drow, ssm_nw, stack=(depth, ssm_s)),
            _swa_sample(proj, nseq_s, cache_k, cache_v, l, cos_s, sin_s, prm,
                        stack_k=(depth, k_s), stack_v=(depth, v_s))], "mixers_sample")
        flat = lambda a: a.reshape(1, bs * ts, a.shape[-1])
        xs = _tail(xs, flat(odn), flat(ys), flat(osw), w_out_b, g2, g3, g4, w_fi_b, w_fo_b, l, bs * ts, TM_DENSE)
        new_s.append((dnc_s[:, -(CONV_WIDTH - 1):], ssmc_s[:, -(CONV_WIDTH - 1):]))

    outs_p = tuple(jnp.stack([st[i] for st in new_p]) for i in range(6))
    dnc_s, ssmc_s = (jnp.stack([st[i] for st in new_s]) for i in range(2))
    outs_s = (dn_s, dnc_s, ssm_s.reshape(depth, bs, SSM_HEADS, SSM_HEADDIM, SSM_STATE), ssmc_s,
              k_s.reshape(depth, bs, WINDOW, SWA_KV_HEADS, SWA_HEAD_DIM),
              v_s.reshape(depth, bs, WINDOW, SWA_KV_HEADS, SWA_HEAD_DIM))
    return (xp, xs.reshape(bs, ts, d)) + outs_p + outs_s
```

```python
import functools

import jax
import jax.numpy as jnp
from jax import lax
from jax.experimental import pallas as pl
from jax.experimental.pallas import tpu as pltpu

F32 = jnp.float32
BF16 = jnp.bfloat16
NT = (((1,), (1,)), ((), ()))
TN = (((0,), (0,)), ((), ()))

N_META = 16
CONV_WIDTH = 4
CHUNK = 64
BLOCK = 128
WINDOW = 128
FRONT_PAD = BLOCK - N_META
ROPE_THETA = 10000.0
PAST_LEN = 8192
EPS = 1e-6

DN_HEADS, DN_DK, DN_DV = 4, 128, 128
DN_QK = DN_HEADS * DN_DK
DN_V = DN_HEADS * DN_DV
DN_CONV = 2 * DN_QK + DN_V
SSM_HEADS, SSM_HEADDIM, SSM_GROUPS, SSM_STATE = 4, 64, 2, 128
SSM_INNER = SSM_HEADS * SSM_HEADDIM
SSM_BC = SSM_GROUPS * SSM_STATE
SSM_CONV = SSM_INNER + 2 * SSM_BC
SWA_Q_HEADS, SWA_KV_HEADS, SWA_HEAD_DIM = 4, 2, 64
SWA_Q = SWA_Q_HEADS * SWA_HEAD_DIM
SWA_KV = SWA_KV_HEADS * SWA_HEAD_DIM
IN_WIDTHS = (DN_CONV, DN_V, DN_HEADS, DN_HEADS, SSM_CONV, SSM_INNER, SSM_HEADS, SWA_Q, SWA_KV, SWA_KV)

LANES = 128
SUBLANES = 8
COL_QKV = (0, DN_CONV)
COL_DNZ = (1536, DN_V)
COL_SSZ = (2048, SSM_INNER)
COL_XBC = (2304, SSM_CONV)
COL_SWQ = (3072, SWA_Q)
COL_SWK = (3328, SWA_KV)
COL_SWV = (3456, SWA_KV)
COL_SM = (3584, LANES)
D_PROJ = 3712
SM_B, SM_A, SM_DT = 0, 4, 8
PRM_DN_ALOG, PRM_DN_DTB, PRM_SSM_ALOG, PRM_SSM_DTB, PRM_SINK = 0, 1, 2, 3, 4
NEG_BIG = -1e30
VMEM_LIMIT = 56 * 1024 * 1024
TM_DENSE = 640


def _bdot(a, b, dims=None):
    a = a.astype(BF16)
    b = b.astype(BF16)
    if dims is None:
        return jnp.dot(a, b, preferred_element_type=F32)
    return lax.dot_general(a, b, dims, preferred_element_type=F32)


def _cumsum_rows(lmat, g):
    hi = g.astype(BF16)
    r1 = g - hi.astype(F32)
    mid = r1.astype(BF16)
    lo = (r1 - mid.astype(F32)).astype(BF16)
    dot = lambda part: jnp.dot(lmat, part, preferred_element_type=F32)
    return dot(hi) + dot(mid) + dot(lo)


def _rmsnorm(x, g):
    return x * lax.rsqrt(jnp.mean(x * x, axis=-1, keepdims=True) + EPS) * g


def _l2norm(x):
    return x * lax.rsqrt(jnp.sum(x * x, axis=-1, keepdims=True) + EPS)


def _sigmoid(x):
    return 1.0 / (1.0 + jnp.exp(-x))


def _silu(x):
    return x * _sigmoid(x)


def _softplus(x):
    return jnp.maximum(x, 0.0) + jnp.log1p(jnp.exp(-jnp.abs(x)))


def _pick_tile(n, target):
    best = None
    for t in range(SUBLANES, min(n, target) + 1, SUBLANES):
        if n % t == 0:
            best = t
    assert best is not None, n
    return best


def _inproj_kernel(x_ref, g_ref, w_ref, o_ref, *, tm, pad_range):
    h = _rmsnorm(x_ref[...], g_ref[...])
    if pad_range is not None:
        r = pl.program_id(1) * tm + lax.broadcasted_iota(jnp.int32, (tm, 1), 0)
        is_pad = (r >= pad_range[0]) & (r < pad_range[1])
        h = jnp.where(is_pad, 0.0, h)
    o_ref[...] = jnp.dot(h.astype(BF16), w_ref[...], preferred_element_type=F32)


def _inproj(x, g, w, layer, pad_range, tm_target):
    nb, rows, d = x.shape
    tm = _pick_tile(rows, tm_target)
    return pl.pallas_call(
        functools.partial(_inproj_kernel, tm=tm, pad_range=pad_range),
        out_shape=jax.ShapeDtypeStruct((nb, rows, D_PROJ), F32),
        grid=(nb, rows // tm),
        in_specs=[pl.BlockSpec((None, tm, d), lambda b, i: (b, i, 0)), _layer_spec(g.shape, layer),
                  _layer_spec(w.shape, layer)],
        out_specs=pl.BlockSpec((None, tm, D_PROJ), lambda b, i: (b, i, 0)),
        compiler_params=pltpu.CompilerParams(dimension_semantics=("arbitrary", "arbitrary"),
                                             vmem_limit_bytes=VMEM_LIMIT),
        name="inproj",
    )(x, g, w)


def _causal_conv(xbuf, raw_ref, cw_ref, cbo_ref, rows):
    x = raw_ref[...]
    prev = xbuf[0:SUBLANES, :]
    cw = cw_ref[...]
    row = lax.broadcasted_iota(jnp.int32, (SUBLANES, 1), 0)
    acc = None
    for i in range(CONV_WIDTH):
        s = CONV_WIDTH - 1 - i
        if s == 0:
            xs = x
        else:
            r = pltpu.roll(x, s, axis=0)
            head = jnp.where(row < s, pltpu.roll(prev, s, axis=0), r[0:SUBLANES])
            xs = head if rows == SUBLANES else jnp.concatenate([head, r[SUBLANES:]], axis=0)
        term = xs * cw[i:i + 1, :]
        acc = term if acc is None else acc + term
    tail = x[rows - SUBLANES:rows]
    cbo_ref[...] = tail
    xbuf[0:SUBLANES, :] = tail
    return acc


def _chunk_masks(c):
    ii = lax.broadcasted_iota(jnp.int32, (c, c), 0)
    jj = lax.broadcasted_iota(jnp.int32, (c, c), 1)
    return ii >= jj, ii > jj, (ii == jj).astype(F32)


def _segment_decay(gc, gct, lane, ge):
    col = gc[:, lane:lane + 1]
    row = gct[lane:lane + 1, :]
    return jnp.where(ge, jnp.exp(jnp.where(ge, col - row, 0.0)), 0.0)


def _inv_unit_lower_minus_eye(a_list, c, nh):
    w = nh * c
    blk_r = lax.broadcasted_iota(jnp.int32, (w, w), 0) // c
    blk_c = lax.broadcasted_iota(jnp.int32, (w, w), 1) // c
    same = blk_r == blk_c

    def block_diag(p):
        return jnp.where(same, jnp.concatenate([p] * nh, axis=0), 0.0).astype(BF16)

    ys = [-a for a in a_list]
    ps = [_bdot(a, block_diag(a)) for a in a_list]
    yield
    n = 2
    while n < c:
        pbds = [block_diag(p) for p in ps]
        n *= 2
        if n < c:
            sts = [_bdot(jnp.concatenate([y, p], axis=0), pbd) for y, p, pbd in zip(ys, ps, pbds)]
            ys = [y + p + st[:c] for y, p, st in zip(ys, ps, sts)]
            ps = [st[c:] for st in sts]
        else:
            ys = [y + p + _bdot(y, pbd) for y, p, pbd in zip(ys, ps, pbds)]
        yield
    return ys


def _run_parts(parts, name):
    grid = parts[0]["grid"]
    assert all(p["grid"] == grid for p in parts)
    n_in = [len(p["inputs"]) for p in parts]
    n_out = [len(p["out_shape"]) for p in parts]
    n_scr = [len(p["scratch"]) for p in parts]
    aliases = {}
    for k, p in enumerate(parts):
        for i, o in p["aliases"].items():
            aliases[sum(n_in[:k]) + i] = sum(n_out[:k]) + o

    def kernel(*refs):
        ins = refs[:sum(n_in)]
        outs = refs[sum(n_in):sum(n_in) + sum(n_out)]
        scr = refs[sum(n_in) + sum(n_out):]
        gens = [p["body"](ins[sum(n_in[:k]):sum(n_in[:k + 1])], outs[sum(n_out[:k]):sum(n_out[:k + 1])],
                          scr[sum(n_scr[:k]):sum(n_scr[:k + 1])]) for k, p in enumerate(parts)]
        for tag in gens[0]:
            if tag == "chain":
                break
        live = list(gens)
        while live:
            for g in list(live):
                if next(g, StopIteration) is StopIteration:
                    live.remove(g)

    results = pl.pallas_call(
        kernel,
        out_shape=tuple(s for p in parts for s in p["out_shape"]),
        grid=grid,
        in_specs=[s for p in parts for s in p["in_specs"]],
        out_specs=tuple(s for p in parts for s in p["out_specs"]),
        scratch_shapes=[s for p in parts for s in p["scratch"]],
        input_output_aliases=aliases,
        compiler_params=pltpu.CompilerParams(dimension_semantics=("arbitrary",) * len(grid),
                                             vmem_limit_bytes=VMEM_LIMIT),
        name=name,
    )(*[a for p in parts for a in p["inputs"]])
    return [list(results[sum(n_out[:k]):sum(n_out[:k + 1])]) for k in range(len(parts))]


def _state_out(stack, layer, nb, nseq, tail, out_index, n_inputs):
    zeros = (0,) * len(tail)
    if stack is None:
        return ((nb,) + tail, pl.BlockSpec((nseq,) + tail, lambda b, *_: (b,) + zeros), [], [], {}, None, 0)
    depth, prev = stack
    shape = (depth, nb) + tail
    if prev is None:
        spec = pl.BlockSpec((depth, nseq) + tail, lambda b, *_: (0, b) + zeros)
        return (shape, spec, [], [], {}, layer, depth)
    spec = pl.BlockSpec((None, nseq) + tail, lambda b, *_: (layer, b) + zeros)
    return (shape, spec, [prev], [pl.BlockSpec(memory_space=pl.ANY)], {n_inputs: out_index}, None, 0)


def _state_view(ref, slot, n_slots):
    if slot is None:
        return ref
    for other in range(n_slots):
        if other != slot:
            ref[other] = jnp.zeros(ref.shape[1:], ref.dtype)
    return ref.at[slot]


def _dn_prep(items, lmat, ge, gt_all, chunk):
    heads = range(DN_HEADS)
    gcs = [_cumsum_rows(lmat, g_all) for _, _, g_all in items]
    gcts = [gc.T for gc in gcs]
    qs = [[_l2norm(qkv[:, h * DN_DK:(h + 1) * DN_DK]) * DN_DK ** -0.5 for h in heads] for qkv, _, _ in items]
    ks = [[_l2norm(qkv[:, DN_QK + h * DN_DK:DN_QK + (h + 1) * DN_DK]) for h in heads] for qkv, _, _ in items]
    vs = [[qkv[:, 2 * DN_QK + h * DN_DV:2 * DN_QK + (h + 1) * DN_DV] for h in heads] for qkv, _, _ in items]
    betas = [[beta_all[:, SM_B + h:SM_B + h + 1] for h in heads] for _, beta_all, _ in items]
    yield
    kbs = [[k.astype(BF16) for k in kk] for kk in ks]
    kks = [jnp.concatenate([_bdot(kb, kb, NT) for kb in kb4], axis=1) for kb4 in kbs]
    qks = [jnp.concatenate([_bdot(q, kb, NT) for q, kb in zip(q4, kb4)], axis=1) for q4, kb4 in zip(qs, kbs)]
    decs = [jnp.concatenate([_segment_decay(gc, gct, SM_A + h, ge) for h in heads], axis=1)
            for gc, gct in zip(gcs, gcts)]
    beta_ws = [jnp.concatenate([jnp.broadcast_to(b, (chunk, chunk)) for b in b4], axis=1) for b4 in betas]
    a_list = [jnp.where(gt_all, bw * kk * dec, 0.0) for bw, kk, dec in zip(beta_ws, kks, decs)]
    yield "chain"
    tms = yield from _inv_unit_lower_minus_eye(a_list, chunk, DN_HEADS)
    attns = [(qk * dec).astype(BF16) for qk, dec in zip(qks, decs)]
    egs = [jnp.exp(gc) for gc in gcs]
    rhss = [[jnp.concatenate([vs[i][h] * betas[i][h], ks[i][h] * (betas[i][h] * egs[i][:, SM_A + h:SM_A + h + 1])],
                             axis=1) for h in heads] for i in range(len(items))]
    uws = [[rhss[i][h] + _bdot(tms[i][:, h * chunk:(h + 1) * chunk], rhss[i][h]) for h in heads]
           for i in range(len(items))]
    yield
    out = []
    for i, gc in enumerate(gcs):
        glast = gc[chunk - 1:chunk, :]
        ekd = jnp.exp(glast - gc)
        egl = jnp.exp(glast)
        per_head = []
        for h in heads:
            uw = uws[i][h]
            wq = jnp.concatenate([uw[:, DN_DV:], qs[i][h] * egs[i][:, SM_A + h:SM_A + h + 1]], axis=0).astype(BF16)
            kd = (ks[i][h] * ekd[:, SM_A + h:SM_A + h + 1]).astype(BF16)
            per_head.append((uw[:, :DN_DV], wq, attns[i][:, h * chunk:(h + 1) * chunk], kd,
                             egl[:, SM_A + h:SM_A + h + 1]))
        out.append(per_head)
    return out


def _dn_body(ins, outs, scratch, *, nseq, rows, chunk, front_pad, slot, n_slots):
    qkv_ref, z_ref, sm_ref, s0_ref, cb_ref, cw_ref, prm_ref, nw_ref = ins[:8]
    o_ref, so_full, cbo_ref = outs
    xbuf, = scratch
    so_ref = _state_view(so_full, slot, n_slots)
    blk = pl.program_id(1)
    first = blk == 0

    @pl.when(first)
    def _():
        so_ref[...] = s0_ref[...]
        xbuf[:, 0:SUBLANES, :] = cb_ref[...]

    prm = prm_ref[...]
    nw = nw_ref[...]
    ge, _, _ = _chunk_masks(chunk)
    lmat = jnp.where(ge, 1.0, 0.0).astype(BF16)
    wide = (chunk, DN_HEADS * chunk)
    gt_all = lax.broadcasted_iota(jnp.int32, wide, 0) > lax.broadcasted_iota(jnp.int32, wide, 1) % chunk
    nchunk = rows // chunk

    items = []
    for s in range(nseq):
        qkv = _silu(_causal_conv(xbuf.at[s], qkv_ref.at[s], cw_ref, cbo_ref.at[s], rows))
        sm = sm_ref[s]
        beta_all = _sigmoid(sm)
        g_all = -jnp.exp(prm[PRM_DN_ALOG:PRM_DN_ALOG + 1, :]) * _softplus(sm + prm[PRM_DN_DTB:PRM_DN_DTB + 1, :])
        if front_pad:
            pos = blk * rows + lax.broadcasted_iota(jnp.int32, (rows, 1), 0)
            g_all = jnp.where(pos < front_pad, 0.0, g_all)
        for c in range(nchunk):
            cs = slice(c * chunk, (c + 1) * chunk)
            items.append((qkv[cs], beta_all[cs], g_all[cs]))
        yield
    prep = yield from _dn_prep(items, lmat, ge, gt_all, chunk)

    chains = [(s, h) for s in range(nseq) for h in range(DN_HEADS)]
    states = [so_ref[s, h] for s, h in chains]
    for c in range(nchunk):
        r0 = c * chunk
        fac = [prep[s * nchunk + c][h] for s, h in chains]
        m1s = [_bdot(f[1], st) for f, st in zip(fac, states)]
        yield
        v_news = [(f[0] - m1[:chunk]).astype(BF16) for f, m1 in zip(fac, m1s)]
        ups = [_bdot(f[3], v, TN) for f, v in zip(fac, v_news)]
        os_ = [m1[chunk:] + _bdot(f[2], v) for f, m1, v in zip(fac, m1s, v_news)]
        yield
        states = [st * f[4] + up for f, st, up in zip(fac, states, ups)]
        for (s, h), o in zip(chains, os_):
            lo = h * DN_DV
            zh = z_ref[s, r0:r0 + chunk, lo:lo + DN_DV]
            o_ref[s, r0:r0 + chunk, lo:lo + DN_DV] = _rmsnorm(o, nw) * _silu(zh)
        yield
    for (s, h), st in zip(chains, states):
        so_ref[s, h] = st


def _dn_mixer(proj, nblk, blk0, rows, chunk, front_pad, nseq, s0, layer, cbuf, cw, prm, nw, stack=None):
    nb = proj.shape[0]
    col = lambda c: c[0] // c[1]
    phys = lambda i: (i + blk0) % nblk
    so_shape, so_spec, extra, extra_specs, aliases, slot, n_slots = _state_out(
        stack, layer, nb, nseq, (DN_HEADS, DN_DK, DN_DV), 1, 8)
    return dict(
        body=functools.partial(_dn_body, nseq=nseq, rows=rows, chunk=chunk, front_pad=front_pad,
                               slot=slot, n_slots=n_slots),
        grid=(nb // nseq, nblk),
        inputs=[proj, proj, proj, s0, cbuf, cw, prm, nw] + extra,
        out_shape=[jax.ShapeDtypeStruct((nb, nblk * rows, DN_V), F32),
                   jax.ShapeDtypeStruct(so_shape, F32),
                   jax.ShapeDtypeStruct((nb, SUBLANES, DN_CONV), F32)],
        in_specs=[
            pl.BlockSpec((nseq, rows, DN_CONV), lambda b, i: (b, phys(i), col(COL_QKV))),
            pl.BlockSpec((nseq, rows, DN_V), lambda b, i: (b, phys(i), col(COL_DNZ))),
            pl.BlockSpec((nseq, rows, LANES), lambda b, i: (b, phys(i), col(COL_SM))),
            pl.BlockSpec((None, nseq, DN_HEADS, DN_DK, DN_DV), lambda b, i: (layer, b, 0, 0, 0)),
            pl.BlockSpec((nseq, SUBLANES, DN_CONV), lambda b, i: (b, 0, 0)),
            pl.BlockSpec((CONV_WIDTH, DN_CONV), lambda b, i: (0, 0)),
            pl.BlockSpec((SUBLANES, LANES), lambda b, i: (0, 0)),
            pl.BlockSpec((1, DN_DV), lambda b, i: (0, 0)),
        ] + extra_specs,
        out_specs=[
            pl.BlockSpec((nseq, rows, DN_V), lambda b, i: (b, phys(i), 0)),
            so_spec,
            pl.BlockSpec((nseq, SUBLANES, DN_CONV), lambda b, i: (b, 0, 0)),
        ],
        scratch=[pltpu.VMEM((nseq, SUBLANES, DN_CONV), F32)],
        aliases=aliases,
    )


def _ssd_body(ins, outs, scratch, *, nseq, rows, chunk, front_pad, slot, n_slots):
    xbc_ref, z_ref, sm_ref, h0_ref, cb_ref, cw_ref, cbias_ref, prm_ref, drow_ref, nw_ref = ins[:10]
    y_ref, ho_full, cbo_ref = outs
    xbuf, = scratch
    ho_ref = _state_view(ho_full, slot, n_slots)
    blk = pl.program_id(1)

    @pl.when(blk == 0)
    def _():
        ho_ref[...] = h0_ref[...]
        xbuf[:, 0:SUBLANES, :] = cb_ref[...]

    prm = prm_ref[...]
    nw = nw_ref[...]
    drow = drow_ref[...]
    cbias = cbias_ref[...]
    ge, _, _ = _chunk_masks(chunk)
    lmat = jnp.where(ge, 1.0, 0.0).astype(BF16)
    hpg = SSM_HEADS // SSM_GROUPS
    gw = hpg * SSM_HEADDIM
    lane = lax.broadcasted_iota(jnp.int32, (1, gw), 1)
    srow = lax.broadcasted_iota(jnp.int32, (gw, 1), 0)
    in_head = [(lane >= j * SSM_HEADDIM) & (lane < (j + 1) * SSM_HEADDIM) for j in range(hpg)]
    nchunk = rows // chunk
    groups = range(SSM_GROUPS)

    items = []
    for s in range(nseq):
        act = _silu(_causal_conv(xbuf.at[s], xbc_ref.at[s], cw_ref, cbo_ref.at[s], rows) + cbias)
        dt_all = _softplus(sm_ref[s] + prm[PRM_SSM_DTB:PRM_SSM_DTB + 1, :])
        if front_pad:
            pos = blk * rows + lax.broadcasted_iota(jnp.int32, (rows, 1), 0)
            dt_all = jnp.where(pos < front_pad, 0.0, dt_all)
        g_all = dt_all * (-jnp.exp(prm[PRM_SSM_ALOG:PRM_SSM_ALOG + 1, :]))
        for c in range(nchunk):
            cs = slice(c * chunk, (c + 1) * chunk)
            items.append((act[cs], dt_all[cs], g_all[cs]))
        yield
    n_items = len(items)
    gcs = [_cumsum_rows(lmat, g) for _, _, g in items]
    gcts = [gc.T for gc in gcs]
    xgs = [[a[:, g * gw:(g + 1) * gw] for g in groups] for a, _, _ in items]
    bgs = [[a[:, SSM_INNER + g * SSM_STATE:SSM_INNER + (g + 1) * SSM_STATE] for g in groups] for a, _, _ in items]
    cgs = [[a[:, SSM_INNER + SSM_BC + g * SSM_STATE:SSM_INNER + SSM_BC + (g + 1) * SSM_STATE] for g in groups]
           for a, _, _ in items]
    cbs = [[_bdot(cgs[i][g], bgs[i][g], NT) for g in groups] for i in range(n_items)]
    yield
    egs = [jnp.exp(gc) for gc in gcs]
    ekds = [jnp.exp(gc[chunk - 1:chunk, :] - gc) for gc in gcs]
    egls = [jnp.exp(gc[chunk - 1:chunk, :]) for gc in gcs]
    heads = [(g, j) for g in groups for j in range(hpg)]
    ln = lambda g, j: SM_DT + g * hpg + j
    xdts = [[jnp.where(in_head[j], xgs[i][g] * items[i][1][:, ln(g, j):ln(g, j) + 1], 0.0).astype(BF16)
             for g, j in heads] for i in range(n_items)]
    attns = [[cbs[i][g] * _segment_decay(gcs[i], gcts[i], ln(g, j), ge) for g, j in heads] for i in range(n_items)]
    y_intras = [[_bdot(attns[i][k], xdts[i][k]) for k in range(len(heads))] for i in range(n_items)]
    yield
    upds = [[_bdot(xdts[i][k], bgs[i][g] * ekds[i][:, ln(g, j):ln(g, j) + 1], TN) for k, (g, j) in enumerate(heads)]
            for i in range(n_items)]
    yield
    cds = [[jnp.concatenate([cgs[i][g] * egs[i][:, ln(g, j):ln(g, j) + 1] for j in range(hpg)], axis=0).astype(BF16)
            for g in groups] for i in range(n_items)]
    y_loc = [[sum(y_intras[i][g * hpg + j] for j in range(hpg)) for g in groups] for i in range(n_items)]
    h_inc = [[sum(upds[i][g * hpg + j] for j in range(hpg)) for g in groups] for i in range(n_items)]
    gl_cols = []
    for i in range(n_items):
        per_group = []
        for g in groups:
            gl = egls[i][:, ln(g, 0):ln(g, 0) + 1]
            for j in range(1, hpg):
                gl = jnp.where(srow < j * SSM_HEADDIM, gl, egls[i][:, ln(g, j):ln(g, j) + 1])
            per_group.append(gl)
        gl_cols.append(per_group)

    chains = [(s, g) for s in range(nseq) for g in groups]
    states = [ho_ref[s, g] for s, g in chains]
    for c in range(nchunk):
        r0 = c * chunk
        idx = [s * nchunk + c for s, _ in chains]
        yis = [_bdot(cds[i][g], st, NT) for i, (_, g), st in zip(idx, chains, states)]
        yield
        states = [st * gl_cols[i][g] + h_inc[i][g] for i, (_, g), st in zip(idx, chains, states)]
        for i, (s, g), yi in zip(idx, chains, yis):
            y_inter = yi[0:chunk]
            for j in range(1, hpg):
                y_inter = jnp.where(in_head[j], yi[j * chunk:(j + 1) * chunk], y_inter)
            yg = y_loc[i][g] + y_inter + xgs[i][g] * drow[:, g * gw:(g + 1) * gw]
            yg = yg * _silu(z_ref[s, r0:r0 + chunk, g * gw:(g + 1) * gw])
            y_ref[s, r0:r0 + chunk, g * gw:(g + 1) * gw] = _rmsnorm(yg, nw[:, g * gw:(g + 1) * gw])
        yield
    for (s, g), st in zip(chains, states):
        ho_ref[s, g] = st


def _ssd_mixer(proj, nblk, blk0, rows, chunk, front_pad, nseq, h0, layer, cbuf, cw, cbias, prm, drow, nw,
               stack=None):
    nb = proj.shape[0]
    col = lambda c: c[0] // c[1]
    phys = lambda i: (i + blk0) % nblk
    gw = (SSM_HEADS // SSM_GROUPS) * SSM_HEADDIM
    ho_shape, ho_spec, extra, extra_specs, aliases, slot, n_slots = _state_out(
        stack, layer, nb, nseq, (SSM_GROUPS, gw, SSM_STATE), 1, 10)
    return dict(
        body=functools.partial(_ssd_body, nseq=nseq, rows=rows, chunk=chunk, front_pad=front_pad,
                               slot=slot, n_slots=n_slots),
        grid=(nb // nseq, nblk),
        inputs=[proj, proj, proj, h0, cbuf, cw, cbias, prm, drow, nw] + extra,
        out_shape=[jax.ShapeDtypeStruct((nb, nblk * rows, SSM_INNER), F32),
                   jax.ShapeDtypeStruct(ho_shape, F32),
                   jax.ShapeDtypeStruct((nb, SUBLANES, SSM_CONV), F32)],
        in_specs=[
            pl.BlockSpec((nseq, rows, SSM_CONV), lambda b, i: (b, phys(i), col(COL_XBC))),
            pl.BlockSpec((nseq, rows, SSM_INNER), lambda b, i: (b, phys(i), col(COL_SSZ))),
            pl.BlockSpec((nseq, rows, LANES), lambda b, i: (b, phys(i), col(COL_SM))),
            pl.BlockSpec((None, nseq, SSM_GROUPS, gw, SSM_STATE), lambda b, i: (layer, b, 0, 0, 0)),
            pl.BlockSpec((nseq, SUBLANES, SSM_CONV), lambda b, i: (b, 0, 0)),
            pl.BlockSpec((CONV_WIDTH, SSM_CONV), lambda b, i: (0, 0)),
            pl.BlockSpec((1, SSM_CONV), lambda b, i: (0, 0)),
            pl.BlockSpec((SUBLANES, LANES), lambda b, i: (0, 0)),
            pl.BlockSpec((1, SSM_INNER), lambda b, i: (0, 0)),
            pl.BlockSpec((1, SSM_INNER), lambda b, i: (0, 0)),
        ] + extra_specs,
        out_specs=[
            pl.BlockSpec((nseq, rows, SSM_INNER), lambda b, i: (b, phys(i), 0)),
            ho_spec,
            pl.BlockSpec((nseq, SUBLANES, SSM_CONV), lambda b, i: (b, 0, 0)),
        ],
        scratch=[pltpu.VMEM((nseq, SUBLANES, SSM_CONV), F32)],
        aliases=aliases,
    )


def _rope(x, cos, sin_signed):
    w = x.shape[-1]
    half = SWA_HEAD_DIM // 2
    lane = lax.broadcasted_iota(jnp.int32, (1, w), 1)
    first_half = (lane % SWA_HEAD_DIM) < half
    swapped = jnp.where(first_half, pltpu.roll(x, w - half, axis=1), pltpu.roll(x, half, axis=1))
    return x * cos + swapped * sin_signed


def _sink_attend(problems):
    scale = SWA_HEAD_DIM ** -0.5
    scores = [[jnp.where(m, _bdot(q, k, NT) * scale, NEG_BIG) for k, m in zip(keys, masks)]
              for q, keys, _, masks, _ in problems]
    yield
    outs = []
    probs, dens = [], []
    def lane_reduce(tiles, combine, reduce):
        merged = {}
        for t in tiles:
            merged[t.shape[-1]] = t if t.shape[-1] not in merged else combine(merged[t.shape[-1]], t)
        return [reduce(t, axis=-1, keepdims=True) for t in merged.values()]

    for (_, _, _, _, sink), ss in zip(problems, scores):
        mx = sink
        for m in lane_reduce(ss, jnp.maximum, jnp.max):
            mx = jnp.maximum(mx, m)
        ps = [jnp.exp(s - mx) for s in ss]
        den = jnp.exp(sink - mx)
        for d in lane_reduce(ps, jnp.add, jnp.sum):
            den = den + d
        probs.append(ps)
        dens.append(den)
    yield
    pvs = [[_bdot(p, v) for p, v in zip(ps, vals)] for (_, _, vals, _, _), ps in zip(problems, probs)]
    yield
    for pv, den in zip(pvs, dens):
        acc = pv[0]
        for extra in pv[1:]:
            acc = acc + extra
        outs.append(acc / den)
    return outs


def _swa_problems(q, key_sets, val_sets, masks, prm, tq):
    grp = SWA_Q_HEADS // SWA_KV_HEADS
    row = lax.broadcasted_iota(jnp.int32, (grp * tq, 1), 0)
    problems = []
    for j in range(SWA_KV_HEADS):
        ks = slice(j * SWA_HEAD_DIM, (j + 1) * SWA_HEAD_DIM)
        heads = [j * grp + g for g in range(grp)]
        qst = jnp.concatenate([q[:, h * SWA_HEAD_DIM:(h + 1) * SWA_HEAD_DIM] for h in heads], axis=0)
        sink = prm[PRM_SINK:PRM_SINK + 1, heads[0]:heads[0] + 1]
        for g in range(1, grp):
            sink = jnp.where(row < g * tq, sink, prm[PRM_SINK:PRM_SINK + 1, heads[g]:heads[g] + 1])
        problems.append((qst, [k[:, ks] for k in key_sets], [v[:, ks] for v in val_sets], masks, sink))
    return problems


def _swa_store(o_ref_at, outs, tq):
    grp = SWA_Q_HEADS // SWA_KV_HEADS
    for j, o in enumerate(outs):
        for g in range(grp):
            h = j * grp + g
            o_ref_at(slice(h * SWA_HEAD_DIM, (h + 1) * SWA_HEAD_DIM), o[g * tq:(g + 1) * tq])


def _swa_prompt_body(ins, outs, scratch, *, nseq, front_pad):
    q_ref, k_ref, v_ref, cos_ref, sin_ref, prm_ref = ins
    o_ref, ko_ref, vo_ref = outs
    kprev, vprev = scratch
    blk = pl.program_id(1)

    @pl.when(blk == 0)
    def _():
        kprev[...] = jnp.zeros_like(kprev)
        vprev[...] = jnp.zeros_like(vprev)

    cos = cos_ref[...]
    sin = sin_ref[...]
    cos_q = jnp.concatenate([cos, cos], axis=1)
    sin_q = jnp.concatenate([sin, sin], axis=1)
    grp = SWA_Q_HEADS // SWA_KV_HEADS
    qi = lax.broadcasted_iota(jnp.int32, (grp * BLOCK, BLOCK), 0) % BLOCK
    kj = lax.broadcasted_iota(jnp.int32, (grp * BLOCK, BLOCK), 1)
    mask_cur = (kj <= qi) & (blk * BLOCK + kj >= front_pad)
    mask_prev = (kj > qi) & ((blk - 1) * BLOCK + kj >= front_pad)
    prm = prm_ref[...]
    problems = []
    for s in range(nseq):
        q = _rope(q_ref[s], cos_q, sin_q)
        k = _rope(k_ref[s], cos, sin)
        v = v_ref[s]
        problems += _swa_problems(q, (kprev[s], k), (vprev[s], v), (mask_prev, mask_cur), prm, BLOCK)
        kprev[s] = k
        vprev[s] = v
        ko_ref[s] = k
        vo_ref[s] = v
        yield
    res = yield from _sink_attend(problems)
    for s in range(nseq):
        def put(cols, val, s=s):
            o_ref[s, :, cols] = val
        _swa_store(put, res[s * SWA_KV_HEADS:(s + 1) * SWA_KV_HEADS], BLOCK)


def _swa_prompt(proj, nblk, blk0, front_pad, cos, sin, prm):
    nb = proj.shape[0]
    col = lambda c: c[0] // c[1]
    phys = lambda i: (i + blk0) % nblk
    return dict(
        body=functools.partial(_swa_prompt_body, nseq=nb, front_pad=front_pad),
        grid=(1, nblk),
        inputs=[proj, proj, proj, cos, sin, prm],
        out_shape=[jax.ShapeDtypeStruct((nb, nblk * BLOCK, SWA_Q), F32),
                   jax.ShapeDtypeStruct((nb, WINDOW, SWA_KV), F32),
                   jax.ShapeDtypeStruct((nb, WINDOW, SWA_KV), F32)],
        in_specs=[
            pl.BlockSpec((nb, BLOCK, SWA_Q), lambda b, i: (0, phys(i), col(COL_SWQ))),
            pl.BlockSpec((nb, BLOCK, SWA_KV), lambda b, i: (0, phys(i), col(COL_SWK))),
            pl.BlockSpec((nb, BLOCK, SWA_KV), lambda b, i: (0, phys(i), col(COL_SWV))),
            pl.BlockSpec((BLOCK, SWA_KV), lambda b, i: (i, 0)),
            pl.BlockSpec((BLOCK, SWA_KV), lambda b, i: (i, 0)),
            pl.BlockSpec((SUBLANES, LANES), lambda b, i: (0, 0)),
        ],
        out_specs=[
            pl.BlockSpec((nb, BLOCK, SWA_Q), lambda b, i: (0, phys(i), 0)),
            pl.BlockSpec((nb, WINDOW, SWA_KV), lambda b, i: (0, 0, 0)),
            pl.BlockSpec((nb, WINDOW, SWA_KV), lambda b, i: (0, 0, 0)),
        ],
        scratch=[pltpu.VMEM((nb, BLOCK, SWA_KV), F32), pltpu.VMEM((nb, BLOCK, SWA_KV), F32)],
        aliases={},
    )


def _swa_sample_body(ins, outs, scratch, *, nseq, steps, slot, n_slots):
    q_ref, k_ref, v_ref, kc_ref, vc_ref, cos_ref, sin_ref, prm_ref = ins[:8]
    o_ref, ko_full, vo_full = outs
    ko_ref = _state_view(ko_full, slot, n_slots)
    vo_ref = _state_view(vo_full, slot, n_slots)
    cos = cos_ref[...]
    sin = sin_ref[...]
    cos_q = jnp.concatenate([cos, cos], axis=1)
    sin_q = jnp.concatenate([sin, sin], axis=1)
    prm = prm_ref[...]
    grp = SWA_Q_HEADS // SWA_KV_HEADS
    ti = lax.broadcasted_iota(jnp.int32, (grp * steps, WINDOW), 0) % steps
    sj = lax.broadcasted_iota(jnp.int32, (grp * steps, WINDOW), 1)
    mask_cache = sj > ti
    tn = lax.broadcasted_iota(jnp.int32, (grp * steps, steps), 0) % steps
    sn = lax.broadcasted_iota(jnp.int32, (grp * steps, steps), 1)
    mask_new = sn <= tn
    problems = []
    for b in range(nseq):
        q = _rope(q_ref[b], cos_q, sin_q)
        k = _rope(k_ref[b], cos, sin)
        v = v_ref[b]
        kc = kc_ref[b]
        vc = vc_ref[b]
        ko_ref[b, 0:WINDOW - steps, :] = kc[steps:WINDOW, :]
        ko_ref[b, WINDOW - steps:WINDOW, :] = k
        vo_ref[b, 0:WINDOW - steps, :] = vc[steps:WINDOW, :]
        vo_ref[b, WINDOW - steps:WINDOW, :] = v
        problems += _swa_problems(q, (kc, k), (vc, v), (mask_cache, mask_new), prm, steps)
        yield
    res = yield from _sink_attend(problems)
    for b in range(nseq):
        def put(cols, val, b=b):
            o_ref[b, :, cols] = val
        _swa_store(put, res[b * SWA_KV_HEADS:(b + 1) * SWA_KV_HEADS], steps)


def _swa_sample(proj, nseq, kc, vc, layer, cos, sin, prm, stack_k=None, stack_v=None):
    nb, steps, _ = proj.shape
    assert WINDOW > steps
    col = lambda c: c[0] // c[1]
    ko_shape, ko_spec, extra_k, specs_k, alias_k, slot, n_slots = _state_out(
        stack_k, layer, nb, nseq, (WINDOW, SWA_KV), 1, 8)
    vo_shape, vo_spec, extra_v, specs_v, alias_v, _, _ = _state_out(
        stack_v, layer, nb, nseq, (WINDOW, SWA_KV), 2, 8 + len(extra_k))
    return dict(
        body=functools.partial(_swa_sample_body, nseq=nseq, steps=steps, slot=slot, n_slots=n_slots),
        grid=(nb // nseq, 1),
        inputs=[proj, proj, proj, kc, vc, cos, sin, prm] + extra_k + extra_v,
        out_shape=[jax.ShapeDtypeStruct((nb, steps, SWA_Q), F32),
                   jax.ShapeDtypeStruct(ko_shape, F32),
                   jax.ShapeDtypeStruct(vo_shape, F32)],
        in_specs=[
            pl.BlockSpec((nseq, steps, SWA_Q), lambda b, i: (b, 0, col(COL_SWQ))),
            pl.BlockSpec((nseq, steps, SWA_KV), lambda b, i: (b, 0, col(COL_SWK))),
            pl.BlockSpec((nseq, steps, SWA_KV), lambda b, i: (b, 0, col(COL_SWV))),
            pl.BlockSpec((None, nseq, WINDOW, SWA_KV), lambda b, i: (layer, b, 0, 0)),
            pl.BlockSpec((None, nseq, WINDOW, SWA_KV), lambda b, i: (layer, b, 0, 0)),
            pl.BlockSpec((steps, SWA_KV), lambda b, i: (0, 0)),
            pl.BlockSpec((steps, SWA_KV), lambda b, i: (0, 0)),
            pl.BlockSpec((SUBLANES, LANES), lambda b, i: (0, 0)),
        ] + specs_k + specs_v,
        out_specs=[pl.BlockSpec((nseq, steps, SWA_Q), lambda b, i: (b, 0, 0)), ko_spec, vo_spec],
        scratch=[],
        aliases={**alias_k, **alias_v},
    )


def _tail_kernel(x_ref, odn_ref, y_ref, osw_ref, wout_ref, g1_ref, g2_ref, g3_ref, wfi_ref, wfo_ref, o_ref,
                 *, d_ff, tf):
    mixed = jnp.concatenate([odn_ref[...], y_ref[...], osw_ref[...]], axis=1).astype(BF16)
    m = jnp.dot(mixed, wout_ref[...], preferred_element_type=F32)
    x1 = x_ref[...] + _rmsnorm(m, g1_ref[...])
    h = _rmsnorm(x1, g2_ref[...]).astype(BF16)
    y2 = None
    for c in range(d_ff // tf):
        gate = jnp.dot(h, wfi_ref[:, c * tf:(c + 1) * tf], preferred_element_type=F32)
        up = jnp.dot(h, wfi_ref[:, d_ff + c * tf:d_ff + (c + 1) * tf], preferred_element_type=F32)
        part = jnp.dot((_silu(gate) * up).astype(BF16), wfo_ref[c * tf:(c + 1) * tf, :], preferred_element_type=F32)
        y2 = part if y2 is None else y2 + part
    o_ref[...] = x1 + _rmsnorm(y2, g3_ref[...])


def _layer_spec(shape, layer):
    nd = len(shape) - 1
    return pl.BlockSpec((None,) + tuple(shape[1:]), lambda *_: (layer,) + (0,) * nd, pipeline_mode=pl.Buffered(1))


def _tail(x, odn, y, osw, wout, g1, g2, g3, wfi, wfo, layer, l_out, tm_target):
    nb, _, d = x.shape
    d_ff = wfo.shape[1]
    tm = _pick_tile(l_out, tm_target)
    tf = 2 * LANES if d_ff % (2 * LANES) == 0 else d_ff
    row = lambda w: pl.BlockSpec((None, tm, w), lambda b, i: (b, i, 0))
    return pl.pallas_call(
        functools.partial(_tail_kernel, d_ff=d_ff, tf=tf),
        out_shape=jax.ShapeDtypeStruct((nb, l_out, d), F32),
        grid=(nb, l_out // tm),
        in_specs=[row(d), row(DN_V), row(SSM_INNER), row(SWA_Q), _layer_spec(wout.shape, layer),
                  _layer_spec(g1.shape, layer), _layer_spec(g2.shape, layer), _layer_spec(g3.shape, layer),
                  _layer_spec(wfi.shape, layer), _layer_spec(wfo.shape, layer)],
        out_specs=row(d),
        compiler_params=pltpu.CompilerParams(dimension_semantics=("arbitrary", "arbitrary"),
                                             vmem_limit_bytes=VMEM_LIMIT),
        name="outproj_ffn",
    )(x, odn, y, osw, wout, g1, g2, g3, wfi, wfo)


def _reorder_w_in_kernel(w_ref, o_ref):
    w = w_ref[0]
    offs = [0]
    for wd in IN_WIDTHS:
        offs.append(offs[-1] + wd)
    seg = lambda i: w[:, offs[i]:offs[i + 1]]
    dn_qkv, dn_z, dn_b, dn_a, ssm_xbc, ssm_z, ssm_dt, sw_q, sw_k, sw_v = (seg(i) for i in range(len(IN_WIDTHS)))
    n_small = dn_b.shape[1] + dn_a.shape[1] + ssm_dt.shape[1]
    small = jnp.concatenate([dn_b, dn_a, ssm_dt, jnp.zeros((w.shape[0], LANES - n_small), w.dtype)], axis=1)
    o_ref[0] = jnp.concatenate([dn_qkv, dn_z, ssm_z, ssm_xbc, sw_q, sw_k, sw_v, small], axis=1)


def _reorder_w_in(w):
    depth, d, d_in = w.shape
    assert d_in == sum(IN_WIDTHS)
    tr = _pick_tile(d, 256)
    return pl.pallas_call(
        _reorder_w_in_kernel,
        out_shape=jax.ShapeDtypeStruct((depth, d, D_PROJ), BF16),
        grid=(depth, d // tr),
        in_specs=[pl.BlockSpec((1, tr, d_in), lambda l, i: (l, i, 0))],
        out_specs=pl.BlockSpec((1, tr, D_PROJ), lambda l, i: (l, i, 0)),
        compiler_params=pltpu.CompilerParams(dimension_semantics=("arbitrary", "arbitrary")),
        name="reorder_w_in",
    )(w)


def _lane_row(pairs):
    row = jnp.zeros((LANES,), F32)
    for off, vec in pairs:
        row = row.at[off:off + vec.shape[0]].set(vec.astype(F32))
    return row[None, :]


def _rope_tables(pos):
    half = SWA_HEAD_DIM // 2
    inv = ROPE_THETA ** (-jnp.arange(half, dtype=F32) / half)
    ang = pos.astype(F32)[:, None] * inv[None, :]
    cos = jnp.cos(ang)
    sin = jnp.sin(ang)
    cos_t = jnp.concatenate([cos, cos] * SWA_KV_HEADS, axis=1)
    sin_t = jnp.concatenate([-sin, sin] * SWA_KV_HEADS, axis=1)
    return cos_t, sin_t


def _pad_conv_state(buf):
    return jnp.pad(buf, ((0, 0), (SUBLANES - (CONV_WIDTH - 1), 0), (0, 0)))


def kernel(x_prompt, x_sample, state_dn, state_dn_conv, state_ssm, state_ssm_conv, cache_swa_k, cache_swa_v,
           meta_tokens, w_in, dn_conv_w, dn_a_log, dn_dt_bias, dn_norm_w, ssm_conv_w, ssm_conv_b, ssm_a_log,
           ssm_dt_bias, ssm_d, ssm_norm_w, swa_sinks, w_out, g_pre_mix, g_post_mix, g_pre_ffn, g_post_ffn,
           w_ffn_in, w_ffn_out):
    bp, seq, d = x_prompt.shape
    bs, ts, _ = x_sample.shape
    depth = w_in.shape[0]
    lp = N_META + seq + FRONT_PAD
    assert lp % BLOCK == 0 and BLOCK % CHUNK == 0 and seq % BLOCK == 0
    nblk = lp // BLOCK
    blk0 = nblk - 1
    pad_range = (seq, seq + FRONT_PAD)

    zpad = jnp.zeros((bp, FRONT_PAD, d), x_prompt.dtype)
    meta = jnp.broadcast_to(meta_tokens.astype(x_prompt.dtype)[None], (bp, N_META, d))
    xp = jnp.concatenate([x_prompt, zpad, meta], axis=1)
    xs = x_sample.reshape(1, bs * ts, d)

    cos_p, sin_p = _rope_tables(jnp.arange(lp, dtype=jnp.int32) - FRONT_PAD)
    cos_s, sin_s = _rope_tables(PAST_LEN + jnp.arange(ts, dtype=jnp.int32))

    gw = (SSM_HEADS // SSM_GROUPS) * SSM_HEADDIM
    nseq_s = _pick_tile(bs, 8) if bs % SUBLANES == 0 else bs
    zero_dn = jnp.zeros((1, bp, DN_HEADS, DN_DK, DN_DV), F32)
    zero_dnc = jnp.zeros((bp, SUBLANES, DN_CONV), F32)
    zero_ssm = jnp.zeros((1, bp, SSM_GROUPS, gw, SSM_STATE), F32)
    zero_ssmc = jnp.zeros((bp, SUBLANES, SSM_CONV), F32)
    state_ssm_g = state_ssm.reshape(depth, bs, SSM_GROUPS, gw, SSM_STATE)
    cache_k = cache_swa_k.reshape(depth, bs, WINDOW, SWA_KV)
    cache_v = cache_swa_v.reshape(depth, bs, WINDOW, SWA_KV)

    w_in_r = _reorder_w_in(w_in.astype(BF16))
    w_out_b = w_out.astype(BF16)
    w_fi_b = w_ffn_in.astype(BF16)
    w_fo_b = w_ffn_out.astype(BF16)
    g1, g2, g3, g4 = (a[:, None, :] for a in (g_pre_mix, g_post_mix, g_pre_ffn, g_post_ffn))

    new_p, new_s = [], []
    dn_s = ssm_s = k_s = v_s = None
    for l in range(depth):
        prm = jnp.concatenate([
            _lane_row([(SM_A, dn_a_log[l])]),
            _lane_row([(SM_A, dn_dt_bias[l])]),
            _lane_row([(SM_DT, ssm_a_log[l])]),
            _lane_row([(SM_DT, ssm_dt_bias[l])]),
            _lane_row([(0, swa_sinks[l])]),
            jnp.zeros((SUBLANES - 5, LANES), F32)], axis=0)
        dn_nw = dn_norm_w[l][None, :]
        ssm_nw = ssm_norm_w[l][None, :]
        drow = jnp.repeat(ssm_d[l], SSM_HEADDIM)[None, :]
        cbias = ssm_conv_b[l][None, :]
        last = l == depth - 1

        proj = _inproj(xp, g1, w_in_r, l, pad_range, TM_DENSE)
        (odn, dn_p, dnc_p), = _run_parts([_dn_mixer(proj, nblk, blk0, BLOCK, CHUNK, FRONT_PAD, bp, zero_dn, 0,
                                                    zero_dnc, dn_conv_w[l], prm, dn_nw)], "dn_mixer")
        (ys, ssm_p, ssmc_p), = _run_parts([_ssd_mixer(proj, nblk, blk0, BLOCK, CHUNK, FRONT_PAD, bp, zero_ssm, 0,
                                                      zero_ssmc, ssm_conv_w[l], cbias, prm, drow, ssm_nw)],
                                          "ssd_mixer")
        (osw, k_p, v_p), = _run_parts([_swa_prompt(proj, nblk, blk0, FRONT_PAD, cos_p, sin_p, prm)], "swa_prompt")
        xp = _tail(xp, odn, ys, osw, w_out_b, g2, g3, g4, w_fi_b, w_fo_b, l, seq if last else lp, TM_DENSE)
        new_p.append((dn_p, dnc_p[:, -(CONV_WIDTH - 1):], ssm_p.reshape(bp, SSM_HEADS, SSM_HEADDIM, SSM_STATE),
                      ssmc_p[:, -(CONV_WIDTH - 1):], k_p.reshape(bp, WINDOW, SWA_KV_HEADS, SWA_HEAD_DIM),
                      v_p.reshape(bp, WINDOW, SWA_KV_HEADS, SWA_HEAD_DIM)))

        proj = _inproj(xs, g1, w_in_r, l, None, TM_DENSE).reshape(bs, ts, D_PROJ)
        (odn, dn_s, dnc_s), (ys, ssm_s, ssmc_s), (osw, k_s, v_s) = _run_parts([
            _dn_mixer(proj, 1, 0, ts, ts, 0, nseq_s, state_dn, l, _pad_conv_state(state_dn_conv[l]), dn_conv_w[l],
                      prm, dn_nw, stack=(depth, dn_s)),
            _ssd_mixer(proj, 1, 0, ts, ts, 0, nseq_s, state_ssm_g, l, _pad_conv_state(state_ssm_conv[l]),
                       ssm_conv_w[l], cbias, prm, drow, ssm_nw, stack=(depth, ssm_s)),
            _swa_sample(proj, nseq_s, cache_k, cache_v, l, cos_s, sin_s, prm,
                        stack_k=(depth, k_s), stack_v=(depth, v_s))], "mixers_sample")
        flat = lambda a: a.reshape(1, bs * ts, a.shape[-1])
        xs = _tail(xs, flat(odn), flat(ys), flat(osw), w_out_b, g2, g3, g4, w_fi_b, w_fo_b, l, bs * ts, TM_DENSE)
        new_s.append((dnc_s[:, -(CONV_WIDTH - 1):], ssmc_s[:, -(CONV_WIDTH - 1):]))

    outs_p = tuple(jnp.stack([st[i] for st in new_p]) for i in range(6))
    dnc_s, ssmc_s = (jnp.stack([st[i] for st in new_s]) for i in range(2))
    outs_s = (dn_s, dnc_s, ssm_s.reshape(depth, bs, SSM_HEADS, SSM_HEADDIM, SSM_STATE), ssmc_s,
              k_s.reshape(depth, bs, WINDOW, SWA_KV_HEADS, SWA_HEAD_DIM),
              v_s.reshape(depth, bs, WINDOW, SWA_KV_HEADS, SWA_HEAD_DIM))
    return (xp, xs.reshape(bs, ts, d)) + outs_p + outs_s
```

```python
import functools

import jax
import jax.numpy as jnp
from jax import lax
from jax.experimental import pallas as pl
from jax.experimental.pallas import tpu as pltpu

F32 = jnp.float32
BF16 = jnp.bfloat16
NT = (((1,), (1,)), ((), ()))
TN = (((0,), (0,)), ((), ()))

N_META = 16
CONV_WIDTH = 4
CHUNK = 64
BLOCK = 128
WINDOW = 128
FRONT_PAD = BLOCK - N_META
ROPE_THETA = 10000.0
PAST_LEN = 8192
EPS = 1e-6

DN_HEADS, DN_DK, DN_DV = 4, 128, 128
DN_QK = DN_HEADS * DN_DK
DN_V = DN_HEADS * DN_DV
DN_CONV = 2 * DN_QK + DN_V
SSM_HEADS, SSM_HEADDIM, SSM_GROUPS, SSM_STATE = 4, 64, 2, 128
SSM_INNER = SSM_HEADS * SSM_HEADDIM
SSM_BC = SSM_GROUPS * SSM_STATE
SSM_CONV = SSM_INNER + 2 * SSM_BC
SWA_Q_HEADS, SWA_KV_HEADS, SWA_HEAD_DIM = 4, 2, 64
SWA_Q = SWA_Q_HEADS * SWA_HEAD_DIM
SWA_KV = SWA_KV_HEADS * SWA_HEAD_DIM
IN_WIDTHS = (DN_CONV, DN_V, DN_HEADS, DN_HEADS, SSM_CONV, SSM_INNER, SSM_HEADS, SWA_Q, SWA_KV, SWA_KV)

LANES = 128
SUBLANES = 8
COL_QKV = (0, DN_CONV)
COL_DNZ = (1536, DN_V)
COL_SSZ = (2048, SSM_INNER)
COL_XBC = (2304, SSM_CONV)
COL_SWQ = (3072, SWA_Q)
COL_SWK = (3328, SWA_KV)
COL_SWV = (3456, SWA_KV)
COL_SM = (3584, LANES)
D_PROJ = 3712
SM_B, SM_A, SM_DT = 0, 4, 8
PRM_DN_ALOG, PRM_DN_DTB, PRM_SSM_ALOG, PRM_SSM_DTB, PRM_SINK = 0, 1, 2, 3, 4
NEG_BIG = -1e30
VMEM_LIMIT = 56 * 1024 * 1024
TM_DENSE = 640


def _bdot(a, b, dims=None):
    a = a.astype(BF16)
    b = b.astype(BF16)
    if dims is None:
        return jnp.dot(a, b, preferred_element_type=F32)
    return lax.dot_general(a, b, dims, preferred_element_type=F32)


def _cumsum_rows(lmat, g):
    hi = g.astype(BF16)
    r1 = g - hi.astype(F32)
    mid = r1.astype(BF16)
    lo = (r1 - mid.astype(F32)).astype(BF16)
    dot = lambda part: jnp.dot(lmat, part, preferred_element_type=F32)
    return dot(hi) + dot(mid) + dot(lo)


def _rmsnorm(x, g):
    return x * lax.rsqrt(jnp.mean(x * x, axis=-1, keepdims=True) + EPS) * g


def _l2norm(x):
    return x * lax.rsqrt(jnp.sum(x * x, axis=-1, keepdims=True) + EPS)


def _sigmoid(x):
    return 1.0 / (1.0 + jnp.exp(-x))


def _silu(x):
    return x * _sigmoid(x)


def _softplus(x):
    return jnp.maximum(x, 0.0) + jnp.log1p(jnp.exp(-jnp.abs(x)))


def _pick_tile(n, target):
    best = None
    for t in range(SUBLANES, min(n, target) + 1, SUBLANES):
        if n % t == 0:
            best = t
    assert best is not None, n
    return best


def _inproj_kernel(x_ref, g_ref, w_ref, o_ref, *, tm, pad_range):
    h = _rmsnorm(x_ref[...], g_ref[...])
    if pad_range is not None:
        r = pl.program_id(1) * tm + lax.broadcasted_iota(jnp.int32, (tm, 1), 0)
        is_pad = (r >= pad_range[0]) & (r < pad_range[1])
        h = jnp.where(is_pad, 0.0, h)
    o_ref[...] = jnp.dot(h.astype(BF16), w_ref[...], preferred_element_type=F32)


def _inproj(x, g, w, layer, pad_range, tm_target):
    nb, rows, d = x.shape
    tm = _pick_tile(rows, tm_target)
    return pl.pallas_call(
        functools.partial(_inproj_kernel, tm=tm, pad_range=pad_range),
        out_shape=jax.ShapeDtypeStruct((nb, rows, D_PROJ), F32),
        grid=(nb, rows // tm),
        in_specs=[pl.BlockSpec((None, tm, d), lambda b, i: (b, i, 0)), _layer_spec(g.shape, layer),
                  _layer_spec(w.shape, layer)],
        out_specs=pl.BlockSpec((None, tm, D_PROJ), lambda b, i: (b, i, 0)),
        compiler_params=pltpu.CompilerParams(dimension_semantics=("arbitrary", "arbitrary"),
                                             vmem_limit_bytes=VMEM_LIMIT),
        name="inproj",
    )(x, g, w)


def _causal_conv(xbuf, raw_ref, cw_ref, cbo_ref, rows):
    x = raw_ref[...]
    prev = xbuf[0:SUBLANES, :]
    cw = cw_ref[...]
    row = lax.broadcasted_iota(jnp.int32, (SUBLANES, 1), 0)
    acc = None
    for i in range(CONV_WIDTH):
        s = CONV_WIDTH - 1 - i
        if s == 0:
            xs = x
        else:
            r = pltpu.roll(x, s, axis=0)
            head = jnp.where(row < s, pltpu.roll(prev, s, axis=0), r[0:SUBLANES])
            xs = head if rows == SUBLANES else jnp.concatenate([head, r[SUBLANES:]], axis=0)
        term = xs * cw[i:i + 1, :]
        acc = term if acc is None else acc + term
    tail = x[rows - SUBLANES:rows]
    cbo_ref[...] = tail
    xbuf[0:SUBLANES, :] = tail
    return acc


def _chunk_masks(c):
    ii = lax.broadcasted_iota(jnp.int32, (c, c), 0)
    jj = lax.broadcasted_iota(jnp.int32, (c, c), 1)
    return ii >= jj, ii > jj, (ii == jj).astype(F32)


def _segment_decay(gc, gct, lane, ge):
    col = gc[:, lane:lane + 1]
    row = gct[lane:lane + 1, :]
    return jnp.where(ge, jnp.exp(jnp.where(ge, col - row, 0.0)), 0.0)


def _inv_unit_lower_minus_eye(a_list, c, nh):
    w = nh * c
    blk_r = lax.broadcasted_iota(jnp.int32, (w, w), 0) // c
    blk_c = lax.broadcasted_iota(jnp.int32, (w, w), 1) // c
    same = blk_r == blk_c

    def block_diag(p):
        return jnp.where(same, jnp.concatenate([p] * nh, axis=0), 0.0).astype(BF16)

    ys = [-a for a in a_list]
    ps = [_bdot(a, block_diag(a)) for a in a_list]
    yield
    n = 2
    while n < c:
        pbds = [block_diag(p) for p in ps]
        n *= 2
        if n < c:
            sts = [_bdot(jnp.concatenate([y, p], axis=0), pbd) for y, p, pbd in zip(ys, ps, pbds)]
            ys = [y + p + st[:c] for y, p, st in zip(ys, ps, sts)]
            ps = [st[c:] for st in sts]
        else:
            ys = [y + p + _bdot(y, pbd) for y, p, pbd in zip(ys, ps, pbds)]
        yield
    return ys


def _run_parts(parts, name):
    grid = parts[0]["grid"]
    assert all(p["grid"] == grid for p in parts)
    n_in = [len(p["inputs"]) for p in parts]
    n_out = [len(p["out_shape"]) for p in parts]
    n_scr = [len(p["scratch"]) for p in parts]
    aliases = {}
    for k, p in enumerate(parts):
        for i, o in p["aliases"].items():
            aliases[sum(n_in[:k]) + i] = sum(n_out[:k]) + o

    def kernel(*refs):
        ins = refs[:sum(n_in)]
        outs = refs[sum(n_in):sum(n_in) + sum(n_out)]
        scr = refs[sum(n_in) + sum(n_out):]
        gens = [p["body"](ins[sum(n_in[:k]):sum(n_in[:k + 1])], outs[sum(n_out[:k]):sum(n_out[:k + 1])],
                          scr[sum(n_scr[:k]):sum(n_scr[:k + 1])]) for k, p in enumerate(parts)]
        for tag in gens[0]:
            if tag == "chain":
                break
        live = list(gens)
        while live:
            for g in list(live):
                if next(g, StopIteration) is StopIteration:
                    live.remove(g)

    results = pl.pallas_call(
        kernel,
        out_shape=tuple(s for p in parts for s in p["out_shape"]),
        grid=grid,
        in_specs=[s for p in parts for s in p["in_specs"]],
        out_specs=tuple(s for p in parts for s in p["out_specs"]),
        scratch_shapes=[s for p in parts for s in p["scratch"]],
        input_output_aliases=aliases,
        compiler_params=pltpu.CompilerParams(dimension_semantics=("arbitrary",) * len(grid),
                                             vmem_limit_bytes=VMEM_LIMIT),
        name=name,
    )(*[a for p in parts for a in p["inputs"]])
    return [list(results[sum(n_out[:k]):sum(n_out[:k + 1])]) for k in range(len(parts))]


def _state_out(stack, layer, nb, nseq, tail, out_index, n_inputs):
    zeros = (0,) * len(tail)
    if stack is None:
        return ((nb,) + tail, pl.BlockSpec((nseq,) + tail, lambda b, *_: (b,) + zeros), [], [], {}, None, 0)
    depth, prev = stack
    shape = (depth, nb) + tail
    if prev is None:
        spec = pl.BlockSpec((depth, nseq) + tail, lambda b, *_: (0, b) + zeros)
        return (shape, spec, [], [], {}, layer, depth)
    spec = pl.BlockSpec((None, nseq) + tail, lambda b, *_: (layer, b) + zeros)
    return (shape, spec, [prev], [pl.BlockSpec(memory_space=pl.ANY)], {n_inputs: out_index}, None, 0)


def _state_view(ref, slot, n_slots):
    if slot is None:
        return ref
    for other in range(n_slots):
        if other != slot:
            ref[other] = jnp.zeros(ref.shape[1:], ref.dtype)
    return ref.at[slot]


def _dn_prep(items, lmat, ge, gt_all, chunk):
    heads = range(DN_HEADS)
    gcs = [_cumsum_rows(lmat, g_all) for _, _, g_all in items]
    gcts = [gc.T for gc in gcs]
    qs = [[_l2norm(qkv[:, h * DN_DK:(h + 1) * DN_DK]) * DN_DK ** -0.5 for h in heads] for qkv, _, _ in items]
    ks = [[_l2norm(qkv[:, DN_QK + h * DN_DK:DN_QK + (h + 1) * DN_DK]) for h in heads] for qkv, _, _ in items]
    vs = [[qkv[:, 2 * DN_QK + h * DN_DV:2 * DN_QK + (h + 1) * DN_DV] for h in heads] for qkv, _, _ in items]
    betas = [[beta_all[:, SM_B + h:SM_B + h + 1] for h in heads] for _, beta_all, _ in items]
    yield
    kbs = [[k.astype(BF16) for k in kk] for kk in ks]
    kks = [jnp.concatenate([_bdot(kb, kb, NT) for kb in kb4], axis=1) for kb4 in kbs]
    qks = [jnp.concatenate([_bdot(q, kb, NT) for q, kb in zip(q4, kb4)], axis=1) for q4, kb4 in zip(qs, kbs)]
    decs = [jnp.concatenate([_segment_decay(gc, gct, SM_A + h, ge) for h in heads], axis=1)
            for gc, gct in zip(gcs, gcts)]
    beta_ws = [jnp.concatenate([jnp.broadcast_to(b, (chunk, chunk)) for b in b4], axis=1) for b4 in betas]
    a_list = [jnp.where(gt_all, bw * kk * dec, 0.0) for bw, kk, dec in zip(beta_ws, kks, decs)]
    yield "chain"
    n = len(items)
    egs, ekds, egls, attns, rhss, qds, kds = ([None] * n for _ in range(7))

    def side_work(i):
        gc = gcs[i]
        glast = gc[chunk - 1:chunk, :]
        egs[i] = jnp.exp(gc)
        ekds[i] = jnp.exp(glast - gc)
        egls[i] = jnp.exp(glast)
        attns[i] = (qks[i] * decs[i]).astype(BF16)
        lane = lambda a, h: a[:, SM_A + h:SM_A + h + 1]
        rhss[i] = [jnp.concatenate([vs[i][h] * betas[i][h], ks[i][h] * (betas[i][h] * lane(egs[i], h))], axis=1)
                   for h in heads]
        qds[i] = [(qs[i][h] * lane(egs[i], h)).astype(BF16) for h in heads]
        kds[i] = [(ks[i][h] * lane(ekds[i], h)).astype(BF16) for h in heads]

    inverse = _inv_unit_lower_minus_eye(a_list, chunk, DN_HEADS)
    todo = list(range(n))
    while True:
        try:
            next(inverse)
        except StopIteration as done:
            tms = done.value
            break
        if todo:
            side_work(todo.pop(0))
        yield
    for i in todo:
        side_work(i)
    uws = [[rhss[i][h] + _bdot(tms[i][:, h * chunk:(h + 1) * chunk], rhss[i][h]) for h in heads] for i in range(n)]
    yield
    out = []
    for i in range(n):
        per_head = []
        for h in heads:
            uw = uws[i][h]
            wq = jnp.concatenate([uw[:, DN_DV:].astype(BF16), qds[i][h]], axis=0)
            per_head.append((uw[:, :DN_DV], wq, attns[i][:, h * chunk:(h + 1) * chunk], kds[i][h],
                             egls[i][:, SM_A + h:SM_A + h + 1]))
        out.append(per_head)
    return out


def _dn_body(ins, outs, scratch, *, nseq, rows, chunk, front_pad, slot, n_slots):
    qkv_ref, z_ref, sm_ref, s0_ref, cb_ref, cw_ref, prm_ref, nw_ref = ins[:8]
    o_ref, so_full, cbo_ref = outs
    xbuf, = scratch
    so_ref = _state_view(so_full, slot, n_slots)
    blk = pl.program_id(1)
    first = blk == 0

    @pl.when(first)
    def _():
        so_ref[...] = s0_ref[...]
        xbuf[:, 0:SUBLANES, :] = cb_ref[...]

    prm = prm_ref[...]
    nw = nw_ref[...]
    ge, _, _ = _chunk_masks(chunk)
    lmat = jnp.where(ge, 1.0, 0.0).astype(BF16)
    wide = (chunk, DN_HEADS * chunk)
    gt_all = lax.broadcasted_iota(jnp.int32, wide, 0) > lax.broadcasted_iota(jnp.int32, wide, 1) % chunk
    nchunk = rows // chunk

    items = []
    for s in range(nseq):
        qkv = _silu(_causal_conv(xbuf.at[s], qkv_ref.at[s], cw_ref, cbo_ref.at[s], rows))
        sm = sm_ref[s]
        beta_all = _sigmoid(sm)
        g_all = -jnp.exp(prm[PRM_DN_ALOG:PRM_DN_ALOG + 1, :]) * _softplus(sm + prm[PRM_DN_DTB:PRM_DN_DTB + 1, :])
        if front_pad:
            pos = blk * rows + lax.broadcasted_iota(jnp.int32, (rows, 1), 0)
            g_all = jnp.where(pos < front_pad, 0.0, g_all)
        for c in range(nchunk):
            cs = slice(c * chunk, (c + 1) * chunk)
            items.append((qkv[cs], beta_all[cs], g_all[cs]))
        yield
    prep = yield from _dn_prep(items, lmat, ge, gt_all, chunk)

    chains = [(s, h) for s in range(nseq) for h in range(DN_HEADS)]
    states = [so_ref[s, h] for s, h in chains]
    for c in range(nchunk):
        r0 = c * chunk
        fac = [prep[s * nchunk + c][h] for s, h in chains]
        m1s = [_bdot(f[1], st) for f, st in zip(fac, states)]
        yield
        v_news = [(f[0] - m1[:chunk]).astype(BF16) for f, m1 in zip(fac, m1s)]
        ups = [_bdot(f[3], v, TN) for f, v in zip(fac, v_news)]
        os_ = [m1[chunk:] + _bdot(f[2], v) for f, m1, v in zip(fac, m1s, v_news)]
        yield
        states = [st * f[4] + up for f, st, up in zip(fac, states, ups)]
        for (s, h), o in zip(chains, os_):
            lo = h * DN_DV
            zh = z_ref[s, r0:r0 + chunk, lo:lo + DN_DV]
            o_ref[s, r0:r0 + chunk, lo:lo + DN_DV] = _rmsnorm(o, nw) * _silu(zh)
        yield
    for (s, h), st in zip(chains, states):
        so_ref[s, h] = st


def _dn_mixer(proj, nblk, blk0, rows, chunk, front_pad, nseq, s0, layer, cbuf, cw, prm, nw, stack=None):
    nb = proj.shape[0]
    col = lambda c: c[0] // c[1]
    phys = lambda i: (i + blk0) % nblk
    so_shape, so_spec, extra, extra_specs, aliases, slot, n_slots = _state_out(
        stack, layer, nb, nseq, (DN_HEADS, DN_DK, DN_DV), 1, 8)
    return dict(
        body=functools.partial(_dn_body, nseq=nseq, rows=rows, chunk=chunk, front_pad=front_pad,
                               slot=slot, n_slots=n_slots),
        grid=(nb // nseq, nblk),
        inputs=[proj, proj, proj, s0, cbuf, cw, prm, nw] + extra,
        out_shape=[jax.ShapeDtypeStruct((nb, nblk * rows, DN_V), F32),
                   jax.ShapeDtypeStruct(so_shape, F32),
                   jax.ShapeDtypeStruct((nb, SUBLANES, DN_CONV), F32)],
        in_specs=[
            pl.BlockSpec((nseq, rows, DN_CONV), lambda b, i: (b, phys(i), col(COL_QKV))),
            pl.BlockSpec((nseq, rows, DN_V), lambda b, i: (b, phys(i), col(COL_DNZ))),
            pl.BlockSpec((nseq, rows, LANES), lambda b, i: (b, phys(i), col(COL_SM))),
            pl.BlockSpec((None, nseq, DN_HEADS, DN_DK, DN_DV), lambda b, i: (layer, b, 0, 0, 0)),
            pl.BlockSpec((nseq, SUBLANES, DN_CONV), lambda b, i: (b, 0, 0)),
            pl.BlockSpec((CONV_WIDTH, DN_CONV), lambda b, i: (0, 0)),
            pl.BlockSpec((SUBLANES, LANES), lambda b, i: (0, 0)),
            pl.BlockSpec((1, DN_DV), lambda b, i: (0, 0)),
        ] + extra_specs,
        out_specs=[
            pl.BlockSpec((nseq, rows, DN_V), lambda b, i: (b, phys(i), 0)),
            so_spec,
            pl.BlockSpec((nseq, SUBLANES, DN_CONV), lambda b, i: (b, 0, 0)),
        ],
        scratch=[pltpu.VMEM((nseq, SUBLANES, DN_CONV), F32)],
        aliases=aliases,
    )


def _ssd_body(ins, outs, scratch, *, nseq, rows, chunk, front_pad, slot, n_slots):
    xbc_ref, z_ref, sm_ref, h0_ref, cb_ref, cw_ref, cbias_ref, prm_ref, drow_ref, nw_ref = ins[:10]
    y_ref, ho_full, cbo_ref = outs
    xbuf, = scratch
    ho_ref = _state_view(ho_full, slot, n_slots)
    blk = pl.program_id(1)

    @pl.when(blk == 0)
    def _():
        ho_ref[...] = h0_ref[...]
        xbuf[:, 0:SUBLANES, :] = cb_ref[...]

    prm = prm_ref[...]
    nw = nw_ref[...]
    drow = drow_ref[...]
    cbias = cbias_ref[...]
    ge, _, _ = _chunk_masks(chunk)
    lmat = jnp.where(ge, 1.0, 0.0).astype(BF16)
    hpg = SSM_HEADS // SSM_GROUPS
    gw = hpg * SSM_HEADDIM
    lane = lax.broadcasted_iota(jnp.int32, (1, gw), 1)
    srow = lax.broadcasted_iota(jnp.int32, (gw, 1), 0)
    in_head = [(lane >= j * SSM_HEADDIM) & (lane < (j + 1) * SSM_HEADDIM) for j in range(hpg)]
    nchunk = rows // chunk
    groups = range(SSM_GROUPS)

    items = []
    for s in range(nseq):
        act = _silu(_causal_conv(xbuf.at[s], xbc_ref.at[s], cw_ref, cbo_ref.at[s], rows) + cbias)
        dt_all = _softplus(sm_ref[s] + prm[PRM_SSM_DTB:PRM_SSM_DTB + 1, :])
        if front_pad:
            pos = blk * rows + lax.broadcasted_iota(jnp.int32, (rows, 1), 0)
            dt_all = jnp.where(pos < front_pad, 0.0, dt_all)
        g_all = dt_all * (-jnp.exp(prm[PRM_SSM_ALOG:PRM_SSM_ALOG + 1, :]))
        for c in range(nchunk):
            cs = slice(c * chunk, (c + 1) * chunk)
            items.append((act[cs], dt_all[cs], g_all[cs]))
        yield
    n_items = len(items)
    gcs = [_cumsum_rows(lmat, g) for _, _, g in items]
    gcts = [gc.T for gc in gcs]
    xgs = [[a[:, g * gw:(g + 1) * gw] for g in groups] for a, _, _ in items]
    bgs = [[a[:, SSM_INNER + g * SSM_STATE:SSM_INNER + (g + 1) * SSM_STATE] for g in groups] for a, _, _ in items]
    cgs = [[a[:, SSM_INNER + SSM_BC + g * SSM_STATE:SSM_INNER + SSM_BC + (g + 1) * SSM_STATE] for g in groups]
           for a, _, _ in items]
    cbs = [[_bdot(cgs[i][g], bgs[i][g], NT) for g in groups] for i in range(n_items)]
    yield
    egs = [jnp.exp(gc) for gc in gcs]
    ekds = [jnp.exp(gc[chunk - 1:chunk, :] - gc) for gc in gcs]
    egls = [jnp.exp(gc[chunk - 1:chunk, :]) for gc in gcs]
    heads = [(g, j) for g in groups for j in range(hpg)]
    ln = lambda g, j: SM_DT + g * hpg + j
    xdts = [[jnp.where(in_head[j], xgs[i][g] * items[i][1][:, ln(g, j):ln(g, j) + 1], 0.0).astype(BF16)
             for g, j in heads] for i in range(n_items)]
    attns = [[cbs[i][g] * _segment_decay(gcs[i], gcts[i], ln(g, j), ge) for g, j in heads] for i in range(n_items)]
    y_intras = [[_bdot(attns[i][k], xdts[i][k]) for k in range(len(heads))] for i in range(n_items)]
    yield
    upds = [[_bdot(xdts[i][k], bgs[i][g] * ekds[i][:, ln(g, j):ln(g, j) + 1], TN) for k, (g, j) in enumerate(heads)]
            for i in range(n_items)]
    yield
    cds = [[jnp.concatenate([cgs[i][g] * egs[i][:, ln(g, j):ln(g, j) + 1] for j in range(hpg)], axis=0).astype(BF16)
            for g in groups] for i in range(n_items)]
    y_loc = [[sum(y_intras[i][g * hpg + j] for j in range(hpg)) for g in groups] for i in range(n_items)]
    h_inc = [[sum(upds[i][g * hpg + j] for j in range(hpg)) for g in groups] for i in range(n_items)]
    gl_cols = []
    for i in range(n_items):
        per_group = []
        for g in groups:
            gl = egls[i][:, ln(g, 0):ln(g, 0) + 1]
            for j in range(1, hpg):
                gl = jnp.where(srow < j * SSM_HEADDIM, gl, egls[i][:, ln(g, j):ln(g, j) + 1])
            per_group.append(gl)
        gl_cols.append(per_group)

    chains = [(s, g) for s in range(nseq) for g in groups]
    states = [ho_ref[s, g] for s, g in chains]
    for c in range(nchunk):
        r0 = c * chunk
        idx = [s * nchunk + c for s, _ in chains]
        yis = [_bdot(cds[i][g], st, NT) for i, (_, g), st in zip(idx, chains, states)]
        yield
        states = [st * gl_cols[i][g] + h_inc[i][g] for i, (_, g), st in zip(idx, chains, states)]
        for i, (s, g), yi in zip(idx, chains, yis):
            y_inter = yi[0:chunk]
            for j in range(1, hpg):
                y_inter = jnp.where(in_head[j], yi[j * chunk:(j + 1) * chunk], y_inter)
            yg = y_loc[i][g] + y_inter + xgs[i][g] * drow[:, g * gw:(g + 1) * gw]
            yg = yg * _silu(z_ref[s, r0:r0 + chunk, g * gw:(g + 1) * gw])
            y_ref[s, r0:r0 + chunk, g * gw:(g + 1) * gw] = _rmsnorm(yg, nw[:, g * gw:(g + 1) * gw])
        yield
    for (s, g), st in zip(chains, states):
        ho_ref[s, g] = st


def _ssd_mixer(proj, nblk, blk0, rows, chunk, front_pad, nseq, h0, layer, cbuf, cw, cbias, prm, drow, nw,
               stack=None):
    nb = proj.shape[0]
    col = lambda c: c[0] // c[1]
    phys = lambda i: (i + blk0) % nblk
    gw = (SSM_HEADS // SSM_GROUPS) * SSM_HEADDIM
    ho_shape, ho_spec, extra, extra_specs, aliases, slot, n_slots = _state_out(
        stack, layer, nb, nseq, (SSM_GROUPS, gw, SSM_STATE), 1, 10)
    return dict(
        body=functools.partial(_ssd_body, nseq=nseq, rows=rows, chunk=chunk, front_pad=front_pad,
                               slot=slot, n_slots=n_slots),
        grid=(nb // nseq, nblk),
        inputs=[proj, proj, proj, h0, cbuf, cw, cbias, prm, drow, nw] + extra,
        out_shape=[jax.ShapeDtypeStruct((nb, nblk * rows, SSM_INNER), F32),
                   jax.ShapeDtypeStruct(ho_shape, F32),
                   jax.ShapeDtypeStruct((nb, SUBLANES, SSM_CONV), F32)],
        in_specs=[
            pl.BlockSpec((nseq, rows, SSM_CONV), lambda b, i: (b, phys(i), col(COL_XBC))),
            pl.BlockSpec((nseq, rows, SSM_INNER), lambda b, i: (b, phys(i), col(COL_SSZ))),
            pl.BlockSpec((nseq, rows, LANES), lambda b, i: (b, phys(i), col(COL_SM))),
            pl.BlockSpec((None, nseq, SSM_GROUPS, gw, SSM_STATE), lambda b, i: (layer, b, 0, 0, 0)),
            pl.BlockSpec((nseq, SUBLANES, SSM_CONV), lambda b, i: (b, 0, 0)),
            pl.BlockSpec((CONV_WIDTH, SSM_CONV), lambda b, i: (0, 0)),
            pl.BlockSpec((1, SSM_CONV), lambda b, i: (0, 0)),
            pl.BlockSpec((SUBLANES, LANES), lambda b, i: (0, 0)),
            pl.BlockSpec((1, SSM_INNER), lambda b, i: (0, 0)),
            pl.BlockSpec((1, SSM_INNER), lambda b, i: (0, 0)),
        ] + extra_specs,
        out_specs=[
            pl.BlockSpec((nseq, rows, SSM_INNER), lambda b, i: (b, phys(i), 0)),
            ho_spec,
            pl.BlockSpec((nseq, SUBLANES, SSM_CONV), lambda b, i: (b, 0, 0)),
        ],
        scratch=[pltpu.VMEM((nseq, SUBLANES, SSM_CONV), F32)],
        aliases=aliases,
    )


def _rope(x, cos, sin_signed):
    w = x.shape[-1]
    half = SWA_HEAD_DIM // 2
    lane = lax.broadcasted_iota(jnp.int32, (1, w), 1)
    first_half = (lane % SWA_HEAD_DIM) < half
    swapped = jnp.where(first_half, pltpu.roll(x, w - half, axis=1), pltpu.roll(x, half, axis=1))
    return x * cos + swapped * sin_signed


def _sink_attend(problems):
    scale = SWA_HEAD_DIM ** -0.5
    scores = [[jnp.where(m, _bdot(q, k, NT) * scale, NEG_BIG) for k, m in zip(keys, masks)]
              for q, keys, _, masks, _ in problems]
    yield
    outs = []
    probs, dens = [], []
    def lane_reduce(tiles, combine, reduce):
        merged = {}
        for t in tiles:
            merged[t.shape[-1]] = t if t.shape[-1] not in merged else combine(merged[t.shape[-1]], t)
        return [reduce(t, axis=-1, keepdims=True) for t in merged.values()]

    for (_, _, _, _, sink), ss in zip(problems, scores):
        mx = sink
        for m in lane_reduce(ss, jnp.maximum, jnp.max):
            mx = jnp.maximum(mx, m)
        ps = [jnp.exp(s - mx) for s in ss]
        den = jnp.exp(sink - mx)
        for d in lane_reduce(ps, jnp.add, jnp.sum):
            den = den + d
        probs.append(ps)
        dens.append(den)
    yield
    pvs = [[_bdot(p, v) for p, v in zip(ps, vals)] for (_, _, vals, _, _), ps in zip(problems, probs)]
    yield
    for pv, den in zip(pvs, dens):
        acc = pv[0]
        for extra in pv[1:]:
            acc = acc + extra
        outs.append(acc / den)
    return outs


def _swa_problems(q, key_sets, val_sets, masks, prm, tq):
    grp = SWA_Q_HEADS // SWA_KV_HEADS
    row = lax.broadcasted_iota(jnp.int32, (grp * tq, 1), 0)
    problems = []
    for j in range(SWA_KV_HEADS):
        ks = slice(j * SWA_HEAD_DIM, (j + 1) * SWA_HEAD_DIM)
        heads = [j * grp + g for g in range(grp)]
        qst = jnp.concatenate([q[:, h * SWA_HEAD_DIM:(h + 1) * SWA_HEAD_DIM] for h in heads], axis=0)
        sink = prm[PRM_SINK:PRM_SINK + 1, heads[0]:heads[0] + 1]
        for g in range(1, grp):
            sink = jnp.where(row < g * tq, sink, prm[PRM_SINK:PRM_SINK + 1, heads[g]:heads[g] + 1])
        problems.append((qst, [k[:, ks] for k in key_sets], [v[:, ks] for v in val_sets], masks, sink))
    return problems


def _swa_store(o_ref_at, outs, tq):
    grp = SWA_Q_HEADS // SWA_KV_HEADS
    for j, o in enumerate(outs):
        for g in range(grp):
            h = j * grp + g
            o_ref_at(slice(h * SWA_HEAD_DIM, (h + 1) * SWA_HEAD_DIM), o[g * tq:(g + 1) * tq])


def _swa_prompt_body(ins, outs, scratch, *, nseq, front_pad):
    q_ref, k_ref, v_ref, cos_ref, sin_ref, prm_ref = ins
    o_ref, ko_ref, vo_ref = outs
    kprev, vprev = scratch
    blk = pl.program_id(1)

    @pl.when(blk == 0)
    def _():
        kprev[...] = jnp.zeros_like(kprev)
        vprev[...] = jnp.zeros_like(vprev)

    cos = cos_ref[...]
    sin = sin_ref[...]
    cos_q = jnp.concatenate([cos, cos], axis=1)
    sin_q = jnp.concatenate([sin, sin], axis=1)
    grp = SWA_Q_HEADS // SWA_KV_HEADS
    qi = lax.broadcasted_iota(jnp.int32, (grp * BLOCK, BLOCK), 0) % BLOCK
    kj = lax.broadcasted_iota(jnp.int32, (grp * BLOCK, BLOCK), 1)
    mask_cur = (kj <= qi) & (blk * BLOCK + kj >= front_pad)
    mask_prev = (kj > qi) & ((blk - 1) * BLOCK + kj >= front_pad)
    prm = prm_ref[...]
    problems = []
    for s in range(nseq):
        q = _rope(q_ref[s], cos_q, sin_q)
        k = _rope(k_ref[s], cos, sin)
        v = v_ref[s]
        problems += _swa_problems(q, (kprev[s], k), (vprev[s], v), (mask_prev, mask_cur), prm, BLOCK)
        kprev[s] = k
        vprev[s] = v
        ko_ref[s] = k
        vo_ref[s] = v
        yield
    res = yield from _sink_attend(problems)
    for s in range(nseq):
        def put(cols, val, s=s):
            o_ref[s, :, cols] = val
        _swa_store(put, res[s * SWA_KV_HEADS:(s + 1) * SWA_KV_HEADS], BLOCK)


def _swa_prompt(proj, nblk, blk0, front_pad, cos, sin, prm):
    nb = proj.shape[0]
    col = lambda c: c[0] // c[1]
    phys = lambda i: (i + blk0) % nblk
    return dict(
        body=functools.partial(_swa_prompt_body, nseq=nb, front_pad=front_pad),
        grid=(1, nblk),
        inputs=[proj, proj, proj, cos, sin, prm],
        out_shape=[jax.ShapeDtypeStruct((nb, nblk * BLOCK, SWA_Q), F32),
                   jax.ShapeDtypeStruct((nb, WINDOW, SWA_KV), F32),
                   jax.ShapeDtypeStruct((nb, WINDOW, SWA_KV), F32)],
        in_specs=[
            pl.BlockSpec((nb, BLOCK, SWA_Q), lambda b, i: (0, phys(i), col(COL_SWQ))),
            pl.BlockSpec((nb, BLOCK, SWA_KV), lambda b, i: (0, phys(i), col(COL_SWK))),
            pl.BlockSpec((nb, BLOCK, SWA_KV), lambda b, i: (0, phys(i), col(COL_SWV))),
            pl.BlockSpec((BLOCK, SWA_KV), lambda b, i: (i, 0)),
            pl.BlockSpec((BLOCK, SWA_KV), lambda b, i: (i, 0)),
            pl.BlockSpec((SUBLANES, LANES), lambda b, i: (0, 0)),
        ],
        out_specs=[
            pl.BlockSpec((nb, BLOCK, SWA_Q), lambda b, i: (0, phys(i), 0)),
            pl.BlockSpec((nb, WINDOW, SWA_KV), lambda b, i: (0, 0, 0)),
            pl.BlockSpec((nb, WINDOW, SWA_KV), lambda b, i: (0, 0, 0)),
        ],
        scratch=[pltpu.VMEM((nb, BLOCK, SWA_KV), F32), pltpu.VMEM((nb, BLOCK, SWA_KV), F32)],
        aliases={},
    )


def _swa_sample_body(ins, outs, scratch, *, nseq, steps, slot, n_slots):
    q_ref, k_ref, v_ref, kc_ref, vc_ref, cos_ref, sin_ref, prm_ref = ins[:8]
    o_ref, ko_full, vo_full = outs
    ko_ref = _state_view(ko_full, slot, n_slots)
    vo_ref = _state_view(vo_full, slot, n_slots)
    cos = cos_ref[...]
    sin = sin_ref[...]
    cos_q = jnp.concatenate([cos, cos], axis=1)
    sin_q = jnp.concatenate([sin, sin], axis=1)
    prm = prm_ref[...]
    grp = SWA_Q_HEADS // SWA_KV_HEADS
    ti = lax.broadcasted_iota(jnp.int32, (grp * steps, WINDOW), 0) % steps
    sj = lax.broadcasted_iota(jnp.int32, (grp * steps, WINDOW), 1)
    mask_cache = sj > ti
    tn = lax.broadcasted_iota(jnp.int32, (grp * steps, steps), 0) % steps
    sn = lax.broadcasted_iota(jnp.int32, (grp * steps, steps), 1)
    mask_new = sn <= tn
    problems = []
    for b in range(nseq):
        q = _rope(q_ref[b], cos_q, sin_q)
        k = _rope(k_ref[b], cos, sin)
        v = v_ref[b]
        kc = kc_ref[b]
        vc = vc_ref[b]
        ko_ref[b, 0:WINDOW - steps, :] = kc[steps:WINDOW, :]
        ko_ref[b, WINDOW - steps:WINDOW, :] = k
        vo_ref[b, 0:WINDOW - steps, :] = vc[steps:WINDOW, :]
        vo_ref[b, WINDOW - steps:WINDOW, :] = v
        problems += _swa_problems(q, (kc, k), (vc, v), (mask_cache, mask_new), prm, steps)
        yield
    res = yield from _sink_attend(problems)
    for b in range(nseq):
        def put(cols, val, b=b):
            o_ref[b, :, cols] = val
        _swa_store(put, res[b * SWA_KV_HEADS:(b + 1) * SWA_KV_HEADS], steps)


def _swa_sample(proj, nseq, kc, vc, layer, cos, sin, prm, stack_k=None, stack_v=None):
    nb, steps, _ = proj.shape
    assert WINDOW > steps
    col = lambda c: c[0] // c[1]
    ko_shape, ko_spec, extra_k, specs_k, alias_k, slot, n_slots = _state_out(
        stack_k, layer, nb, nseq, (WINDOW, SWA_KV), 1, 8)
    vo_shape, vo_spec, extra_v, specs_v, alias_v, _, _ = _state_out(
        stack_v, layer, nb, nseq, (WINDOW, SWA_KV), 2, 8 + len(extra_k))
    return dict(
        body=functools.partial(_swa_sample_body, nseq=nseq, steps=steps, slot=slot, n_slots=n_slots),
        grid=(nb // nseq, 1),
        inputs=[proj, proj, proj, kc, vc, cos, sin, prm] + extra_k + extra_v,
        out_shape=[jax.ShapeDtypeStruct((nb, steps, SWA_Q), F32),
                   jax.ShapeDtypeStruct(ko_shape, F32),
                   jax.ShapeDtypeStruct(vo_shape, F32)],
        in_specs=[
            pl.BlockSpec((nseq, steps, SWA_Q), lambda b, i: (b, 0, col(COL_SWQ))),
            pl.BlockSpec((nseq, steps, SWA_KV), lambda b, i: (b, 0, col(COL_SWK))),
            pl.BlockSpec((nseq, steps, SWA_KV), lambda b, i: (b, 0, col(COL_SWV))),
            pl.BlockSpec((None, nseq, WINDOW, SWA_KV), lambda b, i: (layer, b, 0, 0)),
            pl.BlockSpec((None, nseq, WINDOW, SWA_KV), lambda b, i: (layer, b, 0, 0)),
            pl.BlockSpec((steps, SWA_KV), lambda b, i: (0, 0)),
            pl.BlockSpec((steps, SWA_KV), lambda b, i: (0, 0)),
            pl.BlockSpec((SUBLANES, LANES), lambda b, i: (0, 0)),
        ] + specs_k + specs_v,
        out_specs=[pl.BlockSpec((nseq, steps, SWA_Q), lambda b, i: (b, 0, 0)), ko_spec, vo_spec],
        scratch=[],
        aliases={**alias_k, **alias_v},
    )


def _tail_kernel(x_ref, odn_ref, y_ref, osw_ref, wout_ref, g1_ref, g2_ref, g3_ref, wfi_ref, wfo_ref, o_ref,
                 *, d_ff, tf):
    mixed = jnp.concatenate([odn_ref[...], y_ref[...], osw_ref[...]], axis=1).astype(BF16)
    m = jnp.dot(mixed, wout_ref[...], preferred_element_type=F32)
    x1 = x_ref[...] + _rmsnorm(m, g1_ref[...])
    h = _rmsnorm(x1, g2_ref[...]).astype(BF16)
    y2 = None
    for c in range(d_ff // tf):
        gate = jnp.dot(h, wfi_ref[:, c * tf:(c + 1) * tf], preferred_element_type=F32)
        up = jnp.dot(h, wfi_ref[:, d_ff + c * tf:d_ff + (c + 1) * tf], preferred_element_type=F32)
        part = jnp.dot((_silu(gate) * up).astype(BF16), wfo_ref[c * tf:(c + 1) * tf, :], preferred_element_type=F32)
        y2 = part if y2 is None else y2 + part
    o_ref[...] = x1 + _rmsnorm(y2, g3_ref[...])


def _layer_spec(shape, layer):
    nd = len(shape) - 1
    return pl.BlockSpec((None,) + tuple(shape[1:]), lambda *_: (layer,) + (0,) * nd, pipeline_mode=pl.Buffered(1))


def _tail(x, odn, y, osw, wout, g1, g2, g3, wfi, wfo, layer, l_out, tm_target):
    nb, _, d = x.shape
    d_ff = wfo.shape[1]
    tm = _pick_tile(l_out, tm_target)
    tf = 2 * LANES if d_ff % (2 * LANES) == 0 else d_ff
    row = lambda w: pl.BlockSpec((None, tm, w), lambda b, i: (b, i, 0))
    return pl.pallas_call(
        functools.partial(_tail_kernel, d_ff=d_ff, tf=tf),
        out_shape=jax.ShapeDtypeStruct((nb, l_out, d), F32),
        grid=(nb, l_out // tm),
        in_specs=[row(d), row(DN_V), row(SSM_INNER), row(SWA_Q), _layer_spec(wout.shape, layer),
                  _layer_spec(g1.shape, layer), _layer_spec(g2.shape, layer), _layer_spec(g3.shape, layer),
                  _layer_spec(wfi.shape, layer), _layer_spec(wfo.shape, layer)],
        out_specs=row(d),
        compiler_params=pltpu.CompilerParams(dimension_semantics=("arbitrary", "arbitrary"),
                                             vmem_limit_bytes=VMEM_LIMIT),
        name="outproj_ffn",
    )(x, odn, y, osw, wout, g1, g2, g3, wfi, wfo)


def _reorder_w_in_kernel(w_ref, o_ref):
    w = w_ref[0]
    offs = [0]
    for wd in IN_WIDTHS:
        offs.append(offs[-1] + wd)
    seg = lambda i: w[:, offs[i]:offs[i + 1]]
    dn_qkv, dn_z, dn_b, dn_a, ssm_xbc, ssm_z, ssm_dt, sw_q, sw_k, sw_v = (seg(i) for i in range(len(IN_WIDTHS)))
    n_small = dn_b.shape[1] + dn_a.shape[1] + ssm_dt.shape[1]
    small = jnp.concatenate([dn_b, dn_a, ssm_dt, jnp.zeros((w.shape[0], LANES - n_small), w.dtype)], axis=1)
    o_ref[0] = jnp.concatenate([dn_qkv, dn_z, ssm_z, ssm_xbc, sw_q, sw_k, sw_v, small], axis=1)


def _reorder_w_in(w):
    depth, d, d_in = w.shape
    assert d_in == sum(IN_WIDTHS)
    tr = _pick_tile(d, 256)
    return pl.pallas_call(
        _reorder_w_in_kernel,
        out_shape=jax.ShapeDtypeStruct((depth, d, D_PROJ), BF16),
        grid=(depth, d // tr),
        in_specs=[pl.BlockSpec((1, tr, d_in), lambda l, i: (l, i, 0))],
        out_specs=pl.BlockSpec((1, tr, D_PROJ), lambda l, i: (l, i, 0)),
        compiler_params=pltpu.CompilerParams(dimension_semantics=("arbitrary", "arbitrary")),
        name="reorder_w_in",
    )(w)


def _scalar_param_tiles(rows):
    depth = rows[0][1].shape[0]
    padded = [jnp.pad(v.astype(F32), ((0, 0), (off, LANES - off - v.shape[1]))) for off, v in rows]
    padded += [jnp.zeros((depth, LANES), F32)] * (SUBLANES - len(rows))
    return jnp.stack(padded, axis=1)


def _rope_tables(pos):
    half = SWA_HEAD_DIM // 2
    inv = ROPE_THETA ** (-jnp.arange(half, dtype=F32) / half)
    ang = pos.astype(F32)[:, None] * inv[None, :]
    cos = jnp.cos(ang)
    sin = jnp.sin(ang)
    cos_t = jnp.concatenate([cos, cos] * SWA_KV_HEADS, axis=1)
    sin_t = jnp.concatenate([-sin, sin] * SWA_KV_HEADS, axis=1)
    return cos_t, sin_t


def kernel(x_prompt, x_sample, state_dn, state_dn_conv, state_ssm, state_ssm_conv, cache_swa_k, cache_swa_v,
           meta_tokens, w_in, dn_conv_w, dn_a_log, dn_dt_bias, dn_norm_w, ssm_conv_w, ssm_conv_b, ssm_a_log,
           ssm_dt_bias, ssm_d, ssm_norm_w, swa_sinks, w_out, g_pre_mix, g_post_mix, g_pre_ffn, g_post_ffn,
           w_ffn_in, w_ffn_out):
    bp, seq, d = x_prompt.shape
    bs, ts, _ = x_sample.shape
    depth = w_in.shape[0]
    lp = N_META + seq + FRONT_PAD
    assert lp % BLOCK == 0 and BLOCK % CHUNK == 0 and seq % BLOCK == 0
    nblk = lp // BLOCK
    blk0 = nblk - 1
    pad_range = (seq, seq + FRONT_PAD)

    zpad = jnp.zeros((bp, FRONT_PAD, d), x_prompt.dtype)
    meta = jnp.broadcast_to(meta_tokens.astype(x_prompt.dtype)[None], (bp, N_META, d))
    xp = jnp.concatenate([x_prompt, zpad, meta], axis=1)
    xs = x_sample.reshape(1, bs * ts, d)

    cos_p, sin_p = _rope_tables(jnp.arange(lp, dtype=jnp.int32) - FRONT_PAD)
    cos_s, sin_s = _rope_tables(PAST_LEN + jnp.arange(ts, dtype=jnp.int32))

    gw = (SSM_HEADS // SSM_GROUPS) * SSM_HEADDIM
    nseq_s = _pick_tile(bs, 8) if bs % SUBLANES == 0 else bs
    zero_dn = jnp.zeros((1, bp, DN_HEADS, DN_DK, DN_DV), F32)
    zero_dnc = jnp.zeros((bp, SUBLANES, DN_CONV), F32)
    zero_ssm = jnp.zeros((1, bp, SSM_GROUPS, gw, SSM_STATE), F32)
    zero_ssmc = jnp.zeros((bp, SUBLANES, SSM_CONV), F32)
    state_ssm_g = state_ssm.reshape(depth, bs, SSM_GROUPS, gw, SSM_STATE)
    cache_k = cache_swa_k.reshape(depth, bs, WINDOW, SWA_KV)
    cache_v = cache_swa_v.reshape(depth, bs, WINDOW, SWA_KV)

    w_in_r = _reorder_w_in(w_in.astype(BF16))
    w_out_b = w_out.astype(BF16)
    w_fi_b = w_ffn_in.astype(BF16)
    w_fo_b = w_ffn_out.astype(BF16)
    g1, g2, g3, g4 = (a[:, None, :] for a in (g_pre_mix, g_post_mix, g_pre_ffn, g_post_ffn))

    new_p, new_s = [], []
    dn_s = ssm_s = k_s = v_s = None
    prm_all = _scalar_param_tiles([(SM_A, dn_a_log), (SM_A, dn_dt_bias), (SM_DT, ssm_a_log), (SM_DT, ssm_dt_bias),
                                   (0, swa_sinks)])
    drow_all = jnp.repeat(ssm_d, SSM_HEADDIM, axis=1)
    dn_cbuf_s = jnp.pad(state_dn_conv, ((0, 0), (0, 0), (SUBLANES - (CONV_WIDTH - 1), 0), (0, 0)))
    ssm_cbuf_s = jnp.pad(state_ssm_conv, ((0, 0), (0, 0), (SUBLANES - (CONV_WIDTH - 1), 0), (0, 0)))
    for l in range(depth):
        prm = prm_all[l]
        dn_nw = dn_norm_w[l][None, :]
        ssm_nw = ssm_norm_w[l][None, :]
        drow = drow_all[l][None, :]
        cbias = ssm_conv_b[l][None, :]
        last = l == depth - 1

        proj = _inproj(xp, g1, w_in_r, l, pad_range, TM_DENSE)
        (odn, dn_p, dnc_p), = _run_parts([_dn_mixer(proj, nblk, blk0, BLOCK, CHUNK, FRONT_PAD, bp, zero_dn, 0,
                                                    zero_dnc, dn_conv_w[l], prm, dn_nw)], "dn_mixer")
        (ys, ssm_p, ssmc_p), = _run_parts([_ssd_mixer(proj, nblk, blk0, BLOCK, CHUNK, FRONT_PAD, bp, zero_ssm, 0,
                                                      zero_ssmc, ssm_conv_w[l], cbias, prm, drow, ssm_nw)],
                                          "ssd_mixer")
        (osw, k_p, v_p), = _run_parts([_swa_prompt(proj, nblk, blk0, FRONT_PAD, cos_p, sin_p, prm)], "swa_prompt")
        xp = _tail(xp, odn, ys, osw, w_out_b, g2, g3, g4, w_fi_b, w_fo_b, l, seq if last else lp, TM_DENSE)
        new_p.append((dn_p, dnc_p[:, -(CONV_WIDTH - 1):], ssm_p.reshape(bp, SSM_HEADS, SSM_HEADDIM, SSM_STATE),
                      ssmc_p[:, -(CONV_WIDTH - 1):], k_p.reshape(bp, WINDOW, SWA_KV_HEADS, SWA_HEAD_DIM),
                      v_p.reshape(bp, WINDOW, SWA_KV_HEADS, SWA_HEAD_DIM)))

        proj = _inproj(xs, g1, w_in_r, l, None, TM_DENSE).reshape(bs, ts, D_PROJ)
        (odn, dn_s, dnc_s), (ys, ssm_s, ssmc_s), (osw, k_s, v_s) = _run_parts([
            _dn_mixer(proj, 1, 0, ts, ts, 0, nseq_s, state_dn, l, dn_cbuf_s[l], dn_conv_w[l],
                      prm, dn_nw, stack=(depth, dn_s)),
            _ssd_mixer(proj, 1, 0, ts, ts, 0, nseq_s, state_ssm_g, l, ssm_cbuf_s[l],
                       ssm_conv_w[l], cbias, prm, drow, ssm_nw, stack=(depth, ssm_s)),
            _swa_sample(proj, nseq_s, cache_k, cache_v, l, cos_s, sin_s, prm,
                        stack_k=(depth, k_s), stack_v=(depth, v_s))], "mixers_sample")
        flat = lambda a: a.reshape(1, bs * ts, a.shape[-1])
        xs = _tail(xs, flat(odn), flat(ys), flat(osw), w_out_b, g2, g3, g4, w_fi_b, w_fo_b, l, bs * ts, TM_DENSE)
        new_s.append((dnc_s[:, -(CONV_WIDTH - 1):], ssmc_s[:, -(CONV_WIDTH - 1):]))

    outs_p = tuple(jnp.stack([st[i] for st in new_p]) for i in range(6))
    dnc_s, ssmc_s = (jnp.stack([st[i] for st in new_s]) for i in range(2))
    outs_s = (dn_s, dnc_s, ssm_s.reshape(depth, bs, SSM_HEADS, SSM_HEADDIM, SSM_STATE), ssmc_s,
              k_s.reshape(depth, bs, WINDOW, SWA_KV_HEADS, SWA_HEAD_DIM),
              v_s.reshape(depth, bs, WINDOW, SWA_KV_HEADS, SWA_HEAD_DIM))
    return (xp, xs.reshape(bs, ts, d)) + outs_p + outs_s
```

```python
import functools

import jax
import jax.numpy as jnp
from jax import lax
from jax.experimental import pallas as pl
from jax.experimental.pallas import tpu as pltpu

F32 = jnp.float32
BF16 = jnp.bfloat16
NT = (((1,), (1,)), ((), ()))
TN = (((0,), (0,)), ((), ()))

N_META = 16
CONV_WIDTH = 4
CHUNK = 64
BLOCK = 128
WINDOW = 128
FRONT_PAD = BLOCK - N_META
ROPE_THETA = 10000.0
PAST_LEN = 8192
EPS = 1e-6

DN_HEADS, DN_DK, DN_DV = 4, 128, 128
DN_QK = DN_HEADS * DN_DK
DN_V = DN_HEADS * DN_DV
DN_CONV = 2 * DN_QK + DN_V
SSM_HEADS, SSM_HEADDIM, SSM_GROUPS, SSM_STATE = 4, 64, 2, 128
SSM_INNER = SSM_HEADS * SSM_HEADDIM
SSM_BC = SSM_GROUPS * SSM_STATE
SSM_CONV = SSM_INNER + 2 * SSM_BC
SWA_Q_HEADS, SWA_KV_HEADS, SWA_HEAD_DIM = 4, 2, 64
SWA_Q = SWA_Q_HEADS * SWA_HEAD_DIM
SWA_KV = SWA_KV_HEADS * SWA_HEAD_DIM
IN_WIDTHS = (DN_CONV, DN_V, DN_HEADS, DN_HEADS, SSM_CONV, SSM_INNER, SSM_HEADS, SWA_Q, SWA_KV, SWA_KV)

LANES = 128
SUBLANES = 8
COL_QKV = (0, DN_CONV)
COL_DNZ = (1536, DN_V)
COL_SSZ = (2048, SSM_INNER)
COL_XBC = (2304, SSM_CONV)
COL_SWQ = (3072, SWA_Q)
COL_SWK = (3328, SWA_KV)
COL_SWV = (3456, SWA_KV)
COL_SM = (3584, LANES)
D_PROJ = 3712
SM_B, SM_A, SM_DT = 0, 4, 8
PRM_DN_ALOG, PRM_DN_DTB, PRM_SSM_ALOG, PRM_SSM_DTB, PRM_SINK = 0, 1, 2, 3, 4
NEG_BIG = -1e30
VMEM_LIMIT = 56 * 1024 * 1024
TM_DENSE = 640


def _bdot(a, b, dims=None):
    a = a.astype(BF16)
    b = b.astype(BF16)
    if dims is None:
        return jnp.dot(a, b, preferred_element_type=F32)
    return lax.dot_general(a, b, dims, preferred_element_type=F32)


def _cumsum_rows(lmat, g):
    hi = g.astype(BF16)
    r1 = g - hi.astype(F32)
    mid = r1.astype(BF16)
    lo = (r1 - mid.astype(F32)).astype(BF16)
    dot = lambda part: jnp.dot(lmat, part, preferred_element_type=F32)
    return dot(hi) + dot(mid) + dot(lo)


def _rmsnorm(x, g):
    return x * lax.rsqrt(jnp.mean(x * x, axis=-1, keepdims=True) + EPS) * g


def _l2norm(x):
    return x * lax.rsqrt(jnp.sum(x * x, axis=-1, keepdims=True) + EPS)


def _sigmoid(x):
    return 1.0 / (1.0 + jnp.exp(-x))


def _silu(x):
    return x * _sigmoid(x)


def _softplus(x):
    return jnp.maximum(x, 0.0) + jnp.log1p(jnp.exp(-jnp.abs(x)))


def _pick_tile(n, target):
    best = None
    for t in range(SUBLANES, min(n, target) + 1, SUBLANES):
        if n % t == 0:
            best = t
    assert best is not None, n
    return best


def _inproj_kernel(x_ref, g_ref, w_ref, o_ref, *, tm, pad_range):
    h = _rmsnorm(x_ref[...], g_ref[...])
    if pad_range is not None:
        r = pl.program_id(1) * tm + lax.broadcasted_iota(jnp.int32, (tm, 1), 0)
        is_pad = (r >= pad_range[0]) & (r < pad_range[1])
        h = jnp.where(is_pad, 0.0, h)
    o_ref[...] = jnp.dot(h.astype(BF16), w_ref[...], preferred_element_type=F32)


def _inproj(x, g, w, layer, pad_range, tm_target):
    nb, rows, d = x.shape
    tm = _pick_tile(rows, tm_target)
    return pl.pallas_call(
        functools.partial(_inproj_kernel, tm=tm, pad_range=pad_range),
        out_shape=jax.ShapeDtypeStruct((nb, rows, D_PROJ), F32),
        grid=(nb, rows // tm),
        in_specs=[pl.BlockSpec((None, tm, d), lambda b, i: (b, i, 0)), _layer_spec(g.shape, layer),
                  _layer_spec(w.shape, layer)],
        out_specs=pl.BlockSpec((None, tm, D_PROJ), lambda b, i: (b, i, 0)),
        compiler_params=pltpu.CompilerParams(dimension_semantics=("arbitrary", "arbitrary"),
                                             vmem_limit_bytes=VMEM_LIMIT),
        name="inproj",
    )(x, g, w)


def _causal_conv(xbuf, raw_ref, cw_ref, cbo_ref, rows):
    x = raw_ref[...]
    prev = xbuf[0:SUBLANES, :]
    cw = cw_ref[...]
    row = lax.broadcasted_iota(jnp.int32, (SUBLANES, 1), 0)
    acc = None
    for i in range(CONV_WIDTH):
        s = CONV_WIDTH - 1 - i
        if s == 0:
            xs = x
        else:
            r = pltpu.roll(x, s, axis=0)
            head = jnp.where(row < s, pltpu.roll(prev, s, axis=0), r[0:SUBLANES])
            xs = head if rows == SUBLANES else jnp.concatenate([head, r[SUBLANES:]], axis=0)
        term = xs * cw[i:i + 1, :]
        acc = term if acc is None else acc + term
    tail = x[rows - SUBLANES:rows]
    cbo_ref[...] = tail
    xbuf[0:SUBLANES, :] = tail
    return acc


def _chunk_masks(c):
    ii = lax.broadcasted_iota(jnp.int32, (c, c), 0)
    jj = lax.broadcasted_iota(jnp.int32, (c, c), 1)
    return ii >= jj, ii > jj, (ii == jj).astype(F32)


def _segment_decay(gc, gct, lane, ge):
    col = gc[:, lane:lane + 1]
    row = gct[lane:lane + 1, :]
    return jnp.where(ge, jnp.exp(jnp.where(ge, col - row, 0.0)), 0.0)


def _inv_unit_lower_minus_eye(a_list, c, nh):
    w = nh * c
    blk_r = lax.broadcasted_iota(jnp.int32, (w, w), 0) // c
    blk_c = lax.broadcasted_iota(jnp.int32, (w, w), 1) // c
    same = blk_r == blk_c

    def block_diag(p):
        return jnp.where(same, jnp.concatenate([p] * nh, axis=0), 0.0).astype(BF16)

    ys = [-a for a in a_list]
    ps = [_bdot(a, block_diag(a)) for a in a_list]
    yield
    n = 2
    while n < c:
        pbds = [block_diag(p) for p in ps]
        n *= 2
        if n < c:
            sts = [_bdot(jnp.concatenate([y, p], axis=0), pbd) for y, p, pbd in zip(ys, ps, pbds)]
            ys = [y + p + st[:c] for y, p, st in zip(ys, ps, sts)]
            ps = [st[c:] for st in sts]
        else:
            ys = [y + p + _bdot(y, pbd) for y, p, pbd in zip(ys, ps, pbds)]
        yield
    return ys


def _run_parts(parts, name):
    grid = parts[0]["grid"]
    assert all(p["grid"] == grid for p in parts)
    n_in = [len(p["inputs"]) for p in parts]
    n_out = [len(p["out_shape"]) for p in parts]
    n_scr = [len(p["scratch"]) for p in parts]
    aliases = {}
    for k, p in enumerate(parts):
        for i, o in p["aliases"].items():
            aliases[sum(n_in[:k]) + i] = sum(n_out[:k]) + o

    def kernel(*refs):
        ins = refs[:sum(n_in)]
        outs = refs[sum(n_in):sum(n_in) + sum(n_out)]
        scr = refs[sum(n_in) + sum(n_out):]
        gens = [p["body"](ins[sum(n_in[:k]):sum(n_in[:k + 1])], outs[sum(n_out[:k]):sum(n_out[:k + 1])],
                          scr[sum(n_scr[:k]):sum(n_scr[:k + 1])]) for k, p in enumerate(parts)]
        for tag in gens[0]:
            if tag == "chain":
                break
        live = list(gens)
        while live:
            for g in list(live):
                if next(g, StopIteration) is StopIteration:
                    live.remove(g)

    results = pl.pallas_call(
        kernel,
        out_shape=tuple(s for p in parts for s in p["out_shape"]),
        grid=grid,
        in_specs=[s for p in parts for s in p["in_specs"]],
        out_specs=tuple(s for p in parts for s in p["out_specs"]),
        scratch_shapes=[s for p in parts for s in p["scratch"]],
        input_output_aliases=aliases,
        compiler_params=pltpu.CompilerParams(dimension_semantics=("arbitrary",) * len(grid),
                                             vmem_limit_bytes=VMEM_LIMIT),
        name=name,
    )(*[a for p in parts for a in p["inputs"]])
    return [list(results[sum(n_out[:k]):sum(n_out[:k + 1])]) for k in range(len(parts))]


def _state_out(stack, layer, nb, nseq, tail, out_index, n_inputs):
    zeros = (0,) * len(tail)
    if stack is None:
        return ((nb,) + tail, pl.BlockSpec((nseq,) + tail, lambda b, *_: (b,) + zeros), [], [], {}, None, 0)
    depth, prev = stack
    shape = (depth, nb) + tail
    if prev is None:
        spec = pl.BlockSpec((depth, nseq) + tail, lambda b, *_: (0, b) + zeros)
        return (shape, spec, [], [], {}, layer, depth)
    spec = pl.BlockSpec((None, nseq) + tail, lambda b, *_: (layer, b) + zeros)
    return (shape, spec, [prev], [pl.BlockSpec(memory_space=pl.ANY)], {n_inputs: out_index}, None, 0)


def _state_view(ref, slot, n_slots):
    if slot is None:
        return ref
    for other in range(n_slots):
        if other != slot:
            ref[other] = jnp.zeros(ref.shape[1:], ref.dtype)
    return ref.at[slot]


def _dn_prep(items, lmat, ge, gt_all, chunk):
    heads = range(DN_HEADS)
    gcs = [_cumsum_rows(lmat, g_all) for _, _, g_all in items]
    gcts = [gc.T for gc in gcs]
    qs = [[_l2norm(qkv[:, h * DN_DK:(h + 1) * DN_DK]) * DN_DK ** -0.5 for h in heads] for qkv, _, _ in items]
    ks = [[_l2norm(qkv[:, DN_QK + h * DN_DK:DN_QK + (h + 1) * DN_DK]) for h in heads] for qkv, _, _ in items]
    vs = [[qkv[:, 2 * DN_QK + h * DN_DV:2 * DN_QK + (h + 1) * DN_DV] for h in heads] for qkv, _, _ in items]
    betas = [[beta_all[:, SM_B + h:SM_B + h + 1] for h in heads] for _, beta_all, _ in items]
    yield
    kbs = [[k.astype(BF16) for k in kk] for kk in ks]
    kks = [jnp.concatenate([_bdot(kb, kb, NT) for kb in kb4], axis=1) for kb4 in kbs]
    qks = [jnp.concatenate([_bdot(q, kb, NT) for q, kb in zip(q4, kb4)], axis=1) for q4, kb4 in zip(qs, kbs)]
    decs = [jnp.concatenate([_segment_decay(gc, gct, SM_A + h, ge) for h in heads], axis=1)
            for gc, gct in zip(gcs, gcts)]
    beta_ws = [jnp.concatenate([jnp.broadcast_to(b, (chunk, chunk)) for b in b4], axis=1) for b4 in betas]
    a_list = [jnp.where(gt_all, bw * kk * dec, 0.0) for bw, kk, dec in zip(beta_ws, kks, decs)]
    yield "chain"
    n = len(items)
    egs, ekds, egls, attns, rhss, qds, kds = ([None] * n for _ in range(7))

    def side_work(i):
        gc = gcs[i]
        glast = gc[chunk - 1:chunk, :]
        egs[i] = jnp.exp(gc)
        ekds[i] = jnp.exp(glast - gc)
        egls[i] = jnp.exp(glast)
        attns[i] = (qks[i] * decs[i]).astype(BF16)
        lane = lambda a, h: a[:, SM_A + h:SM_A + h + 1]
        rhss[i] = [jnp.concatenate([vs[i][h] * betas[i][h], ks[i][h] * (betas[i][h] * lane(egs[i], h))], axis=1)
                   for h in heads]
        qds[i] = [(qs[i][h] * lane(egs[i], h)).astype(BF16) for h in heads]
        kds[i] = [(ks[i][h] * lane(ekds[i], h)).astype(BF16) for h in heads]

    inverse = _inv_unit_lower_minus_eye(a_list, chunk, DN_HEADS)
    todo = list(range(n))
    while True:
        try:
            next(inverse)
        except StopIteration as done:
            tms = done.value
            break
        if todo:
            side_work(todo.pop(0))
        yield
    for i in todo:
        side_work(i)
    uws = [[rhss[i][h] + _bdot(tms[i][:, h * chunk:(h + 1) * chunk], rhss[i][h]) for h in heads] for i in range(n)]
    yield
    out = []
    for i in range(n):
        per_head = []
        for h in heads:
            uw = uws[i][h]
            wq = jnp.concatenate([uw[:, DN_DV:].astype(BF16), qds[i][h]], axis=0)
            per_head.append((uw[:, :DN_DV], wq, attns[i][:, h * chunk:(h + 1) * chunk], kds[i][h],
                             egls[i][:, SM_A + h:SM_A + h + 1]))
        out.append(per_head)
    return out


def _dn_body(ins, outs, scratch, *, nseq, rows, chunk, front_pad, slot, n_slots):
    qkv_ref, z_ref, sm_ref, s0_ref, cb_ref, cw_ref, prm_ref, nw_ref = ins[:8]
    o_ref, so_full, cbo_ref = outs
    xbuf, = scratch
    so_ref = _state_view(so_full, slot, n_slots)
    blk = pl.program_id(1)
    first = blk == 0

    @pl.when(first)
    def _():
        so_ref[...] = s0_ref[...]
        xbuf[:, 0:SUBLANES, :] = cb_ref[...]

    prm = prm_ref[...]
    nw = nw_ref[...]
    ge, _, _ = _chunk_masks(chunk)
    lmat = jnp.where(ge, 1.0, 0.0).astype(BF16)
    wide = (chunk, DN_HEADS * chunk)
    gt_all = lax.broadcasted_iota(jnp.int32, wide, 0) > lax.broadcasted_iota(jnp.int32, wide, 1) % chunk
    nchunk = rows // chunk

    items = []
    for s in range(nseq):
        qkv = _silu(_causal_conv(xbuf.at[s], qkv_ref.at[s], cw_ref, cbo_ref.at[s], rows))
        sm = sm_ref[s]
        beta_all = _sigmoid(sm)
        g_all = -jnp.exp(prm[PRM_DN_ALOG:PRM_DN_ALOG + 1, :]) * _softplus(sm + prm[PRM_DN_DTB:PRM_DN_DTB + 1, :])
        if front_pad:
            pos = blk * rows + lax.broadcasted_iota(jnp.int32, (rows, 1), 0)
            g_all = jnp.where(pos < front_pad, 0.0, g_all)
        for c in range(nchunk):
            cs = slice(c * chunk, (c + 1) * chunk)
            items.append((qkv[cs], beta_all[cs], g_all[cs]))
        yield
    prep = yield from _dn_prep(items, lmat, ge, gt_all, chunk)

    chains = [(s, h) for s in range(nseq) for h in range(DN_HEADS)]
    states = [so_ref[s, h] for s, h in chains]
    for c in range(nchunk):
        r0 = c * chunk
        fac = [prep[s * nchunk + c][h] for s, h in chains]
        m1s = [_bdot(f[1], st) for f, st in zip(fac, states)]
        yield
        v_news = [(f[0] - m1[:chunk]).astype(BF16) for f, m1 in zip(fac, m1s)]
        ups = [_bdot(f[3], v, TN) for f, v in zip(fac, v_news)]
        os_ = [m1[chunk:] + _bdot(f[2], v) for f, m1, v in zip(fac, m1s, v_news)]
        yield
        states = [st * f[4] + up for f, st, up in zip(fac, states, ups)]
        for (s, h), o in zip(chains, os_):
            lo = h * DN_DV
            zh = z_ref[s, r0:r0 + chunk, lo:lo + DN_DV]
            o_ref[s, r0:r0 + chunk, lo:lo + DN_DV] = (_rmsnorm(o, nw) * _silu(zh)).astype(o_ref.dtype)
        yield
    for (s, h), st in zip(chains, states):
        so_ref[s, h] = st


def _dn_mixer(proj, nblk, blk0, rows, chunk, front_pad, nseq, s0, layer, cbuf, cw, prm, nw, stack=None,
              out_dtype=F32):
    nb = proj.shape[0]
    col = lambda c: c[0] // c[1]
    phys = lambda i: (i + blk0) % nblk
    so_shape, so_spec, extra, extra_specs, aliases, slot, n_slots = _state_out(
        stack, layer, nb, nseq, (DN_HEADS, DN_DK, DN_DV), 1, 8)
    return dict(
        body=functools.partial(_dn_body, nseq=nseq, rows=rows, chunk=chunk, front_pad=front_pad,
                               slot=slot, n_slots=n_slots),
        grid=(nb // nseq, nblk),
        inputs=[proj, proj, proj, s0, cbuf, cw, prm, nw] + extra,
        out_shape=[jax.ShapeDtypeStruct((nb, nblk * rows, DN_V), out_dtype),
                   jax.ShapeDtypeStruct(so_shape, F32),
                   jax.ShapeDtypeStruct((nb, SUBLANES, DN_CONV), F32)],
        in_specs=[
            pl.BlockSpec((nseq, rows, DN_CONV), lambda b, i: (b, phys(i), col(COL_QKV))),
            pl.BlockSpec((nseq, rows, DN_V), lambda b, i: (b, phys(i), col(COL_DNZ))),
            pl.BlockSpec((nseq, rows, LANES), lambda b, i: (b, phys(i), col(COL_SM))),
            pl.BlockSpec((None, nseq, DN_HEADS, DN_DK, DN_DV), lambda b, i: (layer, b, 0, 0, 0)),
            pl.BlockSpec((nseq, SUBLANES, DN_CONV), lambda b, i: (b, 0, 0)),
            pl.BlockSpec((CONV_WIDTH, DN_CONV), lambda b, i: (0, 0)),
            pl.BlockSpec((SUBLANES, LANES), lambda b, i: (0, 0)),
            pl.BlockSpec((1, DN_DV), lambda b, i: (0, 0)),
        ] + extra_specs,
        out_specs=[
            pl.BlockSpec((nseq, rows, DN_V), lambda b, i: (b, phys(i), 0)),
            so_spec,
            pl.BlockSpec((nseq, SUBLANES, DN_CONV), lambda b, i: (b, 0, 0)),
        ],
        scratch=[pltpu.VMEM((nseq, SUBLANES, DN_CONV), F32)],
        aliases=aliases,
    )


def _ssd_body(ins, outs, scratch, *, nseq, rows, chunk, front_pad, slot, n_slots):
    xbc_ref, z_ref, sm_ref, h0_ref, cb_ref, cw_ref, cbias_ref, prm_ref, drow_ref, nw_ref = ins[:10]
    y_ref, ho_full, cbo_ref = outs
    xbuf, = scratch
    ho_ref = _state_view(ho_full, slot, n_slots)
    blk = pl.program_id(1)

    @pl.when(blk == 0)
    def _():
        ho_ref[...] = h0_ref[...]
        xbuf[:, 0:SUBLANES, :] = cb_ref[...]

    prm = prm_ref[...]
    nw = nw_ref[...]
    drow = drow_ref[...]
    cbias = cbias_ref[...]
    ge, _, _ = _chunk_masks(chunk)
    lmat = jnp.where(ge, 1.0, 0.0).astype(BF16)
    hpg = SSM_HEADS // SSM_GROUPS
    gw = hpg * SSM_HEADDIM
    lane = lax.broadcasted_iota(jnp.int32, (1, gw), 1)
    srow = lax.broadcasted_iota(jnp.int32, (gw, 1), 0)
    in_head = [(lane >= j * SSM_HEADDIM) & (lane < (j + 1) * SSM_HEADDIM) for j in range(hpg)]
    nchunk = rows // chunk
    groups = range(SSM_GROUPS)

    items = []
    for s in range(nseq):
        act = _silu(_causal_conv(xbuf.at[s], xbc_ref.at[s], cw_ref, cbo_ref.at[s], rows) + cbias)
        dt_all = _softplus(sm_ref[s] + prm[PRM_SSM_DTB:PRM_SSM_DTB + 1, :])
        if front_pad:
            pos = blk * rows + lax.broadcasted_iota(jnp.int32, (rows, 1), 0)
            dt_all = jnp.where(pos < front_pad, 0.0, dt_all)
        g_all = dt_all * (-jnp.exp(prm[PRM_SSM_ALOG:PRM_SSM_ALOG + 1, :]))
        for c in range(nchunk):
            cs = slice(c * chunk, (c + 1) * chunk)
            items.append((act[cs], dt_all[cs], g_all[cs]))
        yield
    n_items = len(items)
    gcs = [_cumsum_rows(lmat, g) for _, _, g in items]
    gcts = [gc.T for gc in gcs]
    xgs = [[a[:, g * gw:(g + 1) * gw] for g in groups] for a, _, _ in items]
    bgs = [[a[:, SSM_INNER + g * SSM_STATE:SSM_INNER + (g + 1) * SSM_STATE] for g in groups] for a, _, _ in items]
    cgs = [[a[:, SSM_INNER + SSM_BC + g * SSM_STATE:SSM_INNER + SSM_BC + (g + 1) * SSM_STATE] for g in groups]
           for a, _, _ in items]
    cbs = [[_bdot(cgs[i][g], bgs[i][g], NT) for g in groups] for i in range(n_items)]
    yield
    egs = [jnp.exp(gc) for gc in gcs]
    ekds = [jnp.exp(gc[chunk - 1:chunk, :] - gc) for gc in gcs]
    egls = [jnp.exp(gc[chunk - 1:chunk, :]) for gc in gcs]
    heads = [(g, j) for g in groups for j in range(hpg)]
    ln = lambda g, j: SM_DT + g * hpg + j
    xdts = [[jnp.where(in_head[j], xgs[i][g] * items[i][1][:, ln(g, j):ln(g, j) + 1], 0.0).astype(BF16)
             for g, j in heads] for i in range(n_items)]
    attns = [[cbs[i][g] * _segment_decay(gcs[i], gcts[i], ln(g, j), ge) for g, j in heads] for i in range(n_items)]
    y_intras = [[_bdot(attns[i][k], xdts[i][k]) for k in range(len(heads))] for i in range(n_items)]
    yield
    upds = [[_bdot(xdts[i][k], bgs[i][g] * ekds[i][:, ln(g, j):ln(g, j) + 1], TN) for k, (g, j) in enumerate(heads)]
            for i in range(n_items)]
    yield
    cds = [[jnp.concatenate([cgs[i][g] * egs[i][:, ln(g, j):ln(g, j) + 1] for j in range(hpg)], axis=0).astype(BF16)
            for g in groups] for i in range(n_items)]
    y_loc = [[sum(y_intras[i][g * hpg + j] for j in range(hpg)) for g in groups] for i in range(n_items)]
    h_inc = [[sum(upds[i][g * hpg + j] for j in range(hpg)) for g in groups] for i in range(n_items)]
    gl_cols = []
    for i in range(n_items):
        per_group = []
        for g in groups:
            gl = egls[i][:, ln(g, 0):ln(g, 0) + 1]
            for j in range(1, hpg):
                gl = jnp.where(srow < j * SSM_HEADDIM, gl, egls[i][:, ln(g, j):ln(g, j) + 1])
            per_group.append(gl)
        gl_cols.append(per_group)

    chains = [(s, g) for s in range(nseq) for g in groups]
    states = [ho_ref[s, g] for s, g in chains]
    for c in range(nchunk):
        r0 = c * chunk
        idx = [s * nchunk + c for s, _ in chains]
        yis = [_bdot(cds[i][g], st, NT) for i, (_, g), st in zip(idx, chains, states)]
        yield
        states = [st * gl_cols[i][g] + h_inc[i][g] for i, (_, g), st in zip(idx, chains, states)]
        for i, (s, g), yi in zip(idx, chains, yis):
            y_inter = yi[0:chunk]
            for j in range(1, hpg):
                y_inter = jnp.where(in_head[j], yi[j * chunk:(j + 1) * chunk], y_inter)
            yg = y_loc[i][g] + y_inter + xgs[i][g] * drow[:, g * gw:(g + 1) * gw]
            yg = yg * _silu(z_ref[s, r0:r0 + chunk, g * gw:(g + 1) * gw])
            y_ref[s, r0:r0 + chunk, g * gw:(g + 1) * gw] = _rmsnorm(yg, nw[:, g * gw:(g + 1) * gw]).astype(y_ref.dtype)
        yield
    for (s, g), st in zip(chains, states):
        ho_ref[s, g] = st


def _ssd_mixer(proj, nblk, blk0, rows, chunk, front_pad, nseq, h0, layer, cbuf, cw, cbias, prm, drow, nw,
               stack=None, out_dtype=F32):
    nb = proj.shape[0]
    col = lambda c: c[0] // c[1]
    phys = lambda i: (i + blk0) % nblk
    gw = (SSM_HEADS // SSM_GROUPS) * SSM_HEADDIM
    ho_shape, ho_spec, extra, extra_specs, aliases, slot, n_slots = _state_out(
        stack, layer, nb, nseq, (SSM_GROUPS, gw, SSM_STATE), 1, 10)
    return dict(
        body=functools.partial(_ssd_body, nseq=nseq, rows=rows, chunk=chunk, front_pad=front_pad,
                               slot=slot, n_slots=n_slots),
        grid=(nb // nseq, nblk),
        inputs=[proj, proj, proj, h0, cbuf, cw, cbias, prm, drow, nw] + extra,
        out_shape=[jax.ShapeDtypeStruct((nb, nblk * rows, SSM_INNER), out_dtype),
                   jax.ShapeDtypeStruct(ho_shape, F32),
                   jax.ShapeDtypeStruct((nb, SUBLANES, SSM_CONV), F32)],
        in_specs=[
            pl.BlockSpec((nseq, rows, SSM_CONV), lambda b, i: (b, phys(i), col(COL_XBC))),
            pl.BlockSpec((nseq, rows, SSM_INNER), lambda b, i: (b, phys(i), col(COL_SSZ))),
            pl.BlockSpec((nseq, rows, LANES), lambda b, i: (b, phys(i), col(COL_SM))),
            pl.BlockSpec((None, nseq, SSM_GROUPS, gw, SSM_STATE), lambda b, i: (layer, b, 0, 0, 0)),
            pl.BlockSpec((nseq, SUBLANES, SSM_CONV), lambda b, i: (b, 0, 0)),
            pl.BlockSpec((CONV_WIDTH, SSM_CONV), lambda b, i: (0, 0)),
            pl.BlockSpec((1, SSM_CONV), lambda b, i: (0, 0)),
            pl.BlockSpec((SUBLANES, LANES), lambda b, i: (0, 0)),
            pl.BlockSpec((1, SSM_INNER), lambda b, i: (0, 0)),
            pl.BlockSpec((1, SSM_INNER), lambda b, i: (0, 0)),
        ] + extra_specs,
        out_specs=[
            pl.BlockSpec((nseq, rows, SSM_INNER), lambda b, i: (b, phys(i), 0)),
            ho_spec,
            pl.BlockSpec((nseq, SUBLANES, SSM_CONV), lambda b, i: (b, 0, 0)),
        ],
        scratch=[pltpu.VMEM((nseq, SUBLANES, SSM_CONV), F32)],
        aliases=aliases,
    )


def _rope(x, cos, sin_signed):
    w = x.shape[-1]
    half = SWA_HEAD_DIM // 2
    lane = lax.broadcasted_iota(jnp.int32, (1, w), 1)
    first_half = (lane % SWA_HEAD_DIM) < half
    swapped = jnp.where(first_half, pltpu.roll(x, w - half, axis=1), pltpu.roll(x, half, axis=1))
    return x * cos + swapped * sin_signed


def _sink_attend(problems):
    scale = SWA_HEAD_DIM ** -0.5
    scores = [[jnp.where(m, _bdot(q, k, NT) * scale, NEG_BIG) for k, m in zip(keys, masks)]
              for q, keys, _, masks, _ in problems]
    yield
    outs = []
    probs, dens = [], []
    def lane_reduce(tiles, combine, reduce):
        merged = {}
        for t in tiles:
            merged[t.shape[-1]] = t if t.shape[-1] not in merged else combine(merged[t.shape[-1]], t)
        return [reduce(t, axis=-1, keepdims=True) for t in merged.values()]

    for (_, _, _, _, sink), ss in zip(problems, scores):
        mx = sink
        for m in lane_reduce(ss, jnp.maximum, jnp.max):
            mx = jnp.maximum(mx, m)
        probs.append([jnp.exp(s - mx) for s in ss])
        dens.append(jnp.exp(sink - mx))
    yield
    ones = lambda v: jnp.ones((v.shape[0], SWA_HEAD_DIM), BF16)
    pvs = [[_bdot(p, jnp.concatenate([v.astype(BF16), ones(v)], axis=1)) for p, v in zip(ps, vals)]
           for (_, _, vals, _, _), ps in zip(problems, probs)]
    yield
    for pv, sink_term in zip(pvs, dens):
        acc = pv[0]
        for extra in pv[1:]:
            acc = acc + extra
        den = acc[:, SWA_HEAD_DIM:SWA_HEAD_DIM + 1] + sink_term
        outs.append(acc[:, :SWA_HEAD_DIM] / den)
    return outs


def _swa_problems(q, key_sets, val_sets, masks, prm, tq):
    grp = SWA_Q_HEADS // SWA_KV_HEADS
    row = lax.broadcasted_iota(jnp.int32, (grp * tq, 1), 0)
    problems = []
    for j in range(SWA_KV_HEADS):
        ks = slice(j * SWA_HEAD_DIM, (j + 1) * SWA_HEAD_DIM)
        heads = [j * grp + g for g in range(grp)]
        qst = jnp.concatenate([q[:, h * SWA_HEAD_DIM:(h + 1) * SWA_HEAD_DIM] for h in heads], axis=0)
        sink = prm[PRM_SINK:PRM_SINK + 1, heads[0]:heads[0] + 1]
        for g in range(1, grp):
            sink = jnp.where(row < g * tq, sink, prm[PRM_SINK:PRM_SINK + 1, heads[g]:heads[g] + 1])
        problems.append((qst, [k[:, ks] for k in key_sets], [v[:, ks] for v in val_sets], masks, sink))
    return problems


def _swa_store(o_ref_at, outs, tq):
    grp = SWA_Q_HEADS // SWA_KV_HEADS
    for j, o in enumerate(outs):
        for g in range(grp):
            h = j * grp + g
            o_ref_at(slice(h * SWA_HEAD_DIM, (h + 1) * SWA_HEAD_DIM), o[g * tq:(g + 1) * tq])


def _swa_prompt_body(ins, outs, scratch, *, nseq, front_pad):
    q_ref, k_ref, v_ref, cos_ref, sin_ref, prm_ref = ins
    o_ref, ko_ref, vo_ref = outs
    kprev, vprev = scratch
    blk = pl.program_id(1)

    @pl.when(blk == 0)
    def _():
        kprev[...] = jnp.zeros_like(kprev)
        vprev[...] = jnp.zeros_like(vprev)

    cos = cos_ref[...]
    sin = sin_ref[...]
    cos_q = jnp.concatenate([cos, cos], axis=1)
    sin_q = jnp.concatenate([sin, sin], axis=1)
    grp = SWA_Q_HEADS // SWA_KV_HEADS
    qi = lax.broadcasted_iota(jnp.int32, (grp * BLOCK, BLOCK), 0) % BLOCK
    kj = lax.broadcasted_iota(jnp.int32, (grp * BLOCK, BLOCK), 1)
    mask_cur = (kj <= qi) & (blk * BLOCK + kj >= front_pad)
    mask_prev = (kj > qi) & ((blk - 1) * BLOCK + kj >= front_pad)
    prm = prm_ref[...]
    problems = []
    for s in range(nseq):
        q = _rope(q_ref[s], cos_q, sin_q)
        k = _rope(k_ref[s], cos, sin)
        v = v_ref[s]
        problems += _swa_problems(q, (kprev[s], k), (vprev[s], v), (mask_prev, mask_cur), prm, BLOCK)
        kprev[s] = k
        vprev[s] = v
        ko_ref[s] = k
        vo_ref[s] = v
        yield
    res = yield from _sink_attend(problems)
    for s in range(nseq):
        def put(cols, val, s=s):
            o_ref[s, :, cols] = val.astype(o_ref.dtype)
        _swa_store(put, res[s * SWA_KV_HEADS:(s + 1) * SWA_KV_HEADS], BLOCK)


def _swa_prompt(proj, nblk, blk0, front_pad, cos, sin, prm, out_dtype=F32):
    nb = proj.shape[0]
    col = lambda c: c[0] // c[1]
    phys = lambda i: (i + blk0) % nblk
    return dict(
        body=functools.partial(_swa_prompt_body, nseq=nb, front_pad=front_pad),
        grid=(1, nblk),
        inputs=[proj, proj, proj, cos, sin, prm],
        out_shape=[jax.ShapeDtypeStruct((nb, nblk * BLOCK, SWA_Q), out_dtype),
                   jax.ShapeDtypeStruct((nb, WINDOW, SWA_KV), F32),
                   jax.ShapeDtypeStruct((nb, WINDOW, SWA_KV), F32)],
        in_specs=[
            pl.BlockSpec((nb, BLOCK, SWA_Q), lambda b, i: (0, phys(i), col(COL_SWQ))),
            pl.BlockSpec((nb, BLOCK, SWA_KV), lambda b, i: (0, phys(i), col(COL_SWK))),
            pl.BlockSpec((nb, BLOCK, SWA_KV), lambda b, i: (0, phys(i), col(COL_SWV))),
            pl.BlockSpec((BLOCK, SWA_KV), lambda b, i: (i, 0)),
            pl.BlockSpec((BLOCK, SWA_KV), lambda b, i: (i, 0)),
            pl.BlockSpec((SUBLANES, LANES), lambda b, i: (0, 0)),
        ],
        out_specs=[
            pl.BlockSpec((nb, BLOCK, SWA_Q), lambda b, i: (0, phys(i), 0)),
            pl.BlockSpec((nb, WINDOW, SWA_KV), lambda b, i: (0, 0, 0)),
            pl.BlockSpec((nb, WINDOW, SWA_KV), lambda b, i: (0, 0, 0)),
        ],
        scratch=[pltpu.VMEM((nb, BLOCK, SWA_KV), F32), pltpu.VMEM((nb, BLOCK, SWA_KV), F32)],
        aliases={},
    )


def _swa_sample_body(ins, outs, scratch, *, nseq, steps, slot, n_slots):
    q_ref, k_ref, v_ref, kc_ref, vc_ref, cos_ref, sin_ref, prm_ref = ins[:8]
    o_ref, ko_full, vo_full = outs
    ko_ref = _state_view(ko_full, slot, n_slots)
    vo_ref = _state_view(vo_full, slot, n_slots)
    cos = cos_ref[...]
    sin = sin_ref[...]
    cos_q = jnp.concatenate([cos, cos], axis=1)
    sin_q = jnp.concatenate([sin, sin], axis=1)
    prm = prm_ref[...]
    grp = SWA_Q_HEADS // SWA_KV_HEADS
    ti = lax.broadcasted_iota(jnp.int32, (grp * steps, WINDOW), 0) % steps
    sj = lax.broadcasted_iota(jnp.int32, (grp * steps, WINDOW), 1)
    mask_cache = sj > ti
    tn = lax.broadcasted_iota(jnp.int32, (grp * steps, steps), 0) % steps
    sn = lax.broadcasted_iota(jnp.int32, (grp * steps, steps), 1)
    mask_new = sn <= tn
    problems = []
    for b in range(nseq):
        q = _rope(q_ref[b], cos_q, sin_q)
        k = _rope(k_ref[b], cos, sin)
        v = v_ref[b]
        kc = kc_ref[b]
        vc = vc_ref[b]
        ko_ref[b, 0:WINDOW - steps, :] = kc[steps:WINDOW, :]
        ko_ref[b, WINDOW - steps:WINDOW, :] = k
        vo_ref[b, 0:WINDOW - steps, :] = vc[steps:WINDOW, :]
        vo_ref[b, WINDOW - steps:WINDOW, :] = v
        problems += _swa_problems(q, (kc, k), (vc, v), (mask_cache, mask_new), prm, steps)
        yield
    res = yield from _sink_attend(problems)
    for b in range(nseq):
        def put(cols, val, b=b):
            o_ref[b, :, cols] = val
        _swa_store(put, res[b * SWA_KV_HEADS:(b + 1) * SWA_KV_HEADS], steps)


def _swa_sample(proj, nseq, kc, vc, layer, cos, sin, prm, stack_k=None, stack_v=None):
    nb, steps, _ = proj.shape
    assert WINDOW > steps
    col = lambda c: c[0] // c[1]
    ko_shape, ko_spec, extra_k, specs_k, alias_k, slot, n_slots = _state_out(
        stack_k, layer, nb, nseq, (WINDOW, SWA_KV), 1, 8)
    vo_shape, vo_spec, extra_v, specs_v, alias_v, _, _ = _state_out(
        stack_v, layer, nb, nseq, (WINDOW, SWA_KV), 2, 8 + len(extra_k))
    return dict(
        body=functools.partial(_swa_sample_body, nseq=nseq, steps=steps, slot=slot, n_slots=n_slots),
        grid=(nb // nseq, 1),
        inputs=[proj, proj, proj, kc, vc, cos, sin, prm] + extra_k + extra_v,
        out_shape=[jax.ShapeDtypeStruct((nb, steps, SWA_Q), F32),
                   jax.ShapeDtypeStruct(ko_shape, F32),
                   jax.ShapeDtypeStruct(vo_shape, F32)],
        in_specs=[
            pl.BlockSpec((nseq, steps, SWA_Q), lambda b, i: (b, 0, col(COL_SWQ))),
            pl.BlockSpec((nseq, steps, SWA_KV), lambda b, i: (b, 0, col(COL_SWK))),
            pl.BlockSpec((nseq, steps, SWA_KV), lambda b, i: (b, 0, col(COL_SWV))),
            pl.BlockSpec((None, nseq, WINDOW, SWA_KV), lambda b, i: (layer, b, 0, 0)),
            pl.BlockSpec((None, nseq, WINDOW, SWA_KV), lambda b, i: (layer, b, 0, 0)),
            pl.BlockSpec((steps, SWA_KV), lambda b, i: (0, 0)),
            pl.BlockSpec((steps, SWA_KV), lambda b, i: (0, 0)),
            pl.BlockSpec((SUBLANES, LANES), lambda b, i: (0, 0)),
        ] + specs_k + specs_v,
        out_specs=[pl.BlockSpec((nseq, steps, SWA_Q), lambda b, i: (b, 0, 0)), ko_spec, vo_spec],
        scratch=[],
        aliases={**alias_k, **alias_v},
    )


def _tail_kernel(x_ref, odn_ref, y_ref, osw_ref, wout_ref, g1_ref, g2_ref, g3_ref, wfi_ref, wfo_ref, o_ref,
                 *, d_ff, tf):
    mixed = jnp.concatenate([odn_ref[...], y_ref[...], osw_ref[...]], axis=1).astype(BF16)
    m = jnp.dot(mixed, wout_ref[...], preferred_element_type=F32)
    x1 = x_ref[...] + _rmsnorm(m, g1_ref[...])
    h = _rmsnorm(x1, g2_ref[...]).astype(BF16)
    y2 = None
    for c in range(d_ff // tf):
        gate = jnp.dot(h, wfi_ref[:, c * tf:(c + 1) * tf], preferred_element_type=F32)
        up = jnp.dot(h, wfi_ref[:, d_ff + c * tf:d_ff + (c + 1) * tf], preferred_element_type=F32)
        part = jnp.dot((_silu(gate) * up).astype(BF16), wfo_ref[c * tf:(c + 1) * tf, :], preferred_element_type=F32)
        y2 = part if y2 is None else y2 + part
    o_ref[...] = x1 + _rmsnorm(y2, g3_ref[...])


def _layer_spec(shape, layer):
    nd = len(shape) - 1
    return pl.BlockSpec((None,) + tuple(shape[1:]), lambda *_: (layer,) + (0,) * nd, pipeline_mode=pl.Buffered(1))


def _tail(x, odn, y, osw, wout, g1, g2, g3, wfi, wfo, layer, l_out, tm_target):
    nb, _, d = x.shape
    d_ff = wfo.shape[1]
    tm = _pick_tile(l_out, tm_target)
    tf = 2 * LANES if d_ff % (2 * LANES) == 0 else d_ff
    row = lambda w: pl.BlockSpec((None, tm, w), lambda b, i: (b, i, 0))
    return pl.pallas_call(
        functools.partial(_tail_kernel, d_ff=d_ff, tf=tf),
        out_shape=jax.ShapeDtypeStruct((nb, l_out, d), F32),
        grid=(nb, l_out // tm),
        in_specs=[row(d), row(DN_V), row(SSM_INNER), row(SWA_Q), _layer_spec(wout.shape, layer),
                  _layer_spec(g1.shape, layer), _layer_spec(g2.shape, layer), _layer_spec(g3.shape, layer),
                  _layer_spec(wfi.shape, layer), _layer_spec(wfo.shape, layer)],
        out_specs=row(d),
        compiler_params=pltpu.CompilerParams(dimension_semantics=("arbitrary", "arbitrary"),
                                             vmem_limit_bytes=VMEM_LIMIT),
        name="outproj_ffn",
    )(x, odn, y, osw, wout, g1, g2, g3, wfi, wfo)


def _reorder_w_in_kernel(w_ref, o_ref):
    w = w_ref[0]
    offs = [0]
    for wd in IN_WIDTHS:
        offs.append(offs[-1] + wd)
    seg = lambda i: w[:, offs[i]:offs[i + 1]]
    dn_qkv, dn_z, dn_b, dn_a, ssm_xbc, ssm_z, ssm_dt, sw_q, sw_k, sw_v = (seg(i) for i in range(len(IN_WIDTHS)))
    n_small = dn_b.shape[1] + dn_a.shape[1] + ssm_dt.shape[1]
    small = jnp.concatenate([dn_b, dn_a, ssm_dt, jnp.zeros((w.shape[0], LANES - n_small), w.dtype)], axis=1)
    o_ref[0] = jnp.concatenate([dn_qkv, dn_z, ssm_z, ssm_xbc, sw_q, sw_k, sw_v, small], axis=1)


def _reorder_w_in(w):
    depth, d, d_in = w.shape
    assert d_in == sum(IN_WIDTHS)
    tr = _pick_tile(d, 256)
    return pl.pallas_call(
        _reorder_w_in_kernel,
        out_shape=jax.ShapeDtypeStruct((depth, d, D_PROJ), BF16),
        grid=(depth, d // tr),
        in_specs=[pl.BlockSpec((1, tr, d_in), lambda l, i: (l, i, 0))],
        out_specs=pl.BlockSpec((1, tr, D_PROJ), lambda l, i: (l, i, 0)),
        compiler_params=pltpu.CompilerParams(dimension_semantics=("arbitrary", "arbitrary")),
        name="reorder_w_in",
    )(w)


def _scalar_param_tiles(rows):
    depth = rows[0][1].shape[0]
    padded = [jnp.pad(v.astype(F32), ((0, 0), (off, LANES - off - v.shape[1]))) for off, v in rows]
    padded += [jnp.zeros((depth, LANES), F32)] * (SUBLANES - len(rows))
    return jnp.stack(padded, axis=1)


def _rope_tables(pos):
    half = SWA_HEAD_DIM // 2
    inv = ROPE_THETA ** (-jnp.arange(half, dtype=F32) / half)
    ang = pos.astype(F32)[:, None] * inv[None, :]
    cos = jnp.cos(ang)
    sin = jnp.sin(ang)
    cos_t = jnp.concatenate([cos, cos] * SWA_KV_HEADS, axis=1)
    sin_t = jnp.concatenate([-sin, sin] * SWA_KV_HEADS, axis=1)
    return cos_t, sin_t


def kernel(x_prompt, x_sample, state_dn, state_dn_conv, state_ssm, state_ssm_conv, cache_swa_k, cache_swa_v,
           meta_tokens, w_in, dn_conv_w, dn_a_log, dn_dt_bias, dn_norm_w, ssm_conv_w, ssm_conv_b, ssm_a_log,
           ssm_dt_bias, ssm_d, ssm_norm_w, swa_sinks, w_out, g_pre_mix, g_post_mix, g_pre_ffn, g_post_ffn,
           w_ffn_in, w_ffn_out):
    bp, seq, d = x_prompt.shape
    bs, ts, _ = x_sample.shape
    depth = w_in.shape[0]
    lp = N_META + seq + FRONT_PAD
    assert lp % BLOCK == 0 and BLOCK % CHUNK == 0 and seq % BLOCK == 0
    nblk = lp // BLOCK
    blk0 = nblk - 1
    pad_range = (seq, seq + FRONT_PAD)

    zpad = jnp.zeros((bp, FRONT_PAD, d), x_prompt.dtype)
    meta = jnp.broadcast_to(meta_tokens.astype(x_prompt.dtype)[None], (bp, N_META, d))
    xp = jnp.concatenate([x_prompt, zpad, meta], axis=1)
    xs = x_sample.reshape(1, bs * ts, d)

    cos_p, sin_p = _rope_tables(jnp.arange(lp, dtype=jnp.int32) - FRONT_PAD)
    cos_s, sin_s = _rope_tables(PAST_LEN + jnp.arange(ts, dtype=jnp.int32))

    gw = (SSM_HEADS // SSM_GROUPS) * SSM_HEADDIM
    nseq_s = _pick_tile(bs, 8) if bs % SUBLANES == 0 else bs
    zero_dn = jnp.zeros((1, bp, DN_HEADS, DN_DK, DN_DV), F32)
    zero_dnc = jnp.zeros((bp, SUBLANES, DN_CONV), F32)
    zero_ssm = jnp.zeros((1, bp, SSM_GROUPS, gw, SSM_STATE), F32)
    zero_ssmc = jnp.zeros((bp, SUBLANES, SSM_CONV), F32)
    state_ssm_g = state_ssm.reshape(depth, bs, SSM_GROUPS, gw, SSM_STATE)
    cache_k = cache_swa_k.reshape(depth, bs, WINDOW, SWA_KV)
    cache_v = cache_swa_v.reshape(depth, bs, WINDOW, SWA_KV)

    w_in_r = _reorder_w_in(w_in.astype(BF16))
    w_out_b = w_out.astype(BF16)
    w_fi_b = w_ffn_in.astype(BF16)
    w_fo_b = w_ffn_out.astype(BF16)
    g1, g2, g3, g4 = (a[:, None, :] for a in (g_pre_mix, g_post_mix, g_pre_ffn, g_post_ffn))

    new_p, new_s = [], []
    dn_s = ssm_s = k_s = v_s = None
    prm_all = _scalar_param_tiles([(SM_A, dn_a_log), (SM_A, dn_dt_bias), (SM_DT, ssm_a_log), (SM_DT, ssm_dt_bias),
                                   (0, swa_sinks)])
    drow_all = jnp.repeat(ssm_d, SSM_HEADDIM, axis=1)
    dn_cbuf_s = jnp.pad(state_dn_conv, ((0, 0), (0, 0), (SUBLANES - (CONV_WIDTH - 1), 0), (0, 0)))
    ssm_cbuf_s = jnp.pad(state_ssm_conv, ((0, 0), (0, 0), (SUBLANES - (CONV_WIDTH - 1), 0), (0, 0)))
    for l in range(depth):
        prm = prm_all[l]
        dn_nw = dn_norm_w[l][None, :]
        ssm_nw = ssm_norm_w[l][None, :]
        drow = drow_all[l][None, :]
        cbias = ssm_conv_b[l][None, :]
        last = l == depth - 1

        proj = _inproj(xp, g1, w_in_r, l, pad_range, TM_DENSE)
        (odn, dn_p, dnc_p), = _run_parts([_dn_mixer(proj, nblk, blk0, BLOCK, CHUNK, FRONT_PAD, bp, zero_dn, 0,
                                                    zero_dnc, dn_conv_w[l], prm, dn_nw, out_dtype=BF16)], "dn_mixer")
        (ys, ssm_p, ssmc_p), = _run_parts([_ssd_mixer(proj, nblk, blk0, BLOCK, CHUNK, FRONT_PAD, bp, zero_ssm, 0,
                                                      zero_ssmc, ssm_conv_w[l], cbias, prm, drow, ssm_nw,
                                                      out_dtype=BF16)], "ssd_mixer")
        (osw, k_p, v_p), = _run_parts([_swa_prompt(proj, nblk, blk0, FRONT_PAD, cos_p, sin_p, prm, out_dtype=BF16)],
                                      "swa_prompt")
        xp = _tail(xp, odn, ys, osw, w_out_b, g2, g3, g4, w_fi_b, w_fo_b, l, seq if last else lp, TM_DENSE)
        new_p.append((dn_p, dnc_p[:, -(CONV_WIDTH - 1):], ssm_p.reshape(bp, SSM_HEADS, SSM_HEADDIM, SSM_STATE),
                      ssmc_p[:, -(CONV_WIDTH - 1):], k_p.reshape(bp, WINDOW, SWA_KV_HEADS, SWA_HEAD_DIM),
                      v_p.reshape(bp, WINDOW, SWA_KV_HEADS, SWA_HEAD_DIM)))

        proj = _inproj(xs, g1, w_in_r, l, None, TM_DENSE).reshape(bs, ts, D_PROJ)
        (odn, dn_s, dnc_s), (ys, ssm_s, ssmc_s), (osw, k_s, v_s) = _run_parts([
            _dn_mixer(proj, 1, 0, ts, ts, 0, nseq_s, state_dn, l, dn_cbuf_s[l], dn_conv_w[l],
                      prm, dn_nw, stack=(depth, dn_s)),
            _ssd_mixer(proj, 1, 0, ts, ts, 0, nseq_s, state_ssm_g, l, ssm_cbuf_s[l],
                       ssm_conv_w[l], cbias, prm, drow, ssm_nw, stack=(depth, ssm_s)),
            _swa_sample(proj, nseq_s, cache_k, cache_v, l, cos_s, sin_s, prm,
                        stack_k=(depth, k_s), stack_v=(depth, v_s))], "mixers_sample")
        flat = lambda a: a.reshape(1, bs * ts, a.shape[-1])
        xs = _tail(xs, flat(odn), flat(ys), flat(osw), w_out_b, g2, g3, g4, w_fi_b, w_fo_b, l, bs * ts, TM_DENSE)
        new_s.append((dnc_s[:, -(CONV_WIDTH - 1):], ssmc_s[:, -(CONV_WIDTH - 1):]))

    outs_p = tuple(jnp.stack([st[i] for st in new_p]) for i in range(6))
    dnc_s, ssmc_s = (jnp.stack([st[i] for st in new_s]) for i in range(2))
    outs_s = (dn_s, dnc_s, ssm_s.reshape(depth, bs, SSM_HEADS, SSM_HEADDIM, SSM_STATE), ssmc_s,
              k_s.reshape(depth, bs, WINDOW, SWA_KV_HEADS, SWA_HEAD_DIM),
              v_s.reshape(depth, bs, WINDOW, SWA_KV_HEADS, SWA_HEAD_DIM))
    return (xp, xs.reshape(bs, ts, d)) + outs_p + outs_s
```

```python
import functools

import jax
import jax.numpy as jnp
from jax import lax
from jax.experimental import pallas as pl
from jax.experimental.pallas import tpu as pltpu

F32 = jnp.float32
BF16 = jnp.bfloat16
NT = (((1,), (1,)), ((), ()))
TN = (((0,), (0,)), ((), ()))

N_META = 16
CONV_WIDTH = 4
CHUNK = 64
BLOCK = 128
WINDOW = 128
FRONT_PAD = BLOCK - N_META
ROPE_THETA = 10000.0
PAST_LEN = 8192
EPS = 1e-6

DN_HEADS, DN_DK, DN_DV = 4, 128, 128
DN_QK = DN_HEADS * DN_DK
DN_V = DN_HEADS * DN_DV
DN_CONV = 2 * DN_QK + DN_V
SSM_HEADS, SSM_HEADDIM, SSM_GROUPS, SSM_STATE = 4, 64, 2, 128
SSM_INNER = SSM_HEADS * SSM_HEADDIM
SSM_BC = SSM_GROUPS * SSM_STATE
SSM_CONV = SSM_INNER + 2 * SSM_BC
SWA_Q_HEADS, SWA_KV_HEADS, SWA_HEAD_DIM = 4, 2, 64
SWA_Q = SWA_Q_HEADS * SWA_HEAD_DIM
SWA_KV = SWA_KV_HEADS * SWA_HEAD_DIM
IN_WIDTHS = (DN_CONV, DN_V, DN_HEADS, DN_HEADS, SSM_CONV, SSM_INNER, SSM_HEADS, SWA_Q, SWA_KV, SWA_KV)

LANES = 128
SUBLANES = 8
COL_QKV = (0, DN_CONV)
COL_DNZ = (1536, DN_V)
COL_SSZ = (2048, SSM_INNER)
COL_XBC = (2304, SSM_CONV)
COL_SWQ = (3072, SWA_Q)
COL_SWK = (3328, SWA_KV)
COL_SWV = (3456, SWA_KV)
COL_SM = (3584, LANES)
D_PROJ = 3712
SM_B, SM_A, SM_DT = 0, 4, 8
PRM_DN_ALOG, PRM_DN_DTB, PRM_SSM_ALOG, PRM_SSM_DTB, PRM_SINK = 0, 1, 2, 3, 4
NEG_BIG = -1e30
VMEM_LIMIT = 56 * 1024 * 1024
TM_DENSE = 640


def _bdot(a, b, dims=None):
    a = a.astype(BF16)
    b = b.astype(BF16)
    if dims is None:
        return jnp.dot(a, b, preferred_element_type=F32)
    return lax.dot_general(a, b, dims, preferred_element_type=F32)


def _cumsum_rows(lmat, g):
    hi = g.astype(BF16)
    r1 = g - hi.astype(F32)
    mid = r1.astype(BF16)
    lo = (r1 - mid.astype(F32)).astype(BF16)
    dot = lambda part: jnp.dot(lmat, part, preferred_element_type=F32)
    return dot(hi) + dot(mid) + dot(lo)


def _rmsnorm(x, g):
    return x * lax.rsqrt(jnp.mean(x * x, axis=-1, keepdims=True) + EPS) * g


def _l2norm(x):
    return x * lax.rsqrt(jnp.sum(x * x, axis=-1, keepdims=True) + EPS)


def _sigmoid(x):
    return 1.0 / (1.0 + jnp.exp(-x))


def _silu(x):
    return x * _sigmoid(x)


def _softplus(x):
    return jnp.maximum(x, 0.0) + jnp.log1p(jnp.exp(-jnp.abs(x)))


def _pick_tile(n, target):
    best = None
    for t in range(SUBLANES, min(n, target) + 1, SUBLANES):
        if n % t == 0:
            best = t
    assert best is not None, n
    return best


def _inproj_kernel(x_ref, g_ref, w_ref, o_ref, *, tm, pad_range):
    h = _rmsnorm(x_ref[...], g_ref[...])
    if pad_range is not None:
        r = pl.program_id(1) * tm + lax.broadcasted_iota(jnp.int32, (tm, 1), 0)
        is_pad = (r >= pad_range[0]) & (r < pad_range[1])
        h = jnp.where(is_pad, 0.0, h)
    o_ref[...] = jnp.dot(h.astype(BF16), w_ref[...], preferred_element_type=F32)


def _inproj(x, g, w, layer, pad_range, tm_target):
    nb, rows, d = x.shape
    tm = _pick_tile(rows, tm_target)
    return pl.pallas_call(
        functools.partial(_inproj_kernel, tm=tm, pad_range=pad_range),
        out_shape=jax.ShapeDtypeStruct((nb, rows, D_PROJ), F32),
        grid=(nb, rows // tm),
        in_specs=[pl.BlockSpec((None, tm, d), lambda b, i: (b, i, 0)), _layer_spec(g.shape, layer),
                  _layer_spec(w.shape, layer)],
        out_specs=pl.BlockSpec((None, tm, D_PROJ), lambda b, i: (b, i, 0)),
        compiler_params=pltpu.CompilerParams(dimension_semantics=("arbitrary", "arbitrary"),
                                             vmem_limit_bytes=VMEM_LIMIT),
        name="inproj",
    )(x, g, w)


def _causal_conv(xbuf, raw_ref, cw_ref, cbo_ref, rows):
    x = raw_ref[...]
    prev = xbuf[0:SUBLANES, :]
    cw = cw_ref[...]
    row = lax.broadcasted_iota(jnp.int32, (SUBLANES, 1), 0)
    acc = None
    for i in range(CONV_WIDTH):
        s = CONV_WIDTH - 1 - i
        if s == 0:
            xs = x
        else:
            r = pltpu.roll(x, s, axis=0)
            head = jnp.where(row < s, pltpu.roll(prev, s, axis=0), r[0:SUBLANES])
            xs = head if rows == SUBLANES else jnp.concatenate([head, r[SUBLANES:]], axis=0)
        term = xs * cw[i:i + 1, :]
        acc = term if acc is None else acc + term
    tail = x[rows - SUBLANES:rows]
    cbo_ref[...] = tail
    xbuf[0:SUBLANES, :] = tail
    return acc


def _chunk_masks(c):
    ii = lax.broadcasted_iota(jnp.int32, (c, c), 0)
    jj = lax.broadcasted_iota(jnp.int32, (c, c), 1)
    return ii >= jj, ii > jj, (ii == jj).astype(F32)


def _segment_decay(gc, gct, lane, ge):
    col = gc[:, lane:lane + 1]
    row = gct[lane:lane + 1, :]
    return jnp.where(ge, jnp.exp(jnp.where(ge, col - row, 0.0)), 0.0)


def _inv_unit_lower_minus_eye(a_list, c, nh):
    w = nh * c
    blk_r = lax.broadcasted_iota(jnp.int32, (w, w), 0) // c
    blk_c = lax.broadcasted_iota(jnp.int32, (w, w), 1) // c
    same = blk_r == blk_c

    def block_diag(p):
        return jnp.where(same, jnp.concatenate([p] * nh, axis=0), 0.0).astype(BF16)

    ys = [-a for a in a_list]
    ps = [_bdot(a, block_diag(a)) for a in a_list]
    yield
    n = 2
    while n < c:
        pbds = [block_diag(p) for p in ps]
        n *= 2
        if n < c:
            sts = [_bdot(jnp.concatenate([y, p], axis=0), pbd) for y, p, pbd in zip(ys, ps, pbds)]
            ys = [y + p + st[:c] for y, p, st in zip(ys, ps, sts)]
            ps = [st[c:] for st in sts]
        else:
            ys = [y + p + _bdot(y, pbd) for y, p, pbd in zip(ys, ps, pbds)]
        yield
    return ys


def _run_parts(parts, name):
    grid = parts[0]["grid"]
    assert all(p["grid"] == grid for p in parts)
    n_in = [len(p["inputs"]) for p in parts]
    n_out = [len(p["out_shape"]) for p in parts]
    n_scr = [len(p["scratch"]) for p in parts]
    aliases = {}
    for k, p in enumerate(parts):
        for i, o in p["aliases"].items():
            aliases[sum(n_in[:k]) + i] = sum(n_out[:k]) + o

    def kernel(*refs):
        ins = refs[:sum(n_in)]
        outs = refs[sum(n_in):sum(n_in) + sum(n_out)]
        scr = refs[sum(n_in) + sum(n_out):]
        gens = [p["body"](ins[sum(n_in[:k]):sum(n_in[:k + 1])], outs[sum(n_out[:k]):sum(n_out[:k + 1])],
                          scr[sum(n_scr[:k]):sum(n_scr[:k + 1])]) for k, p in enumerate(parts)]
        for tag in gens[0]:
            if tag == "chain":
                break
        live = list(gens)
        while live:
            for g in list(live):
                if next(g, StopIteration) is StopIteration:
                    live.remove(g)

    results = pl.pallas_call(
        kernel,
        out_shape=tuple(s for p in parts for s in p["out_shape"]),
        grid=grid,
        in_specs=[s for p in parts for s in p["in_specs"]],
        out_specs=tuple(s for p in parts for s in p["out_specs"]),
        scratch_shapes=[s for p in parts for s in p["scratch"]],
        input_output_aliases=aliases,
        compiler_params=pltpu.CompilerParams(dimension_semantics=("arbitrary",) * len(grid),
                                             vmem_limit_bytes=VMEM_LIMIT),
        name=name,
    )(*[a for p in parts for a in p["inputs"]])
    return [list(results[sum(n_out[:k]):sum(n_out[:k + 1])]) for k in range(len(parts))]


def _state_out(stack, layer, nb, nseq, tail, out_index, n_inputs):
    zeros = (0,) * len(tail)
    if stack is None:
        return ((nb,) + tail, pl.BlockSpec((nseq,) + tail, lambda b, *_: (b,) + zeros), [], [], {}, None, 0)
    depth, prev = stack
    shape = (depth, nb) + tail
    if prev is None:
        spec = pl.BlockSpec((depth, nseq) + tail, lambda b, *_: (0, b) + zeros)
        return (shape, spec, [], [], {}, layer, depth)
    spec = pl.BlockSpec((None, nseq) + tail, lambda b, *_: (layer, b) + zeros)
    return (shape, spec, [prev], [pl.BlockSpec(memory_space=pl.ANY)], {n_inputs: out_index}, None, 0)


def _state_view(ref, slot, n_slots):
    if slot is None:
        return ref
    for other in range(n_slots):
        if other != slot:
            ref[other] = jnp.zeros(ref.shape[1:], ref.dtype)
    return ref.at[slot]


def _dn_prep(items, lmat, ge, gt_all, chunk):
    heads = range(DN_HEADS)
    gcs = [_cumsum_rows(lmat, g_all) for _, _, g_all in items]
    gcts = [gc.T for gc in gcs]
    qs = [[_l2norm(qkv[:, h * DN_DK:(h + 1) * DN_DK]) * DN_DK ** -0.5 for h in heads] for qkv, _, _ in items]
    ks = [[_l2norm(qkv[:, DN_QK + h * DN_DK:DN_QK + (h + 1) * DN_DK]) for h in heads] for qkv, _, _ in items]
    vs = [[qkv[:, 2 * DN_QK + h * DN_DV:2 * DN_QK + (h + 1) * DN_DV] for h in heads] for qkv, _, _ in items]
    betas = [[beta_all[:, SM_B + h:SM_B + h + 1] for h in heads] for _, beta_all, _ in items]
    yield
    kbs = [[k.astype(BF16) for k in kk] for kk in ks]
    kks = [jnp.concatenate([_bdot(kb, kb, NT) for kb in kb4], axis=1) for kb4 in kbs]
    qks = [jnp.concatenate([_bdot(q, kb, NT) for q, kb in zip(q4, kb4)], axis=1) for q4, kb4 in zip(qs, kbs)]
    decs = [jnp.concatenate([_segment_decay(gc, gct, SM_A + h, ge) for h in heads], axis=1)
            for gc, gct in zip(gcs, gcts)]
    beta_ws = [jnp.concatenate([jnp.broadcast_to(b, (chunk, chunk)) for b in b4], axis=1) for b4 in betas]
    a_list = [jnp.where(gt_all, bw * kk * dec, 0.0) for bw, kk, dec in zip(beta_ws, kks, decs)]
    yield "chain"
    n = len(items)
    egs, ekds, egls, attns, rhss, qds, kds = ([None] * n for _ in range(7))

    def side_work(i):
        gc = gcs[i]
        glast = gc[chunk - 1:chunk, :]
        egs[i] = jnp.exp(gc)
        ekds[i] = jnp.exp(glast - gc)
        egls[i] = jnp.exp(glast)
        attns[i] = (qks[i] * decs[i]).astype(BF16)
        lane = lambda a, h: a[:, SM_A + h:SM_A + h + 1]
        rhss[i] = [jnp.concatenate([vs[i][h] * betas[i][h], ks[i][h] * (betas[i][h] * lane(egs[i], h))], axis=1)
                   for h in heads]
        qds[i] = [(qs[i][h] * lane(egs[i], h)).astype(BF16) for h in heads]
        kds[i] = [(ks[i][h] * lane(ekds[i], h)).astype(BF16) for h in heads]

    inverse = _inv_unit_lower_minus_eye(a_list, chunk, DN_HEADS)
    todo = list(range(n))
    while True:
        try:
            next(inverse)
        except StopIteration as done:
            tms = done.value
            break
        if todo:
            side_work(todo.pop(0))
        yield
    for i in todo:
        side_work(i)
    uws = [[rhss[i][h] + _bdot(tms[i][:, h * chunk:(h + 1) * chunk], rhss[i][h]) for h in heads] for i in range(n)]
    yield
    out = []
    for i in range(n):
        per_head = []
        for h in heads:
            uw = uws[i][h]
            wq = jnp.concatenate([uw[:, DN_DV:].astype(BF16), qds[i][h]], axis=0)
            per_head.append((uw[:, :DN_DV], wq, attns[i][:, h * chunk:(h + 1) * chunk], kds[i][h],
                             egls[i][:, SM_A + h:SM_A + h + 1]))
        out.append(per_head)
    return out


def _dn_body(ins, outs, scratch, *, nseq, rows, chunk, front_pad, slot, n_slots):
    qkv_ref, z_ref, sm_ref, s0_ref, cb_ref, cw_ref, prm_ref, nw_ref = ins[:8]
    o_ref, so_full, cbo_ref = outs
    xbuf, = scratch
    so_ref = _state_view(so_full, slot, n_slots)
    blk = pl.program_id(1)
    first = blk == 0

    @pl.when(first)
    def _():
        so_ref[...] = s0_ref[...]
        xbuf[:, 0:SUBLANES, :] = cb_ref[...]

    prm = prm_ref[...]
    nw = nw_ref[...]
    ge, _, _ = _chunk_masks(chunk)
    lmat = jnp.where(ge, 1.0, 0.0).astype(BF16)
    wide = (chunk, DN_HEADS * chunk)
    gt_all = lax.broadcasted_iota(jnp.int32, wide, 0) > lax.broadcasted_iota(jnp.int32, wide, 1) % chunk
    nchunk = rows // chunk

    items = []
    for s in range(nseq):
        qkv = _silu(_causal_conv(xbuf.at[s], qkv_ref.at[s], cw_ref, cbo_ref.at[s], rows))
        sm = sm_ref[s]
        beta_all = _sigmoid(sm)
        g_all = -jnp.exp(prm[PRM_DN_ALOG:PRM_DN_ALOG + 1, :]) * _softplus(sm + prm[PRM_DN_DTB:PRM_DN_DTB + 1, :])
        if front_pad:
            pos = blk * rows + lax.broadcasted_iota(jnp.int32, (rows, 1), 0)
            g_all = jnp.where(pos < front_pad, 0.0, g_all)
        for c in range(nchunk):
            cs = slice(c * chunk, (c + 1) * chunk)
            items.append((qkv[cs], beta_all[cs], g_all[cs]))
        yield
    prep = yield from _dn_prep(items, lmat, ge, gt_all, chunk)

    chains = [(s, h) for s in range(nseq) for h in range(DN_HEADS)]
    states = [so_ref[s, h] for s, h in chains]
    for c in range(nchunk):
        r0 = c * chunk
        fac = [prep[s * nchunk + c][h] for s, h in chains]
        m1s = [_bdot(f[1], st) for f, st in zip(fac, states)]
        yield
        v_news = [(f[0] - m1[:chunk]).astype(BF16) for f, m1 in zip(fac, m1s)]
        ups = [_bdot(f[3], v, TN) for f, v in zip(fac, v_news)]
        os_ = [m1[chunk:] + _bdot(f[2], v) for f, m1, v in zip(fac, m1s, v_news)]
        yield
        states = [st * f[4] + up for f, st, up in zip(fac, states, ups)]
        for (s, h), o in zip(chains, os_):
            lo = h * DN_DV
            zh = z_ref[s, r0:r0 + chunk, lo:lo + DN_DV]
            o_ref[s, r0:r0 + chunk, lo:lo + DN_DV] = (_rmsnorm(o, nw) * _silu(zh)).astype(o_ref.dtype)
        yield
    for (s, h), st in zip(chains, states):
        so_ref[s, h] = st


def _dn_mixer(proj, nblk, blk0, rows, chunk, front_pad, nseq, s0, layer, cbuf, cw, prm, nw, stack=None,
              out_dtype=F32):
    nb = proj.shape[0]
    col = lambda c: c[0] // c[1]
    phys = lambda i: (i + blk0) % nblk
    so_shape, so_spec, extra, extra_specs, aliases, slot, n_slots = _state_out(
        stack, layer, nb, nseq, (DN_HEADS, DN_DK, DN_DV), 1, 8)
    return dict(
        body=functools.partial(_dn_body, nseq=nseq, rows=rows, chunk=chunk, front_pad=front_pad,
                               slot=slot, n_slots=n_slots),
        grid=(nb // nseq, nblk),
        inputs=[proj, proj, proj, s0, cbuf, cw, prm, nw] + extra,
        out_shape=[jax.ShapeDtypeStruct((nb, nblk * rows, DN_V), out_dtype),
                   jax.ShapeDtypeStruct(so_shape, F32),
                   jax.ShapeDtypeStruct((nb, SUBLANES, DN_CONV), F32)],
        in_specs=[
            pl.BlockSpec((nseq, rows, DN_CONV), lambda b, i: (b, phys(i), col(COL_QKV))),
            pl.BlockSpec((nseq, rows, DN_V), lambda b, i: (b, phys(i), col(COL_DNZ))),
            pl.BlockSpec((nseq, rows, LANES), lambda b, i: (b, phys(i), col(COL_SM))),
            pl.BlockSpec((None, nseq, DN_HEADS, DN_DK, DN_DV), lambda b, i: (layer, b, 0, 0, 0)),
            pl.BlockSpec((nseq, SUBLANES, DN_CONV), lambda b, i: (b, 0, 0)),
            pl.BlockSpec((CONV_WIDTH, DN_CONV), lambda b, i: (0, 0)),
            pl.BlockSpec((SUBLANES, LANES), lambda b, i: (0, 0)),
            pl.BlockSpec((1, DN_DV), lambda b, i: (0, 0)),
        ] + extra_specs,
        out_specs=[
            pl.BlockSpec((nseq, rows, DN_V), lambda b, i: (b, phys(i), 0)),
            so_spec,
            pl.BlockSpec((nseq, SUBLANES, DN_CONV), lambda b, i: (b, 0, 0)),
        ],
        scratch=[pltpu.VMEM((nseq, SUBLANES, DN_CONV), F32)],
        aliases=aliases,
    )


def _ssd_body(ins, outs, scratch, *, nseq, rows, chunk, front_pad, slot, n_slots):
    xbc_ref, z_ref, sm_ref, h0_ref, cb_ref, cw_ref, cbias_ref, prm_ref, drow_ref, nw_ref = ins[:10]
    y_ref, ho_full, cbo_ref = outs
    xbuf, = scratch
    ho_ref = _state_view(ho_full, slot, n_slots)
    blk = pl.program_id(1)

    @pl.when(blk == 0)
    def _():
        ho_ref[...] = h0_ref[...]
        xbuf[:, 0:SUBLANES, :] = cb_ref[...]

    prm = prm_ref[...]
    nw = nw_ref[...]
    drow = drow_ref[...]
    cbias = cbias_ref[...]
    ge, _, _ = _chunk_masks(chunk)
    lmat = jnp.where(ge, 1.0, 0.0).astype(BF16)
    hpg = SSM_HEADS // SSM_GROUPS
    gw = hpg * SSM_HEADDIM
    lane = lax.broadcasted_iota(jnp.int32, (1, gw), 1)
    srow = lax.broadcasted_iota(jnp.int32, (gw, 1), 0)
    in_head = [(lane >= j * SSM_HEADDIM) & (lane < (j + 1) * SSM_HEADDIM) for j in range(hpg)]
    nchunk = rows // chunk
    groups = range(SSM_GROUPS)

    items = []
    for s in range(nseq):
        act = _silu(_causal_conv(xbuf.at[s], xbc_ref.at[s], cw_ref, cbo_ref.at[s], rows) + cbias)
        dt_all = _softplus(sm_ref[s] + prm[PRM_SSM_DTB:PRM_SSM_DTB + 1, :])
        if front_pad:
            pos = blk * rows + lax.broadcasted_iota(jnp.int32, (rows, 1), 0)
            dt_all = jnp.where(pos < front_pad, 0.0, dt_all)
        g_all = dt_all * (-jnp.exp(prm[PRM_SSM_ALOG:PRM_SSM_ALOG + 1, :]))
        for c in range(nchunk):
            cs = slice(c * chunk, (c + 1) * chunk)
            items.append((act[cs], dt_all[cs], g_all[cs]))
        yield
    n_items = len(items)
    gcs = [_cumsum_rows(lmat, g) for _, _, g in items]
    gcts = [gc.T for gc in gcs]
    xgs = [[a[:, g * gw:(g + 1) * gw] for g in groups] for a, _, _ in items]
    bgs = [[a[:, SSM_INNER + g * SSM_STATE:SSM_INNER + (g + 1) * SSM_STATE] for g in groups] for a, _, _ in items]
    cgs = [[a[:, SSM_INNER + SSM_BC + g * SSM_STATE:SSM_INNER + SSM_BC + (g + 1) * SSM_STATE] for g in groups]
           for a, _, _ in items]
    cbs = [[_bdot(cgs[i][g], bgs[i][g], NT) for g in groups] for i in range(n_items)]
    yield
    egs = [jnp.exp(gc) for gc in gcs]
    ekds = [jnp.exp(gc[chunk - 1:chunk, :] - gc) for gc in gcs]
    egls = [jnp.exp(gc[chunk - 1:chunk, :]) for gc in gcs]
    heads = [(g, j) for g in groups for j in range(hpg)]
    ln = lambda g, j: SM_DT + g * hpg + j
    xdts = [[jnp.where(in_head[j], xgs[i][g] * items[i][1][:, ln(g, j):ln(g, j) + 1], 0.0).astype(BF16)
             for g, j in heads] for i in range(n_items)]
    attns = [[cbs[i][g] * _segment_decay(gcs[i], gcts[i], ln(g, j), ge) for g, j in heads] for i in range(n_items)]
    y_intras = [[_bdot(attns[i][k], xdts[i][k]) for k in range(len(heads))] for i in range(n_items)]
    yield
    upds = [[_bdot(xdts[i][k], bgs[i][g] * ekds[i][:, ln(g, j):ln(g, j) + 1], TN) for k, (g, j) in enumerate(heads)]
            for i in range(n_items)]
    yield
    cds = [[jnp.concatenate([cgs[i][g] * egs[i][:, ln(g, j):ln(g, j) + 1] for j in range(hpg)], axis=0).astype(BF16)
            for g in groups] for i in range(n_items)]
    y_loc = [[sum(y_intras[i][g * hpg + j] for j in range(hpg)) for g in groups] for i in range(n_items)]
    h_inc = [[sum(upds[i][g * hpg + j] for j in range(hpg)) for g in groups] for i in range(n_items)]
    gl_cols = []
    for i in range(n_items):
        per_group = []
        for g in groups:
            gl = egls[i][:, ln(g, 0):ln(g, 0) + 1]
            for j in range(1, hpg):
                gl = jnp.where(srow < j * SSM_HEADDIM, gl, egls[i][:, ln(g, j):ln(g, j) + 1])
            per_group.append(gl)
        gl_cols.append(per_group)

    chains = [(s, g) for s in range(nseq) for g in groups]
    states = [ho_ref[s, g] for s, g in chains]
    for c in range(nchunk):
        r0 = c * chunk
        idx = [s * nchunk + c for s, _ in chains]
        yis = [_bdot(cds[i][g], st, NT) for i, (_, g), st in zip(idx, chains, states)]
        yield
        states = [st * gl_cols[i][g] + h_inc[i][g] for i, (_, g), st in zip(idx, chains, states)]
        for i, (s, g), yi in zip(idx, chains, yis):
            y_inter = yi[0:chunk]
            for j in range(1, hpg):
                y_inter = jnp.where(in_head[j], yi[j * chunk:(j + 1) * chunk], y_inter)
            yg = y_loc[i][g] + y_inter + xgs[i][g] * drow[:, g * gw:(g + 1) * gw]
            yg = yg * _silu(z_ref[s, r0:r0 + chunk, g * gw:(g + 1) * gw])
            y_ref[s, r0:r0 + chunk, g * gw:(g + 1) * gw] = _rmsnorm(yg, nw[:, g * gw:(g + 1) * gw]).astype(y_ref.dtype)
        yield
    for (s, g), st in zip(chains, states):
        ho_ref[s, g] = st


def _ssd_mixer(proj, nblk, blk0, rows, chunk, front_pad, nseq, h0, layer, cbuf, cw, cbias, prm, drow, nw,
               stack=None, out_dtype=F32):
    nb = proj.shape[0]
    col = lambda c: c[0] // c[1]
    phys = lambda i: (i + blk0) % nblk
    gw = (SSM_HEADS // SSM_GROUPS) * SSM_HEADDIM
    ho_shape, ho_spec, extra, extra_specs, aliases, slot, n_slots = _state_out(
        stack, layer, nb, nseq, (SSM_GROUPS, gw, SSM_STATE), 1, 10)
    return dict(
        body=functools.partial(_ssd_body, nseq=nseq, rows=rows, chunk=chunk, front_pad=front_pad,
                               slot=slot, n_slots=n_slots),
        grid=(nb // nseq, nblk),
        inputs=[proj, proj, proj, h0, cbuf, cw, cbias, prm, drow, nw] + extra,
        out_shape=[jax.ShapeDtypeStruct((nb, nblk * rows, SSM_INNER), out_dtype),
                   jax.ShapeDtypeStruct(ho_shape, F32),
                   jax.ShapeDtypeStruct((nb, SUBLANES, SSM_CONV), F32)],
        in_specs=[
            pl.BlockSpec((nseq, rows, SSM_CONV), lambda b, i: (b, phys(i), col(COL_XBC))),
            pl.BlockSpec((nseq, rows, SSM_INNER), lambda b, i: (b, phys(i), col(COL_SSZ))),
            pl.BlockSpec((nseq, rows, LANES), lambda b, i: (b, phys(i), col(COL_SM))),
            pl.BlockSpec((None, nseq, SSM_GROUPS, gw, SSM_STATE), lambda b, i: (layer, b, 0, 0, 0)),
            pl.BlockSpec((nseq, SUBLANES, SSM_CONV), lambda b, i: (b, 0, 0)),
            pl.BlockSpec((CONV_WIDTH, SSM_CONV), lambda b, i: (0, 0)),
            pl.BlockSpec((1, SSM_CONV), lambda b, i: (0, 0)),
            pl.BlockSpec((SUBLANES, LANES), lambda b, i: (0, 0)),
            pl.BlockSpec((1, SSM_INNER), lambda b, i: (0, 0)),
            pl.BlockSpec((1, SSM_INNER), lambda b, i: (0, 0)),
        ] + extra_specs,
        out_specs=[
            pl.BlockSpec((nseq, rows, SSM_INNER), lambda b, i: (b, phys(i), 0)),
            ho_spec,
            pl.BlockSpec((nseq, SUBLANES, SSM_CONV), lambda b, i: (b, 0, 0)),
        ],
        scratch=[pltpu.VMEM((nseq, SUBLANES, SSM_CONV), F32)],
        aliases=aliases,
    )


def _rope(x, cos, sin_signed):
    w = x.shape[-1]
    half = SWA_HEAD_DIM // 2
    lane = lax.broadcasted_iota(jnp.int32, (1, w), 1)
    first_half = (lane % SWA_HEAD_DIM) < half
    swapped = jnp.where(first_half, pltpu.roll(x, w - half, axis=1), pltpu.roll(x, half, axis=1))
    return x * cos + swapped * sin_signed


def _sink_attend(problems):
    scale = SWA_HEAD_DIM ** -0.5
    scores = [[jnp.where(m, _bdot(q, k, NT) * scale, NEG_BIG) for k, m in zip(keys, masks)]
              for q, keys, _, masks, _ in problems]
    yield
    outs = []
    probs, dens = [], []
    def lane_reduce(tiles, combine, reduce):
        merged = {}
        for t in tiles:
            merged[t.shape[-1]] = t if t.shape[-1] not in merged else combine(merged[t.shape[-1]], t)
        return [reduce(t, axis=-1, keepdims=True) for t in merged.values()]

    for (_, _, _, _, sink), ss in zip(problems, scores):
        mx = sink
        for m in lane_reduce(ss, jnp.maximum, jnp.max):
            mx = jnp.maximum(mx, m)
        probs.append([jnp.exp(s - mx) for s in ss])
        dens.append(jnp.exp(sink - mx))
    yield
    pvs = [[_bdot(p, v) for p, v in zip(ps, vals)] for (_, _, vals, _, _), ps in zip(problems, probs)]
    yield
    for pv, sink_term in zip(pvs, dens):
        acc = pv[0]
        for extra in pv[1:]:
            acc = acc + extra
        outs.append(acc / (pltpu.roll(acc, SWA_HEAD_DIM, axis=1) + sink_term))
    return outs


def _swa_head_order():
    grp = SWA_Q_HEADS // SWA_KV_HEADS
    assert SWA_KV_HEADS * SWA_HEAD_DIM == LANES
    return [j * grp + t for t in range(grp) for j in range(SWA_KV_HEADS)]


def _swa_problems(q, key_sets, val_sets, masks, prm, tq):
    grp = SWA_Q_HEADS // SWA_KV_HEADS
    row = lax.broadcasted_iota(jnp.int32, (grp * tq, 1), 0)
    lane = lax.broadcasted_iota(jnp.int32, (1, LANES), 1)
    qst = jnp.concatenate([q[:, t * LANES:(t + 1) * LANES] for t in range(grp)], axis=0).astype(BF16)
    problems = []
    for j in range(SWA_KV_HEADS):
        half = (lane >= j * SWA_HEAD_DIM) & (lane < (j + 1) * SWA_HEAD_DIM)
        sink = prm[PRM_SINK:PRM_SINK + 1, j * grp:j * grp + 1]
        for t in range(1, grp):
            sink = jnp.where(row < t * tq, sink, prm[PRM_SINK:PRM_SINK + 1, j * grp + t:j * grp + t + 1])
        problems.append((qst, [jnp.where(half, k, 0.0) for k in key_sets],
                         [jnp.where(half, v, 1.0) for v in val_sets], masks, sink))
    return problems


def _swa_tiles(outs, tq):
    grp = SWA_Q_HEADS // SWA_KV_HEADS
    lane = lax.broadcasted_iota(jnp.int32, (1, LANES), 1)
    tiles = []
    for t in range(grp):
        tile = outs[0][t * tq:(t + 1) * tq]
        for j in range(1, SWA_KV_HEADS):
            tile = jnp.where(lane < j * SWA_HEAD_DIM, tile, outs[j][t * tq:(t + 1) * tq])
        tiles.append(tile)
    return tiles


def _swa_prompt_body(ins, outs, scratch, *, nseq, front_pad):
    q_ref, k_ref, v_ref, cos_ref, sin_ref, prm_ref = ins
    o_ref, ko_ref, vo_ref = outs
    kprev, vprev = scratch
    blk = pl.program_id(1)

    @pl.when(blk == 0)
    def _():
        kprev[...] = jnp.zeros_like(kprev)
        vprev[...] = jnp.zeros_like(vprev)

    cos = cos_ref[...]
    sin = sin_ref[...]
    cos_q = jnp.concatenate([cos, cos], axis=1)
    sin_q = jnp.concatenate([sin, sin], axis=1)
    grp = SWA_Q_HEADS // SWA_KV_HEADS
    qi = lax.broadcasted_iota(jnp.int32, (grp * BLOCK, BLOCK), 0) % BLOCK
    kj = lax.broadcasted_iota(jnp.int32, (grp * BLOCK, BLOCK), 1)
    mask_cur = (kj <= qi) & (blk * BLOCK + kj >= front_pad)
    mask_prev = (kj > qi) & ((blk - 1) * BLOCK + kj >= front_pad)
    prm = prm_ref[...]
    problems = []
    for s in range(nseq):
        q = _rope(q_ref[s], cos_q, sin_q)
        k = _rope(k_ref[s], cos, sin)
        v = v_ref[s]
        problems += _swa_problems(q, (kprev[s], k), (vprev[s], v), (mask_prev, mask_cur), prm, BLOCK)
        kprev[s] = k
        vprev[s] = v
        ko_ref[s] = k
        vo_ref[s] = v
        yield
    res = yield from _sink_attend(problems)
    for s in range(nseq):
        for t, tile in enumerate(_swa_tiles(res[s * SWA_KV_HEADS:(s + 1) * SWA_KV_HEADS], BLOCK)):
            o_ref[s, :, t * LANES:(t + 1) * LANES] = tile.astype(o_ref.dtype)


def _swa_prompt(proj, nblk, blk0, front_pad, cos, sin, prm, out_dtype=F32):
    nb = proj.shape[0]
    col = lambda c: c[0] // c[1]
    phys = lambda i: (i + blk0) % nblk
    return dict(
        body=functools.partial(_swa_prompt_body, nseq=nb, front_pad=front_pad),
        grid=(1, nblk),
        inputs=[proj, proj, proj, cos, sin, prm],
        out_shape=[jax.ShapeDtypeStruct((nb, nblk * BLOCK, SWA_Q), out_dtype),
                   jax.ShapeDtypeStruct((nb, WINDOW, SWA_KV), F32),
                   jax.ShapeDtypeStruct((nb, WINDOW, SWA_KV), F32)],
        in_specs=[
            pl.BlockSpec((nb, BLOCK, SWA_Q), lambda b, i: (0, phys(i), col(COL_SWQ))),
            pl.BlockSpec((nb, BLOCK, SWA_KV), lambda b, i: (0, phys(i), col(COL_SWK))),
            pl.BlockSpec((nb, BLOCK, SWA_KV), lambda b, i: (0, phys(i), col(COL_SWV))),
            pl.BlockSpec((BLOCK, SWA_KV), lambda b, i: (i, 0)),
            pl.BlockSpec((BLOCK, SWA_KV), lambda b, i: (i, 0)),
            pl.BlockSpec((SUBLANES, LANES), lambda b, i: (0, 0)),
        ],
        out_specs=[
            pl.BlockSpec((nb, BLOCK, SWA_Q), lambda b, i: (0, phys(i), 0)),
            pl.BlockSpec((nb, WINDOW, SWA_KV), lambda b, i: (0, 0, 0)),
            pl.BlockSpec((nb, WINDOW, SWA_KV), lambda b, i: (0, 0, 0)),
        ],
        scratch=[pltpu.VMEM((nb, BLOCK, SWA_KV), F32), pltpu.VMEM((nb, BLOCK, SWA_KV), F32)],
        aliases={},
    )


def _swa_sample_body(ins, outs, scratch, *, nseq, steps, slot, n_slots):
    q_ref, k_ref, v_ref, kc_ref, vc_ref, cos_ref, sin_ref, prm_ref = ins[:8]
    o_ref, ko_full, vo_full = outs
    ko_ref = _state_view(ko_full, slot, n_slots)
    vo_ref = _state_view(vo_full, slot, n_slots)
    cos = cos_ref[...]
    sin = sin_ref[...]
    cos_q = jnp.concatenate([cos, cos], axis=1)
    sin_q = jnp.concatenate([sin, sin], axis=1)
    prm = prm_ref[...]
    grp = SWA_Q_HEADS // SWA_KV_HEADS
    ti = lax.broadcasted_iota(jnp.int32, (grp * steps, WINDOW), 0) % steps
    sj = lax.broadcasted_iota(jnp.int32, (grp * steps, WINDOW), 1)
    mask_cache = sj > ti
    tn = lax.broadcasted_iota(jnp.int32, (grp * steps, steps), 0) % steps
    sn = lax.broadcasted_iota(jnp.int32, (grp * steps, steps), 1)
    mask_new = sn <= tn
    problems = []
    for b in range(nseq):
        q = _rope(q_ref[b], cos_q, sin_q)
        k = _rope(k_ref[b], cos, sin)
        v = v_ref[b]
        kc = kc_ref[b]
        vc = vc_ref[b]
        ko_ref[b, 0:WINDOW - steps, :] = kc[steps:WINDOW, :]
        ko_ref[b, WINDOW - steps:WINDOW, :] = k
        vo_ref[b, 0:WINDOW - steps, :] = vc[steps:WINDOW, :]
        vo_ref[b, WINDOW - steps:WINDOW, :] = v
        problems += _swa_problems(q, (kc, k), (vc, v), (mask_cache, mask_new), prm, steps)
        yield
    res = yield from _sink_attend(problems)
    for b in range(nseq):
        for t, tile in enumerate(_swa_tiles(res[b * SWA_KV_HEADS:(b + 1) * SWA_KV_HEADS], steps)):
            o_ref[b, :, t * LANES:(t + 1) * LANES] = tile


def _swa_sample(proj, nseq, kc, vc, layer, cos, sin, prm, stack_k=None, stack_v=None):
    nb, steps, _ = proj.shape
    assert WINDOW > steps
    col = lambda c: c[0] // c[1]
    ko_shape, ko_spec, extra_k, specs_k, alias_k, slot, n_slots = _state_out(
        stack_k, layer, nb, nseq, (WINDOW, SWA_KV), 1, 8)
    vo_shape, vo_spec, extra_v, specs_v, alias_v, _, _ = _state_out(
        stack_v, layer, nb, nseq, (WINDOW, SWA_KV), 2, 8 + len(extra_k))
    return dict(
        body=functools.partial(_swa_sample_body, nseq=nseq, steps=steps, slot=slot, n_slots=n_slots),
        grid=(nb // nseq, 1),
        inputs=[proj, proj, proj, kc, vc, cos, sin, prm] + extra_k + extra_v,
        out_shape=[jax.ShapeDtypeStruct((nb, steps, SWA_Q), F32),
                   jax.ShapeDtypeStruct(ko_shape, F32),
                   jax.ShapeDtypeStruct(vo_shape, F32)],
        in_specs=[
            pl.BlockSpec((nseq, steps, SWA_Q), lambda b, i: (b, 0, col(COL_SWQ))),
            pl.BlockSpec((nseq, steps, SWA_KV), lambda b, i: (b, 0, col(COL_SWK))),
            pl.BlockSpec((nseq, steps, SWA_KV), lambda b, i: (b, 0, col(COL_SWV))),
            pl.BlockSpec((None, nseq, WINDOW, SWA_KV), lambda b, i: (layer, b, 0, 0)),
            pl.BlockSpec((None, nseq, WINDOW, SWA_KV), lambda b, i: (layer, b, 0, 0)),
            pl.BlockSpec((steps, SWA_KV), lambda b, i: (0, 0)),
            pl.BlockSpec((steps, SWA_KV), lambda b, i: (0, 0)),
            pl.BlockSpec((SUBLANES, LANES), lambda b, i: (0, 0)),
        ] + specs_k + specs_v,
        out_specs=[pl.BlockSpec((nseq, steps, SWA_Q), lambda b, i: (b, 0, 0)), ko_spec, vo_spec],
        scratch=[],
        aliases={**alias_k, **alias_v},
    )


def _tail_kernel(x_ref, odn_ref, y_ref, osw_ref, wout_ref, g1_ref, g2_ref, g3_ref, wfi_ref, wfo_ref, o_ref,
                 *, d_ff, tf):
    mixed = jnp.concatenate([odn_ref[...], y_ref[...], osw_ref[...]], axis=1).astype(BF16)
    m = jnp.dot(mixed, wout_ref[...], preferred_element_type=F32)
    x1 = x_ref[...] + _rmsnorm(m, g1_ref[...])
    h = _rmsnorm(x1, g2_ref[...]).astype(BF16)
    y2 = None
    for c in range(d_ff // tf):
        gate = jnp.dot(h, wfi_ref[:, c * tf:(c + 1) * tf], preferred_element_type=F32)
        up = jnp.dot(h, wfi_ref[:, d_ff + c * tf:d_ff + (c + 1) * tf], preferred_element_type=F32)
        part = jnp.dot((_silu(gate) * up).astype(BF16), wfo_ref[c * tf:(c + 1) * tf, :], preferred_element_type=F32)
        y2 = part if y2 is None else y2 + part
    o_ref[...] = x1 + _rmsnorm(y2, g3_ref[...])


def _layer_spec(shape, layer):
    nd = len(shape) - 1
    return pl.BlockSpec((None,) + tuple(shape[1:]), lambda *_: (layer,) + (0,) * nd, pipeline_mode=pl.Buffered(1))


def _tail(x, odn, y, osw, wout, g1, g2, g3, wfi, wfo, layer, l_out, tm_target):
    nb, _, d = x.shape
    d_ff = wfo.shape[1]
    tm = _pick_tile(l_out, tm_target)
    tf = 2 * LANES if d_ff % (2 * LANES) == 0 else d_ff
    row = lambda w: pl.BlockSpec((None, tm, w), lambda b, i: (b, i, 0))
    return pl.pallas_call(
        functools.partial(_tail_kernel, d_ff=d_ff, tf=tf),
        out_shape=jax.ShapeDtypeStruct((nb, l_out, d), F32),
        grid=(nb, l_out // tm),
        in_specs=[row(d), row(DN_V), row(SSM_INNER), row(SWA_Q), _layer_spec(wout.shape, layer),
                  _layer_spec(g1.shape, layer), _layer_spec(g2.shape, layer), _layer_spec(g3.shape, layer),
                  _layer_spec(wfi.shape, layer), _layer_spec(wfo.shape, layer)],
        out_specs=row(d),
        compiler_params=pltpu.CompilerParams(dimension_semantics=("arbitrary", "arbitrary"),
                                             vmem_limit_bytes=VMEM_LIMIT),
        name="outproj_ffn",
    )(x, odn, y, osw, wout, g1, g2, g3, wfi, wfo)


def _reorder_w_in_kernel(w_ref, o_ref):
    w = w_ref[0]
    offs = [0]
    for wd in IN_WIDTHS:
        offs.append(offs[-1] + wd)
    seg = lambda i: w[:, offs[i]:offs[i + 1]]
    dn_qkv, dn_z, dn_b, dn_a, ssm_xbc, ssm_z, ssm_dt, sw_q, sw_k, sw_v = (seg(i) for i in range(len(IN_WIDTHS)))
    n_small = dn_b.shape[1] + dn_a.shape[1] + ssm_dt.shape[1]
    small = jnp.concatenate([dn_b, dn_a, ssm_dt, jnp.zeros((w.shape[0], LANES - n_small), w.dtype)], axis=1)
    sw_q = jnp.concatenate([sw_q[:, h * SWA_HEAD_DIM:(h + 1) * SWA_HEAD_DIM] for h in _swa_head_order()], axis=1)
    o_ref[0] = jnp.concatenate([dn_qkv, dn_z, ssm_z, ssm_xbc, sw_q, sw_k, sw_v, small], axis=1)


def _reorder_w_in(w):
    depth, d, d_in = w.shape
    assert d_in == sum(IN_WIDTHS)
    tr = _pick_tile(d, 256)
    return pl.pallas_call(
        _reorder_w_in_kernel,
        out_shape=jax.ShapeDtypeStruct((depth, d, D_PROJ), BF16),
        grid=(depth, d // tr),
        in_specs=[pl.BlockSpec((1, tr, d_in), lambda l, i: (l, i, 0))],
        out_specs=pl.BlockSpec((1, tr, D_PROJ), lambda l, i: (l, i, 0)),
        compiler_params=pltpu.CompilerParams(dimension_semantics=("arbitrary", "arbitrary")),
        name="reorder_w_in",
    )(w)


def _scalar_param_tiles(rows):
    depth = rows[0][1].shape[0]
    padded = [jnp.pad(v.astype(F32), ((0, 0), (off, LANES - off - v.shape[1]))) for off, v in rows]
    padded += [jnp.zeros((depth, LANES), F32)] * (SUBLANES - len(rows))
    return jnp.stack(padded, axis=1)


def _rope_tables(pos):
    half = SWA_HEAD_DIM // 2
    inv = ROPE_THETA ** (-jnp.arange(half, dtype=F32) / half)
    ang = pos.astype(F32)[:, None] * inv[None, :]
    cos = jnp.cos(ang)
    sin = jnp.sin(ang)
    cos_t = jnp.concatenate([cos, cos] * SWA_KV_HEADS, axis=1)
    sin_t = jnp.concatenate([-sin, sin] * SWA_KV_HEADS, axis=1)
    return cos_t, sin_t


def kernel(x_prompt, x_sample, state_dn, state_dn_conv, state_ssm, state_ssm_conv, cache_swa_k, cache_swa_v,
           meta_tokens, w_in, dn_conv_w, dn_a_log, dn_dt_bias, dn_norm_w, ssm_conv_w, ssm_conv_b, ssm_a_log,
           ssm_dt_bias, ssm_d, ssm_norm_w, swa_sinks, w_out, g_pre_mix, g_post_mix, g_pre_ffn, g_post_ffn,
           w_ffn_in, w_ffn_out):
    bp, seq, d = x_prompt.shape
    bs, ts, _ = x_sample.shape
    depth = w_in.shape[0]
    lp = N_META + seq + FRONT_PAD
    assert lp % BLOCK == 0 and BLOCK % CHUNK == 0 and seq % BLOCK == 0
    nblk = lp // BLOCK
    blk0 = nblk - 1
    pad_range = (seq, seq + FRONT_PAD)

    zpad = jnp.zeros((bp, FRONT_PAD, d), x_prompt.dtype)
    meta = jnp.broadcast_to(meta_tokens.astype(x_prompt.dtype)[None], (bp, N_META, d))
    xp = jnp.concatenate([x_prompt, zpad, meta], axis=1)
    xs = x_sample.reshape(1, bs * ts, d)

    cos_p, sin_p = _rope_tables(jnp.arange(lp, dtype=jnp.int32) - FRONT_PAD)
    cos_s, sin_s = _rope_tables(PAST_LEN + jnp.arange(ts, dtype=jnp.int32))

    gw = (SSM_HEADS // SSM_GROUPS) * SSM_HEADDIM
    nseq_s = _pick_tile(bs, 8) if bs % SUBLANES == 0 else bs
    zero_dn = jnp.zeros((1, bp, DN_HEADS, DN_DK, DN_DV), F32)
    zero_dnc = jnp.zeros((bp, SUBLANES, DN_CONV), F32)
    zero_ssm = jnp.zeros((1, bp, SSM_GROUPS, gw, SSM_STATE), F32)
    zero_ssmc = jnp.zeros((bp, SUBLANES, SSM_CONV), F32)
    state_ssm_g = state_ssm.reshape(depth, bs, SSM_GROUPS, gw, SSM_STATE)
    cache_k = cache_swa_k.reshape(depth, bs, WINDOW, SWA_KV)
    cache_v = cache_swa_v.reshape(depth, bs, WINDOW, SWA_KV)

    w_in_r = _reorder_w_in(w_in.astype(BF16))
    swa0 = DN_V + SSM_INNER
    w_out_b = jnp.concatenate(
        [w_out[:, :swa0]] + [w_out[:, swa0 + h * SWA_HEAD_DIM:swa0 + (h + 1) * SWA_HEAD_DIM] for h in _swa_head_order()],
        axis=1).astype(BF16)
    w_fi_b = w_ffn_in.astype(BF16)
    w_fo_b = w_ffn_out.astype(BF16)
    g1, g2, g3, g4 = (a[:, None, :] for a in (g_pre_mix, g_post_mix, g_pre_ffn, g_post_ffn))

    new_p, new_s = [], []
    dn_s = ssm_s = k_s = v_s = None
    prm_all = _scalar_param_tiles([(SM_A, dn_a_log), (SM_A, dn_dt_bias), (SM_DT, ssm_a_log), (SM_DT, ssm_dt_bias),
                                   (0, swa_sinks)])
    drow_all = jnp.repeat(ssm_d, SSM_HEADDIM, axis=1)
    dn_cbuf_s = jnp.pad(state_dn_conv, ((0, 0), (0, 0), (SUBLANES - (CONV_WIDTH - 1), 0), (0, 0)))
    ssm_cbuf_s = jnp.pad(state_ssm_conv, ((0, 0), (0, 0), (SUBLANES - (CONV_WIDTH - 1), 0), (0, 0)))
    for l in range(depth):
        prm = prm_all[l]
        dn_nw = dn_norm_w[l][None, :]
        ssm_nw = ssm_norm_w[l][None, :]
        drow = drow_all[l][None, :]
        cbias = ssm_conv_b[l][None, :]
        last = l == depth - 1

        proj = _inproj(xp, g1, w_in_r, l, pad_range, TM_DENSE)
        (odn, dn_p, dnc_p), = _run_parts([_dn_mixer(proj, nblk, blk0, BLOCK, CHUNK, FRONT_PAD, bp, zero_dn, 0,
                                                    zero_dnc, dn_conv_w[l], prm, dn_nw, out_dtype=BF16)], "dn_mixer")
        (ys, ssm_p, ssmc_p), = _run_parts([_ssd_mixer(proj, nblk, blk0, BLOCK, CHUNK, FRONT_PAD, bp, zero_ssm, 0,
                                                      zero_ssmc, ssm_conv_w[l], cbias, prm, drow, ssm_nw,
                                                      out_dtype=BF16)], "ssd_mixer")
        (osw, k_p, v_p), = _run_parts([_swa_prompt(proj, nblk, blk0, FRONT_PAD, cos_p, sin_p, prm, out_dtype=BF16)],
                                      "swa_prompt")
        xp = _tail(xp, odn, ys, osw, w_out_b, g2, g3, g4, w_fi_b, w_fo_b, l, seq if last else lp, TM_DENSE)
        new_p.append((dn_p, dnc_p[:, -(CONV_WIDTH - 1):], ssm_p.reshape(bp, SSM_HEADS, SSM_HEADDIM, SSM_STATE),
                      ssmc_p[:, -(CONV_WIDTH - 1):], k_p.reshape(bp, WINDOW, SWA_KV_HEADS, SWA_HEAD_DIM),
                      v_p.reshape(bp, WINDOW, SWA_KV_HEADS, SWA_HEAD_DIM)))

        proj = _inproj(xs, g1, w_in_r, l, None, TM_DENSE).reshape(bs, ts, D_PROJ)
        (odn, dn_s, dnc_s), (ys, ssm_s, ssmc_s), (osw, k_s, v_s) = _run_parts([
            _dn_mixer(proj, 1, 0, ts, ts, 0, nseq_s, state_dn, l, dn_cbuf_s[l], dn_conv_w[l],
                      prm, dn_nw, stack=(depth, dn_s)),
            _ssd_mixer(proj, 1, 0, ts, ts, 0, nseq_s, state_ssm_g, l, ssm_cbuf_s[l],
                       ssm_conv_w[l], cbias, prm, drow, ssm_nw, stack=(depth, ssm_s)),
            _swa_sample(proj, nseq_s, cache_k, cache_v, l, cos_s, sin_s, prm,
                        stack_k=(depth, k_s), stack_v=(depth, v_s))], "mixers_sample")
        flat = lambda a: a.reshape(1, bs * ts, a.shape[-1])
        xs = _tail(xs, flat(odn), flat(ys), flat(osw), w_out_b, g2, g3, g4, w_fi_b, w_fo_b, l, bs * ts, TM_DENSE)
        new_s.append((dnc_s[:, -(CONV_WIDTH - 1):], ssmc_s[:, -(CONV_WIDTH - 1):]))

    outs_p = tuple(jnp.stack([st[i] for st in new_p]) for i in range(6))
    dnc_s, ssmc_s = (jnp.stack([st[i] for st in new_s]) for i in range(2))
    outs_s = (dn_s, dnc_s, ssm_s.reshape(depth, bs, SSM_HEADS, SSM_HEADDIM, SSM_STATE), ssmc_s,
              k_s.reshape(depth, bs, WINDOW, SWA_KV_HEADS, SWA_HEAD_DIM),
              v_s.reshape(depth, bs, WINDOW, SWA_KV_HEADS, SWA_HEAD_DIM))
    return (xp, xs.reshape(bs, ts, d)) + outs_p + outs_s
```

```python
import functools

import jax
import jax.numpy as jnp
from jax import lax
from jax.experimental import pallas as pl
from jax.experimental.pallas import tpu as pltpu

F32 = jnp.float32
BF16 = jnp.bfloat16
NT = (((1,), (1,)), ((), ()))
TN = (((0,), (0,)), ((), ()))

N_META = 16
CONV_WIDTH = 4
CHUNK = 64
BLOCK = 128
WINDOW = 128
FRONT_PAD = BLOCK - N_META
ROPE_THETA = 10000.0
PAST_LEN = 8192
EPS = 1e-6

DN_HEADS, DN_DK, DN_DV = 4, 128, 128
DN_QK = DN_HEADS * DN_DK
DN_V = DN_HEADS * DN_DV
DN_CONV = 2 * DN_QK + DN_V
SSM_HEADS, SSM_HEADDIM, SSM_GROUPS, SSM_STATE = 4, 64, 2, 128
SSM_INNER = SSM_HEADS * SSM_HEADDIM
SSM_BC = SSM_GROUPS * SSM_STATE
SSM_CONV = SSM_INNER + 2 * SSM_BC
SWA_Q_HEADS, SWA_KV_HEADS, SWA_HEAD_DIM = 4, 2, 64
SWA_Q = SWA_Q_HEADS * SWA_HEAD_DIM
SWA_KV = SWA_KV_HEADS * SWA_HEAD_DIM
IN_WIDTHS = (DN_CONV, DN_V, DN_HEADS, DN_HEADS, SSM_CONV, SSM_INNER, SSM_HEADS, SWA_Q, SWA_KV, SWA_KV)

LANES = 128
SUBLANES = 8
COL_QKV = (0, DN_CONV)
COL_DNZ = (COL_QKV[0] + DN_CONV, DN_V)
COL_SSZ = (COL_DNZ[0] + DN_V, SSM_INNER)
COL_XBC = (COL_SSZ[0] + SSM_INNER, SSM_CONV)
COL_SWQ = (COL_XBC[0] + SSM_CONV, SWA_Q)
COL_SWK = (COL_SWQ[0] + SWA_Q, SWA_KV)
COL_SWV = (COL_SWK[0] + SWA_KV, SWA_KV)
COL_SM = (COL_SWV[0] + SWA_KV, LANES)
D_PROJ = COL_SM[0] + LANES
assert all(off % width == 0 for off, width in (COL_QKV, COL_DNZ, COL_SSZ, COL_XBC, COL_SWQ, COL_SWK, COL_SWV, COL_SM))
SM_B, SM_A, SM_DT = 0, DN_HEADS, 2 * DN_HEADS
PRM_DN_ALOG, PRM_DN_DTB, PRM_SSM_ALOG, PRM_SSM_DTB, PRM_SINK = 0, 1, 2, 3, 4
NEG_BIG = -1e30
VMEM_LIMIT = 56 * 1024 * 1024
TM_DENSE = 640
TM_FIRST = 512


def _bdot(a, b, dims=None):
    a = a.astype(BF16)
    b = b.astype(BF16)
    if dims is None:
        return jnp.dot(a, b, preferred_element_type=F32)
    return lax.dot_general(a, b, dims, preferred_element_type=F32)


def _cumsum_rows(lmat, g):
    hi = g.astype(BF16)
    r1 = g - hi.astype(F32)
    mid = r1.astype(BF16)
    lo = (r1 - mid.astype(F32)).astype(BF16)
    dot = lambda part: jnp.dot(lmat, part, preferred_element_type=F32)
    return dot(hi) + dot(mid) + dot(lo)


def _rmsnorm(x, g):
    return x * lax.rsqrt(jnp.mean(x * x, axis=-1, keepdims=True) + EPS) * g


def _l2norm(x):
    return x * lax.rsqrt(jnp.sum(x * x, axis=-1, keepdims=True) + EPS)


def _sigmoid(x):
    return 1.0 / (1.0 + jnp.exp(-x))


def _silu(x):
    return x * _sigmoid(x)


def _softplus(x):
    return jnp.maximum(x, 0.0) + jnp.log1p(jnp.exp(-jnp.abs(x)))


def _pick_tile(n, target):
    best = None
    for t in range(SUBLANES, min(n, target) + 1, SUBLANES):
        if n % t == 0:
            best = t
    assert best is not None, n
    return best


def _project_rows(x, g, w, tm, pad_range):
    h = _rmsnorm(x, g)
    if pad_range:
        r = pl.program_id(1) * tm + lax.broadcasted_iota(jnp.int32, (tm, 1), 0)
        is_pad = None
        for lo, hi in pad_range:
            hit = (r >= lo) & (r < hi)
            is_pad = hit if is_pad is None else is_pad | hit
        h = jnp.where(is_pad, 0.0, h)
    return jnp.dot(h.astype(BF16), w, preferred_element_type=F32)


def _inproj_kernel(x_ref, g_ref, w_ref, o_ref, *, tm, pad_range):
    o_ref[...] = _project_rows(x_ref[...], g_ref[...], w_ref[...], tm, pad_range)


def _inproj_first_kernel(xm_ref, xt_ref, g_ref, w_ref, o_ref, xo_ref, *, tm, n_main, pad_range):
    x = jnp.where(pl.program_id(1) == n_main, xt_ref[...], xm_ref[...])
    xo_ref[...] = x
    o_ref[...] = _project_rows(x, g_ref[...], w_ref[...], tm, pad_range)


def _inproj_first(x_main, x_tail, g, w, layer, pad_range, tm):
    nb, rows, d = x_main.shape
    assert rows % tm == 0 and x_tail.shape == (tm, d)
    n_main = rows // tm
    return pl.pallas_call(
        functools.partial(_inproj_first_kernel, tm=tm, n_main=n_main, pad_range=pad_range),
        out_shape=(jax.ShapeDtypeStruct((nb, rows + tm, D_PROJ), F32), jax.ShapeDtypeStruct((nb, rows + tm, d), F32)),
        grid=(nb, n_main + 1),
        in_specs=[pl.BlockSpec((None, tm, d), lambda b, i: (b, jnp.minimum(i, n_main - 1), 0)),
                  pl.BlockSpec((tm, d), lambda b, i: (0, 0)),
                  _layer_spec(g.shape, layer), _layer_spec(w.shape, layer)],
        out_specs=(pl.BlockSpec((None, tm, D_PROJ), lambda b, i: (b, i, 0)),
                   pl.BlockSpec((None, tm, d), lambda b, i: (b, i, 0))),
        compiler_params=pltpu.CompilerParams(dimension_semantics=("arbitrary", "arbitrary"),
                                             vmem_limit_bytes=VMEM_LIMIT),
        name="inproj_first",
    )(x_main, x_tail, g, w)


def _inproj(x, g, w, layer, pad_range, tm_target):
    nb, rows, d = x.shape
    tm = _pick_tile(rows, tm_target)
    return pl.pallas_call(
        functools.partial(_inproj_kernel, tm=tm, pad_range=pad_range),
        out_shape=jax.ShapeDtypeStruct((nb, rows, D_PROJ), F32),
        grid=(nb, rows // tm),
        in_specs=[pl.BlockSpec((None, tm, d), lambda b, i: (b, i, 0)), _layer_spec(g.shape, layer),
                  _layer_spec(w.shape, layer)],
        out_specs=pl.BlockSpec((None, tm, D_PROJ), lambda b, i: (b, i, 0)),
        compiler_params=pltpu.CompilerParams(dimension_semantics=("arbitrary", "arbitrary"),
                                             vmem_limit_bytes=VMEM_LIMIT),
        name="inproj",
    )(x, g, w)


def _causal_conv(xbuf, raw_ref, cw_ref, cbo_ref, rows):
    x = raw_ref[...]
    prev = xbuf[0:SUBLANES, :]
    cw = cw_ref[...]
    row = lax.broadcasted_iota(jnp.int32, (SUBLANES, 1), 0)
    acc = None
    for i in range(CONV_WIDTH):
        s = CONV_WIDTH - 1 - i
        if s == 0:
            xs = x
        else:
            r = pltpu.roll(x, s, axis=0)
            head = jnp.where(row < s, pltpu.roll(prev, s, axis=0), r[0:SUBLANES])
            xs = head if rows == SUBLANES else jnp.concatenate([head, r[SUBLANES:]], axis=0)
        term = xs * cw[i:i + 1, :]
        acc = term if acc is None else acc + term
    tail = x[rows - SUBLANES:rows]
    cbo_ref[...] = tail
    xbuf[0:SUBLANES, :] = tail
    return acc


def _causal_mask(c):
    return lax.broadcasted_iota(jnp.int32, (c, c), 0) >= lax.broadcasted_iota(jnp.int32, (c, c), 1)


def _segment_decay(gc, gct, lane, ge):
    col = gc[:, lane:lane + 1]
    row = gct[lane:lane + 1, :]
    return jnp.where(ge, jnp.exp(jnp.where(ge, col - row, 0.0)), 0.0)


def _inv_unit_lower_minus_eye(a_list, c, nh):
    w = nh * c
    blk_r = lax.broadcasted_iota(jnp.int32, (w, w), 0) // c
    blk_c = lax.broadcasted_iota(jnp.int32, (w, w), 1) // c
    same = blk_r == blk_c

    def block_diag(p):
        return jnp.where(same, jnp.concatenate([p] * nh, axis=0), 0.0).astype(BF16)

    ys = [-a for a in a_list]
    ps = [_bdot(a, block_diag(a)) for a in a_list]
    yield
    n = 2
    while n < c:
        pbds = [block_diag(p) for p in ps]
        n *= 2
        if n < c:
            sts = [_bdot(jnp.concatenate([y, p], axis=0), pbd) for y, p, pbd in zip(ys, ps, pbds)]
            ys = [y + p + st[:c] for y, p, st in zip(ys, ps, sts)]
            ps = [st[c:] for st in sts]
        else:
            ys = [y + p + _bdot(y, pbd) for y, p, pbd in zip(ys, ps, pbds)]
        yield
    return ys


def _run_parts(parts, name):
    grid = parts[0]["grid"]
    assert all(p["grid"] == grid for p in parts)
    n_in = [len(p["inputs"]) for p in parts]
    n_out = [len(p["out_shape"]) for p in parts]
    n_scr = [len(p["scratch"]) for p in parts]
    aliases = {}
    for k, p in enumerate(parts):
        for i, o in p["aliases"].items():
            aliases[sum(n_in[:k]) + i] = sum(n_out[:k]) + o

    def kernel(*refs):
        ins = refs[:sum(n_in)]
        outs = refs[sum(n_in):sum(n_in) + sum(n_out)]
        scr = refs[sum(n_in) + sum(n_out):]
        gens = [p["body"](ins[sum(n_in[:k]):sum(n_in[:k + 1])], outs[sum(n_out[:k]):sum(n_out[:k + 1])],
                          scr[sum(n_scr[:k]):sum(n_scr[:k + 1])]) for k, p in enumerate(parts)]
        for tag in gens[0]:
            if tag == "chain":
                break
        live = list(gens)
        while live:
            for g in list(live):
                if next(g, StopIteration) is StopIteration:
                    live.remove(g)

    results = pl.pallas_call(
        kernel,
        out_shape=tuple(s for p in parts for s in p["out_shape"]),
        grid=grid,
        in_specs=[s for p in parts for s in p["in_specs"]],
        out_specs=tuple(s for p in parts for s in p["out_specs"]),
        scratch_shapes=[s for p in parts for s in p["scratch"]],
        input_output_aliases=aliases,
        compiler_params=pltpu.CompilerParams(dimension_semantics=("arbitrary",) * len(grid),
                                             vmem_limit_bytes=VMEM_LIMIT),
        name=name,
    )(*[a for p in parts for a in p["inputs"]])
    return [list(results[sum(n_out[:k]):sum(n_out[:k + 1])]) for k in range(len(parts))]


def _state_out(stack, layer, nb, nseq, tail, out_index, n_inputs):
    zeros = (0,) * len(tail)
    if stack is None:
        return ((nb,) + tail, pl.BlockSpec((nseq,) + tail, lambda b, *_: (b,) + zeros), [], [], {}, None, 0)
    depth, prev = stack
    shape = (depth, nb) + tail
    if prev is None:
        spec = pl.BlockSpec((depth, nseq) + tail, lambda b, *_: (0, b) + zeros)
        return (shape, spec, [], [], {}, layer, depth)
    spec = pl.BlockSpec((None, nseq) + tail, lambda b, *_: (layer, b) + zeros)
    return (shape, spec, [prev], [pl.BlockSpec(memory_space=pl.ANY)], {n_inputs: out_index}, None, 0)


def _state_view(ref, slot, n_slots):
    if slot is None:
        return ref
    for other in range(n_slots):
        if other != slot:
            ref[other] = jnp.zeros(ref.shape[1:], ref.dtype)
    return ref.at[slot]


def _dn_prep(items, lmat, ge, gt_all, chunk):
    heads = range(DN_HEADS)
    gcs = [_cumsum_rows(lmat, g_all) for _, _, g_all in items]
    gcts = [gc.T for gc in gcs]
    qs = [[_l2norm(qkv[:, h * DN_DK:(h + 1) * DN_DK]) * DN_DK ** -0.5 for h in heads] for qkv, _, _ in items]
    ks = [[_l2norm(qkv[:, DN_QK + h * DN_DK:DN_QK + (h + 1) * DN_DK]) for h in heads] for qkv, _, _ in items]
    vs = [[qkv[:, 2 * DN_QK + h * DN_DV:2 * DN_QK + (h + 1) * DN_DV] for h in heads] for qkv, _, _ in items]
    betas = [[beta_all[:, SM_B + h:SM_B + h + 1] for h in heads] for _, beta_all, _ in items]
    yield
    kbs = [[k.astype(BF16) for k in kk] for kk in ks]
    kks = [jnp.concatenate([_bdot(kb, kb, NT) for kb in kb4], axis=1) for kb4 in kbs]
    qks = [jnp.concatenate([_bdot(q, kb, NT) for q, kb in zip(q4, kb4)], axis=1) for q4, kb4 in zip(qs, kbs)]
    decs = [jnp.concatenate([_segment_decay(gc, gct, SM_A + h, ge) for h in heads], axis=1)
            for gc, gct in zip(gcs, gcts)]
    beta_ws = [jnp.concatenate([jnp.broadcast_to(b, (chunk, chunk)) for b in b4], axis=1) for b4 in betas]
    a_list = [jnp.where(gt_all, bw * kk * dec, 0.0) for bw, kk, dec in zip(beta_ws, kks, decs)]
    yield "chain"
    n = len(items)
    egs, ekds, egls, attns, rhss, qds, kds = ([None] * n for _ in range(7))

    def side_work(i):
        gc = gcs[i]
        glast = gc[chunk - 1:chunk, :]
        egs[i] = jnp.exp(gc)
        ekds[i] = jnp.exp(glast - gc)
        egls[i] = jnp.exp(glast)
        attns[i] = (qks[i] * decs[i]).astype(BF16)
        lane = lambda a, h: a[:, SM_A + h:SM_A + h + 1]
        rhss[i] = [jnp.concatenate([vs[i][h] * betas[i][h], ks[i][h] * (betas[i][h] * lane(egs[i], h))], axis=1)
                   for h in heads]
        qds[i] = [(qs[i][h] * lane(egs[i], h)).astype(BF16) for h in heads]
        kds[i] = [(ks[i][h] * lane(ekds[i], h)).astype(BF16) for h in heads]

    inverse = _inv_unit_lower_minus_eye(a_list, chunk, DN_HEADS)
    todo = list(range(n))
    while True:
        try:
            next(inverse)
        except StopIteration as done:
            tms = done.value
            break
        if todo:
            side_work(todo.pop(0))
        yield
    for i in todo:
        side_work(i)
    uws = [[rhss[i][h] + _bdot(tms[i][:, h * chunk:(h + 1) * chunk], rhss[i][h]) for h in heads] for i in range(n)]
    yield
    out = []
    for i in range(n):
        per_head = []
        for h in heads:
            uw = uws[i][h]
            wq = jnp.concatenate([uw[:, DN_DV:].astype(BF16), qds[i][h]], axis=0)
            per_head.append((uw[:, :DN_DV], wq, attns[i][:, h * chunk:(h + 1) * chunk], kds[i][h],
                             egls[i][:, SM_A + h:SM_A + h + 1]))
        out.append(per_head)
    return out


def _dn_body(ins, outs, scratch, *, nseq, rows, chunk, front_pad, slot, n_slots):
    qkv_ref, z_ref, sm_ref, s0_ref, cb_ref, cw_ref, prm_ref, nw_ref = ins[:8]
    o_ref, so_full, cbo_ref = outs
    xbuf, = scratch
    so_ref = _state_view(so_full, slot, n_slots)
    blk = pl.program_id(1)
    first = blk == 0

    @pl.when(first)
    def _():
        so_ref[...] = s0_ref[...]
        xbuf[:, 0:SUBLANES, :] = cb_ref[...]

    prm = prm_ref[...]
    nw = nw_ref[...]
    ge = _causal_mask(chunk)
    lmat = jnp.where(ge, 1.0, 0.0).astype(BF16)
    wide = (chunk, DN_HEADS * chunk)
    gt_all = lax.broadcasted_iota(jnp.int32, wide, 0) > lax.broadcasted_iota(jnp.int32, wide, 1) % chunk
    nchunk = rows // chunk

    items = []
    for s in range(nseq):
        qkv = _silu(_causal_conv(xbuf.at[s], qkv_ref.at[s], cw_ref, cbo_ref.at[s], rows))
        sm = sm_ref[s]
        beta_all = _sigmoid(sm)
        g_all = -jnp.exp(prm[PRM_DN_ALOG:PRM_DN_ALOG + 1, :]) * _softplus(sm + prm[PRM_DN_DTB:PRM_DN_DTB + 1, :])
        if front_pad:
            pos = blk * rows + lax.broadcasted_iota(jnp.int32, (rows, 1), 0)
            g_all = jnp.where(pos < front_pad, 0.0, g_all)
        for c in range(nchunk):
            cs = slice(c * chunk, (c + 1) * chunk)
            items.append((qkv[cs], beta_all[cs], g_all[cs]))
        yield
    prep = yield from _dn_prep(items, lmat, ge, gt_all, chunk)

    chains = [(s, h) for s in range(nseq) for h in range(DN_HEADS)]
    states = [so_ref[s, h] for s, h in chains]
    for c in range(nchunk):
        r0 = c * chunk
        fac = [prep[s * nchunk + c][h] for s, h in chains]
        m1s = [_bdot(f[1], st) for f, st in zip(fac, states)]
        yield
        v_news = [(f[0] - m1[:chunk]).astype(BF16) for f, m1 in zip(fac, m1s)]
        ups = [_bdot(f[3], v, TN) for f, v in zip(fac, v_news)]
        os_ = [m1[chunk:] + _bdot(f[2], v) for f, m1, v in zip(fac, m1s, v_news)]
        yield
        states = [st * f[4] + up for f, st, up in zip(fac, states, ups)]
        for (s, h), o in zip(chains, os_):
            lo = h * DN_DV
            zh = z_ref[s, r0:r0 + chunk, lo:lo + DN_DV]
            o_ref[s, r0:r0 + chunk, lo:lo + DN_DV] = (_rmsnorm(o, nw) * _silu(zh)).astype(o_ref.dtype)
        yield
    for (s, h), st in zip(chains, states):
        so_ref[s, h] = st


def _dn_mixer(proj, nblk, blk0, rows, chunk, front_pad, nseq, s0, layer, cbuf, cw, prm, nw, stack=None,
              out_dtype=F32):
    nb = proj.shape[0]
    col = lambda c: c[0] // c[1]
    phys = lambda i: (i + blk0) % nblk
    so_shape, so_spec, extra, extra_specs, aliases, slot, n_slots = _state_out(
        stack, layer, nb, nseq, (DN_HEADS, DN_DK, DN_DV), 1, 8)
    return dict(
        body=functools.partial(_dn_body, nseq=nseq, rows=rows, chunk=chunk, front_pad=front_pad,
                               slot=slot, n_slots=n_slots),
        grid=(nb // nseq, nblk),
        inputs=[proj, proj, proj, s0, cbuf, cw, prm, nw] + extra,
        out_shape=[jax.ShapeDtypeStruct((nb, nblk * rows, DN_V), out_dtype),
                   jax.ShapeDtypeStruct(so_shape, F32),
                   jax.ShapeDtypeStruct((nb, SUBLANES, DN_CONV), F32)],
        in_specs=[
            pl.BlockSpec((nseq, rows, DN_CONV), lambda b, i: (b, phys(i), col(COL_QKV))),
            pl.BlockSpec((nseq, rows, DN_V), lambda b, i: (b, phys(i), col(COL_DNZ))),
            pl.BlockSpec((nseq, rows, LANES), lambda b, i: (b, phys(i), col(COL_SM))),
            pl.BlockSpec((None, nseq, DN_HEADS, DN_DK, DN_DV), lambda b, i: (layer, b, 0, 0, 0)),
            pl.BlockSpec((nseq, SUBLANES, DN_CONV), lambda b, i: (b, 0, 0)),
            pl.BlockSpec((CONV_WIDTH, DN_CONV), lambda b, i: (0, 0)),
            pl.BlockSpec((SUBLANES, LANES), lambda b, i: (0, 0)),
            pl.BlockSpec((1, DN_DV), lambda b, i: (0, 0)),
        ] + extra_specs,
        out_specs=[
            pl.BlockSpec((nseq, rows, DN_V), lambda b, i: (b, phys(i), 0)),
            so_spec,
            pl.BlockSpec((nseq, SUBLANES, DN_CONV), lambda b, i: (b, 0, 0)),
        ],
        scratch=[pltpu.VMEM((nseq, SUBLANES, DN_CONV), F32)],
        aliases=aliases,
    )


def _ssd_body(ins, outs, scratch, *, nseq, rows, chunk, front_pad, slot, n_slots):
    xbc_ref, z_ref, sm_ref, h0_ref, cb_ref, cw_ref, cbias_ref, prm_ref, drow_ref, nw_ref = ins[:10]
    y_ref, ho_full, cbo_ref = outs
    xbuf, = scratch
    ho_ref = _state_view(ho_full, slot, n_slots)
    blk = pl.program_id(1)

    @pl.when(blk == 0)
    def _():
        ho_ref[...] = h0_ref[...]
        xbuf[:, 0:SUBLANES, :] = cb_ref[...]

    prm = prm_ref[...]
    nw = nw_ref[...]
    drow = drow_ref[...]
    cbias = cbias_ref[...]
    ge = _causal_mask(chunk)
    lmat = jnp.where(ge, 1.0, 0.0).astype(BF16)
    hpg = SSM_HEADS // SSM_GROUPS
    gw = hpg * SSM_HEADDIM
    lane = lax.broadcasted_iota(jnp.int32, (1, gw), 1)
    srow = lax.broadcasted_iota(jnp.int32, (gw, 1), 0)
    in_head = [(lane >= j * SSM_HEADDIM) & (lane < (j + 1) * SSM_HEADDIM) for j in range(hpg)]
    nchunk = rows // chunk
    groups = range(SSM_GROUPS)

    items = []
    for s in range(nseq):
        act = _silu(_causal_conv(xbuf.at[s], xbc_ref.at[s], cw_ref, cbo_ref.at[s], rows) + cbias)
        dt_all = _softplus(sm_ref[s] + prm[PRM_SSM_DTB:PRM_SSM_DTB + 1, :])
        if front_pad:
            pos = blk * rows + lax.broadcasted_iota(jnp.int32, (rows, 1), 0)
            dt_all = jnp.where(pos < front_pad, 0.0, dt_all)
        g_all = dt_all * (-jnp.exp(prm[PRM_SSM_ALOG:PRM_SSM_ALOG + 1, :]))
        for c in range(nchunk):
            cs = slice(c * chunk, (c + 1) * chunk)
            items.append((act[cs], dt_all[cs], g_all[cs]))
        yield
    n_items = len(items)
    gcs = [_cumsum_rows(lmat, g) for _, _, g in items]
    gcts = [gc.T for gc in gcs]
    xgs = [[a[:, g * gw:(g + 1) * gw] for g in groups] for a, _, _ in items]
    bgs = [[a[:, SSM_INNER + g * SSM_STATE:SSM_INNER + (g + 1) * SSM_STATE] for g in groups] for a, _, _ in items]
    cgs = [[a[:, SSM_INNER + SSM_BC + g * SSM_STATE:SSM_INNER + SSM_BC + (g + 1) * SSM_STATE] for g in groups]
           for a, _, _ in items]
    cbs = [[_bdot(cgs[i][g], bgs[i][g], NT) for g in groups] for i in range(n_items)]
    yield
    egs = [jnp.exp(gc) for gc in gcs]
    ekds = [jnp.exp(gc[chunk - 1:chunk, :] - gc) for gc in gcs]
    egls = [jnp.exp(gc[chunk - 1:chunk, :]) for gc in gcs]
    heads = [(g, j) for g in groups for j in range(hpg)]
    ln = lambda g, j: SM_DT + g * hpg + j
    xdts = [[jnp.where(in_head[j], xgs[i][g] * items[i][1][:, ln(g, j):ln(g, j) + 1], 0.0).astype(BF16)
             for g, j in heads] for i in range(n_items)]
    attns = [[cbs[i][g] * _segment_decay(gcs[i], gcts[i], ln(g, j), ge) for g, j in heads] for i in range(n_items)]
    y_intras = [[_bdot(attns[i][k], xdts[i][k]) for k in range(len(heads))] for i in range(n_items)]
    yield
    upds = [[_bdot(xdts[i][k], bgs[i][g] * ekds[i][:, ln(g, j):ln(g, j) + 1], TN) for k, (g, j) in enumerate(heads)]
            for i in range(n_items)]
    yield
    cds = [[jnp.concatenate([cgs[i][g] * egs[i][:, ln(g, j):ln(g, j) + 1] for j in range(hpg)], axis=0).astype(BF16)
            for g in groups] for i in range(n_items)]
    y_loc = [[sum(y_intras[i][g * hpg + j] for j in range(hpg)) for g in groups] for i in range(n_items)]
    h_inc = [[sum(upds[i][g * hpg + j] for j in range(hpg)) for g in groups] for i in range(n_items)]
    gl_cols = []
    for i in range(n_items):
        per_group = []
        for g in groups:
            gl = egls[i][:, ln(g, 0):ln(g, 0) + 1]
            for j in range(1, hpg):
                gl = jnp.where(srow < j * SSM_HEADDIM, gl, egls[i][:, ln(g, j):ln(g, j) + 1])
            per_group.append(gl)
        gl_cols.append(per_group)

    chains = [(s, g) for s in range(nseq) for g in groups]
    states = [ho_ref[s, g] for s, g in chains]
    for c in range(nchunk):
        r0 = c * chunk
        idx = [s * nchunk + c for s, _ in chains]
        yis = [_bdot(cds[i][g], st, NT) for i, (_, g), st in zip(idx, chains, states)]
        yield
        states = [st * gl_cols[i][g] + h_inc[i][g] for i, (_, g), st in zip(idx, chains, states)]
        for i, (s, g), yi in zip(idx, chains, yis):
            y_inter = yi[0:chunk]
            for j in range(1, hpg):
                y_inter = jnp.where(in_head[j], yi[j * chunk:(j + 1) * chunk], y_inter)
            yg = y_loc[i][g] + y_inter + xgs[i][g] * drow[:, g * gw:(g + 1) * gw]
            yg = yg * _silu(z_ref[s, r0:r0 + chunk, g * gw:(g + 1) * gw])
            y_ref[s, r0:r0 + chunk, g * gw:(g + 1) * gw] = _rmsnorm(yg, nw[:, g * gw:(g + 1) * gw]).astype(y_ref.dtype)
        yield
    for (s, g), st in zip(chains, states):
        ho_ref[s, g] = st


def _ssd_mixer(proj, nblk, blk0, rows, chunk, front_pad, nseq, h0, layer, cbuf, cw, cbias, prm, drow, nw,
               stack=None, out_dtype=F32):
    nb = proj.shape[0]
    col = lambda c: c[0] // c[1]
    phys = lambda i: (i + blk0) % nblk
    gw = (SSM_HEADS // SSM_GROUPS) * SSM_HEADDIM
    ho_shape, ho_spec, extra, extra_specs, aliases, slot, n_slots = _state_out(
        stack, layer, nb, nseq, (SSM_GROUPS, gw, SSM_STATE), 1, 10)
    return dict(
        body=functools.partial(_ssd_body, nseq=nseq, rows=rows, chunk=chunk, front_pad=front_pad,
                               slot=slot, n_slots=n_slots),
        grid=(nb // nseq, nblk),
        inputs=[proj, proj, proj, h0, cbuf, cw, cbias, prm, drow, nw] + extra,
        out_shape=[jax.ShapeDtypeStruct((nb, nblk * rows, SSM_INNER), out_dtype),
                   jax.ShapeDtypeStruct(ho_shape, F32),
                   jax.ShapeDtypeStruct((nb, SUBLANES, SSM_CONV), F32)],
        in_specs=[
            pl.BlockSpec((nseq, rows, SSM_CONV), lambda b, i: (b, phys(i), col(COL_XBC))),
            pl.BlockSpec((nseq, rows, SSM_INNER), lambda b, i: (b, phys(i), col(COL_SSZ))),
            pl.BlockSpec((nseq, rows, LANES), lambda b, i: (b, phys(i), col(COL_SM))),
            pl.BlockSpec((None, nseq, SSM_GROUPS, gw, SSM_STATE), lambda b, i: (layer, b, 0, 0, 0)),
            pl.BlockSpec((nseq, SUBLANES, SSM_CONV), lambda b, i: (b, 0, 0)),
            pl.BlockSpec((CONV_WIDTH, SSM_CONV), lambda b, i: (0, 0)),
            pl.BlockSpec((1, SSM_CONV), lambda b, i: (0, 0)),
            pl.BlockSpec((SUBLANES, LANES), lambda b, i: (0, 0)),
            pl.BlockSpec((1, SSM_INNER), lambda b, i: (0, 0)),
            pl.BlockSpec((1, SSM_INNER), lambda b, i: (0, 0)),
        ] + extra_specs,
        out_specs=[
            pl.BlockSpec((nseq, rows, SSM_INNER), lambda b, i: (b, phys(i), 0)),
            ho_spec,
            pl.BlockSpec((nseq, SUBLANES, SSM_CONV), lambda b, i: (b, 0, 0)),
        ],
        scratch=[pltpu.VMEM((nseq, SUBLANES, SSM_CONV), F32)],
        aliases=aliases,
    )


def _rope(x, cos, sin_signed):
    w = x.shape[-1]
    half = SWA_HEAD_DIM // 2
    lane = lax.broadcasted_iota(jnp.int32, (1, w), 1)
    first_half = (lane % SWA_HEAD_DIM) < half
    swapped = jnp.where(first_half, pltpu.roll(x, w - half, axis=1), pltpu.roll(x, half, axis=1))
    return x * cos + swapped * sin_signed


def _sink_attend(problems):
    scale = SWA_HEAD_DIM ** -0.5
    scores = [[jnp.where(m, _bdot(q, k, NT) * scale, NEG_BIG) for k, m in zip(keys, masks)]
              for q, keys, _, masks, _ in problems]
    yield
    outs = []
    probs, dens = [], []
    def lane_reduce(tiles, combine, reduce):
        merged = {}
        for t in tiles:
            merged[t.shape[-1]] = t if t.shape[-1] not in merged else combine(merged[t.shape[-1]], t)
        return [reduce(t, axis=-1, keepdims=True) for t in merged.values()]

    for (_, _, _, _, sink), ss in zip(problems, scores):
        mx = sink
        for m in lane_reduce(ss, jnp.maximum, jnp.max):
            mx = jnp.maximum(mx, m)
        probs.append([jnp.exp(s - mx) for s in ss])
        dens.append(jnp.exp(sink - mx))
    yield
    pvs = [[_bdot(p, v) for p, v in zip(ps, vals)] for (_, _, vals, _, _), ps in zip(problems, probs)]
    yield
    for pv, sink_term in zip(pvs, dens):
        acc = pv[0]
        for extra in pv[1:]:
            acc = acc + extra
        outs.append(acc / (pltpu.roll(acc, SWA_HEAD_DIM, axis=1) + sink_term))
    return outs


def _swa_head_order():
    grp = SWA_Q_HEADS // SWA_KV_HEADS
    assert SWA_KV_HEADS * SWA_HEAD_DIM == LANES
    return [j * grp + t for t in range(grp) for j in range(SWA_KV_HEADS)]


def _swa_problems(q, key_sets, val_sets, masks, prm, tq):
    grp = SWA_Q_HEADS // SWA_KV_HEADS
    row = lax.broadcasted_iota(jnp.int32, (grp * tq, 1), 0)
    lane = lax.broadcasted_iota(jnp.int32, (1, LANES), 1)
    qst = jnp.concatenate([q[:, t * LANES:(t + 1) * LANES] for t in range(grp)], axis=0).astype(BF16)
    problems = []
    for j in range(SWA_KV_HEADS):
        half = (lane >= j * SWA_HEAD_DIM) & (lane < (j + 1) * SWA_HEAD_DIM)
        sink = prm[PRM_SINK:PRM_SINK + 1, j * grp:j * grp + 1]
        for t in range(1, grp):
            sink = jnp.where(row < t * tq, sink, prm[PRM_SINK:PRM_SINK + 1, j * grp + t:j * grp + t + 1])
        problems.append((qst, [jnp.where(half, k, 0.0) for k in key_sets],
                         [jnp.where(half, v, 1.0) for v in val_sets], masks, sink))
    return problems


def _swa_tiles(outs, tq):
    grp = SWA_Q_HEADS // SWA_KV_HEADS
    lane = lax.broadcasted_iota(jnp.int32, (1, LANES), 1)
    tiles = []
    for t in range(grp):
        tile = outs[0][t * tq:(t + 1) * tq]
        for j in range(1, SWA_KV_HEADS):
            tile = jnp.where(lane < j * SWA_HEAD_DIM, tile, outs[j][t * tq:(t + 1) * tq])
        tiles.append(tile)
    return tiles


def _swa_prompt_body(ins, outs, scratch, *, nseq, front_pad):
    q_ref, k_ref, v_ref, cos_ref, sin_ref, prm_ref = ins
    o_ref, ko_ref, vo_ref = outs
    kprev, vprev = scratch
    blk = pl.program_id(1)

    @pl.when(blk == 0)
    def _():
        kprev[...] = jnp.zeros_like(kprev)
        vprev[...] = jnp.zeros_like(vprev)

    cos = cos_ref[...]
    sin = sin_ref[...]
    cos_q = jnp.concatenate([cos, cos], axis=1)
    sin_q = jnp.concatenate([sin, sin], axis=1)
    grp = SWA_Q_HEADS // SWA_KV_HEADS
    qi = lax.broadcasted_iota(jnp.int32, (grp * BLOCK, BLOCK), 0) % BLOCK
    kj = lax.broadcasted_iota(jnp.int32, (grp * BLOCK, BLOCK), 1)
    mask_cur = (kj <= qi) & (blk * BLOCK + kj >= front_pad)
    mask_prev = (kj > qi) & ((blk - 1) * BLOCK + kj >= front_pad)
    prm = prm_ref[...]
    problems = []
    for s in range(nseq):
        q = _rope(q_ref[s], cos_q, sin_q)
        k = _rope(k_ref[s], cos, sin)
        v = v_ref[s]
        problems += _swa_problems(q, (kprev[s], k), (vprev[s], v), (mask_prev, mask_cur), prm, BLOCK)
        kprev[s] = k
        vprev[s] = v
        ko_ref[s] = k
        vo_ref[s] = v
        yield
    res = yield from _sink_attend(problems)
    for s in range(nseq):
        for t, tile in enumerate(_swa_tiles(res[s * SWA_KV_HEADS:(s + 1) * SWA_KV_HEADS], BLOCK)):
            o_ref[s, :, t * LANES:(t + 1) * LANES] = tile.astype(o_ref.dtype)


def _swa_prompt(proj, nblk, blk0, front_pad, cos, sin, prm, out_dtype=F32):
    nb = proj.shape[0]
    col = lambda c: c[0] // c[1]
    phys = lambda i: (i + blk0) % nblk
    return dict(
        body=functools.partial(_swa_prompt_body, nseq=nb, front_pad=front_pad),
        grid=(1, nblk),
        inputs=[proj, proj, proj, cos, sin, prm],
        out_shape=[jax.ShapeDtypeStruct((nb, nblk * BLOCK, SWA_Q), out_dtype),
                   jax.ShapeDtypeStruct((nb, WINDOW, SWA_KV), F32),
                   jax.ShapeDtypeStruct((nb, WINDOW, SWA_KV), F32)],
        in_specs=[
            pl.BlockSpec((nb, BLOCK, SWA_Q), lambda b, i: (0, phys(i), col(COL_SWQ))),
            pl.BlockSpec((nb, BLOCK, SWA_KV), lambda b, i: (0, phys(i), col(COL_SWK))),
            pl.BlockSpec((nb, BLOCK, SWA_KV), lambda b, i: (0, phys(i), col(COL_SWV))),
            pl.BlockSpec((BLOCK, SWA_KV), lambda b, i: (i, 0)),
            pl.BlockSpec((BLOCK, SWA_KV), lambda b, i: (i, 0)),
            pl.BlockSpec((SUBLANES, LANES), lambda b, i: (0, 0)),
        ],
        out_specs=[
            pl.BlockSpec((nb, BLOCK, SWA_Q), lambda b, i: (0, phys(i), 0)),
            pl.BlockSpec((nb, WINDOW, SWA_KV), lambda b, i: (0, 0, 0)),
            pl.BlockSpec((nb, WINDOW, SWA_KV), lambda b, i: (0, 0, 0)),
        ],
        scratch=[pltpu.VMEM((nb, BLOCK, SWA_KV), F32), pltpu.VMEM((nb, BLOCK, SWA_KV), F32)],
        aliases={},
    )


def _swa_sample_body(ins, outs, scratch, *, nseq, steps, slot, n_slots):
    q_ref, k_ref, v_ref, kc_ref, vc_ref, cos_ref, sin_ref, prm_ref = ins[:8]
    o_ref, ko_full, vo_full = outs
    ko_ref = _state_view(ko_full, slot, n_slots)
    vo_ref = _state_view(vo_full, slot, n_slots)
    cos = cos_ref[...]
    sin = sin_ref[...]
    cos_q = jnp.concatenate([cos, cos], axis=1)
    sin_q = jnp.concatenate([sin, sin], axis=1)
    prm = prm_ref[...]
    grp = SWA_Q_HEADS // SWA_KV_HEADS
    ti = lax.broadcasted_iota(jnp.int32, (grp * steps, WINDOW), 0) % steps
    sj = lax.broadcasted_iota(jnp.int32, (grp * steps, WINDOW), 1)
    mask_cache = sj > ti
    tn = lax.broadcasted_iota(jnp.int32, (grp * steps, steps), 0) % steps
    sn = lax.broadcasted_iota(jnp.int32, (grp * steps, steps), 1)
    mask_new = sn <= tn
    problems = []
    for b in range(nseq):
        q = _rope(q_ref[b], cos_q, sin_q)
        k = _rope(k_ref[b], cos, sin)
        v = v_ref[b]
        kc = kc_ref[b]
        vc = vc_ref[b]
        ko_ref[b, 0:WINDOW - steps, :] = kc[steps:WINDOW, :]
        ko_ref[b, WINDOW - steps:WINDOW, :] = k
        vo_ref[b, 0:WINDOW - steps, :] = vc[steps:WINDOW, :]
        vo_ref[b, WINDOW - steps:WINDOW, :] = v
        problems += _swa_problems(q, (kc, k), (vc, v), (mask_cache, mask_new), prm, steps)
        yield
    res = yield from _sink_attend(problems)
    for b in range(nseq):
        for t, tile in enumerate(_swa_tiles(res[b * SWA_KV_HEADS:(b + 1) * SWA_KV_HEADS], steps)):
            o_ref[b, :, t * LANES:(t + 1) * LANES] = tile


def _swa_sample(proj, nseq, kc, vc, layer, cos, sin, prm, stack_k=None, stack_v=None):
    nb, steps, _ = proj.shape
    assert WINDOW > steps
    col = lambda c: c[0] // c[1]
    ko_shape, ko_spec, extra_k, specs_k, alias_k, slot, n_slots = _state_out(
        stack_k, layer, nb, nseq, (WINDOW, SWA_KV), 1, 8)
    vo_shape, vo_spec, extra_v, specs_v, alias_v, _, _ = _state_out(
        stack_v, layer, nb, nseq, (WINDOW, SWA_KV), 2, 8 + len(extra_k))
    return dict(
        body=functools.partial(_swa_sample_body, nseq=nseq, steps=steps, slot=slot, n_slots=n_slots),
        grid=(nb // nseq, 1),
        inputs=[proj, proj, proj, kc, vc, cos, sin, prm] + extra_k + extra_v,
        out_shape=[jax.ShapeDtypeStruct((nb, steps, SWA_Q), F32),
                   jax.ShapeDtypeStruct(ko_shape, F32),
                   jax.ShapeDtypeStruct(vo_shape, F32)],
        in_specs=[
            pl.BlockSpec((nseq, steps, SWA_Q), lambda b, i: (b, 0, col(COL_SWQ))),
            pl.BlockSpec((nseq, steps, SWA_KV), lambda b, i: (b, 0, col(COL_SWK))),
            pl.BlockSpec((nseq, steps, SWA_KV), lambda b, i: (b, 0, col(COL_SWV))),
            pl.BlockSpec((None, nseq, WINDOW, SWA_KV), lambda b, i: (layer, b, 0, 0)),
            pl.BlockSpec((None, nseq, WINDOW, SWA_KV), lambda b, i: (layer, b, 0, 0)),
            pl.BlockSpec((steps, SWA_KV), lambda b, i: (0, 0)),
            pl.BlockSpec((steps, SWA_KV), lambda b, i: (0, 0)),
            pl.BlockSpec((SUBLANES, LANES), lambda b, i: (0, 0)),
        ] + specs_k + specs_v,
        out_specs=[pl.BlockSpec((nseq, steps, SWA_Q), lambda b, i: (b, 0, 0)), ko_spec, vo_spec],
        scratch=[],
        aliases={**alias_k, **alias_v},
    )


def _tail_kernel(x_ref, odn_ref, y_ref, osw_ref, wout_ref, g1_ref, g2_ref, g3_ref, wfi_ref, wfo_ref, o_ref,
                 *, d_ff, tf):
    mixed = jnp.concatenate([odn_ref[...], y_ref[...], osw_ref[...]], axis=1).astype(BF16)
    m = jnp.dot(mixed, wout_ref[...], preferred_element_type=F32)
    x1 = x_ref[...] + _rmsnorm(m, g1_ref[...])
    h = _rmsnorm(x1, g2_ref[...]).astype(BF16)
    y2 = None
    for c in range(d_ff // tf):
        gate = jnp.dot(h, wfi_ref[:, c * tf:(c + 1) * tf], preferred_element_type=F32)
        up = jnp.dot(h, wfi_ref[:, d_ff + c * tf:d_ff + (c + 1) * tf], preferred_element_type=F32)
        part = jnp.dot((_silu(gate) * up).astype(BF16), wfo_ref[c * tf:(c + 1) * tf, :], preferred_element_type=F32)
        y2 = part if y2 is None else y2 + part
    o_ref[...] = x1 + _rmsnorm(y2, g3_ref[...])


def _layer_spec(shape, layer):
    nd = len(shape) - 1
    return pl.BlockSpec((None,) + tuple(shape[1:]), lambda *_: (layer,) + (0,) * nd, pipeline_mode=pl.Buffered(1))


def _tail(x, odn, y, osw, wout, g1, g2, g3, wfi, wfo, layer, l_out, tm_target):
    nb, _, d = x.shape
    d_ff = wfo.shape[1]
    tm = _pick_tile(l_out, tm_target)
    tf = 2 * LANES if d_ff % (2 * LANES) == 0 else d_ff
    row = lambda w: pl.BlockSpec((None, tm, w), lambda b, i: (b, i, 0))
    return pl.pallas_call(
        functools.partial(_tail_kernel, d_ff=d_ff, tf=tf),
        out_shape=jax.ShapeDtypeStruct((nb, l_out, d), F32),
        grid=(nb, l_out // tm),
        in_specs=[row(d), row(DN_V), row(SSM_INNER), row(SWA_Q), _layer_spec(wout.shape, layer),
                  _layer_spec(g1.shape, layer), _layer_spec(g2.shape, layer), _layer_spec(g3.shape, layer),
                  _layer_spec(wfi.shape, layer), _layer_spec(wfo.shape, layer)],
        out_specs=row(d),
        compiler_params=pltpu.CompilerParams(dimension_semantics=("arbitrary", "arbitrary"),
                                             vmem_limit_bytes=VMEM_LIMIT),
        name="outproj_ffn",
    )(x, odn, y, osw, wout, g1, g2, g3, wfi, wfo)


def _reorder_w_in_kernel(w_ref, o_ref):
    w = w_ref[0]
    offs = [0]
    for wd in IN_WIDTHS:
        offs.append(offs[-1] + wd)
    seg = lambda i: w[:, offs[i]:offs[i + 1]]
    dn_qkv, dn_z, dn_b, dn_a, ssm_xbc, ssm_z, ssm_dt, sw_q, sw_k, sw_v = (seg(i) for i in range(len(IN_WIDTHS)))
    n_small = dn_b.shape[1] + dn_a.shape[1] + ssm_dt.shape[1]
    small = jnp.concatenate([dn_b, dn_a, ssm_dt, jnp.zeros((w.shape[0], LANES - n_small), w.dtype)], axis=1)
    sw_q = jnp.concatenate([sw_q[:, h * SWA_HEAD_DIM:(h + 1) * SWA_HEAD_DIM] for h in _swa_head_order()], axis=1)
    o_ref[0] = jnp.concatenate([dn_qkv, dn_z, ssm_z, ssm_xbc, sw_q, sw_k, sw_v, small], axis=1).astype(BF16)


def _reorder_w_in(w):
    depth, d, d_in = w.shape
    assert d_in == sum(IN_WIDTHS)
    tr = _pick_tile(d, 256)
    return pl.pallas_call(
        _reorder_w_in_kernel,
        out_shape=jax.ShapeDtypeStruct((depth, d, D_PROJ), BF16),
        grid=(depth, d // tr),
        in_specs=[pl.BlockSpec((1, tr, d_in), lambda l, i: (l, i, 0))],
        out_specs=pl.BlockSpec((1, tr, D_PROJ), lambda l, i: (l, i, 0)),
        compiler_params=pltpu.CompilerParams(dimension_semantics=("arbitrary", "arbitrary")),
        name="reorder_w_in",
    )(w)


def _scalar_param_tiles(rows):
    depth = rows[0][1].shape[0]
    padded = [jnp.pad(v.astype(F32), ((0, 0), (off, LANES - off - v.shape[1]))) for off, v in rows]
    padded += [jnp.zeros((depth, LANES), F32)] * (SUBLANES - len(rows))
    return jnp.stack(padded, axis=1)


def _rope_tables(pos):
    half = SWA_HEAD_DIM // 2
    inv = ROPE_THETA ** (-jnp.arange(half, dtype=F32) / half)
    ang = pos.astype(F32)[:, None] * inv[None, :]
    cos = jnp.cos(ang)
    sin = jnp.sin(ang)
    cos_t = jnp.concatenate([cos, cos] * SWA_KV_HEADS, axis=1)
    sin_t = jnp.concatenate([-sin, sin] * SWA_KV_HEADS, axis=1)
    return cos_t, sin_t


def kernel(x_prompt, x_sample, state_dn, state_dn_conv, state_ssm, state_ssm_conv, cache_swa_k, cache_swa_v,
           meta_tokens, w_in, dn_conv_w, dn_a_log, dn_dt_bias, dn_norm_w, ssm_conv_w, ssm_conv_b, ssm_a_log,
           ssm_dt_bias, ssm_d, ssm_norm_w, swa_sinks, w_out, g_pre_mix, g_post_mix, g_pre_ffn, g_post_ffn,
           w_ffn_in, w_ffn_out):
    bp, seq, d = x_prompt.shape
    bs, ts, _ = x_sample.shape
    depth = w_in.shape[0]
    lp = N_META + seq + FRONT_PAD
    assert lp % BLOCK == 0 and BLOCK % CHUNK == 0 and seq % BLOCK == 0
    nblk = lp // BLOCK
    blk0 = nblk - 1
    tm0 = _pick_tile(seq, TM_FIRST)
    assert tm0 >= BLOCK
    x_front = jnp.concatenate([jnp.zeros((FRONT_PAD, d), x_prompt.dtype), meta_tokens.astype(x_prompt.dtype),
                               jnp.zeros((tm0 - BLOCK, d), x_prompt.dtype)], axis=0)
    pad_first = ((seq, seq + FRONT_PAD), (lp, seq + tm0))
    pad_range = ((seq, seq + FRONT_PAD),)
    xp = None
    xs = x_sample.reshape(1, bs * ts, d)

    cos_p, sin_p = _rope_tables(jnp.arange(lp, dtype=jnp.int32) - FRONT_PAD)
    cos_s, sin_s = _rope_tables(PAST_LEN + jnp.arange(ts, dtype=jnp.int32))

    gw = (SSM_HEADS // SSM_GROUPS) * SSM_HEADDIM
    nseq_s = _pick_tile(bs, 8) if bs % SUBLANES == 0 else bs
    zero_dn = jnp.zeros((1, bp, DN_HEADS, DN_DK, DN_DV), F32)
    zero_dnc = jnp.zeros((bp, SUBLANES, DN_CONV), F32)
    zero_ssm = jnp.zeros((1, bp, SSM_GROUPS, gw, SSM_STATE), F32)
    zero_ssmc = jnp.zeros((bp, SUBLANES, SSM_CONV), F32)
    state_ssm_g = state_ssm.reshape(depth, bs, SSM_GROUPS, gw, SSM_STATE)
    cache_k = cache_swa_k.reshape(depth, bs, WINDOW, SWA_KV)
    cache_v = cache_swa_v.reshape(depth, bs, WINDOW, SWA_KV)

    w_in_r = _reorder_w_in(w_in)
    swa0 = DN_V + SSM_INNER
    w_out_b = jnp.concatenate(
        [w_out[:, :swa0]] + [w_out[:, swa0 + h * SWA_HEAD_DIM:swa0 + (h + 1) * SWA_HEAD_DIM] for h in _swa_head_order()],
        axis=1).astype(BF16)
    w_fi_b = w_ffn_in.astype(BF16)
    w_fo_b = w_ffn_out.astype(BF16)
    g1, g2, g3, g4 = (a[:, None, :] for a in (g_pre_mix, g_post_mix, g_pre_ffn, g_post_ffn))

    new_p, new_s = [], []
    dn_s = ssm_s = k_s = v_s = None
    prm_all = _scalar_param_tiles([(SM_A, dn_a_log), (SM_A, dn_dt_bias), (SM_DT, ssm_a_log), (SM_DT, ssm_dt_bias),
                                   (0, swa_sinks)])
    drow_all = jnp.repeat(ssm_d, SSM_HEADDIM, axis=1)
    dn_cbuf_s = jnp.pad(state_dn_conv, ((0, 0), (0, 0), (SUBLANES - (CONV_WIDTH - 1), 0), (0, 0)))
    ssm_cbuf_s = jnp.pad(state_ssm_conv, ((0, 0), (0, 0), (SUBLANES - (CONV_WIDTH - 1), 0), (0, 0)))
    for l in range(depth):
        prm = prm_all[l]
        dn_nw = dn_norm_w[l][None, :]
        ssm_nw = ssm_norm_w[l][None, :]
        drow = drow_all[l][None, :]
        cbias = ssm_conv_b[l][None, :]
        last = l == depth - 1

        if l == 0:
            proj, xp = _inproj_first(x_prompt, x_front, g1, w_in_r, l, pad_first, tm0)
        else:
            proj = _inproj(xp, g1, w_in_r, l, pad_range, TM_DENSE)
        (odn, dn_p, dnc_p), = _run_parts([_dn_mixer(proj, nblk, blk0, BLOCK, CHUNK, FRONT_PAD, bp, zero_dn, 0,
                                                    zero_dnc, dn_conv_w[l], prm, dn_nw, out_dtype=BF16)], "dn_mixer")
        (ys, ssm_p, ssmc_p), = _run_parts([_ssd_mixer(proj, nblk, blk0, BLOCK, CHUNK, FRONT_PAD, bp, zero_ssm, 0,
                                                      zero_ssmc, ssm_conv_w[l], cbias, prm, drow, ssm_nw,
                                                      out_dtype=BF16)], "ssd_mixer")
        (osw, k_p, v_p), = _run_parts([_swa_prompt(proj, nblk, blk0, FRONT_PAD, cos_p, sin_p, prm, out_dtype=BF16)],
                                      "swa_prompt")
        xp = _tail(xp, odn, ys, osw, w_out_b, g2, g3, g4, w_fi_b, w_fo_b, l, seq if last else lp, TM_DENSE)
        new_p.append((dn_p, dnc_p[:, -(CONV_WIDTH - 1):], ssm_p.reshape(bp, SSM_HEADS, SSM_HEADDIM, SSM_STATE),
                      ssmc_p[:, -(CONV_WIDTH - 1):], k_p.reshape(bp, WINDOW, SWA_KV_HEADS, SWA_HEAD_DIM),
                      v_p.reshape(bp, WINDOW, SWA_KV_HEADS, SWA_HEAD_DIM)))

        proj = _inproj(xs, g1, w_in_r, l, None, TM_DENSE).reshape(bs, ts, D_PROJ)
        (odn, dn_s, dnc_s), (ys, ssm_s, ssmc_s), (osw, k_s, v_s) = _run_parts([
            _dn_mixer(proj, 1, 0, ts, ts, 0, nseq_s, state_dn, l, dn_cbuf_s[l], dn_conv_w[l],
                      prm, dn_nw, stack=(depth, dn_s)),
            _ssd_mixer(proj, 1, 0, ts, ts, 0, nseq_s, state_ssm_g, l, ssm_cbuf_s[l],
                       ssm_conv_w[l], cbias, prm, drow, ssm_nw, stack=(depth, ssm_s)),
            _swa_sample(proj, nseq_s, cache_k, cache_v, l, cos_s, sin_s, prm,
                        stack_k=(depth, k_s), stack_v=(depth, v_s))], "mixers_sample")
        flat = lambda a: a.reshape(1, bs * ts, a.shape[-1])
        xs = _tail(xs, flat(odn), flat(ys), flat(osw), w_out_b, g2, g3, g4, w_fi_b, w_fo_b, l, bs * ts, TM_DENSE)
        new_s.append((dnc_s[:, -(CONV_WIDTH - 1):], ssmc_s[:, -(CONV_WIDTH - 1):]))

    outs_p = tuple(jnp.stack([st[i] for st in new_p]) for i in range(6))
    dnc_s, ssmc_s = (jnp.stack([st[i] for st in new_s]) for i in range(2))
    outs_s = (dn_s, dnc_s, ssm_s.reshape(depth, bs, SSM_HEADS, SSM_HEADDIM, SSM_STATE), ssmc_s,
              k_s.reshape(depth, bs, WINDOW, SWA_KV_HEADS, SWA_HEAD_DIM),
              v_s.reshape(depth, bs, WINDOW, SWA_KV_HEADS, SWA_HEAD_DIM))
    return (xp, xs.reshape(bs, ts, d)) + outs_p + outs_s
```

```python
import functools

import jax
import jax.numpy as jnp
from jax import lax
from jax.experimental import pallas as pl
from jax.experimental.pallas import tpu as pltpu

F32 = jnp.float32
BF16 = jnp.bfloat16
NT = (((1,), (1,)), ((), ()))
TN = (((0,), (0,)), ((), ()))

N_META = 16
CONV_WIDTH = 4
CHUNK = 64
BLOCK = 128
WINDOW = 128
FRONT_PAD = BLOCK - N_META
ROPE_THETA = 10000.0
PAST_LEN = 8192
EPS = 1e-6

DN_HEADS, DN_DK, DN_DV = 4, 128, 128
DN_QK = DN_HEADS * DN_DK
DN_V = DN_HEADS * DN_DV
DN_CONV = 2 * DN_QK + DN_V
SSM_HEADS, SSM_HEADDIM, SSM_GROUPS, SSM_STATE = 4, 64, 2, 128
SSM_INNER = SSM_HEADS * SSM_HEADDIM
SSM_BC = SSM_GROUPS * SSM_STATE
SSM_CONV = SSM_INNER + 2 * SSM_BC
SWA_Q_HEADS, SWA_KV_HEADS, SWA_HEAD_DIM = 4, 2, 64
SWA_Q = SWA_Q_HEADS * SWA_HEAD_DIM
SWA_KV = SWA_KV_HEADS * SWA_HEAD_DIM
IN_WIDTHS = (DN_CONV, DN_V, DN_HEADS, DN_HEADS, SSM_CONV, SSM_INNER, SSM_HEADS, SWA_Q, SWA_KV, SWA_KV)

LANES = 128
SUBLANES = 8
COL_QKV = (0, DN_CONV)
COL_DNZ = (COL_QKV[0] + DN_CONV, DN_V)
COL_SSZ = (COL_DNZ[0] + DN_V, SSM_INNER)
COL_XBC = (COL_SSZ[0] + SSM_INNER, SSM_CONV)
COL_SWQ = (COL_XBC[0] + SSM_CONV, SWA_Q)
COL_SWK = (COL_SWQ[0] + SWA_Q, SWA_KV)
COL_SWV = (COL_SWK[0] + SWA_KV, SWA_KV)
COL_SM = (COL_SWV[0] + SWA_KV, LANES)
D_PROJ = COL_SM[0] + LANES
assert all(off % width == 0 for off, width in (COL_QKV, COL_DNZ, COL_SSZ, COL_XBC, COL_SWQ, COL_SWK, COL_SWV, COL_SM))
SM_B, SM_A, SM_DT = 0, DN_HEADS, 2 * DN_HEADS
PRM_DN_ALOG, PRM_DN_DTB, PRM_SSM_ALOG, PRM_SSM_DTB, PRM_SINK = 0, 1, 2, 3, 4
NEG_BIG = -1e30
VMEM_LIMIT = 56 * 1024 * 1024
TM_DENSE = 640
TM_FIRST = 512


def _bdot(a, b, dims=None):
    a = a.astype(BF16)
    b = b.astype(BF16)
    if dims is None:
        return jnp.dot(a, b, preferred_element_type=F32)
    return lax.dot_general(a, b, dims, preferred_element_type=F32)


def _cumsum_rows(lmat, g):
    hi = g.astype(BF16)
    r1 = g - hi.astype(F32)
    mid = r1.astype(BF16)
    lo = (r1 - mid.astype(F32)).astype(BF16)
    dot = lambda part: jnp.dot(lmat, part, preferred_element_type=F32)
    return dot(hi) + dot(mid) + dot(lo)


def _rmsnorm(x, g):
    return x * lax.rsqrt(jnp.mean(x * x, axis=-1, keepdims=True) + EPS) * g


def _l2norm(x):
    return x * lax.rsqrt(jnp.sum(x * x, axis=-1, keepdims=True) + EPS)


def _sigmoid(x):
    return 1.0 / (1.0 + jnp.exp(-x))


def _silu(x):
    return x * _sigmoid(x)


def _softplus(x):
    return jnp.maximum(x, 0.0) + jnp.log1p(jnp.exp(-jnp.abs(x)))


def _pick_tile(n, target):
    best = None
    for t in range(SUBLANES, min(n, target) + 1, SUBLANES):
        if n % t == 0:
            best = t
    assert best is not None, n
    return best


def _project_rows(x, g, w, tm, pad_range):
    h = _rmsnorm(x, g)
    if pad_range:
        r = pl.program_id(1) * tm + lax.broadcasted_iota(jnp.int32, (tm, 1), 0)
        is_pad = None
        for lo, hi in pad_range:
            hit = (r >= lo) & (r < hi)
            is_pad = hit if is_pad is None else is_pad | hit
        h = jnp.where(is_pad, 0.0, h)
    return jnp.dot(h.astype(BF16), w, preferred_element_type=F32)


def _inproj_kernel(x_ref, g_ref, w_ref, o_ref, *, tm, pad_range):
    o_ref[...] = _project_rows(x_ref[...], g_ref[...], w_ref[...], tm, pad_range)


def _inproj_first_kernel(xm_ref, xt_ref, g_ref, w_ref, o_ref, xo_ref, *, tm, n_main, pad_range):
    x = jnp.where(pl.program_id(1) == n_main, xt_ref[...], xm_ref[...])
    xo_ref[...] = x
    o_ref[...] = _project_rows(x, g_ref[...], w_ref[...], tm, pad_range)


def _inproj_first(x_main, x_tail, g, w, layer, pad_range, tm):
    nb, rows, d = x_main.shape
    assert rows % tm == 0 and x_tail.shape == (tm, d)
    n_main = rows // tm
    return pl.pallas_call(
        functools.partial(_inproj_first_kernel, tm=tm, n_main=n_main, pad_range=pad_range),
        out_shape=(jax.ShapeDtypeStruct((nb, rows + tm, D_PROJ), F32), jax.ShapeDtypeStruct((nb, rows + tm, d), F32)),
        grid=(nb, n_main + 1),
        in_specs=[pl.BlockSpec((None, tm, d), lambda b, i: (b, jnp.minimum(i, n_main - 1), 0)),
                  pl.BlockSpec((tm, d), lambda b, i: (0, 0)),
                  _layer_spec(g.shape, layer), _layer_spec(w.shape, layer)],
        out_specs=(pl.BlockSpec((None, tm, D_PROJ), lambda b, i: (b, i, 0)),
                   pl.BlockSpec((None, tm, d), lambda b, i: (b, i, 0))),
        compiler_params=pltpu.CompilerParams(dimension_semantics=("arbitrary", "arbitrary"),
                                             vmem_limit_bytes=VMEM_LIMIT),
        name="inproj_first",
    )(x_main, x_tail, g, w)


def _inproj(x, g, w, layer, pad_range, tm_target):
    nb, rows, d = x.shape
    tm = _pick_tile(rows, tm_target)
    return pl.pallas_call(
        functools.partial(_inproj_kernel, tm=tm, pad_range=pad_range),
        out_shape=jax.ShapeDtypeStruct((nb, rows, D_PROJ), F32),
        grid=(nb, rows // tm),
        in_specs=[pl.BlockSpec((None, tm, d), lambda b, i: (b, i, 0)), _layer_spec(g.shape, layer),
                  _layer_spec(w.shape, layer)],
        out_specs=pl.BlockSpec((None, tm, D_PROJ), lambda b, i: (b, i, 0)),
        compiler_params=pltpu.CompilerParams(dimension_semantics=("arbitrary", "arbitrary"),
                                             vmem_limit_bytes=VMEM_LIMIT),
        name="inproj",
    )(x, g, w)


def _causal_conv(xbuf, raw_ref, cw_ref, cbo_ref, rows):
    x = raw_ref[...]
    prev = xbuf[0:SUBLANES, :]
    cw = cw_ref[...]
    row = lax.broadcasted_iota(jnp.int32, (SUBLANES, 1), 0)
    acc = None
    for i in range(CONV_WIDTH):
        s = CONV_WIDTH - 1 - i
        if s == 0:
            xs = x
        else:
            r = pltpu.roll(x, s, axis=0)
            head = jnp.where(row < s, pltpu.roll(prev, s, axis=0), r[0:SUBLANES])
            xs = head if rows == SUBLANES else jnp.concatenate([head, r[SUBLANES:]], axis=0)
        term = xs * cw[i:i + 1, :]
        acc = term if acc is None else acc + term
    tail = x[rows - SUBLANES:rows]
    cbo_ref[...] = tail
    xbuf[0:SUBLANES, :] = tail
    return acc


def _causal_mask(c):
    return lax.broadcasted_iota(jnp.int32, (c, c), 0) >= lax.broadcasted_iota(jnp.int32, (c, c), 1)


def _segment_decay(gc, gct, lane, ge):
    col = gc[:, lane:lane + 1]
    row = gct[lane:lane + 1, :]
    return jnp.where(ge, jnp.exp(jnp.where(ge, col - row, 0.0)), 0.0)


def _inv_unit_lower_minus_eye(a_list, c, nh):
    w = nh * c
    blk_r = lax.broadcasted_iota(jnp.int32, (w, w), 0) // c
    blk_c = lax.broadcasted_iota(jnp.int32, (w, w), 1) // c
    same = blk_r == blk_c

    def block_diag(p):
        return jnp.where(same, jnp.concatenate([p] * nh, axis=0), 0.0).astype(BF16)

    ys = [-a for a in a_list]
    ps = [_bdot(a, block_diag(a)) for a in a_list]
    yield
    n = 2
    while n < c:
        pbds = [block_diag(p) for p in ps]
        n *= 2
        if n < c:
            sts = [_bdot(jnp.concatenate([y, p], axis=0), pbd) for y, p, pbd in zip(ys, ps, pbds)]
            ys = [y + p + st[:c] for y, p, st in zip(ys, ps, sts)]
            ps = [st[c:] for st in sts]
        else:
            ys = [y + p + _bdot(y, pbd) for y, p, pbd in zip(ys, ps, pbds)]
        yield
    return ys


def _run_parts(parts, name):
    grid = parts[0]["grid"]
    assert all(p["grid"] == grid for p in parts)
    n_in = [len(p["inputs"]) for p in parts]
    n_out = [len(p["out_shape"]) for p in parts]
    n_scr = [len(p["scratch"]) for p in parts]
    aliases = {}
    for k, p in enumerate(parts):
        for i, o in p["aliases"].items():
            aliases[sum(n_in[:k]) + i] = sum(n_out[:k]) + o

    def kernel(*refs):
        ins = refs[:sum(n_in)]
        outs = refs[sum(n_in):sum(n_in) + sum(n_out)]
        scr = refs[sum(n_in) + sum(n_out):]
        gens = [p["body"](ins[sum(n_in[:k]):sum(n_in[:k + 1])], outs[sum(n_out[:k]):sum(n_out[:k + 1])],
                          scr[sum(n_scr[:k]):sum(n_scr[:k + 1])]) for k, p in enumerate(parts)]
        for tag in gens[0]:
            if tag == "chain":
                break
        live = list(gens)
        while live:
            for g in list(live):
                if next(g, StopIteration) is StopIteration:
                    live.remove(g)

    results = pl.pallas_call(
        kernel,
        out_shape=tuple(s for p in parts for s in p["out_shape"]),
        grid=grid,
        in_specs=[s for p in parts for s in p["in_specs"]],
        out_specs=tuple(s for p in parts for s in p["out_specs"]),
        scratch_shapes=[s for p in parts for s in p["scratch"]],
        input_output_aliases=aliases,
        compiler_params=pltpu.CompilerParams(dimension_semantics=("arbitrary",) * len(grid),
                                             vmem_limit_bytes=VMEM_LIMIT),
        name=name,
    )(*[a for p in parts for a in p["inputs"]])
    return [list(results[sum(n_out[:k]):sum(n_out[:k + 1])]) for k in range(len(parts))]


def _state_out(stack, layer, nb, nseq, tail, out_index, n_inputs):
    zeros = (0,) * len(tail)
    if stack is None:
        return ((nb,) + tail, pl.BlockSpec((nseq,) + tail, lambda b, *_: (b,) + zeros), [], [], {}, None, 0)
    depth, prev = stack
    shape = (depth, nb) + tail
    if prev is None:
        spec = pl.BlockSpec((depth, nseq) + tail, lambda b, *_: (0, b) + zeros)
        return (shape, spec, [], [], {}, layer, depth)
    spec = pl.BlockSpec((None, nseq) + tail, lambda b, *_: (layer, b) + zeros)
    return (shape, spec, [prev], [pl.BlockSpec(memory_space=pl.ANY)], {n_inputs: out_index}, None, 0)


def _state_view(ref, slot, n_slots):
    if slot is None:
        return ref
    for other in range(n_slots):
        if other != slot:
            ref[other] = jnp.zeros(ref.shape[1:], ref.dtype)
    return ref.at[slot]


def _dn_prep(items, lmat, ge, gt_all, chunk):
    heads = range(DN_HEADS)
    gcs = [_cumsum_rows(lmat, g_all) for _, _, g_all in items]
    gcts = [gc.T for gc in gcs]
    qs = [[_l2norm(qkv[:, h * DN_DK:(h + 1) * DN_DK]) * DN_DK ** -0.5 for h in heads] for qkv, _, _ in items]
    ks = [[_l2norm(qkv[:, DN_QK + h * DN_DK:DN_QK + (h + 1) * DN_DK]) for h in heads] for qkv, _, _ in items]
    vs = [[qkv[:, 2 * DN_QK + h * DN_DV:2 * DN_QK + (h + 1) * DN_DV] for h in heads] for qkv, _, _ in items]
    betas = [[beta_all[:, SM_B + h:SM_B + h + 1] for h in heads] for _, beta_all, _ in items]
    yield
    kbs = [[k.astype(BF16) for k in kk] for kk in ks]
    kks = [jnp.concatenate([_bdot(kb, kb, NT) for kb in kb4], axis=1) for kb4 in kbs]
    qks = [jnp.concatenate([_bdot(q, kb, NT) for q, kb in zip(q4, kb4)], axis=1) for q4, kb4 in zip(qs, kbs)]
    decs = [jnp.concatenate([_segment_decay(gc, gct, SM_A + h, ge) for h in heads], axis=1)
            for gc, gct in zip(gcs, gcts)]
    beta_ws = [jnp.concatenate([jnp.broadcast_to(b, (chunk, chunk)) for b in b4], axis=1) for b4 in betas]
    a_list = [jnp.where(gt_all, bw * kk * dec, 0.0) for bw, kk, dec in zip(beta_ws, kks, decs)]
    yield "chain"
    n = len(items)
    egs, ekds, egls, attns, rhss, qds, kds = ([None] * n for _ in range(7))

    def side_work(i):
        gc = gcs[i]
        glast = gc[chunk - 1:chunk, :]
        egs[i] = jnp.exp(gc)
        ekds[i] = jnp.exp(glast - gc)
        egls[i] = jnp.exp(glast)
        attns[i] = (qks[i] * decs[i]).astype(BF16)
        lane = lambda a, h: a[:, SM_A + h:SM_A + h + 1]
        rhss[i] = [jnp.concatenate([vs[i][h] * betas[i][h], ks[i][h] * (betas[i][h] * lane(egs[i], h))], axis=1)
                   for h in heads]
        qds[i] = [(qs[i][h] * lane(egs[i], h)).astype(BF16) for h in heads]
        kds[i] = [(ks[i][h] * lane(ekds[i], h)).astype(BF16) for h in heads]

    inverse = _inv_unit_lower_minus_eye(a_list, chunk, DN_HEADS)
    todo = list(range(n))
    while True:
        try:
            next(inverse)
        except StopIteration as done:
            tms = done.value
            break
        if todo:
            side_work(todo.pop(0))
        yield
    for i in todo:
        side_work(i)
    uws = [[rhss[i][h] + _bdot(tms[i][:, h * chunk:(h + 1) * chunk], rhss[i][h]) for h in heads] for i in range(n)]
    yield
    out = []
    for i in range(n):
        per_head = []
        for h in heads:
            uw = uws[i][h]
            wq = jnp.concatenate([uw[:, DN_DV:].astype(BF16), qds[i][h]], axis=0)
            per_head.append((uw[:, :DN_DV], wq, attns[i][:, h * chunk:(h + 1) * chunk], kds[i][h],
                             egls[i][:, SM_A + h:SM_A + h + 1]))
        out.append(per_head)
    return out


def _dn_body(ins, outs, scratch, *, nseq, rows, chunk, front_pad, slot, n_slots):
    qkv_ref, sm_ref, s0_ref, cb_ref, cw_ref, prm_ref = ins[:6]
    o_ref, so_full, cbo_ref = outs
    xbuf, = scratch
    so_ref = _state_view(so_full, slot, n_slots)
    blk = pl.program_id(1)
    first = blk == 0

    @pl.when(first)
    def _():
        so_ref[...] = s0_ref[...]
        xbuf[:, 0:SUBLANES, :] = cb_ref[...]

    prm = prm_ref[...]
    ge = _causal_mask(chunk)
    lmat = jnp.where(ge, 1.0, 0.0).astype(BF16)
    wide = (chunk, DN_HEADS * chunk)
    gt_all = lax.broadcasted_iota(jnp.int32, wide, 0) > lax.broadcasted_iota(jnp.int32, wide, 1) % chunk
    nchunk = rows // chunk

    items = []
    for s in range(nseq):
        qkv = _silu(_causal_conv(xbuf.at[s], qkv_ref.at[s], cw_ref, cbo_ref.at[s], rows))
        sm = sm_ref[s]
        beta_all = _sigmoid(sm)
        g_all = -jnp.exp(prm[PRM_DN_ALOG:PRM_DN_ALOG + 1, :]) * _softplus(sm + prm[PRM_DN_DTB:PRM_DN_DTB + 1, :])
        if front_pad:
            pos = blk * rows + lax.broadcasted_iota(jnp.int32, (rows, 1), 0)
            g_all = jnp.where(pos < front_pad, 0.0, g_all)
        for c in range(nchunk):
            cs = slice(c * chunk, (c + 1) * chunk)
            items.append((qkv[cs], beta_all[cs], g_all[cs]))
        yield
    prep = yield from _dn_prep(items, lmat, ge, gt_all, chunk)

    chains = [(s, h) for s in range(nseq) for h in range(DN_HEADS)]
    states = [so_ref[s, h] for s, h in chains]
    for c in range(nchunk):
        r0 = c * chunk
        fac = [prep[s * nchunk + c][h] for s, h in chains]
        m1s = [_bdot(f[1], st) for f, st in zip(fac, states)]
        yield
        v_news = [(f[0] - m1[:chunk]).astype(BF16) for f, m1 in zip(fac, m1s)]
        ups = [_bdot(f[3], v, TN) for f, v in zip(fac, v_news)]
        os_ = [m1[chunk:] + _bdot(f[2], v) for f, m1, v in zip(fac, m1s, v_news)]
        yield
        states = [st * f[4] + up for f, st, up in zip(fac, states, ups)]
        for (s, h), o in zip(chains, os_):
            lo = h * DN_DV
            o_ref[s, r0:r0 + chunk, lo:lo + DN_DV] = o
        yield
    for (s, h), st in zip(chains, states):
        so_ref[s, h] = st


def _dn_mixer(proj, nblk, blk0, rows, chunk, front_pad, nseq, s0, layer, cbuf, cw, prm, stack=None):
    nb = proj.shape[0]
    col = lambda c: c[0] // c[1]
    phys = lambda i: (i + blk0) % nblk
    so_shape, so_spec, extra, extra_specs, aliases, slot, n_slots = _state_out(
        stack, layer, nb, nseq, (DN_HEADS, DN_DK, DN_DV), 1, 6)
    return dict(
        body=functools.partial(_dn_body, nseq=nseq, rows=rows, chunk=chunk, front_pad=front_pad,
                               slot=slot, n_slots=n_slots),
        grid=(nb // nseq, nblk),
        inputs=[proj, proj, s0, cbuf, cw, prm] + extra,
        out_shape=[jax.ShapeDtypeStruct((nb, nblk * rows, DN_V), F32),
                   jax.ShapeDtypeStruct(so_shape, F32),
                   jax.ShapeDtypeStruct((nb, SUBLANES, DN_CONV), F32)],
        in_specs=[
            pl.BlockSpec((nseq, rows, DN_CONV), lambda b, i: (b, phys(i), col(COL_QKV))),
            pl.BlockSpec((nseq, rows, LANES), lambda b, i: (b, phys(i), col(COL_SM))),
            pl.BlockSpec((None, nseq, DN_HEADS, DN_DK, DN_DV), lambda b, i: (layer, b, 0, 0, 0)),
            pl.BlockSpec((nseq, SUBLANES, DN_CONV), lambda b, i: (b, 0, 0)),
            pl.BlockSpec((CONV_WIDTH, DN_CONV), lambda b, i: (0, 0)),
            pl.BlockSpec((SUBLANES, LANES), lambda b, i: (0, 0)),
        ] + extra_specs,
        out_specs=[
            pl.BlockSpec((nseq, rows, DN_V), lambda b, i: (b, phys(i), 0)),
            so_spec,
            pl.BlockSpec((nseq, SUBLANES, DN_CONV), lambda b, i: (b, 0, 0)),
        ],
        scratch=[pltpu.VMEM((nseq, SUBLANES, DN_CONV), F32)],
        aliases=aliases,
    )


def _ssd_body(ins, outs, scratch, *, nseq, rows, chunk, front_pad, slot, n_slots):
    xbc_ref, sm_ref, h0_ref, cb_ref, cw_ref, cbias_ref, prm_ref, drow_ref = ins[:8]
    y_ref, ho_full, cbo_ref = outs
    xbuf, = scratch
    ho_ref = _state_view(ho_full, slot, n_slots)
    blk = pl.program_id(1)

    @pl.when(blk == 0)
    def _():
        ho_ref[...] = h0_ref[...]
        xbuf[:, 0:SUBLANES, :] = cb_ref[...]

    prm = prm_ref[...]
    drow = drow_ref[...]
    cbias = cbias_ref[...]
    ge = _causal_mask(chunk)
    lmat = jnp.where(ge, 1.0, 0.0).astype(BF16)
    hpg = SSM_HEADS // SSM_GROUPS
    gw = hpg * SSM_HEADDIM
    lane = lax.broadcasted_iota(jnp.int32, (1, gw), 1)
    srow = lax.broadcasted_iota(jnp.int32, (gw, 1), 0)
    in_head = [(lane >= j * SSM_HEADDIM) & (lane < (j + 1) * SSM_HEADDIM) for j in range(hpg)]
    nchunk = rows // chunk
    groups = range(SSM_GROUPS)

    items = []
    for s in range(nseq):
        act = _silu(_causal_conv(xbuf.at[s], xbc_ref.at[s], cw_ref, cbo_ref.at[s], rows) + cbias)
        dt_all = _softplus(sm_ref[s] + prm[PRM_SSM_DTB:PRM_SSM_DTB + 1, :])
        if front_pad:
            pos = blk * rows + lax.broadcasted_iota(jnp.int32, (rows, 1), 0)
            dt_all = jnp.where(pos < front_pad, 0.0, dt_all)
        g_all = dt_all * (-jnp.exp(prm[PRM_SSM_ALOG:PRM_SSM_ALOG + 1, :]))
        for c in range(nchunk):
            cs = slice(c * chunk, (c + 1) * chunk)
            items.append((act[cs], dt_all[cs], g_all[cs]))
        yield
    n_items = len(items)
    gcs = [_cumsum_rows(lmat, g) for _, _, g in items]
    gcts = [gc.T for gc in gcs]
    xgs = [[a[:, g * gw:(g + 1) * gw] for g in groups] for a, _, _ in items]
    bgs = [[a[:, SSM_INNER + g * SSM_STATE:SSM_INNER + (g + 1) * SSM_STATE] for g in groups] for a, _, _ in items]
    cgs = [[a[:, SSM_INNER + SSM_BC + g * SSM_STATE:SSM_INNER + SSM_BC + (g + 1) * SSM_STATE] for g in groups]
           for a, _, _ in items]
    cbs = [[_bdot(cgs[i][g], bgs[i][g], NT) for g in groups] for i in range(n_items)]
    yield
    egs = [jnp.exp(gc) for gc in gcs]
    ekds = [jnp.exp(gc[chunk - 1:chunk, :] - gc) for gc in gcs]
    egls = [jnp.exp(gc[chunk - 1:chunk, :]) for gc in gcs]
    heads = [(g, j) for g in groups for j in range(hpg)]
    ln = lambda g, j: SM_DT + g * hpg + j
    xdts = [[jnp.where(in_head[j], xgs[i][g] * items[i][1][:, ln(g, j):ln(g, j) + 1], 0.0).astype(BF16)
             for g, j in heads] for i in range(n_items)]
    attns = [[cbs[i][g] * _segment_decay(gcs[i], gcts[i], ln(g, j), ge) for g, j in heads] for i in range(n_items)]
    y_intras = [[_bdot(attns[i][k], xdts[i][k]) for k in range(len(heads))] for i in range(n_items)]
    yield
    upds = [[_bdot(xdts[i][k], bgs[i][g] * ekds[i][:, ln(g, j):ln(g, j) + 1], TN) for k, (g, j) in enumerate(heads)]
            for i in range(n_items)]
    yield
    cds = [[jnp.concatenate([cgs[i][g] * egs[i][:, ln(g, j):ln(g, j) + 1] for j in range(hpg)], axis=0).astype(BF16)
            for g in groups] for i in range(n_items)]
    y_loc = [[sum(y_intras[i][g * hpg + j] for j in range(hpg)) for g in groups] for i in range(n_items)]
    h_inc = [[sum(upds[i][g * hpg + j] for j in range(hpg)) for g in groups] for i in range(n_items)]
    gl_cols = []
    for i in range(n_items):
        per_group = []
        for g in groups:
            gl = egls[i][:, ln(g, 0):ln(g, 0) + 1]
            for j in range(1, hpg):
                gl = jnp.where(srow < j * SSM_HEADDIM, gl, egls[i][:, ln(g, j):ln(g, j) + 1])
            per_group.append(gl)
        gl_cols.append(per_group)

    chains = [(s, g) for s in range(nseq) for g in groups]
    states = [ho_ref[s, g] for s, g in chains]
    for c in range(nchunk):
        r0 = c * chunk
        idx = [s * nchunk + c for s, _ in chains]
        yis = [_bdot(cds[i][g], st, NT) for i, (_, g), st in zip(idx, chains, states)]
        yield
        states = [st * gl_cols[i][g] + h_inc[i][g] for i, (_, g), st in zip(idx, chains, states)]
        for i, (s, g), yi in zip(idx, chains, yis):
            y_inter = yi[0:chunk]
            for j in range(1, hpg):
                y_inter = jnp.where(in_head[j], yi[j * chunk:(j + 1) * chunk], y_inter)
            y_ref[s, r0:r0 + chunk, g * gw:(g + 1) * gw] = (y_loc[i][g] + y_inter
                                                           + xgs[i][g] * drow[:, g * gw:(g + 1) * gw])
        yield
    for (s, g), st in zip(chains, states):
        ho_ref[s, g] = st


def _ssd_mixer(proj, nblk, blk0, rows, chunk, front_pad, nseq, h0, layer, cbuf, cw, cbias, prm, drow, stack=None):
    nb = proj.shape[0]
    col = lambda c: c[0] // c[1]
    phys = lambda i: (i + blk0) % nblk
    gw = (SSM_HEADS // SSM_GROUPS) * SSM_HEADDIM
    ho_shape, ho_spec, extra, extra_specs, aliases, slot, n_slots = _state_out(
        stack, layer, nb, nseq, (SSM_GROUPS, gw, SSM_STATE), 1, 8)
    return dict(
        body=functools.partial(_ssd_body, nseq=nseq, rows=rows, chunk=chunk, front_pad=front_pad,
                               slot=slot, n_slots=n_slots),
        grid=(nb // nseq, nblk),
        inputs=[proj, proj, h0, cbuf, cw, cbias, prm, drow] + extra,
        out_shape=[jax.ShapeDtypeStruct((nb, nblk * rows, SSM_INNER), F32),
                   jax.ShapeDtypeStruct(ho_shape, F32),
                   jax.ShapeDtypeStruct((nb, SUBLANES, SSM_CONV), F32)],
        in_specs=[
            pl.BlockSpec((nseq, rows, SSM_CONV), lambda b, i: (b, phys(i), col(COL_XBC))),
            pl.BlockSpec((nseq, rows, LANES), lambda b, i: (b, phys(i), col(COL_SM))),
            pl.BlockSpec((None, nseq, SSM_GROUPS, gw, SSM_STATE), lambda b, i: (layer, b, 0, 0, 0)),
            pl.BlockSpec((nseq, SUBLANES, SSM_CONV), lambda b, i: (b, 0, 0)),
            pl.BlockSpec((CONV_WIDTH, SSM_CONV), lambda b, i: (0, 0)),
            pl.BlockSpec((1, SSM_CONV), lambda b, i: (0, 0)),
            pl.BlockSpec((SUBLANES, LANES), lambda b, i: (0, 0)),
            pl.BlockSpec((1, SSM_INNER), lambda b, i: (0, 0)),
        ] + extra_specs,
        out_specs=[
            pl.BlockSpec((nseq, rows, SSM_INNER), lambda b, i: (b, phys(i), 0)),
            ho_spec,
            pl.BlockSpec((nseq, SUBLANES, SSM_CONV), lambda b, i: (b, 0, 0)),
        ],
        scratch=[pltpu.VMEM((nseq, SUBLANES, SSM_CONV), F32)],
        aliases=aliases,
    )


def _rope(x, cos, sin_signed):
    w = x.shape[-1]
    half = SWA_HEAD_DIM // 2
    lane = lax.broadcasted_iota(jnp.int32, (1, w), 1)
    first_half = (lane % SWA_HEAD_DIM) < half
    swapped = jnp.where(first_half, pltpu.roll(x, w - half, axis=1), pltpu.roll(x, half, axis=1))
    return x * cos + swapped * sin_signed


def _sink_attend(problems):
    scale = SWA_HEAD_DIM ** -0.5
    scores = [[jnp.where(m, _bdot(q, k, NT) * scale, NEG_BIG) for k, m in zip(keys, masks)]
              for q, keys, _, masks, _ in problems]
    yield
    outs = []
    probs, dens = [], []
    def lane_reduce(tiles, combine, reduce):
        merged = {}
        for t in tiles:
            merged[t.shape[-1]] = t if t.shape[-1] not in merged else combine(merged[t.shape[-1]], t)
        return [reduce(t, axis=-1, keepdims=True) for t in merged.values()]

    for (_, _, _, _, sink), ss in zip(problems, scores):
        mx = sink
        for m in lane_reduce(ss, jnp.maximum, jnp.max):
            mx = jnp.maximum(mx, m)
        probs.append([jnp.exp(s - mx) for s in ss])
        dens.append(jnp.exp(sink - mx))
    yield
    pvs = [[_bdot(p, v) for p, v in zip(ps, vals)] for (_, _, vals, _, _), ps in zip(problems, probs)]
    yield
    for pv, sink_term in zip(pvs, dens):
        acc = pv[0]
        for extra in pv[1:]:
            acc = acc + extra
        outs.append(acc / (pltpu.roll(acc, SWA_HEAD_DIM, axis=1) + sink_term))
    return outs


def _swa_head_order():
    grp = SWA_Q_HEADS // SWA_KV_HEADS
    assert SWA_KV_HEADS * SWA_HEAD_DIM == LANES
    return [j * grp + t for t in range(grp) for j in range(SWA_KV_HEADS)]


def _swa_problems(q, key_sets, val_sets, masks, prm, tq):
    grp = SWA_Q_HEADS // SWA_KV_HEADS
    row = lax.broadcasted_iota(jnp.int32, (grp * tq, 1), 0)
    lane = lax.broadcasted_iota(jnp.int32, (1, LANES), 1)
    qst = jnp.concatenate([q[:, t * LANES:(t + 1) * LANES] for t in range(grp)], axis=0).astype(BF16)
    problems = []
    for j in range(SWA_KV_HEADS):
        half = (lane >= j * SWA_HEAD_DIM) & (lane < (j + 1) * SWA_HEAD_DIM)
        sink = prm[PRM_SINK:PRM_SINK + 1, j * grp:j * grp + 1]
        for t in range(1, grp):
            sink = jnp.where(row < t * tq, sink, prm[PRM_SINK:PRM_SINK + 1, j * grp + t:j * grp + t + 1])
        problems.append((qst, [jnp.where(half, k, 0.0) for k in key_sets],
                         [jnp.where(half, v, 1.0) for v in val_sets], masks, sink))
    return problems


def _swa_tiles(outs, tq):
    grp = SWA_Q_HEADS // SWA_KV_HEADS
    lane = lax.broadcasted_iota(jnp.int32, (1, LANES), 1)
    tiles = []
    for t in range(grp):
        tile = outs[0][t * tq:(t + 1) * tq]
        for j in range(1, SWA_KV_HEADS):
            tile = jnp.where(lane < j * SWA_HEAD_DIM, tile, outs[j][t * tq:(t + 1) * tq])
        tiles.append(tile)
    return tiles


def _swa_prompt_body(ins, outs, scratch, *, nseq, front_pad):
    q_ref, k_ref, v_ref, cos_ref, sin_ref, prm_ref = ins
    o_ref, ko_ref, vo_ref = outs
    kprev, vprev = scratch
    blk = pl.program_id(1)

    @pl.when(blk == 0)
    def _():
        kprev[...] = jnp.zeros_like(kprev)
        vprev[...] = jnp.zeros_like(vprev)

    cos = cos_ref[...]
    sin = sin_ref[...]
    cos_q = jnp.concatenate([cos, cos], axis=1)
    sin_q = jnp.concatenate([sin, sin], axis=1)
    grp = SWA_Q_HEADS // SWA_KV_HEADS
    qi = lax.broadcasted_iota(jnp.int32, (grp * BLOCK, BLOCK), 0) % BLOCK
    kj = lax.broadcasted_iota(jnp.int32, (grp * BLOCK, BLOCK), 1)
    mask_cur = (kj <= qi) & (blk * BLOCK + kj >= front_pad)
    mask_prev = (kj > qi) & ((blk - 1) * BLOCK + kj >= front_pad)
    prm = prm_ref[...]
    problems = []
    for s in range(nseq):
        q = _rope(q_ref[s], cos_q, sin_q)
        k = _rope(k_ref[s], cos, sin)
        v = v_ref[s]
        problems += _swa_problems(q, (kprev[s], k), (vprev[s], v), (mask_prev, mask_cur), prm, BLOCK)
        kprev[s] = k
        vprev[s] = v
        ko_ref[s] = k
        vo_ref[s] = v
        yield
    res = yield from _sink_attend(problems)
    for s in range(nseq):
        for t, tile in enumerate(_swa_tiles(res[s * SWA_KV_HEADS:(s + 1) * SWA_KV_HEADS], BLOCK)):
            o_ref[s, :, t * LANES:(t + 1) * LANES] = tile.astype(o_ref.dtype)


def _swa_prompt(proj, nblk, blk0, front_pad, cos, sin, prm, out_dtype=F32):
    nb = proj.shape[0]
    col = lambda c: c[0] // c[1]
    phys = lambda i: (i + blk0) % nblk
    return dict(
        body=functools.partial(_swa_prompt_body, nseq=nb, front_pad=front_pad),
        grid=(1, nblk),
        inputs=[proj, proj, proj, cos, sin, prm],
        out_shape=[jax.ShapeDtypeStruct((nb, nblk * BLOCK, SWA_Q), out_dtype),
                   jax.ShapeDtypeStruct((nb, WINDOW, SWA_KV), F32),
                   jax.ShapeDtypeStruct((nb, WINDOW, SWA_KV), F32)],
        in_specs=[
            pl.BlockSpec((nb, BLOCK, SWA_Q), lambda b, i: (0, phys(i), col(COL_SWQ))),
            pl.BlockSpec((nb, BLOCK, SWA_KV), lambda b, i: (0, phys(i), col(COL_SWK))),
            pl.BlockSpec((nb, BLOCK, SWA_KV), lambda b, i: (0, phys(i), col(COL_SWV))),
            pl.BlockSpec((BLOCK, SWA_KV), lambda b, i: (i, 0)),
            pl.BlockSpec((BLOCK, SWA_KV), lambda b, i: (i, 0)),
            pl.BlockSpec((SUBLANES, LANES), lambda b, i: (0, 0)),
        ],
        out_specs=[
            pl.BlockSpec((nb, BLOCK, SWA_Q), lambda b, i: (0, phys(i), 0)),
            pl.BlockSpec((nb, WINDOW, SWA_KV), lambda b, i: (0, 0, 0)),
            pl.BlockSpec((nb, WINDOW, SWA_KV), lambda b, i: (0, 0, 0)),
        ],
        scratch=[pltpu.VMEM((nb, BLOCK, SWA_KV), F32), pltpu.VMEM((nb, BLOCK, SWA_KV), F32)],
        aliases={},
    )


def _swa_sample_body(ins, outs, scratch, *, nseq, steps, slot, n_slots):
    q_ref, k_ref, v_ref, kc_ref, vc_ref, cos_ref, sin_ref, prm_ref = ins[:8]
    o_ref, ko_full, vo_full = outs
    ko_ref = _state_view(ko_full, slot, n_slots)
    vo_ref = _state_view(vo_full, slot, n_slots)
    cos = cos_ref[...]
    sin = sin_ref[...]
    cos_q = jnp.concatenate([cos, cos], axis=1)
    sin_q = jnp.concatenate([sin, sin], axis=1)
    prm = prm_ref[...]
    grp = SWA_Q_HEADS // SWA_KV_HEADS
    ti = lax.broadcasted_iota(jnp.int32, (grp * steps, WINDOW), 0) % steps
    sj = lax.broadcasted_iota(jnp.int32, (grp * steps, WINDOW), 1)
    mask_cache = sj > ti
    tn = lax.broadcasted_iota(jnp.int32, (grp * steps, steps), 0) % steps
    sn = lax.broadcasted_iota(jnp.int32, (grp * steps, steps), 1)
    mask_new = sn <= tn
    problems = []
    for b in range(nseq):
        q = _rope(q_ref[b], cos_q, sin_q)
        k = _rope(k_ref[b], cos, sin)
        v = v_ref[b]
        kc = kc_ref[b]
        vc = vc_ref[b]
        ko_ref[b, 0:WINDOW - steps, :] = kc[steps:WINDOW, :]
        ko_ref[b, WINDOW - steps:WINDOW, :] = k
        vo_ref[b, 0:WINDOW - steps, :] = vc[steps:WINDOW, :]
        vo_ref[b, WINDOW - steps:WINDOW, :] = v
        problems += _swa_problems(q, (kc, k), (vc, v), (mask_cache, mask_new), prm, steps)
        yield
    res = yield from _sink_attend(problems)
    for b in range(nseq):
        for t, tile in enumerate(_swa_tiles(res[b * SWA_KV_HEADS:(b + 1) * SWA_KV_HEADS], steps)):
            o_ref[b, :, t * LANES:(t + 1) * LANES] = tile


def _swa_sample(proj, nseq, kc, vc, layer, cos, sin, prm, stack_k=None, stack_v=None):
    nb, steps, _ = proj.shape
    assert WINDOW > steps
    col = lambda c: c[0] // c[1]
    ko_shape, ko_spec, extra_k, specs_k, alias_k, slot, n_slots = _state_out(
        stack_k, layer, nb, nseq, (WINDOW, SWA_KV), 1, 8)
    vo_shape, vo_spec, extra_v, specs_v, alias_v, _, _ = _state_out(
        stack_v, layer, nb, nseq, (WINDOW, SWA_KV), 2, 8 + len(extra_k))
    return dict(
        body=functools.partial(_swa_sample_body, nseq=nseq, steps=steps, slot=slot, n_slots=n_slots),
        grid=(nb // nseq, 1),
        inputs=[proj, proj, proj, kc, vc, cos, sin, prm] + extra_k + extra_v,
        out_shape=[jax.ShapeDtypeStruct((nb, steps, SWA_Q), F32),
                   jax.ShapeDtypeStruct(ko_shape, F32),
                   jax.ShapeDtypeStruct(vo_shape, F32)],
        in_specs=[
            pl.BlockSpec((nseq, steps, SWA_Q), lambda b, i: (b, 0, col(COL_SWQ))),
            pl.BlockSpec((nseq, steps, SWA_KV), lambda b, i: (b, 0, col(COL_SWK))),
            pl.BlockSpec((nseq, steps, SWA_KV), lambda b, i: (b, 0, col(COL_SWV))),
            pl.BlockSpec((None, nseq, WINDOW, SWA_KV), lambda b, i: (layer, b, 0, 0)),
            pl.BlockSpec((None, nseq, WINDOW, SWA_KV), lambda b, i: (layer, b, 0, 0)),
            pl.BlockSpec((steps, SWA_KV), lambda b, i: (0, 0)),
            pl.BlockSpec((steps, SWA_KV), lambda b, i: (0, 0)),
            pl.BlockSpec((SUBLANES, LANES), lambda b, i: (0, 0)),
        ] + specs_k + specs_v,
        out_specs=[pl.BlockSpec((nseq, steps, SWA_Q), lambda b, i: (b, 0, 0)), ko_spec, vo_spec],
        scratch=[],
        aliases={**alias_k, **alias_v},
    )


def _tail_kernel(x_ref, odn_ref, zdn_ref, y_ref, zss_ref, osw_ref, dnw_ref, ssw_ref, wout_ref, g1_ref, g2_ref, g3_ref,
                 wfi_ref, wfo_ref, o_ref, *, d_ff, tf):
    odn = odn_ref[...]
    dnw = dnw_ref[...]
    dn = jnp.concatenate([_rmsnorm(odn[:, h * DN_DV:(h + 1) * DN_DV], dnw) for h in range(DN_HEADS)], axis=1)
    dn = dn * _silu(zdn_ref[...])
    yg = y_ref[...] * _silu(zss_ref[...])
    ssw = ssw_ref[...]
    gw = SSM_INNER // SSM_GROUPS
    ys = jnp.concatenate([_rmsnorm(yg[:, g * gw:(g + 1) * gw], ssw[:, g * gw:(g + 1) * gw]) for g in range(SSM_GROUPS)],
                         axis=1)
    mixed = jnp.concatenate([dn.astype(BF16), ys.astype(BF16), osw_ref[...].astype(BF16)], axis=1)
    m = jnp.dot(mixed, wout_ref[...], preferred_element_type=F32)
    x1 = x_ref[...] + _rmsnorm(m, g1_ref[...])
    h = _rmsnorm(x1, g2_ref[...]).astype(BF16)
    y2 = None
    for c in range(d_ff // tf):
        gate = jnp.dot(h, wfi_ref[:, c * tf:(c + 1) * tf], preferred_element_type=F32)
        up = jnp.dot(h, wfi_ref[:, d_ff + c * tf:d_ff + (c + 1) * tf], preferred_element_type=F32)
        part = jnp.dot((_silu(gate) * up).astype(BF16), wfo_ref[c * tf:(c + 1) * tf, :], preferred_element_type=F32)
        y2 = part if y2 is None else y2 + part
    o_ref[...] = x1 + _rmsnorm(y2, g3_ref[...])


def _layer_spec(shape, layer):
    nd = len(shape) - 1
    return pl.BlockSpec((None,) + tuple(shape[1:]), lambda *_: (layer,) + (0,) * nd, pipeline_mode=pl.Buffered(1))


def _tail(x, proj, odn, y, osw, dnw, ssw, wout, g1, g2, g3, wfi, wfo, layer, l_out, tm_target):
    nb, _, d = x.shape
    d_ff = wfo.shape[1]
    tm = _pick_tile(l_out, tm_target)
    tf = 2 * LANES if d_ff % (2 * LANES) == 0 else d_ff
    row = lambda w: pl.BlockSpec((None, tm, w), lambda b, i: (b, i, 0))
    gate = lambda c: pl.BlockSpec((None, tm, c[1]), lambda b, i: (b, i, c[0] // c[1]))
    return pl.pallas_call(
        functools.partial(_tail_kernel, d_ff=d_ff, tf=tf),
        out_shape=jax.ShapeDtypeStruct((nb, l_out, d), F32),
        grid=(nb, l_out // tm),
        in_specs=[row(d), row(DN_V), gate(COL_DNZ), row(SSM_INNER), gate(COL_SSZ), row(SWA_Q),
                  _layer_spec(dnw.shape, layer), _layer_spec(ssw.shape, layer), _layer_spec(wout.shape, layer),
                  _layer_spec(g1.shape, layer), _layer_spec(g2.shape, layer), _layer_spec(g3.shape, layer),
                  _layer_spec(wfi.shape, layer), _layer_spec(wfo.shape, layer)],
        out_specs=row(d),
        compiler_params=pltpu.CompilerParams(dimension_semantics=("arbitrary", "arbitrary"),
                                             vmem_limit_bytes=VMEM_LIMIT),
        name="outproj_ffn",
    )(x, odn, proj, y, proj, osw, dnw, ssw, wout, g1, g2, g3, wfi, wfo)


def _reorder_w_in_kernel(w_ref, o_ref):
    w = w_ref[0]
    offs = [0]
    for wd in IN_WIDTHS:
        offs.append(offs[-1] + wd)
    seg = lambda i: w[:, offs[i]:offs[i + 1]]
    dn_qkv, dn_z, dn_b, dn_a, ssm_xbc, ssm_z, ssm_dt, sw_q, sw_k, sw_v = (seg(i) for i in range(len(IN_WIDTHS)))
    n_small = dn_b.shape[1] + dn_a.shape[1] + ssm_dt.shape[1]
    small = jnp.concatenate([dn_b, dn_a, ssm_dt, jnp.zeros((w.shape[0], LANES - n_small), w.dtype)], axis=1)
    sw_q = jnp.concatenate([sw_q[:, h * SWA_HEAD_DIM:(h + 1) * SWA_HEAD_DIM] for h in _swa_head_order()], axis=1)
    o_ref[0] = jnp.concatenate([dn_qkv, dn_z, ssm_z, ssm_xbc, sw_q, sw_k, sw_v, small], axis=1).astype(BF16)


def _reorder_w_in(w):
    depth, d, d_in = w.shape
    assert d_in == sum(IN_WIDTHS)
    tr = _pick_tile(d, 256)
    return pl.pallas_call(
        _reorder_w_in_kernel,
        out_shape=jax.ShapeDtypeStruct((depth, d, D_PROJ), BF16),
        grid=(depth, d // tr),
        in_specs=[pl.BlockSpec((1, tr, d_in), lambda l, i: (l, i, 0))],
        out_specs=pl.BlockSpec((1, tr, D_PROJ), lambda l, i: (l, i, 0)),
        compiler_params=pltpu.CompilerParams(dimension_semantics=("arbitrary", "arbitrary")),
        name="reorder_w_in",
    )(w)


def _scalar_param_tiles(rows):
    depth = rows[0][1].shape[0]
    padded = [jnp.pad(v.astype(F32), ((0, 0), (off, LANES - off - v.shape[1]))) for off, v in rows]
    padded += [jnp.zeros((depth, LANES), F32)] * (SUBLANES - len(rows))
    return jnp.stack(padded, axis=1)


def _rope_tables(pos):
    half = SWA_HEAD_DIM // 2
    inv = ROPE_THETA ** (-jnp.arange(half, dtype=F32) / half)
    ang = pos.astype(F32)[:, None] * inv[None, :]
    cos = jnp.cos(ang)
    sin = jnp.sin(ang)
    cos_t = jnp.concatenate([cos, cos] * SWA_KV_HEADS, axis=1)
    sin_t = jnp.concatenate([-sin, sin] * SWA_KV_HEADS, axis=1)
    return cos_t, sin_t


def kernel(x_prompt, x_sample, state_dn, state_dn_conv, state_ssm, state_ssm_conv, cache_swa_k, cache_swa_v,
           meta_tokens, w_in, dn_conv_w, dn_a_log, dn_dt_bias, dn_norm_w, ssm_conv_w, ssm_conv_b, ssm_a_log,
           ssm_dt_bias, ssm_d, ssm_norm_w, swa_sinks, w_out, g_pre_mix, g_post_mix, g_pre_ffn, g_post_ffn,
           w_ffn_in, w_ffn_out):
    bp, seq, d = x_prompt.shape
    bs, ts, _ = x_sample.shape
    depth = w_in.shape[0]
    lp = N_META + seq + FRONT_PAD
    assert lp % BLOCK == 0 and BLOCK % CHUNK == 0 and seq % BLOCK == 0
    nblk = lp // BLOCK
    blk0 = nblk - 1
    tm0 = _pick_tile(seq, TM_FIRST)
    assert tm0 >= BLOCK
    x_front = jnp.concatenate([jnp.zeros((FRONT_PAD, d), x_prompt.dtype), meta_tokens.astype(x_prompt.dtype),
                               jnp.zeros((tm0 - BLOCK, d), x_prompt.dtype)], axis=0)
    pad_first = ((seq, seq + FRONT_PAD), (lp, seq + tm0))
    pad_range = ((seq, seq + FRONT_PAD),)
    xp = None
    xs = x_sample.reshape(1, bs * ts, d)

    cos_p, sin_p = _rope_tables(jnp.arange(lp, dtype=jnp.int32) - FRONT_PAD)
    cos_s, sin_s = _rope_tables(PAST_LEN + jnp.arange(ts, dtype=jnp.int32))

    gw = (SSM_HEADS // SSM_GROUPS) * SSM_HEADDIM
    nseq_s = _pick_tile(bs, 8) if bs % SUBLANES == 0 else bs
    zero_dn = jnp.zeros((1, bp, DN_HEADS, DN_DK, DN_DV), F32)
    zero_dnc = jnp.zeros((bp, SUBLANES, DN_CONV), F32)
    zero_ssm = jnp.zeros((1, bp, SSM_GROUPS, gw, SSM_STATE), F32)
    zero_ssmc = jnp.zeros((bp, SUBLANES, SSM_CONV), F32)
    state_ssm_g = state_ssm.reshape(depth, bs, SSM_GROUPS, gw, SSM_STATE)
    cache_k = cache_swa_k.reshape(depth, bs, WINDOW, SWA_KV)
    cache_v = cache_swa_v.reshape(depth, bs, WINDOW, SWA_KV)

    w_in_r = _reorder_w_in(w_in)
    swa0 = DN_V + SSM_INNER
    w_out_b = jnp.concatenate(
        [w_out[:, :swa0]] + [w_out[:, swa0 + h * SWA_HEAD_DIM:swa0 + (h + 1) * SWA_HEAD_DIM] for h in _swa_head_order()],
        axis=1).astype(BF16)
    w_fi_b = w_ffn_in.astype(BF16)
    w_fo_b = w_ffn_out.astype(BF16)
    g1, g2, g3, g4 = (a[:, None, :] for a in (g_pre_mix, g_post_mix, g_pre_ffn, g_post_ffn))
    dn_nw = dn_norm_w[:, None, :]
    ssm_nw = ssm_norm_w[:, None, :]

    new_p, new_s = [], []
    dn_s = ssm_s = k_s = v_s = None
    prm_all = _scalar_param_tiles([(SM_A, dn_a_log), (SM_A, dn_dt_bias), (SM_DT, ssm_a_log), (SM_DT, ssm_dt_bias),
                                   (0, swa_sinks)])
    drow_all = jnp.repeat(ssm_d, SSM_HEADDIM, axis=1)
    dn_cbuf_s = jnp.pad(state_dn_conv, ((0, 0), (0, 0), (SUBLANES - (CONV_WIDTH - 1), 0), (0, 0)))
    ssm_cbuf_s = jnp.pad(state_ssm_conv, ((0, 0), (0, 0), (SUBLANES - (CONV_WIDTH - 1), 0), (0, 0)))
    for l in range(depth):
        prm = prm_all[l]
        drow = drow_all[l][None, :]
        cbias = ssm_conv_b[l][None, :]
        last = l == depth - 1

        if l == 0:
            proj, xp = _inproj_first(x_prompt, x_front, g1, w_in_r, l, pad_first, tm0)
        else:
            proj = _inproj(xp, g1, w_in_r, l, pad_range, TM_DENSE)
        (odn, dn_p, dnc_p), = _run_parts([_dn_mixer(proj, nblk, blk0, BLOCK, CHUNK, FRONT_PAD, bp, zero_dn, 0,
                                                    zero_dnc, dn_conv_w[l], prm)], "dn_mixer")
        (ys, ssm_p, ssmc_p), = _run_parts([_ssd_mixer(proj, nblk, blk0, BLOCK, CHUNK, FRONT_PAD, bp, zero_ssm, 0,
                                                      zero_ssmc, ssm_conv_w[l], cbias, prm, drow)], "ssd_mixer")
        (osw, k_p, v_p), = _run_parts([_swa_prompt(proj, nblk, blk0, FRONT_PAD, cos_p, sin_p, prm, out_dtype=BF16)],
                                      "swa_prompt")
        xp = _tail(xp, proj, odn, ys, osw, dn_nw, ssm_nw, w_out_b, g2, g3, g4, w_fi_b, w_fo_b, l,
                   seq if last else lp, TM_DENSE)
        new_p.append((dn_p, dnc_p[:, -(CONV_WIDTH - 1):], ssm_p.reshape(bp, SSM_HEADS, SSM_HEADDIM, SSM_STATE),
                      ssmc_p[:, -(CONV_WIDTH - 1):], k_p.reshape(bp, WINDOW, SWA_KV_HEADS, SWA_HEAD_DIM),
                      v_p.reshape(bp, WINDOW, SWA_KV_HEADS, SWA_HEAD_DIM)))

        proj_s = _inproj(xs, g1, w_in_r, l, None, TM_DENSE)
        proj = proj_s.reshape(bs, ts, D_PROJ)
        (odn, dn_s, dnc_s), (ys, ssm_s, ssmc_s), (osw, k_s, v_s) = _run_parts([
            _dn_mixer(proj, 1, 0, ts, ts, 0, nseq_s, state_dn, l, dn_cbuf_s[l], dn_conv_w[l], prm,
                      stack=(depth, dn_s)),
            _ssd_mixer(proj, 1, 0, ts, ts, 0, nseq_s, state_ssm_g, l, ssm_cbuf_s[l], ssm_conv_w[l], cbias, prm, drow,
                       stack=(depth, ssm_s)),
            _swa_sample(proj, nseq_s, cache_k, cache_v, l, cos_s, sin_s, prm,
                        stack_k=(depth, k_s), stack_v=(depth, v_s))], "mixers_sample")
        flat = lambda a: a.reshape(1, bs * ts, a.shape[-1])
        xs = _tail(xs, proj_s, flat(odn), flat(ys), flat(osw), dn_nw, ssm_nw, w_out_b, g2, g3, g4, w_fi_b, w_fo_b, l,
                   bs * ts, TM_DENSE)
        new_s.append((dnc_s[:, -(CONV_WIDTH - 1):], ssmc_s[:, -(CONV_WIDTH - 1):]))

    outs_p = tuple(jnp.stack([st[i] for st in new_p]) for i in range(6))
    dnc_s, ssmc_s = (jnp.stack([st[i] for st in new_s]) for i in range(2))
    outs_s = (dn_s, dnc_s, ssm_s.reshape(depth, bs, SSM_HEADS, SSM_HEADDIM, SSM_STATE), ssmc_s,
              k_s.reshape(depth, bs, WINDOW, SWA_KV_HEADS, SWA_HEAD_DIM),
              v_s.reshape(depth, bs, WINDOW, SWA_KV_HEADS, SWA_HEAD_DIM))
    return (xp, xs.reshape(bs, ts, d)) + outs_p + outs_s
```

```python
import functools

import jax
import jax.numpy as jnp
from jax import lax
from jax.experimental import pallas as pl
from jax.experimental.pallas import tpu as pltpu

F32 = jnp.float32
BF16 = jnp.bfloat16
NT = (((1,), (1,)), ((), ()))
TN = (((0,), (0,)), ((), ()))

N_META = 16
CONV_WIDTH = 4
CHUNK = 64
BLOCK = 128
WINDOW = 128
FRONT_PAD = BLOCK - N_META
ROPE_THETA = 10000.0
PAST_LEN = 8192
EPS = 1e-6

DN_HEADS, DN_DK, DN_DV = 4, 128, 128
DN_QK = DN_HEADS * DN_DK
DN_V = DN_HEADS * DN_DV
DN_CONV = 2 * DN_QK + DN_V
SSM_HEADS, SSM_HEADDIM, SSM_GROUPS, SSM_STATE = 4, 64, 2, 128
SSM_INNER = SSM_HEADS * SSM_HEADDIM
SSM_BC = SSM_GROUPS * SSM_STATE
SSM_CONV = SSM_INNER + 2 * SSM_BC
SWA_Q_HEADS, SWA_KV_HEADS, SWA_HEAD_DIM = 4, 2, 64
SWA_Q = SWA_Q_HEADS * SWA_HEAD_DIM
SWA_KV = SWA_KV_HEADS * SWA_HEAD_DIM
IN_WIDTHS = (DN_CONV, DN_V, DN_HEADS, DN_HEADS, SSM_CONV, SSM_INNER, SSM_HEADS, SWA_Q, SWA_KV, SWA_KV)

LANES = 128
SUBLANES = 8
COL_QKV = (0, DN_CONV)
COL_DNZ = (COL_QKV[0] + DN_CONV, DN_V)
COL_SSZ = (COL_DNZ[0] + DN_V, SSM_INNER)
COL_XBC = (COL_SSZ[0] + SSM_INNER, SSM_CONV)
COL_SWQ = (COL_XBC[0] + SSM_CONV, SWA_Q)
COL_SWK = (COL_SWQ[0] + SWA_Q, SWA_KV)
COL_SWV = (COL_SWK[0] + SWA_KV, SWA_KV)
COL_SM = (COL_SWV[0] + SWA_KV, LANES)
D_PROJ = COL_SM[0] + LANES
assert all(off % width == 0 for off, width in (COL_QKV, COL_DNZ, COL_SSZ, COL_XBC, COL_SWQ, COL_SWK, COL_SWV, COL_SM))
SM_B, SM_A, SM_DT = 0, DN_HEADS, 2 * DN_HEADS
PRM_DN_ALOG, PRM_DN_DTB, PRM_SSM_ALOG, PRM_SSM_DTB, PRM_SINK = 0, 1, 2, 3, 4
NEG_BIG = -1e30
VMEM_LIMIT = 56 * 1024 * 1024
TM_DENSE = 640
TM_FIRST = 512


def _bdot(a, b, dims=None):
    a = a.astype(BF16)
    b = b.astype(BF16)
    if dims is None:
        return jnp.dot(a, b, preferred_element_type=F32)
    return lax.dot_general(a, b, dims, preferred_element_type=F32)


def _cumsum_rows(lmat, g):
    hi = g.astype(BF16)
    r1 = g - hi.astype(F32)
    mid = r1.astype(BF16)
    lo = (r1 - mid.astype(F32)).astype(BF16)
    dot = lambda part: jnp.dot(lmat, part, preferred_element_type=F32)
    return dot(hi) + dot(mid) + dot(lo)


def _rmsnorm(x, g):
    return x * lax.rsqrt(jnp.mean(x * x, axis=-1, keepdims=True) + EPS) * g


def _l2norm(x):
    return x * lax.rsqrt(jnp.sum(x * x, axis=-1, keepdims=True) + EPS)


def _sigmoid(x):
    return 1.0 / (1.0 + jnp.exp(-x))


def _silu(x):
    return x * _sigmoid(x)


def _softplus(x):
    return jnp.maximum(x, 0.0) + jnp.log1p(jnp.exp(-jnp.abs(x)))


def _pick_tile(n, target):
    best = None
    for t in range(SUBLANES, min(n, target) + 1, SUBLANES):
        if n % t == 0:
            best = t
    assert best is not None, n
    return best


def _project_rows(x, g_ref, w_ref, prm_ref, cos_ref, sin_ref, o_ref, tm, pad_range):
    h = _rmsnorm(x, g_ref[...])
    is_pad = None
    if pad_range:
        r = pl.program_id(1) * tm + lax.broadcasted_iota(jnp.int32, (tm, 1), 0)
        for lo, hi in pad_range:
            hit = (r >= lo) & (r < hi)
            is_pad = hit if is_pad is None else is_pad | hit
        h = jnp.where(is_pad, 0.0, h)
    proj = jnp.dot(h.astype(BF16), w_ref[...], preferred_element_type=F32)
    q0, k0, v0, s0 = COL_SWQ[0], COL_SWK[0], COL_SWV[0], COL_SM[0]
    cos, sin = cos_ref[...], sin_ref[...]
    reps = SWA_Q // SWA_KV
    o_ref[:, :q0] = proj[:, :q0]
    o_ref[:, q0:k0] = _rope(proj[:, q0:k0], jnp.concatenate([cos] * reps, axis=1), jnp.concatenate([sin] * reps, axis=1))
    o_ref[:, k0:v0] = _rope(proj[:, k0:v0], cos, sin)
    o_ref[:, v0:s0] = proj[:, v0:s0]
    sm = proj[:, s0:]
    prm = prm_ref[...]
    beta = _sigmoid(sm)
    decay = -jnp.exp(prm[PRM_DN_ALOG:PRM_DN_ALOG + 1, :]) * _softplus(sm + prm[PRM_DN_DTB:PRM_DN_DTB + 1, :])
    dt = _softplus(sm + prm[PRM_SSM_DTB:PRM_SSM_DTB + 1, :])
    if is_pad is not None:
        decay = jnp.where(is_pad, 0.0, decay)
        dt = jnp.where(is_pad, 0.0, dt)
    lane = lax.broadcasted_iota(jnp.int32, (1, LANES), 1)
    o_ref[:, s0:] = jnp.where(lane < SM_A, beta, jnp.where(lane < SM_DT, decay, dt))


def _inproj_kernel(x_ref, g_ref, w_ref, prm_ref, cos_ref, sin_ref, o_ref, *, tm, pad_range):
    _project_rows(x_ref[...], g_ref, w_ref, prm_ref, cos_ref, sin_ref, o_ref, tm, pad_range)


def _inproj_first_kernel(xm_ref, xt_ref, g_ref, w_ref, prm_ref, cos_ref, sin_ref, o_ref, xo_ref,
                         *, tm, n_main, pad_range):
    x = jnp.where(pl.program_id(1) == n_main, xt_ref[...], xm_ref[...])
    xo_ref[...] = x
    _project_rows(x, g_ref, w_ref, prm_ref, cos_ref, sin_ref, o_ref, tm, pad_range)


def _inproj_first(x_main, x_tail, g, w, prm, cos, sin, layer, pad_range, tm):
    nb, rows, d = x_main.shape
    assert rows % tm == 0 and x_tail.shape == (tm, d) and cos.shape == (rows + tm, SWA_KV)
    n_main = rows // tm
    table = pl.BlockSpec((tm, SWA_KV), lambda b, i: (i, 0))
    return pl.pallas_call(
        functools.partial(_inproj_first_kernel, tm=tm, n_main=n_main, pad_range=pad_range),
        out_shape=(jax.ShapeDtypeStruct((nb, rows + tm, D_PROJ), F32), jax.ShapeDtypeStruct((nb, rows + tm, d), F32)),
        grid=(nb, n_main + 1),
        in_specs=[pl.BlockSpec((None, tm, d), lambda b, i: (b, jnp.minimum(i, n_main - 1), 0)),
                  pl.BlockSpec((tm, d), lambda b, i: (0, 0)),
                  _layer_spec(g.shape, layer), _layer_spec(w.shape, layer), _layer_spec(prm.shape, layer), table, table],
        out_specs=(pl.BlockSpec((None, tm, D_PROJ), lambda b, i: (b, i, 0)),
                   pl.BlockSpec((None, tm, d), lambda b, i: (b, i, 0))),
        compiler_params=pltpu.CompilerParams(dimension_semantics=("arbitrary", "arbitrary"),
                                             vmem_limit_bytes=VMEM_LIMIT),
        name="inproj_first",
    )(x_main, x_tail, g, w, prm, cos, sin)


def _inproj(x, g, w, prm, cos, sin, layer, pad_range, tm_target):
    nb, rows, d = x.shape
    tm = _pick_tile(rows, tm_target)
    assert cos.shape == (rows, SWA_KV)
    table = pl.BlockSpec((tm, SWA_KV), lambda b, i: (i, 0))
    return pl.pallas_call(
        functools.partial(_inproj_kernel, tm=tm, pad_range=pad_range),
        out_shape=jax.ShapeDtypeStruct((nb, rows, D_PROJ), F32),
        grid=(nb, rows // tm),
        in_specs=[pl.BlockSpec((None, tm, d), lambda b, i: (b, i, 0)), _layer_spec(g.shape, layer),
                  _layer_spec(w.shape, layer), _layer_spec(prm.shape, layer), table, table],
        out_specs=pl.BlockSpec((None, tm, D_PROJ), lambda b, i: (b, i, 0)),
        compiler_params=pltpu.CompilerParams(dimension_semantics=("arbitrary", "arbitrary"),
                                             vmem_limit_bytes=VMEM_LIMIT),
        name="inproj",
    )(x, g, w, prm, cos, sin)


def _causal_conv(xbuf, raw_ref, cw_ref, cbo_ref, rows):
    x = raw_ref[...]
    prev = xbuf[0:SUBLANES, :]
    cw = cw_ref[...]
    row = lax.broadcasted_iota(jnp.int32, (SUBLANES, 1), 0)
    acc = None
    for i in range(CONV_WIDTH):
        s = CONV_WIDTH - 1 - i
        if s == 0:
            xs = x
        else:
            r = pltpu.roll(x, s, axis=0)
            head = jnp.where(row < s, pltpu.roll(prev, s, axis=0), r[0:SUBLANES])
            xs = head if rows == SUBLANES else jnp.concatenate([head, r[SUBLANES:]], axis=0)
        term = xs * cw[i:i + 1, :]
        acc = term if acc is None else acc + term
    tail = x[rows - SUBLANES:rows]
    cbo_ref[...] = tail
    xbuf[0:SUBLANES, :] = tail
    return acc


def _causal_mask(c):
    return lax.broadcasted_iota(jnp.int32, (c, c), 0) >= lax.broadcasted_iota(jnp.int32, (c, c), 1)


def _segment_decay(gc, gct, lane, ge):
    col = gc[:, lane:lane + 1]
    row = gct[lane:lane + 1, :]
    return jnp.where(ge, jnp.exp(jnp.where(ge, col - row, 0.0)), 0.0)


def _inv_unit_lower_minus_eye(a_list, c, nh):
    w = nh * c
    blk_r = lax.broadcasted_iota(jnp.int32, (w, w), 0) // c
    blk_c = lax.broadcasted_iota(jnp.int32, (w, w), 1) // c
    same = blk_r == blk_c

    def block_diag(p):
        return jnp.where(same, jnp.concatenate([p] * nh, axis=0), 0.0).astype(BF16)

    ys = [-a for a in a_list]
    ps = [_bdot(a, block_diag(a)) for a in a_list]
    yield
    n = 2
    while n < c:
        pbds = [block_diag(p) for p in ps]
        n *= 2
        if n < c:
            sts = [_bdot(jnp.concatenate([y, p], axis=0), pbd) for y, p, pbd in zip(ys, ps, pbds)]
            ys = [y + p + st[:c] for y, p, st in zip(ys, ps, sts)]
            ps = [st[c:] for st in sts]
        else:
            ys = [y + p + _bdot(y, pbd) for y, p, pbd in zip(ys, ps, pbds)]
        yield
    return ys


def _run_parts(parts, name):
    grid = parts[0]["grid"]
    assert all(p["grid"] == grid for p in parts)
    n_in = [len(p["inputs"]) for p in parts]
    n_out = [len(p["out_shape"]) for p in parts]
    n_scr = [len(p["scratch"]) for p in parts]
    aliases = {}
    for k, p in enumerate(parts):
        for i, o in p["aliases"].items():
            aliases[sum(n_in[:k]) + i] = sum(n_out[:k]) + o

    def kernel(*refs):
        ins = refs[:sum(n_in)]
        outs = refs[sum(n_in):sum(n_in) + sum(n_out)]
        scr = refs[sum(n_in) + sum(n_out):]
        gens = [p["body"](ins[sum(n_in[:k]):sum(n_in[:k + 1])], outs[sum(n_out[:k]):sum(n_out[:k + 1])],
                          scr[sum(n_scr[:k]):sum(n_scr[:k + 1])]) for k, p in enumerate(parts)]
        for tag in gens[0]:
            if tag == "chain":
                break
        live = list(gens)
        while live:
            for g in list(live):
                if next(g, StopIteration) is StopIteration:
                    live.remove(g)

    results = pl.pallas_call(
        kernel,
        out_shape=tuple(s for p in parts for s in p["out_shape"]),
        grid=grid,
        in_specs=[s for p in parts for s in p["in_specs"]],
        out_specs=tuple(s for p in parts for s in p["out_specs"]),
        scratch_shapes=[s for p in parts for s in p["scratch"]],
        input_output_aliases=aliases,
        compiler_params=pltpu.CompilerParams(dimension_semantics=("arbitrary",) * len(grid),
                                             vmem_limit_bytes=VMEM_LIMIT),
        name=name,
    )(*[a for p in parts for a in p["inputs"]])
    return [list(results[sum(n_out[:k]):sum(n_out[:k + 1])]) for k in range(len(parts))]


def _state_out(stack, layer, nb, nseq, tail, out_index, n_inputs):
    zeros = (0,) * len(tail)
    if stack is None:
        return ((nb,) + tail, pl.BlockSpec((nseq,) + tail, lambda b, *_: (b,) + zeros), [], [], {}, None, 0)
    depth, prev = stack
    shape = (depth, nb) + tail
    if prev is None:
        spec = pl.BlockSpec((depth, nseq) + tail, lambda b, *_: (0, b) + zeros)
        return (shape, spec, [], [], {}, layer, depth)
    spec = pl.BlockSpec((None, nseq) + tail, lambda b, *_: (layer, b) + zeros)
    return (shape, spec, [prev], [pl.BlockSpec(memory_space=pl.ANY)], {n_inputs: out_index}, None, 0)


def _state_view(ref, slot, n_slots):
    if slot is None:
        return ref
    for other in range(n_slots):
        if other != slot:
            ref[other] = jnp.zeros(ref.shape[1:], ref.dtype)
    return ref.at[slot]


def _dn_prep(items, lmat, ge, gt_all, chunk):
    heads = range(DN_HEADS)
    gcs = [_cumsum_rows(lmat, g_all) for _, _, g_all in items]
    gcts = [gc.T for gc in gcs]
    qs = [[_l2norm(qkv[:, h * DN_DK:(h + 1) * DN_DK]) * DN_DK ** -0.5 for h in heads] for qkv, _, _ in items]
    ks = [[_l2norm(qkv[:, DN_QK + h * DN_DK:DN_QK + (h + 1) * DN_DK]) for h in heads] for qkv, _, _ in items]
    vs = [[qkv[:, 2 * DN_QK + h * DN_DV:2 * DN_QK + (h + 1) * DN_DV] for h in heads] for qkv, _, _ in items]
    betas = [[beta_all[:, SM_B + h:SM_B + h + 1] for h in heads] for _, beta_all, _ in items]
    yield
    kbs = [[k.astype(BF16) for k in kk] for kk in ks]
    kks = [jnp.concatenate([_bdot(kb, kb, NT) for kb in kb4], axis=1) for kb4 in kbs]
    qks = [jnp.concatenate([_bdot(q, kb, NT) for q, kb in zip(q4, kb4)], axis=1) for q4, kb4 in zip(qs, kbs)]
    decs = [jnp.concatenate([_segment_decay(gc, gct, SM_A + h, ge) for h in heads], axis=1)
            for gc, gct in zip(gcs, gcts)]
    beta_ws = [jnp.concatenate([jnp.broadcast_to(b, (chunk, chunk)) for b in b4], axis=1) for b4 in betas]
    a_list = [jnp.where(gt_all, bw * kk * dec, 0.0) for bw, kk, dec in zip(beta_ws, kks, decs)]
    yield "chain"
    n = len(items)
    egs, ekds, egls, attns, rhss, qds, kds = ([None] * n for _ in range(7))

    def side_work(i):
        gc = gcs[i]
        glast = gc[chunk - 1:chunk, :]
        egs[i] = jnp.exp(gc)
        ekds[i] = jnp.exp(glast - gc)
        egls[i] = jnp.exp(glast)
        attns[i] = (qks[i] * decs[i]).astype(BF16)
        lane = lambda a, h: a[:, SM_A + h:SM_A + h + 1]
        rhss[i] = [jnp.concatenate([vs[i][h] * betas[i][h], ks[i][h] * (betas[i][h] * lane(egs[i], h))], axis=1)
                   for h in heads]
        qds[i] = [(qs[i][h] * lane(egs[i], h)).astype(BF16) for h in heads]
        kds[i] = [(ks[i][h] * lane(ekds[i], h)).astype(BF16) for h in heads]

    inverse = _inv_unit_lower_minus_eye(a_list, chunk, DN_HEADS)
    todo = list(range(n))
    while True:
        try:
            next(inverse)
        except StopIteration as done:
            tms = done.value
            break
        if todo:
            side_work(todo.pop(0))
        yield
    for i in todo:
        side_work(i)
    uws = [[rhss[i][h] + _bdot(tms[i][:, h * chunk:(h + 1) * chunk], rhss[i][h]) for h in heads] for i in range(n)]
    yield
    out = []
    for i in range(n):
        per_head = []
        for h in heads:
            uw = uws[i][h]
            wq = jnp.concatenate([uw[:, DN_DV:].astype(BF16), qds[i][h]], axis=0)
            per_head.append((uw[:, :DN_DV], wq, attns[i][:, h * chunk:(h + 1) * chunk], kds[i][h],
                             egls[i][:, SM_A + h:SM_A + h + 1]))
        out.append(per_head)
    return out


def _dn_body(ins, outs, scratch, *, nseq, rows, chunk, slot, n_slots):
    qkv_ref, sm_ref, s0_ref, cb_ref, cw_ref = ins[:5]
    o_ref, so_full, cbo_ref = outs
    xbuf, = scratch
    so_ref = _state_view(so_full, slot, n_slots)

    @pl.when(pl.program_id(1) == 0)
    def _():
        so_ref[...] = s0_ref[...]
        xbuf[:, 0:SUBLANES, :] = cb_ref[...]

    ge = _causal_mask(chunk)
    lmat = jnp.where(ge, 1.0, 0.0).astype(BF16)
    wide = (chunk, DN_HEADS * chunk)
    gt_all = lax.broadcasted_iota(jnp.int32, wide, 0) > lax.broadcasted_iota(jnp.int32, wide, 1) % chunk
    nchunk = rows // chunk

    items = []
    for s in range(nseq):
        qkv = _silu(_causal_conv(xbuf.at[s], qkv_ref.at[s], cw_ref, cbo_ref.at[s], rows))
        sm = sm_ref[s]
        for c in range(nchunk):
            cs = slice(c * chunk, (c + 1) * chunk)
            items.append((qkv[cs], sm[cs], sm[cs]))
        yield
    prep = yield from _dn_prep(items, lmat, ge, gt_all, chunk)

    chains = [(s, h) for s in range(nseq) for h in range(DN_HEADS)]
    states = [so_ref[s, h] for s, h in chains]
    for c in range(nchunk):
        r0 = c * chunk
        fac = [prep[s * nchunk + c][h] for s, h in chains]
        m1s = [_bdot(f[1], st) for f, st in zip(fac, states)]
        yield
        v_news = [(f[0] - m1[:chunk]).astype(BF16) for f, m1 in zip(fac, m1s)]
        ups = [_bdot(f[3], v, TN) for f, v in zip(fac, v_news)]
        os_ = [m1[chunk:] + _bdot(f[2], v) for f, m1, v in zip(fac, m1s, v_news)]
        yield
        states = [st * f[4] + up for f, st, up in zip(fac, states, ups)]
        for (s, h), o in zip(chains, os_):
            lo = h * DN_DV
            o_ref[s, r0:r0 + chunk, lo:lo + DN_DV] = o
        yield
    for (s, h), st in zip(chains, states):
        so_ref[s, h] = st


def _dn_mixer(proj, nblk, blk0, rows, chunk, nseq, s0, layer, cbuf, cw, stack=None):
    nb = proj.shape[0]
    col = lambda c: c[0] // c[1]
    phys = lambda i: (i + blk0) % nblk
    so_shape, so_spec, extra, extra_specs, aliases, slot, n_slots = _state_out(
        stack, layer, nb, nseq, (DN_HEADS, DN_DK, DN_DV), 1, 5)
    return dict(
        body=functools.partial(_dn_body, nseq=nseq, rows=rows, chunk=chunk, slot=slot, n_slots=n_slots),
        grid=(nb // nseq, nblk),
        inputs=[proj, proj, s0, cbuf, cw] + extra,
        out_shape=[jax.ShapeDtypeStruct((nb, nblk * rows, DN_V), F32),
                   jax.ShapeDtypeStruct(so_shape, F32),
                   jax.ShapeDtypeStruct((nb, SUBLANES, DN_CONV), F32)],
        in_specs=[
            pl.BlockSpec((nseq, rows, DN_CONV), lambda b, i: (b, phys(i), col(COL_QKV))),
            pl.BlockSpec((nseq, rows, LANES), lambda b, i: (b, phys(i), col(COL_SM))),
            pl.BlockSpec((None, nseq, DN_HEADS, DN_DK, DN_DV), lambda b, i: (layer, b, 0, 0, 0)),
            pl.BlockSpec((nseq, SUBLANES, DN_CONV), lambda b, i: (b, 0, 0)),
            pl.BlockSpec((CONV_WIDTH, DN_CONV), lambda b, i: (0, 0)),
        ] + extra_specs,
        out_specs=[
            pl.BlockSpec((nseq, rows, DN_V), lambda b, i: (b, phys(i), 0)),
            so_spec,
            pl.BlockSpec((nseq, SUBLANES, DN_CONV), lambda b, i: (b, 0, 0)),
        ],
        scratch=[pltpu.VMEM((nseq, SUBLANES, DN_CONV), F32)],
        aliases=aliases,
    )


def _ssd_body(ins, outs, scratch, *, nseq, rows, chunk, slot, n_slots):
    xbc_ref, sm_ref, h0_ref, cb_ref, cw_ref, cbias_ref, prm_ref, drow_ref = ins[:8]
    y_ref, ho_full, cbo_ref = outs
    xbuf, = scratch
    ho_ref = _state_view(ho_full, slot, n_slots)

    @pl.when(pl.program_id(1) == 0)
    def _():
        ho_ref[...] = h0_ref[...]
        xbuf[:, 0:SUBLANES, :] = cb_ref[...]

    prm = prm_ref[...]
    drow = drow_ref[...]
    cbias = cbias_ref[...]
    ge = _causal_mask(chunk)
    lmat = jnp.where(ge, 1.0, 0.0).astype(BF16)
    hpg = SSM_HEADS // SSM_GROUPS
    gw = hpg * SSM_HEADDIM
    lane = lax.broadcasted_iota(jnp.int32, (1, gw), 1)
    srow = lax.broadcasted_iota(jnp.int32, (gw, 1), 0)
    in_head = [(lane >= j * SSM_HEADDIM) & (lane < (j + 1) * SSM_HEADDIM) for j in range(hpg)]
    nchunk = rows // chunk
    groups = range(SSM_GROUPS)

    items = []
    for s in range(nseq):
        act = _silu(_causal_conv(xbuf.at[s], xbc_ref.at[s], cw_ref, cbo_ref.at[s], rows) + cbias)
        dt_all = sm_ref[s]
        g_all = dt_all * (-jnp.exp(prm[PRM_SSM_ALOG:PRM_SSM_ALOG + 1, :]))
        for c in range(nchunk):
            cs = slice(c * chunk, (c + 1) * chunk)
            items.append((act[cs], dt_all[cs], g_all[cs]))
        yield
    n_items = len(items)
    gcs = [_cumsum_rows(lmat, g) for _, _, g in items]
    gcts = [gc.T for gc in gcs]
    xgs = [[a[:, g * gw:(g + 1) * gw] for g in groups] for a, _, _ in items]
    bgs = [[a[:, SSM_INNER + g * SSM_STATE:SSM_INNER + (g + 1) * SSM_STATE] for g in groups] for a, _, _ in items]
    cgs = [[a[:, SSM_INNER + SSM_BC + g * SSM_STATE:SSM_INNER + SSM_BC + (g + 1) * SSM_STATE] for g in groups]
           for a, _, _ in items]
    cbs = [[_bdot(cgs[i][g], bgs[i][g], NT) for g in groups] for i in range(n_items)]
    yield
    egs = [jnp.exp(gc) for gc in gcs]
    ekds = [jnp.exp(gc[chunk - 1:chunk, :] - gc) for gc in gcs]
    egls = [jnp.exp(gc[chunk - 1:chunk, :]) for gc in gcs]
    heads = [(g, j) for g in groups for j in range(hpg)]
    ln = lambda g, j: SM_DT + g * hpg + j
    xdts = [[jnp.where(in_head[j], xgs[i][g] * items[i][1][:, ln(g, j):ln(g, j) + 1], 0.0).astype(BF16)
             for g, j in heads] for i in range(n_items)]
    attns = [[cbs[i][g] * _segment_decay(gcs[i], gcts[i], ln(g, j), ge) for g, j in heads] for i in range(n_items)]
    y_intras = [[_bdot(attns[i][k], xdts[i][k]) for k in range(len(heads))] for i in range(n_items)]
    yield
    upds = [[_bdot(xdts[i][k], bgs[i][g] * ekds[i][:, ln(g, j):ln(g, j) + 1], TN) for k, (g, j) in enumerate(heads)]
            for i in range(n_items)]
    yield
    cds = [[jnp.concatenate([cgs[i][g] * egs[i][:, ln(g, j):ln(g, j) + 1] for j in range(hpg)], axis=0).astype(BF16)
            for g in groups] for i in range(n_items)]
    y_loc = [[sum(y_intras[i][g * hpg + j] for j in range(hpg)) for g in groups] for i in range(n_items)]
    h_inc = [[sum(upds[i][g * hpg + j] for j in range(hpg)) for g in groups] for i in range(n_items)]
    gl_cols = []
    for i in range(n_items):
        per_group = []
        for g in groups:
            gl = egls[i][:, ln(g, 0):ln(g, 0) + 1]
            for j in range(1, hpg):
                gl = jnp.where(srow < j * SSM_HEADDIM, gl, egls[i][:, ln(g, j):ln(g, j) + 1])
            per_group.append(gl)
        gl_cols.append(per_group)

    chains = [(s, g) for s in range(nseq) for g in groups]
    states = [ho_ref[s, g] for s, g in chains]
    for c in range(nchunk):
        r0 = c * chunk
        idx = [s * nchunk + c for s, _ in chains]
        yis = [_bdot(cds[i][g], st, NT) for i, (_, g), st in zip(idx, chains, states)]
        yield
        states = [st * gl_cols[i][g] + h_inc[i][g] for i, (_, g), st in zip(idx, chains, states)]
        for i, (s, g), yi in zip(idx, chains, yis):
            y_inter = yi[0:chunk]
            for j in range(1, hpg):
                y_inter = jnp.where(in_head[j], yi[j * chunk:(j + 1) * chunk], y_inter)
            y_ref[s, r0:r0 + chunk, g * gw:(g + 1) * gw] = (y_loc[i][g] + y_inter
                                                           + xgs[i][g] * drow[:, g * gw:(g + 1) * gw])
        yield
    for (s, g), st in zip(chains, states):
        ho_ref[s, g] = st


def _ssd_mixer(proj, nblk, blk0, rows, chunk, nseq, h0, layer, cbuf, cw, cbias, prm, drow, stack=None):
    nb = proj.shape[0]
    col = lambda c: c[0] // c[1]
    phys = lambda i: (i + blk0) % nblk
    gw = (SSM_HEADS // SSM_GROUPS) * SSM_HEADDIM
    ho_shape, ho_spec, extra, extra_specs, aliases, slot, n_slots = _state_out(
        stack, layer, nb, nseq, (SSM_GROUPS, gw, SSM_STATE), 1, 8)
    return dict(
        body=functools.partial(_ssd_body, nseq=nseq, rows=rows, chunk=chunk, slot=slot, n_slots=n_slots),
        grid=(nb // nseq, nblk),
        inputs=[proj, proj, h0, cbuf, cw, cbias, prm, drow] + extra,
        out_shape=[jax.ShapeDtypeStruct((nb, nblk * rows, SSM_INNER), F32),
                   jax.ShapeDtypeStruct(ho_shape, F32),
                   jax.ShapeDtypeStruct((nb, SUBLANES, SSM_CONV), F32)],
        in_specs=[
            pl.BlockSpec((nseq, rows, SSM_CONV), lambda b, i: (b, phys(i), col(COL_XBC))),
            pl.BlockSpec((nseq, rows, LANES), lambda b, i: (b, phys(i), col(COL_SM))),
            pl.BlockSpec((None, nseq, SSM_GROUPS, gw, SSM_STATE), lambda b, i: (layer, b, 0, 0, 0)),
            pl.BlockSpec((nseq, SUBLANES, SSM_CONV), lambda b, i: (b, 0, 0)),
            pl.BlockSpec((CONV_WIDTH, SSM_CONV), lambda b, i: (0, 0)),
            pl.BlockSpec((1, SSM_CONV), lambda b, i: (0, 0)),
            pl.BlockSpec((SUBLANES, LANES), lambda b, i: (0, 0)),
            pl.BlockSpec((1, SSM_INNER), lambda b, i: (0, 0)),
        ] + extra_specs,
        out_specs=[
            pl.BlockSpec((nseq, rows, SSM_INNER), lambda b, i: (b, phys(i), 0)),
            ho_spec,
            pl.BlockSpec((nseq, SUBLANES, SSM_CONV), lambda b, i: (b, 0, 0)),
        ],
        scratch=[pltpu.VMEM((nseq, SUBLANES, SSM_CONV), F32)],
        aliases=aliases,
    )


def _rope(x, cos, sin_signed):
    w = x.shape[-1]
    half = SWA_HEAD_DIM // 2
    lane = lax.broadcasted_iota(jnp.int32, (1, w), 1)
    first_half = (lane % SWA_HEAD_DIM) < half
    swapped = jnp.where(first_half, pltpu.roll(x, w - half, axis=1), pltpu.roll(x, half, axis=1))
    return x * cos + swapped * sin_signed


def _sink_attend(problems):
    scale = SWA_HEAD_DIM ** -0.5
    scores = [[jnp.where(m, _bdot(q, k, NT) * scale, NEG_BIG) for k, m in zip(keys, masks)]
              for q, keys, _, masks, _ in problems]
    yield
    outs = []
    probs, dens = [], []
    def lane_reduce(tiles, combine, reduce):
        merged = {}
        for t in tiles:
            merged[t.shape[-1]] = t if t.shape[-1] not in merged else combine(merged[t.shape[-1]], t)
        return [reduce(t, axis=-1, keepdims=True) for t in merged.values()]

    for (_, _, _, _, sink), ss in zip(problems, scores):
        mx = sink
        for m in lane_reduce(ss, jnp.maximum, jnp.max):
            mx = jnp.maximum(mx, m)
        probs.append([jnp.exp(s - mx) for s in ss])
        dens.append(jnp.exp(sink - mx))
    yield
    pvs = [[_bdot(p, v) for p, v in zip(ps, vals)] for (_, _, vals, _, _), ps in zip(problems, probs)]
    yield
    for pv, sink_term in zip(pvs, dens):
        acc = pv[0]
        for extra in pv[1:]:
            acc = acc + extra
        outs.append(acc / (pltpu.roll(acc, SWA_HEAD_DIM, axis=1) + sink_term))
    return outs


def _swa_head_order():
    grp = SWA_Q_HEADS // SWA_KV_HEADS
    assert SWA_KV_HEADS * SWA_HEAD_DIM == LANES
    return [j * grp + t for t in range(grp) for j in range(SWA_KV_HEADS)]


def _swa_problems(q, key_sets, val_sets, masks, prm, tq):
    grp = SWA_Q_HEADS // SWA_KV_HEADS
    row = lax.broadcasted_iota(jnp.int32, (grp * tq, 1), 0)
    lane = lax.broadcasted_iota(jnp.int32, (1, LANES), 1)
    qst = jnp.concatenate([q[:, t * LANES:(t + 1) * LANES] for t in range(grp)], axis=0).astype(BF16)
    problems = []
    for j in range(SWA_KV_HEADS):
        half = (lane >= j * SWA_HEAD_DIM) & (lane < (j + 1) * SWA_HEAD_DIM)
        sink = prm[PRM_SINK:PRM_SINK + 1, j * grp:j * grp + 1]
        for t in range(1, grp):
            sink = jnp.where(row < t * tq, sink, prm[PRM_SINK:PRM_SINK + 1, j * grp + t:j * grp + t + 1])
        problems.append((qst, [jnp.where(half, k, 0.0) for k in key_sets],
                         [jnp.where(half, v, 1.0) for v in val_sets], masks, sink))
    return problems


def _swa_tiles(outs, tq):
    grp = SWA_Q_HEADS // SWA_KV_HEADS
    lane = lax.broadcasted_iota(jnp.int32, (1, LANES), 1)
    tiles = []
    for t in range(grp):
        tile = outs[0][t * tq:(t + 1) * tq]
        for j in range(1, SWA_KV_HEADS):
            tile = jnp.where(lane < j * SWA_HEAD_DIM, tile, outs[j][t * tq:(t + 1) * tq])
        tiles.append(tile)
    return tiles


def _swa_prompt_body(ins, outs, scratch, *, nseq, front_pad):
    q_ref, k_ref, v_ref, prm_ref = ins
    o_ref, ko_ref, vo_ref = outs
    kprev, vprev = scratch
    blk = pl.program_id(1)

    @pl.when(blk == 0)
    def _():
        kprev[...] = jnp.zeros_like(kprev)
        vprev[...] = jnp.zeros_like(vprev)

    grp = SWA_Q_HEADS // SWA_KV_HEADS
    qi = lax.broadcasted_iota(jnp.int32, (grp * BLOCK, BLOCK), 0) % BLOCK
    kj = lax.broadcasted_iota(jnp.int32, (grp * BLOCK, BLOCK), 1)
    mask_cur = (kj <= qi) & (blk * BLOCK + kj >= front_pad)
    mask_prev = (kj > qi) & ((blk - 1) * BLOCK + kj >= front_pad)
    prm = prm_ref[...]
    problems = []
    for s in range(nseq):
        k = k_ref[s]
        v = v_ref[s]
        problems += _swa_problems(q_ref[s], (kprev[s], k), (vprev[s], v), (mask_prev, mask_cur), prm, BLOCK)
        kprev[s] = k
        vprev[s] = v
        ko_ref[s] = k
        vo_ref[s] = v
        yield
    res = yield from _sink_attend(problems)
    for s in range(nseq):
        for t, tile in enumerate(_swa_tiles(res[s * SWA_KV_HEADS:(s + 1) * SWA_KV_HEADS], BLOCK)):
            o_ref[s, :, t * LANES:(t + 1) * LANES] = tile.astype(o_ref.dtype)


def _swa_prompt(proj, nblk, blk0, front_pad, prm, out_dtype=F32):
    nb = proj.shape[0]
    col = lambda c: c[0] // c[1]
    phys = lambda i: (i + blk0) % nblk
    return dict(
        body=functools.partial(_swa_prompt_body, nseq=nb, front_pad=front_pad),
        grid=(1, nblk),
        inputs=[proj, proj, proj, prm],
        out_shape=[jax.ShapeDtypeStruct((nb, nblk * BLOCK, SWA_Q), out_dtype),
                   jax.ShapeDtypeStruct((nb, WINDOW, SWA_KV), F32),
                   jax.ShapeDtypeStruct((nb, WINDOW, SWA_KV), F32)],
        in_specs=[
            pl.BlockSpec((nb, BLOCK, SWA_Q), lambda b, i: (0, phys(i), col(COL_SWQ))),
            pl.BlockSpec((nb, BLOCK, SWA_KV), lambda b, i: (0, phys(i), col(COL_SWK))),
            pl.BlockSpec((nb, BLOCK, SWA_KV), lambda b, i: (0, phys(i), col(COL_SWV))),
            pl.BlockSpec((SUBLANES, LANES), lambda b, i: (0, 0)),
        ],
        out_specs=[
            pl.BlockSpec((nb, BLOCK, SWA_Q), lambda b, i: (0, phys(i), 0)),
            pl.BlockSpec((nb, WINDOW, SWA_KV), lambda b, i: (0, 0, 0)),
            pl.BlockSpec((nb, WINDOW, SWA_KV), lambda b, i: (0, 0, 0)),
        ],
        scratch=[pltpu.VMEM((nb, BLOCK, SWA_KV), F32), pltpu.VMEM((nb, BLOCK, SWA_KV), F32)],
        aliases={},
    )


def _swa_sample_body(ins, outs, scratch, *, nseq, steps, slot, n_slots):
    q_ref, k_ref, v_ref, kc_ref, vc_ref, prm_ref = ins[:6]
    o_ref, ko_full, vo_full = outs
    ko_ref = _state_view(ko_full, slot, n_slots)
    vo_ref = _state_view(vo_full, slot, n_slots)
    prm = prm_ref[...]
    grp = SWA_Q_HEADS // SWA_KV_HEADS
    ti = lax.broadcasted_iota(jnp.int32, (grp * steps, WINDOW), 0) % steps
    sj = lax.broadcasted_iota(jnp.int32, (grp * steps, WINDOW), 1)
    mask_cache = sj > ti
    tn = lax.broadcasted_iota(jnp.int32, (grp * steps, steps), 0) % steps
    sn = lax.broadcasted_iota(jnp.int32, (grp * steps, steps), 1)
    mask_new = sn <= tn
    problems = []
    for b in range(nseq):
        q = q_ref[b]
        k = k_ref[b]
        v = v_ref[b]
        kc = kc_ref[b]
        vc = vc_ref[b]
        ko_ref[b, 0:WINDOW - steps, :] = kc[steps:WINDOW, :]
        ko_ref[b, WINDOW - steps:WINDOW, :] = k
        vo_ref[b, 0:WINDOW - steps, :] = vc[steps:WINDOW, :]
        vo_ref[b, WINDOW - steps:WINDOW, :] = v
        problems += _swa_problems(q, (kc, k), (vc, v), (mask_cache, mask_new), prm, steps)
        yield
    res = yield from _sink_attend(problems)
    for b in range(nseq):
        for t, tile in enumerate(_swa_tiles(res[b * SWA_KV_HEADS:(b + 1) * SWA_KV_HEADS], steps)):
            o_ref[b, :, t * LANES:(t + 1) * LANES] = tile


def _swa_sample(proj, nseq, kc, vc, layer, prm, stack_k=None, stack_v=None):
    nb, steps, _ = proj.shape
    assert WINDOW > steps
    col = lambda c: c[0] // c[1]
    ko_shape, ko_spec, extra_k, specs_k, alias_k, slot, n_slots = _state_out(
        stack_k, layer, nb, nseq, (WINDOW, SWA_KV), 1, 6)
    vo_shape, vo_spec, extra_v, specs_v, alias_v, _, _ = _state_out(
        stack_v, layer, nb, nseq, (WINDOW, SWA_KV), 2, 6 + len(extra_k))
    return dict(
        body=functools.partial(_swa_sample_body, nseq=nseq, steps=steps, slot=slot, n_slots=n_slots),
        grid=(nb // nseq, 1),
        inputs=[proj, proj, proj, kc, vc, prm] + extra_k + extra_v,
        out_shape=[jax.ShapeDtypeStruct((nb, steps, SWA_Q), F32),
                   jax.ShapeDtypeStruct(ko_shape, F32),
                   jax.ShapeDtypeStruct(vo_shape, F32)],
        in_specs=[
            pl.BlockSpec((nseq, steps, SWA_Q), lambda b, i: (b, 0, col(COL_SWQ))),
            pl.BlockSpec((nseq, steps, SWA_KV), lambda b, i: (b, 0, col(COL_SWK))),
            pl.BlockSpec((nseq, steps, SWA_KV), lambda b, i: (b, 0, col(COL_SWV))),
            pl.BlockSpec((None, nseq, WINDOW, SWA_KV), lambda b, i: (layer, b, 0, 0)),
            pl.BlockSpec((None, nseq, WINDOW, SWA_KV), lambda b, i: (layer, b, 0, 0)),
            pl.BlockSpec((SUBLANES, LANES), lambda b, i: (0, 0)),
        ] + specs_k + specs_v,
        out_specs=[pl.BlockSpec((nseq, steps, SWA_Q), lambda b, i: (b, 0, 0)), ko_spec, vo_spec],
        scratch=[],
        aliases={**alias_k, **alias_v},
    )


def _tail_kernel(x_ref, odn_ref, zdn_ref, y_ref, zss_ref, osw_ref, dnw_ref, ssw_ref, wout_ref, g1_ref, g2_ref, g3_ref,
                 wfi_ref, wfo_ref, o_ref, *, d_ff, tf):
    odn = odn_ref[...]
    dnw = dnw_ref[...]
    dn = jnp.concatenate([_rmsnorm(odn[:, h * DN_DV:(h + 1) * DN_DV], dnw) for h in range(DN_HEADS)], axis=1)
    dn = dn * _silu(zdn_ref[...])
    yg = y_ref[...] * _silu(zss_ref[...])
    ssw = ssw_ref[...]
    gw = SSM_INNER // SSM_GROUPS
    ys = jnp.concatenate([_rmsnorm(yg[:, g * gw:(g + 1) * gw], ssw[:, g * gw:(g + 1) * gw]) for g in range(SSM_GROUPS)],
                         axis=1)
    mixed = jnp.concatenate([dn.astype(BF16), ys.astype(BF16), osw_ref[...].astype(BF16)], axis=1)
    m = jnp.dot(mixed, wout_ref[...], preferred_element_type=F32)
    x1 = x_ref[...] + _rmsnorm(m, g1_ref[...])
    h = _rmsnorm(x1, g2_ref[...]).astype(BF16)
    y2 = None
    for c in range(d_ff // tf):
        gate = jnp.dot(h, wfi_ref[:, c * tf:(c + 1) * tf], preferred_element_type=F32)
        up = jnp.dot(h, wfi_ref[:, d_ff + c * tf:d_ff + (c + 1) * tf], preferred_element_type=F32)
        part = jnp.dot((_silu(gate) * up).astype(BF16), wfo_ref[c * tf:(c + 1) * tf, :], preferred_element_type=F32)
        y2 = part if y2 is None else y2 + part
    o_ref[...] = x1 + _rmsnorm(y2, g3_ref[...])


def _layer_spec(shape, layer):
    nd = len(shape) - 1
    return pl.BlockSpec((None,) + tuple(shape[1:]), lambda *_: (layer,) + (0,) * nd, pipeline_mode=pl.Buffered(1))


def _tail(x, proj, odn, y, osw, dnw, ssw, wout, g1, g2, g3, wfi, wfo, layer, l_out, tm_target):
    nb, _, d = x.shape
    d_ff = wfo.shape[1]
    tm = _pick_tile(l_out, tm_target)
    tf = 2 * LANES if d_ff % (2 * LANES) == 0 else d_ff
    row = lambda w: pl.BlockSpec((None, tm, w), lambda b, i: (b, i, 0))
    gate = lambda c: pl.BlockSpec((None, tm, c[1]), lambda b, i: (b, i, c[0] // c[1]))
    return pl.pallas_call(
        functools.partial(_tail_kernel, d_ff=d_ff, tf=tf),
        out_shape=jax.ShapeDtypeStruct((nb, l_out, d), F32),
        grid=(nb, l_out // tm),
        in_specs=[row(d), row(DN_V), gate(COL_DNZ), row(SSM_INNER), gate(COL_SSZ), row(SWA_Q),
                  _layer_spec(dnw.shape, layer), _layer_spec(ssw.shape, layer), _layer_spec(wout.shape, layer),
                  _layer_spec(g1.shape, layer), _layer_spec(g2.shape, layer), _layer_spec(g3.shape, layer),
                  _layer_spec(wfi.shape, layer), _layer_spec(wfo.shape, layer)],
        out_specs=row(d),
        compiler_params=pltpu.CompilerParams(dimension_semantics=("arbitrary", "arbitrary"),
                                             vmem_limit_bytes=VMEM_LIMIT),
        name="outproj_ffn",
    )(x, odn, proj, y, proj, osw, dnw, ssw, wout, g1, g2, g3, wfi, wfo)


def _reorder_w_in_kernel(w_ref, o_ref):
    w = w_ref[0]
    offs = [0]
    for wd in IN_WIDTHS:
        offs.append(offs[-1] + wd)
    seg = lambda i: w[:, offs[i]:offs[i + 1]]
    dn_qkv, dn_z, dn_b, dn_a, ssm_xbc, ssm_z, ssm_dt, sw_q, sw_k, sw_v = (seg(i) for i in range(len(IN_WIDTHS)))
    n_small = dn_b.shape[1] + dn_a.shape[1] + ssm_dt.shape[1]
    small = jnp.concatenate([dn_b, dn_a, ssm_dt, jnp.zeros((w.shape[0], LANES - n_small), w.dtype)], axis=1)
    sw_q = jnp.concatenate([sw_q[:, h * SWA_HEAD_DIM:(h + 1) * SWA_HEAD_DIM] for h in _swa_head_order()], axis=1)
    o_ref[0] = jnp.concatenate([dn_qkv, dn_z, ssm_z, ssm_xbc, sw_q, sw_k, sw_v, small], axis=1).astype(BF16)


def _reorder_w_in(w):
    depth, d, d_in = w.shape
    assert d_in == sum(IN_WIDTHS)
    tr = _pick_tile(d, 256)
    return pl.pallas_call(
        _reorder_w_in_kernel,
        out_shape=jax.ShapeDtypeStruct((depth, d, D_PROJ), BF16),
        grid=(depth, d // tr),
        in_specs=[pl.BlockSpec((1, tr, d_in), lambda l, i: (l, i, 0))],
        out_specs=pl.BlockSpec((1, tr, D_PROJ), lambda l, i: (l, i, 0)),
        compiler_params=pltpu.CompilerParams(dimension_semantics=("arbitrary", "arbitrary")),
        name="reorder_w_in",
    )(w)


def _scalar_param_tiles(rows):
    depth = rows[0][1].shape[0]
    padded = [jnp.pad(v.astype(F32), ((0, 0), (off, LANES - off - v.shape[1]))) for off, v in rows]
    padded += [jnp.zeros((depth, LANES), F32)] * (SUBLANES - len(rows))
    return jnp.stack(padded, axis=1)


def _rope_tables(pos):
    half = SWA_HEAD_DIM // 2
    inv = ROPE_THETA ** (-jnp.arange(half, dtype=F32) / half)
    ang = pos.astype(F32)[:, None] * inv[None, :]
    cos = jnp.cos(ang)
    sin = jnp.sin(ang)
    cos_t = jnp.concatenate([cos, cos] * SWA_KV_HEADS, axis=1)
    sin_t = jnp.concatenate([-sin, sin] * SWA_KV_HEADS, axis=1)
    return cos_t, sin_t


def kernel(x_prompt, x_sample, state_dn, state_dn_conv, state_ssm, state_ssm_conv, cache_swa_k, cache_swa_v,
           meta_tokens, w_in, dn_conv_w, dn_a_log, dn_dt_bias, dn_norm_w, ssm_conv_w, ssm_conv_b, ssm_a_log,
           ssm_dt_bias, ssm_d, ssm_norm_w, swa_sinks, w_out, g_pre_mix, g_post_mix, g_pre_ffn, g_post_ffn,
           w_ffn_in, w_ffn_out):
    bp, seq, d = x_prompt.shape
    bs, ts, _ = x_sample.shape
    depth = w_in.shape[0]
    lp = N_META + seq + FRONT_PAD
    assert lp % BLOCK == 0 and BLOCK % CHUNK == 0 and seq % BLOCK == 0
    nblk = lp // BLOCK
    blk0 = nblk - 1
    tm0 = _pick_tile(seq, TM_FIRST)
    assert tm0 >= BLOCK
    x_front = jnp.concatenate([jnp.zeros((FRONT_PAD, d), x_prompt.dtype), meta_tokens.astype(x_prompt.dtype),
                               jnp.zeros((tm0 - BLOCK, d), x_prompt.dtype)], axis=0)
    pad_first = ((seq, seq + FRONT_PAD), (lp, seq + tm0))
    pad_range = ((seq, seq + FRONT_PAD),)
    xp = None
    xs = x_sample.reshape(1, bs * ts, d)

    pos_p = jnp.concatenate([N_META + jnp.arange(seq, dtype=jnp.int32), jnp.zeros((FRONT_PAD,), jnp.int32),
                             jnp.arange(N_META, dtype=jnp.int32), jnp.zeros((tm0 - BLOCK,), jnp.int32)])
    cos_p, sin_p = _rope_tables(pos_p)
    cos_s, sin_s = _rope_tables(PAST_LEN + jnp.tile(jnp.arange(ts, dtype=jnp.int32), bs))

    gw = (SSM_HEADS // SSM_GROUPS) * SSM_HEADDIM
    nseq_s = _pick_tile(bs, 8) if bs % SUBLANES == 0 else bs
    zero_dn = jnp.zeros((1, bp, DN_HEADS, DN_DK, DN_DV), F32)
    zero_dnc = jnp.zeros((bp, SUBLANES, DN_CONV), F32)
    zero_ssm = jnp.zeros((1, bp, SSM_GROUPS, gw, SSM_STATE), F32)
    zero_ssmc = jnp.zeros((bp, SUBLANES, SSM_CONV), F32)
    state_ssm_g = state_ssm.reshape(depth, bs, SSM_GROUPS, gw, SSM_STATE)
    cache_k = cache_swa_k.reshape(depth, bs, WINDOW, SWA_KV)
    cache_v = cache_swa_v.reshape(depth, bs, WINDOW, SWA_KV)

    w_in_r = _reorder_w_in(w_in)
    swa0 = DN_V + SSM_INNER
    w_out_b = jnp.concatenate(
        [w_out[:, :swa0]] + [w_out[:, swa0 + h * SWA_HEAD_DIM:swa0 + (h + 1) * SWA_HEAD_DIM] for h in _swa_head_order()],
        axis=1).astype(BF16)
    w_fi_b = w_ffn_in.astype(BF16)
    w_fo_b = w_ffn_out.astype(BF16)
    g1, g2, g3, g4 = (a[:, None, :] for a in (g_pre_mix, g_post_mix, g_pre_ffn, g_post_ffn))
    dn_nw = dn_norm_w[:, None, :]
    ssm_nw = ssm_norm_w[:, None, :]

    new_p, new_s = [], []
    dn_s = ssm_s = k_s = v_s = None
    prm_all = _scalar_param_tiles([(SM_A, dn_a_log), (SM_A, dn_dt_bias), (SM_DT, ssm_a_log), (SM_DT, ssm_dt_bias),
                                   (0, swa_sinks)])
    drow_all = jnp.repeat(ssm_d, SSM_HEADDIM, axis=1)
    dn_cbuf_s = jnp.pad(state_dn_conv, ((0, 0), (0, 0), (SUBLANES - (CONV_WIDTH - 1), 0), (0, 0)))
    ssm_cbuf_s = jnp.pad(state_ssm_conv, ((0, 0), (0, 0), (SUBLANES - (CONV_WIDTH - 1), 0), (0, 0)))
    for l in range(depth):
        prm = prm_all[l]
        drow = drow_all[l][None, :]
        cbias = ssm_conv_b[l][None, :]
        last = l == depth - 1

        if l == 0:
            proj, xp = _inproj_first(x_prompt, x_front, g1, w_in_r, prm_all, cos_p, sin_p, l, pad_first, tm0)
        else:
            proj = _inproj(xp, g1, w_in_r, prm_all, cos_p[:lp], sin_p[:lp], l, pad_range, TM_DENSE)
        (odn, dn_p, dnc_p), = _run_parts([_dn_mixer(proj, nblk, blk0, BLOCK, CHUNK, bp, zero_dn, 0,
                                                    zero_dnc, dn_conv_w[l])], "dn_mixer")
        (ys, ssm_p, ssmc_p), = _run_parts([_ssd_mixer(proj, nblk, blk0, BLOCK, CHUNK, bp, zero_ssm, 0,
                                                      zero_ssmc, ssm_conv_w[l], cbias, prm, drow)], "ssd_mixer")
        (osw, k_p, v_p), = _run_parts([_swa_prompt(proj, nblk, blk0, FRONT_PAD, prm, out_dtype=BF16)], "swa_prompt")
        xp = _tail(xp, proj, odn, ys, osw, dn_nw, ssm_nw, w_out_b, g2, g3, g4, w_fi_b, w_fo_b, l,
                   seq if last else lp, TM_DENSE)
        new_p.append((dn_p, dnc_p[:, -(CONV_WIDTH - 1):], ssm_p.reshape(bp, SSM_HEADS, SSM_HEADDIM, SSM_STATE),
                      ssmc_p[:, -(CONV_WIDTH - 1):], k_p.reshape(bp, WINDOW, SWA_KV_HEADS, SWA_HEAD_DIM),
                      v_p.reshape(bp, WINDOW, SWA_KV_HEADS, SWA_HEAD_DIM)))

        proj_s = _inproj(xs, g1, w_in_r, prm_all, cos_s, sin_s, l, None, TM_DENSE)
        proj = proj_s.reshape(bs, ts, D_PROJ)
        (odn, dn_s, dnc_s), (ys, ssm_s, ssmc_s), (osw, k_s, v_s) = _run_parts([
            _dn_mixer(proj, 1, 0, ts, ts, nseq_s, state_dn, l, dn_cbuf_s[l], dn_conv_w[l], stack=(depth, dn_s)),
            _ssd_mixer(proj, 1, 0, ts, ts, nseq_s, state_ssm_g, l, ssm_cbuf_s[l], ssm_conv_w[l], cbias, prm, drow,
                       stack=(depth, ssm_s)),
            _swa_sample(proj, nseq_s, cache_k, cache_v, l, prm,
                        stack_k=(depth, k_s), stack_v=(depth, v_s))], "mixers_sample")
        flat = lambda a: a.reshape(1, bs * ts, a.shape[-1])
        xs = _tail(xs, proj_s, flat(odn), flat(ys), flat(osw), dn_nw, ssm_nw, w_out_b, g2, g3, g4, w_fi_b, w_fo_b, l,
                   bs * ts, TM_DENSE)
        new_s.append((dnc_s[:, -(CONV_WIDTH - 1):], ssmc_s[:, -(CONV_WIDTH - 1):]))

    outs_p = tuple(jnp.stack([st[i] for st in new_p]) for i in range(6))
    dnc_s, ssmc_s = (jnp.stack([st[i] for st in new_s]) for i in range(2))
    outs_s = (dn_s, dnc_s, ssm_s.reshape(depth, bs, SSM_HEADS, SSM_HEADDIM, SSM_STATE), ssmc_s,
              k_s.reshape(depth, bs, WINDOW, SWA_KV_HEADS, SWA_HEAD_DIM),
              v_s.reshape(depth, bs, WINDOW, SWA_KV_HEADS, SWA_HEAD_DIM))
    return (xp, xs.reshape(bs, ts, d)) + outs_p + outs_s
```

```python
import functools

import jax
import jax.numpy as jnp
from jax import lax
from jax.experimental import pallas as pl
from jax.experimental.pallas import tpu as pltpu

F32 = jnp.float32
BF16 = jnp.bfloat16
NT = (((1,), (1,)), ((), ()))
TN = (((0,), (0,)), ((), ()))

N_META = 16
CONV_WIDTH = 4
CHUNK = 64
BLOCK = 128
WINDOW = 128
FRONT_PAD = BLOCK - N_META
ROPE_THETA = 10000.0
PAST_LEN = 8192
EPS = 1e-6

DN_HEADS, DN_DK, DN_DV = 4, 128, 128
DN_QK = DN_HEADS * DN_DK
DN_V = DN_HEADS * DN_DV
DN_CONV = 2 * DN_QK + DN_V
SSM_HEADS, SSM_HEADDIM, SSM_GROUPS, SSM_STATE = 4, 64, 2, 128
SSM_INNER = SSM_HEADS * SSM_HEADDIM
SSM_BC = SSM_GROUPS * SSM_STATE
SSM_CONV = SSM_INNER + 2 * SSM_BC
SWA_Q_HEADS, SWA_KV_HEADS, SWA_HEAD_DIM = 4, 2, 64
SWA_Q = SWA_Q_HEADS * SWA_HEAD_DIM
SWA_KV = SWA_KV_HEADS * SWA_HEAD_DIM
IN_WIDTHS = (DN_CONV, DN_V, DN_HEADS, DN_HEADS, SSM_CONV, SSM_INNER, SSM_HEADS, SWA_Q, SWA_KV, SWA_KV)

LANES = 128
SUBLANES = 8
COL_QKV = (0, DN_CONV)
COL_DNZ = (COL_QKV[0] + DN_CONV, DN_V)
COL_SSZ = (COL_DNZ[0] + DN_V, SSM_INNER)
COL_XBC = (COL_SSZ[0] + SSM_INNER, SSM_CONV)
COL_SWQ = (COL_XBC[0] + SSM_CONV, SWA_Q)
COL_SWK = (COL_SWQ[0] + SWA_Q, SWA_KV)
COL_SWV = (COL_SWK[0] + SWA_KV, SWA_KV)
COL_SM = (COL_SWV[0] + SWA_KV, LANES)
D_PROJ = COL_SM[0] + LANES
assert all(off % width == 0 for off, width in (COL_QKV, COL_DNZ, COL_SSZ, COL_XBC, COL_SWQ, COL_SWK, COL_SWV, COL_SM))
SM_B, SM_A, SM_DT = 0, DN_HEADS, 2 * DN_HEADS
PRM_DN_ALOG, PRM_DN_DTB, PRM_SSM_ALOG, PRM_SSM_DTB, PRM_SINK = 0, 1, 2, 3, 4
NEG_BIG = -1e30
VMEM_LIMIT = 56 * 1024 * 1024
TM_DENSE = 640
TM_FIRST = 512


def _bdot(a, b, dims=None):
    a = a.astype(BF16)
    b = b.astype(BF16)
    if dims is None:
        return jnp.dot(a, b, preferred_element_type=F32)
    return lax.dot_general(a, b, dims, preferred_element_type=F32)


def _cumsum_rows(lmat, g):
    hi = g.astype(BF16)
    r1 = g - hi.astype(F32)
    mid = r1.astype(BF16)
    lo = (r1 - mid.astype(F32)).astype(BF16)
    dot = lambda part: jnp.dot(lmat, part, preferred_element_type=F32)
    return dot(hi) + dot(mid) + dot(lo)


def _rmsnorm(x, g):
    return x * lax.rsqrt(jnp.mean(x * x, axis=-1, keepdims=True) + EPS) * g


def _l2norm(x):
    return x * lax.rsqrt(jnp.sum(x * x, axis=-1, keepdims=True) + EPS)


def _sigmoid(x):
    return 0.5 * jnp.tanh(0.5 * x) + 0.5


def _silu(x):
    half = 0.5 * x
    return half * jnp.tanh(half) + half


def _softplus(x):
    return jnp.maximum(x, 0.0) + jnp.log1p(jnp.exp(-jnp.abs(x)))


def _pick_tile(n, target):
    best = None
    for t in range(SUBLANES, min(n, target) + 1, SUBLANES):
        if n % t == 0:
            best = t
    assert best is not None, n
    return best


def _project_rows(x, g, w, tm, pad_range):
    h = _rmsnorm(x, g)
    if pad_range:
        r = pl.program_id(1) * tm + lax.broadcasted_iota(jnp.int32, (tm, 1), 0)
        is_pad = None
        for lo, hi in pad_range:
            hit = (r >= lo) & (r < hi)
            is_pad = hit if is_pad is None else is_pad | hit
        h = jnp.where(is_pad, 0.0, h)
    return jnp.dot(h.astype(BF16), w, preferred_element_type=F32)


def _inproj_kernel(x_ref, g_ref, w_ref, o_ref, *, tm, pad_range):
    o_ref[...] = _project_rows(x_ref[...], g_ref[...], w_ref[...], tm, pad_range)


def _inproj_first_kernel(xm_ref, xt_ref, g_ref, w_ref, o_ref, xo_ref, *, tm, n_main, pad_range):
    x = jnp.where(pl.program_id(1) == n_main, xt_ref[...], xm_ref[...])
    xo_ref[...] = x
    o_ref[...] = _project_rows(x, g_ref[...], w_ref[...], tm, pad_range)


def _inproj_first(x_main, x_tail, g, w, layer, pad_range, tm):
    nb, rows, d = x_main.shape
    assert rows % tm == 0 and x_tail.shape == (tm, d)
    n_main = rows // tm
    return pl.pallas_call(
        functools.partial(_inproj_first_kernel, tm=tm, n_main=n_main, pad_range=pad_range),
        out_shape=(jax.ShapeDtypeStruct((nb, rows + tm, D_PROJ), F32), jax.ShapeDtypeStruct((nb, rows + tm, d), F32)),
        grid=(nb, n_main + 1),
        in_specs=[pl.BlockSpec((None, tm, d), lambda b, i: (b, jnp.minimum(i, n_main - 1), 0)),
                  pl.BlockSpec((tm, d), lambda b, i: (0, 0)),
                  _layer_spec(g.shape, layer), _layer_spec(w.shape, layer)],
        out_specs=(pl.BlockSpec((None, tm, D_PROJ), lambda b, i: (b, i, 0)),
                   pl.BlockSpec((None, tm, d), lambda b, i: (b, i, 0))),
        compiler_params=pltpu.CompilerParams(dimension_semantics=("arbitrary", "arbitrary"),
                                             vmem_limit_bytes=VMEM_LIMIT),
        name="inproj_first",
    )(x_main, x_tail, g, w)


def _inproj(x, g, w, layer, pad_range, tm_target):
    nb, rows, d = x.shape
    tm = _pick_tile(rows, tm_target)
    return pl.pallas_call(
        functools.partial(_inproj_kernel, tm=tm, pad_range=pad_range),
        out_shape=jax.ShapeDtypeStruct((nb, rows, D_PROJ), F32),
        grid=(nb, rows // tm),
        in_specs=[pl.BlockSpec((None, tm, d), lambda b, i: (b, i, 0)), _layer_spec(g.shape, layer),
                  _layer_spec(w.shape, layer)],
        out_specs=pl.BlockSpec((None, tm, D_PROJ), lambda b, i: (b, i, 0)),
        compiler_params=pltpu.CompilerParams(dimension_semantics=("arbitrary", "arbitrary"),
                                             vmem_limit_bytes=VMEM_LIMIT),
        name="inproj",
    )(x, g, w)


def _causal_conv(xbuf, raw_ref, cw_ref, cbo_ref, rows):
    x = raw_ref[...]
    prev = xbuf[0:SUBLANES, :]
    cw = cw_ref[...]
    row = lax.broadcasted_iota(jnp.int32, (SUBLANES, 1), 0)
    acc = None
    for i in range(CONV_WIDTH):
        s = CONV_WIDTH - 1 - i
        if s == 0:
            xs = x
        else:
            r = pltpu.roll(x, s, axis=0)
            head = jnp.where(row < s, pltpu.roll(prev, s, axis=0), r[0:SUBLANES])
            xs = head if rows == SUBLANES else jnp.concatenate([head, r[SUBLANES:]], axis=0)
        term = xs * cw[i:i + 1, :]
        acc = term if acc is None else acc + term
    tail = x[rows - SUBLANES:rows]
    cbo_ref[...] = tail
    xbuf[0:SUBLANES, :] = tail
    return acc


def _causal_mask(c):
    return lax.broadcasted_iota(jnp.int32, (c, c), 0) >= lax.broadcasted_iota(jnp.int32, (c, c), 1)


def _segment_decay(gc, gct, lane, ge):
    col = gc[:, lane:lane + 1]
    row = gct[lane:lane + 1, :]
    return jnp.where(ge, jnp.exp(jnp.where(ge, col - row, 0.0)), 0.0)


def _inv_unit_lower_minus_eye(a_list, c, nh):
    w = nh * c
    blk_r = lax.broadcasted_iota(jnp.int32, (w, w), 0) // c
    blk_c = lax.broadcasted_iota(jnp.int32, (w, w), 1) // c
    same = blk_r == blk_c

    def block_diag(p):
        return jnp.where(same, jnp.concatenate([p] * nh, axis=0), 0.0).astype(BF16)

    ys = [-a for a in a_list]
    ps = [_bdot(a, block_diag(a)) for a in a_list]
    yield
    n = 2
    while n < c:
        pbds = [block_diag(p) for p in ps]
        n *= 2
        if n < c:
            sts = [_bdot(jnp.concatenate([y, p], axis=0), pbd) for y, p, pbd in zip(ys, ps, pbds)]
            ys = [y + p + st[:c] for y, p, st in zip(ys, ps, sts)]
            ps = [st[c:] for st in sts]
        else:
            ys = [y + p + _bdot(y, pbd) for y, p, pbd in zip(ys, ps, pbds)]
        yield
    return ys


def _run_parts(parts, name):
    grid = parts[0]["grid"]
    assert all(p["grid"] == grid for p in parts)
    n_in = [len(p["inputs"]) for p in parts]
    n_out = [len(p["out_shape"]) for p in parts]
    n_scr = [len(p["scratch"]) for p in parts]
    aliases = {}
    for k, p in enumerate(parts):
        for i, o in p["aliases"].items():
            aliases[sum(n_in[:k]) + i] = sum(n_out[:k]) + o

    def kernel(*refs):
        ins = refs[:sum(n_in)]
        outs = refs[sum(n_in):sum(n_in) + sum(n_out)]
        scr = refs[sum(n_in) + sum(n_out):]
        gens = [p["body"](ins[sum(n_in[:k]):sum(n_in[:k + 1])], outs[sum(n_out[:k]):sum(n_out[:k + 1])],
                          scr[sum(n_scr[:k]):sum(n_scr[:k + 1])]) for k, p in enumerate(parts)]
        for tag in gens[0]:
            if tag == "chain":
                break
        live = list(gens)
        while live:
            for g in list(live):
                if next(g, StopIteration) is StopIteration:
                    live.remove(g)

    results = pl.pallas_call(
        kernel,
        out_shape=tuple(s for p in parts for s in p["out_shape"]),
        grid=grid,
        in_specs=[s for p in parts for s in p["in_specs"]],
        out_specs=tuple(s for p in parts for s in p["out_specs"]),
        scratch_shapes=[s for p in parts for s in p["scratch"]],
        input_output_aliases=aliases,
        compiler_params=pltpu.CompilerParams(dimension_semantics=("arbitrary",) * len(grid),
                                             vmem_limit_bytes=VMEM_LIMIT),
        name=name,
    )(*[a for p in parts for a in p["inputs"]])
    return [list(results[sum(n_out[:k]):sum(n_out[:k + 1])]) for k in range(len(parts))]


def _state_out(stack, layer, nb, nseq, tail, out_index, n_inputs):
    zeros = (0,) * len(tail)
    if stack is None:
        return ((nb,) + tail, pl.BlockSpec((nseq,) + tail, lambda b, *_: (b,) + zeros), [], [], {}, None, 0)
    depth, prev = stack
    shape = (depth, nb) + tail
    if prev is None:
        spec = pl.BlockSpec((depth, nseq) + tail, lambda b, *_: (0, b) + zeros)
        return (shape, spec, [], [], {}, layer, depth)
    spec = pl.BlockSpec((None, nseq) + tail, lambda b, *_: (layer, b) + zeros)
    return (shape, spec, [prev], [pl.BlockSpec(memory_space=pl.ANY)], {n_inputs: out_index}, None, 0)


def _state_view(ref, slot, n_slots):
    if slot is None:
        return ref
    for other in range(n_slots):
        if other != slot:
            ref[other] = jnp.zeros(ref.shape[1:], ref.dtype)
    return ref.at[slot]


def _dn_prep(items, lmat, ge, gt_all, chunk):
    heads = range(DN_HEADS)
    gcs = [_cumsum_rows(lmat, g_all) for _, _, g_all in items]
    gcts = [gc.T for gc in gcs]
    qs = [[_l2norm(qkv[:, h * DN_DK:(h + 1) * DN_DK]) * DN_DK ** -0.5 for h in heads] for qkv, _, _ in items]
    ks = [[_l2norm(qkv[:, DN_QK + h * DN_DK:DN_QK + (h + 1) * DN_DK]) for h in heads] for qkv, _, _ in items]
    vs = [[qkv[:, 2 * DN_QK + h * DN_DV:2 * DN_QK + (h + 1) * DN_DV] for h in heads] for qkv, _, _ in items]
    betas = [[beta_all[:, SM_B + h:SM_B + h + 1] for h in heads] for _, beta_all, _ in items]
    yield
    kbs = [[k.astype(BF16) for k in kk] for kk in ks]
    kks = [jnp.concatenate([_bdot(kb, kb, NT) for kb in kb4], axis=1) for kb4 in kbs]
    qks = [jnp.concatenate([_bdot(q, kb, NT) for q, kb in zip(q4, kb4)], axis=1) for q4, kb4 in zip(qs, kbs)]
    decs = [jnp.concatenate([_segment_decay(gc, gct, SM_A + h, ge) for h in heads], axis=1)
            for gc, gct in zip(gcs, gcts)]
    beta_ws = [jnp.concatenate([jnp.broadcast_to(b, (chunk, chunk)) for b in b4], axis=1) for b4 in betas]
    a_list = [jnp.where(gt_all, bw * kk * dec, 0.0) for bw, kk, dec in zip(beta_ws, kks, decs)]
    yield "chain"
    n = len(items)
    egs, ekds, egls, attns, rhss, qds, kds = ([None] * n for _ in range(7))

    def side_work(i):
        gc = gcs[i]
        glast = gc[chunk - 1:chunk, :]
        egs[i] = jnp.exp(gc)
        ekds[i] = jnp.exp(glast - gc)
        egls[i] = jnp.exp(glast)
        attns[i] = (qks[i] * decs[i]).astype(BF16)
        lane = lambda a, h: a[:, SM_A + h:SM_A + h + 1]
        rhss[i] = [jnp.concatenate([vs[i][h] * betas[i][h], ks[i][h] * (betas[i][h] * lane(egs[i], h))], axis=1)
                   for h in heads]
        qds[i] = [(qs[i][h] * lane(egs[i], h)).astype(BF16) for h in heads]
        kds[i] = [(ks[i][h] * lane(ekds[i], h)).astype(BF16) for h in heads]

    inverse = _inv_unit_lower_minus_eye(a_list, chunk, DN_HEADS)
    todo = list(range(n))
    while True:
        try:
            next(inverse)
        except StopIteration as done:
            tms = done.value
            break
        if todo:
            side_work(todo.pop(0))
        yield
    for i in todo:
        side_work(i)
    uws = [[rhss[i][h] + _bdot(tms[i][:, h * chunk:(h + 1) * chunk], rhss[i][h]) for h in heads] for i in range(n)]
    yield
    out = []
    for i in range(n):
        per_head = []
        for h in heads:
            uw = uws[i][h]
            wq = jnp.concatenate([uw[:, DN_DV:].astype(BF16), qds[i][h]], axis=0)
            per_head.append((uw[:, :DN_DV], wq, attns[i][:, h * chunk:(h + 1) * chunk], kds[i][h],
                             egls[i][:, SM_A + h:SM_A + h + 1]))
        out.append(per_head)
    return out


def _dn_body(ins, outs, scratch, *, nseq, rows, chunk, front_pad, slot, n_slots):
    qkv_ref, sm_ref, s0_ref, cb_ref, cw_ref, prm_ref = ins[:6]
    o_ref, so_full, cbo_ref = outs
    xbuf, = scratch
    so_ref = _state_view(so_full, slot, n_slots)
    blk = pl.program_id(1)
    first = blk == 0

    @pl.when(first)
    def _():
        so_ref[...] = s0_ref[...]
        xbuf[:, 0:SUBLANES, :] = cb_ref[...]

    prm = prm_ref[...]
    ge = _causal_mask(chunk)
    lmat = jnp.where(ge, 1.0, 0.0).astype(BF16)
    wide = (chunk, DN_HEADS * chunk)
    gt_all = lax.broadcasted_iota(jnp.int32, wide, 0) > lax.broadcasted_iota(jnp.int32, wide, 1) % chunk
    nchunk = rows // chunk

    items = []
    for s in range(nseq):
        qkv = _silu(_causal_conv(xbuf.at[s], qkv_ref.at[s], cw_ref, cbo_ref.at[s], rows))
        sm = sm_ref[s]
        beta_all = _sigmoid(sm)
        g_all = -jnp.exp(prm[PRM_DN_ALOG:PRM_DN_ALOG + 1, :]) * _softplus(sm + prm[PRM_DN_DTB:PRM_DN_DTB + 1, :])
        if front_pad:
            pos = blk * rows + lax.broadcasted_iota(jnp.int32, (rows, 1), 0)
            g_all = jnp.where(pos < front_pad, 0.0, g_all)
        for c in range(nchunk):
            cs = slice(c * chunk, (c + 1) * chunk)
            items.append((qkv[cs], beta_all[cs], g_all[cs]))
        yield
    prep = yield from _dn_prep(items, lmat, ge, gt_all, chunk)

    chains = [(s, h) for s in range(nseq) for h in range(DN_HEADS)]
    states = [so_ref[s, h] for s, h in chains]
    for c in range(nchunk):
        r0 = c * chunk
        fac = [prep[s * nchunk + c][h] for s, h in chains]
        m1s = [_bdot(f[1], st) for f, st in zip(fac, states)]
        yield
        v_news = [(f[0] - m1[:chunk]).astype(BF16) for f, m1 in zip(fac, m1s)]
        ups = [_bdot(f[3], v, TN) for f, v in zip(fac, v_news)]
        os_ = [m1[chunk:] + _bdot(f[2], v) for f, m1, v in zip(fac, m1s, v_news)]
        yield
        states = [st * f[4] + up for f, st, up in zip(fac, states, ups)]
        for (s, h), o in zip(chains, os_):
            lo = h * DN_DV
            o_ref[s, r0:r0 + chunk, lo:lo + DN_DV] = o
        yield
    for (s, h), st in zip(chains, states):
        so_ref[s, h] = st


def _dn_mixer(proj, nblk, blk0, rows, chunk, front_pad, nseq, s0, layer, cbuf, cw, prm, stack=None):
    nb = proj.shape[0]
    col = lambda c: c[0] // c[1]
    phys = lambda i: (i + blk0) % nblk
    so_shape, so_spec, extra, extra_specs, aliases, slot, n_slots = _state_out(
        stack, layer, nb, nseq, (DN_HEADS, DN_DK, DN_DV), 1, 6)
    return dict(
        body=functools.partial(_dn_body, nseq=nseq, rows=rows, chunk=chunk, front_pad=front_pad,
                               slot=slot, n_slots=n_slots),
        grid=(nb // nseq, nblk),
        inputs=[proj, proj, s0, cbuf, cw, prm] + extra,
        out_shape=[jax.ShapeDtypeStruct((nb, nblk * rows, DN_V), F32),
                   jax.ShapeDtypeStruct(so_shape, F32),
                   jax.ShapeDtypeStruct((nb, SUBLANES, DN_CONV), F32)],
        in_specs=[
            pl.BlockSpec((nseq, rows, DN_CONV), lambda b, i: (b, phys(i), col(COL_QKV))),
            pl.BlockSpec((nseq, rows, LANES), lambda b, i: (b, phys(i), col(COL_SM))),
            pl.BlockSpec((None, nseq, DN_HEADS, DN_DK, DN_DV), lambda b, i: (layer, b, 0, 0, 0)),
            pl.BlockSpec((nseq, SUBLANES, DN_CONV), lambda b, i: (b, 0, 0)),
            pl.BlockSpec((CONV_WIDTH, DN_CONV), lambda b, i: (0, 0)),
            pl.BlockSpec((SUBLANES, LANES), lambda b, i: (0, 0)),
        ] + extra_specs,
        out_specs=[
            pl.BlockSpec((nseq, rows, DN_V), lambda b, i: (b, phys(i), 0)),
            so_spec,
            pl.BlockSpec((nseq, SUBLANES, DN_CONV), lambda b, i: (b, 0, 0)),
        ],
        scratch=[pltpu.VMEM((nseq, SUBLANES, DN_CONV), F32)],
        aliases=aliases,
    )


def _ssd_body(ins, outs, scratch, *, nseq, rows, chunk, front_pad, slot, n_slots):
    xbc_ref, sm_ref, h0_ref, cb_ref, cw_ref, cbias_ref, prm_ref, drow_ref = ins[:8]
    y_ref, ho_full, cbo_ref = outs
    xbuf, = scratch
    ho_ref = _state_view(ho_full, slot, n_slots)
    blk = pl.program_id(1)

    @pl.when(blk == 0)
    def _():
        ho_ref[...] = h0_ref[...]
        xbuf[:, 0:SUBLANES, :] = cb_ref[...]

    prm = prm_ref[...]
    drow = drow_ref[...]
    cbias = cbias_ref[...]
    ge = _causal_mask(chunk)
    lmat = jnp.where(ge, 1.0, 0.0).astype(BF16)
    hpg = SSM_HEADS // SSM_GROUPS
    gw = hpg * SSM_HEADDIM
    lane = lax.broadcasted_iota(jnp.int32, (1, gw), 1)
    srow = lax.broadcasted_iota(jnp.int32, (gw, 1), 0)
    in_head = [(lane >= j * SSM_HEADDIM) & (lane < (j + 1) * SSM_HEADDIM) for j in range(hpg)]
    nchunk = rows // chunk
    groups = range(SSM_GROUPS)

    items = []
    for s in range(nseq):
        act = _silu(_causal_conv(xbuf.at[s], xbc_ref.at[s], cw_ref, cbo_ref.at[s], rows) + cbias)
        dt_all = _softplus(sm_ref[s] + prm[PRM_SSM_DTB:PRM_SSM_DTB + 1, :])
        if front_pad:
            pos = blk * rows + lax.broadcasted_iota(jnp.int32, (rows, 1), 0)
            dt_all = jnp.where(pos < front_pad, 0.0, dt_all)
        g_all = dt_all * (-jnp.exp(prm[PRM_SSM_ALOG:PRM_SSM_ALOG + 1, :]))
        for c in range(nchunk):
            cs = slice(c * chunk, (c + 1) * chunk)
            items.append((act[cs], dt_all[cs], g_all[cs]))
        yield
    n_items = len(items)
    gcs = [_cumsum_rows(lmat, g) for _, _, g in items]
    gcts = [gc.T for gc in gcs]
    xgs = [[a[:, g * gw:(g + 1) * gw] for g in groups] for a, _, _ in items]
    bgs = [[a[:, SSM_INNER + g * SSM_STATE:SSM_INNER + (g + 1) * SSM_STATE] for g in groups] for a, _, _ in items]
    cgs = [[a[:, SSM_INNER + SSM_BC + g * SSM_STATE:SSM_INNER + SSM_BC + (g + 1) * SSM_STATE] for g in groups]
           for a, _, _ in items]
    cbs = [[_bdot(cgs[i][g], bgs[i][g], NT) for g in groups] for i in range(n_items)]
    yield
    egs = [jnp.exp(gc) for gc in gcs]
    ekds = [jnp.exp(gc[chunk - 1:chunk, :] - gc) for gc in gcs]
    egls = [jnp.exp(gc[chunk - 1:chunk, :]) for gc in gcs]
    heads = [(g, j) for g in groups for j in range(hpg)]
    ln = lambda g, j: SM_DT + g * hpg + j
    xdts = [[jnp.where(in_head[j], xgs[i][g] * items[i][1][:, ln(g, j):ln(g, j) + 1], 0.0).astype(BF16)
             for g, j in heads] for i in range(n_items)]
    attns = [[cbs[i][g] * _segment_decay(gcs[i], gcts[i], ln(g, j), ge) for g, j in heads] for i in range(n_items)]
    y_intras = [[_bdot(attns[i][k], xdts[i][k]) for k in range(len(heads))] for i in range(n_items)]
    yield
    upds = [[_bdot(xdts[i][k], bgs[i][g] * ekds[i][:, ln(g, j):ln(g, j) + 1], TN) for k, (g, j) in enumerate(heads)]
            for i in range(n_items)]
    yield
    cds = [[jnp.concatenate([cgs[i][g] * egs[i][:, ln(g, j):ln(g, j) + 1] for j in range(hpg)], axis=0).astype(BF16)
            for g in groups] for i in range(n_items)]
    y_loc = [[sum(y_intras[i][g * hpg + j] for j in range(hpg)) for g in groups] for i in range(n_items)]
    h_inc = [[sum(upds[i][g * hpg + j] for j in range(hpg)) for g in groups] for i in range(n_items)]
    gl_cols = []
    for i in range(n_items):
        per_group = []
        for g in groups:
            gl = egls[i][:, ln(g, 0):ln(g, 0) + 1]
            for j in range(1, hpg):
                gl = jnp.where(srow < j * SSM_HEADDIM, gl, egls[i][:, ln(g, j):ln(g, j) + 1])
            per_group.append(gl)
        gl_cols.append(per_group)

    chains = [(s, g) for s in range(nseq) for g in groups]
    states = [ho_ref[s, g] for s, g in chains]
    for c in range(nchunk):
        r0 = c * chunk
        idx = [s * nchunk + c for s, _ in chains]
        yis = [_bdot(cds[i][g], st, NT) for i, (_, g), st in zip(idx, chains, states)]
        yield
        states = [st * gl_cols[i][g] + h_inc[i][g] for i, (_, g), st in zip(idx, chains, states)]
        for i, (s, g), yi in zip(idx, chains, yis):
            y_inter = yi[0:chunk]
            for j in range(1, hpg):
                y_inter = jnp.where(in_head[j], yi[j * chunk:(j + 1) * chunk], y_inter)
            y_ref[s, r0:r0 + chunk, g * gw:(g + 1) * gw] = (y_loc[i][g] + y_inter
                                                           + xgs[i][g] * drow[:, g * gw:(g + 1) * gw])
        yield
    for (s, g), st in zip(chains, states):
        ho_ref[s, g] = st


def _ssd_mixer(proj, nblk, blk0, rows, chunk, front_pad, nseq, h0, layer, cbuf, cw, cbias, prm, drow, stack=None):
    nb = proj.shape[0]
    col = lambda c: c[0] // c[1]
    phys = lambda i: (i + blk0) % nblk
    gw = (SSM_HEADS // SSM_GROUPS) * SSM_HEADDIM
    ho_shape, ho_spec, extra, extra_specs, aliases, slot, n_slots = _state_out(
        stack, layer, nb, nseq, (SSM_GROUPS, gw, SSM_STATE), 1, 8)
    return dict(
        body=functools.partial(_ssd_body, nseq=nseq, rows=rows, chunk=chunk, front_pad=front_pad,
                               slot=slot, n_slots=n_slots),
        grid=(nb // nseq, nblk),
        inputs=[proj, proj, h0, cbuf, cw, cbias, prm, drow] + extra,
        out_shape=[jax.ShapeDtypeStruct((nb, nblk * rows, SSM_INNER), F32),
                   jax.ShapeDtypeStruct(ho_shape, F32),
                   jax.ShapeDtypeStruct((nb, SUBLANES, SSM_CONV), F32)],
        in_specs=[
            pl.BlockSpec((nseq, rows, SSM_CONV), lambda b, i: (b, phys(i), col(COL_XBC))),
            pl.BlockSpec((nseq, rows, LANES), lambda b, i: (b, phys(i), col(COL_SM))),
            pl.BlockSpec((None, nseq, SSM_GROUPS, gw, SSM_STATE), lambda b, i: (layer, b, 0, 0, 0)),
            pl.BlockSpec((nseq, SUBLANES, SSM_CONV), lambda b, i: (b, 0, 0)),
            pl.BlockSpec((CONV_WIDTH, SSM_CONV), lambda b, i: (0, 0)),
            pl.BlockSpec((1, SSM_CONV), lambda b, i: (0, 0)),
            pl.BlockSpec((SUBLANES, LANES), lambda b, i: (0, 0)),
            pl.BlockSpec((1, SSM_INNER), lambda b, i: (0, 0)),
        ] + extra_specs,
        out_specs=[
            pl.BlockSpec((nseq, rows, SSM_INNER), lambda b, i: (b, phys(i), 0)),
            ho_spec,
            pl.BlockSpec((nseq, SUBLANES, SSM_CONV), lambda b, i: (b, 0, 0)),
        ],
        scratch=[pltpu.VMEM((nseq, SUBLANES, SSM_CONV), F32)],
        aliases=aliases,
    )


def _rope(x, cos, sin_signed):
    w = x.shape[-1]
    half = SWA_HEAD_DIM // 2
    lane = lax.broadcasted_iota(jnp.int32, (1, w), 1)
    first_half = (lane % SWA_HEAD_DIM) < half
    swapped = jnp.where(first_half, pltpu.roll(x, w - half, axis=1), pltpu.roll(x, half, axis=1))
    return x * cos + swapped * sin_signed


def _sink_attend(problems):
    scale = SWA_HEAD_DIM ** -0.5
    scores = [[jnp.where(m, _bdot(q, k, NT) * scale, NEG_BIG) for k, m in zip(keys, masks)]
              for q, keys, _, masks, _ in problems]
    yield
    outs = []
    probs, dens = [], []
    def lane_reduce(tiles, combine, reduce):
        merged = {}
        for t in tiles:
            merged[t.shape[-1]] = t if t.shape[-1] not in merged else combine(merged[t.shape[-1]], t)
        return [reduce(t, axis=-1, keepdims=True) for t in merged.values()]

    for (_, _, _, _, sink), ss in zip(problems, scores):
        mx = sink
        for m in lane_reduce(ss, jnp.maximum, jnp.max):
            mx = jnp.maximum(mx, m)
        probs.append([jnp.exp(s - mx) for s in ss])
        dens.append(jnp.exp(sink - mx))
    yield
    pvs = [[_bdot(p, v) for p, v in zip(ps, vals)] for (_, _, vals, _, _), ps in zip(problems, probs)]
    yield
    for pv, sink_term in zip(pvs, dens):
        acc = pv[0]
        for extra in pv[1:]:
            acc = acc + extra
        outs.append(acc / (pltpu.roll(acc, SWA_HEAD_DIM, axis=1) + sink_term))
    return outs


def _swa_head_order():
    grp = SWA_Q_HEADS // SWA_KV_HEADS
    assert SWA_KV_HEADS * SWA_HEAD_DIM == LANES
    return [j * grp + t for t in range(grp) for j in range(SWA_KV_HEADS)]


def _swa_problems(q, key_sets, val_sets, masks, prm, tq):
    grp = SWA_Q_HEADS // SWA_KV_HEADS
    row = lax.broadcasted_iota(jnp.int32, (grp * tq, 1), 0)
    lane = lax.broadcasted_iota(jnp.int32, (1, LANES), 1)
    qst = jnp.concatenate([q[:, t * LANES:(t + 1) * LANES] for t in range(grp)], axis=0).astype(BF16)
    problems = []
    for j in range(SWA_KV_HEADS):
        half = (lane >= j * SWA_HEAD_DIM) & (lane < (j + 1) * SWA_HEAD_DIM)
        sink = prm[PRM_SINK:PRM_SINK + 1, j * grp:j * grp + 1]
        for t in range(1, grp):
            sink = jnp.where(row < t * tq, sink, prm[PRM_SINK:PRM_SINK + 1, j * grp + t:j * grp + t + 1])
        problems.append((qst, [jnp.where(half, k, 0.0) for k in key_sets],
                         [jnp.where(half, v, 1.0) for v in val_sets], masks, sink))
    return problems


def _swa_tiles(outs, tq):
    grp = SWA_Q_HEADS // SWA_KV_HEADS
    lane = lax.broadcasted_iota(jnp.int32, (1, LANES), 1)
    tiles = []
    for t in range(grp):
        tile = outs[0][t * tq:(t + 1) * tq]
        for j in range(1, SWA_KV_HEADS):
            tile = jnp.where(lane < j * SWA_HEAD_DIM, tile, outs[j][t * tq:(t + 1) * tq])
        tiles.append(tile)
    return tiles


def _swa_prompt_body(ins, outs, scratch, *, nseq, front_pad):
    q_ref, k_ref, v_ref, cos_ref, sin_ref, prm_ref = ins
    o_ref, ko_ref, vo_ref = outs
    kprev, vprev = scratch
    blk = pl.program_id(1)

    @pl.when(blk == 0)
    def _():
        kprev[...] = jnp.zeros_like(kprev)
        vprev[...] = jnp.zeros_like(vprev)

    cos = cos_ref[...]
    sin = sin_ref[...]
    cos_q = jnp.concatenate([cos, cos], axis=1)
    sin_q = jnp.concatenate([sin, sin], axis=1)
    grp = SWA_Q_HEADS // SWA_KV_HEADS
    qi = lax.broadcasted_iota(jnp.int32, (grp * BLOCK, BLOCK), 0) % BLOCK
    kj = lax.broadcasted_iota(jnp.int32, (grp * BLOCK, BLOCK), 1)
    mask_cur = (kj <= qi) & (blk * BLOCK + kj >= front_pad)
    mask_prev = (kj > qi) & ((blk - 1) * BLOCK + kj >= front_pad)
    prm = prm_ref[...]
    problems = []
    for s in range(nseq):
        q = _rope(q_ref[s], cos_q, sin_q)
        k = _rope(k_ref[s], cos, sin)
        v = v_ref[s]
        problems += _swa_problems(q, (kprev[s], k), (vprev[s], v), (mask_prev, mask_cur), prm, BLOCK)
        kprev[s] = k
        vprev[s] = v
        ko_ref[s] = k
        vo_ref[s] = v
        yield
    res = yield from _sink_attend(problems)
    for s in range(nseq):
        for t, tile in enumerate(_swa_tiles(res[s * SWA_KV_HEADS:(s + 1) * SWA_KV_HEADS], BLOCK)):
            o_ref[s, :, t * LANES:(t + 1) * LANES] = tile.astype(o_ref.dtype)


def _swa_prompt(proj, nblk, blk0, front_pad, cos, sin, prm, out_dtype=F32):
    nb = proj.shape[0]
    col = lambda c: c[0] // c[1]
    phys = lambda i: (i + blk0) % nblk
    return dict(
        body=functools.partial(_swa_prompt_body, nseq=nb, front_pad=front_pad),
        grid=(1, nblk),
        inputs=[proj, proj, proj, cos, sin, prm],
        out_shape=[jax.ShapeDtypeStruct((nb, nblk * BLOCK, SWA_Q), out_dtype),
                   jax.ShapeDtypeStruct((nb, WINDOW, SWA_KV), F32),
                   jax.ShapeDtypeStruct((nb, WINDOW, SWA_KV), F32)],
        in_specs=[
            pl.BlockSpec((nb, BLOCK, SWA_Q), lambda b, i: (0, phys(i), col(COL_SWQ))),
            pl.BlockSpec((nb, BLOCK, SWA_KV), lambda b, i: (0, phys(i), col(COL_SWK))),
            pl.BlockSpec((nb, BLOCK, SWA_KV), lambda b, i: (0, phys(i), col(COL_SWV))),
            pl.BlockSpec((BLOCK, SWA_KV), lambda b, i: (i, 0)),
            pl.BlockSpec((BLOCK, SWA_KV), lambda b, i: (i, 0)),
            pl.BlockSpec((SUBLANES, LANES), lambda b, i: (0, 0)),
        ],
        out_specs=[
            pl.BlockSpec((nb, BLOCK, SWA_Q), lambda b, i: (0, phys(i), 0)),
            pl.BlockSpec((nb, WINDOW, SWA_KV), lambda b, i: (0, 0, 0)),
            pl.BlockSpec((nb, WINDOW, SWA_KV), lambda b, i: (0, 0, 0)),
        ],
        scratch=[pltpu.VMEM((nb, BLOCK, SWA_KV), F32), pltpu.VMEM((nb, BLOCK, SWA_KV), F32)],
        aliases={},
    )


def _swa_sample_body(ins, outs, scratch, *, nseq, steps, slot, n_slots):
    q_ref, k_ref, v_ref, kc_ref, vc_ref, cos_ref, sin_ref, prm_ref = ins[:8]
    o_ref, ko_full, vo_full = outs
    ko_ref = _state_view(ko_full, slot, n_slots)
    vo_ref = _state_view(vo_full, slot, n_slots)
    cos = cos_ref[...]
    sin = sin_ref[...]
    cos_q = jnp.concatenate([cos, cos], axis=1)
    sin_q = jnp.concatenate([sin, sin], axis=1)
    prm = prm_ref[...]
    grp = SWA_Q_HEADS // SWA_KV_HEADS
    ti = lax.broadcasted_iota(jnp.int32, (grp * steps, WINDOW), 0) % steps
    sj = lax.broadcasted_iota(jnp.int32, (grp * steps, WINDOW), 1)
    mask_cache = sj > ti
    tn = lax.broadcasted_iota(jnp.int32, (grp * steps, steps), 0) % steps
    sn = lax.broadcasted_iota(jnp.int32, (grp * steps, steps), 1)
    mask_new = sn <= tn
    problems = []
    for b in range(nseq):
        q = _rope(q_ref[b], cos_q, sin_q)
        k = _rope(k_ref[b], cos, sin)
        v = v_ref[b]
        kc = kc_ref[b]
        vc = vc_ref[b]
        ko_ref[b, 0:WINDOW - steps, :] = kc[steps:WINDOW, :]
        ko_ref[b, WINDOW - steps:WINDOW, :] = k
        vo_ref[b, 0:WINDOW - steps, :] = vc[steps:WINDOW, :]
        vo_ref[b, WINDOW - steps:WINDOW, :] = v
        problems += _swa_problems(q, (kc, k), (vc, v), (mask_cache, mask_new), prm, steps)
        yield
    res = yield from _sink_attend(problems)
    for b in range(nseq):
        for t, tile in enumerate(_swa_tiles(res[b * SWA_KV_HEADS:(b + 1) * SWA_KV_HEADS], steps)):
            o_ref[b, :, t * LANES:(t + 1) * LANES] = tile


def _swa_sample(proj, nseq, kc, vc, layer, cos, sin, prm, stack_k=None, stack_v=None):
    nb, steps, _ = proj.shape
    assert WINDOW > steps
    col = lambda c: c[0] // c[1]
    ko_shape, ko_spec, extra_k, specs_k, alias_k, slot, n_slots = _state_out(
        stack_k, layer, nb, nseq, (WINDOW, SWA_KV), 1, 8)
    vo_shape, vo_spec, extra_v, specs_v, alias_v, _, _ = _state_out(
        stack_v, layer, nb, nseq, (WINDOW, SWA_KV), 2, 8 + len(extra_k))
    return dict(
        body=functools.partial(_swa_sample_body, nseq=nseq, steps=steps, slot=slot, n_slots=n_slots),
        grid=(nb // nseq, 1),
        inputs=[proj, proj, proj, kc, vc, cos, sin, prm] + extra_k + extra_v,
        out_shape=[jax.ShapeDtypeStruct((nb, steps, SWA_Q), F32),
                   jax.ShapeDtypeStruct(ko_shape, F32),
                   jax.ShapeDtypeStruct(vo_shape, F32)],
        in_specs=[
            pl.BlockSpec((nseq, steps, SWA_Q), lambda b, i: (b, 0, col(COL_SWQ))),
            pl.BlockSpec((nseq, steps, SWA_KV), lambda b, i: (b, 0, col(COL_SWK))),
            pl.BlockSpec((nseq, steps, SWA_KV), lambda b, i: (b, 0, col(COL_SWV))),
            pl.BlockSpec((None, nseq, WINDOW, SWA_KV), lambda b, i: (layer, b, 0, 0)),
            pl.BlockSpec((None, nseq, WINDOW, SWA_KV), lambda b, i: (layer, b, 0, 0)),
            pl.BlockSpec((steps, SWA_KV), lambda b, i: (0, 0)),
            pl.BlockSpec((steps, SWA_KV), lambda b, i: (0, 0)),
            pl.BlockSpec((SUBLANES, LANES), lambda b, i: (0, 0)),
        ] + specs_k + specs_v,
        out_specs=[pl.BlockSpec((nseq, steps, SWA_Q), lambda b, i: (b, 0, 0)), ko_spec, vo_spec],
        scratch=[],
        aliases={**alias_k, **alias_v},
    )


def _tail_kernel(x_ref, odn_ref, zdn_ref, y_ref, zss_ref, osw_ref, dnw_ref, ssw_ref, wout_ref, g1_ref, g2_ref, g3_ref,
                 wfi_ref, wfo_ref, o_ref, *, d_ff, tf):
    odn = odn_ref[...]
    dnw = dnw_ref[...]
    dn = jnp.concatenate([_rmsnorm(odn[:, h * DN_DV:(h + 1) * DN_DV], dnw) for h in range(DN_HEADS)], axis=1)
    dn = dn * _silu(zdn_ref[...])
    yg = y_ref[...] * _silu(zss_ref[...])
    ssw = ssw_ref[...]
    gw = SSM_INNER // SSM_GROUPS
    ys = jnp.concatenate([_rmsnorm(yg[:, g * gw:(g + 1) * gw], ssw[:, g * gw:(g + 1) * gw]) for g in range(SSM_GROUPS)],
                         axis=1)
    mixed = jnp.concatenate([dn.astype(BF16), ys.astype(BF16), osw_ref[...].astype(BF16)], axis=1)
    m = jnp.dot(mixed, wout_ref[...], preferred_element_type=F32)
    x1 = x_ref[...] + _rmsnorm(m, g1_ref[...])
    h = _rmsnorm(x1, g2_ref[...]).astype(BF16)
    y2 = None
    for c in range(d_ff // tf):
        gate = jnp.dot(h, wfi_ref[:, c * tf:(c + 1) * tf], preferred_element_type=F32)
        up = jnp.dot(h, wfi_ref[:, d_ff + c * tf:d_ff + (c + 1) * tf], preferred_element_type=F32)
        part = jnp.dot((_silu(gate) * up).astype(BF16), wfo_ref[c * tf:(c + 1) * tf, :], preferred_element_type=F32)
        y2 = part if y2 is None else y2 + part
    o_ref[...] = x1 + _rmsnorm(y2, g3_ref[...])


def _layer_spec(shape, layer):
    nd = len(shape) - 1
    return pl.BlockSpec((None,) + tuple(shape[1:]), lambda *_: (layer,) + (0,) * nd, pipeline_mode=pl.Buffered(1))


def _tail(x, proj, odn, y, osw, dnw, ssw, wout, g1, g2, g3, wfi, wfo, layer, l_out, tm_target):
    nb, _, d = x.shape
    d_ff = wfo.shape[1]
    tm = _pick_tile(l_out, tm_target)
    tf = 2 * LANES if d_ff % (2 * LANES) == 0 else d_ff
    row = lambda w: pl.BlockSpec((None, tm, w), lambda b, i: (b, i, 0))
    gate = lambda c: pl.BlockSpec((None, tm, c[1]), lambda b, i: (b, i, c[0] // c[1]))
    return pl.pallas_call(
        functools.partial(_tail_kernel, d_ff=d_ff, tf=tf),
        out_shape=jax.ShapeDtypeStruct((nb, l_out, d), F32),
        grid=(nb, l_out // tm),
        in_specs=[row(d), row(DN_V), gate(COL_DNZ), row(SSM_INNER), gate(COL_SSZ), row(SWA_Q),
                  _layer_spec(dnw.shape, layer), _layer_spec(ssw.shape, layer), _layer_spec(wout.shape, layer),
                  _layer_spec(g1.shape, layer), _layer_spec(g2.shape, layer), _layer_spec(g3.shape, layer),
                  _layer_spec(wfi.shape, layer), _layer_spec(wfo.shape, layer)],
        out_specs=row(d),
        compiler_params=pltpu.CompilerParams(dimension_semantics=("arbitrary", "arbitrary"),
                                             vmem_limit_bytes=VMEM_LIMIT),
        name="outproj_ffn",
    )(x, odn, proj, y, proj, osw, dnw, ssw, wout, g1, g2, g3, wfi, wfo)


def _reorder_w_in_kernel(w_ref, o_ref):
    w = w_ref[0]
    offs = [0]
    for wd in IN_WIDTHS:
        offs.append(offs[-1] + wd)
    seg = lambda i: w[:, offs[i]:offs[i + 1]]
    dn_qkv, dn_z, dn_b, dn_a, ssm_xbc, ssm_z, ssm_dt, sw_q, sw_k, sw_v = (seg(i) for i in range(len(IN_WIDTHS)))
    n_small = dn_b.shape[1] + dn_a.shape[1] + ssm_dt.shape[1]
    small = jnp.concatenate([dn_b, dn_a, ssm_dt, jnp.zeros((w.shape[0], LANES - n_small), w.dtype)], axis=1)
    sw_q = jnp.concatenate([sw_q[:, h * SWA_HEAD_DIM:(h + 1) * SWA_HEAD_DIM] for h in _swa_head_order()], axis=1)
    o_ref[0] = jnp.concatenate([dn_qkv, dn_z, ssm_z, ssm_xbc, sw_q, sw_k, sw_v, small], axis=1).astype(BF16)


def _reorder_w_in(w):
    depth, d, d_in = w.shape
    assert d_in == sum(IN_WIDTHS)
    tr = _pick_tile(d, 256)
    return pl.pallas_call(
        _reorder_w_in_kernel,
        out_shape=jax.ShapeDtypeStruct((depth, d, D_PROJ), BF16),
        grid=(depth, d // tr),
        in_specs=[pl.BlockSpec((1, tr, d_in), lambda l, i: (l, i, 0))],
        out_specs=pl.BlockSpec((1, tr, D_PROJ), lambda l, i: (l, i, 0)),
        compiler_params=pltpu.CompilerParams(dimension_semantics=("arbitrary", "arbitrary")),
        name="reorder_w_in",
    )(w)


def _scalar_param_tiles(rows):
    depth = rows[0][1].shape[0]
    padded = [jnp.pad(v.astype(F32), ((0, 0), (off, LANES - off - v.shape[1]))) for off, v in rows]
    padded += [jnp.zeros((depth, LANES), F32)] * (SUBLANES - len(rows))
    return jnp.stack(padded, axis=1)


def _rope_tables(pos):
    half = SWA_HEAD_DIM // 2
    inv = ROPE_THETA ** (-jnp.arange(half, dtype=F32) / half)
    ang = pos.astype(F32)[:, None] * inv[None, :]
    cos = jnp.cos(ang)
    sin = jnp.sin(ang)
    cos_t = jnp.concatenate([cos, cos] * SWA_KV_HEADS, axis=1)
    sin_t = jnp.concatenate([-sin, sin] * SWA_KV_HEADS, axis=1)
    return cos_t, sin_t


def kernel(x_prompt, x_sample, state_dn, state_dn_conv, state_ssm, state_ssm_conv, cache_swa_k, cache_swa_v,
           meta_tokens, w_in, dn_conv_w, dn_a_log, dn_dt_bias, dn_norm_w, ssm_conv_w, ssm_conv_b, ssm_a_log,
           ssm_dt_bias, ssm_d, ssm_norm_w, swa_sinks, w_out, g_pre_mix, g_post_mix, g_pre_ffn, g_post_ffn,
           w_ffn_in, w_ffn_out):
    bp, seq, d = x_prompt.shape
    bs, ts, _ = x_sample.shape
    depth = w_in.shape[0]
    lp = N_META + seq + FRONT_PAD
    assert lp % BLOCK == 0 and BLOCK % CHUNK == 0 and seq % BLOCK == 0
    nblk = lp // BLOCK
    blk0 = nblk - 1
    tm0 = _pick_tile(seq, TM_FIRST)
    assert tm0 >= BLOCK
    x_front = jnp.concatenate([jnp.zeros((FRONT_PAD, d), x_prompt.dtype), meta_tokens.astype(x_prompt.dtype),
                               jnp.zeros((tm0 - BLOCK, d), x_prompt.dtype)], axis=0)
    pad_first = ((seq, seq + FRONT_PAD), (lp, seq + tm0))
    pad_range = ((seq, seq + FRONT_PAD),)
    xp = None
    xs = x_sample.reshape(1, bs * ts, d)

    cos_p, sin_p = _rope_tables(jnp.arange(lp, dtype=jnp.int32) - FRONT_PAD)
    cos_s, sin_s = _rope_tables(PAST_LEN + jnp.arange(ts, dtype=jnp.int32))

    gw = (SSM_HEADS // SSM_GROUPS) * SSM_HEADDIM
    nseq_s = _pick_tile(bs, 8) if bs % SUBLANES == 0 else bs
    zero_dn = jnp.zeros((1, bp, DN_HEADS, DN_DK, DN_DV), F32)
    zero_dnc = jnp.zeros((bp, SUBLANES, DN_CONV), F32)
    zero_ssm = jnp.zeros((1, bp, SSM_GROUPS, gw, SSM_STATE), F32)
    zero_ssmc = jnp.zeros((bp, SUBLANES, SSM_CONV), F32)
    state_ssm_g = state_ssm.reshape(depth, bs, SSM_GROUPS, gw, SSM_STATE)
    cache_k = cache_swa_k.reshape(depth, bs, WINDOW, SWA_KV)
    cache_v = cache_swa_v.reshape(depth, bs, WINDOW, SWA_KV)

    w_in_r = _reorder_w_in(w_in)
    swa0 = DN_V + SSM_INNER
    w_out_b = jnp.concatenate(
        [w_out[:, :swa0]] + [w_out[:, swa0 + h * SWA_HEAD_DIM:swa0 + (h + 1) * SWA_HEAD_DIM] for h in _swa_head_order()],
        axis=1).astype(BF16)
    w_fi_b = w_ffn_in.astype(BF16)
    w_fo_b = w_ffn_out.astype(BF16)
    g1, g2, g3, g4 = (a[:, None, :] for a in (g_pre_mix, g_post_mix, g_pre_ffn, g_post_ffn))
    dn_nw = dn_norm_w[:, None, :]
    ssm_nw = ssm_norm_w[:, None, :]

    new_p, new_s = [], []
    dn_s = ssm_s = k_s = v_s = None
    prm_all = _scalar_param_tiles([(SM_A, dn_a_log), (SM_A, dn_dt_bias), (SM_DT, ssm_a_log), (SM_DT, ssm_dt_bias),
                                   (0, swa_sinks)])
    drow_all = jnp.repeat(ssm_d, SSM_HEADDIM, axis=1)
    dn_cbuf_s = jnp.pad(state_dn_conv, ((0, 0), (0, 0), (SUBLANES - (CONV_WIDTH - 1), 0), (0, 0)))
    ssm_cbuf_s = jnp.pad(state_ssm_conv, ((0, 0), (0, 0), (SUBLANES - (CONV_WIDTH - 1), 0), (0, 0)))
    for l in range(depth):
        prm = prm_all[l]
        drow = drow_all[l][None, :]
        cbias = ssm_conv_b[l][None, :]
        last = l == depth - 1

        if l == 0:
            proj, xp = _inproj_first(x_prompt, x_front, g1, w_in_r, l, pad_first, tm0)
        else:
            proj = _inproj(xp, g1, w_in_r, l, pad_range, TM_DENSE)
        (odn, dn_p, dnc_p), = _run_parts([_dn_mixer(proj, nblk, blk0, BLOCK, CHUNK, FRONT_PAD, bp, zero_dn, 0,
                                                    zero_dnc, dn_conv_w[l], prm)], "dn_mixer")
        (ys, ssm_p, ssmc_p), = _run_parts([_ssd_mixer(proj, nblk, blk0, BLOCK, CHUNK, FRONT_PAD, bp, zero_ssm, 0,
                                                      zero_ssmc, ssm_conv_w[l], cbias, prm, drow)], "ssd_mixer")
        (osw, k_p, v_p), = _run_parts([_swa_prompt(proj, nblk, blk0, FRONT_PAD, cos_p, sin_p, prm, out_dtype=BF16)],
                                      "swa_prompt")
        xp = _tail(xp, proj, odn, ys, osw, dn_nw, ssm_nw, w_out_b, g2, g3, g4, w_fi_b, w_fo_b, l,
                   seq if last else lp, TM_DENSE)
        new_p.append((dn_p, dnc_p[:, -(CONV_WIDTH - 1):], ssm_p.reshape(bp, SSM_HEADS, SSM_HEADDIM, SSM_STATE),
                      ssmc_p[:, -(CONV_WIDTH - 1):], k_p.reshape(bp, WINDOW, SWA_KV_HEADS, SWA_HEAD_DIM),
                      v_p.reshape(bp, WINDOW, SWA_KV_HEADS, SWA_HEAD_DIM)))

        proj_s = _inproj(xs, g1, w_in_r, l, None, TM_DENSE)
        proj = proj_s.reshape(bs, ts, D_PROJ)
        (odn, dn_s, dnc_s), (ys, ssm_s, ssmc_s), (osw, k_s, v_s) = _run_parts([
            _dn_mixer(proj, 1, 0, ts, ts, 0, nseq_s, state_dn, l, dn_cbuf_s[l], dn_conv_w[l], prm,
                      stack=(depth, dn_s)),
            _ssd_mixer(proj, 1, 0, ts, ts, 0, nseq_s, state_ssm_g, l, ssm_cbuf_s[l], ssm_conv_w[l], cbias, prm, drow,
                       stack=(depth, ssm_s)),
            _swa_sample(proj, nseq_s, cache_k, cache_v, l, cos_s, sin_s, prm,
                        stack_k=(depth, k_s), stack_v=(depth, v_s))], "mixers_sample")
        flat = lambda a: a.reshape(1, bs * ts, a.shape[-1])
        xs = _tail(xs, proj_s, flat(odn), flat(ys), flat(osw), dn_nw, ssm_nw, w_out_b, g2, g3, g4, w_fi_b, w_fo_b, l,
                   bs * ts, TM_DENSE)
        new_s.append((dnc_s[:, -(CONV_WIDTH - 1):], ssmc_s[:, -(CONV_WIDTH - 1):]))

    outs_p = tuple(jnp.stack([st[i] for st in new_p]) for i in range(6))
    dnc_s, ssmc_s = (jnp.stack([st[i] for st in new_s]) for i in range(2))
    outs_s = (dn_s, dnc_s, ssm_s.reshape(depth, bs, SSM_HEADS, SSM_HEADDIM, SSM_STATE), ssmc_s,
              k_s.reshape(depth, bs, WINDOW, SWA_KV_HEADS, SWA_HEAD_DIM),
              v_s.reshape(depth, bs, WINDOW, SWA_KV_HEADS, SWA_HEAD_DIM))
    return (xp, xs.reshape(bs, ts, d)) + outs_p + outs_s
```

```python
import functools

import jax
import jax.numpy as jnp
from jax import lax
from jax.experimental import pallas as pl
from jax.experimental.pallas import tpu as pltpu

F32 = jnp.float32
BF16 = jnp.bfloat16
NT = (((1,), (1,)), ((), ()))
TN = (((0,), (0,)), ((), ()))

N_META = 16
CONV_WIDTH = 4
CHUNK = 64
SSD_CHUNK = 128
BLOCK = 128
WINDOW = 128
FRONT_PAD = BLOCK - N_META
ROPE_THETA = 10000.0
PAST_LEN = 8192
EPS = 1e-6

DN_HEADS, DN_DK, DN_DV = 4, 128, 128
DN_QK = DN_HEADS * DN_DK
DN_V = DN_HEADS * DN_DV
DN_CONV = 2 * DN_QK + DN_V
SSM_HEADS, SSM_HEADDIM, SSM_GROUPS, SSM_STATE = 4, 64, 2, 128
SSM_INNER = SSM_HEADS * SSM_HEADDIM
SSM_BC = SSM_GROUPS * SSM_STATE
SSM_CONV = SSM_INNER + 2 * SSM_BC
SWA_Q_HEADS, SWA_KV_HEADS, SWA_HEAD_DIM = 4, 2, 64
SWA_Q = SWA_Q_HEADS * SWA_HEAD_DIM
SWA_KV = SWA_KV_HEADS * SWA_HEAD_DIM
IN_WIDTHS = (DN_CONV, DN_V, DN_HEADS, DN_HEADS, SSM_CONV, SSM_INNER, SSM_HEADS, SWA_Q, SWA_KV, SWA_KV)

LANES = 128
SUBLANES = 8
COL_QKV = (0, DN_CONV)
COL_DNZ = (COL_QKV[0] + DN_CONV, DN_V)
COL_SSZ = (COL_DNZ[0] + DN_V, SSM_INNER)
COL_XBC = (COL_SSZ[0] + SSM_INNER, SSM_CONV)
COL_SWQ = (COL_XBC[0] + SSM_CONV, SWA_Q)
COL_SWK = (COL_SWQ[0] + SWA_Q, SWA_KV)
COL_SWV = (COL_SWK[0] + SWA_KV, SWA_KV)
COL_SM = (COL_SWV[0] + SWA_KV, LANES)
D_PROJ = COL_SM[0] + LANES
assert all(off % width == 0 for off, width in (COL_QKV, COL_DNZ, COL_SSZ, COL_XBC, COL_SWQ, COL_SWK, COL_SWV, COL_SM))
SM_B, SM_A, SM_DT = 0, DN_HEADS, 2 * DN_HEADS
PRM_DN_ALOG, PRM_DN_DTB, PRM_SSM_ALOG, PRM_SSM_DTB, PRM_SINK = 0, 1, 2, 3, 4
NEG_BIG = -1e30
VMEM_LIMIT = 56 * 1024 * 1024
TM_DENSE = 640
TM_FIRST = 512


def _bdot(a, b, dims=None):
    a = a.astype(BF16)
    b = b.astype(BF16)
    if dims is None:
        return jnp.dot(a, b, preferred_element_type=F32)
    return lax.dot_general(a, b, dims, preferred_element_type=F32)


def _cumsum_rows(lmat, g):
    hi = g.astype(BF16)
    r1 = g - hi.astype(F32)
    mid = r1.astype(BF16)
    lo = (r1 - mid.astype(F32)).astype(BF16)
    dot = lambda part: jnp.dot(lmat, part, preferred_element_type=F32)
    return dot(hi) + dot(mid) + dot(lo)


def _rmsnorm(x, g):
    return x * lax.rsqrt(jnp.mean(x * x, axis=-1, keepdims=True) + EPS) * g


def _l2norm(x):
    return x * lax.rsqrt(jnp.sum(x * x, axis=-1, keepdims=True) + EPS)


def _sigmoid(x):
    return 0.5 * jnp.tanh(0.5 * x) + 0.5


def _silu(x):
    half = 0.5 * x
    return half * jnp.tanh(half) + half


def _softplus(x):
    return jnp.maximum(x, 0.0) + jnp.log1p(jnp.exp(-jnp.abs(x)))


def _pick_tile(n, target):
    best = None
    for t in range(SUBLANES, min(n, target) + 1, SUBLANES):
        if n % t == 0:
            best = t
    assert best is not None, n
    return best


def _project_rows(x, g, w, tm, pad_range):
    h = _rmsnorm(x, g)
    if pad_range:
        r = pl.program_id(1) * tm + lax.broadcasted_iota(jnp.int32, (tm, 1), 0)
        is_pad = None
        for lo, hi in pad_range:
            hit = (r >= lo) & (r < hi)
            is_pad = hit if is_pad is None else is_pad | hit
        h = jnp.where(is_pad, 0.0, h)
    return jnp.dot(h.astype(BF16), w, preferred_element_type=F32)


def _inproj_kernel(x_ref, g_ref, w_ref, o_ref, *, tm, pad_range):
    o_ref[...] = _project_rows(x_ref[...], g_ref[...], w_ref[...], tm, pad_range)


def _inproj_first_kernel(xm_ref, xt_ref, g_ref, w_ref, o_ref, xo_ref, *, tm, n_main, pad_range):
    x = jnp.where(pl.program_id(1) == n_main, xt_ref[...], xm_ref[...])
    xo_ref[...] = x
    o_ref[...] = _project_rows(x, g_ref[...], w_ref[...], tm, pad_range)


def _inproj_first(x_main, x_tail, g, w, layer, pad_range, tm):
    nb, rows, d = x_main.shape
    assert rows % tm == 0 and x_tail.shape == (tm, d)
    n_main = rows // tm
    return pl.pallas_call(
        functools.partial(_inproj_first_kernel, tm=tm, n_main=n_main, pad_range=pad_range),
        out_shape=(jax.ShapeDtypeStruct((nb, rows + tm, D_PROJ), F32), jax.ShapeDtypeStruct((nb, rows + tm, d), F32)),
        grid=(nb, n_main + 1),
        in_specs=[pl.BlockSpec((None, tm, d), lambda b, i: (b, jnp.minimum(i, n_main - 1), 0)),
                  pl.BlockSpec((tm, d), lambda b, i: (0, 0)),
                  _layer_spec(g.shape, layer), _layer_spec(w.shape, layer)],
        out_specs=(pl.BlockSpec((None, tm, D_PROJ), lambda b, i: (b, i, 0)),
                   pl.BlockSpec((None, tm, d), lambda b, i: (b, i, 0))),
        compiler_params=pltpu.CompilerParams(dimension_semantics=("arbitrary", "arbitrary"),
                                             vmem_limit_bytes=VMEM_LIMIT),
        name="inproj_first",
    )(x_main, x_tail, g, w)


def _inproj(x, g, w, layer, pad_range, tm_target):
    nb, rows, d = x.shape
    tm = _pick_tile(rows, tm_target)
    return pl.pallas_call(
        functools.partial(_inproj_kernel, tm=tm, pad_range=pad_range),
        out_shape=jax.ShapeDtypeStruct((nb, rows, D_PROJ), F32),
        grid=(nb, rows // tm),
        in_specs=[pl.BlockSpec((None, tm, d), lambda b, i: (b, i, 0)), _layer_spec(g.shape, layer),
                  _layer_spec(w.shape, layer)],
        out_specs=pl.BlockSpec((None, tm, D_PROJ), lambda b, i: (b, i, 0)),
        compiler_params=pltpu.CompilerParams(dimension_semantics=("arbitrary", "arbitrary"),
                                             vmem_limit_bytes=VMEM_LIMIT),
        name="inproj",
    )(x, g, w)


def _causal_conv(xbuf, raw_ref, cw_ref, cbo_ref, rows):
    x = raw_ref[...]
    prev = xbuf[0:SUBLANES, :]
    cw = cw_ref[...]
    row = lax.broadcasted_iota(jnp.int32, (SUBLANES, 1), 0)
    acc = None
    for i in range(CONV_WIDTH):
        s = CONV_WIDTH - 1 - i
        if s == 0:
            xs = x
        else:
            r = pltpu.roll(x, s, axis=0)
            head = jnp.where(row < s, pltpu.roll(prev, s, axis=0), r[0:SUBLANES])
            xs = head if rows == SUBLANES else jnp.concatenate([head, r[SUBLANES:]], axis=0)
        term = xs * cw[i:i + 1, :]
        acc = term if acc is None else acc + term
    tail = x[rows - SUBLANES:rows]
    cbo_ref[...] = tail
    xbuf[0:SUBLANES, :] = tail
    return acc


def _causal_mask(c):
    return lax.broadcasted_iota(jnp.int32, (c, c), 0) >= lax.broadcasted_iota(jnp.int32, (c, c), 1)


def _segment_decay(gc, gct, lane, ge):
    col = gc[:, lane:lane + 1]
    row = gct[lane:lane + 1, :]
    return jnp.where(ge, jnp.exp(jnp.where(ge, col - row, 0.0)), 0.0)


def _inv_unit_lower_minus_eye(a_list, c, nh):
    w = nh * c
    blk_r = lax.broadcasted_iota(jnp.int32, (w, w), 0) // c
    blk_c = lax.broadcasted_iota(jnp.int32, (w, w), 1) // c
    same = blk_r == blk_c

    def block_diag(p):
        return jnp.where(same, jnp.concatenate([p] * nh, axis=0), 0.0).astype(BF16)

    ys = [-a for a in a_list]
    ps = [_bdot(a, block_diag(a)) for a in a_list]
    yield
    n = 2
    while n < c:
        pbds = [block_diag(p) for p in ps]
        n *= 2
        if n < c:
            sts = [_bdot(jnp.concatenate([y, p], axis=0), pbd) for y, p, pbd in zip(ys, ps, pbds)]
            ys = [y + p + st[:c] for y, p, st in zip(ys, ps, sts)]
            ps = [st[c:] for st in sts]
        else:
            ys = [y + p + _bdot(y, pbd) for y, p, pbd in zip(ys, ps, pbds)]
        yield
    return ys


def _run_parts(parts, name):
    grid = parts[0]["grid"]
    assert all(p["grid"] == grid for p in parts)
    n_in = [len(p["inputs"]) for p in parts]
    n_out = [len(p["out_shape"]) for p in parts]
    n_scr = [len(p["scratch"]) for p in parts]
    aliases = {}
    for k, p in enumerate(parts):
        for i, o in p["aliases"].items():
            aliases[sum(n_in[:k]) + i] = sum(n_out[:k]) + o

    def kernel(*refs):
        ins = refs[:sum(n_in)]
        outs = refs[sum(n_in):sum(n_in) + sum(n_out)]
        scr = refs[sum(n_in) + sum(n_out):]
        gens = [p["body"](ins[sum(n_in[:k]):sum(n_in[:k + 1])], outs[sum(n_out[:k]):sum(n_out[:k + 1])],
                          scr[sum(n_scr[:k]):sum(n_scr[:k + 1])]) for k, p in enumerate(parts)]
        for tag in gens[0]:
            if tag == "chain":
                break
        live = list(gens)
        while live:
            for g in list(live):
                if next(g, StopIteration) is StopIteration:
                    live.remove(g)

    results = pl.pallas_call(
        kernel,
        out_shape=tuple(s for p in parts for s in p["out_shape"]),
        grid=grid,
        in_specs=[s for p in parts for s in p["in_specs"]],
        out_specs=tuple(s for p in parts for s in p["out_specs"]),
        scratch_shapes=[s for p in parts for s in p["scratch"]],
        input_output_aliases=aliases,
        compiler_params=pltpu.CompilerParams(dimension_semantics=("arbitrary",) * len(grid),
                                             vmem_limit_bytes=VMEM_LIMIT),
        name=name,
    )(*[a for p in parts for a in p["inputs"]])
    return [list(results[sum(n_out[:k]):sum(n_out[:k + 1])]) for k in range(len(parts))]


def _state_out(stack, layer, nb, nseq, tail, out_index, n_inputs):
    zeros = (0,) * len(tail)
    if stack is None:
        return ((nb,) + tail, pl.BlockSpec((nseq,) + tail, lambda b, *_: (b,) + zeros), [], [], {}, None, 0)
    depth, prev = stack
    shape = (depth, nb) + tail
    if prev is None:
        spec = pl.BlockSpec((depth, nseq) + tail, lambda b, *_: (0, b) + zeros)
        return (shape, spec, [], [], {}, layer, depth)
    spec = pl.BlockSpec((None, nseq) + tail, lambda b, *_: (layer, b) + zeros)
    return (shape, spec, [prev], [pl.BlockSpec(memory_space=pl.ANY)], {n_inputs: out_index}, None, 0)


def _state_view(ref, slot, n_slots):
    if slot is None:
        return ref
    for other in range(n_slots):
        if other != slot:
            ref[other] = jnp.zeros(ref.shape[1:], ref.dtype)
    return ref.at[slot]


def _dn_prep(items, lmat, ge, gt_all, chunk):
    heads = range(DN_HEADS)
    gcs = [_cumsum_rows(lmat, g_all) for _, _, g_all in items]
    gcts = [gc.T for gc in gcs]
    qs = [[_l2norm(qkv[:, h * DN_DK:(h + 1) * DN_DK]) * DN_DK ** -0.5 for h in heads] for qkv, _, _ in items]
    ks = [[_l2norm(qkv[:, DN_QK + h * DN_DK:DN_QK + (h + 1) * DN_DK]) for h in heads] for qkv, _, _ in items]
    vs = [[qkv[:, 2 * DN_QK + h * DN_DV:2 * DN_QK + (h + 1) * DN_DV] for h in heads] for qkv, _, _ in items]
    betas = [[beta_all[:, SM_B + h:SM_B + h + 1] for h in heads] for _, beta_all, _ in items]
    yield
    kbs = [[k.astype(BF16) for k in kk] for kk in ks]
    kks = [jnp.concatenate([_bdot(kb, kb, NT) for kb in kb4], axis=1) for kb4 in kbs]
    qks = [jnp.concatenate([_bdot(q, kb, NT) for q, kb in zip(q4, kb4)], axis=1) for q4, kb4 in zip(qs, kbs)]
    decs = [jnp.concatenate([_segment_decay(gc, gct, SM_A + h, ge) for h in heads], axis=1)
            for gc, gct in zip(gcs, gcts)]
    beta_ws = [jnp.concatenate([jnp.broadcast_to(b, (chunk, chunk)) for b in b4], axis=1) for b4 in betas]
    a_list = [jnp.where(gt_all, bw * kk * dec, 0.0) for bw, kk, dec in zip(beta_ws, kks, decs)]
    yield "chain"
    n = len(items)
    egs, ekds, egls, attns, rhss, qds, kds = ([None] * n for _ in range(7))

    def side_work(i):
        gc = gcs[i]
        glast = gc[chunk - 1:chunk, :]
        egs[i] = jnp.exp(gc)
        ekds[i] = jnp.exp(glast - gc)
        egls[i] = jnp.exp(glast)
        attns[i] = (qks[i] * decs[i]).astype(BF16)
        lane = lambda a, h: a[:, SM_A + h:SM_A + h + 1]
        rhss[i] = [jnp.concatenate([vs[i][h] * betas[i][h], ks[i][h] * (betas[i][h] * lane(egs[i], h))], axis=1)
                   for h in heads]
        qds[i] = [(qs[i][h] * lane(egs[i], h)).astype(BF16) for h in heads]
        kds[i] = [(ks[i][h] * lane(ekds[i], h)).astype(BF16) for h in heads]

    inverse = _inv_unit_lower_minus_eye(a_list, chunk, DN_HEADS)
    todo = list(range(n))
    while True:
        try:
            next(inverse)
        except StopIteration as done:
            tms = done.value
            break
        if todo:
            side_work(todo.pop(0))
        yield
    for i in todo:
        side_work(i)
    uws = [[rhss[i][h] + _bdot(tms[i][:, h * chunk:(h + 1) * chunk], rhss[i][h]) for h in heads] for i in range(n)]
    yield
    out = []
    for i in range(n):
        per_head = []
        for h in heads:
            uw = uws[i][h]
            wq = jnp.concatenate([uw[:, DN_DV:].astype(BF16), qds[i][h]], axis=0)
            per_head.append((uw[:, :DN_DV], wq, attns[i][:, h * chunk:(h + 1) * chunk], kds[i][h],
                             egls[i][:, SM_A + h:SM_A + h + 1]))
        out.append(per_head)
    return out


def _dn_body(ins, outs, scratch, *, nseq, rows, chunk, front_pad, slot, n_slots):
    qkv_ref, sm_ref, s0_ref, cb_ref, cw_ref, prm_ref = ins[:6]
    o_ref, so_full, cbo_ref = outs
    xbuf, = scratch
    so_ref = _state_view(so_full, slot, n_slots)
    blk = pl.program_id(1)
    first = blk == 0

    @pl.when(first)
    def _():
        so_ref[...] = s0_ref[...]
        xbuf[:, 0:SUBLANES, :] = cb_ref[...]

    prm = prm_ref[...]
    ge = _causal_mask(chunk)
    lmat = jnp.where(ge, 1.0, 0.0).astype(BF16)
    wide = (chunk, DN_HEADS * chunk)
    gt_all = lax.broadcasted_iota(jnp.int32, wide, 0) > lax.broadcasted_iota(jnp.int32, wide, 1) % chunk
    nchunk = rows // chunk

    items = []
    for s in range(nseq):
        qkv = _silu(_causal_conv(xbuf.at[s], qkv_ref.at[s], cw_ref, cbo_ref.at[s], rows))
        sm = sm_ref[s]
        beta_all = _sigmoid(sm)
        g_all = -jnp.exp(prm[PRM_DN_ALOG:PRM_DN_ALOG + 1, :]) * _softplus(sm + prm[PRM_DN_DTB:PRM_DN_DTB + 1, :])
        if front_pad:
            pos = blk * rows + lax.broadcasted_iota(jnp.int32, (rows, 1), 0)
            g_all = jnp.where(pos < front_pad, 0.0, g_all)
        for c in range(nchunk):
            cs = slice(c * chunk, (c + 1) * chunk)
            items.append((qkv[cs], beta_all[cs], g_all[cs]))
        yield
    prep = yield from _dn_prep(items, lmat, ge, gt_all, chunk)

    chains = [(s, h) for s in range(nseq) for h in range(DN_HEADS)]
    states = [so_ref[s, h] for s, h in chains]
    for c in range(nchunk):
        r0 = c * chunk
        fac = [prep[s * nchunk + c][h] for s, h in chains]
        m1s = [_bdot(f[1], st) for f, st in zip(fac, states)]
        yield
        v_news = [(f[0] - m1[:chunk]).astype(BF16) for f, m1 in zip(fac, m1s)]
        ups = [_bdot(f[3], v, TN) for f, v in zip(fac, v_news)]
        os_ = [m1[chunk:] + _bdot(f[2], v) for f, m1, v in zip(fac, m1s, v_news)]
        yield
        states = [st * f[4] + up for f, st, up in zip(fac, states, ups)]
        for (s, h), o in zip(chains, os_):
            lo = h * DN_DV
            o_ref[s, r0:r0 + chunk, lo:lo + DN_DV] = o
        yield
    for (s, h), st in zip(chains, states):
        so_ref[s, h] = st


def _dn_mixer(proj, nblk, blk0, rows, chunk, front_pad, nseq, s0, layer, cbuf, cw, prm, stack=None):
    nb = proj.shape[0]
    col = lambda c: c[0] // c[1]
    phys = lambda i: (i + blk0) % nblk
    so_shape, so_spec, extra, extra_specs, aliases, slot, n_slots = _state_out(
        stack, layer, nb, nseq, (DN_HEADS, DN_DK, DN_DV), 1, 6)
    return dict(
        body=functools.partial(_dn_body, nseq=nseq, rows=rows, chunk=chunk, front_pad=front_pad,
                               slot=slot, n_slots=n_slots),
        grid=(nb // nseq, nblk),
        inputs=[proj, proj, s0, cbuf, cw, prm] + extra,
        out_shape=[jax.ShapeDtypeStruct((nb, nblk * rows, DN_V), F32),
                   jax.ShapeDtypeStruct(so_shape, F32),
                   jax.ShapeDtypeStruct((nb, SUBLANES, DN_CONV), F32)],
        in_specs=[
            pl.BlockSpec((nseq, rows, DN_CONV), lambda b, i: (b, phys(i), col(COL_QKV))),
            pl.BlockSpec((nseq, rows, LANES), lambda b, i: (b, phys(i), col(COL_SM))),
            pl.BlockSpec((None, nseq, DN_HEADS, DN_DK, DN_DV), lambda b, i: (layer, b, 0, 0, 0)),
            pl.BlockSpec((nseq, SUBLANES, DN_CONV), lambda b, i: (b, 0, 0)),
            pl.BlockSpec((CONV_WIDTH, DN_CONV), lambda b, i: (0, 0)),
            pl.BlockSpec((SUBLANES, LANES), lambda b, i: (0, 0)),
        ] + extra_specs,
        out_specs=[
            pl.BlockSpec((nseq, rows, DN_V), lambda b, i: (b, phys(i), 0)),
            so_spec,
            pl.BlockSpec((nseq, SUBLANES, DN_CONV), lambda b, i: (b, 0, 0)),
        ],
        scratch=[pltpu.VMEM((nseq, SUBLANES, DN_CONV), F32)],
        aliases=aliases,
    )


def _ssd_body(ins, outs, scratch, *, nseq, rows, chunk, front_pad, slot, n_slots):
    xbc_ref, sm_ref, h0_ref, cb_ref, cw_ref, cbias_ref, prm_ref, drow_ref = ins[:8]
    y_ref, ho_full, cbo_ref = outs
    xbuf, = scratch
    ho_ref = _state_view(ho_full, slot, n_slots)
    blk = pl.program_id(1)

    @pl.when(blk == 0)
    def _():
        ho_ref[...] = h0_ref[...]
        xbuf[:, 0:SUBLANES, :] = cb_ref[...]

    prm = prm_ref[...]
    drow = drow_ref[...]
    cbias = cbias_ref[...]
    ge = _causal_mask(chunk)
    lmat = jnp.where(ge, 1.0, 0.0).astype(BF16)
    hpg = SSM_HEADS // SSM_GROUPS
    gw = hpg * SSM_HEADDIM
    lane = lax.broadcasted_iota(jnp.int32, (1, gw), 1)
    srow = lax.broadcasted_iota(jnp.int32, (gw, 1), 0)
    in_head = [(lane >= j * SSM_HEADDIM) & (lane < (j + 1) * SSM_HEADDIM) for j in range(hpg)]
    nchunk = rows // chunk
    groups = range(SSM_GROUPS)

    items = []
    for s in range(nseq):
        act = _silu(_causal_conv(xbuf.at[s], xbc_ref.at[s], cw_ref, cbo_ref.at[s], rows) + cbias)
        dt_all = _softplus(sm_ref[s] + prm[PRM_SSM_DTB:PRM_SSM_DTB + 1, :])
        if front_pad:
            pos = blk * rows + lax.broadcasted_iota(jnp.int32, (rows, 1), 0)
            dt_all = jnp.where(pos < front_pad, 0.0, dt_all)
        g_all = dt_all * (-jnp.exp(prm[PRM_SSM_ALOG:PRM_SSM_ALOG + 1, :]))
        for c in range(nchunk):
            cs = slice(c * chunk, (c + 1) * chunk)
            items.append((act[cs], dt_all[cs], g_all[cs]))
        yield
    n_items = len(items)
    gcs = [_cumsum_rows(lmat, g) for _, _, g in items]
    gcts = [gc.T for gc in gcs]
    xgs = [[a[:, g * gw:(g + 1) * gw] for g in groups] for a, _, _ in items]
    bgs = [[a[:, SSM_INNER + g * SSM_STATE:SSM_INNER + (g + 1) * SSM_STATE] for g in groups] for a, _, _ in items]
    cgs = [[a[:, SSM_INNER + SSM_BC + g * SSM_STATE:SSM_INNER + SSM_BC + (g + 1) * SSM_STATE] for g in groups]
           for a, _, _ in items]
    cbs = [[_bdot(cgs[i][g], bgs[i][g], NT) for g in groups] for i in range(n_items)]
    yield
    egs = [jnp.exp(gc) for gc in gcs]
    ekds = [jnp.exp(gc[chunk - 1:chunk, :] - gc) for gc in gcs]
    egls = [jnp.exp(gc[chunk - 1:chunk, :]) for gc in gcs]
    heads = [(g, j) for g in groups for j in range(hpg)]
    ln = lambda g, j: SM_DT + g * hpg + j
    xdts = [[jnp.where(in_head[j], xgs[i][g] * items[i][1][:, ln(g, j):ln(g, j) + 1], 0.0).astype(BF16)
             for g, j in heads] for i in range(n_items)]
    attns = [[cbs[i][g] * _segment_decay(gcs[i], gcts[i], ln(g, j), ge) for g, j in heads] for i in range(n_items)]
    y_intras = [[_bdot(attns[i][k], xdts[i][k]) for k in range(len(heads))] for i in range(n_items)]
    yield
    upds = [[_bdot(xdts[i][k], bgs[i][g] * ekds[i][:, ln(g, j):ln(g, j) + 1], TN) for k, (g, j) in enumerate(heads)]
            for i in range(n_items)]
    yield
    cds = [[jnp.concatenate([cgs[i][g] * egs[i][:, ln(g, j):ln(g, j) + 1] for j in range(hpg)], axis=0).astype(BF16)
            for g in groups] for i in range(n_items)]
    y_loc = [[sum(y_intras[i][g * hpg + j] for j in range(hpg)) for g in groups] for i in range(n_items)]
    h_inc = [[sum(upds[i][g * hpg + j] for j in range(hpg)) for g in groups] for i in range(n_items)]
    gl_cols = []
    for i in range(n_items):
        per_group = []
        for g in groups:
            gl = egls[i][:, ln(g, 0):ln(g, 0) + 1]
            for j in range(1, hpg):
                gl = jnp.where(srow < j * SSM_HEADDIM, gl, egls[i][:, ln(g, j):ln(g, j) + 1])
            per_group.append(gl)
        gl_cols.append(per_group)

    chains = [(s, g) for s in range(nseq) for g in groups]
    states = [ho_ref[s, g] for s, g in chains]
    for c in range(nchunk):
        r0 = c * chunk
        idx = [s * nchunk + c for s, _ in chains]
        yis = [_bdot(cds[i][g], st, NT) for i, (_, g), st in zip(idx, chains, states)]
        yield
        states = [st * gl_cols[i][g] + h_inc[i][g] for i, (_, g), st in zip(idx, chains, states)]
        for i, (s, g), yi in zip(idx, chains, yis):
            y_inter = yi[0:chunk]
            for j in range(1, hpg):
                y_inter = jnp.where(in_head[j], yi[j * chunk:(j + 1) * chunk], y_inter)
            y_ref[s, r0:r0 + chunk, g * gw:(g + 1) * gw] = (y_loc[i][g] + y_inter
                                                           + xgs[i][g] * drow[:, g * gw:(g + 1) * gw])
        yield
    for (s, g), st in zip(chains, states):
        ho_ref[s, g] = st


def _ssd_mixer(proj, nblk, blk0, rows, chunk, front_pad, nseq, h0, layer, cbuf, cw, cbias, prm, drow, stack=None):
    nb = proj.shape[0]
    col = lambda c: c[0] // c[1]
    phys = lambda i: (i + blk0) % nblk
    gw = (SSM_HEADS // SSM_GROUPS) * SSM_HEADDIM
    ho_shape, ho_spec, extra, extra_specs, aliases, slot, n_slots = _state_out(
        stack, layer, nb, nseq, (SSM_GROUPS, gw, SSM_STATE), 1, 8)
    return dict(
        body=functools.partial(_ssd_body, nseq=nseq, rows=rows, chunk=chunk, front_pad=front_pad,
                               slot=slot, n_slots=n_slots),
        grid=(nb // nseq, nblk),
        inputs=[proj, proj, h0, cbuf, cw, cbias, prm, drow] + extra,
        out_shape=[jax.ShapeDtypeStruct((nb, nblk * rows, SSM_INNER), F32),
                   jax.ShapeDtypeStruct(ho_shape, F32),
                   jax.ShapeDtypeStruct((nb, SUBLANES, SSM_CONV), F32)],
        in_specs=[
            pl.BlockSpec((nseq, rows, SSM_CONV), lambda b, i: (b, phys(i), col(COL_XBC))),
            pl.BlockSpec((nseq, rows, LANES), lambda b, i: (b, phys(i), col(COL_SM))),
            pl.BlockSpec((None, nseq, SSM_GROUPS, gw, SSM_STATE), lambda b, i: (layer, b, 0, 0, 0)),
            pl.BlockSpec((nseq, SUBLANES, SSM_CONV), lambda b, i: (b, 0, 0)),
            pl.BlockSpec((CONV_WIDTH, SSM_CONV), lambda b, i: (0, 0)),
            pl.BlockSpec((1, SSM_CONV), lambda b, i: (0, 0)),
            pl.BlockSpec((SUBLANES, LANES), lambda b, i: (0, 0)),
            pl.BlockSpec((1, SSM_INNER), lambda b, i: (0, 0)),
        ] + extra_specs,
        out_specs=[
            pl.BlockSpec((nseq, rows, SSM_INNER), lambda b, i: (b, phys(i), 0)),
            ho_spec,
            pl.BlockSpec((nseq, SUBLANES, SSM_CONV), lambda b, i: (b, 0, 0)),
        ],
        scratch=[pltpu.VMEM((nseq, SUBLANES, SSM_CONV), F32)],
        aliases=aliases,
    )


def _rope(x, cos, sin_signed):
    w = x.shape[-1]
    half = SWA_HEAD_DIM // 2
    lane = lax.broadcasted_iota(jnp.int32, (1, w), 1)
    first_half = (lane % SWA_HEAD_DIM) < half
    swapped = jnp.where(first_half, pltpu.roll(x, w - half, axis=1), pltpu.roll(x, half, axis=1))
    return x * cos + swapped * sin_signed


def _sink_attend(problems):
    scale = SWA_HEAD_DIM ** -0.5
    scores = [[jnp.where(m, _bdot(q, k, NT) * scale, NEG_BIG) for k, m in zip(keys, masks)]
              for q, keys, _, masks, _ in problems]
    yield
    outs = []
    probs, dens = [], []
    def lane_reduce(tiles, combine, reduce):
        merged = {}
        for t in tiles:
            merged[t.shape[-1]] = t if t.shape[-1] not in merged else combine(merged[t.shape[-1]], t)
        return [reduce(t, axis=-1, keepdims=True) for t in merged.values()]

    for (_, _, _, _, sink), ss in zip(problems, scores):
        mx = sink
        for m in lane_reduce(ss, jnp.maximum, jnp.max):
            mx = jnp.maximum(mx, m)
        probs.append([jnp.exp(s - mx) for s in ss])
        dens.append(jnp.exp(sink - mx))
    yield
    pvs = [[_bdot(p, v) for p, v in zip(ps, vals)] for (_, _, vals, _, _), ps in zip(problems, probs)]
    yield
    for pv, sink_term in zip(pvs, dens):
        acc = pv[0]
        for extra in pv[1:]:
            acc = acc + extra
        outs.append(acc / (pltpu.roll(acc, SWA_HEAD_DIM, axis=1) + sink_term))
    return outs


def _swa_head_order():
    grp = SWA_Q_HEADS // SWA_KV_HEADS
    assert SWA_KV_HEADS * SWA_HEAD_DIM == LANES
    return [j * grp + t for t in range(grp) for j in range(SWA_KV_HEADS)]


def _swa_problems(q, key_sets, val_sets, masks, prm, tq):
    grp = SWA_Q_HEADS // SWA_KV_HEADS
    row = lax.broadcasted_iota(jnp.int32, (grp * tq, 1), 0)
    lane = lax.broadcasted_iota(jnp.int32, (1, LANES), 1)
    qst = jnp.concatenate([q[:, t * LANES:(t + 1) * LANES] for t in range(grp)], axis=0).astype(BF16)
    problems = []
    for j in range(SWA_KV_HEADS):
        half = (lane >= j * SWA_HEAD_DIM) & (lane < (j + 1) * SWA_HEAD_DIM)
        sink = prm[PRM_SINK:PRM_SINK + 1, j * grp:j * grp + 1]
        for t in range(1, grp):
            sink = jnp.where(row < t * tq, sink, prm[PRM_SINK:PRM_SINK + 1, j * grp + t:j * grp + t + 1])
        problems.append((qst, [jnp.where(half, k, 0.0) for k in key_sets],
                         [jnp.where(half, v, 1.0) for v in val_sets], masks, sink))
    return problems


def _swa_tiles(outs, tq):
    grp = SWA_Q_HEADS // SWA_KV_HEADS
    lane = lax.broadcasted_iota(jnp.int32, (1, LANES), 1)
    tiles = []
    for t in range(grp):
        tile = outs[0][t * tq:(t + 1) * tq]
        for j in range(1, SWA_KV_HEADS):
            tile = jnp.where(lane < j * SWA_HEAD_DIM, tile, outs[j][t * tq:(t + 1) * tq])
        tiles.append(tile)
    return tiles


def _swa_prompt_body(ins, outs, scratch, *, nseq, front_pad):
    q_ref, k_ref, v_ref, cos_ref, sin_ref, prm_ref = ins
    o_ref, ko_ref, vo_ref = outs
    kprev, vprev = scratch
    blk = pl.program_id(1)

    @pl.when(blk == 0)
    def _():
        kprev[...] = jnp.zeros_like(kprev)
        vprev[...] = jnp.zeros_like(vprev)

    cos = cos_ref[...]
    sin = sin_ref[...]
    cos_q = jnp.concatenate([cos, cos], axis=1)
    sin_q = jnp.concatenate([sin, sin], axis=1)
    grp = SWA_Q_HEADS // SWA_KV_HEADS
    qi = lax.broadcasted_iota(jnp.int32, (grp * BLOCK, BLOCK), 0) % BLOCK
    kj = lax.broadcasted_iota(jnp.int32, (grp * BLOCK, BLOCK), 1)
    mask_cur = (kj <= qi) & (blk * BLOCK + kj >= front_pad)
    mask_prev = (kj > qi) & ((blk - 1) * BLOCK + kj >= front_pad)
    prm = prm_ref[...]
    problems = []
    for s in range(nseq):
        q = _rope(q_ref[s], cos_q, sin_q)
        k = _rope(k_ref[s], cos, sin)
        v = v_ref[s]
        problems += _swa_problems(q, (kprev[s], k), (vprev[s], v), (mask_prev, mask_cur), prm, BLOCK)
        kprev[s] = k
        vprev[s] = v
        ko_ref[s] = k
        vo_ref[s] = v
        yield
    res = yield from _sink_attend(problems)
    for s in range(nseq):
        for t, tile in enumerate(_swa_tiles(res[s * SWA_KV_HEADS:(s + 1) * SWA_KV_HEADS], BLOCK)):
            o_ref[s, :, t * LANES:(t + 1) * LANES] = tile.astype(o_ref.dtype)


def _swa_prompt(proj, nblk, blk0, front_pad, cos, sin, prm, out_dtype=F32):
    nb = proj.shape[0]
    col = lambda c: c[0] // c[1]
    phys = lambda i: (i + blk0) % nblk
    return dict(
        body=functools.partial(_swa_prompt_body, nseq=nb, front_pad=front_pad),
        grid=(1, nblk),
        inputs=[proj, proj, proj, cos, sin, prm],
        out_shape=[jax.ShapeDtypeStruct((nb, nblk * BLOCK, SWA_Q), out_dtype),
                   jax.ShapeDtypeStruct((nb, WINDOW, SWA_KV), F32),
                   jax.ShapeDtypeStruct((nb, WINDOW, SWA_KV), F32)],
        in_specs=[
            pl.BlockSpec((nb, BLOCK, SWA_Q), lambda b, i: (0, phys(i), col(COL_SWQ))),
            pl.BlockSpec((nb, BLOCK, SWA_KV), lambda b, i: (0, phys(i), col(COL_SWK))),
            pl.BlockSpec((nb, BLOCK, SWA_KV), lambda b, i: (0, phys(i), col(COL_SWV))),
            pl.BlockSpec((BLOCK, SWA_KV), lambda b, i: (i, 0)),
            pl.BlockSpec((BLOCK, SWA_KV), lambda b, i: (i, 0)),
            pl.BlockSpec((SUBLANES, LANES), lambda b, i: (0, 0)),
        ],
        out_specs=[
            pl.BlockSpec((nb, BLOCK, SWA_Q), lambda b, i: (0, phys(i), 0)),
            pl.BlockSpec((nb, WINDOW, SWA_KV), lambda b, i: (0, 0, 0)),
            pl.BlockSpec((nb, WINDOW, SWA_KV), lambda b, i: (0, 0, 0)),
        ],
        scratch=[pltpu.VMEM((nb, BLOCK, SWA_KV), F32), pltpu.VMEM((nb, BLOCK, SWA_KV), F32)],
        aliases={},
    )


def _swa_sample_body(ins, outs, scratch, *, nseq, steps, slot, n_slots):
    q_ref, k_ref, v_ref, kc_ref, vc_ref, cos_ref, sin_ref, prm_ref = ins[:8]
    o_ref, ko_full, vo_full = outs
    ko_ref = _state_view(ko_full, slot, n_slots)
    vo_ref = _state_view(vo_full, slot, n_slots)
    cos = cos_ref[...]
    sin = sin_ref[...]
    cos_q = jnp.concatenate([cos, cos], axis=1)
    sin_q = jnp.concatenate([sin, sin], axis=1)
    prm = prm_ref[...]
    grp = SWA_Q_HEADS // SWA_KV_HEADS
    ti = lax.broadcasted_iota(jnp.int32, (grp * steps, WINDOW), 0) % steps
    sj = lax.broadcasted_iota(jnp.int32, (grp * steps, WINDOW), 1)
    mask_cache = sj > ti
    tn = lax.broadcasted_iota(jnp.int32, (grp * steps, steps), 0) % steps
    sn = lax.broadcasted_iota(jnp.int32, (grp * steps, steps), 1)
    mask_new = sn <= tn
    problems = []
    for b in range(nseq):
        q = _rope(q_ref[b], cos_q, sin_q)
        k = _rope(k_ref[b], cos, sin)
        v = v_ref[b]
        kc = kc_ref[b]
        vc = vc_ref[b]
        ko_ref[b, 0:WINDOW - steps, :] = kc[steps:WINDOW, :]
        ko_ref[b, WINDOW - steps:WINDOW, :] = k
        vo_ref[b, 0:WINDOW - steps, :] = vc[steps:WINDOW, :]
        vo_ref[b, WINDOW - steps:WINDOW, :] = v
        problems += _swa_problems(q, (kc, k), (vc, v), (mask_cache, mask_new), prm, steps)
        yield
    res = yield from _sink_attend(problems)
    for b in range(nseq):
        for t, tile in enumerate(_swa_tiles(res[b * SWA_KV_HEADS:(b + 1) * SWA_KV_HEADS], steps)):
            o_ref[b, :, t * LANES:(t + 1) * LANES] = tile


def _swa_sample(proj, nseq, kc, vc, layer, cos, sin, prm, stack_k=None, stack_v=None):
    nb, steps, _ = proj.shape
    assert WINDOW > steps
    col = lambda c: c[0] // c[1]
    ko_shape, ko_spec, extra_k, specs_k, alias_k, slot, n_slots = _state_out(
        stack_k, layer, nb, nseq, (WINDOW, SWA_KV), 1, 8)
    vo_shape, vo_spec, extra_v, specs_v, alias_v, _, _ = _state_out(
        stack_v, layer, nb, nseq, (WINDOW, SWA_KV), 2, 8 + len(extra_k))
    return dict(
        body=functools.partial(_swa_sample_body, nseq=nseq, steps=steps, slot=slot, n_slots=n_slots),
        grid=(nb // nseq, 1),
        inputs=[proj, proj, proj, kc, vc, cos, sin, prm] + extra_k + extra_v,
        out_shape=[jax.ShapeDtypeStruct((nb, steps, SWA_Q), F32),
                   jax.ShapeDtypeStruct(ko_shape, F32),
                   jax.ShapeDtypeStruct(vo_shape, F32)],
        in_specs=[
            pl.BlockSpec((nseq, steps, SWA_Q), lambda b, i: (b, 0, col(COL_SWQ))),
            pl.BlockSpec((nseq, steps, SWA_KV), lambda b, i: (b, 0, col(COL_SWK))),
            pl.BlockSpec((nseq, steps, SWA_KV), lambda b, i: (b, 0, col(COL_SWV))),
            pl.BlockSpec((None, nseq, WINDOW, SWA_KV), lambda b, i: (layer, b, 0, 0)),
            pl.BlockSpec((None, nseq, WINDOW, SWA_KV), lambda b, i: (layer, b, 0, 0)),
            pl.BlockSpec((steps, SWA_KV), lambda b, i: (0, 0)),
            pl.BlockSpec((steps, SWA_KV), lambda b, i: (0, 0)),
            pl.BlockSpec((SUBLANES, LANES), lambda b, i: (0, 0)),
        ] + specs_k + specs_v,
        out_specs=[pl.BlockSpec((nseq, steps, SWA_Q), lambda b, i: (b, 0, 0)), ko_spec, vo_spec],
        scratch=[],
        aliases={**alias_k, **alias_v},
    )


def _tail_kernel(x_ref, odn_ref, zdn_ref, y_ref, zss_ref, osw_ref, dnw_ref, ssw_ref, wout_ref, g1_ref, g2_ref, g3_ref,
                 wfi_ref, wfo_ref, o_ref, *, d_ff, tf):
    odn = odn_ref[...]
    dnw = dnw_ref[...]
    dn = jnp.concatenate([_rmsnorm(odn[:, h * DN_DV:(h + 1) * DN_DV], dnw) for h in range(DN_HEADS)], axis=1)
    dn = dn * _silu(zdn_ref[...])
    yg = y_ref[...] * _silu(zss_ref[...])
    ssw = ssw_ref[...]
    gw = SSM_INNER // SSM_GROUPS
    ys = jnp.concatenate([_rmsnorm(yg[:, g * gw:(g + 1) * gw], ssw[:, g * gw:(g + 1) * gw]) for g in range(SSM_GROUPS)],
                         axis=1)
    mixed = jnp.concatenate([dn.astype(BF16), ys.astype(BF16), osw_ref[...].astype(BF16)], axis=1)
    m = jnp.dot(mixed, wout_ref[...], preferred_element_type=F32)
    x1 = x_ref[...] + _rmsnorm(m, g1_ref[...])
    h = _rmsnorm(x1, g2_ref[...]).astype(BF16)
    y2 = None
    for c in range(d_ff // tf):
        gate = jnp.dot(h, wfi_ref[:, c * tf:(c + 1) * tf], preferred_element_type=F32)
        up = jnp.dot(h, wfi_ref[:, d_ff + c * tf:d_ff + (c + 1) * tf], preferred_element_type=F32)
        part = jnp.dot((_silu(gate) * up).astype(BF16), wfo_ref[c * tf:(c + 1) * tf, :], preferred_element_type=F32)
        y2 = part if y2 is None else y2 + part
    o_ref[...] = x1 + _rmsnorm(y2, g3_ref[...])


def _layer_spec(shape, layer):
    nd = len(shape) - 1
    return pl.BlockSpec((None,) + tuple(shape[1:]), lambda *_: (layer,) + (0,) * nd, pipeline_mode=pl.Buffered(1))


def _tail(x, proj, odn, y, osw, dnw, ssw, wout, g1, g2, g3, wfi, wfo, layer, l_out, tm_target):
    nb, _, d = x.shape
    d_ff = wfo.shape[1]
    tm = _pick_tile(l_out, tm_target)
    tf = 2 * LANES if d_ff % (2 * LANES) == 0 else d_ff
    row = lambda w: pl.BlockSpec((None, tm, w), lambda b, i: (b, i, 0))
    gate = lambda c: pl.BlockSpec((None, tm, c[1]), lambda b, i: (b, i, c[0] // c[1]))
    return pl.pallas_call(
        functools.partial(_tail_kernel, d_ff=d_ff, tf=tf),
        out_shape=jax.ShapeDtypeStruct((nb, l_out, d), F32),
        grid=(nb, l_out // tm),
        in_specs=[row(d), row(DN_V), gate(COL_DNZ), row(SSM_INNER), gate(COL_SSZ), row(SWA_Q),
                  _layer_spec(dnw.shape, layer), _layer_spec(ssw.shape, layer), _layer_spec(wout.shape, layer),
                  _layer_spec(g1.shape, layer), _layer_spec(g2.shape, layer), _layer_spec(g3.shape, layer),
                  _layer_spec(wfi.shape, layer), _layer_spec(wfo.shape, layer)],
        out_specs=row(d),
        compiler_params=pltpu.CompilerParams(dimension_semantics=("arbitrary", "arbitrary"),
                                             vmem_limit_bytes=VMEM_LIMIT),
        name="outproj_ffn",
    )(x, odn, proj, y, proj, osw, dnw, ssw, wout, g1, g2, g3, wfi, wfo)


def _reorder_w_in_kernel(w_ref, o_ref):
    w = w_ref[0]
    offs = [0]
    for wd in IN_WIDTHS:
        offs.append(offs[-1] + wd)
    seg = lambda i: w[:, offs[i]:offs[i + 1]]
    dn_qkv, dn_z, dn_b, dn_a, ssm_xbc, ssm_z, ssm_dt, sw_q, sw_k, sw_v = (seg(i) for i in range(len(IN_WIDTHS)))
    n_small = dn_b.shape[1] + dn_a.shape[1] + ssm_dt.shape[1]
    small = jnp.concatenate([dn_b, dn_a, ssm_dt, jnp.zeros((w.shape[0], LANES - n_small), w.dtype)], axis=1)
    sw_q = jnp.concatenate([sw_q[:, h * SWA_HEAD_DIM:(h + 1) * SWA_HEAD_DIM] for h in _swa_head_order()], axis=1)
    o_ref[0] = jnp.concatenate([dn_qkv, dn_z, ssm_z, ssm_xbc, sw_q, sw_k, sw_v, small], axis=1).astype(BF16)


def _reorder_w_in(w):
    depth, d, d_in = w.shape
    assert d_in == sum(IN_WIDTHS)
    tr = _pick_tile(d, 256)
    return pl.pallas_call(
        _reorder_w_in_kernel,
        out_shape=jax.ShapeDtypeStruct((depth, d, D_PROJ), BF16),
        grid=(depth, d // tr),
        in_specs=[pl.BlockSpec((1, tr, d_in), lambda l, i: (l, i, 0))],
        out_specs=pl.BlockSpec((1, tr, D_PROJ), lambda l, i: (l, i, 0)),
        compiler_params=pltpu.CompilerParams(dimension_semantics=("arbitrary", "arbitrary")),
        name="reorder_w_in",
    )(w)


def _scalar_param_tiles(rows):
    depth = rows[0][1].shape[0]
    padded = [jnp.pad(v.astype(F32), ((0, 0), (off, LANES - off - v.shape[1]))) for off, v in rows]
    padded += [jnp.zeros((depth, LANES), F32)] * (SUBLANES - len(rows))
    return jnp.stack(padded, axis=1)


def _rope_tables(pos):
    half = SWA_HEAD_DIM // 2
    inv = ROPE_THETA ** (-jnp.arange(half, dtype=F32) / half)
    ang = pos.astype(F32)[:, None] * inv[None, :]
    cos = jnp.cos(ang)
    sin = jnp.sin(ang)
    cos_t = jnp.concatenate([cos, cos] * SWA_KV_HEADS, axis=1)
    sin_t = jnp.concatenate([-sin, sin] * SWA_KV_HEADS, axis=1)
    return cos_t, sin_t


def kernel(x_prompt, x_sample, state_dn, state_dn_conv, state_ssm, state_ssm_conv, cache_swa_k, cache_swa_v,
           meta_tokens, w_in, dn_conv_w, dn_a_log, dn_dt_bias, dn_norm_w, ssm_conv_w, ssm_conv_b, ssm_a_log,
           ssm_dt_bias, ssm_d, ssm_norm_w, swa_sinks, w_out, g_pre_mix, g_post_mix, g_pre_ffn, g_post_ffn,
           w_ffn_in, w_ffn_out):
    bp, seq, d = x_prompt.shape
    bs, ts, _ = x_sample.shape
    depth = w_in.shape[0]
    lp = N_META + seq + FRONT_PAD
    assert lp % BLOCK == 0 and BLOCK % CHUNK == 0 and BLOCK % SSD_CHUNK == 0 and seq % BLOCK == 0
    nblk = lp // BLOCK
    blk0 = nblk - 1
    tm0 = _pick_tile(seq, TM_FIRST)
    assert tm0 >= BLOCK
    x_front = jnp.concatenate([jnp.zeros((FRONT_PAD, d), x_prompt.dtype), meta_tokens.astype(x_prompt.dtype),
                               jnp.zeros((tm0 - BLOCK, d), x_prompt.dtype)], axis=0)
    pad_first = ((seq, seq + FRONT_PAD), (lp, seq + tm0))
    pad_range = ((seq, seq + FRONT_PAD),)
    xp = None
    xs = x_sample.reshape(1, bs * ts, d)

    cos_p, sin_p = _rope_tables(jnp.arange(lp, dtype=jnp.int32) - FRONT_PAD)
    cos_s, sin_s = _rope_tables(PAST_LEN + jnp.arange(ts, dtype=jnp.int32))

    gw = (SSM_HEADS // SSM_GROUPS) * SSM_HEADDIM
    nseq_s = _pick_tile(bs, 8) if bs % SUBLANES == 0 else bs
    zero_dn = jnp.zeros((1, bp, DN_HEADS, DN_DK, DN_DV), F32)
    zero_dnc = jnp.zeros((bp, SUBLANES, DN_CONV), F32)
    zero_ssm = jnp.zeros((1, bp, SSM_GROUPS, gw, SSM_STATE), F32)
    zero_ssmc = jnp.zeros((bp, SUBLANES, SSM_CONV), F32)
    state_ssm_g = state_ssm.reshape(depth, bs, SSM_GROUPS, gw, SSM_STATE)
    cache_k = cache_swa_k.reshape(depth, bs, WINDOW, SWA_KV)
    cache_v = cache_swa_v.reshape(depth, bs, WINDOW, SWA_KV)

    w_in_r = _reorder_w_in(w_in)
    swa0 = DN_V + SSM_INNER
    w_out_b = jnp.concatenate(
        [w_out[:, :swa0]] + [w_out[:, swa0 + h * SWA_HEAD_DIM:swa0 + (h + 1) * SWA_HEAD_DIM] for h in _swa_head_order()],
        axis=1).astype(BF16)
    w_fi_b = w_ffn_in.astype(BF16)
    w_fo_b = w_ffn_out.astype(BF16)
    g1, g2, g3, g4 = (a[:, None, :] for a in (g_pre_mix, g_post_mix, g_pre_ffn, g_post_ffn))
    dn_nw = dn_norm_w[:, None, :]
    ssm_nw = ssm_norm_w[:, None, :]

    new_p, new_s = [], []
    dn_s = ssm_s = k_s = v_s = None
    prm_all = _scalar_param_tiles([(SM_A, dn_a_log), (SM_A, dn_dt_bias), (SM_DT, ssm_a_log), (SM_DT, ssm_dt_bias),
                                   (0, swa_sinks)])
    drow_all = jnp.repeat(ssm_d, SSM_HEADDIM, axis=1)
    dn_cbuf_s = jnp.pad(state_dn_conv, ((0, 0), (0, 0), (SUBLANES - (CONV_WIDTH - 1), 0), (0, 0)))
    ssm_cbuf_s = jnp.pad(state_ssm_conv, ((0, 0), (0, 0), (SUBLANES - (CONV_WIDTH - 1), 0), (0, 0)))
    for l in range(depth):
        prm = prm_all[l]
        drow = drow_all[l][None, :]
        cbias = ssm_conv_b[l][None, :]
        last = l == depth - 1

        if l == 0:
            proj, xp = _inproj_first(x_prompt, x_front, g1, w_in_r, l, pad_first, tm0)
        else:
            proj = _inproj(xp, g1, w_in_r, l, pad_range, TM_DENSE)
        (odn, dn_p, dnc_p), = _run_parts([_dn_mixer(proj, nblk, blk0, BLOCK, CHUNK, FRONT_PAD, bp, zero_dn, 0,
                                                    zero_dnc, dn_conv_w[l], prm)], "dn_mixer")
        (ys, ssm_p, ssmc_p), = _run_parts([_ssd_mixer(proj, nblk, blk0, BLOCK, SSD_CHUNK, FRONT_PAD, bp, zero_ssm, 0,
                                                      zero_ssmc, ssm_conv_w[l], cbias, prm, drow)], "ssd_mixer")
        (osw, k_p, v_p), = _run_parts([_swa_prompt(proj, nblk, blk0, FRONT_PAD, cos_p, sin_p, prm, out_dtype=BF16)],
                                      "swa_prompt")
        xp = _tail(xp, proj, odn, ys, osw, dn_nw, ssm_nw, w_out_b, g2, g3, g4, w_fi_b, w_fo_b, l,
                   seq if last else lp, TM_DENSE)
        new_p.append((dn_p, dnc_p[:, -(CONV_WIDTH - 1):], ssm_p.reshape(bp, SSM_HEADS, SSM_HEADDIM, SSM_STATE),
                      ssmc_p[:, -(CONV_WIDTH - 1):], k_p.reshape(bp, WINDOW, SWA_KV_HEADS, SWA_HEAD_DIM),
                      v_p.reshape(bp, WINDOW, SWA_KV_HEADS, SWA_HEAD_DIM)))

        proj_s = _inproj(xs, g1, w_in_r, l, None, TM_DENSE)
        proj = proj_s.reshape(bs, ts, D_PROJ)
        (odn, dn_s, dnc_s), (ys, ssm_s, ssmc_s), (osw, k_s, v_s) = _run_parts([
            _dn_mixer(proj, 1, 0, ts, ts, 0, nseq_s, state_dn, l, dn_cbuf_s[l], dn_conv_w[l], prm,
                      stack=(depth, dn_s)),
            _ssd_mixer(proj, 1, 0, ts, ts, 0, nseq_s, state_ssm_g, l, ssm_cbuf_s[l], ssm_conv_w[l], cbias, prm, drow,
                       stack=(depth, ssm_s)),
            _swa_sample(proj, nseq_s, cache_k, cache_v, l, cos_s, sin_s, prm,
                        stack_k=(depth, k_s), stack_v=(depth, v_s))], "mixers_sample")
        flat = lambda a: a.reshape(1, bs * ts, a.shape[-1])
        xs = _tail(xs, proj_s, flat(odn), flat(ys), flat(osw), dn_nw, ssm_nw, w_out_b, g2, g3, g4, w_fi_b, w_fo_b, l,
                   bs * ts, TM_DENSE)
        new_s.append((dnc_s[:, -(CONV_WIDTH - 1):], ssmc_s[:, -(CONV_WIDTH - 1):]))

    outs_p = tuple(jnp.stack([st[i] for st in new_p]) for i in range(6))
    dnc_s, ssmc_s = (jnp.stack([st[i] for st in new_s]) for i in range(2))
    outs_s = (dn_s, dnc_s, ssm_s.reshape(depth, bs, SSM_HEADS, SSM_HEADDIM, SSM_STATE), ssmc_s,
              k_s.reshape(depth, bs, WINDOW, SWA_KV_HEADS, SWA_HEAD_DIM),
              v_s.reshape(depth, bs, WINDOW, SWA_KV_HEADS, SWA_HEAD_DIM))
    return (xp, xs.reshape(bs, ts, d)) + outs_p + outs_s
```

```python
import functools

import jax
import jax.numpy as jnp
from jax import lax
from jax.experimental import pallas as pl
from jax.experimental.pallas import tpu as pltpu

F32 = jnp.float32
BF16 = jnp.bfloat16
NT = (((1,), (1,)), ((), ()))
TN = (((0,), (0,)), ((), ()))

N_META = 16
CONV_WIDTH = 4
CHUNK = 128
BLOCK = 128
WINDOW = 128
FRONT_PAD = BLOCK - N_META
ROPE_THETA = 10000.0
PAST_LEN = 8192
EPS = 1e-6

DN_HEADS, DN_DK, DN_DV = 4, 128, 128
DN_QK = DN_HEADS * DN_DK
DN_V = DN_HEADS * DN_DV
DN_CONV = 2 * DN_QK + DN_V
SSM_HEADS, SSM_HEADDIM, SSM_GROUPS, SSM_STATE = 4, 64, 2, 128
SSM_INNER = SSM_HEADS * SSM_HEADDIM
SSM_BC = SSM_GROUPS * SSM_STATE
SSM_CONV = SSM_INNER + 2 * SSM_BC
SWA_Q_HEADS, SWA_KV_HEADS, SWA_HEAD_DIM = 4, 2, 64
SWA_Q = SWA_Q_HEADS * SWA_HEAD_DIM
SWA_KV = SWA_KV_HEADS * SWA_HEAD_DIM
IN_WIDTHS = (DN_CONV, DN_V, DN_HEADS, DN_HEADS, SSM_CONV, SSM_INNER, SSM_HEADS, SWA_Q, SWA_KV, SWA_KV)

LANES = 128
SUBLANES = 8
COL_QKV = (0, DN_CONV)
COL_DNZ = (COL_QKV[0] + DN_CONV, DN_V)
COL_SSZ = (COL_DNZ[0] + DN_V, SSM_INNER)
COL_XBC = (COL_SSZ[0] + SSM_INNER, SSM_CONV)
COL_SWQ = (COL_XBC[0] + SSM_CONV, SWA_Q)
COL_SWK = (COL_SWQ[0] + SWA_Q, SWA_KV)
COL_SWV = (COL_SWK[0] + SWA_KV, SWA_KV)
COL_SM = (COL_SWV[0] + SWA_KV, LANES)
D_PROJ = COL_SM[0] + LANES
assert all(off % width == 0 for off, width in (COL_QKV, COL_DNZ, COL_SSZ, COL_XBC, COL_SWQ, COL_SWK, COL_SWV, COL_SM))
SM_B, SM_A, SM_DT = 0, DN_HEADS, 2 * DN_HEADS
PRM_DN_ALOG, PRM_DN_DTB, PRM_SSM_ALOG, PRM_SSM_DTB, PRM_SINK = 0, 1, 2, 3, 4
NEG_BIG = -1e30
VMEM_LIMIT = 56 * 1024 * 1024
TM_DENSE = 640
TM_FIRST = 512


def _bdot(a, b, dims=None):
    a = a.astype(BF16)
    b = b.astype(BF16)
    if dims is None:
        return jnp.dot(a, b, preferred_element_type=F32)
    return lax.dot_general(a, b, dims, preferred_element_type=F32)


def _cumsum_rows(lmat, g):
    hi = g.astype(BF16)
    r1 = g - hi.astype(F32)
    mid = r1.astype(BF16)
    lo = (r1 - mid.astype(F32)).astype(BF16)
    dot = lambda part: jnp.dot(lmat, part, preferred_element_type=F32)
    return dot(hi) + dot(mid) + dot(lo)


def _rmsnorm(x, g):
    return x * lax.rsqrt(jnp.mean(x * x, axis=-1, keepdims=True) + EPS) * g


def _l2norm(x):
    return x * lax.rsqrt(jnp.sum(x * x, axis=-1, keepdims=True) + EPS)


def _sigmoid(x):
    return 0.5 * jnp.tanh(0.5 * x) + 0.5


def _silu(x):
    half = 0.5 * x
    return half * jnp.tanh(half) + half


def _softplus(x):
    return jnp.maximum(x, 0.0) + jnp.log1p(jnp.exp(-jnp.abs(x)))


def _pick_tile(n, target):
    best = None
    for t in range(SUBLANES, min(n, target) + 1, SUBLANES):
        if n % t == 0:
            best = t
    assert best is not None, n
    return best


def _project_rows(x, g, w, tm, pad_range):
    h = _rmsnorm(x, g)
    if pad_range:
        r = pl.program_id(1) * tm + lax.broadcasted_iota(jnp.int32, (tm, 1), 0)
        is_pad = None
        for lo, hi in pad_range:
            hit = (r >= lo) & (r < hi)
            is_pad = hit if is_pad is None else is_pad | hit
        h = jnp.where(is_pad, 0.0, h)
    return jnp.dot(h.astype(BF16), w, preferred_element_type=F32)


def _inproj_kernel(x_ref, g_ref, w_ref, o_ref, *, tm, pad_range):
    o_ref[...] = _project_rows(x_ref[...], g_ref[...], w_ref[...], tm, pad_range)


def _inproj_first_kernel(xm_ref, xt_ref, g_ref, w_ref, o_ref, xo_ref, *, tm, n_main, pad_range):
    x = jnp.where(pl.program_id(1) == n_main, xt_ref[...], xm_ref[...])
    xo_ref[...] = x
    o_ref[...] = _project_rows(x, g_ref[...], w_ref[...], tm, pad_range)


def _inproj_first(x_main, x_tail, g, w, layer, pad_range, tm):
    nb, rows, d = x_main.shape
    assert rows % tm == 0 and x_tail.shape == (tm, d)
    n_main = rows // tm
    return pl.pallas_call(
        functools.partial(_inproj_first_kernel, tm=tm, n_main=n_main, pad_range=pad_range),
        out_shape=(jax.ShapeDtypeStruct((nb, rows + tm, D_PROJ), F32), jax.ShapeDtypeStruct((nb, rows + tm, d), F32)),
        grid=(nb, n_main + 1),
        in_specs=[pl.BlockSpec((None, tm, d), lambda b, i: (b, jnp.minimum(i, n_main - 1), 0)),
                  pl.BlockSpec((tm, d), lambda b, i: (0, 0)),
                  _layer_spec(g.shape, layer), _layer_spec(w.shape, layer)],
        out_specs=(pl.BlockSpec((None, tm, D_PROJ), lambda b, i: (b, i, 0)),
                   pl.BlockSpec((None, tm, d), lambda b, i: (b, i, 0))),
        compiler_params=pltpu.CompilerParams(dimension_semantics=("arbitrary", "arbitrary"),
                                             vmem_limit_bytes=VMEM_LIMIT),
        name="inproj_first",
    )(x_main, x_tail, g, w)


def _inproj(x, g, w, layer, pad_range, tm_target):
    nb, rows, d = x.shape
    tm = _pick_tile(rows, tm_target)
    return pl.pallas_call(
        functools.partial(_inproj_kernel, tm=tm, pad_range=pad_range),
        out_shape=jax.ShapeDtypeStruct((nb, rows, D_PROJ), F32),
        grid=(nb, rows // tm),
        in_specs=[pl.BlockSpec((None, tm, d), lambda b, i: (b, i, 0)), _layer_spec(g.shape, layer),
                  _layer_spec(w.shape, layer)],
        out_specs=pl.BlockSpec((None, tm, D_PROJ), lambda b, i: (b, i, 0)),
        compiler_params=pltpu.CompilerParams(dimension_semantics=("arbitrary", "arbitrary"),
                                             vmem_limit_bytes=VMEM_LIMIT),
        name="inproj",
    )(x, g, w)


def _causal_conv(xbuf, raw_ref, cw_ref, cbo_ref, rows):
    x = raw_ref[...]
    prev = xbuf[0:SUBLANES, :]
    cw = cw_ref[...]
    row = lax.broadcasted_iota(jnp.int32, (SUBLANES, 1), 0)
    acc = None
    for i in range(CONV_WIDTH):
        s = CONV_WIDTH - 1 - i
        if s == 0:
            xs = x
        else:
            r = pltpu.roll(x, s, axis=0)
            head = jnp.where(row < s, pltpu.roll(prev, s, axis=0), r[0:SUBLANES])
            xs = head if rows == SUBLANES else jnp.concatenate([head, r[SUBLANES:]], axis=0)
        term = xs * cw[i:i + 1, :]
        acc = term if acc is None else acc + term
    tail = x[rows - SUBLANES:rows]
    cbo_ref[...] = tail
    xbuf[0:SUBLANES, :] = tail
    return acc


def _causal_mask(c):
    return lax.broadcasted_iota(jnp.int32, (c, c), 0) >= lax.broadcasted_iota(jnp.int32, (c, c), 1)


def _segment_decay(gc, gct, lane, ge):
    col = gc[:, lane:lane + 1]
    row = gct[lane:lane + 1, :]
    return jnp.where(ge, jnp.exp(jnp.where(ge, col - row, 0.0)), 0.0)


def _inv_lane_tile_blocks(a_list, c):
    half = c // 2
    on_diag = (lax.broadcasted_iota(jnp.int32, (c, c), 0) // half) == (lax.broadcasted_iota(jnp.int32, (c, c), 1) // half)
    ds = [jnp.where(on_diag, a, 0.0) for a in a_list]
    ls = [jnp.where(on_diag, 0.0, a) for a in a_list]
    mds = yield from _inv_unit_lower_minus_eye(ds, c, 1, order=half)
    bs = [l + _bdot(md, l) for md, l in zip(mds, ls)]
    yield
    xs = [b + _bdot(b, md) for b, md in zip(bs, mds)]
    yield
    return [md - x for md, x in zip(mds, xs)]


def _inv_unit_lower_minus_eye(a_list, c, nh, order=None):
    if nh > 1 and c % LANES == 0:
        flat = [a[:, h * c:(h + 1) * c] for a in a_list for h in range(nh)]
        ys = yield from _inv_lane_tile_blocks(flat, c)
        return [jnp.concatenate(ys[i * nh:(i + 1) * nh], axis=1) for i in range(len(a_list))]
    order = c if order is None else order
    w = nh * c
    blk_r = lax.broadcasted_iota(jnp.int32, (w, w), 0) // c
    blk_c = lax.broadcasted_iota(jnp.int32, (w, w), 1) // c
    same = blk_r == blk_c

    def block_diag(p):
        if nh == 1:
            return p.astype(BF16)
        return jnp.where(same, jnp.concatenate([p] * nh, axis=0), 0.0).astype(BF16)

    ys = [-a for a in a_list]
    ps = [_bdot(a, block_diag(a)) for a in a_list]
    yield
    n = 2
    while n < order:
        pbds = [block_diag(p) for p in ps]
        n *= 2
        if n < order:
            sts = [_bdot(jnp.concatenate([y, p], axis=0), pbd) for y, p, pbd in zip(ys, ps, pbds)]
            ys = [y + p + st[:c] for y, p, st in zip(ys, ps, sts)]
            ps = [st[c:] for st in sts]
        else:
            ys = [y + p + _bdot(y, pbd) for y, p, pbd in zip(ys, ps, pbds)]
        yield
    return ys


def _run_parts(parts, name):
    grid = parts[0]["grid"]
    assert all(p["grid"] == grid for p in parts)
    n_in = [len(p["inputs"]) for p in parts]
    n_out = [len(p["out_shape"]) for p in parts]
    n_scr = [len(p["scratch"]) for p in parts]
    aliases = {}
    for k, p in enumerate(parts):
        for i, o in p["aliases"].items():
            aliases[sum(n_in[:k]) + i] = sum(n_out[:k]) + o

    def kernel(*refs):
        ins = refs[:sum(n_in)]
        outs = refs[sum(n_in):sum(n_in) + sum(n_out)]
        scr = refs[sum(n_in) + sum(n_out):]
        gens = [p["body"](ins[sum(n_in[:k]):sum(n_in[:k + 1])], outs[sum(n_out[:k]):sum(n_out[:k + 1])],
                          scr[sum(n_scr[:k]):sum(n_scr[:k + 1])]) for k, p in enumerate(parts)]
        for tag in gens[0]:
            if tag == "chain":
                break
        live = list(gens)
        while live:
            for g in list(live):
                if next(g, StopIteration) is StopIteration:
                    live.remove(g)

    results = pl.pallas_call(
        kernel,
        out_shape=tuple(s for p in parts for s in p["out_shape"]),
        grid=grid,
        in_specs=[s for p in parts for s in p["in_specs"]],
        out_specs=tuple(s for p in parts for s in p["out_specs"]),
        scratch_shapes=[s for p in parts for s in p["scratch"]],
        input_output_aliases=aliases,
        compiler_params=pltpu.CompilerParams(dimension_semantics=("arbitrary",) * len(grid),
                                             vmem_limit_bytes=VMEM_LIMIT),
        name=name,
    )(*[a for p in parts for a in p["inputs"]])
    return [list(results[sum(n_out[:k]):sum(n_out[:k + 1])]) for k in range(len(parts))]


def _state_out(stack, layer, nb, nseq, tail, out_index, n_inputs):
    zeros = (0,) * len(tail)
    if stack is None:
        return ((nb,) + tail, pl.BlockSpec((nseq,) + tail, lambda b, *_: (b,) + zeros), [], [], {}, None, 0)
    depth, prev = stack
    shape = (depth, nb) + tail
    if prev is None:
        spec = pl.BlockSpec((depth, nseq) + tail, lambda b, *_: (0, b) + zeros)
        return (shape, spec, [], [], {}, layer, depth)
    spec = pl.BlockSpec((None, nseq) + tail, lambda b, *_: (layer, b) + zeros)
    return (shape, spec, [prev], [pl.BlockSpec(memory_space=pl.ANY)], {n_inputs: out_index}, None, 0)


def _state_view(ref, slot, n_slots):
    if slot is None:
        return ref
    for other in range(n_slots):
        if other != slot:
            ref[other] = jnp.zeros(ref.shape[1:], ref.dtype)
    return ref.at[slot]


def _dn_prep(items, lmat, ge, gt_all, chunk):
    heads = range(DN_HEADS)
    gcs = [_cumsum_rows(lmat, g_all) for _, _, g_all in items]
    gcts = [gc.T for gc in gcs]
    qs = [[_l2norm(qkv[:, h * DN_DK:(h + 1) * DN_DK]) * DN_DK ** -0.5 for h in heads] for qkv, _, _ in items]
    ks = [[_l2norm(qkv[:, DN_QK + h * DN_DK:DN_QK + (h + 1) * DN_DK]) for h in heads] for qkv, _, _ in items]
    vs = [[qkv[:, 2 * DN_QK + h * DN_DV:2 * DN_QK + (h + 1) * DN_DV] for h in heads] for qkv, _, _ in items]
    betas = [[beta_all[:, SM_B + h:SM_B + h + 1] for h in heads] for _, beta_all, _ in items]
    yield
    kbs = [[k.astype(BF16) for k in kk] for kk in ks]
    kks = [jnp.concatenate([_bdot(kb, kb, NT) for kb in kb4], axis=1) for kb4 in kbs]
    qks = [jnp.concatenate([_bdot(q, kb, NT) for q, kb in zip(q4, kb4)], axis=1) for q4, kb4 in zip(qs, kbs)]
    decs = [jnp.concatenate([_segment_decay(gc, gct, SM_A + h, ge) for h in heads], axis=1)
            for gc, gct in zip(gcs, gcts)]
    beta_ws = [jnp.concatenate([jnp.broadcast_to(b, (chunk, chunk)) for b in b4], axis=1) for b4 in betas]
    a_list = [jnp.where(gt_all, bw * kk * dec, 0.0) for bw, kk, dec in zip(beta_ws, kks, decs)]
    yield "chain"
    n = len(items)
    egs, ekds, egls, attns, rhss, qds, kds = ([None] * n for _ in range(7))

    def side_work(i):
        gc = gcs[i]
        glast = gc[chunk - 1:chunk, :]
        egs[i] = jnp.exp(gc)
        ekds[i] = jnp.exp(glast - gc)
        egls[i] = jnp.exp(glast)
        attns[i] = (qks[i] * decs[i]).astype(BF16)
        lane = lambda a, h: a[:, SM_A + h:SM_A + h + 1]
        rhss[i] = [jnp.concatenate([vs[i][h] * betas[i][h], ks[i][h] * (betas[i][h] * lane(egs[i], h))], axis=1)
                   for h in heads]
        qds[i] = [(qs[i][h] * lane(egs[i], h)).astype(BF16) for h in heads]
        kds[i] = [(ks[i][h] * lane(ekds[i], h)).astype(BF16) for h in heads]

    inverse = _inv_unit_lower_minus_eye(a_list, chunk, DN_HEADS)
    todo = list(range(n))
    while True:
        try:
            next(inverse)
        except StopIteration as done:
            tms = done.value
            break
        if todo:
            side_work(todo.pop(0))
        yield
    for i in todo:
        side_work(i)
    uws = [[rhss[i][h] + _bdot(tms[i][:, h * chunk:(h + 1) * chunk], rhss[i][h]) for h in heads] for i in range(n)]
    yield
    out = []
    for i in range(n):
        per_head = []
        for h in heads:
            uw = uws[i][h]
            wq = jnp.concatenate([uw[:, DN_DV:].astype(BF16), qds[i][h]], axis=0)
            per_head.append((uw[:, :DN_DV], wq, attns[i][:, h * chunk:(h + 1) * chunk], kds[i][h],
                             egls[i][:, SM_A + h:SM_A + h + 1]))
        out.append(per_head)
    return out


def _dn_body(ins, outs, scratch, *, nseq, rows, chunk, front_pad, slot, n_slots):
    qkv_ref, sm_ref, s0_ref, cb_ref, cw_ref, prm_ref = ins[:6]
    o_ref, so_full, cbo_ref = outs
    xbuf, = scratch
    so_ref = _state_view(so_full, slot, n_slots)
    blk = pl.program_id(1)
    first = blk == 0

    @pl.when(first)
    def _():
        so_ref[...] = s0_ref[...]
        xbuf[:, 0:SUBLANES, :] = cb_ref[...]

    prm = prm_ref[...]
    ge = _causal_mask(chunk)
    lmat = jnp.where(ge, 1.0, 0.0).astype(BF16)
    wide = (chunk, DN_HEADS * chunk)
    gt_all = lax.broadcasted_iota(jnp.int32, wide, 0) > lax.broadcasted_iota(jnp.int32, wide, 1) % chunk
    nchunk = rows // chunk

    items = []
    for s in range(nseq):
        qkv = _silu(_causal_conv(xbuf.at[s], qkv_ref.at[s], cw_ref, cbo_ref.at[s], rows))
        sm = sm_ref[s]
        beta_all = _sigmoid(sm)
        g_all = -jnp.exp(prm[PRM_DN_ALOG:PRM_DN_ALOG + 1, :]) * _softplus(sm + prm[PRM_DN_DTB:PRM_DN_DTB + 1, :])
        if front_pad:
            pos = blk * rows + lax.broadcasted_iota(jnp.int32, (rows, 1), 0)
            g_all = jnp.where(pos < front_pad, 0.0, g_all)
        for c in range(nchunk):
            cs = slice(c * chunk, (c + 1) * chunk)
            items.append((qkv[cs], beta_all[cs], g_all[cs]))
        yield
    prep = yield from _dn_prep(items, lmat, ge, gt_all, chunk)

    chains = [(s, h) for s in range(nseq) for h in range(DN_HEADS)]
    states = [so_ref[s, h] for s, h in chains]
    for c in range(nchunk):
        r0 = c * chunk
        fac = [prep[s * nchunk + c][h] for s, h in chains]
        m1s = [_bdot(f[1], st) for f, st in zip(fac, states)]
        yield
        v_news = [(f[0] - m1[:chunk]).astype(BF16) for f, m1 in zip(fac, m1s)]
        ups = [_bdot(f[3], v, TN) for f, v in zip(fac, v_news)]
        os_ = [m1[chunk:] + _bdot(f[2], v) for f, m1, v in zip(fac, m1s, v_news)]
        yield
        states = [st * f[4] + up for f, st, up in zip(fac, states, ups)]
        for (s, h), o in zip(chains, os_):
            lo = h * DN_DV
            o_ref[s, r0:r0 + chunk, lo:lo + DN_DV] = o
        yield
    for (s, h), st in zip(chains, states):
        so_ref[s, h] = st


def _dn_mixer(proj, nblk, blk0, rows, chunk, front_pad, nseq, s0, layer, cbuf, cw, prm, stack=None):
    nb = proj.shape[0]
    col = lambda c: c[0] // c[1]
    phys = lambda i: (i + blk0) % nblk
    so_shape, so_spec, extra, extra_specs, aliases, slot, n_slots = _state_out(
        stack, layer, nb, nseq, (DN_HEADS, DN_DK, DN_DV), 1, 6)
    return dict(
        body=functools.partial(_dn_body, nseq=nseq, rows=rows, chunk=chunk, front_pad=front_pad,
                               slot=slot, n_slots=n_slots),
        grid=(nb // nseq, nblk),
        inputs=[proj, proj, s0, cbuf, cw, prm] + extra,
        out_shape=[jax.ShapeDtypeStruct((nb, nblk * rows, DN_V), F32),
                   jax.ShapeDtypeStruct(so_shape, F32),
                   jax.ShapeDtypeStruct((nb, SUBLANES, DN_CONV), F32)],
        in_specs=[
            pl.BlockSpec((nseq, rows, DN_CONV), lambda b, i: (b, phys(i), col(COL_QKV))),
            pl.BlockSpec((nseq, rows, LANES), lambda b, i: (b, phys(i), col(COL_SM))),
            pl.BlockSpec((None, nseq, DN_HEADS, DN_DK, DN_DV), lambda b, i: (layer, b, 0, 0, 0)),
            pl.BlockSpec((nseq, SUBLANES, DN_CONV), lambda b, i: (b, 0, 0)),
            pl.BlockSpec((CONV_WIDTH, DN_CONV), lambda b, i: (0, 0)),
            pl.BlockSpec((SUBLANES, LANES), lambda b, i: (0, 0)),
        ] + extra_specs,
        out_specs=[
            pl.BlockSpec((nseq, rows, DN_V), lambda b, i: (b, phys(i), 0)),
            so_spec,
            pl.BlockSpec((nseq, SUBLANES, DN_CONV), lambda b, i: (b, 0, 0)),
        ],
        scratch=[pltpu.VMEM((nseq, SUBLANES, DN_CONV), F32)],
        aliases=aliases,
    )


def _ssd_body(ins, outs, scratch, *, nseq, rows, chunk, front_pad, slot, n_slots):
    xbc_ref, sm_ref, h0_ref, cb_ref, cw_ref, cbias_ref, prm_ref, drow_ref = ins[:8]
    y_ref, ho_full, cbo_ref = outs
    xbuf, = scratch
    ho_ref = _state_view(ho_full, slot, n_slots)
    blk = pl.program_id(1)

    @pl.when(blk == 0)
    def _():
        ho_ref[...] = h0_ref[...]
        xbuf[:, 0:SUBLANES, :] = cb_ref[...]

    prm = prm_ref[...]
    drow = drow_ref[...]
    cbias = cbias_ref[...]
    ge = _causal_mask(chunk)
    lmat = jnp.where(ge, 1.0, 0.0).astype(BF16)
    hpg = SSM_HEADS // SSM_GROUPS
    gw = hpg * SSM_HEADDIM
    lane = lax.broadcasted_iota(jnp.int32, (1, gw), 1)
    srow = lax.broadcasted_iota(jnp.int32, (gw, 1), 0)
    in_head = [(lane >= j * SSM_HEADDIM) & (lane < (j + 1) * SSM_HEADDIM) for j in range(hpg)]
    nchunk = rows // chunk
    groups = range(SSM_GROUPS)

    items = []
    for s in range(nseq):
        act = _silu(_causal_conv(xbuf.at[s], xbc_ref.at[s], cw_ref, cbo_ref.at[s], rows) + cbias)
        dt_all = _softplus(sm_ref[s] + prm[PRM_SSM_DTB:PRM_SSM_DTB + 1, :])
        if front_pad:
            pos = blk * rows + lax.broadcasted_iota(jnp.int32, (rows, 1), 0)
            dt_all = jnp.where(pos < front_pad, 0.0, dt_all)
        g_all = dt_all * (-jnp.exp(prm[PRM_SSM_ALOG:PRM_SSM_ALOG + 1, :]))
        for c in range(nchunk):
            cs = slice(c * chunk, (c + 1) * chunk)
            items.append((act[cs], dt_all[cs], g_all[cs]))
        yield
    n_items = len(items)
    gcs = [_cumsum_rows(lmat, g) for _, _, g in items]
    gcts = [gc.T for gc in gcs]
    xgs = [[a[:, g * gw:(g + 1) * gw] for g in groups] for a, _, _ in items]
    bgs = [[a[:, SSM_INNER + g * SSM_STATE:SSM_INNER + (g + 1) * SSM_STATE] for g in groups] for a, _, _ in items]
    cgs = [[a[:, SSM_INNER + SSM_BC + g * SSM_STATE:SSM_INNER + SSM_BC + (g + 1) * SSM_STATE] for g in groups]
           for a, _, _ in items]
    cbs = [[_bdot(cgs[i][g], bgs[i][g], NT) for g in groups] for i in range(n_items)]
    yield
    egs = [jnp.exp(gc) for gc in gcs]
    ekds = [jnp.exp(gc[chunk - 1:chunk, :] - gc) for gc in gcs]
    egls = [jnp.exp(gc[chunk - 1:chunk, :]) for gc in gcs]
    heads = [(g, j) for g in groups for j in range(hpg)]
    ln = lambda g, j: SM_DT + g * hpg + j
    xdts = [[jnp.where(in_head[j], xgs[i][g] * items[i][1][:, ln(g, j):ln(g, j) + 1], 0.0).astype(BF16)
             for g, j in heads] for i in range(n_items)]
    attns = [[cbs[i][g] * _segment_decay(gcs[i], gcts[i], ln(g, j), ge) for g, j in heads] for i in range(n_items)]
    y_intras = [[_bdot(attns[i][k], xdts[i][k]) for k in range(len(heads))] for i in range(n_items)]
    yield
    upds = [[_bdot(xdts[i][k], bgs[i][g] * ekds[i][:, ln(g, j):ln(g, j) + 1], TN) for k, (g, j) in enumerate(heads)]
            for i in range(n_items)]
    yield
    cds = [[jnp.concatenate([cgs[i][g] * egs[i][:, ln(g, j):ln(g, j) + 1] for j in range(hpg)], axis=0).astype(BF16)
            for g in groups] for i in range(n_items)]
    y_loc = [[sum(y_intras[i][g * hpg + j] for j in range(hpg)) for g in groups] for i in range(n_items)]
    h_inc = [[sum(upds[i][g * hpg + j] for j in range(hpg)) for g in groups] for i in range(n_items)]
    gl_cols = []
    for i in range(n_items):
        per_group = []
        for g in groups:
            gl = egls[i][:, ln(g, 0):ln(g, 0) + 1]
            for j in range(1, hpg):
                gl = jnp.where(srow < j * SSM_HEADDIM, gl, egls[i][:, ln(g, j):ln(g, j) + 1])
            per_group.append(gl)
        gl_cols.append(per_group)

    chains = [(s, g) for s in range(nseq) for g in groups]
    states = [ho_ref[s, g] for s, g in chains]
    for c in range(nchunk):
        r0 = c * chunk
        idx = [s * nchunk + c for s, _ in chains]
        yis = [_bdot(cds[i][g], st, NT) for i, (_, g), st in zip(idx, chains, states)]
        yield
        states = [st * gl_cols[i][g] + h_inc[i][g] for i, (_, g), st in zip(idx, chains, states)]
        for i, (s, g), yi in zip(idx, chains, yis):
            y_inter = yi[0:chunk]
            for j in range(1, hpg):
                y_inter = jnp.where(in_head[j], yi[j * chunk:(j + 1) * chunk], y_inter)
            y_ref[s, r0:r0 + chunk, g * gw:(g + 1) * gw] = (y_loc[i][g] + y_inter
                                                           + xgs[i][g] * drow[:, g * gw:(g + 1) * gw])
        yield
    for (s, g), st in zip(chains, states):
        ho_ref[s, g] = st


def _ssd_mixer(proj, nblk, blk0, rows, chunk, front_pad, nseq, h0, layer, cbuf, cw, cbias, prm, drow, stack=None):
    nb = proj.shape[0]
    col = lambda c: c[0] // c[1]
    phys = lambda i: (i + blk0) % nblk
    gw = (SSM_HEADS // SSM_GROUPS) * SSM_HEADDIM
    ho_shape, ho_spec, extra, extra_specs, aliases, slot, n_slots = _state_out(
        stack, layer, nb, nseq, (SSM_GROUPS, gw, SSM_STATE), 1, 8)
    return dict(
        body=functools.partial(_ssd_body, nseq=nseq, rows=rows, chunk=chunk, front_pad=front_pad,
                               slot=slot, n_slots=n_slots),
        grid=(nb // nseq, nblk),
        inputs=[proj, proj, h0, cbuf, cw, cbias, prm, drow] + extra,
        out_shape=[jax.ShapeDtypeStruct((nb, nblk * rows, SSM_INNER), F32),
                   jax.ShapeDtypeStruct(ho_shape, F32),
                   jax.ShapeDtypeStruct((nb, SUBLANES, SSM_CONV), F32)],
        in_specs=[
            pl.BlockSpec((nseq, rows, SSM_CONV), lambda b, i: (b, phys(i), col(COL_XBC))),
            pl.BlockSpec((nseq, rows, LANES), lambda b, i: (b, phys(i), col(COL_SM))),
            pl.BlockSpec((None, nseq, SSM_GROUPS, gw, SSM_STATE), lambda b, i: (layer, b, 0, 0, 0)),
            pl.BlockSpec((nseq, SUBLANES, SSM_CONV), lambda b, i: (b, 0, 0)),
            pl.BlockSpec((CONV_WIDTH, SSM_CONV), lambda b, i: (0, 0)),
            pl.BlockSpec((1, SSM_CONV), lambda b, i: (0, 0)),
            pl.BlockSpec((SUBLANES, LANES), lambda b, i: (0, 0)),
            pl.BlockSpec((1, SSM_INNER), lambda b, i: (0, 0)),
        ] + extra_specs,
        out_specs=[
            pl.BlockSpec((nseq, rows, SSM_INNER), lambda b, i: (b, phys(i), 0)),
            ho_spec,
            pl.BlockSpec((nseq, SUBLANES, SSM_CONV), lambda b, i: (b, 0, 0)),
        ],
        scratch=[pltpu.VMEM((nseq, SUBLANES, SSM_CONV), F32)],
        aliases=aliases,
    )


def _rope(x, cos, sin_signed):
    w = x.shape[-1]
    half = SWA_HEAD_DIM // 2
    lane = lax.broadcasted_iota(jnp.int32, (1, w), 1)
    first_half = (lane % SWA_HEAD_DIM) < half
    swapped = jnp.where(first_half, pltpu.roll(x, w - half, axis=1), pltpu.roll(x, half, axis=1))
    return x * cos + swapped * sin_signed


def _sink_attend(problems):
    scale = SWA_HEAD_DIM ** -0.5
    scores = [[jnp.where(m, _bdot(q, k, NT) * scale, NEG_BIG) for k, m in zip(keys, masks)]
              for q, keys, _, masks, _ in problems]
    yield
    outs = []
    probs, dens = [], []
    def lane_reduce(tiles, combine, reduce):
        merged = {}
        for t in tiles:
            merged[t.shape[-1]] = t if t.shape[-1] not in merged else combine(merged[t.shape[-1]], t)
        return [reduce(t, axis=-1, keepdims=True) for t in merged.values()]

    for (_, _, _, _, sink), ss in zip(problems, scores):
        mx = sink
        for m in lane_reduce(ss, jnp.maximum, jnp.max):
            mx = jnp.maximum(mx, m)
        probs.append([jnp.exp(s - mx) for s in ss])
        dens.append(jnp.exp(sink - mx))
    yield
    pvs = [[_bdot(p, v) for p, v in zip(ps, vals)] for (_, _, vals, _, _), ps in zip(problems, probs)]
    yield
    for pv, sink_term in zip(pvs, dens):
        acc = pv[0]
        for extra in pv[1:]:
            acc = acc + extra
        outs.append(acc / (pltpu.roll(acc, SWA_HEAD_DIM, axis=1) + sink_term))
    return outs


def _swa_head_order():
    grp = SWA_Q_HEADS // SWA_KV_HEADS
    assert SWA_KV_HEADS * SWA_HEAD_DIM == LANES
    return [j * grp + t for t in range(grp) for j in range(SWA_KV_HEADS)]


def _swa_problems(q, key_sets, val_sets, masks, prm, tq):
    grp = SWA_Q_HEADS // SWA_KV_HEADS
    row = lax.broadcasted_iota(jnp.int32, (grp * tq, 1), 0)
    lane = lax.broadcasted_iota(jnp.int32, (1, LANES), 1)
    qst = jnp.concatenate([q[:, t * LANES:(t + 1) * LANES] for t in range(grp)], axis=0).astype(BF16)
    problems = []
    for j in range(SWA_KV_HEADS):
        half = (lane >= j * SWA_HEAD_DIM) & (lane < (j + 1) * SWA_HEAD_DIM)
        sink = prm[PRM_SINK:PRM_SINK + 1, j * grp:j * grp + 1]
        for t in range(1, grp):
            sink = jnp.where(row < t * tq, sink, prm[PRM_SINK:PRM_SINK + 1, j * grp + t:j * grp + t + 1])
        problems.append((qst, [jnp.where(half, k, 0.0) for k in key_sets],
                         [jnp.where(half, v, 1.0) for v in val_sets], masks, sink))
    return problems


def _swa_tiles(outs, tq):
    grp = SWA_Q_HEADS // SWA_KV_HEADS
    lane = lax.broadcasted_iota(jnp.int32, (1, LANES), 1)
    tiles = []
    for t in range(grp):
        tile = outs[0][t * tq:(t + 1) * tq]
        for j in range(1, SWA_KV_HEADS):
            tile = jnp.where(lane < j * SWA_HEAD_DIM, tile, outs[j][t * tq:(t + 1) * tq])
        tiles.append(tile)
    return tiles


def _swa_prompt_body(ins, outs, scratch, *, nseq, front_pad):
    q_ref, k_ref, v_ref, cos_ref, sin_ref, prm_ref = ins
    o_ref, ko_ref, vo_ref = outs
    kprev, vprev = scratch
    blk = pl.program_id(1)

    @pl.when(blk == 0)
    def _():
        kprev[...] = jnp.zeros_like(kprev)
        vprev[...] = jnp.zeros_like(vprev)

    cos = cos_ref[...]
    sin = sin_ref[...]
    cos_q = jnp.concatenate([cos, cos], axis=1)
    sin_q = jnp.concatenate([sin, sin], axis=1)
    grp = SWA_Q_HEADS // SWA_KV_HEADS
    qi = lax.broadcasted_iota(jnp.int32, (grp * BLOCK, BLOCK), 0) % BLOCK
    kj = lax.broadcasted_iota(jnp.int32, (grp * BLOCK, BLOCK), 1)
    mask_cur = (kj <= qi) & (blk * BLOCK + kj >= front_pad)
    mask_prev = (kj > qi) & ((blk - 1) * BLOCK + kj >= front_pad)
    prm = prm_ref[...]
    problems = []
    for s in range(nseq):
        q = _rope(q_ref[s], cos_q, sin_q)
        k = _rope(k_ref[s], cos, sin)
        v = v_ref[s]
        problems += _swa_problems(q, (kprev[s], k), (vprev[s], v), (mask_prev, mask_cur), prm, BLOCK)
        kprev[s] = k
        vprev[s] = v
        ko_ref[s] = k
        vo_ref[s] = v
        yield
    res = yield from _sink_attend(problems)
    for s in range(nseq):
        for t, tile in enumerate(_swa_tiles(res[s * SWA_KV_HEADS:(s + 1) * SWA_KV_HEADS], BLOCK)):
            o_ref[s, :, t * LANES:(t + 1) * LANES] = tile.astype(o_ref.dtype)


def _swa_prompt(proj, nblk, blk0, front_pad, cos, sin, prm, out_dtype=F32):
    nb = proj.shape[0]
    col = lambda c: c[0] // c[1]
    phys = lambda i: (i + blk0) % nblk
    return dict(
        body=functools.partial(_swa_prompt_body, nseq=nb, front_pad=front_pad),
        grid=(1, nblk),
        inputs=[proj, proj, proj, cos, sin, prm],
        out_shape=[jax.ShapeDtypeStruct((nb, nblk * BLOCK, SWA_Q), out_dtype),
                   jax.ShapeDtypeStruct((nb, WINDOW, SWA_KV), F32),
                   jax.ShapeDtypeStruct((nb, WINDOW, SWA_KV), F32)],
        in_specs=[
            pl.BlockSpec((nb, BLOCK, SWA_Q), lambda b, i: (0, phys(i), col(COL_SWQ))),
            pl.BlockSpec((nb, BLOCK, SWA_KV), lambda b, i: (0, phys(i), col(COL_SWK))),
            pl.BlockSpec((nb, BLOCK, SWA_KV), lambda b, i: (0, phys(i), col(COL_SWV))),
            pl.BlockSpec((BLOCK, SWA_KV), lambda b, i: (i, 0)),
            pl.BlockSpec((BLOCK, SWA_KV), lambda b, i: (i, 0)),
            pl.BlockSpec((SUBLANES, LANES), lambda b, i: (0, 0)),
        ],
        out_specs=[
            pl.BlockSpec((nb, BLOCK, SWA_Q), lambda b, i: (0, phys(i), 0)),
            pl.BlockSpec((nb, WINDOW, SWA_KV), lambda b, i: (0, 0, 0)),
            pl.BlockSpec((nb, WINDOW, SWA_KV), lambda b, i: (0, 0, 0)),
        ],
        scratch=[pltpu.VMEM((nb, BLOCK, SWA_KV), F32), pltpu.VMEM((nb, BLOCK, SWA_KV), F32)],
        aliases={},
    )


def _swa_sample_body(ins, outs, scratch, *, nseq, steps, slot, n_slots):
    q_ref, k_ref, v_ref, kc_ref, vc_ref, cos_ref, sin_ref, prm_ref = ins[:8]
    o_ref, ko_full, vo_full = outs
    ko_ref = _state_view(ko_full, slot, n_slots)
    vo_ref = _state_view(vo_full, slot, n_slots)
    cos = cos_ref[...]
    sin = sin_ref[...]
    cos_q = jnp.concatenate([cos, cos], axis=1)
    sin_q = jnp.concatenate([sin, sin], axis=1)
    prm = prm_ref[...]
    grp = SWA_Q_HEADS // SWA_KV_HEADS
    ti = lax.broadcasted_iota(jnp.int32, (grp * steps, WINDOW), 0) % steps
    sj = lax.broadcasted_iota(jnp.int32, (grp * steps, WINDOW), 1)
    mask_cache = sj > ti
    tn = lax.broadcasted_iota(jnp.int32, (grp * steps, steps), 0) % steps
    sn = lax.broadcasted_iota(jnp.int32, (grp * steps, steps), 1)
    mask_new = sn <= tn
    problems = []
    for b in range(nseq):
        q = _rope(q_ref[b], cos_q, sin_q)
        k = _rope(k_ref[b], cos, sin)
        v = v_ref[b]
        kc = kc_ref[b]
        vc = vc_ref[b]
        ko_ref[b, 0:WINDOW - steps, :] = kc[steps:WINDOW, :]
        ko_ref[b, WINDOW - steps:WINDOW, :] = k
        vo_ref[b, 0:WINDOW - steps, :] = vc[steps:WINDOW, :]
        vo_ref[b, WINDOW - steps:WINDOW, :] = v
        problems += _swa_problems(q, (kc, k), (vc, v), (mask_cache, mask_new), prm, steps)
        yield
    res = yield from _sink_attend(problems)
    for b in range(nseq):
        for t, tile in enumerate(_swa_tiles(res[b * SWA_KV_HEADS:(b + 1) * SWA_KV_HEADS], steps)):
            o_ref[b, :, t * LANES:(t + 1) * LANES] = tile


def _swa_sample(proj, nseq, kc, vc, layer, cos, sin, prm, stack_k=None, stack_v=None):
    nb, steps, _ = proj.shape
    assert WINDOW > steps
    col = lambda c: c[0] // c[1]
    ko_shape, ko_spec, extra_k, specs_k, alias_k, slot, n_slots = _state_out(
        stack_k, layer, nb, nseq, (WINDOW, SWA_KV), 1, 8)
    vo_shape, vo_spec, extra_v, specs_v, alias_v, _, _ = _state_out(
        stack_v, layer, nb, nseq, (WINDOW, SWA_KV), 2, 8 + len(extra_k))
    return dict(
        body=functools.partial(_swa_sample_body, nseq=nseq, steps=steps, slot=slot, n_slots=n_slots),
        grid=(nb // nseq, 1),
        inputs=[proj, proj, proj, kc, vc, cos, sin, prm] + extra_k + extra_v,
        out_shape=[jax.ShapeDtypeStruct((nb, steps, SWA_Q), F32),
                   jax.ShapeDtypeStruct(ko_shape, F32),
                   jax.ShapeDtypeStruct(vo_shape, F32)],
        in_specs=[
            pl.BlockSpec((nseq, steps, SWA_Q), lambda b, i: (b, 0, col(COL_SWQ))),
            pl.BlockSpec((nseq, steps, SWA_KV), lambda b, i: (b, 0, col(COL_SWK))),
            pl.BlockSpec((nseq, steps, SWA_KV), lambda b, i: (b, 0, col(COL_SWV))),
            pl.BlockSpec((None, nseq, WINDOW, SWA_KV), lambda b, i: (layer, b, 0, 0)),
            pl.BlockSpec((None, nseq, WINDOW, SWA_KV), lambda b, i: (layer, b, 0, 0)),
            pl.BlockSpec((steps, SWA_KV), lambda b, i: (0, 0)),
            pl.BlockSpec((steps, SWA_KV), lambda b, i: (0, 0)),
            pl.BlockSpec((SUBLANES, LANES), lambda b, i: (0, 0)),
        ] + specs_k + specs_v,
        out_specs=[pl.BlockSpec((nseq, steps, SWA_Q), lambda b, i: (b, 0, 0)), ko_spec, vo_spec],
        scratch=[],
        aliases={**alias_k, **alias_v},
    )


def _tail_kernel(x_ref, odn_ref, zdn_ref, y_ref, zss_ref, osw_ref, dnw_ref, ssw_ref, wout_ref, g1_ref, g2_ref, g3_ref,
                 wfi_ref, wfo_ref, o_ref, *, d_ff, tf):
    odn = odn_ref[...]
    dnw = dnw_ref[...]
    dn = jnp.concatenate([_rmsnorm(odn[:, h * DN_DV:(h + 1) * DN_DV], dnw) for h in range(DN_HEADS)], axis=1)
    dn = dn * _silu(zdn_ref[...])
    yg = y_ref[...] * _silu(zss_ref[...])
    ssw = ssw_ref[...]
    gw = SSM_INNER // SSM_GROUPS
    ys = jnp.concatenate([_rmsnorm(yg[:, g * gw:(g + 1) * gw], ssw[:, g * gw:(g + 1) * gw]) for g in range(SSM_GROUPS)],
                         axis=1)
    mixed = jnp.concatenate([dn.astype(BF16), ys.astype(BF16), osw_ref[...].astype(BF16)], axis=1)
    m = jnp.dot(mixed, wout_ref[...], preferred_element_type=F32)
    x1 = x_ref[...] + _rmsnorm(m, g1_ref[...])
    h = _rmsnorm(x1, g2_ref[...]).astype(BF16)
    y2 = None
    for c in range(d_ff // tf):
        gate = jnp.dot(h, wfi_ref[:, c * tf:(c + 1) * tf], preferred_element_type=F32)
        up = jnp.dot(h, wfi_ref[:, d_ff + c * tf:d_ff + (c + 1) * tf], preferred_element_type=F32)
        part = jnp.dot((_silu(gate) * up).astype(BF16), wfo_ref[c * tf:(c + 1) * tf, :], preferred_element_type=F32)
        y2 = part if y2 is None else y2 + part
    o_ref[...] = x1 + _rmsnorm(y2, g3_ref[...])


def _layer_spec(shape, layer):
    nd = len(shape) - 1
    return pl.BlockSpec((None,) + tuple(shape[1:]), lambda *_: (layer,) + (0,) * nd, pipeline_mode=pl.Buffered(1))


def _tail(x, proj, odn, y, osw, dnw, ssw, wout, g1, g2, g3, wfi, wfo, layer, l_out, tm_target):
    nb, _, d = x.shape
    d_ff = wfo.shape[1]
    tm = _pick_tile(l_out, tm_target)
    tf = 2 * LANES if d_ff % (2 * LANES) == 0 else d_ff
    row = lambda w: pl.BlockSpec((None, tm, w), lambda b, i: (b, i, 0))
    gate = lambda c: pl.BlockSpec((None, tm, c[1]), lambda b, i: (b, i, c[0] // c[1]))
    return pl.pallas_call(
        functools.partial(_tail_kernel, d_ff=d_ff, tf=tf),
        out_shape=jax.ShapeDtypeStruct((nb, l_out, d), F32),
        grid=(nb, l_out // tm),
        in_specs=[row(d), row(DN_V), gate(COL_DNZ), row(SSM_INNER), gate(COL_SSZ), row(SWA_Q),
                  _layer_spec(dnw.shape, layer), _layer_spec(ssw.shape, layer), _layer_spec(wout.shape, layer),
                  _layer_spec(g1.shape, layer), _layer_spec(g2.shape, layer), _layer_spec(g3.shape, layer),
                  _layer_spec(wfi.shape, layer), _layer_spec(wfo.shape, layer)],
        out_specs=row(d),
        compiler_params=pltpu.CompilerParams(dimension_semantics=("arbitrary", "arbitrary"),
                                             vmem_limit_bytes=VMEM_LIMIT),
        name="outproj_ffn",
    )(x, odn, proj, y, proj, osw, dnw, ssw, wout, g1, g2, g3, wfi, wfo)


def _reorder_w_in_kernel(w_ref, o_ref):
    w = w_ref[0]
    offs = [0]
    for wd in IN_WIDTHS:
        offs.append(offs[-1] + wd)
    seg = lambda i: w[:, offs[i]:offs[i + 1]]
    dn_qkv, dn_z, dn_b, dn_a, ssm_xbc, ssm_z, ssm_dt, sw_q, sw_k, sw_v = (seg(i) for i in range(len(IN_WIDTHS)))
    n_small = dn_b.shape[1] + dn_a.shape[1] + ssm_dt.shape[1]
    small = jnp.concatenate([dn_b, dn_a, ssm_dt, jnp.zeros((w.shape[0], LANES - n_small), w.dtype)], axis=1)
    sw_q = jnp.concatenate([sw_q[:, h * SWA_HEAD_DIM:(h + 1) * SWA_HEAD_DIM] for h in _swa_head_order()], axis=1)
    o_ref[0] = jnp.concatenate([dn_qkv, dn_z, ssm_z, ssm_xbc, sw_q, sw_k, sw_v, small], axis=1).astype(BF16)


def _reorder_w_in(w):
    depth, d, d_in = w.shape
    assert d_in == sum(IN_WIDTHS)
    tr = _pick_tile(d, 256)
    return pl.pallas_call(
        _reorder_w_in_kernel,
        out_shape=jax.ShapeDtypeStruct((depth, d, D_PROJ), BF16),
        grid=(depth, d // tr),
        in_specs=[pl.BlockSpec((1, tr, d_in), lambda l, i: (l, i, 0))],
        out_specs=pl.BlockSpec((1, tr, D_PROJ), lambda l, i: (l, i, 0)),
        compiler_params=pltpu.CompilerParams(dimension_semantics=("arbitrary", "arbitrary")),
        name="reorder_w_in",
    )(w)


def _scalar_param_tiles(rows):
    depth = rows[0][1].shape[0]
    padded = [jnp.pad(v.astype(F32), ((0, 0), (off, LANES - off - v.shape[1]))) for off, v in rows]
    padded += [jnp.zeros((depth, LANES), F32)] * (SUBLANES - len(rows))
    return jnp.stack(padded, axis=1)


def _rope_tables(pos):
    half = SWA_HEAD_DIM // 2
    inv = ROPE_THETA ** (-jnp.arange(half, dtype=F32) / half)
    ang = pos.astype(F32)[:, None] * inv[None, :]
    cos = jnp.cos(ang)
    sin = jnp.sin(ang)
    cos_t = jnp.concatenate([cos, cos] * SWA_KV_HEADS, axis=1)
    sin_t = jnp.concatenate([-sin, sin] * SWA_KV_HEADS, axis=1)
    return cos_t, sin_t


def kernel(x_prompt, x_sample, state_dn, state_dn_conv, state_ssm, state_ssm_conv, cache_swa_k, cache_swa_v,
           meta_tokens, w_in, dn_conv_w, dn_a_log, dn_dt_bias, dn_norm_w, ssm_conv_w, ssm_conv_b, ssm_a_log,
           ssm_dt_bias, ssm_d, ssm_norm_w, swa_sinks, w_out, g_pre_mix, g_post_mix, g_pre_ffn, g_post_ffn,
           w_ffn_in, w_ffn_out):
    bp, seq, d = x_prompt.shape
    bs, ts, _ = x_sample.shape
    depth = w_in.shape[0]
    lp = N_META + seq + FRONT_PAD
    assert lp % BLOCK == 0 and BLOCK % CHUNK == 0 and seq % BLOCK == 0
    nblk = lp // BLOCK
    blk0 = nblk - 1
    tm0 = _pick_tile(seq, TM_FIRST)
    assert tm0 >= BLOCK
    x_front = jnp.concatenate([jnp.zeros((FRONT_PAD, d), x_prompt.dtype), meta_tokens.astype(x_prompt.dtype),
                               jnp.zeros((tm0 - BLOCK, d), x_prompt.dtype)], axis=0)
    pad_first = ((seq, seq + FRONT_PAD), (lp, seq + tm0))
    pad_range = ((seq, seq + FRONT_PAD),)
    xp = None
    xs = x_sample.reshape(1, bs * ts, d)

    cos_p, sin_p = _rope_tables(jnp.arange(lp, dtype=jnp.int32) - FRONT_PAD)
    cos_s, sin_s = _rope_tables(PAST_LEN + jnp.arange(ts, dtype=jnp.int32))

    gw = (SSM_HEADS // SSM_GROUPS) * SSM_HEADDIM
    nseq_s = _pick_tile(bs, 8) if bs % SUBLANES == 0 else bs
    zero_dn = jnp.zeros((1, bp, DN_HEADS, DN_DK, DN_DV), F32)
    zero_dnc = jnp.zeros((bp, SUBLANES, DN_CONV), F32)
    zero_ssm = jnp.zeros((1, bp, SSM_GROUPS, gw, SSM_STATE), F32)
    zero_ssmc = jnp.zeros((bp, SUBLANES, SSM_CONV), F32)
    state_ssm_g = state_ssm.reshape(depth, bs, SSM_GROUPS, gw, SSM_STATE)
    cache_k = cache_swa_k.reshape(depth, bs, WINDOW, SWA_KV)
    cache_v = cache_swa_v.reshape(depth, bs, WINDOW, SWA_KV)

    w_in_r = _reorder_w_in(w_in)
    swa0 = DN_V + SSM_INNER
    w_out_b = jnp.concatenate(
        [w_out[:, :swa0]] + [w_out[:, swa0 + h * SWA_HEAD_DIM:swa0 + (h + 1) * SWA_HEAD_DIM] for h in _swa_head_order()],
        axis=1).astype(BF16)
    w_fi_b = w_ffn_in.astype(BF16)
    w_fo_b = w_ffn_out.astype(BF16)
    g1, g2, g3, g4 = (a[:, None, :] for a in (g_pre_mix, g_post_mix, g_pre_ffn, g_post_ffn))
    dn_nw = dn_norm_w[:, None, :]
    ssm_nw = ssm_norm_w[:, None, :]

    new_p, new_s = [], []
    dn_s = ssm_s = k_s = v_s = None
    prm_all = _scalar_param_tiles([(SM_A, dn_a_log), (SM_A, dn_dt_bias), (SM_DT, ssm_a_log), (SM_DT, ssm_dt_bias),
                                   (0, swa_sinks)])
    drow_all = jnp.repeat(ssm_d, SSM_HEADDIM, axis=1)
    dn_cbuf_s = jnp.pad(state_dn_conv, ((0, 0), (0, 0), (SUBLANES - (CONV_WIDTH - 1), 0), (0, 0)))
    ssm_cbuf_s = jnp.pad(state_ssm_conv, ((0, 0), (0, 0), (SUBLANES - (CONV_WIDTH - 1), 0), (0, 0)))
    for l in range(depth):
        prm = prm_all[l]
        drow = drow_all[l][None, :]
        cbias = ssm_conv_b[l][None, :]
        last = l == depth - 1

        if l == 0:
            proj, xp = _inproj_first(x_prompt, x_front, g1, w_in_r, l, pad_first, tm0)
        else:
            proj = _inproj(xp, g1, w_in_r, l, pad_range, TM_DENSE)
        (odn, dn_p, dnc_p), = _run_parts([_dn_mixer(proj, nblk, blk0, BLOCK, CHUNK, FRONT_PAD, bp, zero_dn, 0,
                                                    zero_dnc, dn_conv_w[l], prm)], "dn_mixer")
        (ys, ssm_p, ssmc_p), = _run_parts([_ssd_mixer(proj, nblk, blk0, BLOCK, CHUNK, FRONT_PAD, bp, zero_ssm, 0,
                                                      zero_ssmc, ssm_conv_w[l], cbias, prm, drow)], "ssd_mixer")
        (osw, k_p, v_p), = _run_parts([_swa_prompt(proj, nblk, blk0, FRONT_PAD, cos_p, sin_p, prm, out_dtype=BF16)],
                                      "swa_prompt")
        xp = _tail(xp, proj, odn, ys, osw, dn_nw, ssm_nw, w_out_b, g2, g3, g4, w_fi_b, w_fo_b, l,
                   seq if last else lp, TM_DENSE)
        new_p.append((dn_p, dnc_p[:, -(CONV_WIDTH - 1):], ssm_p.reshape(bp, SSM_HEADS, SSM_HEADDIM, SSM_STATE),
                      ssmc_p[:, -(CONV_WIDTH - 1):], k_p.reshape(bp, WINDOW, SWA_KV_HEADS, SWA_HEAD_DIM),
                      v_p.reshape(bp, WINDOW, SWA_KV_HEADS, SWA_HEAD_DIM)))

        proj_s = _inproj(xs, g1, w_in_r, l, None, TM_DENSE)
        proj = proj_s.reshape(bs, ts, D_PROJ)
        (odn, dn_s, dnc_s), (ys, ssm_s, ssmc_s), (osw, k_s, v_s) = _run_parts([
            _dn_mixer(proj, 1, 0, ts, ts, 0, nseq_s, state_dn, l, dn_cbuf_s[l], dn_conv_w[l], prm,
                      stack=(depth, dn_s)),
            _ssd_mixer(proj, 1, 0, ts, ts, 0, nseq_s, state_ssm_g, l, ssm_cbuf_s[l], ssm_conv_w[l], cbias, prm, drow,
                       stack=(depth, ssm_s)),
            _swa_sample(proj, nseq_s, cache_k, cache_v, l, cos_s, sin_s, prm,
                        stack_k=(depth, k_s), stack_v=(depth, v_s))], "mixers_sample")
        flat = lambda a: a.reshape(1, bs * ts, a.shape[-1])
        xs = _tail(xs, proj_s, flat(odn), flat(ys), flat(osw), dn_nw, ssm_nw, w_out_b, g2, g3, g4, w_fi_b, w_fo_b, l,
                   bs * ts, TM_DENSE)
        new_s.append((dnc_s[:, -(CONV_WIDTH - 1):], ssmc_s[:, -(CONV_WIDTH - 1):]))

    outs_p = tuple(jnp.stack([st[i] for st in new_p]) for i in range(6))
    dnc_s, ssmc_s = (jnp.stack([st[i] for st in new_s]) for i in range(2))
    outs_s = (dn_s, dnc_s, ssm_s.reshape(depth, bs, SSM_HEADS, SSM_HEADDIM, SSM_STATE), ssmc_s,
              k_s.reshape(depth, bs, WINDOW, SWA_KV_HEADS, SWA_HEAD_DIM),
              v_s.reshape(depth, bs, WINDOW, SWA_KV_HEADS, SWA_HEAD_DIM))
    return (xp, xs.reshape(bs, ts, d)) + outs_p + outs_s
```

```python
import functools

import jax
import jax.numpy as jnp
from jax import lax
from jax.experimental import pallas as pl
from jax.experimental.pallas import tpu as pltpu

F32 = jnp.float32
BF16 = jnp.bfloat16
NT = (((1,), (1,)), ((), ()))
TN = (((0,), (0,)), ((), ()))

N_META = 16
CONV_WIDTH = 4
CHUNK = 128
BLOCK = 128
WINDOW = 128
FRONT_PAD = BLOCK - N_META
ROPE_THETA = 10000.0
PAST_LEN = 8192
EPS = 1e-6

DN_HEADS, DN_DK, DN_DV = 4, 128, 128
DN_QK = DN_HEADS * DN_DK
DN_V = DN_HEADS * DN_DV
DN_CONV = 2 * DN_QK + DN_V
SSM_HEADS, SSM_HEADDIM, SSM_GROUPS, SSM_STATE = 4, 64, 2, 128
SSM_INNER = SSM_HEADS * SSM_HEADDIM
SSM_BC = SSM_GROUPS * SSM_STATE
SSM_CONV = SSM_INNER + 2 * SSM_BC
SWA_Q_HEADS, SWA_KV_HEADS, SWA_HEAD_DIM = 4, 2, 64
SWA_Q = SWA_Q_HEADS * SWA_HEAD_DIM
SWA_KV = SWA_KV_HEADS * SWA_HEAD_DIM
IN_WIDTHS = (DN_CONV, DN_V, DN_HEADS, DN_HEADS, SSM_CONV, SSM_INNER, SSM_HEADS, SWA_Q, SWA_KV, SWA_KV)

LANES = 128
SUBLANES = 8
COL_QKV = (0, DN_CONV)
COL_DNZ = (COL_QKV[0] + DN_CONV, DN_V)
COL_SSZ = (COL_DNZ[0] + DN_V, SSM_INNER)
COL_XBC = (COL_SSZ[0] + SSM_INNER, SSM_CONV)
COL_SWQ = (COL_XBC[0] + SSM_CONV, SWA_Q)
COL_SWK = (COL_SWQ[0] + SWA_Q, SWA_KV)
COL_SWV = (COL_SWK[0] + SWA_KV, SWA_KV)
COL_SM = (COL_SWV[0] + SWA_KV, LANES)
D_PROJ = COL_SM[0] + LANES
assert all(off % width == 0 for off, width in (COL_QKV, COL_DNZ, COL_SSZ, COL_XBC, COL_SWQ, COL_SWK, COL_SWV, COL_SM))
SM_B, SM_A, SM_DT = 0, DN_HEADS, 2 * DN_HEADS
PRM_DN_ALOG, PRM_DN_DTB, PRM_SSM_ALOG, PRM_SSM_DTB, PRM_SINK = 0, 1, 2, 3, 4
NEG_BIG = -1e30
VMEM_LIMIT = 56 * 1024 * 1024
TM_DENSE = 640
TM_FIRST = 512


def _bdot(a, b, dims=None):
    a = a.astype(BF16)
    b = b.astype(BF16)
    if dims is None:
        return jnp.dot(a, b, preferred_element_type=F32)
    return lax.dot_general(a, b, dims, preferred_element_type=F32)


def _cumsum_rows(lmat, g):
    hi = g.astype(BF16)
    r1 = g - hi.astype(F32)
    mid = r1.astype(BF16)
    lo = (r1 - mid.astype(F32)).astype(BF16)
    w = g.shape[1]
    parts = jnp.dot(lmat, jnp.concatenate([hi, mid, lo], axis=1), preferred_element_type=F32)
    return parts[:, :w] + parts[:, w:2 * w] + parts[:, 2 * w:]


def _rmsnorm(x, g):
    return x * lax.rsqrt(jnp.mean(x * x, axis=-1, keepdims=True) + EPS) * g


def _l2norm(x):
    return x * lax.rsqrt(jnp.sum(x * x, axis=-1, keepdims=True) + EPS)


def _sigmoid(x):
    return 0.5 * jnp.tanh(0.5 * x) + 0.5


def _silu(x):
    half = 0.5 * x
    return half * jnp.tanh(half) + half


def _softplus(x):
    return jnp.maximum(x, 0.0) + jnp.log1p(jnp.exp(-jnp.abs(x)))


def _pick_tile(n, target):
    best = None
    for t in range(SUBLANES, min(n, target) + 1, SUBLANES):
        if n % t == 0:
            best = t
    assert best is not None, n
    return best


def _project_rows(x, g, w, tm, pad_range):
    h = _rmsnorm(x, g)
    if pad_range:
        r = pl.program_id(1) * tm + lax.broadcasted_iota(jnp.int32, (tm, 1), 0)
        is_pad = None
        for lo, hi in pad_range:
            hit = (r >= lo) & (r < hi)
            is_pad = hit if is_pad is None else is_pad | hit
        h = jnp.where(is_pad, 0.0, h)
    return jnp.dot(h.astype(BF16), w, preferred_element_type=F32)


def _inproj_kernel(x_ref, g_ref, w_ref, o_ref, *, tm, pad_range):
    o_ref[...] = _project_rows(x_ref[...], g_ref[...], w_ref[...], tm, pad_range)


def _inproj_first_kernel(xm_ref, xt_ref, g_ref, w_ref, o_ref, xo_ref, *, tm, n_main, pad_range):
    x = jnp.where(pl.program_id(1) == n_main, xt_ref[...], xm_ref[...])
    xo_ref[...] = x
    o_ref[...] = _project_rows(x, g_ref[...], w_ref[...], tm, pad_range)


def _inproj_first(x_main, x_tail, g, w, layer, pad_range, tm):
    nb, rows, d = x_main.shape
    assert rows % tm == 0 and x_tail.shape == (tm, d)
    n_main = rows // tm
    return pl.pallas_call(
        functools.partial(_inproj_first_kernel, tm=tm, n_main=n_main, pad_range=pad_range),
        out_shape=(jax.ShapeDtypeStruct((nb, rows + tm, D_PROJ), F32), jax.ShapeDtypeStruct((nb, rows + tm, d), F32)),
        grid=(nb, n_main + 1),
        in_specs=[pl.BlockSpec((None, tm, d), lambda b, i: (b, jnp.minimum(i, n_main - 1), 0)),
                  pl.BlockSpec((tm, d), lambda b, i: (0, 0)),
                  _layer_spec(g.shape, layer), _layer_spec(w.shape, layer)],
        out_specs=(pl.BlockSpec((None, tm, D_PROJ), lambda b, i: (b, i, 0)),
                   pl.BlockSpec((None, tm, d), lambda b, i: (b, i, 0))),
        compiler_params=pltpu.CompilerParams(dimension_semantics=("arbitrary", "arbitrary"),
                                             vmem_limit_bytes=VMEM_LIMIT),
        name="inproj_first",
    )(x_main, x_tail, g, w)


def _inproj(x, g, w, layer, pad_range, tm_target):
    nb, rows, d = x.shape
    tm = _pick_tile(rows, tm_target)
    return pl.pallas_call(
        functools.partial(_inproj_kernel, tm=tm, pad_range=pad_range),
        out_shape=jax.ShapeDtypeStruct((nb, rows, D_PROJ), F32),
        grid=(nb, rows // tm),
        in_specs=[pl.BlockSpec((None, tm, d), lambda b, i: (b, i, 0)), _layer_spec(g.shape, layer),
                  _layer_spec(w.shape, layer)],
        out_specs=pl.BlockSpec((None, tm, D_PROJ), lambda b, i: (b, i, 0)),
        compiler_params=pltpu.CompilerParams(dimension_semantics=("arbitrary", "arbitrary"),
                                             vmem_limit_bytes=VMEM_LIMIT),
        name="inproj",
    )(x, g, w)


def _causal_conv(xbuf, raw_ref, cw_ref, cbo_ref, rows):
    x = raw_ref[...]
    prev = xbuf[0:SUBLANES, :]
    cw = cw_ref[...]
    row = lax.broadcasted_iota(jnp.int32, (SUBLANES, 1), 0)
    acc = None
    for i in range(CONV_WIDTH):
        s = CONV_WIDTH - 1 - i
        if s == 0:
            xs = x
        else:
            r = pltpu.roll(x, s, axis=0)
            head = jnp.where(row < s, pltpu.roll(prev, s, axis=0), r[0:SUBLANES])
            xs = head if rows == SUBLANES else jnp.concatenate([head, r[SUBLANES:]], axis=0)
        term = xs * cw[i:i + 1, :]
        acc = term if acc is None else acc + term
    tail = x[rows - SUBLANES:rows]
    cbo_ref[...] = tail
    xbuf[0:SUBLANES, :] = tail
    return acc


def _causal_mask(c):
    return lax.broadcasted_iota(jnp.int32, (c, c), 0) >= lax.broadcasted_iota(jnp.int32, (c, c), 1)


def _segment_decay(gc, gct, lane, ge):
    col = gc[:, lane:lane + 1]
    row = gct[lane:lane + 1, :]
    return jnp.where(ge, jnp.exp(jnp.where(ge, col - row, 0.0)), 0.0)


def _inv_lane_tile_blocks(a_list, c):
    half = c // 2
    on_diag = (lax.broadcasted_iota(jnp.int32, (c, c), 0) // half) == (lax.broadcasted_iota(jnp.int32, (c, c), 1) // half)
    ds = [jnp.where(on_diag, a, 0.0) for a in a_list]
    ls = [jnp.where(on_diag, 0.0, a) for a in a_list]
    mds = yield from _inv_unit_lower_minus_eye(ds, c, 1, order=half)
    bs = [l + _bdot(md, l) for md, l in zip(mds, ls)]
    yield
    xs = [b + _bdot(b, md) for b, md in zip(bs, mds)]
    yield
    return [md - x for md, x in zip(mds, xs)]


def _inv_unit_lower_minus_eye(a_list, c, nh, order=None):
    if nh > 1 and c % LANES == 0:
        flat = [a[:, h * c:(h + 1) * c] for a in a_list for h in range(nh)]
        ys = yield from _inv_lane_tile_blocks(flat, c)
        return [jnp.concatenate(ys[i * nh:(i + 1) * nh], axis=1) for i in range(len(a_list))]
    order = c if order is None else order
    w = nh * c
    blk_r = lax.broadcasted_iota(jnp.int32, (w, w), 0) // c
    blk_c = lax.broadcasted_iota(jnp.int32, (w, w), 1) // c
    same = blk_r == blk_c

    def block_diag(p):
        if nh == 1:
            return p.astype(BF16)
        return jnp.where(same, jnp.concatenate([p] * nh, axis=0), 0.0).astype(BF16)

    ys = [-a for a in a_list]
    ps = [_bdot(a, block_diag(a)) for a in a_list]
    yield
    n = 2
    while n < order:
        pbds = [block_diag(p) for p in ps]
        n *= 2
        if n < order:
            sts = [_bdot(jnp.concatenate([y, p], axis=0), pbd) for y, p, pbd in zip(ys, ps, pbds)]
            ys = [y + p + st[:c] for y, p, st in zip(ys, ps, sts)]
            ps = [st[c:] for st in sts]
        else:
            ys = [y + p + _bdot(y, pbd) for y, p, pbd in zip(ys, ps, pbds)]
        yield
    return ys


def _run_parts(parts, name):
    grid = parts[0]["grid"]
    assert all(p["grid"] == grid for p in parts)
    n_in = [len(p["inputs"]) for p in parts]
    n_out = [len(p["out_shape"]) for p in parts]
    n_scr = [len(p["scratch"]) for p in parts]
    aliases = {}
    for k, p in enumerate(parts):
        for i, o in p["aliases"].items():
            aliases[sum(n_in[:k]) + i] = sum(n_out[:k]) + o

    def kernel(*refs):
        ins = refs[:sum(n_in)]
        outs = refs[sum(n_in):sum(n_in) + sum(n_out)]
        scr = refs[sum(n_in) + sum(n_out):]
        gens = [p["body"](ins[sum(n_in[:k]):sum(n_in[:k + 1])], outs[sum(n_out[:k]):sum(n_out[:k + 1])],
                          scr[sum(n_scr[:k]):sum(n_scr[:k + 1])]) for k, p in enumerate(parts)]
        for tag in gens[0]:
            if tag == "chain":
                break
        live = list(gens)
        while live:
            for g in list(live):
                if next(g, StopIteration) is StopIteration:
                    live.remove(g)

    results = pl.pallas_call(
        kernel,
        out_shape=tuple(s for p in parts for s in p["out_shape"]),
        grid=grid,
        in_specs=[s for p in parts for s in p["in_specs"]],
        out_specs=tuple(s for p in parts for s in p["out_specs"]),
        scratch_shapes=[s for p in parts for s in p["scratch"]],
        input_output_aliases=aliases,
        compiler_params=pltpu.CompilerParams(dimension_semantics=("arbitrary",) * len(grid),
                                             vmem_limit_bytes=VMEM_LIMIT),
        name=name,
    )(*[a for p in parts for a in p["inputs"]])
    return [list(results[sum(n_out[:k]):sum(n_out[:k + 1])]) for k in range(len(parts))]


def _state_out(stack, layer, nb, nseq, tail, out_index, n_inputs):
    zeros = (0,) * len(tail)
    if stack is None:
        return ((nb,) + tail, pl.BlockSpec((nseq,) + tail, lambda b, *_: (b,) + zeros), [], [], {}, None, 0)
    depth, prev = stack
    shape = (depth, nb) + tail
    if prev is None:
        spec = pl.BlockSpec((depth, nseq) + tail, lambda b, *_: (0, b) + zeros)
        return (shape, spec, [], [], {}, layer, depth)
    spec = pl.BlockSpec((None, nseq) + tail, lambda b, *_: (layer, b) + zeros)
    return (shape, spec, [prev], [pl.BlockSpec(memory_space=pl.ANY)], {n_inputs: out_index}, None, 0)


def _state_view(ref, slot, n_slots):
    if slot is None:
        return ref
    for other in range(n_slots):
        if other != slot:
            ref[other] = jnp.zeros(ref.shape[1:], ref.dtype)
    return ref.at[slot]


def _dn_prep(items, lmat, ge, gt_all, chunk):
    heads = range(DN_HEADS)
    gcs = [_cumsum_rows(lmat, g_all) for _, _, g_all in items]
    gcts = [gc.T for gc in gcs]
    qs = [[_l2norm(qkv[:, h * DN_DK:(h + 1) * DN_DK]) * DN_DK ** -0.5 for h in heads] for qkv, _, _ in items]
    ks = [[_l2norm(qkv[:, DN_QK + h * DN_DK:DN_QK + (h + 1) * DN_DK]) for h in heads] for qkv, _, _ in items]
    vs = [[qkv[:, 2 * DN_QK + h * DN_DV:2 * DN_QK + (h + 1) * DN_DV] for h in heads] for qkv, _, _ in items]
    betas = [[beta_all[:, SM_B + h:SM_B + h + 1] for h in heads] for _, beta_all, _ in items]
    yield
    kbs = [[k.astype(BF16) for k in kk] for kk in ks]
    kks = [jnp.concatenate([_bdot(kb, kb, NT) for kb in kb4], axis=1) for kb4 in kbs]
    qks = [jnp.concatenate([_bdot(q, kb, NT) for q, kb in zip(q4, kb4)], axis=1) for q4, kb4 in zip(qs, kbs)]
    decs = [jnp.concatenate([_segment_decay(gc, gct, SM_A + h, ge) for h in heads], axis=1)
            for gc, gct in zip(gcs, gcts)]
    beta_ws = [jnp.concatenate([jnp.broadcast_to(b, (chunk, chunk)) for b in b4], axis=1) for b4 in betas]
    a_list = [jnp.where(gt_all, bw * kk * dec, 0.0) for bw, kk, dec in zip(beta_ws, kks, decs)]
    yield "chain"
    n = len(items)
    egs, ekds, egls, attns, rhss, qds, kds = ([None] * n for _ in range(7))

    def side_work(i):
        gc = gcs[i]
        glast = gc[chunk - 1:chunk, :]
        egs[i] = jnp.exp(gc)
        ekds[i] = jnp.exp(glast - gc)
        egls[i] = jnp.exp(glast)
        attns[i] = (qks[i] * decs[i]).astype(BF16)
        lane = lambda a, h: a[:, SM_A + h:SM_A + h + 1]
        rhss[i] = [jnp.concatenate([vs[i][h] * betas[i][h], ks[i][h] * (betas[i][h] * lane(egs[i], h))], axis=1)
                   for h in heads]
        qds[i] = [(qs[i][h] * lane(egs[i], h)).astype(BF16) for h in heads]
        kds[i] = [(ks[i][h] * lane(ekds[i], h)).astype(BF16) for h in heads]

    inverse = _inv_unit_lower_minus_eye(a_list, chunk, DN_HEADS)
    todo = list(range(n))
    while True:
        try:
            next(inverse)
        except StopIteration as done:
            tms = done.value
            break
        if todo:
            side_work(todo.pop(0))
        yield
    for i in todo:
        side_work(i)
    uws = [[rhss[i][h] + _bdot(tms[i][:, h * chunk:(h + 1) * chunk], rhss[i][h]) for h in heads] for i in range(n)]
    yield
    out = []
    for i in range(n):
        per_head = []
        for h in heads:
            uw = uws[i][h]
            wq = jnp.concatenate([uw[:, DN_DV:].astype(BF16), qds[i][h]], axis=0)
            per_head.append((uw[:, :DN_DV], wq, attns[i][:, h * chunk:(h + 1) * chunk], kds[i][h],
                             egls[i][:, SM_A + h:SM_A + h + 1]))
        out.append(per_head)
    return out


def _dn_body(ins, outs, scratch, *, nseq, rows, chunk, front_pad, slot, n_slots):
    qkv_ref, sm_ref, s0_ref, cb_ref, cw_ref, prm_ref = ins[:6]
    o_ref, so_full, cbo_ref = outs
    xbuf, = scratch
    so_ref = _state_view(so_full, slot, n_slots)
    blk = pl.program_id(1)
    first = blk == 0

    @pl.when(first)
    def _():
        so_ref[...] = s0_ref[...]
        xbuf[:, 0:SUBLANES, :] = cb_ref[...]

    prm = prm_ref[...]
    ge = _causal_mask(chunk)
    lmat = jnp.where(ge, 1.0, 0.0).astype(BF16)
    wide = (chunk, DN_HEADS * chunk)
    gt_all = lax.broadcasted_iota(jnp.int32, wide, 0) > lax.broadcasted_iota(jnp.int32, wide, 1) % chunk
    nchunk = rows // chunk

    items = []
    for s in range(nseq):
        qkv = _silu(_causal_conv(xbuf.at[s], qkv_ref.at[s], cw_ref, cbo_ref.at[s], rows))
        sm = sm_ref[s]
        beta_all = _sigmoid(sm)
        g_all = -jnp.exp(prm[PRM_DN_ALOG:PRM_DN_ALOG + 1, :]) * _softplus(sm + prm[PRM_DN_DTB:PRM_DN_DTB + 1, :])
        if front_pad:
            pos = blk * rows + lax.broadcasted_iota(jnp.int32, (rows, 1), 0)
            g_all = jnp.where(pos < front_pad, 0.0, g_all)
        for c in range(nchunk):
            cs = slice(c * chunk, (c + 1) * chunk)
            items.append((qkv[cs], beta_all[cs], g_all[cs]))
        yield
    prep = yield from _dn_prep(items, lmat, ge, gt_all, chunk)

    chains = [(s, h) for s in range(nseq) for h in range(DN_HEADS)]
    states = [so_ref[s, h] for s, h in chains]
    for c in range(nchunk):
        r0 = c * chunk
        fac = [prep[s * nchunk + c][h] for s, h in chains]
        m1s = [_bdot(f[1], st) for f, st in zip(fac, states)]
        yield
        v_news = [(f[0] - m1[:chunk]).astype(BF16) for f, m1 in zip(fac, m1s)]
        ups = [_bdot(f[3], v, TN) for f, v in zip(fac, v_news)]
        os_ = [m1[chunk:] + _bdot(f[2], v) for f, m1, v in zip(fac, m1s, v_news)]
        yield
        states = [st * f[4] + up for f, st, up in zip(fac, states, ups)]
        for (s, h), o in zip(chains, os_):
            lo = h * DN_DV
            o_ref[s, r0:r0 + chunk, lo:lo + DN_DV] = o
        yield
    for (s, h), st in zip(chains, states):
        so_ref[s, h] = st


def _dn_mixer(proj, nblk, blk0, rows, chunk, front_pad, nseq, s0, layer, cbuf, cw, prm, stack=None):
    nb = proj.shape[0]
    col = lambda c: c[0] // c[1]
    phys = lambda i: (i + blk0) % nblk
    so_shape, so_spec, extra, extra_specs, aliases, slot, n_slots = _state_out(
        stack, layer, nb, nseq, (DN_HEADS, DN_DK, DN_DV), 1, 6)
    return dict(
        body=functools.partial(_dn_body, nseq=nseq, rows=rows, chunk=chunk, front_pad=front_pad,
                               slot=slot, n_slots=n_slots),
        grid=(nb // nseq, nblk),
        inputs=[proj, proj, s0, cbuf, cw, prm] + extra,
        out_shape=[jax.ShapeDtypeStruct((nb, nblk * rows, DN_V), F32),
                   jax.ShapeDtypeStruct(so_shape, F32),
                   jax.ShapeDtypeStruct((nb, SUBLANES, DN_CONV), F32)],
        in_specs=[
            pl.BlockSpec((nseq, rows, DN_CONV), lambda b, i: (b, phys(i), col(COL_QKV))),
            pl.BlockSpec((nseq, rows, LANES), lambda b, i: (b, phys(i), col(COL_SM))),
            pl.BlockSpec((None, nseq, DN_HEADS, DN_DK, DN_DV), lambda b, i: (layer, b, 0, 0, 0)),
            pl.BlockSpec((nseq, SUBLANES, DN_CONV), lambda b, i: (b, 0, 0)),
            pl.BlockSpec((CONV_WIDTH, DN_CONV), lambda b, i: (0, 0)),
            pl.BlockSpec((SUBLANES, LANES), lambda b, i: (0, 0)),
        ] + extra_specs,
        out_specs=[
            pl.BlockSpec((nseq, rows, DN_V), lambda b, i: (b, phys(i), 0)),
            so_spec,
            pl.BlockSpec((nseq, SUBLANES, DN_CONV), lambda b, i: (b, 0, 0)),
        ],
        scratch=[pltpu.VMEM((nseq, SUBLANES, DN_CONV), F32)],
        aliases=aliases,
    )


def _ssd_body(ins, outs, scratch, *, nseq, rows, chunk, front_pad, slot, n_slots):
    xbc_ref, sm_ref, h0_ref, cb_ref, cw_ref, cbias_ref, prm_ref, drow_ref = ins[:8]
    y_ref, ho_full, cbo_ref = outs
    xbuf, = scratch
    ho_ref = _state_view(ho_full, slot, n_slots)
    blk = pl.program_id(1)

    @pl.when(blk == 0)
    def _():
        ho_ref[...] = h0_ref[...]
        xbuf[:, 0:SUBLANES, :] = cb_ref[...]

    prm = prm_ref[...]
    drow = drow_ref[...]
    cbias = cbias_ref[...]
    ge = _causal_mask(chunk)
    lmat = jnp.where(ge, 1.0, 0.0).astype(BF16)
    hpg = SSM_HEADS // SSM_GROUPS
    gw = hpg * SSM_HEADDIM
    lane = lax.broadcasted_iota(jnp.int32, (1, gw), 1)
    srow = lax.broadcasted_iota(jnp.int32, (gw, 1), 0)
    in_head = [(lane >= j * SSM_HEADDIM) & (lane < (j + 1) * SSM_HEADDIM) for j in range(hpg)]
    nchunk = rows // chunk
    groups = range(SSM_GROUPS)

    items = []
    for s in range(nseq):
        act = _silu(_causal_conv(xbuf.at[s], xbc_ref.at[s], cw_ref, cbo_ref.at[s], rows) + cbias)
        dt_all = _softplus(sm_ref[s] + prm[PRM_SSM_DTB:PRM_SSM_DTB + 1, :])
        if front_pad:
            pos = blk * rows + lax.broadcasted_iota(jnp.int32, (rows, 1), 0)
            dt_all = jnp.where(pos < front_pad, 0.0, dt_all)
        g_all = dt_all * (-jnp.exp(prm[PRM_SSM_ALOG:PRM_SSM_ALOG + 1, :]))
        for c in range(nchunk):
            cs = slice(c * chunk, (c + 1) * chunk)
            items.append((act[cs], dt_all[cs], g_all[cs]))
        yield
    n_items = len(items)
    gcs = [_cumsum_rows(lmat, g) for _, _, g in items]
    gcts = [gc.T for gc in gcs]
    xgs = [[a[:, g * gw:(g + 1) * gw] for g in groups] for a, _, _ in items]
    bgs = [[a[:, SSM_INNER + g * SSM_STATE:SSM_INNER + (g + 1) * SSM_STATE] for g in groups] for a, _, _ in items]
    cgs = [[a[:, SSM_INNER + SSM_BC + g * SSM_STATE:SSM_INNER + SSM_BC + (g + 1) * SSM_STATE] for g in groups]
           for a, _, _ in items]
    cbs = [[_bdot(cgs[i][g], bgs[i][g], NT) for g in groups] for i in range(n_items)]
    yield
    egs = [jnp.exp(gc) for gc in gcs]
    ekds = [jnp.exp(gc[chunk - 1:chunk, :] - gc) for gc in gcs]
    egls = [jnp.exp(gc[chunk - 1:chunk, :]) for gc in gcs]
    heads = [(g, j) for g in groups for j in range(hpg)]
    ln = lambda g, j: SM_DT + g * hpg + j
    xdts = [[jnp.where(in_head[j], xgs[i][g] * items[i][1][:, ln(g, j):ln(g, j) + 1], 0.0).astype(BF16)
             for g, j in heads] for i in range(n_items)]
    attns = [[cbs[i][g] * _segment_decay(gcs[i], gcts[i], ln(g, j), ge) for g, j in heads] for i in range(n_items)]
    y_intras = [[_bdot(attns[i][k], xdts[i][k]) for k in range(len(heads))] for i in range(n_items)]
    yield
    upds = [[_bdot(xdts[i][k], bgs[i][g] * ekds[i][:, ln(g, j):ln(g, j) + 1], TN) for k, (g, j) in enumerate(heads)]
            for i in range(n_items)]
    yield
    cds = [[jnp.concatenate([cgs[i][g] * egs[i][:, ln(g, j):ln(g, j) + 1] for j in range(hpg)], axis=0).astype(BF16)
            for g in groups] for i in range(n_items)]
    y_loc = [[sum(y_intras[i][g * hpg + j] for j in range(hpg)) for g in groups] for i in range(n_items)]
    h_inc = [[sum(upds[i][g * hpg + j] for j in range(hpg)) for g in groups] for i in range(n_items)]
    gl_cols = []
    for i in range(n_items):
        per_group = []
        for g in groups:
            gl = egls[i][:, ln(g, 0):ln(g, 0) + 1]
            for j in range(1, hpg):
                gl = jnp.where(srow < j * SSM_HEADDIM, gl, egls[i][:, ln(g, j):ln(g, j) + 1])
            per_group.append(gl)
        gl_cols.append(per_group)

    chains = [(s, g) for s in range(nseq) for g in groups]
    states = [ho_ref[s, g] for s, g in chains]
    for c in range(nchunk):
        r0 = c * chunk
        idx = [s * nchunk + c for s, _ in chains]
        yis = [_bdot(cds[i][g], st, NT) for i, (_, g), st in zip(idx, chains, states)]
        yield
        states = [st * gl_cols[i][g] + h_inc[i][g] for i, (_, g), st in zip(idx, chains, states)]
        for i, (s, g), yi in zip(idx, chains, yis):
            y_inter = yi[0:chunk]
            for j in range(1, hpg):
                y_inter = jnp.where(in_head[j], yi[j * chunk:(j + 1) * chunk], y_inter)
            y_ref[s, r0:r0 + chunk, g * gw:(g + 1) * gw] = (y_loc[i][g] + y_inter
                                                           + xgs[i][g] * drow[:, g * gw:(g + 1) * gw])
        yield
    for (s, g), st in zip(chains, states):
        ho_ref[s, g] = st


def _ssd_mixer(proj, nblk, blk0, rows, chunk, front_pad, nseq, h0, layer, cbuf, cw, cbias, prm, drow, stack=None):
    nb = proj.shape[0]
    col = lambda c: c[0] // c[1]
    phys = lambda i: (i + blk0) % nblk
    gw = (SSM_HEADS // SSM_GROUPS) * SSM_HEADDIM
    ho_shape, ho_spec, extra, extra_specs, aliases, slot, n_slots = _state_out(
        stack, layer, nb, nseq, (SSM_GROUPS, gw, SSM_STATE), 1, 8)
    return dict(
        body=functools.partial(_ssd_body, nseq=nseq, rows=rows, chunk=chunk, front_pad=front_pad,
                               slot=slot, n_slots=n_slots),
        grid=(nb // nseq, nblk),
        inputs=[proj, proj, h0, cbuf, cw, cbias, prm, drow] + extra,
        out_shape=[jax.ShapeDtypeStruct((nb, nblk * rows, SSM_INNER), F32),
                   jax.ShapeDtypeStruct(ho_shape, F32),
                   jax.ShapeDtypeStruct((nb, SUBLANES, SSM_CONV), F32)],
        in_specs=[
            pl.BlockSpec((nseq, rows, SSM_CONV), lambda b, i: (b, phys(i), col(COL_XBC))),
            pl.BlockSpec((nseq, rows, LANES), lambda b, i: (b, phys(i), col(COL_SM))),
            pl.BlockSpec((None, nseq, SSM_GROUPS, gw, SSM_STATE), lambda b, i: (layer, b, 0, 0, 0)),
            pl.BlockSpec((nseq, SUBLANES, SSM_CONV), lambda b, i: (b, 0, 0)),
            pl.BlockSpec((CONV_WIDTH, SSM_CONV), lambda b, i: (0, 0)),
            pl.BlockSpec((1, SSM_CONV), lambda b, i: (0, 0)),
            pl.BlockSpec((SUBLANES, LANES), lambda b, i: (0, 0)),
            pl.BlockSpec((1, SSM_INNER), lambda b, i: (0, 0)),
        ] + extra_specs,
        out_specs=[
            pl.BlockSpec((nseq, rows, SSM_INNER), lambda b, i: (b, phys(i), 0)),
            ho_spec,
            pl.BlockSpec((nseq, SUBLANES, SSM_CONV), lambda b, i: (b, 0, 0)),
        ],
        scratch=[pltpu.VMEM((nseq, SUBLANES, SSM_CONV), F32)],
        aliases=aliases,
    )


def _rope(x, cos, sin_signed):
    w = x.shape[-1]
    half = SWA_HEAD_DIM // 2
    lane = lax.broadcasted_iota(jnp.int32, (1, w), 1)
    first_half = (lane % SWA_HEAD_DIM) < half
    swapped = jnp.where(first_half, pltpu.roll(x, w - half, axis=1), pltpu.roll(x, half, axis=1))
    return x * cos + swapped * sin_signed


def _sink_attend(problems):
    scale = SWA_HEAD_DIM ** -0.5
    scores = [[jnp.where(m, _bdot(q, k, NT) * scale, NEG_BIG) for k, m in zip(keys, masks)]
              for q, keys, _, masks, _ in problems]
    yield
    outs = []
    probs, dens = [], []
    def lane_reduce(tiles, combine, reduce):
        merged = {}
        for t in tiles:
            merged[t.shape[-1]] = t if t.shape[-1] not in merged else combine(merged[t.shape[-1]], t)
        return [reduce(t, axis=-1, keepdims=True) for t in merged.values()]

    for (_, _, _, _, sink), ss in zip(problems, scores):
        mx = sink
        for m in lane_reduce(ss, jnp.maximum, jnp.max):
            mx = jnp.maximum(mx, m)
        probs.append([jnp.exp(s - mx) for s in ss])
        dens.append(jnp.exp(sink - mx))
    yield
    pvs = [[_bdot(p, v) for p, v in zip(ps, vals)] for (_, _, vals, _, _), ps in zip(problems, probs)]
    yield
    for pv, sink_term in zip(pvs, dens):
        acc = pv[0]
        for extra in pv[1:]:
            acc = acc + extra
        outs.append(acc / (pltpu.roll(acc, SWA_HEAD_DIM, axis=1) + sink_term))
    return outs


def _swa_head_order():
    grp = SWA_Q_HEADS // SWA_KV_HEADS
    assert SWA_KV_HEADS * SWA_HEAD_DIM == LANES
    return [j * grp + t for t in range(grp) for j in range(SWA_KV_HEADS)]


def _swa_problems(q, key_sets, val_sets, masks, prm, tq):
    grp = SWA_Q_HEADS // SWA_KV_HEADS
    row = lax.broadcasted_iota(jnp.int32, (grp * tq, 1), 0)
    lane = lax.broadcasted_iota(jnp.int32, (1, LANES), 1)
    qst = jnp.concatenate([q[:, t * LANES:(t + 1) * LANES] for t in range(grp)], axis=0).astype(BF16)
    problems = []
    for j in range(SWA_KV_HEADS):
        half = (lane >= j * SWA_HEAD_DIM) & (lane < (j + 1) * SWA_HEAD_DIM)
        sink = prm[PRM_SINK:PRM_SINK + 1, j * grp:j * grp + 1]
        for t in range(1, grp):
            sink = jnp.where(row < t * tq, sink, prm[PRM_SINK:PRM_SINK + 1, j * grp + t:j * grp + t + 1])
        problems.append((qst, [jnp.where(half, k, 0.0) for k in key_sets],
                         [jnp.where(half, v, 1.0) for v in val_sets], masks, sink))
    return problems


def _swa_tiles(outs, tq):
    grp = SWA_Q_HEADS // SWA_KV_HEADS
    lane = lax.broadcasted_iota(jnp.int32, (1, LANES), 1)
    tiles = []
    for t in range(grp):
        tile = outs[0][t * tq:(t + 1) * tq]
        for j in range(1, SWA_KV_HEADS):
            tile = jnp.where(lane < j * SWA_HEAD_DIM, tile, outs[j][t * tq:(t + 1) * tq])
        tiles.append(tile)
    return tiles


def _swa_prompt_body(ins, outs, scratch, *, nseq, front_pad):
    q_ref, k_ref, v_ref, cos_ref, sin_ref, prm_ref = ins
    o_ref, ko_ref, vo_ref = outs
    kprev, vprev = scratch
    blk = pl.program_id(1)

    @pl.when(blk == 0)
    def _():
        kprev[...] = jnp.zeros_like(kprev)
        vprev[...] = jnp.zeros_like(vprev)

    cos = cos_ref[...]
    sin = sin_ref[...]
    cos_q = jnp.concatenate([cos, cos], axis=1)
    sin_q = jnp.concatenate([sin, sin], axis=1)
    grp = SWA_Q_HEADS // SWA_KV_HEADS
    qi = lax.broadcasted_iota(jnp.int32, (grp * BLOCK, BLOCK), 0) % BLOCK
    kj = lax.broadcasted_iota(jnp.int32, (grp * BLOCK, BLOCK), 1)
    mask_cur = (kj <= qi) & (blk * BLOCK + kj >= front_pad)
    mask_prev = (kj > qi) & ((blk - 1) * BLOCK + kj >= front_pad)
    prm = prm_ref[...]
    problems = []
    for s in range(nseq):
        q = _rope(q_ref[s], cos_q, sin_q)
        k = _rope(k_ref[s], cos, sin)
        v = v_ref[s]
        problems += _swa_problems(q, (kprev[s], k), (vprev[s], v), (mask_prev, mask_cur), prm, BLOCK)
        kprev[s] = k
        vprev[s] = v
        ko_ref[s] = k
        vo_ref[s] = v
        yield
    res = yield from _sink_attend(problems)
    for s in range(nseq):
        for t, tile in enumerate(_swa_tiles(res[s * SWA_KV_HEADS:(s + 1) * SWA_KV_HEADS], BLOCK)):
            o_ref[s, :, t * LANES:(t + 1) * LANES] = tile.astype(o_ref.dtype)


def _swa_prompt(proj, nblk, blk0, front_pad, cos, sin, prm, out_dtype=F32):
    nb = proj.shape[0]
    col = lambda c: c[0] // c[1]
    phys = lambda i: (i + blk0) % nblk
    return dict(
        body=functools.partial(_swa_prompt_body, nseq=nb, front_pad=front_pad),
        grid=(1, nblk),
        inputs=[proj, proj, proj, cos, sin, prm],
        out_shape=[jax.ShapeDtypeStruct((nb, nblk * BLOCK, SWA_Q), out_dtype),
                   jax.ShapeDtypeStruct((nb, WINDOW, SWA_KV), F32),
                   jax.ShapeDtypeStruct((nb, WINDOW, SWA_KV), F32)],
        in_specs=[
            pl.BlockSpec((nb, BLOCK, SWA_Q), lambda b, i: (0, phys(i), col(COL_SWQ))),
            pl.BlockSpec((nb, BLOCK, SWA_KV), lambda b, i: (0, phys(i), col(COL_SWK))),
            pl.BlockSpec((nb, BLOCK, SWA_KV), lambda b, i: (0, phys(i), col(COL_SWV))),
            pl.BlockSpec((BLOCK, SWA_KV), lambda b, i: (i, 0)),
            pl.BlockSpec((BLOCK, SWA_KV), lambda b, i: (i, 0)),
            pl.BlockSpec((SUBLANES, LANES), lambda b, i: (0, 0)),
        ],
        out_specs=[
            pl.BlockSpec((nb, BLOCK, SWA_Q), lambda b, i: (0, phys(i), 0)),
            pl.BlockSpec((nb, WINDOW, SWA_KV), lambda b, i: (0, 0, 0)),
            pl.BlockSpec((nb, WINDOW, SWA_KV), lambda b, i: (0, 0, 0)),
        ],
        scratch=[pltpu.VMEM((nb, BLOCK, SWA_KV), F32), pltpu.VMEM((nb, BLOCK, SWA_KV), F32)],
        aliases={},
    )


def _swa_sample_body(ins, outs, scratch, *, nseq, steps, slot, n_slots):
    q_ref, k_ref, v_ref, kc_ref, vc_ref, cos_ref, sin_ref, prm_ref = ins[:8]
    o_ref, ko_full, vo_full = outs
    ko_ref = _state_view(ko_full, slot, n_slots)
    vo_ref = _state_view(vo_full, slot, n_slots)
    cos = cos_ref[...]
    sin = sin_ref[...]
    cos_q = jnp.concatenate([cos, cos], axis=1)
    sin_q = jnp.concatenate([sin, sin], axis=1)
    prm = prm_ref[...]
    grp = SWA_Q_HEADS // SWA_KV_HEADS
    ti = lax.broadcasted_iota(jnp.int32, (grp * steps, WINDOW), 0) % steps
    sj = lax.broadcasted_iota(jnp.int32, (grp * steps, WINDOW), 1)
    mask_cache = sj > ti
    tn = lax.broadcasted_iota(jnp.int32, (grp * steps, steps), 0) % steps
    sn = lax.broadcasted_iota(jnp.int32, (grp * steps, steps), 1)
    mask_new = sn <= tn
    problems = []
    for b in range(nseq):
        q = _rope(q_ref[b], cos_q, sin_q)
        k = _rope(k_ref[b], cos, sin)
        v = v_ref[b]
        kc = kc_ref[b]
        vc = vc_ref[b]
        ko_ref[b, 0:WINDOW - steps, :] = kc[steps:WINDOW, :]
        ko_ref[b, WINDOW - steps:WINDOW, :] = k
        vo_ref[b, 0:WINDOW - steps, :] = vc[steps:WINDOW, :]
        vo_ref[b, WINDOW - steps:WINDOW, :] = v
        problems += _swa_problems(q, (kc, k), (vc, v), (mask_cache, mask_new), prm, steps)
        yield
    res = yield from _sink_attend(problems)
    for b in range(nseq):
        for t, tile in enumerate(_swa_tiles(res[b * SWA_KV_HEADS:(b + 1) * SWA_KV_HEADS], steps)):
            o_ref[b, :, t * LANES:(t + 1) * LANES] = tile


def _swa_sample(proj, nseq, kc, vc, layer, cos, sin, prm, stack_k=None, stack_v=None):
    nb, steps, _ = proj.shape
    assert WINDOW > steps
    col = lambda c: c[0] // c[1]
    ko_shape, ko_spec, extra_k, specs_k, alias_k, slot, n_slots = _state_out(
        stack_k, layer, nb, nseq, (WINDOW, SWA_KV), 1, 8)
    vo_shape, vo_spec, extra_v, specs_v, alias_v, _, _ = _state_out(
        stack_v, layer, nb, nseq, (WINDOW, SWA_KV), 2, 8 + len(extra_k))
    return dict(
        body=functools.partial(_swa_sample_body, nseq=nseq, steps=steps, slot=slot, n_slots=n_slots),
        grid=(nb // nseq, 1),
        inputs=[proj, proj, proj, kc, vc, cos, sin, prm] + extra_k + extra_v,
        out_shape=[jax.ShapeDtypeStruct((nb, steps, SWA_Q), F32),
                   jax.ShapeDtypeStruct(ko_shape, F32),
                   jax.ShapeDtypeStruct(vo_shape, F32)],
        in_specs=[
            pl.BlockSpec((nseq, steps, SWA_Q), lambda b, i: (b, 0, col(COL_SWQ))),
            pl.BlockSpec((nseq, steps, SWA_KV), lambda b, i: (b, 0, col(COL_SWK))),
            pl.BlockSpec((nseq, steps, SWA_KV), lambda b, i: (b, 0, col(COL_SWV))),
            pl.BlockSpec((None, nseq, WINDOW, SWA_KV), lambda b, i: (layer, b, 0, 0)),
            pl.BlockSpec((None, nseq, WINDOW, SWA_KV), lambda b, i: (layer, b, 0, 0)),
            pl.BlockSpec((steps, SWA_KV), lambda b, i: (0, 0)),
            pl.BlockSpec((steps, SWA_KV), lambda b, i: (0, 0)),
            pl.BlockSpec((SUBLANES, LANES), lambda b, i: (0, 0)),
        ] + specs_k + specs_v,
        out_specs=[pl.BlockSpec((nseq, steps, SWA_Q), lambda b, i: (b, 0, 0)), ko_spec, vo_spec],
        scratch=[],
        aliases={**alias_k, **alias_v},
    )


def _tail_kernel(x_ref, odn_ref, zdn_ref, y_ref, zss_ref, osw_ref, dnw_ref, ssw_ref, wout_ref, g1_ref, g2_ref, g3_ref,
                 wfi_ref, wfo_ref, o_ref, *, d_ff, tf):
    odn = odn_ref[...]
    dnw = dnw_ref[...]
    dn = jnp.concatenate([_rmsnorm(odn[:, h * DN_DV:(h + 1) * DN_DV], dnw) for h in range(DN_HEADS)], axis=1)
    dn = dn * _silu(zdn_ref[...])
    yg = y_ref[...] * _silu(zss_ref[...])
    ssw = ssw_ref[...]
    gw = SSM_INNER // SSM_GROUPS
    ys = jnp.concatenate([_rmsnorm(yg[:, g * gw:(g + 1) * gw], ssw[:, g * gw:(g + 1) * gw]) for g in range(SSM_GROUPS)],
                         axis=1)
    mixed = jnp.concatenate([dn.astype(BF16), ys.astype(BF16), osw_ref[...].astype(BF16)], axis=1)
    m = jnp.dot(mixed, wout_ref[...], preferred_element_type=F32)
    x1 = x_ref[...] + _rmsnorm(m, g1_ref[...])
    h = _rmsnorm(x1, g2_ref[...]).astype(BF16)
    y2 = None
    for c in range(d_ff // tf):
        gate = jnp.dot(h, wfi_ref[:, c * tf:(c + 1) * tf], preferred_element_type=F32)
        up = jnp.dot(h, wfi_ref[:, d_ff + c * tf:d_ff + (c + 1) * tf], preferred_element_type=F32)
        part = jnp.dot((_silu(gate) * up).astype(BF16), wfo_ref[c * tf:(c + 1) * tf, :], preferred_element_type=F32)
        y2 = part if y2 is None else y2 + part
    o_ref[...] = x1 + _rmsnorm(y2, g3_ref[...])


def _layer_spec(shape, layer):
    nd = len(shape) - 1
    return pl.BlockSpec((None,) + tuple(shape[1:]), lambda *_: (layer,) + (0,) * nd, pipeline_mode=pl.Buffered(1))


def _tail(x, proj, odn, y, osw, dnw, ssw, wout, g1, g2, g3, wfi, wfo, layer, l_out, tm_target):
    nb, _, d = x.shape
    d_ff = wfo.shape[1]
    tm = _pick_tile(l_out, tm_target)
    tf = 2 * LANES if d_ff % (2 * LANES) == 0 else d_ff
    row = lambda w: pl.BlockSpec((None, tm, w), lambda b, i: (b, i, 0))
    gate = lambda c: pl.BlockSpec((None, tm, c[1]), lambda b, i: (b, i, c[0] // c[1]))
    return pl.pallas_call(
        functools.partial(_tail_kernel, d_ff=d_ff, tf=tf),
        out_shape=jax.ShapeDtypeStruct((nb, l_out, d), F32),
        grid=(nb, l_out // tm),
        in_specs=[row(d), row(DN_V), gate(COL_DNZ), row(SSM_INNER), gate(COL_SSZ), row(SWA_Q),
                  _layer_spec(dnw.shape, layer), _layer_spec(ssw.shape, layer), _layer_spec(wout.shape, layer),
                  _layer_spec(g1.shape, layer), _layer_spec(g2.shape, layer), _layer_spec(g3.shape, layer),
                  _layer_spec(wfi.shape, layer), _layer_spec(wfo.shape, layer)],
        out_specs=row(d),
        compiler_params=pltpu.CompilerParams(dimension_semantics=("arbitrary", "arbitrary"),
                                             vmem_limit_bytes=VMEM_LIMIT),
        name="outproj_ffn",
    )(x, odn, proj, y, proj, osw, dnw, ssw, wout, g1, g2, g3, wfi, wfo)


def _reorder_w_in_kernel(w_ref, o_ref):
    w = w_ref[0]
    offs = [0]
    for wd in IN_WIDTHS:
        offs.append(offs[-1] + wd)
    seg = lambda i: w[:, offs[i]:offs[i + 1]]
    dn_qkv, dn_z, dn_b, dn_a, ssm_xbc, ssm_z, ssm_dt, sw_q, sw_k, sw_v = (seg(i) for i in range(len(IN_WIDTHS)))
    n_small = dn_b.shape[1] + dn_a.shape[1] + ssm_dt.shape[1]
    small = jnp.concatenate([dn_b, dn_a, ssm_dt, jnp.zeros((w.shape[0], LANES - n_small), w.dtype)], axis=1)
    sw_q = jnp.concatenate([sw_q[:, h * SWA_HEAD_DIM:(h + 1) * SWA_HEAD_DIM] for h in _swa_head_order()], axis=1)
    o_ref[0] = jnp.concatenate([dn_qkv, dn_z, ssm_z, ssm_xbc, sw_q, sw_k, sw_v, small], axis=1).astype(BF16)


def _reorder_w_in(w):
    depth, d, d_in = w.shape
    assert d_in == sum(IN_WIDTHS)
    tr = _pick_tile(d, 256)
    return pl.pallas_call(
        _reorder_w_in_kernel,
        out_shape=jax.ShapeDtypeStruct((depth, d, D_PROJ), BF16),
        grid=(depth, d // tr),
        in_specs=[pl.BlockSpec((1, tr, d_in), lambda l, i: (l, i, 0))],
        out_specs=pl.BlockSpec((1, tr, D_PROJ), lambda l, i: (l, i, 0)),
        compiler_params=pltpu.CompilerParams(dimension_semantics=("arbitrary", "arbitrary")),
        name="reorder_w_in",
    )(w)


def _scalar_param_tiles(rows):
    depth = rows[0][1].shape[0]
    padded = [jnp.pad(v.astype(F32), ((0, 0), (off, LANES - off - v.shape[1]))) for off, v in rows]
    padded += [jnp.zeros((depth, LANES), F32)] * (SUBLANES - len(rows))
    return jnp.stack(padded, axis=1)


def _rope_tables(pos):
    half = SWA_HEAD_DIM // 2
    inv = ROPE_THETA ** (-jnp.arange(half, dtype=F32) / half)
    ang = pos.astype(F32)[:, None] * inv[None, :]
    cos = jnp.cos(ang)
    sin = jnp.sin(ang)
    cos_t = jnp.concatenate([cos, cos] * SWA_KV_HEADS, axis=1)
    sin_t = jnp.concatenate([-sin, sin] * SWA_KV_HEADS, axis=1)
    return cos_t, sin_t


def kernel(x_prompt, x_sample, state_dn, state_dn_conv, state_ssm, state_ssm_conv, cache_swa_k, cache_swa_v,
           meta_tokens, w_in, dn_conv_w, dn_a_log, dn_dt_bias, dn_norm_w, ssm_conv_w, ssm_conv_b, ssm_a_log,
           ssm_dt_bias, ssm_d, ssm_norm_w, swa_sinks, w_out, g_pre_mix, g_post_mix, g_pre_ffn, g_post_ffn,
           w_ffn_in, w_ffn_out):
    bp, seq, d = x_prompt.shape
    bs, ts, _ = x_sample.shape
    depth = w_in.shape[0]
    lp = N_META + seq + FRONT_PAD
    assert lp % BLOCK == 0 and BLOCK % CHUNK == 0 and seq % BLOCK == 0
    nblk = lp // BLOCK
    blk0 = nblk - 1
    tm0 = _pick_tile(seq, TM_FIRST)
    assert tm0 >= BLOCK
    x_front = jnp.concatenate([jnp.zeros((FRONT_PAD, d), x_prompt.dtype), meta_tokens.astype(x_prompt.dtype),
                               jnp.zeros((tm0 - BLOCK, d), x_prompt.dtype)], axis=0)
    pad_first = ((seq, seq + FRONT_PAD), (lp, seq + tm0))
    pad_range = ((seq, seq + FRONT_PAD),)
    xp = None
    xs = x_sample.reshape(1, bs * ts, d)

    cos_p, sin_p = _rope_tables(jnp.arange(lp, dtype=jnp.int32) - FRONT_PAD)
    cos_s, sin_s = _rope_tables(PAST_LEN + jnp.arange(ts, dtype=jnp.int32))

    gw = (SSM_HEADS // SSM_GROUPS) * SSM_HEADDIM
    nseq_s = _pick_tile(bs, 8) if bs % SUBLANES == 0 else bs
    zero_dn = jnp.zeros((1, bp, DN_HEADS, DN_DK, DN_DV), F32)
    zero_dnc = jnp.zeros((bp, SUBLANES, DN_CONV), F32)
    zero_ssm = jnp.zeros((1, bp, SSM_GROUPS, gw, SSM_STATE), F32)
    zero_ssmc = jnp.zeros((bp, SUBLANES, SSM_CONV), F32)
    state_ssm_g = state_ssm.reshape(depth, bs, SSM_GROUPS, gw, SSM_STATE)
    cache_k = cache_swa_k.reshape(depth, bs, WINDOW, SWA_KV)
    cache_v = cache_swa_v.reshape(depth, bs, WINDOW, SWA_KV)

    w_in_r = _reorder_w_in(w_in)
    swa0 = DN_V + SSM_INNER
    w_out_b = jnp.concatenate(
        [w_out[:, :swa0]] + [w_out[:, swa0 + h * SWA_HEAD_DIM:swa0 + (h + 1) * SWA_HEAD_DIM] for h in _swa_head_order()],
        axis=1).astype(BF16)
    w_fi_b = w_ffn_in.astype(BF16)
    w_fo_b = w_ffn_out.astype(BF16)
    g1, g2, g3, g4 = (a[:, None, :] for a in (g_pre_mix, g_post_mix, g_pre_ffn, g_post_ffn))
    dn_nw = dn_norm_w[:, None, :]
    ssm_nw = ssm_norm_w[:, None, :]

    new_p, new_s = [], []
    dn_s = ssm_s = k_s = v_s = None
    prm_all = _scalar_param_tiles([(SM_A, dn_a_log), (SM_A, dn_dt_bias), (SM_DT, ssm_a_log), (SM_DT, ssm_dt_bias),
                                   (0, swa_sinks)])
    drow_all = jnp.repeat(ssm_d, SSM_HEADDIM, axis=1)
    dn_cbuf_s = jnp.pad(state_dn_conv, ((0, 0), (0, 0), (SUBLANES - (CONV_WIDTH - 1), 0), (0, 0)))
    ssm_cbuf_s = jnp.pad(state_ssm_conv, ((0, 0), (0, 0), (SUBLANES - (CONV_WIDTH - 1), 0), (0, 0)))
    for l in range(depth):
        prm = prm_all[l]
        drow = drow_all[l][None, :]
        cbias = ssm_conv_b[l][None, :]
        last = l == depth - 1

        if l == 0:
            proj, xp = _inproj_first(x_prompt, x_front, g1, w_in_r, l, pad_first, tm0)
        else:
            proj = _inproj(xp, g1, w_in_r, l, pad_range, TM_DENSE)
        (odn, dn_p, dnc_p), = _run_parts([_dn_mixer(proj, nblk, blk0, BLOCK, CHUNK, FRONT_PAD, bp, zero_dn, 0,
                                                    zero_dnc, dn_conv_w[l], prm)], "dn_mixer")
        (ys, ssm_p, ssmc_p), = _run_parts([_ssd_mixer(proj, nblk, blk0, BLOCK, CHUNK, FRONT_PAD, bp, zero_ssm, 0,
                                                      zero_ssmc, ssm_conv_w[l], cbias, prm, drow)], "ssd_mixer")
        (osw, k_p, v_p), = _run_parts([_swa_prompt(proj, nblk, blk0, FRONT_PAD, cos_p, sin_p, prm, out_dtype=BF16)],
                                      "swa_prompt")
        xp = _tail(xp, proj, odn, ys, osw, dn_nw, ssm_nw, w_out_b, g2, g3, g4, w_fi_b, w_fo_b, l,
                   seq if last else lp, TM_DENSE)
        new_p.append((dn_p, dnc_p[:, -(CONV_WIDTH - 1):], ssm_p.reshape(bp, SSM_HEADS, SSM_HEADDIM, SSM_STATE),
                      ssmc_p[:, -(CONV_WIDTH - 1):], k_p.reshape(bp, WINDOW, SWA_KV_HEADS, SWA_HEAD_DIM),
                      v_p.reshape(bp, WINDOW, SWA_KV_HEADS, SWA_HEAD_DIM)))

        proj_s = _inproj(xs, g1, w_in_r, l, None, TM_DENSE)
        proj = proj_s.reshape(bs, ts, D_PROJ)
        (odn, dn_s, dnc_s), (ys, ssm_s, ssmc_s), (osw, k_s, v_s) = _run_parts([
            _dn_mixer(proj, 1, 0, ts, ts, 0, nseq_s, state_dn, l, dn_cbuf_s[l], dn_conv_w[l], prm,
                      stack=(depth, dn_s)),
            _ssd_mixer(proj, 1, 0, ts, ts, 0, nseq_s, state_ssm_g, l, ssm_cbuf_s[l], ssm_conv_w[l], cbias, prm, drow,
                       stack=(depth, ssm_s)),
            _swa_sample(proj, nseq_s, cache_k, cache_v, l, cos_s, sin_s, prm,
                        stack_k=(depth, k_s), stack_v=(depth, v_s))], "mixers_sample")
        flat = lambda a: a.reshape(1, bs * ts, a.shape[-1])
        xs = _tail(xs, proj_s, flat(odn), flat(ys), flat(osw), dn_nw, ssm_nw, w_out_b, g2, g3, g4, w_fi_b, w_fo_b, l,
                   bs * ts, TM_DENSE)
        new_s.append((dnc_s[:, -(CONV_WIDTH - 1):], ssmc_s[:, -(CONV_WIDTH - 1):]))

    outs_p = tuple(jnp.stack([st[i] for st in new_p]) for i in range(6))
    dnc_s, ssmc_s = (jnp.stack([st[i] for st in new_s]) for i in range(2))
    outs_s = (dn_s, dnc_s, ssm_s.reshape(depth, bs, SSM_HEADS, SSM_HEADDIM, SSM_STATE), ssmc_s,
              k_s.reshape(depth, bs, WINDOW, SWA_KV_HEADS, SWA_HEAD_DIM),
              v_s.reshape(depth, bs, WINDOW, SWA_KV_HEADS, SWA_HEAD_DIM))
    return (xp, xs.reshape(bs, ts, d)) + outs_p + outs_s
```
